```python
import math
import jax
import jax.numpy as jnp
from jax import lax
import numpy as np

D_MODEL = 1024
BATCH = 16
SEQ = 2048
DEPTH = 1
DEC_BATCH = 16
DEC_SEQ = 32
PAST_LEN = 2048

CHUNK = 64
HEAD_DIM = 64
N_HEADS = D_MODEL // HEAD_DIM
N_KV_HEADS = 4
GROUP = N_HEADS // N_KV_HEADS
WINDOW = 128
N_BUCKETS = 32
MAX_DISTANCE = 128
CONV_WIDTH = 3
D_CONV = D_MODEL
N_GROUPS = 8
EXPERTS_PER_GROUP = 8
N_EXPERTS = N_GROUPS * EXPERTS_PER_GROUP
TOP_K_IN_GROUP = 2
D_EXPERT = 512
EXPERT_BLOCK = 128
RMS_EPS = 1e-6
Q_COLS = N_HEADS * HEAD_DIM
KV_COLS = N_KV_HEADS * HEAD_DIM
COL_SIZES = (Q_COLS, KV_COLS, KV_COLS, D_CONV, D_CONV, D_CONV, D_MODEL, D_MODEL)
SPLIT_POINTS = tuple(int(v) for v in np.cumsum(COL_SIZES)[:-1])
IN_COLS = sum(COL_SIZES)

kernel_name = "hybrid_swa_shortconv_hiermoe_stream_step"


def rmsnorm(x, w):
    x32 = x.astype(jnp.float32)
    y = x32 * lax.rsqrt(jnp.mean(x32 * x32, axis=-1, keepdims=True) + RMS_EPS)
    return (y * w.astype(jnp.float32)).astype(x.dtype)


def t5_bucket(rel):
    half = N_BUCKETS // 2
    max_exact = half // 2
    n = jnp.abs(rel)
    far = max_exact + (jnp.log(jnp.maximum(n, 1).astype(jnp.float32) / max_exact)
                       / math.log(MAX_DISTANCE / max_exact) * (half - max_exact)).astype(jnp.int32)
    far = jnp.minimum(far, half - 1)
    return jnp.where(rel > 0, half, 0) + jnp.where(n < max_exact, n, far)


def window_sink_attention(q, k_all, v_all, n_past, past_valid, rel_bias, attn_sinks):
    bsz, t = q.shape[0], q.shape[1]
    cq = min(t, CHUNK)
    nc = t // cq
    n_keys = n_past + cq
    idx = jnp.arange(nc)[:, None] * cq + jnp.arange(n_keys)[None, :]
    kb = k_all[:, idx].astype(jnp.float32)
    vb = v_all[:, idx].astype(jnp.float32)
    valid = idx >= (0 if past_valid else n_past)
    rel = jnp.arange(n_keys)[None, :] - n_past - jnp.arange(cq)[:, None]
    bias = rel_bias.astype(jnp.float32)[t5_bucket(rel)]
    bias = jnp.transpose(bias, (2, 0, 1)).reshape(N_KV_HEADS, GROUP, cq, n_keys)
    qb = q.reshape(bsz, nc, cq, N_KV_HEADS, GROUP, HEAD_DIM).astype(jnp.float32)
    s = jnp.einsum("bnqkgd,bnskd->bnkgqs", qb, kb) * (HEAD_DIM ** -0.5) + bias
    s = jnp.where(valid[None, :, None, None, None, :], s, -jnp.inf)
    sink = attn_sinks.astype(jnp.float32).reshape(N_KV_HEADS, GROUP)[None, None, :, :, None, None]
    m = jnp.maximum(jnp.max(s, axis=-1, keepdims=True), sink)
    p = jnp.exp(s - m)
    den = jnp.sum(p, axis=-1, keepdims=True) + jnp.exp(sink - m)
    o = jnp.einsum("bnkgqs,bnskd->bnqkgd", p / den, vb)
    return o.reshape(bsz, t, N_HEADS * HEAD_DIM).astype(q.dtype)


def token_mixer(h, k_past, v_past, conv_past, past_valid, rel_bias, w_in, q_norm_w, k_norm_w,
                attn_sinks, conv_w, w_out):
    bsz, t, _ = h.shape
    n_past = k_past.shape[1]
    proj = h @ w_in
    q, k, v, b_g, c_g, u_in, g_a, g_c = jnp.split(proj, SPLIT_POINTS, axis=-1)
    q = rmsnorm(q.reshape(bsz, t, N_HEADS, HEAD_DIM), q_norm_w)
    k = rmsnorm(k.reshape(bsz, t, N_KV_HEADS, HEAD_DIM), k_norm_w)
    v = v.reshape(bsz, t, N_KV_HEADS, HEAD_DIM)
    k_all = jnp.concatenate([k_past, k], axis=1)
    v_all = jnp.concatenate([v_past, v], axis=1)
    attn = window_sink_attention(q, k_all, v_all, n_past, past_valid, rel_bias, attn_sinks)
    u_all = jnp.concatenate([conv_past, c_g * u_in], axis=1)
    conv = conv_w[0] * u_all[:, 0:t]
    for i in range(1, CONV_WIDTH):
        conv = conv + conv_w[i] * u_all[:, i:i + t]
    merged = jax.nn.sigmoid(g_a) * attn + jax.nn.sigmoid(g_c) * (b_g * conv)
    return (merged @ w_out, k_all[:, -n_past:], v_all[:, -n_past:], u_all[:, -(CONV_WIDTH - 1):])


def hierarchical_moe(h, w_router_group, b_router_group, w_router_expert, b_router_expert,
                     w_gate, w_up, w_down):
    lead = h.shape[:-1]
    hf = h.reshape(-1, D_MODEL)
    n = hf.shape[0]
    h32 = hf.astype(jnp.float32)
    g_prob = jax.nn.softmax(h32 @ w_router_group.astype(jnp.float32)
                            + b_router_group.astype(jnp.float32), axis=-1)
    g_w, g_idx = lax.top_k(g_prob, 1)
    e_logits = (jnp.einsum("nd,gde->nge", h32, w_router_expert.astype(jnp.float32))
                + b_router_expert.astype(jnp.float32))
    e_logits = jnp.take_along_axis(e_logits, g_idx[:, :, None], axis=1)[:, 0]
    e_w, e_idx = lax.top_k(jax.nn.softmax(e_logits, axis=-1), TOP_K_IN_GROUP)
    e_w = e_w / jnp.sum(e_w, axis=-1, keepdims=True)
    expert_id = g_idx * EXPERTS_PER_GROUP + e_idx
    combine_w = g_w * e_w
    n_assign = n * TOP_K_IN_GROUP
    flat_e = expert_id.reshape(-1)
    flat_tok = jnp.repeat(jnp.arange(n, dtype=jnp.int32), TOP_K_IN_GROUP)
    flat_w = combine_w.reshape(-1)
    order = jnp.argsort(flat_e)
    sorted_e = flat_e[order]
    counts = jnp.bincount(flat_e, length=N_EXPERTS)
    starts = jnp.cumsum(counts) - counts
    padded = (counts + EXPERT_BLOCK - 1) // EXPERT_BLOCK * EXPERT_BLOCK
    ends = jnp.cumsum(padded)
    dest = (ends - padded)[sorted_e] + jnp.arange(n_assign) - starts[sorted_e]
    n_rows = -(-n_assign // EXPERT_BLOCK) * EXPERT_BLOCK + N_EXPERTS * EXPERT_BLOCK
    n_blocks = n_rows // EXPERT_BLOCK
    tok_buf = jnp.full((n_rows,), n, jnp.int32).at[dest].set(flat_tok[order])
    w_buf = jnp.zeros((n_rows,), jnp.float32).at[dest].set(flat_w[order])
    block_e = jnp.minimum(jnp.searchsorted(ends, jnp.arange(n_blocks) * EXPERT_BLOCK, side="right"),
                          N_EXPERTS - 1)
    x_rows = jnp.concatenate([hf, jnp.zeros((1, D_MODEL), hf.dtype)], axis=0)[tok_buf]
    x_rows = x_rows.reshape(n_blocks, EXPERT_BLOCK, D_MODEL)

    def expert_block(args):
        xb, e = args
        return (jax.nn.silu(xb @ w_gate[e]) * (xb @ w_up[e])) @ w_down[e]

    y_rows = lax.map(expert_block, (x_rows, block_e)).reshape(n_rows, D_MODEL)
    y = jnp.zeros((n + 1, D_MODEL), jnp.float32).at[tok_buf].add(
        y_rows.astype(jnp.float32) * w_buf[:, None])
    return y[:n].astype(h.dtype).reshape(lead + (D_MODEL,))


def encoder_layer(x, c, k_past, v_past, conv_past, past_valid, rel_bias, w_ada, b_ada, norm1_w,
                  w_in, q_norm_w, k_norm_w, attn_sinks, conv_w, w_out, norm2_w, w_router_group,
                  b_router_group, w_router_expert, b_router_expert, w_gate, w_up, w_down):
    mod = (jax.nn.silu(c.astype(jnp.float32)) @ w_ada.astype(jnp.float32)
           + b_ada.astype(jnp.float32)).astype(x.dtype)[:, None, :]
    shift1, scale1, gate1, shift2, scale2, gate2 = jnp.split(mod, 6, axis=-1)
    h = rmsnorm(x, norm1_w) * (1 + scale1) + shift1
    mix, k_new, v_new, conv_new = token_mixer(h, k_past, v_past, conv_past, past_valid, rel_bias,
                                              w_in, q_norm_w, k_norm_w, attn_sinks, conv_w, w_out)
    x = x + gate1 * mix
    h = rmsnorm(x, norm2_w) * (1 + scale2) + shift2
    x = x + gate2 * hierarchical_moe(h, w_router_group, b_router_group, w_router_expert,
                                     b_router_expert, w_gate, w_up, w_down)
    return x, k_new, v_new, conv_new


def setup_inputs(seed: int = 0) -> dict:
    key = jax.random.key(seed)
    ks = jax.random.split(key, 26)
    f32 = jnp.float32
    L = DEPTH
    rows = min(WINDOW, PAST_LEN)

    def nrm(k, shape, scale):
        return jax.random.normal(k, shape, f32) * scale

    return {
        "x_prompt": nrm(ks[0], (BATCH, SEQ, D_MODEL), 1.0),
        "x_sample": nrm(ks[1], (DEC_BATCH, DEC_SEQ, D_MODEL), 1.0),
        "state_attn_k": nrm(ks[2], (L, DEC_BATCH, rows, N_KV_HEADS, HEAD_DIM), 1.0),
        "state_attn_v": nrm(ks[3], (L, DEC_BATCH, rows, N_KV_HEADS, HEAD_DIM), 1.0),
        "state_conv": nrm(ks[4], (L, DEC_BATCH, CONV_WIDTH - 1, D_CONV), 1.0),
        "c_prompt": nrm(ks[5], (BATCH, D_MODEL), 1.0),
        "c_sample": nrm(ks[6], (DEC_BATCH, D_MODEL), 1.0),
        "rel_bias": nrm(ks[7], (N_BUCKETS, N_HEADS), 0.5),
        "w_ada": nrm(ks[8], (L, D_MODEL, 6 * D_MODEL), D_MODEL ** -0.5),
        "b_ada": nrm(ks[9], (L, 6 * D_MODEL), 0.02),
        "norm1_w": 1.0 + nrm(ks[10], (L, D_MODEL), 0.05),
        "w_in": nrm(ks[11], (L, D_MODEL, IN_COLS), D_MODEL ** -0.5),
        "q_norm_w": 1.0 + nrm(ks[12], (L, HEAD_DIM), 0.05),
        "k_norm_w": 1.0 + nrm(ks[13], (L, HEAD_DIM), 0.05),
        "attn_sinks": nrm(ks[14], (L, N_HEADS), 1.0),
        "conv_w": nrm(ks[15], (L, CONV_WIDTH, D_CONV), CONV_WIDTH ** -0.5),
        "w_out": nrm(ks[16], (L, D_MODEL, D_MODEL), D_MODEL ** -0.5),
        "norm2_w": 1.0 + nrm(ks[17], (L, D_MODEL), 0.05),
        "w_router_group": nrm(ks[18], (L, D_MODEL, N_GROUPS), D_MODEL ** -0.5),
        "b_router_group": nrm(ks[19], (L, N_GROUPS), 0.01),
        "w_router_expert": nrm(ks[20], (L, N_GROUPS, D_MODEL, EXPERTS_PER_GROUP), D_MODEL ** -0.5),
        "b_router_expert": nrm(ks[21], (L, N_GROUPS, EXPERTS_PER_GROUP), 0.01),
        "w_gate": nrm(ks[22], (L, N_EXPERTS, D_MODEL, D_EXPERT), D_MODEL ** -0.5),
        "w_up": nrm(ks[23], (L, N_EXPERTS, D_MODEL, D_EXPERT), D_MODEL ** -0.5),
        "w_down": nrm(ks[24], (L, N_EXPERTS, D_EXPERT, D_MODEL), D_EXPERT ** -0.5),
    }


def reference(x_prompt, x_sample, state_attn_k, state_attn_v, state_conv, c_prompt, c_sample,
              rel_bias, w_ada, b_ada, norm1_w, w_in, q_norm_w, k_norm_w, attn_sinks, conv_w,
              w_out, norm2_w, w_router_group, b_router_group, w_router_expert, b_router_expert,
              w_gate, w_up, w_down):
    y_prompt, y_sample = x_prompt, x_sample
    bp = x_prompt.shape[0]
    kp, vp, cp, ks, vs, cs = [], [], [], [], [], []
    for l in range(DEPTH):
        zero_kv = jnp.zeros((bp, WINDOW, N_KV_HEADS, HEAD_DIM), x_prompt.dtype)
        zero_conv = jnp.zeros((bp, CONV_WIDTH - 1, D_CONV), x_prompt.dtype)
        y_prompt, k_new, v_new, c_new = encoder_layer(
            y_prompt, c_prompt, zero_kv, zero_kv, zero_conv, False, rel_bias, w_ada[l], b_ada[l],
            norm1_w[l], w_in[l], q_norm_w[l], k_norm_w[l], attn_sinks[l], conv_w[l], w_out[l],
            norm2_w[l], w_router_group[l], b_router_group[l], w_router_expert[l],
            b_router_expert[l], w_gate[l], w_up[l], w_down[l])
        kp.append(k_new)
        vp.append(v_new)
        cp.append(c_new)
        y_sample, k_new, v_new, c_new = encoder_layer(
            y_sample, c_sample, state_attn_k[l], state_attn_v[l], state_conv[l], True, rel_bias,
            w_ada[l], b_ada[l], norm1_w[l], w_in[l], q_norm_w[l], k_norm_w[l], attn_sinks[l],
            conv_w[l], w_out[l], norm2_w[l], w_router_group[l], b_router_group[l],
            w_router_expert[l], b_router_expert[l], w_gate[l], w_up[l], w_down[l])
        ks.append(k_new)
        vs.append(v_new)
        cs.append(c_new)
    new_attn_k_prompt = jnp.stack(kp)
    new_attn_v_prompt = jnp.stack(vp)
    new_conv_prompt = jnp.stack(cp)
    new_attn_k_sample = jnp.stack(ks)
    new_attn_v_sample = jnp.stack(vs)
    new_conv_sample = jnp.stack(cs)
    return (y_prompt, y_sample, new_attn_k_prompt, new_attn_v_prompt, new_conv_prompt,
            new_attn_k_sample, new_attn_v_sample, new_conv_sample)
```

```python
import functools
import math

import numpy as np
import jax
import jax.numpy as jnp
from jax import lax
from jax.experimental import pallas as pl
from jax.experimental.pallas import tpu as pltpu
from jax.experimental.pallas import tpu_sc as plsc

F32 = jnp.float32
BF16 = jnp.bfloat16
I32 = jnp.int32

D_MODEL = 1024
HEAD_DIM = 64
N_HEADS = 16
N_KV = 4
GROUP = 4
CHUNK = 64
WINDOW = 128
N_BUCKETS = 32
MAX_DISTANCE = 128
N_GROUPS = 8
EPG = 8
N_EXPERTS = 64
D_EXPERT = 512
RMS_EPS = 1e-6
KV_COLS = N_KV * HEAD_DIM
IN_COLS = 6 * D_MODEL + 2 * KV_COLS
HALF = D_MODEL // 2
LANES = 128

VMEM_LIMIT = 56 * 1024 * 1024
INPROJ_TN = 512
ROW_TILE = 512
MIX_TILE = 256
EXPERT_BLOCK = 256
RANK_TILE = 512
SC_CORES = 2
SC_SUBCORES = 16
SC_WORKERS = SC_CORES * SC_SUBCORES
SC_MAX_WINDOW = 128


def _cparams(sem):
    return pltpu.CompilerParams(dimension_semantics=sem, vmem_limit_bytes=VMEM_LIMIT)


def _split_bf16(a):
    hi = a.astype(BF16)
    lo = (a - hi.astype(F32)).astype(BF16)
    return hi, lo


def _dot3(a, b):
    ah, al = _split_bf16(a)
    bh, bl = _split_bf16(b)
    d = functools.partial(jnp.dot, preferred_element_type=F32)
    return d(ah, bh) + (d(ah, bl) + d(al, bh))


def _pack_pairs(y):
    a = lax.bitcast_convert_type(y[:, :HALF].astype(BF16).astype(F32), I32)
    b = lax.bitcast_convert_type(y[:, HALF:].astype(BF16).astype(F32), I32)
    return a | lax.shift_right_logical(b, jnp.int32(16))


def _unpack_pairs(w):
    a = lax.bitcast_convert_type(w & jnp.int32(-65536), F32)
    b = lax.bitcast_convert_type(lax.shift_left(w, jnp.int32(16)), F32)
    return a, b


def _ada_kernel(c_ref, w_ref, b_ref, o_ref):
    c = c_ref[...]
    s = c * jax.nn.sigmoid(c)
    o_ref[...] = _dot3(s, w_ref[...]) + b_ref[...]


def _ada(c_all, w_ada, b_ada):
    r, d = c_all.shape
    n = w_ada.shape[1]
    tn = 1024
    return pl.pallas_call(
        _ada_kernel,
        out_shape=jax.ShapeDtypeStruct((r, n), F32),
        grid=(n // tn,),
        in_specs=[pl.BlockSpec((r, d), lambda j: (0, 0)),
                  pl.BlockSpec((d, tn), lambda j: (0, j)),
                  pl.BlockSpec((1, tn), lambda j: (0, j))],
        out_specs=pl.BlockSpec((r, tn), lambda j: (0, j)),
        compiler_params=_cparams(("arbitrary",)),
        name="ada",
    )(c_all, w_ada, b_ada.reshape(1, n))


def _inproj_kernel(x_ref, mod_ref, nw_ref, w_ref, o_ref):
    x = x_ref[0]
    mod = mod_ref[0]
    h = x * lax.rsqrt(jnp.mean(x * x, axis=-1, keepdims=True) + RMS_EPS) * nw_ref[...]
    h = h * (1.0 + mod[1:2]) + mod[0:1]
    hb = h.astype(BF16)
    for j in range(IN_COLS // INPROJ_TN):
        sl = slice(j * INPROJ_TN, (j + 1) * INPROJ_TN)
        o_ref[0, :, sl] = jnp.dot(hb, w_ref[:, sl], preferred_element_type=F32).astype(BF16)


def _inproj(x, mod, nw, w_in_b, tm):
    b, t, d = x.shape
    return pl.pallas_call(
        _inproj_kernel,
        out_shape=jax.ShapeDtypeStruct((b, t, IN_COLS), BF16),
        grid=(b, t // tm),
        in_specs=[pl.BlockSpec((1, tm, d), lambda i, j: (i, j, 0)),
                  pl.BlockSpec((1, 6, d), lambda i, j: (i, 0, 0)),
                  pl.BlockSpec((1, d), lambda i, j: (0, 0)),
                  pl.BlockSpec((d, IN_COLS), lambda i, j: (0, 0), pipeline_mode=pl.Buffered(1))],
        out_specs=pl.BlockSpec((1, tm, IN_COLS), lambda i, j: (i, j, 0)),
        compiler_params=_cparams(("arbitrary", "arbitrary")),
        name="inproj",
    )(x, mod, nw, w_in_b)


def _head_rms_scale(xf, n_heads):
    out = []
    for h in range(n_heads):
        seg = xf[:, h * HEAD_DIM:(h + 1) * HEAD_DIM]
        out.append(lax.rsqrt(jnp.mean(seg * seg, axis=-1, keepdims=True) + RMS_EPS))
    return out


def _mixer_body(q_ref, k_ref, v_ref, bg_ref, c_ref, u_ref, ga_ref, gc_ref,
                knw_ref, qsc_ref, bias_ref, sink_ref, cw_ref,
                kpast_ref, vpast_ref, cpast_ref,
                merged_ref, knew_ref, vnew_ref, cnew_ref,
                kq_buf, v_buf, u_buf, attn_buf, *, tq, cq, stateful):
    n_past = WINDOW
    nk = n_past + cq
    t = pl.program_id(1)

    k = k_ref[0].astype(F32)
    rk = _head_rms_scale(k, N_KV)
    kn = jnp.concatenate([k[:, h * HEAD_DIM:(h + 1) * HEAD_DIM] * rk[h] for h in range(N_KV)], axis=-1)
    kn = kn * knw_ref[...]
    kq = (kn * qsc_ref[...]).astype(BF16)
    vb = v_ref[0]

    if stateful:
        kp = kpast_ref[0]
        vp = vpast_ref[0]
        kq_buf[0:n_past] = (kp * qsc_ref[...]).astype(BF16)
        v_buf[0:n_past] = vp.astype(BF16)
        u_buf[...] = jnp.zeros_like(u_buf)
        u_buf[6:8] = cpast_ref[0]
        knew_ref[0] = jnp.concatenate([kp[tq:], kn], axis=0)
        vnew_ref[0] = jnp.concatenate([vp[tq:], vb.astype(F32)], axis=0)
    else:
        @pl.when(t == 0)
        def _():
            kq_buf[0:n_past] = jnp.zeros((n_past, KV_COLS), BF16)
            v_buf[0:n_past] = jnp.zeros((n_past, KV_COLS), BF16)
            u_buf[...] = jnp.zeros_like(u_buf)
        knew_ref[0] = kn[tq - n_past:]
        vnew_ref[0] = vb[tq - n_past:].astype(F32)
    kq_buf[n_past:n_past + tq] = kq
    v_buf[n_past:n_past + tq] = vb

    q = q_ref[0]
    rq = _head_rms_scale(q.astype(F32), N_HEADS)
    col = lax.broadcasted_iota(I32, (GROUP * cq, nk), 1)
    for c in range(tq // cq):
        r0 = c * cq
        for kv in range(N_KV):
            heads = [kv * GROUP + g for g in range(GROUP)]
            qs = jnp.concatenate([q[r0:r0 + cq, h * HEAD_DIM:(h + 1) * HEAD_DIM] for h in heads], axis=0)
            rqs = jnp.concatenate([rq[h][r0:r0 + cq] for h in heads], axis=0)
            kw = kq_buf[r0:r0 + nk, kv * HEAD_DIM:(kv + 1) * HEAD_DIM]
            vw = v_buf[r0:r0 + nk, kv * HEAD_DIM:(kv + 1) * HEAD_DIM]
            s = lax.dot_general(qs, kw, (((1,), (1,)), ((), ())), preferred_element_type=F32)
            s = s * rqs + bias_ref[kv]
            if (not stateful) and r0 < n_past:
                thr = jnp.where(t == 0, n_past - r0, 0)
                s = jnp.where(col >= thr, s, -jnp.inf)
            sink = sink_ref[kv]
            m = jnp.maximum(jnp.max(s, axis=-1, keepdims=True), sink)
            p = jnp.exp(s - m)
            den = jnp.sum(p, axis=-1, keepdims=True) + jnp.exp(sink - m)
            o = jnp.dot(p.astype(BF16), vw, preferred_element_type=F32) / den
            for g, h in enumerate(heads):
                attn_buf[r0:r0 + cq, h * HEAD_DIM:(h + 1) * HEAD_DIM] = o[g * cq:(g + 1) * cq]

    cu = c_ref[0].astype(F32) * u_ref[0].astype(F32)
    u_all = jnp.concatenate([u_buf[...], cu], axis=0)
    cw = cw_ref[...]
    conv = cw[0:1] * u_all[6:6 + tq] + cw[1:2] * u_all[7:7 + tq] + cw[2:3] * u_all[8:8 + tq]
    cnew_ref[0] = u_all[tq + 6:tq + 8]
    u_buf[...] = u_all[tq:tq + 8]
    if not stateful:
        kq_buf[0:n_past] = kq_buf[tq:tq + n_past]
        v_buf[0:n_past] = v_buf[tq:tq + n_past]

    merged = (jax.nn.sigmoid(ga_ref[0].astype(F32)) * attn_buf[...]
              + jax.nn.sigmoid(gc_ref[0].astype(F32)) * (bg_ref[0].astype(F32) * conv))
    merged_ref[0] = merged.astype(BF16)


def _mixer(proj, knw, qsc, bias_tab, sink_tab, conv_w, kpast, vpast, cpast, tq, cq, stateful):
    b, t, _ = proj.shape
    d = D_MODEL
    nk = WINDOW + cq
    wide = lambda j: pl.BlockSpec((1, tq, d), lambda i, s, j=j: (i, s, j))
    kvspec = lambda j: pl.BlockSpec((1, tq, KV_COLS), lambda i, s, j=j: (i, s, j))
    const2 = lambda shp: pl.BlockSpec(shp, lambda i, s: (0, 0))
    const3 = lambda shp: pl.BlockSpec(shp, lambda i, s: (0, 0, 0))
    per_b = lambda shp: pl.BlockSpec(shp, lambda i, s: (i, 0, 0))
    kvblk = 6 * d // KV_COLS
    in_specs = [wide(0), kvspec(kvblk), kvspec(kvblk + 1), wide(1), wide(2), wide(3), wide(4), wide(5),
                const2((1, KV_COLS)), const2((1, KV_COLS)),
                const3((N_KV, GROUP * cq, nk)), const3((N_KV, GROUP * cq, 1)), const2((3, d)),
                per_b((1, WINDOW, KV_COLS)), per_b((1, WINDOW, KV_COLS)), per_b((1, 2, d))]
    out_shape = (jax.ShapeDtypeStruct((b, t, d), BF16),
                 jax.ShapeDtypeStruct((b, WINDOW, KV_COLS), F32),
                 jax.ShapeDtypeStruct((b, WINDOW, KV_COLS), F32),
                 jax.ShapeDtypeStruct((b, 2, d), F32))
    out_specs = (pl.BlockSpec((1, tq, d), lambda i, s: (i, s, 0)),
                 per_b((1, WINDOW, KV_COLS)), per_b((1, WINDOW, KV_COLS)), per_b((1, 2, d)))
    return pl.pallas_call(
        functools.partial(_mixer_body, tq=tq, cq=cq, stateful=stateful),
        out_shape=out_shape,
        grid=(b, t // tq),
        in_specs=in_specs,
        out_specs=out_specs,
        scratch_shapes=[pltpu.VMEM((WINDOW + tq, KV_COLS), BF16),
                        pltpu.VMEM((WINDOW + tq, KV_COLS), BF16),
                        pltpu.VMEM((8, d), F32),
                        pltpu.VMEM((tq, d), F32)],
        compiler_params=_cparams(("arbitrary", "arbitrary")),
        name="mixer_state" if stateful else "mixer",
    )(proj, proj, proj, proj, proj, proj, proj, proj, knw, qsc, bias_tab, sink_tab, conv_w,
      kpast, vpast, cpast)


def _route(logits):
    lane = lax.broadcasted_iota(I32, logits.shape, 1)
    neg = -jnp.inf
    big = jnp.int32(1 << 20)
    gl = jnp.where(lane < N_GROUPS, logits, neg)
    gmax = jnp.max(gl, axis=-1, keepdims=True)
    g_idx = jnp.min(jnp.where(gl == gmax, lane, big), axis=-1, keepdims=True)
    g_w = 1.0 / jnp.sum(jnp.exp(gl - gmax), axis=-1, keepdims=True)
    lo = N_GROUPS + g_idx * EPG
    el = jnp.where((lane >= lo) & (lane < lo + EPG), logits, neg)
    m1 = jnp.max(el, axis=-1, keepdims=True)
    i1 = jnp.min(jnp.where(el == m1, lane, big), axis=-1, keepdims=True)
    el2 = jnp.where(lane == i1, neg, el)
    m2 = jnp.max(el2, axis=-1, keepdims=True)
    i2 = jnp.min(jnp.where(el2 == m2, lane, big), axis=-1, keepdims=True)
    r = jnp.exp(m2 - m1)
    w1 = 1.0 / (1.0 + r)
    w2 = r / (1.0 + r)
    return i1 - N_GROUPS, i2 - N_GROUPS, g_w * w1, g_w * w2


def _outproj_kernel(m_ref, x_ref, mod_ref, wo_ref, nw_ref, wr_ref, br_ref,
                    x1_ref, h2_ref, eid_ref, ew_ref):
    mod = mod_ref[0]
    mix = jnp.dot(m_ref[0], wo_ref[...], preferred_element_type=F32)
    x1 = x_ref[0] + mod[2:3] * mix
    x1_ref[0] = x1
    h = x1 * lax.rsqrt(jnp.mean(x1 * x1, axis=-1, keepdims=True) + RMS_EPS) * nw_ref[...]
    h = h * (1.0 + mod[4:5]) + mod[3:4]
    h2_ref[...] = _pack_pairs(h)
    logits = _dot3(h, wr_ref[...]) + br_ref[...]
    e1, e2, w1, w2 = _route(logits)
    lane8 = lax.broadcasted_iota(I32, (h.shape[0], 8), 1)
    eid_ref[...] = jnp.where(lane8 == 0, e1, jnp.where(lane8 == 1, e2, 0))
    ew_ref[...] = jnp.where(lane8 == 0, w1, jnp.where(lane8 == 1, w2, 0.0))


def _outproj(merged, x, mod, w_out_b, nw, w_r, b_r, tm):
    b, t, d = x.shape
    nt = t // tm
    flat = lambda i, j: (i * nt + j, 0)
    return pl.pallas_call(
        _outproj_kernel,
        out_shape=(jax.ShapeDtypeStruct((b, t, d), F32),
                   jax.ShapeDtypeStruct((b * t, HALF), I32),
                   jax.ShapeDtypeStruct((b * t, 8), I32),
                   jax.ShapeDtypeStruct((b * t, 8), F32)),
        grid=(b, nt),
        in_specs=[pl.BlockSpec((1, tm, d), lambda i, j: (i, j, 0)),
                  pl.BlockSpec((1, tm, d), lambda i, j: (i, j, 0)),
                  pl.BlockSpec((1, 6, d), lambda i, j: (i, 0, 0)),
                  pl.BlockSpec((d, d), lambda i, j: (0, 0)),
                  pl.BlockSpec((1, d), lambda i, j: (0, 0)),
                  pl.BlockSpec((d, LANES), lambda i, j: (0, 0)),
                  pl.BlockSpec((1, LANES), lambda i, j: (0, 0))],
        out_specs=(pl.BlockSpec((1, tm, d), lambda i, j: (i, j, 0)),
                   pl.BlockSpec((tm, HALF), flat),
                   pl.BlockSpec((tm, 8), flat),
                   pl.BlockSpec((tm, 8), flat)),
        compiler_params=_cparams(("arbitrary", "arbitrary")),
        name="outproj",
    )(merged, x, mod, w_out_b, nw, w_r, b_r)


def _rank_kernel(eid_ref, tri_ref, upper_ref, dest_ref, tot_ref, cnt, starts, *, block):
    ph = pl.program_id(0)
    i = pl.program_id(1)
    tm = eid_ref.shape[0]
    lane = lax.broadcasted_iota(I32, (tm, LANES), 1)
    e0 = eid_ref[:, 0:1]
    e1 = eid_ref[:, 1:2]
    hot0 = lane == e0
    hot1 = lane == e1
    onehot = jnp.where(hot0 | hot1, 1.0, 0.0)
    colsum = jnp.sum(onehot, axis=0, keepdims=True)

    @pl.when((ph == 0) & (i == 0))
    def _():
        cnt[...] = jnp.zeros_like(cnt)

    @pl.when(ph == 0)
    def _():
        cnt[0:1] = cnt[0:1] + colsum

    @pl.when((ph == 1) & (i == 0))
    def _():
        tot = cnt[0:1]
        tot_ref[...] = tot.astype(I32)
        nblk = jnp.floor((tot + (block - 1)) * (1.0 / block))
        hi = jnp.floor(nblk * (1.0 / 16.0))
        lo = nblk - hi * 16.0
        up = upper_ref[...]
        excl = (jnp.dot(jnp.broadcast_to(hi, (8, LANES)).astype(BF16), up, preferred_element_type=F32) * 16.0
                + jnp.dot(jnp.broadcast_to(lo, (8, LANES)).astype(BF16), up, preferred_element_type=F32))
        starts[...] = excl * float(block)
        cnt[...] = jnp.zeros_like(cnt)

    @pl.when(ph == 1)
    def _():
        prefix = jnp.dot(tri_ref[...], onehot.astype(BF16), preferred_element_type=F32)
        pos = prefix + (starts[0:1] + cnt[0:1])
        d0 = jnp.sum(jnp.where(hot0, pos, 0.0), axis=-1, keepdims=True).astype(I32)
        d1 = jnp.sum(jnp.where(hot1, pos, 0.0), axis=-1, keepdims=True).astype(I32)
        lane8 = lax.broadcasted_iota(I32, (tm, 8), 1)
        dest_ref[...] = jnp.where(lane8 == 0, d0, jnp.where(lane8 == 1, d1, 0))
        cnt[0:1] = cnt[0:1] + colsum


def _rank(eid, block):
    n = eid.shape[0]
    tm = math.gcd(n, RANK_TILE)
    r = np.arange(tm)
    tri = jnp.asarray((r[:, None] > r[None, :]).astype(np.float32), BF16)
    l = np.arange(LANES)
    upper = jnp.asarray(((l[:, None] < l[None, :]) & (l[None, :] < N_EXPERTS)).astype(np.float32), BF16)
    return pl.pallas_call(
        functools.partial(_rank_kernel, block=block),
        out_shape=(jax.ShapeDtypeStruct((n, 8), I32), jax.ShapeDtypeStruct((1, LANES), I32)),
        grid=(2, n // tm),
        in_specs=[pl.BlockSpec((tm, 8), lambda p, i: (i, 0)),
                  pl.BlockSpec((tm, tm), lambda p, i: (0, 0)),
                  pl.BlockSpec((LANES, LANES), lambda p, i: (0, 0))],
        out_specs=(pl.BlockSpec((tm, 8), lambda p, i: (i * p, 0)),
                   pl.BlockSpec((1, LANES), lambda p, i: (0, 0))),
        scratch_shapes=[pltpu.VMEM((8, LANES), F32), pltpu.VMEM((8, LANES), F32)],
        compiler_params=_cparams(("arbitrary", "arbitrary")),
        name="rank",
    )(eid, tri, upper)


def _expert_kernel(be_ref, nv_ref, xs_ref, wg_ref, wu_ref, wd_ref, ys_ref, wg_s, wu_s, wd_s):
    b = pl.program_id(0)

    @pl.when(b < nv_ref[0])
    def _():
        prev = be_ref[jnp.maximum(b - 1, 0)]
        fresh = (b == 0) | (be_ref[b] != prev)

        @pl.when(fresh)
        def _():
            wg_s[...] = wg_ref[0].astype(BF16)
            wu_s[...] = wu_ref[0].astype(BF16)
            wd_s[...] = wd_ref[0].astype(BF16)

        a, c = _unpack_pairs(xs_ref[...])
        x = jnp.concatenate([a.astype(BF16), c.astype(BF16)], axis=1)
        g = jnp.dot(x, wg_s[...], preferred_element_type=F32)
        u = jnp.dot(x, wu_s[...], preferred_element_type=F32)
        hmid = (g * jax.nn.sigmoid(g) * u).astype(BF16)
        ys_ref[...] = _pack_pairs(jnp.dot(hmid, wd_s[...], preferred_element_type=F32))


def _experts(xs, block_e, n_valid, w_gate, w_up, w_down, block):
    n_rows = xs.shape[0]
    nb = n_rows // block
    rowblk = lambda b, be, nv: (jnp.minimum(b, nv[0] - 1), 0)
    wblk = lambda b, be, nv: (be[jnp.minimum(b, nv[0] - 1)], 0, 0)
    grid_spec = pltpu.PrefetchScalarGridSpec(
        num_scalar_prefetch=2,
        grid=(nb,),
        in_specs=[pl.BlockSpec((block, HALF), rowblk),
                  pl.BlockSpec((1, D_MODEL, D_EXPERT), wblk),
                  pl.BlockSpec((1, D_MODEL, D_EXPERT), wblk),
                  pl.BlockSpec((1, D_EXPERT, D_MODEL), wblk)],
        out_specs=pl.BlockSpec((block, HALF), rowblk),
        scratch_shapes=[pltpu.VMEM((D_MODEL, D_EXPERT), BF16),
                        pltpu.VMEM((D_MODEL, D_EXPERT), BF16),
                        pltpu.VMEM((D_EXPERT, D_MODEL), BF16)])
    return pl.pallas_call(
        _expert_kernel,
        out_shape=jax.ShapeDtypeStruct((n_rows, HALF), I32),
        grid_spec=grid_spec,
        compiler_params=_cparams(("arbitrary",)),
        name="experts",
    )(block_e, n_valid, xs, w_gate, w_up, w_down)


def _final_kernel(x1_ref, y0_ref, y1_ref, ew_ref, mod_ref, o_ref):
    a0, b0 = _unpack_pairs(y0_ref[...])
    a1, b1 = _unpack_pairs(y1_ref[...])
    w0 = ew_ref[:, 0:1]
    w1 = ew_ref[:, 1:2]
    moe = jnp.concatenate([w0 * a0 + w1 * a1, w0 * b0 + w1 * b1], axis=1)
    o_ref[0] = x1_ref[0] + mod_ref[0][5:6] * moe


def _final(x1, y0, y1, ew, mod, tm):
    b, t, d = x1.shape
    nt = t // tm
    flat = lambda i, j: (i * nt + j, 0)
    return pl.pallas_call(
        _final_kernel,
        out_shape=jax.ShapeDtypeStruct((b, t, d), F32),
        grid=(b, nt),
        in_specs=[pl.BlockSpec((1, tm, d), lambda i, j: (i, j, 0)),
                  pl.BlockSpec((tm, HALF), flat),
                  pl.BlockSpec((tm, HALF), flat),
                  pl.BlockSpec((tm, 8), flat),
                  pl.BlockSpec((1, 6, d), lambda i, j: (i, 0, 0))],
        out_specs=pl.BlockSpec((1, tm, d), lambda i, j: (i, j, 0)),
        compiler_params=_cparams(("arbitrary", "arbitrary")),
        name="final",
    )(x1, y0, y1, ew, mod)


def _sc_window(rows_per_worker):
    for w in range(SC_MAX_WINDOW, 7, -8):
        if rows_per_worker % w == 0:
            return w
    raise ValueError(f"no SparseCore window divides {rows_per_worker} rows per worker")


def _sc_split(idx):
    n = idx.shape[0]
    per = n // SC_WORKERS
    assert per * SC_WORKERS == n
    win = _sc_window(per)
    return idx.reshape(SC_WORKERS, per // win, win), per // win, win


def _sc_worker_id():
    return lax.axis_index("s") * SC_CORES + lax.axis_index("c")


def _dispatch_rows(h2_groups, dest_groups, n_rows):
    splits = [(_sc_split(d[:, 0]), _sc_split(d[:, 1])) for d in dest_groups]
    ng = len(h2_groups)
    scratch = []
    for (_, _, win), _ in splits:
        scratch += [pltpu.VMEM((win,), I32), pltpu.VMEM((win,), I32), pltpu.VMEM((win, HALF), I32)]

    @functools.partial(
        pl.kernel,
        mesh=plsc.VectorSubcoreMesh(core_axis_name="c", subcore_axis_name="s"),
        out_type=jax.ShapeDtypeStruct((n_rows, HALF), I32),
        scratch_types=scratch,
        name="sc_dispatch",
    )
    def k(*refs):
        x_refs, idx_refs, o_hbm, bufs = refs[:ng], refs[ng:3 * ng], refs[3 * ng], refs[3 * ng + 1:]
        wid = _sc_worker_id()
        for g in range(ng):
            (_, nwin, win), _ = splits[g]
            x_hbm, d0_hbm, d1_hbm = x_refs[g], idx_refs[2 * g], idx_refs[2 * g + 1]
            i0_v, i1_v, rows_v = bufs[3 * g:3 * g + 3]

            @pl.loop(0, nwin)
            def _(j, nwin=nwin, win=win, x_hbm=x_hbm, d0_hbm=d0_hbm, d1_hbm=d1_hbm,
                  i0_v=i0_v, i1_v=i1_v, rows_v=rows_v):
                base = pl.multiple_of((wid * nwin + j) * win, 8)
                pltpu.sync_copy(d0_hbm.at[wid, j], i0_v)
                pltpu.sync_copy(d1_hbm.at[wid, j], i1_v)
                pltpu.sync_copy(x_hbm.at[pl.ds(base, win)], rows_v)
                pltpu.sync_copy(rows_v, o_hbm.at[i0_v])
                pltpu.sync_copy(rows_v, o_hbm.at[i1_v])

    idx_args = []
    for (s0, s1) in splits:
        idx_args += [s0[0], s1[0]]
    return k(*h2_groups, *idx_args)


def _collect_rows(ys, dest_groups):
    splits = [(_sc_split(d[:, 0]), _sc_split(d[:, 1])) for d in dest_groups]
    ng = len(dest_groups)
    outs, scratch = [], []
    for d, ((_, _, win), _) in zip(dest_groups, splits):
        o = jax.ShapeDtypeStruct((d.shape[0], HALF), I32)
        outs += [o, o]
        scratch += [pltpu.VMEM((win,), I32), pltpu.VMEM((win, HALF), I32)]

    @functools.partial(
        pl.kernel,
        mesh=plsc.VectorSubcoreMesh(core_axis_name="c", subcore_axis_name="s"),
        out_type=tuple(outs),
        scratch_types=scratch,
        name="sc_collect",
    )
    def k(*refs):
        ys_hbm, idx_refs, out_refs, bufs = refs[0], refs[1:1 + 2 * ng], refs[1 + 2 * ng:1 + 4 * ng], refs[1 + 4 * ng:]
        wid = _sc_worker_id()
        for g in range(ng):
            (_, nwin, win), _ = splits[g]
            i_v, rows_v = bufs[2 * g:2 * g + 2]
            for kk in range(2):
                d_hbm, y_hbm = idx_refs[2 * g + kk], out_refs[2 * g + kk]

                @pl.loop(0, nwin)
                def _(j, nwin=nwin, win=win, d_hbm=d_hbm, y_hbm=y_hbm, i_v=i_v, rows_v=rows_v):
                    base = pl.multiple_of((wid * nwin + j) * win, 8)
                    pltpu.sync_copy(d_hbm.at[wid, j], i_v)
                    pltpu.sync_copy(ys_hbm.at[i_v], rows_v)
                    pltpu.sync_copy(rows_v, y_hbm.at[pl.ds(base, win)])

    idx_args = []
    for (s0, s1) in splits:
        idx_args += [s0[0], s1[0]]
    res = k(ys, *idx_args)
    return [(res[2 * g], res[2 * g + 1]) for g in range(ng)]


def _t5_bucket(rel):
    half = N_BUCKETS // 2
    max_exact = half // 2
    n = jnp.abs(rel)
    far = max_exact + (jnp.log(jnp.maximum(n, 1).astype(F32) / max_exact)
                       / math.log(MAX_DISTANCE / max_exact) * (half - max_exact)).astype(I32)
    far = jnp.minimum(far, half - 1)
    return jnp.where(rel > 0, half, 0) + jnp.where(n < max_exact, n, far)


def _bias_table(rel_bias, cq):
    nk = WINDOW + cq
    rel = jnp.arange(nk)[None, :] - WINDOW - jnp.arange(cq)[:, None]
    bias = rel_bias.astype(F32)[_t5_bucket(rel)]
    bias = jnp.transpose(bias, (2, 0, 1)).reshape(N_KV, GROUP * cq, nk)
    return bias


def _sink_table(sinks, cq):
    s = sinks.astype(F32).reshape(N_KV, GROUP, 1, 1)
    return jnp.broadcast_to(s, (N_KV, GROUP, cq, 1)).reshape(N_KV, GROUP * cq, 1)


def kernel(x_prompt, x_sample, state_attn_k, state_attn_v, state_conv, c_prompt, c_sample,
           rel_bias, w_ada, b_ada, norm1_w, w_in, q_norm_w, k_norm_w, attn_sinks, conv_w,
           w_out, norm2_w, w_router_group, b_router_group, w_router_expert, b_router_expert,
           w_gate, w_up, w_down):
    depth = w_ada.shape[0]
    assert depth == 1
    bp, tp, d = x_prompt.shape
    bs, ts, _ = x_sample.shape
    n_p, n_s = bp * tp, bs * ts
    n_tok = n_p + n_s
    l = 0

    wi = w_in[l]
    qw, kw, vw, rest = wi[:, :d], wi[:, d:d + KV_COLS], wi[:, d + KV_COLS:d + 2 * KV_COLS], wi[:, d + 2 * KV_COLS:]
    w_in_b = jnp.concatenate([qw, rest, kw, vw], axis=1).astype(BF16)
    w_out_b = w_out[l].astype(BF16)
    w_r = jnp.concatenate([w_router_group[l],
                           jnp.transpose(w_router_expert[l], (1, 0, 2)).reshape(d, N_EXPERTS),
                           jnp.zeros((d, LANES - N_GROUPS - N_EXPERTS), F32)], axis=1)
    b_r = jnp.concatenate([b_router_group[l], b_router_expert[l].reshape(-1),
                           jnp.zeros((LANES - N_GROUPS - N_EXPERTS,), F32)]).reshape(1, LANES)
    knw = jnp.tile(k_norm_w[l], N_KV).reshape(1, KV_COLS)
    qsc = jnp.tile(q_norm_w[l] * (HEAD_DIM ** -0.5), N_KV).reshape(1, KV_COLS)
    n1w = norm1_w[l].reshape(1, d)
    n2w = norm2_w[l].reshape(1, d)

    mod = _ada(jnp.concatenate([c_prompt, c_sample], axis=0), w_ada[l], b_ada[l]).reshape(bp + bs, 6, d)
    mod_p, mod_s = mod[:bp], mod[bp:]

    proj_p = _inproj(x_prompt, mod_p, n1w, w_in_b, ROW_TILE)
    proj_s = _inproj(x_sample, mod_s, n1w, w_in_b, ts)
    zk = jnp.zeros((bp, WINDOW, KV_COLS), F32)
    zc = jnp.zeros((bp, 2, d), F32)
    merged_p, k_p, v_p, c_p = _mixer(proj_p, knw, qsc, _bias_table(rel_bias, CHUNK), _sink_table(attn_sinks[l], CHUNK),
                                     conv_w[l], zk, zk, zc, MIX_TILE, CHUNK, False)
    merged_s, k_s, v_s, c_s = _mixer(proj_s, knw, qsc, _bias_table(rel_bias, ts), _sink_table(attn_sinks[l], ts),
                                     conv_w[l], state_attn_k[l].reshape(bs, WINDOW, KV_COLS),
                                     state_attn_v[l].reshape(bs, WINDOW, KV_COLS), state_conv[l], ts, ts, True)

    x1_p, h2_p, eid_p, ew_p = _outproj(merged_p, x_prompt, mod_p, w_out_b, n2w, w_r, b_r, ROW_TILE)
    x1_s, h2_s, eid_s, ew_s = _outproj(merged_s, x_sample, mod_s, w_out_b, n2w, w_r, b_r, ts)

    dest, totals = _rank(jnp.concatenate([eid_p, eid_s], axis=0), EXPERT_BLOCK)
    dests = [dest[:n_p], dest[n_p:]]
    nblk = (totals[0, :N_EXPERTS] + EXPERT_BLOCK - 1) // EXPERT_BLOCK
    blk_end = jnp.cumsum(nblk)
    nb_max = -(-2 * n_tok // EXPERT_BLOCK) + N_EXPERTS
    block_e = jnp.minimum(jnp.searchsorted(blk_end, jnp.arange(nb_max), side="right"), N_EXPERTS - 1).astype(I32)
    n_valid = blk_end[-1:].astype(I32)

    xs = _dispatch_rows([h2_p, h2_s], dests, nb_max * EXPERT_BLOCK)
    ys = _experts(xs, block_e, n_valid, w_gate[l], w_up[l], w_down[l], EXPERT_BLOCK)
    (y0_p, y1_p), (y0_s, y1_s) = _collect_rows(ys, dests)

    y_p = _final(x1_p, y0_p, y1_p, ew_p, mod_p, ROW_TILE)
    y_s = _final(x1_s, y0_s, y1_s, ew_s, mod_s, ts)

    kv_shape = (1, -1, WINDOW, N_KV, HEAD_DIM)
    return (y_p, y_s, k_p.reshape(kv_shape), v_p.reshape(kv_shape), c_p[None],
            k_s.reshape(kv_shape), v_s.reshape(kv_shape), c_s[None])
```

```python
import functools
import math

import numpy as np
import jax
import jax.numpy as jnp
from jax import lax
from jax.experimental import pallas as pl
from jax.experimental.pallas import tpu as pltpu
from jax.experimental.pallas import tpu_sc as plsc

F32 = jnp.float32
BF16 = jnp.bfloat16
I32 = jnp.int32

D_MODEL = 1024
HEAD_DIM = 64
N_HEADS = 16
N_KV = 4
GROUP = 4
CHUNK = 64
WINDOW = 128
N_BUCKETS = 32
MAX_DISTANCE = 128
N_GROUPS = 8
EPG = 8
N_EXPERTS = 64
D_EXPERT = 512
RMS_EPS = 1e-6
KV_COLS = N_KV * HEAD_DIM
IN_COLS = 6 * D_MODEL + 2 * KV_COLS
HALF = D_MODEL // 2
LANES = 128

VMEM_LIMIT = 56 * 1024 * 1024
INPROJ_TN = 512
ROW_TILE = 512
MIX_TILE = 256
UNIT_Q = 2 * CHUNK
KEY_WIN = WINDOW + UNIT_Q
EXPERT_BLOCK = 256
RANK_TILE = 512
SC_CORES = 2
SC_SUBCORES = 16
SC_WORKERS = SC_CORES * SC_SUBCORES
SC_MAX_WINDOW = 128


def _cparams(sem):
    return pltpu.CompilerParams(dimension_semantics=sem, vmem_limit_bytes=VMEM_LIMIT)


def _split_bf16(a):
    hi = a.astype(BF16)
    lo = (a - hi.astype(F32)).astype(BF16)
    return hi, lo


def _dot3(a, b):
    ah, al = _split_bf16(a)
    bh, bl = _split_bf16(b)
    d = functools.partial(jnp.dot, preferred_element_type=F32)
    return d(ah, bh) + (d(ah, bl) + d(al, bh))


def _pack_pairs(y):
    a = lax.bitcast_convert_type(y[:, :HALF].astype(BF16).astype(F32), I32)
    b = lax.bitcast_convert_type(y[:, HALF:].astype(BF16).astype(F32), I32)
    return a | lax.shift_right_logical(b, jnp.int32(16))


def _unpack_pairs(w):
    a = lax.bitcast_convert_type(w & jnp.int32(-65536), F32)
    b = lax.bitcast_convert_type(lax.shift_left(w, jnp.int32(16)), F32)
    return a, b


def _ada_kernel(c_ref, w_ref, b_ref, o_ref):
    c = c_ref[...]
    s = c * jax.nn.sigmoid(c)
    o_ref[...] = _dot3(s, w_ref[...]) + b_ref[...]


def _ada(c_all, w_ada, b_ada):
    r, d = c_all.shape
    n = w_ada.shape[1]
    tn = 1024
    return pl.pallas_call(
        _ada_kernel,
        out_shape=jax.ShapeDtypeStruct((r, n), F32),
        grid=(n // tn,),
        in_specs=[pl.BlockSpec((r, d), lambda j: (0, 0)),
                  pl.BlockSpec((d, tn), lambda j: (0, j)),
                  pl.BlockSpec((1, tn), lambda j: (0, j))],
        out_specs=pl.BlockSpec((r, tn), lambda j: (0, j)),
        compiler_params=_cparams(("arbitrary",)),
        name="ada",
    )(c_all, w_ada, b_ada.reshape(1, n))


def _inproj_kernel(x_ref, mod_ref, nw_ref, w_ref, o_ref):
    x = x_ref[0]
    mod = mod_ref[0]
    h = x * lax.rsqrt(jnp.mean(x * x, axis=-1, keepdims=True) + RMS_EPS) * nw_ref[...]
    h = h * (1.0 + mod[1:2]) + mod[0:1]
    hb = h.astype(BF16)
    for j in range(IN_COLS // INPROJ_TN):
        sl = slice(j * INPROJ_TN, (j + 1) * INPROJ_TN)
        o_ref[0, :, sl] = jnp.dot(hb, w_ref[:, sl], preferred_element_type=F32).astype(BF16)


def _inproj(x, mod, nw, w_in_b, tm):
    b, t, d = x.shape
    return pl.pallas_call(
        _inproj_kernel,
        out_shape=jax.ShapeDtypeStruct((b, t, IN_COLS), BF16),
        grid=(b, t // tm),
        in_specs=[pl.BlockSpec((1, tm, d), lambda i, j: (i, j, 0)),
                  pl.BlockSpec((1, 6, d), lambda i, j: (i, 0, 0)),
                  pl.BlockSpec((1, d), lambda i, j: (0, 0)),
                  pl.BlockSpec((d, IN_COLS), lambda i, j: (0, 0), pipeline_mode=pl.Buffered(1))],
        out_specs=pl.BlockSpec((1, tm, IN_COLS), lambda i, j: (i, j, 0)),
        compiler_params=_cparams(("arbitrary", "arbitrary")),
        name="inproj",
    )(x, mod, nw, w_in_b)


def _head_inv_rms(xf, bd):
    hi, lo = _split_bf16(xf * xf)
    ssq = jnp.dot(hi, bd, preferred_element_type=F32) + jnp.dot(lo, bd, preferred_element_type=F32)
    return lax.rsqrt(ssq * (1.0 / HEAD_DIM) + RMS_EPS)


def _mixer_body(q_ref, k_ref, v_ref, bg_ref, c_ref, u_ref, ga_ref, gc_ref,
                knw_ref, qsc_ref, bd_ref, bias_ref, sink_ref, cw_ref,
                kpast_ref, vpast_ref, cpast_ref, upast_ref,
                merged_ref, knew_ref, vnew_ref, cnew_ref,
                kq_buf, vt_buf, attn_buf, *, tq, nq, stateful):
    hb = WINDOW
    pw = UNIT_Q
    t = pl.program_id(1)
    bd = bd_ref[...]

    k = k_ref[0].astype(F32)
    kn = k * _head_inv_rms(k, bd) * knw_ref[...]
    kq = (kn * qsc_ref[...]).astype(BF16)
    vb = v_ref[0]
    vt = vb.astype(F32).T.astype(BF16)

    if stateful:
        kp = kpast_ref[0]
        vp = vpast_ref[0]
        for kv in range(N_KV):
            kq_buf[kv, hb + tq:] = jnp.zeros((KEY_WIN - hb - tq, HEAD_DIM), BF16)
        vt_buf[:, hb + tq:] = jnp.zeros((KV_COLS, KEY_WIN - hb - tq), BF16)
        u_hist = jnp.concatenate([jnp.zeros((6, D_MODEL), F32), cpast_ref[0]], axis=0)
        knew_ref[0] = jnp.concatenate([kp[tq:], kn], axis=0)
        vnew_ref[0] = jnp.concatenate([vp[tq:], vb.astype(F32)], axis=0)
    else:
        kraw = kpast_ref[0].astype(F32)
        kp = kraw * _head_inv_rms(kraw, bd) * knw_ref[...]
        vp = vpast_ref[0].astype(F32)
        u_hist = jnp.where(t == 0, 0.0, cpast_ref[0].astype(F32) * upast_ref[0].astype(F32))
        knew_ref[0] = kn[tq - hb:]
        vnew_ref[0] = vb[tq - hb:].astype(F32)
    kqp = (kp * qsc_ref[...]).astype(BF16)
    for kv in range(N_KV):
        kq_buf[kv, 0:hb] = kqp[:, kv * HEAD_DIM:(kv + 1) * HEAD_DIM]
        kq_buf[kv, hb:hb + tq] = kq[:, kv * HEAD_DIM:(kv + 1) * HEAD_DIM]
    vt_buf[:, 0:hb] = vp.T.astype(BF16)
    vt_buf[:, hb:hb + tq] = vt

    q = q_ref[0]
    for kv in range(N_KV):
        qf = q[:, kv * KV_COLS:(kv + 1) * KV_COLS].astype(F32)
        qn = (qf * _head_inv_rms(qf, bd)).astype(BF16)
        for u in range(tq // nq):
            r0 = u * nq
            parts = [qn[r0:r0 + nq, g * HEAD_DIM:(g + 1) * HEAD_DIM] for g in range(GROUP)]
            if nq < pw:
                zpad = jnp.zeros((pw - nq, HEAD_DIM), BF16)
                parts = [x for p_ in parts for x in (p_, zpad)]
            qs = jnp.concatenate(parts, axis=0)
            kw = kq_buf[kv, r0:r0 + KEY_WIN]
            st = lax.dot_general(kw, qs, (((1,), (1,)), ((), ())), preferred_element_type=F32)
            if (not stateful) and u == 0:
                bias = jnp.where(t == 0, bias_ref[kv + N_KV], bias_ref[kv])
            else:
                bias = bias_ref[kv]
            st = st + bias
            sink = sink_ref[kv]
            m = jnp.maximum(jnp.max(st, axis=0, keepdims=True), sink)
            p = jnp.exp(st - m)
            den = jnp.sum(p, axis=0, keepdims=True) + jnp.exp(sink - m)
            ot = jnp.dot(vt_buf[kv * HEAD_DIM:(kv + 1) * HEAD_DIM, r0:r0 + KEY_WIN], p.astype(BF16),
                         preferred_element_type=F32) / den
            for gp in range(GROUP // 2):
                blk = jnp.concatenate([ot[:, (2 * gp) * pw:(2 * gp + 1) * pw],
                                       ot[:, (2 * gp + 1) * pw:(2 * gp + 2) * pw]], axis=0)
                c0 = (kv * GROUP + 2 * gp) * HEAD_DIM
                attn_buf[r0:r0 + nq, c0:c0 + 2 * HEAD_DIM] = blk.T[:nq]

    cu = c_ref[0].astype(F32) * u_ref[0].astype(F32)
    u_all = jnp.concatenate([u_hist, cu], axis=0)
    cw = cw_ref[...]
    conv = cw[0:1] * u_all[6:6 + tq] + cw[1:2] * u_all[7:7 + tq] + cw[2:3] * u_all[8:8 + tq]
    cnew_ref[0] = u_all[tq + 6:tq + 8]

    merged = (jax.nn.sigmoid(ga_ref[0].astype(F32)) * attn_buf[...]
              + jax.nn.sigmoid(gc_ref[0].astype(F32)) * (bg_ref[0].astype(F32) * conv))
    merged_ref[0] = merged.astype(BF16)


def _mixer(proj, knw, qsc, bias_tab, sink_tab, conv_w, state, tq, nq):
    b, t, _ = proj.shape
    d = D_MODEL
    stateful = state is not None
    key_rows = max(WINDOW + tq, KEY_WIN)
    r = np.arange(KV_COLS) // HEAD_DIM
    bd = jnp.asarray((r[:, None] == r[None, :]).astype(np.float32), BF16)
    wide = lambda j: pl.BlockSpec((1, tq, d), lambda i, s, j=j: (i, s, j))
    kvspec = lambda j: pl.BlockSpec((1, tq, KV_COLS), lambda i, s, j=j: (i, s, j))
    const2 = lambda shp: pl.BlockSpec(shp, lambda i, s: (0, 0))
    const3 = lambda shp: pl.BlockSpec(shp, lambda i, s: (0, 0, 0))
    per_b = lambda shp: pl.BlockSpec(shp, lambda i, s: (i, 0, 0))
    kvblk = 6 * d // KV_COLS
    if stateful:
        hist_specs = [per_b((1, WINDOW, KV_COLS)), per_b((1, WINDOW, KV_COLS)), per_b((1, 2, d)), per_b((1, 2, d))]
        hist_args = [state[0], state[1], state[2], state[2]]
    else:
        kw_ = tq // WINDOW
        prev_kv = lambda j: pl.BlockSpec((1, WINDOW, KV_COLS),
                                         lambda i, s, j=j: (i, jnp.maximum(s * kw_ - 1, 0), j))
        prev8 = lambda j: pl.BlockSpec((1, 8, d), lambda i, s, j=j: (i, jnp.maximum(s * (tq // 8) - 1, 0), j))
        hist_specs = [prev_kv(kvblk), prev_kv(kvblk + 1), prev8(2), prev8(3)]
        hist_args = [proj, proj, proj, proj]
    in_specs = [wide(0), kvspec(kvblk), kvspec(kvblk + 1), wide(1), wide(2), wide(3), wide(4), wide(5),
                const2((1, KV_COLS)), const2((1, KV_COLS)), const2((KV_COLS, KV_COLS)),
                const3(bias_tab.shape), const3(sink_tab.shape), const2((3, d))] + hist_specs
    out_shape = (jax.ShapeDtypeStruct((b, t, d), BF16),
                 jax.ShapeDtypeStruct((b, WINDOW, KV_COLS), F32),
                 jax.ShapeDtypeStruct((b, WINDOW, KV_COLS), F32),
                 jax.ShapeDtypeStruct((b, 2, d), F32))
    out_specs = (pl.BlockSpec((1, tq, d), lambda i, s: (i, s, 0)),
                 per_b((1, WINDOW, KV_COLS)), per_b((1, WINDOW, KV_COLS)), per_b((1, 2, d)))
    return pl.pallas_call(
        functools.partial(_mixer_body, tq=tq, nq=nq, stateful=stateful),
        out_shape=out_shape,
        grid=(b, t // tq),
        in_specs=in_specs,
        out_specs=out_specs,
        scratch_shapes=[pltpu.VMEM((N_KV, key_rows, HEAD_DIM), BF16),
                        pltpu.VMEM((KV_COLS, key_rows), BF16),
                        pltpu.VMEM((tq, d), F32)],
        compiler_params=_cparams(("arbitrary", "arbitrary")),
        name="mixer_state" if stateful else "mixer",
    )(proj, proj, proj, proj, proj, proj, proj, proj, knw, qsc, bd, bias_tab, sink_tab, conv_w, *hist_args)


def _route(logits):
    lane = lax.broadcasted_iota(I32, logits.shape, 1)
    neg = -jnp.inf
    big = jnp.int32(1 << 20)
    gl = jnp.where(lane < N_GROUPS, logits, neg)
    gmax = jnp.max(gl, axis=-1, keepdims=True)
    g_idx = jnp.min(jnp.where(gl == gmax, lane, big), axis=-1, keepdims=True)
    g_w = 1.0 / jnp.sum(jnp.exp(gl - gmax), axis=-1, keepdims=True)
    lo = N_GROUPS + g_idx * EPG
    el = jnp.where((lane >= lo) & (lane < lo + EPG), logits, neg)
    m1 = jnp.max(el, axis=-1, keepdims=True)
    i1 = jnp.min(jnp.where(el == m1, lane, big), axis=-1, keepdims=True)
    el2 = jnp.where(lane == i1, neg, el)
    m2 = jnp.max(el2, axis=-1, keepdims=True)
    i2 = jnp.min(jnp.where(el2 == m2, lane, big), axis=-1, keepdims=True)
    r = jnp.exp(m2 - m1)
    w1 = 1.0 / (1.0 + r)
    w2 = r / (1.0 + r)
    return i1 - N_GROUPS, i2 - N_GROUPS, g_w * w1, g_w * w2


def _outproj_kernel(m_ref, x_ref, mod_ref, wo_ref, nw_ref, wr_ref, br_ref,
                    x1_ref, h2_ref, eid_ref, ew_ref):
    mod = mod_ref[0]
    mix = jnp.dot(m_ref[0], wo_ref[...], preferred_element_type=F32)
    x1 = x_ref[0] + mod[2:3] * mix
    x1_ref[0] = x1
    h = x1 * lax.rsqrt(jnp.mean(x1 * x1, axis=-1, keepdims=True) + RMS_EPS) * nw_ref[...]
    h = h * (1.0 + mod[4:5]) + mod[3:4]
    h2_ref[...] = _pack_pairs(h)
    tm = h.shape[0]
    h_hi, h_lo = _split_bf16(h)
    prod = jnp.dot(jnp.concatenate([h_hi, h_lo], axis=0), wr_ref[...], preferred_element_type=F32)
    logits = prod[:tm, :LANES] + (prod[:tm, LANES:] + prod[tm:, :LANES]) + br_ref[...]
    e1, e2, w1, w2 = _route(logits)
    lane8 = lax.broadcasted_iota(I32, (h.shape[0], 8), 1)
    eid_ref[...] = jnp.where(lane8 == 0, e1, jnp.where(lane8 == 1, e2, 0))
    ew_ref[...] = jnp.where(lane8 == 0, w1, jnp.where(lane8 == 1, w2, 0.0))


def _outproj(merged, x, mod, w_out_b, nw, w_r, b_r, tm):
    b, t, d = x.shape
    nt = t // tm
    flat = lambda i, j: (i * nt + j, 0)
    return pl.pallas_call(
        _outproj_kernel,
        out_shape=(jax.ShapeDtypeStruct((b, t, d), F32),
                   jax.ShapeDtypeStruct((b * t, HALF), I32),
                   jax.ShapeDtypeStruct((b * t, 8), I32),
                   jax.ShapeDtypeStruct((b * t, 8), F32)),
        grid=(b, nt),
        in_specs=[pl.BlockSpec((1, tm, d), lambda i, j: (i, j, 0)),
                  pl.BlockSpec((1, tm, d), lambda i, j: (i, j, 0)),
                  pl.BlockSpec((1, 6, d), lambda i, j: (i, 0, 0)),
                  pl.BlockSpec((d, d), lambda i, j: (0, 0)),
                  pl.BlockSpec((1, d), lambda i, j: (0, 0)),
                  pl.BlockSpec((d, 2 * LANES), lambda i, j: (0, 0)),
                  pl.BlockSpec((1, LANES), lambda i, j: (0, 0))],
        out_specs=(pl.BlockSpec((1, tm, d), lambda i, j: (i, j, 0)),
                   pl.BlockSpec((tm, HALF), flat),
                   pl.BlockSpec((tm, 8), flat),
                   pl.BlockSpec((tm, 8), flat)),
        compiler_params=_cparams(("arbitrary", "arbitrary")),
        name="outproj",
    )(merged, x, mod, w_out_b, nw, w_r, b_r)


def _rank_kernel(eid_ref, tri_ref, upper_ref, dest_ref, tot_ref, cnt, starts, *, block):
    ph = pl.program_id(0)
    i = pl.program_id(1)
    tm = eid_ref.shape[0]
    lane = lax.broadcasted_iota(I32, (tm, LANES), 1)
    e0 = eid_ref[:, 0:1]
    e1 = eid_ref[:, 1:2]
    hot0 = lane == e0
    hot1 = lane == e1
    onehot = jnp.where(hot0 | hot1, 1.0, 0.0)
    colsum = jnp.sum(onehot, axis=0, keepdims=True)

    @pl.when((ph == 0) & (i == 0))
    def _():
        cnt[...] = jnp.zeros_like(cnt)

    @pl.when(ph == 0)
    def _():
        cnt[0:1] = cnt[0:1] + colsum

    @pl.when((ph == 1) & (i == 0))
    def _():
        tot = cnt[0:1]
        tot_ref[...] = tot.astype(I32)
        nblk = jnp.floor((tot + (block - 1)) * (1.0 / block))
        hi = jnp.floor(nblk * (1.0 / 16.0))
        lo = nblk - hi * 16.0
        up = upper_ref[...]
        excl = (jnp.dot(jnp.broadcast_to(hi, (8, LANES)).astype(BF16), up, preferred_element_type=F32) * 16.0
                + jnp.dot(jnp.broadcast_to(lo, (8, LANES)).astype(BF16), up, preferred_element_type=F32))
        starts[...] = excl * float(block)
        cnt[...] = jnp.zeros_like(cnt)

    @pl.when(ph == 1)
    def _():
        prefix = jnp.dot(tri_ref[...], onehot.astype(BF16), preferred_element_type=F32)
        pos = prefix + (starts[0:1] + cnt[0:1])
        d0 = jnp.sum(jnp.where(hot0, pos, 0.0), axis=-1, keepdims=True).astype(I32)
        d1 = jnp.sum(jnp.where(hot1, pos, 0.0), axis=-1, keepdims=True).astype(I32)
        lane8 = lax.broadcasted_iota(I32, (tm, 8), 1)
        dest_ref[...] = jnp.where(lane8 == 0, d0, jnp.where(lane8 == 1, d1, 0))
        cnt[0:1] = cnt[0:1] + colsum


def _rank(eid, block):
    n = eid.shape[0]
    tm = math.gcd(n, RANK_TILE)
    r = np.arange(tm)
    tri = jnp.asarray((r[:, None] > r[None, :]).astype(np.float32), BF16)
    l = np.arange(LANES)
    upper = jnp.asarray(((l[:, None] < l[None, :]) & (l[None, :] < N_EXPERTS)).astype(np.float32), BF16)
    return pl.pallas_call(
        functools.partial(_rank_kernel, block=block),
        out_shape=(jax.ShapeDtypeStruct((n, 8), I32), jax.ShapeDtypeStruct((1, LANES), I32)),
        grid=(2, n // tm),
        in_specs=[pl.BlockSpec((tm, 8), lambda p, i: (i, 0)),
                  pl.BlockSpec((tm, tm), lambda p, i: (0, 0)),
                  pl.BlockSpec((LANES, LANES), lambda p, i: (0, 0))],
        out_specs=(pl.BlockSpec((tm, 8), lambda p, i: (i * p, 0)),
                   pl.BlockSpec((1, LANES), lambda p, i: (0, 0))),
        scratch_shapes=[pltpu.VMEM((8, LANES), F32), pltpu.VMEM((8, LANES), F32)],
        compiler_params=_cparams(("arbitrary", "arbitrary")),
        name="rank",
    )(eid, tri, upper)


def _expert_kernel(be_ref, nv_ref, xs_ref, wg_ref, wu_ref, wd_ref, ys_ref, wg_s, wu_s, wd_s):
    b = pl.program_id(0)

    @pl.when(b < nv_ref[0])
    def _():
        prev = be_ref[jnp.maximum(b - 1, 0)]
        fresh = (b == 0) | (be_ref[b] != prev)

        @pl.when(fresh)
        def _():
            wg_s[...] = wg_ref[0].astype(BF16)
            wu_s[...] = wu_ref[0].astype(BF16)
            wd_s[...] = wd_ref[0].astype(BF16)

        a, c = _unpack_pairs(xs_ref[...])
        x = jnp.concatenate([a.astype(BF16), c.astype(BF16)], axis=1)
        g = jnp.dot(x, wg_s[...], preferred_element_type=F32)
        u = jnp.dot(x, wu_s[...], preferred_element_type=F32)
        hmid = (g * jax.nn.sigmoid(g) * u).astype(BF16)
        ys_ref[...] = _pack_pairs(jnp.dot(hmid, wd_s[...], preferred_element_type=F32))


def _experts(xs, block_e, n_valid, w_gate, w_up, w_down, block):
    n_rows = xs.shape[0]
    nb = n_rows // block
    rowblk = lambda b, be, nv: (jnp.minimum(b, nv[0] - 1), 0)
    wblk = lambda b, be, nv: (be[jnp.minimum(b, nv[0] - 1)], 0, 0)
    grid_spec = pltpu.PrefetchScalarGridSpec(
        num_scalar_prefetch=2,
        grid=(nb,),
        in_specs=[pl.BlockSpec((block, HALF), rowblk),
                  pl.BlockSpec((1, D_MODEL, D_EXPERT), wblk),
                  pl.BlockSpec((1, D_MODEL, D_EXPERT), wblk),
                  pl.BlockSpec((1, D_EXPERT, D_MODEL), wblk)],
        out_specs=pl.BlockSpec((block, HALF), rowblk),
        scratch_shapes=[pltpu.VMEM((D_MODEL, D_EXPERT), BF16),
                        pltpu.VMEM((D_MODEL, D_EXPERT), BF16),
                        pltpu.VMEM((D_EXPERT, D_MODEL), BF16)])
    return pl.pallas_call(
        _expert_kernel,
        out_shape=jax.ShapeDtypeStruct((n_rows, HALF), I32),
        grid_spec=grid_spec,
        compiler_params=_cparams(("arbitrary",)),
        name="experts",
    )(block_e, n_valid, xs, w_gate, w_up, w_down)


def _final_kernel(x1_ref, y0_ref, y1_ref, ew_ref, mod_ref, o_ref):
    a0, b0 = _unpack_pairs(y0_ref[...])
    a1, b1 = _unpack_pairs(y1_ref[...])
    w0 = ew_ref[:, 0:1]
    w1 = ew_ref[:, 1:2]
    moe = jnp.concatenate([w0 * a0 + w1 * a1, w0 * b0 + w1 * b1], axis=1)
    o_ref[0] = x1_ref[0] + mod_ref[0][5:6] * moe


def _final(x1, y0, y1, ew, mod, tm):
    b, t, d = x1.shape
    nt = t // tm
    flat = lambda i, j: (i * nt + j, 0)
    return pl.pallas_call(
        _final_kernel,
        out_shape=jax.ShapeDtypeStruct((b, t, d), F32),
        grid=(b, nt),
        in_specs=[pl.BlockSpec((1, tm, d), lambda i, j: (i, j, 0)),
                  pl.BlockSpec((tm, HALF), flat),
                  pl.BlockSpec((tm, HALF), flat),
                  pl.BlockSpec((tm, 8), flat),
                  pl.BlockSpec((1, 6, d), lambda i, j: (i, 0, 0))],
        out_specs=pl.BlockSpec((1, tm, d), lambda i, j: (i, j, 0)),
        compiler_params=_cparams(("arbitrary", "arbitrary")),
        name="final",
    )(x1, y0, y1, ew, mod)


def _sc_window(rows_per_worker):
    for w in range(SC_MAX_WINDOW, 7, -8):
        if rows_per_worker % w == 0:
            return w
    raise ValueError(f"no SparseCore window divides {rows_per_worker} rows per worker")


def _sc_split(idx):
    n = idx.shape[0]
    per = n // SC_WORKERS
    assert per * SC_WORKERS == n
    win = _sc_window(per)
    return idx.reshape(SC_WORKERS, per // win, win), per // win, win


def _sc_worker_id():
    return lax.axis_index("s") * SC_CORES + lax.axis_index("c")


def _dispatch_rows(h2_groups, dest_groups, n_rows):
    splits = [(_sc_split(d[:, 0]), _sc_split(d[:, 1])) for d in dest_groups]
    ng = len(h2_groups)
    scratch = []
    for (_, _, win), _ in splits:
        scratch += [pltpu.VMEM((win,), I32), pltpu.VMEM((win,), I32), pltpu.VMEM((win, HALF), I32)]

    @functools.partial(
        pl.kernel,
        mesh=plsc.VectorSubcoreMesh(core_axis_name="c", subcore_axis_name="s"),
        out_type=jax.ShapeDtypeStruct((n_rows, HALF), I32),
        scratch_types=scratch,
        name="sc_dispatch",
    )
    def k(*refs):
        x_refs, idx_refs, o_hbm, bufs = refs[:ng], refs[ng:3 * ng], refs[3 * ng], refs[3 * ng + 1:]
        wid = _sc_worker_id()
        for g in range(ng):
            (_, nwin, win), _ = splits[g]
            x_hbm, d0_hbm, d1_hbm = x_refs[g], idx_refs[2 * g], idx_refs[2 * g + 1]
            i0_v, i1_v, rows_v = bufs[3 * g:3 * g + 3]

            @pl.loop(0, nwin)
            def _(j, nwin=nwin, win=win, x_hbm=x_hbm, d0_hbm=d0_hbm, d1_hbm=d1_hbm,
                  i0_v=i0_v, i1_v=i1_v, rows_v=rows_v):
                base = pl.multiple_of((wid * nwin + j) * win, 8)
                pltpu.sync_copy(d0_hbm.at[wid, j], i0_v)
                pltpu.sync_copy(d1_hbm.at[wid, j], i1_v)
                pltpu.sync_copy(x_hbm.at[pl.ds(base, win)], rows_v)
                pltpu.sync_copy(rows_v, o_hbm.at[i0_v])
                pltpu.sync_copy(rows_v, o_hbm.at[i1_v])

    idx_args = []
    for (s0, s1) in splits:
        idx_args += [s0[0], s1[0]]
    return k(*h2_groups, *idx_args)


def _collect_rows(ys, dest_groups):
    splits = [(_sc_split(d[:, 0]), _sc_split(d[:, 1])) for d in dest_groups]
    ng = len(dest_groups)
    outs, scratch = [], []
    for d, ((_, _, win), _) in zip(dest_groups, splits):
        o = jax.ShapeDtypeStruct((d.shape[0], HALF), I32)
        outs += [o, o]
        scratch += [pltpu.VMEM((win,), I32), pltpu.VMEM((win, HALF), I32)]

    @functools.partial(
        pl.kernel,
        mesh=plsc.VectorSubcoreMesh(core_axis_name="c", subcore_axis_name="s"),
        out_type=tuple(outs),
        scratch_types=scratch,
        name="sc_collect",
    )
    def k(*refs):
        ys_hbm, idx_refs, out_refs, bufs = refs[0], refs[1:1 + 2 * ng], refs[1 + 2 * ng:1 + 4 * ng], refs[1 + 4 * ng:]
        wid = _sc_worker_id()
        for g in range(ng):
            (_, nwin, win), _ = splits[g]
            i_v, rows_v = bufs[2 * g:2 * g + 2]
            for kk in range(2):
                d_hbm, y_hbm = idx_refs[2 * g + kk], out_refs[2 * g + kk]

                @pl.loop(0, nwin)
                def _(j, nwin=nwin, win=win, d_hbm=d_hbm, y_hbm=y_hbm, i_v=i_v, rows_v=rows_v):
                    base = pl.multiple_of((wid * nwin + j) * win, 8)
                    pltpu.sync_copy(d_hbm.at[wid, j], i_v)
                    pltpu.sync_copy(ys_hbm.at[i_v], rows_v)
                    pltpu.sync_copy(rows_v, y_hbm.at[pl.ds(base, win)])

    idx_args = []
    for (s0, s1) in splits:
        idx_args += [s0[0], s1[0]]
    res = k(ys, *idx_args)
    return [(res[2 * g], res[2 * g + 1]) for g in range(ng)]


def _t5_bucket(rel):
    half = N_BUCKETS // 2
    max_exact = half // 2
    n = jnp.abs(rel)
    far = max_exact + (jnp.log(jnp.maximum(n, 1).astype(F32) / max_exact)
                       / math.log(MAX_DISTANCE / max_exact) * (half - max_exact)).astype(I32)
    far = jnp.minimum(far, half - 1)
    return jnp.where(rel > 0, half, 0) + jnp.where(n < max_exact, n, far)


def _bias_table(rel_bias, cq, nq, no_history):
    nk = WINDOW + cq
    j = jnp.arange(KEY_WIN)[:, None]
    c = jnp.arange(UNIT_Q)[None, :]
    jj = j - (c // cq) * cq
    valid = (jj >= 0) & (jj < nk) & (c < nq)
    if no_history:
        valid = valid & (j >= WINDOW)
    rel = jj - WINDOW - (c % cq)
    bias = rel_bias.astype(F32)[_t5_bucket(rel)]
    bias = jnp.where(valid[:, :, None], bias, -jnp.inf)
    bias = jnp.transpose(bias.reshape(KEY_WIN, UNIT_Q, N_KV, GROUP), (2, 0, 3, 1))
    return bias.reshape(N_KV, KEY_WIN, GROUP * UNIT_Q)


def _sink_table(sinks):
    s = sinks.astype(F32).reshape(N_KV, 1, GROUP, 1)
    return jnp.broadcast_to(s, (N_KV, 1, GROUP, UNIT_Q)).reshape(N_KV, 1, GROUP * UNIT_Q)


def kernel(x_prompt, x_sample, state_attn_k, state_attn_v, state_conv, c_prompt, c_sample,
           rel_bias, w_ada, b_ada, norm1_w, w_in, q_norm_w, k_norm_w, attn_sinks, conv_w,
           w_out, norm2_w, w_router_group, b_router_group, w_router_expert, b_router_expert,
           w_gate, w_up, w_down):
    depth = w_ada.shape[0]
    assert depth == 1
    bp, tp, d = x_prompt.shape
    bs, ts, _ = x_sample.shape
    n_p, n_s = bp * tp, bs * ts
    n_tok = n_p + n_s
    l = 0

    wi = w_in[l]
    qw, kw, vw, rest = wi[:, :d], wi[:, d:d + KV_COLS], wi[:, d + KV_COLS:d + 2 * KV_COLS], wi[:, d + 2 * KV_COLS:]
    w_in_b = jnp.concatenate([qw, rest, kw, vw], axis=1).astype(BF16)
    w_out_b = w_out[l].astype(BF16)
    w_r = jnp.concatenate([w_router_group[l],
                           jnp.transpose(w_router_expert[l], (1, 0, 2)).reshape(d, N_EXPERTS),
                           jnp.zeros((d, LANES - N_GROUPS - N_EXPERTS), F32)], axis=1)
    w_r_hi = lax.reduce_precision(w_r, exponent_bits=8, mantissa_bits=7)
    w_r = jnp.concatenate([w_r_hi.astype(BF16), (w_r - w_r_hi).astype(BF16)], axis=1)
    b_r = jnp.concatenate([b_router_group[l], b_router_expert[l].reshape(-1),
                           jnp.zeros((LANES - N_GROUPS - N_EXPERTS,), F32)]).reshape(1, LANES)
    knw = jnp.tile(k_norm_w[l], N_KV).reshape(1, KV_COLS)
    qsc = jnp.tile(q_norm_w[l] * (HEAD_DIM ** -0.5), N_KV).reshape(1, KV_COLS)
    n1w = norm1_w[l].reshape(1, d)
    n2w = norm2_w[l].reshape(1, d)

    mod = _ada(jnp.concatenate([c_prompt, c_sample], axis=0), w_ada[l], b_ada[l]).reshape(bp + bs, 6, d)
    mod_p, mod_s = mod[:bp], mod[bp:]

    proj_p = _inproj(x_prompt, mod_p, n1w, w_in_b, ROW_TILE)
    proj_s = _inproj(x_sample, mod_s, n1w, w_in_b, ts)
    sink_tab = _sink_table(attn_sinks[l])
    bias_p = jnp.concatenate([_bias_table(rel_bias, CHUNK, UNIT_Q, False),
                              _bias_table(rel_bias, CHUNK, UNIT_Q, True)], axis=0)
    merged_p, k_p, v_p, c_p = _mixer(proj_p, knw, qsc, bias_p, sink_tab, conv_w[l], None, MIX_TILE, UNIT_Q)
    state = (state_attn_k[l].reshape(bs, WINDOW, KV_COLS), state_attn_v[l].reshape(bs, WINDOW, KV_COLS),
             state_conv[l])
    merged_s, k_s, v_s, c_s = _mixer(proj_s, knw, qsc, _bias_table(rel_bias, ts, ts, False), sink_tab, conv_w[l],
                                     state, ts, ts)

    x1_p, h2_p, eid_p, ew_p = _outproj(merged_p, x_prompt, mod_p, w_out_b, n2w, w_r, b_r, ROW_TILE)
    x1_s, h2_s, eid_s, ew_s = _outproj(merged_s, x_sample, mod_s, w_out_b, n2w, w_r, b_r, ts)

    dest, totals = _rank(jnp.concatenate([eid_p, eid_s], axis=0), EXPERT_BLOCK)
    dests = [dest[:n_p], dest[n_p:]]
    nblk = (totals[0, :N_EXPERTS] + EXPERT_BLOCK - 1) // EXPERT_BLOCK
    blk_end = jnp.cumsum(nblk)
    nb_max = -(-2 * n_tok // EXPERT_BLOCK) + N_EXPERTS
    block_e = jnp.minimum(jnp.sum(blk_end[None, :] <= jnp.arange(nb_max)[:, None], axis=1), N_EXPERTS - 1).astype(I32)
    n_valid = blk_end[-1:].astype(I32)

    xs = _dispatch_rows([h2_p, h2_s], dests, nb_max * EXPERT_BLOCK)
    ys = _experts(xs, block_e, n_valid, w_gate[l], w_up[l], w_down[l], EXPERT_BLOCK)
    (y0_p, y1_p), (y0_s, y1_s) = _collect_rows(ys, dests)

    y_p = _final(x1_p, y0_p, y1_p, ew_p, mod_p, ROW_TILE)
    y_s = _final(x1_s, y0_s, y1_s, ew_s, mod_s, ts)

    kv_shape = (1, -1, WINDOW, N_KV, HEAD_DIM)
    return (y_p, y_s, k_p.reshape(kv_shape), v_p.reshape(kv_shape), c_p[None],
            k_s.reshape(kv_shape), v_s.reshape(kv_shape), c_s[None])
```

```python
import functools
import math

import numpy as np
import jax
import jax.numpy as jnp
from jax import lax
from jax.experimental import pallas as pl
from jax.experimental.pallas import tpu as pltpu
from jax.experimental.pallas import tpu_sc as plsc

F32 = jnp.float32
BF16 = jnp.bfloat16
I32 = jnp.int32

D_MODEL = 1024
HEAD_DIM = 64
N_HEADS = 16
N_KV = 4
GROUP = 4
CHUNK = 64
WINDOW = 128
N_BUCKETS = 32
MAX_DISTANCE = 128
N_GROUPS = 8
EPG = 8
N_EXPERTS = 64
D_EXPERT = 512
RMS_EPS = 1e-6
KV_COLS = N_KV * HEAD_DIM
IN_COLS = 6 * D_MODEL + 2 * KV_COLS
HALF = D_MODEL // 2
LANES = 128

VMEM_LIMIT = 56 * 1024 * 1024
INPROJ_TN = 512
ROW_TILE = 512
MIX_TILE = 256
UNIT_Q = 2 * CHUNK
KEY_WIN = WINDOW + UNIT_Q
EXPERT_BLOCK = 256
RANK_TILE = 512
SC_CORES = 2
SC_SUBCORES = 16
SC_WORKERS = SC_CORES * SC_SUBCORES
SC_MAX_WINDOW = 128


def _cparams(sem):
    return pltpu.CompilerParams(dimension_semantics=sem, vmem_limit_bytes=VMEM_LIMIT)


def _split_bf16(a):
    hi = a.astype(BF16)
    lo = (a - hi.astype(F32)).astype(BF16)
    return hi, lo


def _dot3(a, b):
    ah, al = _split_bf16(a)
    bh, bl = _split_bf16(b)
    d = functools.partial(jnp.dot, preferred_element_type=F32)
    return d(ah, bh) + (d(ah, bl) + d(al, bh))


def _pack_pairs(y):
    a = lax.bitcast_convert_type(y[:, :HALF].astype(BF16).astype(F32), I32)
    b = lax.bitcast_convert_type(y[:, HALF:].astype(BF16).astype(F32), I32)
    return a | lax.shift_right_logical(b, jnp.int32(16))


def _unpack_pairs(w):
    a = lax.bitcast_convert_type(w & jnp.int32(-65536), F32)
    b = lax.bitcast_convert_type(lax.shift_left(w, jnp.int32(16)), F32)
    return a, b


def _ada_kernel(c_ref, w_ref, b_ref, o_ref):
    c = c_ref[...]
    s = c * jax.nn.sigmoid(c)
    o_ref[...] = _dot3(s, w_ref[...]) + b_ref[...]


def _ada(c_all, w_ada, b_ada):
    r, d = c_all.shape
    n = w_ada.shape[1]
    tn = 1024
    return pl.pallas_call(
        _ada_kernel,
        out_shape=jax.ShapeDtypeStruct((r, n), F32),
        grid=(n // tn,),
        in_specs=[pl.BlockSpec((r, d), lambda j: (0, 0)),
                  pl.BlockSpec((d, tn), lambda j: (0, j)),
                  pl.BlockSpec((1, tn), lambda j: (0, j))],
        out_specs=pl.BlockSpec((r, tn), lambda j: (0, j)),
        compiler_params=_cparams(("arbitrary",)),
        name="ada",
    )(c_all, w_ada, b_ada.reshape(1, n))


def _inproj_kernel(x_ref, mod_ref, nw_ref, w_ref, o_ref):
    x = x_ref[0]
    mod = mod_ref[0]
    h = x * lax.rsqrt(jnp.mean(x * x, axis=-1, keepdims=True) + RMS_EPS) * nw_ref[...]
    h = h * (1.0 + mod[1:2]) + mod[0:1]
    hb = h.astype(BF16)
    for j in range(IN_COLS // INPROJ_TN):
        sl = slice(j * INPROJ_TN, (j + 1) * INPROJ_TN)
        o_ref[0, :, sl] = jnp.dot(hb, w_ref[:, sl], preferred_element_type=F32).astype(BF16)


def _inproj(x, mod, nw, w_in_b, tm):
    b, t, d = x.shape
    return pl.pallas_call(
        _inproj_kernel,
        out_shape=jax.ShapeDtypeStruct((b, t, IN_COLS), BF16),
        grid=(b, t // tm),
        in_specs=[pl.BlockSpec((1, tm, d), lambda i, j: (i, j, 0)),
                  pl.BlockSpec((1, 6, d), lambda i, j: (i, 0, 0)),
                  pl.BlockSpec((1, d), lambda i, j: (0, 0)),
                  pl.BlockSpec((d, IN_COLS), lambda i, j: (0, 0), pipeline_mode=pl.Buffered(1))],
        out_specs=pl.BlockSpec((1, tm, IN_COLS), lambda i, j: (i, j, 0)),
        compiler_params=_cparams(("arbitrary", "arbitrary")),
        name="inproj",
    )(x, mod, nw, w_in_b)


def _head_inv_rms(xf, bd):
    hi, lo = _split_bf16(xf * xf)
    ssq = jnp.dot(hi, bd, preferred_element_type=F32) + jnp.dot(lo, bd, preferred_element_type=F32)
    return lax.rsqrt(ssq * (1.0 / HEAD_DIM) + RMS_EPS)


def _mixer_body(q_ref, k_ref, v_ref, bg_ref, c_ref, u_ref, ga_ref, gc_ref,
                knw_ref, qsc_ref, bd_ref, bias_ref, sink_ref, cw_ref,
                kpast_ref, vpast_ref, cpast_ref, upast_ref,
                merged_ref, knew_ref, vnew_ref, cnew_ref,
                kq_buf, vt_buf, attn_buf, *, tq, nq, stateful):
    hb = WINDOW
    pw = UNIT_Q
    t = pl.program_id(1)
    bd = bd_ref[...]

    k = k_ref[0].astype(F32)
    kn = k * _head_inv_rms(k, bd) * knw_ref[...]
    kq = (kn * qsc_ref[...]).astype(BF16)
    vb = v_ref[0]
    vt = vb.astype(F32).T.astype(BF16)

    if stateful:
        kp = kpast_ref[0]
        vp = vpast_ref[0]
        for kv in range(N_KV):
            kq_buf[kv, hb + tq:] = jnp.zeros((KEY_WIN - hb - tq, HEAD_DIM), BF16)
        vt_buf[:, hb + tq:] = jnp.zeros((KV_COLS, KEY_WIN - hb - tq), BF16)
        u_hist = jnp.concatenate([jnp.zeros((6, D_MODEL), F32), cpast_ref[0]], axis=0)
        knew_ref[0] = jnp.concatenate([kp[tq:], kn], axis=0)
        vnew_ref[0] = jnp.concatenate([vp[tq:], vb.astype(F32)], axis=0)
    else:
        kraw = kpast_ref[0].astype(F32)
        kp = kraw * _head_inv_rms(kraw, bd) * knw_ref[...]
        vp = vpast_ref[0].astype(F32)
        u_hist = jnp.where(t == 0, 0.0, cpast_ref[0].astype(F32) * upast_ref[0].astype(F32))
        knew_ref[0] = kn[tq - hb:]
        vnew_ref[0] = vb[tq - hb:].astype(F32)
    kqp = (kp * qsc_ref[...]).astype(BF16)
    for kv in range(N_KV):
        kq_buf[kv, 0:hb] = kqp[:, kv * HEAD_DIM:(kv + 1) * HEAD_DIM]
        kq_buf[kv, hb:hb + tq] = kq[:, kv * HEAD_DIM:(kv + 1) * HEAD_DIM]
    vt_buf[:, 0:hb] = vp.T.astype(BF16)
    vt_buf[:, hb:hb + tq] = vt

    q = q_ref[0]
    for kv in range(N_KV):
        qf = q[:, kv * KV_COLS:(kv + 1) * KV_COLS].astype(F32)
        qn = (qf * _head_inv_rms(qf, bd)).astype(BF16)
        for u in range(tq // nq):
            r0 = u * nq
            parts = [qn[r0:r0 + nq, g * HEAD_DIM:(g + 1) * HEAD_DIM] for g in range(GROUP)]
            if nq < pw:
                zpad = jnp.zeros((pw - nq, HEAD_DIM), BF16)
                parts = [x for p_ in parts for x in (p_, zpad)]
            qs = jnp.concatenate(parts, axis=0)
            kw = kq_buf[kv, r0:r0 + KEY_WIN]
            st = lax.dot_general(kw, qs, (((1,), (1,)), ((), ())), preferred_element_type=F32)
            if (not stateful) and u == 0:
                bias = jnp.where(t == 0, bias_ref[kv + N_KV], bias_ref[kv])
            else:
                bias = bias_ref[kv]
            st = st + bias
            sink = sink_ref[kv]
            m = jnp.maximum(jnp.max(st, axis=0, keepdims=True), sink)
            p = jnp.exp(st - m)
            den = jnp.sum(p, axis=0, keepdims=True) + jnp.exp(sink - m)
            ot = jnp.dot(vt_buf[kv * HEAD_DIM:(kv + 1) * HEAD_DIM, r0:r0 + KEY_WIN], p.astype(BF16),
                         preferred_element_type=F32) / den
            for gp in range(GROUP // 2):
                blk = jnp.concatenate([ot[:, (2 * gp) * pw:(2 * gp + 1) * pw],
                                       ot[:, (2 * gp + 1) * pw:(2 * gp + 2) * pw]], axis=0)
                c0 = (kv * GROUP + 2 * gp) * HEAD_DIM
                attn_buf[r0:r0 + nq, c0:c0 + 2 * HEAD_DIM] = blk.T[:nq]

    cu = c_ref[0].astype(F32) * u_ref[0].astype(F32)
    u_all = jnp.concatenate([u_hist, cu], axis=0)
    cw = cw_ref[...]
    conv = cw[0:1] * u_all[6:6 + tq] + cw[1:2] * u_all[7:7 + tq] + cw[2:3] * u_all[8:8 + tq]
    cnew_ref[0] = u_all[tq + 6:tq + 8]

    merged = (jax.nn.sigmoid(ga_ref[0].astype(F32)) * attn_buf[...]
              + jax.nn.sigmoid(gc_ref[0].astype(F32)) * (bg_ref[0].astype(F32) * conv))
    merged_ref[0] = merged.astype(BF16)


def _mixer(proj, knw, qsc, bias_tab, sink_tab, conv_w, state, tq, nq):
    b, t, _ = proj.shape
    d = D_MODEL
    stateful = state is not None
    key_rows = max(WINDOW + tq, KEY_WIN)
    r = np.arange(KV_COLS) // HEAD_DIM
    bd = jnp.asarray((r[:, None] == r[None, :]).astype(np.float32), BF16)
    wide = lambda j: pl.BlockSpec((1, tq, d), lambda i, s, j=j: (i, s, j))
    kvspec = lambda j: pl.BlockSpec((1, tq, KV_COLS), lambda i, s, j=j: (i, s, j))
    const2 = lambda shp: pl.BlockSpec(shp, lambda i, s: (0, 0))
    const3 = lambda shp: pl.BlockSpec(shp, lambda i, s: (0, 0, 0))
    per_b = lambda shp: pl.BlockSpec(shp, lambda i, s: (i, 0, 0))
    kvblk = 6 * d // KV_COLS
    if stateful:
        hist_specs = [per_b((1, WINDOW, KV_COLS)), per_b((1, WINDOW, KV_COLS)), per_b((1, 2, d)), per_b((1, 2, d))]
        hist_args = [state[0], state[1], state[2], state[2]]
    else:
        kw_ = tq // WINDOW
        prev_kv = lambda j: pl.BlockSpec((1, WINDOW, KV_COLS),
                                         lambda i, s, j=j: (i, jnp.maximum(s * kw_ - 1, 0), j))
        prev8 = lambda j: pl.BlockSpec((1, 8, d), lambda i, s, j=j: (i, jnp.maximum(s * (tq // 8) - 1, 0), j))
        hist_specs = [prev_kv(kvblk), prev_kv(kvblk + 1), prev8(2), prev8(3)]
        hist_args = [proj, proj, proj, proj]
    in_specs = [wide(0), kvspec(kvblk), kvspec(kvblk + 1), wide(1), wide(2), wide(3), wide(4), wide(5),
                const2((1, KV_COLS)), const2((1, KV_COLS)), const2((KV_COLS, KV_COLS)),
                const3(bias_tab.shape), const3(sink_tab.shape), const2((3, d))] + hist_specs
    out_shape = (jax.ShapeDtypeStruct((b, t, d), BF16),
                 jax.ShapeDtypeStruct((b, WINDOW, KV_COLS), F32),
                 jax.ShapeDtypeStruct((b, WINDOW, KV_COLS), F32),
                 jax.ShapeDtypeStruct((b, 2, d), F32))
    out_specs = (pl.BlockSpec((1, tq, d), lambda i, s: (i, s, 0)),
                 per_b((1, WINDOW, KV_COLS)), per_b((1, WINDOW, KV_COLS)), per_b((1, 2, d)))
    return pl.pallas_call(
        functools.partial(_mixer_body, tq=tq, nq=nq, stateful=stateful),
        out_shape=out_shape,
        grid=(b, t // tq),
        in_specs=in_specs,
        out_specs=out_specs,
        scratch_shapes=[pltpu.VMEM((N_KV, key_rows, HEAD_DIM), BF16),
                        pltpu.VMEM((KV_COLS, key_rows), BF16),
                        pltpu.VMEM((tq, d), F32)],
        compiler_params=_cparams(("arbitrary", "arbitrary")),
        name="mixer_state" if stateful else "mixer",
    )(proj, proj, proj, proj, proj, proj, proj, proj, knw, qsc, bd, bias_tab, sink_tab, conv_w, *hist_args)


def _route(logits):
    lane = lax.broadcasted_iota(I32, logits.shape, 1)
    neg = -jnp.inf
    big = jnp.int32(1 << 20)
    gl = jnp.where(lane < N_GROUPS, logits, neg)
    gmax = jnp.max(gl, axis=-1, keepdims=True)
    g_idx = jnp.min(jnp.where(gl == gmax, lane, big), axis=-1, keepdims=True)
    g_w = 1.0 / jnp.sum(jnp.exp(gl - gmax), axis=-1, keepdims=True)
    lo = N_GROUPS + g_idx * EPG
    el = jnp.where((lane >= lo) & (lane < lo + EPG), logits, neg)
    m1 = jnp.max(el, axis=-1, keepdims=True)
    i1 = jnp.min(jnp.where(el == m1, lane, big), axis=-1, keepdims=True)
    el2 = jnp.where(lane == i1, neg, el)
    m2 = jnp.max(el2, axis=-1, keepdims=True)
    i2 = jnp.min(jnp.where(el2 == m2, lane, big), axis=-1, keepdims=True)
    r = jnp.exp(m2 - m1)
    w1 = 1.0 / (1.0 + r)
    w2 = r / (1.0 + r)
    return i1 - N_GROUPS, i2 - N_GROUPS, g_w * w1, g_w * w2


def _outproj_kernel(m_ref, x_ref, mod_ref, wo_ref, nw_ref, wr_ref, br_ref,
                    x1_ref, h2_ref, eid_ref, ew_ref):
    mod = mod_ref[0]
    mix = jnp.dot(m_ref[0], wo_ref[...], preferred_element_type=F32)
    x1 = x_ref[0] + mod[2:3] * mix
    x1_ref[0] = x1
    h = x1 * lax.rsqrt(jnp.mean(x1 * x1, axis=-1, keepdims=True) + RMS_EPS) * nw_ref[...]
    h = h * (1.0 + mod[4:5]) + mod[3:4]
    h2_ref[...] = _pack_pairs(h)
    tm = h.shape[0]
    h_hi, h_lo = _split_bf16(h)
    prod = jnp.dot(jnp.concatenate([h_hi, h_lo], axis=0), wr_ref[...], preferred_element_type=F32)
    logits = prod[:tm, :LANES] + (prod[:tm, LANES:] + prod[tm:, :LANES]) + br_ref[...]
    e1, e2, w1, w2 = _route(logits)
    lane8 = lax.broadcasted_iota(I32, (h.shape[0], 8), 1)
    eid_ref[...] = jnp.where(lane8 == 0, e1, jnp.where(lane8 == 1, e2, 0))
    ew_ref[...] = jnp.where(lane8 == 0, w1, jnp.where(lane8 == 1, w2, 0.0))


def _outproj(merged, x, mod, w_out_b, nw, w_r, b_r, tm):
    b, t, d = x.shape
    nt = t // tm
    flat = lambda i, j: (i * nt + j, 0)
    return pl.pallas_call(
        _outproj_kernel,
        out_shape=(jax.ShapeDtypeStruct((b, t, d), F32),
                   jax.ShapeDtypeStruct((b * t, HALF), I32),
                   jax.ShapeDtypeStruct((b * t, 8), I32),
                   jax.ShapeDtypeStruct((b * t, 8), F32)),
        grid=(b, nt),
        in_specs=[pl.BlockSpec((1, tm, d), lambda i, j: (i, j, 0)),
                  pl.BlockSpec((1, tm, d), lambda i, j: (i, j, 0)),
                  pl.BlockSpec((1, 6, d), lambda i, j: (i, 0, 0)),
                  pl.BlockSpec((d, d), lambda i, j: (0, 0)),
                  pl.BlockSpec((1, d), lambda i, j: (0, 0)),
                  pl.BlockSpec((d, 2 * LANES), lambda i, j: (0, 0)),
                  pl.BlockSpec((1, LANES), lambda i, j: (0, 0))],
        out_specs=(pl.BlockSpec((1, tm, d), lambda i, j: (i, j, 0)),
                   pl.BlockSpec((tm, HALF), flat),
                   pl.BlockSpec((tm, 8), flat),
                   pl.BlockSpec((tm, 8), flat)),
        compiler_params=_cparams(("arbitrary", "arbitrary")),
        name="outproj",
    )(merged, x, mod, w_out_b, nw, w_r, b_r)


def _rank_kernel(eid_ref, tri_ref, upper_ref, dest_ref, tot_ref, cnt, starts, *, block):
    ph = pl.program_id(0)
    i = pl.program_id(1)
    tm = eid_ref.shape[0]
    lane = lax.broadcasted_iota(I32, (tm, LANES), 1)
    e0 = eid_ref[:, 0:1]
    e1 = eid_ref[:, 1:2]
    hot0 = lane == e0
    hot1 = lane == e1
    onehot = jnp.where(hot0 | hot1, 1.0, 0.0)
    colsum = jnp.sum(onehot, axis=0, keepdims=True)

    @pl.when((ph == 0) & (i == 0))
    def _():
        cnt[...] = jnp.zeros_like(cnt)

    @pl.when(ph == 0)
    def _():
        cnt[0:1] = cnt[0:1] + colsum

    @pl.when((ph == 1) & (i == 0))
    def _():
        tot = cnt[0:1]
        tot_ref[...] = tot.astype(I32)
        nblk = jnp.floor((tot + (block - 1)) * (1.0 / block))
        hi = jnp.floor(nblk * (1.0 / 16.0))
        lo = nblk - hi * 16.0
        up = upper_ref[...]
        excl = (jnp.dot(jnp.broadcast_to(hi, (8, LANES)).astype(BF16), up, preferred_element_type=F32) * 16.0
                + jnp.dot(jnp.broadcast_to(lo, (8, LANES)).astype(BF16), up, preferred_element_type=F32))
        starts[...] = excl * float(block)
        cnt[...] = jnp.zeros_like(cnt)

    @pl.when(ph == 1)
    def _():
        prefix = jnp.dot(tri_ref[...], onehot.astype(BF16), preferred_element_type=F32)
        pos = prefix + (starts[0:1] + cnt[0:1])
        d0 = jnp.sum(jnp.where(hot0, pos, 0.0), axis=-1, keepdims=True).astype(I32)
        d1 = jnp.sum(jnp.where(hot1, pos, 0.0), axis=-1, keepdims=True).astype(I32)
        lane8 = lax.broadcasted_iota(I32, (tm, 8), 1)
        dest_ref[...] = jnp.where(lane8 == 0, d0, jnp.where(lane8 == 1, d1, 0))
        cnt[0:1] = cnt[0:1] + colsum


def _rank(eid, block):
    n = eid.shape[0]
    tm = math.gcd(n, RANK_TILE)
    r = np.arange(tm)
    tri = jnp.asarray((r[:, None] > r[None, :]).astype(np.float32), BF16)
    l = np.arange(LANES)
    upper = jnp.asarray(((l[:, None] < l[None, :]) & (l[None, :] < N_EXPERTS)).astype(np.float32), BF16)
    return pl.pallas_call(
        functools.partial(_rank_kernel, block=block),
        out_shape=(jax.ShapeDtypeStruct((n, 8), I32), jax.ShapeDtypeStruct((1, LANES), I32)),
        grid=(2, n // tm),
        in_specs=[pl.BlockSpec((tm, 8), lambda p, i: (i, 0)),
                  pl.BlockSpec((tm, tm), lambda p, i: (0, 0)),
                  pl.BlockSpec((LANES, LANES), lambda p, i: (0, 0))],
        out_specs=(pl.BlockSpec((tm, 8), lambda p, i: (i * p, 0)),
                   pl.BlockSpec((1, LANES), lambda p, i: (0, 0))),
        scratch_shapes=[pltpu.VMEM((8, LANES), F32), pltpu.VMEM((8, LANES), F32)],
        compiler_params=_cparams(("arbitrary", "arbitrary")),
        name="rank",
    )(eid, tri, upper)


def _expert_kernel(be_ref, nv_ref, xs_ref, wg_ref, wu_ref, wd_ref, ys_ref, wg_s, wu_s, wd_s):
    b = pl.program_id(0)

    @pl.when(b < nv_ref[0])
    def _():
        prev = be_ref[jnp.maximum(b - 1, 0)]
        fresh = (b == 0) | (be_ref[b] != prev)

        @pl.when(fresh)
        def _():
            wg_s[...] = wg_ref[0].astype(BF16)
            wu_s[...] = wu_ref[0].astype(BF16)
            wd_s[...] = wd_ref[0].astype(BF16)

        a, c = _unpack_pairs(xs_ref[...])
        x = jnp.concatenate([a.astype(BF16), c.astype(BF16)], axis=1)
        g = jnp.dot(x, wg_s[...], preferred_element_type=F32)
        u = jnp.dot(x, wu_s[...], preferred_element_type=F32)
        hmid = (g * jax.nn.sigmoid(g) * u).astype(BF16)
        ys_ref[...] = _pack_pairs(jnp.dot(hmid, wd_s[...], preferred_element_type=F32))


def _experts(xs, block_e, n_valid, w_gate, w_up, w_down, block):
    n_rows = xs.shape[0]
    nb = n_rows // block
    rowblk = lambda b, be, nv: (jnp.minimum(b, nv[0] - 1), 0)
    wblk = lambda b, be, nv: (be[jnp.minimum(b, nv[0] - 1)], 0, 0)
    grid_spec = pltpu.PrefetchScalarGridSpec(
        num_scalar_prefetch=2,
        grid=(nb,),
        in_specs=[pl.BlockSpec((block, HALF), rowblk),
                  pl.BlockSpec((1, D_MODEL, D_EXPERT), wblk),
                  pl.BlockSpec((1, D_MODEL, D_EXPERT), wblk),
                  pl.BlockSpec((1, D_EXPERT, D_MODEL), wblk)],
        out_specs=pl.BlockSpec((block, HALF), rowblk),
        scratch_shapes=[pltpu.VMEM((D_MODEL, D_EXPERT), BF16),
                        pltpu.VMEM((D_MODEL, D_EXPERT), BF16),
                        pltpu.VMEM((D_EXPERT, D_MODEL), BF16)])
    return pl.pallas_call(
        _expert_kernel,
        out_shape=jax.ShapeDtypeStruct((n_rows, HALF), I32),
        grid_spec=grid_spec,
        compiler_params=_cparams(("arbitrary",)),
        name="experts",
    )(block_e, n_valid, xs, w_gate, w_up, w_down)


def _final_kernel(x1_ref, y0_ref, y1_ref, ew_ref, mod_ref, o_ref):
    a0, b0 = _unpack_pairs(y0_ref[...])
    a1, b1 = _unpack_pairs(y1_ref[...])
    w0 = ew_ref[:, 0:1]
    w1 = ew_ref[:, 1:2]
    moe = jnp.concatenate([w0 * a0 + w1 * a1, w0 * b0 + w1 * b1], axis=1)
    o_ref[0] = x1_ref[0] + mod_ref[0][5:6] * moe


def _final(x1, y0, y1, ew, mod, tm):
    b, t, d = x1.shape
    nt = t // tm
    flat = lambda i, j: (i * nt + j, 0)
    return pl.pallas_call(
        _final_kernel,
        out_shape=jax.ShapeDtypeStruct((b, t, d), F32),
        grid=(b, nt),
        in_specs=[pl.BlockSpec((1, tm, d), lambda i, j: (i, j, 0)),
                  pl.BlockSpec((tm, HALF), flat),
                  pl.BlockSpec((tm, HALF), flat),
                  pl.BlockSpec((tm, 8), flat),
                  pl.BlockSpec((1, 6, d), lambda i, j: (i, 0, 0))],
        out_specs=pl.BlockSpec((1, tm, d), lambda i, j: (i, j, 0)),
        compiler_params=_cparams(("arbitrary", "arbitrary")),
        name="final",
    )(x1, y0, y1, ew, mod)


def _sc_window(rows_per_worker):
    for w in range(SC_MAX_WINDOW, 7, -8):
        if rows_per_worker % w == 0:
            return w
    raise ValueError(f"no SparseCore window divides {rows_per_worker} rows per worker")


def _sc_split(idx):
    n = idx.shape[0]
    per = n // SC_WORKERS
    assert per * SC_WORKERS == n
    win = _sc_window(per)
    return idx.reshape(SC_WORKERS, per // win, win), per // win, win


def _sc_worker_id():
    return lax.axis_index("s") * SC_CORES + lax.axis_index("c")


def _dispatch_rows(h2_groups, dest_groups, n_rows):
    splits = [(_sc_split(d[:, 0]), _sc_split(d[:, 1])) for d in dest_groups]
    ng = len(h2_groups)
    scratch = []
    for (_, _, win), _ in splits:
        scratch += [pltpu.VMEM((win,), I32), pltpu.VMEM((win,), I32), pltpu.VMEM((win, HALF), I32)]

    @functools.partial(
        pl.kernel,
        mesh=plsc.VectorSubcoreMesh(core_axis_name="c", subcore_axis_name="s"),
        out_type=jax.ShapeDtypeStruct((n_rows, HALF), I32),
        scratch_types=scratch,
        name="sc_dispatch",
    )
    def k(*refs):
        x_refs, idx_refs, o_hbm, bufs = refs[:ng], refs[ng:3 * ng], refs[3 * ng], refs[3 * ng + 1:]
        wid = _sc_worker_id()
        for g in range(ng):
            (_, nwin, win), _ = splits[g]
            x_hbm, d0_hbm, d1_hbm = x_refs[g], idx_refs[2 * g], idx_refs[2 * g + 1]
            i0_v, i1_v, rows_v = bufs[3 * g:3 * g + 3]

            @pl.loop(0, nwin)
            def _(j, nwin=nwin, win=win, x_hbm=x_hbm, d0_hbm=d0_hbm, d1_hbm=d1_hbm,
                  i0_v=i0_v, i1_v=i1_v, rows_v=rows_v):
                base = pl.multiple_of((wid * nwin + j) * win, 8)
                pltpu.sync_copy(d0_hbm.at[wid, j], i0_v)
                pltpu.sync_copy(d1_hbm.at[wid, j], i1_v)
                pltpu.sync_copy(x_hbm.at[pl.ds(base, win)], rows_v)
                pltpu.sync_copy(rows_v, o_hbm.at[i0_v])
                pltpu.sync_copy(rows_v, o_hbm.at[i1_v])

    idx_args = []
    for (s0, s1) in splits:
        idx_args += [s0[0], s1[0]]
    return k(*h2_groups, *idx_args)


def _collect_rows(ys, dest_groups):
    splits = [(_sc_split(d[:, 0]), _sc_split(d[:, 1])) for d in dest_groups]
    ng = len(dest_groups)
    outs, scratch = [], []
    for d, ((_, _, win), _) in zip(dest_groups, splits):
        o = jax.ShapeDtypeStruct((d.shape[0], HALF), I32)
        outs += [o, o]
        scratch += [pltpu.VMEM((win,), I32), pltpu.VMEM((win, HALF), I32)]

    @functools.partial(
        pl.kernel,
        mesh=plsc.VectorSubcoreMesh(core_axis_name="c", subcore_axis_name="s"),
        out_type=tuple(outs),
        scratch_types=scratch,
        name="sc_collect",
    )
    def k(*refs):
        ys_hbm, idx_refs, out_refs, bufs = refs[0], refs[1:1 + 2 * ng], refs[1 + 2 * ng:1 + 4 * ng], refs[1 + 4 * ng:]
        wid = _sc_worker_id()
        for g in range(ng):
            (_, nwin, win), _ = splits[g]
            i_v, rows_v = bufs[2 * g:2 * g + 2]
            for kk in range(2):
                d_hbm, y_hbm = idx_refs[2 * g + kk], out_refs[2 * g + kk]

                @pl.loop(0, nwin)
                def _(j, nwin=nwin, win=win, d_hbm=d_hbm, y_hbm=y_hbm, i_v=i_v, rows_v=rows_v):
                    base = pl.multiple_of((wid * nwin + j) * win, 8)
                    pltpu.sync_copy(d_hbm.at[wid, j], i_v)
                    pltpu.sync_copy(ys_hbm.at[i_v], rows_v)
                    pltpu.sync_copy(rows_v, y_hbm.at[pl.ds(base, win)])

    idx_args = []
    for (s0, s1) in splits:
        idx_args += [s0[0], s1[0]]
    res = k(ys, *idx_args)
    return [(res[2 * g], res[2 * g + 1]) for g in range(ng)]


def _t5_bucket(rel):
    half = N_BUCKETS // 2
    max_exact = half // 2
    n = jnp.abs(rel)
    far = max_exact + (jnp.log(jnp.maximum(n, 1).astype(F32) / max_exact)
                       / math.log(MAX_DISTANCE / max_exact) * (half - max_exact)).astype(I32)
    far = jnp.minimum(far, half - 1)
    return jnp.where(rel > 0, half, 0) + jnp.where(n < max_exact, n, far)


def _bias_table(rel_bias, cq, nq, no_history):
    nk = WINDOW + cq
    j = jnp.arange(KEY_WIN)[:, None]
    c = jnp.arange(UNIT_Q)[None, :]
    jj = j - (c // cq) * cq
    valid = (jj >= 0) & (jj < nk) & (c < nq)
    if no_history:
        valid = valid & (j >= WINDOW)
    rel = jj - WINDOW - (c % cq)
    onehot = (_t5_bucket(rel)[:, :, None] == jnp.arange(N_BUCKETS)).astype(F32)
    bias = jnp.einsum("jcb,bh->jch", onehot, rel_bias.astype(F32), precision=lax.Precision.HIGHEST)
    bias = jnp.where(valid[:, :, None], bias, -jnp.inf)
    bias = jnp.transpose(bias.reshape(KEY_WIN, UNIT_Q, N_KV, GROUP), (2, 0, 3, 1))
    return bias.reshape(N_KV, KEY_WIN, GROUP * UNIT_Q)


def _sink_table(sinks):
    s = sinks.astype(F32).reshape(N_KV, 1, GROUP, 1)
    return jnp.broadcast_to(s, (N_KV, 1, GROUP, UNIT_Q)).reshape(N_KV, 1, GROUP * UNIT_Q)


def kernel(x_prompt, x_sample, state_attn_k, state_attn_v, state_conv, c_prompt, c_sample,
           rel_bias, w_ada, b_ada, norm1_w, w_in, q_norm_w, k_norm_w, attn_sinks, conv_w,
           w_out, norm2_w, w_router_group, b_router_group, w_router_expert, b_router_expert,
           w_gate, w_up, w_down):
    depth = w_ada.shape[0]
    assert depth == 1
    bp, tp, d = x_prompt.shape
    bs, ts, _ = x_sample.shape
    n_p, n_s = bp * tp, bs * ts
    n_tok = n_p + n_s
    l = 0

    wi = w_in[l]
    qw, kw, vw, rest = wi[:, :d], wi[:, d:d + KV_COLS], wi[:, d + KV_COLS:d + 2 * KV_COLS], wi[:, d + 2 * KV_COLS:]
    w_in_b = jnp.concatenate([qw, rest, kw, vw], axis=1).astype(BF16)
    w_out_b = w_out[l].astype(BF16)
    w_r = jnp.concatenate([w_router_group[l],
                           jnp.transpose(w_router_expert[l], (1, 0, 2)).reshape(d, N_EXPERTS),
                           jnp.zeros((d, LANES - N_GROUPS - N_EXPERTS), F32)], axis=1)
    w_r_hi = lax.reduce_precision(w_r, exponent_bits=8, mantissa_bits=7)
    w_r = jnp.concatenate([w_r_hi.astype(BF16), (w_r - w_r_hi).astype(BF16)], axis=1)
    b_r = jnp.concatenate([b_router_group[l], b_router_expert[l].reshape(-1),
                           jnp.zeros((LANES - N_GROUPS - N_EXPERTS,), F32)]).reshape(1, LANES)
    knw = jnp.tile(k_norm_w[l], N_KV).reshape(1, KV_COLS)
    qsc = jnp.tile(q_norm_w[l] * (HEAD_DIM ** -0.5), N_KV).reshape(1, KV_COLS)
    n1w = norm1_w[l].reshape(1, d)
    n2w = norm2_w[l].reshape(1, d)

    mod = _ada(jnp.concatenate([c_prompt, c_sample], axis=0), w_ada[l], b_ada[l]).reshape(bp + bs, 6, d)
    mod_p, mod_s = mod[:bp], mod[bp:]

    proj_p = _inproj(x_prompt, mod_p, n1w, w_in_b, ROW_TILE)
    proj_s = _inproj(x_sample, mod_s, n1w, w_in_b, ts)
    sink_tab = _sink_table(attn_sinks[l])
    bias_p = jnp.concatenate([_bias_table(rel_bias, CHUNK, UNIT_Q, False),
                              _bias_table(rel_bias, CHUNK, UNIT_Q, True)], axis=0)
    merged_p, k_p, v_p, c_p = _mixer(proj_p, knw, qsc, bias_p, sink_tab, conv_w[l], None, MIX_TILE, UNIT_Q)
    state = (state_attn_k[l].reshape(bs, WINDOW, KV_COLS), state_attn_v[l].reshape(bs, WINDOW, KV_COLS),
             state_conv[l])
    merged_s, k_s, v_s, c_s = _mixer(proj_s, knw, qsc, _bias_table(rel_bias, ts, ts, False), sink_tab, conv_w[l],
                                     state, ts, ts)

    x1_p, h2_p, eid_p, ew_p = _outproj(merged_p, x_prompt, mod_p, w_out_b, n2w, w_r, b_r, ROW_TILE)
    x1_s, h2_s, eid_s, ew_s = _outproj(merged_s, x_sample, mod_s, w_out_b, n2w, w_r, b_r, ts)

    dest, totals = _rank(jnp.concatenate([eid_p, eid_s], axis=0), EXPERT_BLOCK)
    dests = [dest[:n_p], dest[n_p:]]
    nblk = (totals[0, :N_EXPERTS] + EXPERT_BLOCK - 1) // EXPERT_BLOCK
    blk_end = jnp.cumsum(nblk)
    nb_max = -(-2 * n_tok // EXPERT_BLOCK) + N_EXPERTS
    block_e = jnp.minimum(jnp.sum(blk_end[None, :] <= jnp.arange(nb_max)[:, None], axis=1), N_EXPERTS - 1).astype(I32)
    n_valid = blk_end[-1:].astype(I32)

    xs = _dispatch_rows([h2_p, h2_s], dests, nb_max * EXPERT_BLOCK)
    ys = _experts(xs, block_e, n_valid, w_gate[l], w_up[l], w_down[l], EXPERT_BLOCK)
    (y0_p, y1_p), (y0_s, y1_s) = _collect_rows(ys, dests)

    y_p = _final(x1_p, y0_p, y1_p, ew_p, mod_p, ROW_TILE)
    y_s = _final(x1_s, y0_s, y1_s, ew_s, mod_s, ts)

    kv_shape = (1, -1, WINDOW, N_KV, HEAD_DIM)
    return (y_p, y_s, k_p.reshape(kv_shape), v_p.reshape(kv_shape), c_p[None],
            k_s.reshape(kv_shape), v_s.reshape(kv_shape), c_s[None])
```

```python
import functools
import math

import numpy as np
import jax
import jax.numpy as jnp
from jax import lax
from jax.experimental import pallas as pl
from jax.experimental.pallas import tpu as pltpu
from jax.experimental.pallas import tpu_sc as plsc

F32 = jnp.float32
BF16 = jnp.bfloat16
I32 = jnp.int32

D_MODEL = 1024
HEAD_DIM = 64
N_HEADS = 16
N_KV = 4
GROUP = 4
CHUNK = 64
WINDOW = 128
N_BUCKETS = 32
MAX_DISTANCE = 128
N_GROUPS = 8
EPG = 8
N_EXPERTS = 64
D_EXPERT = 512
RMS_EPS = 1e-6
KV_COLS = N_KV * HEAD_DIM
IN_COLS = 6 * D_MODEL + 2 * KV_COLS
HALF = D_MODEL // 2
LANES = 128

VMEM_LIMIT = 56 * 1024 * 1024
INPROJ_TN = 512
ROW_TILE = 512
MIX_TILE = 512
UNIT_Q = 2 * CHUNK
KEY_WIN = WINDOW + UNIT_Q
EXPERT_BLOCK = 256
RANK_TILE = 512
SC_CORES = 2
SC_SUBCORES = 16
SC_WORKERS = SC_CORES * SC_SUBCORES
SC_MAX_WINDOW = 128


def _cparams(sem):
    return pltpu.CompilerParams(dimension_semantics=sem, vmem_limit_bytes=VMEM_LIMIT)


def _split_bf16(a):
    hi = a.astype(BF16)
    lo = (a - hi.astype(F32)).astype(BF16)
    return hi, lo


def _dot3(a, b):
    ah, al = _split_bf16(a)
    bh, bl = _split_bf16(b)
    d = functools.partial(jnp.dot, preferred_element_type=F32)
    return d(ah, bh) + (d(ah, bl) + d(al, bh))


def _pack_pairs(y):
    a = lax.bitcast_convert_type(y[:, :HALF].astype(BF16).astype(F32), I32)
    b = lax.bitcast_convert_type(y[:, HALF:].astype(BF16).astype(F32), I32)
    return a | lax.shift_right_logical(b, jnp.int32(16))


def _unpack_pairs(w):
    a = lax.bitcast_convert_type(w & jnp.int32(-65536), F32)
    b = lax.bitcast_convert_type(lax.shift_left(w, jnp.int32(16)), F32)
    return a, b


def _ada_kernel(c_ref, w_ref, b_ref, o_ref):
    c = c_ref[...]
    s = c * jax.nn.sigmoid(c)
    o_ref[...] = _dot3(s, w_ref[...]) + b_ref[...]


def _ada(c_all, w_ada, b_ada):
    r, d = c_all.shape
    n = w_ada.shape[1]
    tn = 1024
    return pl.pallas_call(
        _ada_kernel,
        out_shape=jax.ShapeDtypeStruct((r, n), F32),
        grid=(n // tn,),
        in_specs=[pl.BlockSpec((r, d), lambda j: (0, 0)),
                  pl.BlockSpec((d, tn), lambda j: (0, j)),
                  pl.BlockSpec((1, tn), lambda j: (0, j))],
        out_specs=pl.BlockSpec((r, tn), lambda j: (0, j)),
        compiler_params=_cparams(("arbitrary",)),
        name="ada",
    )(c_all, w_ada, b_ada.reshape(1, n))


def _inproj_kernel(x_ref, mod_ref, nw_ref, w_ref, o_ref):
    x = x_ref[0]
    mod = mod_ref[0]
    h = x * lax.rsqrt(jnp.mean(x * x, axis=-1, keepdims=True) + RMS_EPS) * nw_ref[...]
    h = h * (1.0 + mod[1:2]) + mod[0:1]
    hb = h.astype(BF16)
    for j in range(IN_COLS // INPROJ_TN):
        sl = slice(j * INPROJ_TN, (j + 1) * INPROJ_TN)
        o_ref[0, :, sl] = jnp.dot(hb, w_ref[:, sl], preferred_element_type=F32).astype(BF16)


def _inproj(x, mod, nw, w_in_b, tm):
    b, t, d = x.shape
    return pl.pallas_call(
        _inproj_kernel,
        out_shape=jax.ShapeDtypeStruct((b, t, IN_COLS), BF16),
        grid=(b, t // tm),
        in_specs=[pl.BlockSpec((1, tm, d), lambda i, j: (i, j, 0)),
                  pl.BlockSpec((1, 6, d), lambda i, j: (i, 0, 0)),
                  pl.BlockSpec((1, d), lambda i, j: (0, 0)),
                  pl.BlockSpec((d, IN_COLS), lambda i, j: (0, 0), pipeline_mode=pl.Buffered(1))],
        out_specs=pl.BlockSpec((1, tm, IN_COLS), lambda i, j: (i, j, 0)),
        compiler_params=_cparams(("arbitrary", "arbitrary")),
        name="inproj",
    )(x, mod, nw, w_in_b)


def _head_inv_rms(xf, bd):
    hi, lo = _split_bf16(xf * xf)
    ssq = jnp.dot(hi, bd, preferred_element_type=F32) + jnp.dot(lo, bd, preferred_element_type=F32)
    return lax.rsqrt(ssq * (1.0 / HEAD_DIM) + RMS_EPS)


def _mixer_body(q_ref, k_ref, v_ref, bg_ref, c_ref, u_ref, ga_ref, gc_ref,
                knw_ref, qsc_ref, bd_ref, bias_ref, sink_ref, cw_ref,
                kpast_ref, vpast_ref, cpast_ref, upast_ref,
                merged_ref, knew_ref, vnew_ref, cnew_ref,
                kq_buf, vt_buf, attn_buf, *, tq, nq, stateful):
    hb = WINDOW
    pw = UNIT_Q
    t = pl.program_id(1)
    bd = bd_ref[...]

    k = k_ref[0].astype(F32)
    kn = k * _head_inv_rms(k, bd) * knw_ref[...]
    kq = (kn * qsc_ref[...]).astype(BF16)
    vb = v_ref[0]
    vt = vb.astype(F32).T.astype(BF16)

    if stateful:
        kp = kpast_ref[0]
        vp = vpast_ref[0]
        for kv in range(N_KV):
            kq_buf[kv, hb + tq:] = jnp.zeros((KEY_WIN - hb - tq, HEAD_DIM), BF16)
        vt_buf[:, hb + tq:] = jnp.zeros((KV_COLS, KEY_WIN - hb - tq), BF16)
        u_hist = jnp.concatenate([jnp.zeros((6, D_MODEL), F32), cpast_ref[0]], axis=0)
        knew_ref[0] = jnp.concatenate([kp[tq:], kn], axis=0)
        vnew_ref[0] = jnp.concatenate([vp[tq:], vb.astype(F32)], axis=0)
    else:
        kraw = kpast_ref[0].astype(F32)
        kp = kraw * _head_inv_rms(kraw, bd) * knw_ref[...]
        vp = vpast_ref[0].astype(F32)
        u_hist = jnp.where(t == 0, 0.0, cpast_ref[0].astype(F32) * upast_ref[0].astype(F32))
        knew_ref[0] = kn[tq - hb:]
        vnew_ref[0] = vb[tq - hb:].astype(F32)
    _fill_keys(kq_buf, vt_buf, kp, vp, kq, vt, qsc_ref, tq)
    q = q_ref[0]
    _attention_units(lambda kv: q[:, kv * KV_COLS:(kv + 1) * KV_COLS].astype(F32), kq_buf, vt_buf,
                     bias_ref, sink_ref, bd, attn_buf, tq, nq, None if stateful else t)
    cu = c_ref[0].astype(F32) * u_ref[0].astype(F32)
    conv, u_all = _short_conv(u_hist, cu, cw_ref[...], tq)
    cnew_ref[0] = u_all[tq + 6:tq + 8]
    merged = (jax.nn.sigmoid(ga_ref[0].astype(F32)) * attn_buf[...]
              + jax.nn.sigmoid(gc_ref[0].astype(F32)) * (bg_ref[0].astype(F32) * conv))
    merged_ref[0] = merged.astype(BF16)


def _fill_keys(kq_buf, vt_buf, kp, vp, kq, vt, qsc_ref, tq):
    hb = WINDOW
    kqp = (kp * qsc_ref[...]).astype(BF16)
    for kv in range(N_KV):
        kq_buf[kv, 0:hb] = kqp[:, kv * HEAD_DIM:(kv + 1) * HEAD_DIM]
        kq_buf[kv, hb:hb + tq] = kq[:, kv * HEAD_DIM:(kv + 1) * HEAD_DIM]
    vt_buf[:, 0:hb] = vp.T.astype(BF16)
    vt_buf[:, hb:hb + tq] = vt


def _attention_units(q_group, kq_buf, vt_buf, bias_ref, sink_ref, bd, attn_buf, tq, nq, t_first):
    pw = UNIT_Q
    for kv in range(N_KV):
        qf = q_group(kv)
        qn = (qf * _head_inv_rms(qf, bd)).astype(BF16)
        for u in range(tq // nq):
            r0 = u * nq
            parts = [qn[r0:r0 + nq, g * HEAD_DIM:(g + 1) * HEAD_DIM] for g in range(GROUP)]
            if nq < pw:
                zpad = jnp.zeros((pw - nq, HEAD_DIM), BF16)
                parts = [x for p_ in parts for x in (p_, zpad)]
            qs = jnp.concatenate(parts, axis=0)
            kw = kq_buf[kv, r0:r0 + KEY_WIN]
            st = lax.dot_general(kw, qs, (((1,), (1,)), ((), ())), preferred_element_type=F32)
            if t_first is not None and u == 0:
                bias = jnp.where(t_first == 0, bias_ref[kv + N_KV], bias_ref[kv])
            else:
                bias = bias_ref[kv]
            st = st + bias
            sink = sink_ref[kv]
            m = jnp.maximum(jnp.max(st, axis=0, keepdims=True), sink)
            p = jnp.exp(st - m)
            den = jnp.sum(p, axis=0, keepdims=True) + jnp.exp(sink - m)
            ot = jnp.dot(vt_buf[kv * HEAD_DIM:(kv + 1) * HEAD_DIM, r0:r0 + KEY_WIN], p.astype(BF16),
                         preferred_element_type=F32) / den
            for gp in range(GROUP // 2):
                blk = jnp.concatenate([ot[:, (2 * gp) * pw:(2 * gp + 1) * pw],
                                       ot[:, (2 * gp + 1) * pw:(2 * gp + 2) * pw]], axis=0)
                c0 = (kv * GROUP + 2 * gp) * HEAD_DIM
                attn_buf[r0:r0 + nq, c0:c0 + 2 * HEAD_DIM] = blk.T[:nq]


def _short_conv(u_hist, cu, cw, tq):
    u_all = jnp.concatenate([u_hist, cu], axis=0)
    conv = cw[0:1] * u_all[6:6 + tq] + cw[1:2] * u_all[7:7 + tq] + cw[2:3] * u_all[8:8 + tq]
    return conv, u_all


def _front_body(x_ref, xp_ref, mod_ref, nw_ref, w_ref, knw_ref, qsc_ref, bd_ref, bias_ref, sink_ref, cw_ref,
                merged_ref, knew_ref, vnew_ref, cnew_ref,
                kq_buf, vt_buf, u_buf, attn_buf, *, tq, nq):
    hb = WINDOW
    d = D_MODEL
    t = pl.program_id(1)
    bd = bd_ref[...]
    mod = mod_ref[0]

    @pl.when(t == 0)
    def _():
        u_buf[0] = jnp.zeros((8, d), F32)

    def modnorm(x):
        h = x * lax.rsqrt(jnp.mean(x * x, axis=-1, keepdims=True) + RMS_EPS) * nw_ref[...]
        return (h * (1.0 + mod[1:2]) + mod[0:1]).astype(BF16)

    def proj(hrows, c0, width):
        return jnp.dot(hrows, w_ref[:, c0:c0 + width], preferred_element_type=F32)

    hbf = modnorm(x_ref[0])
    kv_t = proj(hbf, 6 * d, 2 * KV_COLS)
    kv_p = proj(modnorm(xp_ref[0]), 6 * d, 2 * KV_COLS)
    k, v = kv_t[:, :KV_COLS], kv_t[:, KV_COLS:]
    kn = k * _head_inv_rms(k, bd) * knw_ref[...]
    kq = (kn * qsc_ref[...]).astype(BF16)
    kraw = kv_p[:, :KV_COLS]
    kp = kraw * _head_inv_rms(kraw, bd) * knw_ref[...]
    knew_ref[0] = kn[tq - hb:]
    vnew_ref[0] = v[tq - hb:]
    _fill_keys(kq_buf, vt_buf, kp, kv_p[:, KV_COLS:], kq, v.T.astype(BF16), qsc_ref, tq)

    _attention_units(lambda kv: proj(hbf, kv * KV_COLS, KV_COLS), kq_buf, vt_buf,
                     bias_ref, sink_ref, bd, attn_buf, tq, nq, t)

    cu = proj(hbf, 2 * d, d) * proj(hbf, 3 * d, d)
    conv, u_all = _short_conv(u_buf[t % 2], cu, cw_ref[...], tq)
    cnew_ref[0] = u_all[tq + 6:tq + 8]
    u_buf[(t + 1) % 2] = u_all[tq:tq + 8]
    merged = (jax.nn.sigmoid(proj(hbf, 4 * d, d)) * attn_buf[...]
              + jax.nn.sigmoid(proj(hbf, 5 * d, d)) * (proj(hbf, d, d) * conv))
    merged_ref[0] = merged.astype(BF16)


def _front(x, mod, nw, w_in_b, knw, qsc, bias_tab, sink_tab, conv_w, tq, nq):
    b, t, d = x.shape
    r = np.arange(KV_COLS) // HEAD_DIM
    bd = jnp.asarray((r[:, None] == r[None, :]).astype(np.float32), BF16)
    const2 = lambda shp: pl.BlockSpec(shp, lambda i, s: (0, 0))
    const3 = lambda shp: pl.BlockSpec(shp, lambda i, s: (0, 0, 0))
    per_b = lambda shp: pl.BlockSpec(shp, lambda i, s: (i, 0, 0))
    kw_ = tq // WINDOW
    in_specs = [pl.BlockSpec((1, tq, d), lambda i, s: (i, s, 0)),
                pl.BlockSpec((1, WINDOW, d), lambda i, s: (i, jnp.maximum(s * kw_ - 1, 0), 0)),
                per_b((1, 6, d)), const2((1, d)),
                pl.BlockSpec((d, IN_COLS), lambda i, s: (0, 0), pipeline_mode=pl.Buffered(1)),
                const2((1, KV_COLS)), const2((1, KV_COLS)), const2((KV_COLS, KV_COLS)),
                const3(bias_tab.shape), const3(sink_tab.shape), const2((3, d))]
    out_shape = (jax.ShapeDtypeStruct((b, t, d), BF16),
                 jax.ShapeDtypeStruct((b, WINDOW, KV_COLS), F32),
                 jax.ShapeDtypeStruct((b, WINDOW, KV_COLS), F32),
                 jax.ShapeDtypeStruct((b, 2, d), F32))
    out_specs = (pl.BlockSpec((1, tq, d), lambda i, s: (i, s, 0)),
                 per_b((1, WINDOW, KV_COLS)), per_b((1, WINDOW, KV_COLS)), per_b((1, 2, d)))
    return pl.pallas_call(
        functools.partial(_front_body, tq=tq, nq=nq),
        out_shape=out_shape,
        grid=(b, t // tq),
        in_specs=in_specs,
        out_specs=out_specs,
        scratch_shapes=[pltpu.VMEM((N_KV, WINDOW + tq, HEAD_DIM), BF16),
                        pltpu.VMEM((KV_COLS, WINDOW + tq), BF16),
                        pltpu.VMEM((2, 8, d), F32),
                        pltpu.VMEM((tq, d), F32)],
        compiler_params=_cparams(("arbitrary", "arbitrary")),
        name="front",
    )(x, x, mod, nw, w_in_b, knw, qsc, bd, bias_tab, sink_tab, conv_w)


def _mixer(proj, knw, qsc, bias_tab, sink_tab, conv_w, state, tq, nq):
    b, t, _ = proj.shape
    d = D_MODEL
    stateful = state is not None
    key_rows = max(WINDOW + tq, KEY_WIN)
    r = np.arange(KV_COLS) // HEAD_DIM
    bd = jnp.asarray((r[:, None] == r[None, :]).astype(np.float32), BF16)
    wide = lambda j: pl.BlockSpec((1, tq, d), lambda i, s, j=j: (i, s, j))
    kvspec = lambda j: pl.BlockSpec((1, tq, KV_COLS), lambda i, s, j=j: (i, s, j))
    const2 = lambda shp: pl.BlockSpec(shp, lambda i, s: (0, 0))
    const3 = lambda shp: pl.BlockSpec(shp, lambda i, s: (0, 0, 0))
    per_b = lambda shp: pl.BlockSpec(shp, lambda i, s: (i, 0, 0))
    kvblk = 6 * d // KV_COLS
    if stateful:
        hist_specs = [per_b((1, WINDOW, KV_COLS)), per_b((1, WINDOW, KV_COLS)), per_b((1, 2, d)), per_b((1, 2, d))]
        hist_args = [state[0], state[1], state[2], state[2]]
    else:
        kw_ = tq // WINDOW
        prev_kv = lambda j: pl.BlockSpec((1, WINDOW, KV_COLS),
                                         lambda i, s, j=j: (i, jnp.maximum(s * kw_ - 1, 0), j))
        prev8 = lambda j: pl.BlockSpec((1, 8, d), lambda i, s, j=j: (i, jnp.maximum(s * (tq // 8) - 1, 0), j))
        hist_specs = [prev_kv(kvblk), prev_kv(kvblk + 1), prev8(2), prev8(3)]
        hist_args = [proj, proj, proj, proj]
    in_specs = [wide(0), kvspec(kvblk), kvspec(kvblk + 1), wide(1), wide(2), wide(3), wide(4), wide(5),
                const2((1, KV_COLS)), const2((1, KV_COLS)), const2((KV_COLS, KV_COLS)),
                const3(bias_tab.shape), const3(sink_tab.shape), const2((3, d))] + hist_specs
    out_shape = (jax.ShapeDtypeStruct((b, t, d), BF16),
                 jax.ShapeDtypeStruct((b, WINDOW, KV_COLS), F32),
                 jax.ShapeDtypeStruct((b, WINDOW, KV_COLS), F32),
                 jax.ShapeDtypeStruct((b, 2, d), F32))
    out_specs = (pl.BlockSpec((1, tq, d), lambda i, s: (i, s, 0)),
                 per_b((1, WINDOW, KV_COLS)), per_b((1, WINDOW, KV_COLS)), per_b((1, 2, d)))
    return pl.pallas_call(
        functools.partial(_mixer_body, tq=tq, nq=nq, stateful=stateful),
        out_shape=out_shape,
        grid=(b, t // tq),
        in_specs=in_specs,
        out_specs=out_specs,
        scratch_shapes=[pltpu.VMEM((N_KV, key_rows, HEAD_DIM), BF16),
                        pltpu.VMEM((KV_COLS, key_rows), BF16),
                        pltpu.VMEM((tq, d), F32)],
        compiler_params=_cparams(("arbitrary", "arbitrary")),
        name="mixer_state" if stateful else "mixer",
    )(proj, proj, proj, proj, proj, proj, proj, proj, knw, qsc, bd, bias_tab, sink_tab, conv_w, *hist_args)


def _route(logits):
    lane = lax.broadcasted_iota(I32, logits.shape, 1)
    neg = -jnp.inf
    big = jnp.int32(1 << 20)
    gl = jnp.where(lane < N_GROUPS, logits, neg)
    gmax = jnp.max(gl, axis=-1, keepdims=True)
    g_idx = jnp.min(jnp.where(gl == gmax, lane, big), axis=-1, keepdims=True)
    g_w = 1.0 / jnp.sum(jnp.exp(gl - gmax), axis=-1, keepdims=True)
    lo = N_GROUPS + g_idx * EPG
    el = jnp.where((lane >= lo) & (lane < lo + EPG), logits, neg)
    m1 = jnp.max(el, axis=-1, keepdims=True)
    i1 = jnp.min(jnp.where(el == m1, lane, big), axis=-1, keepdims=True)
    el2 = jnp.where(lane == i1, neg, el)
    m2 = jnp.max(el2, axis=-1, keepdims=True)
    i2 = jnp.min(jnp.where(el2 == m2, lane, big), axis=-1, keepdims=True)
    r = jnp.exp(m2 - m1)
    w1 = 1.0 / (1.0 + r)
    w2 = r / (1.0 + r)
    return i1 - N_GROUPS, i2 - N_GROUPS, g_w * w1, g_w * w2


def _outproj_kernel(m_ref, x_ref, mod_ref, wo_ref, nw_ref, wr_ref, br_ref,
                    x1_ref, h2_ref, eid_ref, ew_ref):
    mod = mod_ref[0]
    mix = jnp.dot(m_ref[0], wo_ref[...], preferred_element_type=F32)
    x1 = x_ref[0] + mod[2:3] * mix
    x1_ref[0] = x1
    h = x1 * lax.rsqrt(jnp.mean(x1 * x1, axis=-1, keepdims=True) + RMS_EPS) * nw_ref[...]
    h = h * (1.0 + mod[4:5]) + mod[3:4]
    h2_ref[...] = _pack_pairs(h)
    tm = h.shape[0]
    h_hi, h_lo = _split_bf16(h)
    prod = jnp.dot(jnp.concatenate([h_hi, h_lo], axis=0), wr_ref[...], preferred_element_type=F32)
    logits = prod[:tm, :LANES] + (prod[:tm, LANES:] + prod[tm:, :LANES]) + br_ref[...]
    e1, e2, w1, w2 = _route(logits)
    lane8 = lax.broadcasted_iota(I32, (h.shape[0], 8), 1)
    eid_ref[...] = jnp.where(lane8 == 0, e1, jnp.where(lane8 == 1, e2, 0))
    ew_ref[...] = jnp.where(lane8 == 0, w1, jnp.where(lane8 == 1, w2, 0.0))


def _outproj(merged, x, mod, w_out_b, nw, w_r, b_r, tm):
    b, t, d = x.shape
    nt = t // tm
    flat = lambda i, j: (i * nt + j, 0)
    return pl.pallas_call(
        _outproj_kernel,
        out_shape=(jax.ShapeDtypeStruct((b, t, d), F32),
                   jax.ShapeDtypeStruct((b * t, HALF), I32),
                   jax.ShapeDtypeStruct((b * t, 8), I32),
                   jax.ShapeDtypeStruct((b * t, 8), F32)),
        grid=(b, nt),
        in_specs=[pl.BlockSpec((1, tm, d), lambda i, j: (i, j, 0)),
                  pl.BlockSpec((1, tm, d), lambda i, j: (i, j, 0)),
                  pl.BlockSpec((1, 6, d), lambda i, j: (i, 0, 0)),
                  pl.BlockSpec((d, d), lambda i, j: (0, 0)),
                  pl.BlockSpec((1, d), lambda i, j: (0, 0)),
                  pl.BlockSpec((d, 2 * LANES), lambda i, j: (0, 0)),
                  pl.BlockSpec((1, LANES), lambda i, j: (0, 0))],
        out_specs=(pl.BlockSpec((1, tm, d), lambda i, j: (i, j, 0)),
                   pl.BlockSpec((tm, HALF), flat),
                   pl.BlockSpec((tm, 8), flat),
                   pl.BlockSpec((tm, 8), flat)),
        compiler_params=_cparams(("arbitrary", "arbitrary")),
        name="outproj",
    )(merged, x, mod, w_out_b, nw, w_r, b_r)


def _rank_kernel(eid_ref, tri_ref, upper_ref, dest_ref, tot_ref, cnt, starts, *, block):
    ph = pl.program_id(0)
    i = pl.program_id(1)
    tm = eid_ref.shape[0]
    lane = lax.broadcasted_iota(I32, (tm, LANES), 1)
    e0 = eid_ref[:, 0:1]
    e1 = eid_ref[:, 1:2]
    hot0 = lane == e0
    hot1 = lane == e1
    onehot = jnp.where(hot0 | hot1, 1.0, 0.0)
    colsum = jnp.sum(onehot, axis=0, keepdims=True)

    @pl.when((ph == 0) & (i == 0))
    def _():
        cnt[...] = jnp.zeros_like(cnt)

    @pl.when(ph == 0)
    def _():
        cnt[0:1] = cnt[0:1] + colsum

    @pl.when((ph == 1) & (i == 0))
    def _():
        tot = cnt[0:1]
        tot_ref[...] = tot.astype(I32)
        nblk = jnp.floor((tot + (block - 1)) * (1.0 / block))
        hi = jnp.floor(nblk * (1.0 / 16.0))
        lo = nblk - hi * 16.0
        up = upper_ref[...]
        excl = (jnp.dot(jnp.broadcast_to(hi, (8, LANES)).astype(BF16), up, preferred_element_type=F32) * 16.0
                + jnp.dot(jnp.broadcast_to(lo, (8, LANES)).astype(BF16), up, preferred_element_type=F32))
        starts[...] = excl * float(block)
        cnt[...] = jnp.zeros_like(cnt)

    @pl.when(ph == 1)
    def _():
        prefix = jnp.dot(tri_ref[...], onehot.astype(BF16), preferred_element_type=F32)
        pos = prefix + (starts[0:1] + cnt[0:1])
        d0 = jnp.sum(jnp.where(hot0, pos, 0.0), axis=-1, keepdims=True).astype(I32)
        d1 = jnp.sum(jnp.where(hot1, pos, 0.0), axis=-1, keepdims=True).astype(I32)
        lane8 = lax.broadcasted_iota(I32, (tm, 8), 1)
        dest_ref[...] = jnp.where(lane8 == 0, d0, jnp.where(lane8 == 1, d1, 0))
        cnt[0:1] = cnt[0:1] + colsum


def _rank(eid, block):
    n = eid.shape[0]
    tm = math.gcd(n, RANK_TILE)
    r = np.arange(tm)
    tri = jnp.asarray((r[:, None] > r[None, :]).astype(np.float32), BF16)
    l = np.arange(LANES)
    upper = jnp.asarray(((l[:, None] < l[None, :]) & (l[None, :] < N_EXPERTS)).astype(np.float32), BF16)
    return pl.pallas_call(
        functools.partial(_rank_kernel, block=block),
        out_shape=(jax.ShapeDtypeStruct((n, 8), I32), jax.ShapeDtypeStruct((1, LANES), I32)),
        grid=(2, n // tm),
        in_specs=[pl.BlockSpec((tm, 8), lambda p, i: (i, 0)),
                  pl.BlockSpec((tm, tm), lambda p, i: (0, 0)),
                  pl.BlockSpec((LANES, LANES), lambda p, i: (0, 0))],
        out_specs=(pl.BlockSpec((tm, 8), lambda p, i: (i * p, 0)),
                   pl.BlockSpec((1, LANES), lambda p, i: (0, 0))),
        scratch_shapes=[pltpu.VMEM((8, LANES), F32), pltpu.VMEM((8, LANES), F32)],
        compiler_params=_cparams(("arbitrary", "arbitrary")),
        name="rank",
    )(eid, tri, upper)


def _expert_kernel(be_ref, nv_ref, xs_ref, wg_ref, wu_ref, wd_ref, ys_ref, wg_s, wu_s, wd_s):
    b = pl.program_id(0)

    @pl.when(b < nv_ref[0])
    def _():
        prev = be_ref[jnp.maximum(b - 1, 0)]
        fresh = (b == 0) | (be_ref[b] != prev)

        @pl.when(fresh)
        def _():
            wg_s[...] = wg_ref[0].astype(BF16)
            wu_s[...] = wu_ref[0].astype(BF16)
            wd_s[...] = wd_ref[0].astype(BF16)

        a, c = _unpack_pairs(xs_ref[...])
        x = jnp.concatenate([a.astype(BF16), c.astype(BF16)], axis=1)
        g = jnp.dot(x, wg_s[...], preferred_element_type=F32)
        u = jnp.dot(x, wu_s[...], preferred_element_type=F32)
        hmid = (g * jax.nn.sigmoid(g) * u).astype(BF16)
        ys_ref[...] = _pack_pairs(jnp.dot(hmid, wd_s[...], preferred_element_type=F32))


def _experts(xs, block_e, n_valid, w_gate, w_up, w_down, block):
    n_rows = xs.shape[0]
    nb = n_rows // block
    rowblk = lambda b, be, nv: (jnp.minimum(b, nv[0] - 1), 0)
    wblk = lambda b, be, nv: (be[jnp.minimum(b, nv[0] - 1)], 0, 0)
    grid_spec = pltpu.PrefetchScalarGridSpec(
        num_scalar_prefetch=2,
        grid=(nb,),
        in_specs=[pl.BlockSpec((block, HALF), rowblk),
                  pl.BlockSpec((1, D_MODEL, D_EXPERT), wblk),
                  pl.BlockSpec((1, D_MODEL, D_EXPERT), wblk),
                  pl.BlockSpec((1, D_EXPERT, D_MODEL), wblk)],
        out_specs=pl.BlockSpec((block, HALF), rowblk),
        scratch_shapes=[pltpu.VMEM((D_MODEL, D_EXPERT), BF16),
                        pltpu.VMEM((D_MODEL, D_EXPERT), BF16),
                        pltpu.VMEM((D_EXPERT, D_MODEL), BF16)])
    return pl.pallas_call(
        _expert_kernel,
        out_shape=jax.ShapeDtypeStruct((n_rows, HALF), I32),
        grid_spec=grid_spec,
        compiler_params=_cparams(("arbitrary",)),
        name="experts",
    )(block_e, n_valid, xs, w_gate, w_up, w_down)


def _final_kernel(x1_ref, y0_ref, y1_ref, ew_ref, mod_ref, o_ref):
    a0, b0 = _unpack_pairs(y0_ref[...])
    a1, b1 = _unpack_pairs(y1_ref[...])
    w0 = ew_ref[:, 0:1]
    w1 = ew_ref[:, 1:2]
    moe = jnp.concatenate([w0 * a0 + w1 * a1, w0 * b0 + w1 * b1], axis=1)
    o_ref[0] = x1_ref[0] + mod_ref[0][5:6] * moe


def _final(x1, y0, y1, ew, mod, tm):
    b, t, d = x1.shape
    nt = t // tm
    flat = lambda i, j: (i * nt + j, 0)
    return pl.pallas_call(
        _final_kernel,
        out_shape=jax.ShapeDtypeStruct((b, t, d), F32),
        grid=(b, nt),
        in_specs=[pl.BlockSpec((1, tm, d), lambda i, j: (i, j, 0)),
                  pl.BlockSpec((tm, HALF), flat),
                  pl.BlockSpec((tm, HALF), flat),
                  pl.BlockSpec((tm, 8), flat),
                  pl.BlockSpec((1, 6, d), lambda i, j: (i, 0, 0))],
        out_specs=pl.BlockSpec((1, tm, d), lambda i, j: (i, j, 0)),
        compiler_params=_cparams(("arbitrary", "arbitrary")),
        name="final",
    )(x1, y0, y1, ew, mod)


def _sc_window(rows_per_worker):
    for w in range(SC_MAX_WINDOW, 7, -8):
        if rows_per_worker % w == 0:
            return w
    raise ValueError(f"no SparseCore window divides {rows_per_worker} rows per worker")


def _sc_split(idx):
    n = idx.shape[0]
    per = n // SC_WORKERS
    assert per * SC_WORKERS == n
    win = _sc_window(per)
    return idx.reshape(SC_WORKERS, per // win, win), per // win, win


def _sc_worker_id():
    return lax.axis_index("s") * SC_CORES + lax.axis_index("c")


def _dispatch_rows(h2_groups, dest_groups, n_rows):
    splits = [(_sc_split(d[:, 0]), _sc_split(d[:, 1])) for d in dest_groups]
    ng = len(h2_groups)
    scratch = []
    for (_, _, win), _ in splits:
        scratch += [pltpu.VMEM((win,), I32), pltpu.VMEM((win,), I32), pltpu.VMEM((win, HALF), I32)]

    @functools.partial(
        pl.kernel,
        mesh=plsc.VectorSubcoreMesh(core_axis_name="c", subcore_axis_name="s"),
        out_type=jax.ShapeDtypeStruct((n_rows, HALF), I32),
        scratch_types=scratch,
        name="sc_dispatch",
    )
    def k(*refs):
        x_refs, idx_refs, o_hbm, bufs = refs[:ng], refs[ng:3 * ng], refs[3 * ng], refs[3 * ng + 1:]
        wid = _sc_worker_id()
        for g in range(ng):
            (_, nwin, win), _ = splits[g]
            x_hbm, d0_hbm, d1_hbm = x_refs[g], idx_refs[2 * g], idx_refs[2 * g + 1]
            i0_v, i1_v, rows_v = bufs[3 * g:3 * g + 3]

            @pl.loop(0, nwin)
            def _(j, nwin=nwin, win=win, x_hbm=x_hbm, d0_hbm=d0_hbm, d1_hbm=d1_hbm,
                  i0_v=i0_v, i1_v=i1_v, rows_v=rows_v):
                base = pl.multiple_of((wid * nwin + j) * win, 8)
                pltpu.sync_copy(d0_hbm.at[wid, j], i0_v)
                pltpu.sync_copy(d1_hbm.at[wid, j], i1_v)
                pltpu.sync_copy(x_hbm.at[pl.ds(base, win)], rows_v)
                pltpu.sync_copy(rows_v, o_hbm.at[i0_v])
                pltpu.sync_copy(rows_v, o_hbm.at[i1_v])

    idx_args = []
    for (s0, s1) in splits:
        idx_args += [s0[0], s1[0]]
    return k(*h2_groups, *idx_args)


def _collect_rows(ys, dest_groups):
    splits = [(_sc_split(d[:, 0]), _sc_split(d[:, 1])) for d in dest_groups]
    ng = len(dest_groups)
    outs, scratch = [], []
    for d, ((_, _, win), _) in zip(dest_groups, splits):
        o = jax.ShapeDtypeStruct((d.shape[0], HALF), I32)
        outs += [o, o]
        scratch += [pltpu.VMEM((win,), I32), pltpu.VMEM((win, HALF), I32)]

    @functools.partial(
        pl.kernel,
        mesh=plsc.VectorSubcoreMesh(core_axis_name="c", subcore_axis_name="s"),
        out_type=tuple(outs),
        scratch_types=scratch,
        name="sc_collect",
    )
    def k(*refs):
        ys_hbm, idx_refs, out_refs, bufs = refs[0], refs[1:1 + 2 * ng], refs[1 + 2 * ng:1 + 4 * ng], refs[1 + 4 * ng:]
        wid = _sc_worker_id()
        for g in range(ng):
            (_, nwin, win), _ = splits[g]
            i_v, rows_v = bufs[2 * g:2 * g + 2]
            for kk in range(2):
                d_hbm, y_hbm = idx_refs[2 * g + kk], out_refs[2 * g + kk]

                @pl.loop(0, nwin)
                def _(j, nwin=nwin, win=win, d_hbm=d_hbm, y_hbm=y_hbm, i_v=i_v, rows_v=rows_v):
                    base = pl.multiple_of((wid * nwin + j) * win, 8)
                    pltpu.sync_copy(d_hbm.at[wid, j], i_v)
                    pltpu.sync_copy(ys_hbm.at[i_v], rows_v)
                    pltpu.sync_copy(rows_v, y_hbm.at[pl.ds(base, win)])

    idx_args = []
    for (s0, s1) in splits:
        idx_args += [s0[0], s1[0]]
    res = k(ys, *idx_args)
    return [(res[2 * g], res[2 * g + 1]) for g in range(ng)]


def _t5_bucket(rel):
    half = N_BUCKETS // 2
    max_exact = half // 2
    n = jnp.abs(rel)
    far = max_exact + (jnp.log(jnp.maximum(n, 1).astype(F32) / max_exact)
                       / math.log(MAX_DISTANCE / max_exact) * (half - max_exact)).astype(I32)
    far = jnp.minimum(far, half - 1)
    return jnp.where(rel > 0, half, 0) + jnp.where(n < max_exact, n, far)


def _bias_table(rel_bias, cq, nq, no_history):
    nk = WINDOW + cq
    j = jnp.arange(KEY_WIN)[:, None]
    c = jnp.arange(UNIT_Q)[None, :]
    jj = j - (c // cq) * cq
    valid = (jj >= 0) & (jj < nk) & (c < nq)
    if no_history:
        valid = valid & (j >= WINDOW)
    rel = jj - WINDOW - (c % cq)
    onehot = (_t5_bucket(rel)[:, :, None] == jnp.arange(N_BUCKETS)).astype(F32)
    bias = jnp.einsum("jcb,bh->jch", onehot, rel_bias.astype(F32), precision=lax.Precision.HIGHEST)
    bias = jnp.where(valid[:, :, None], bias, -jnp.inf)
    bias = jnp.transpose(bias.reshape(KEY_WIN, UNIT_Q, N_KV, GROUP), (2, 0, 3, 1))
    return bias.reshape(N_KV, KEY_WIN, GROUP * UNIT_Q)


def _sink_table(sinks):
    s = sinks.astype(F32).reshape(N_KV, 1, GROUP, 1)
    return jnp.broadcast_to(s, (N_KV, 1, GROUP, UNIT_Q)).reshape(N_KV, 1, GROUP * UNIT_Q)


def kernel(x_prompt, x_sample, state_attn_k, state_attn_v, state_conv, c_prompt, c_sample,
           rel_bias, w_ada, b_ada, norm1_w, w_in, q_norm_w, k_norm_w, attn_sinks, conv_w,
           w_out, norm2_w, w_router_group, b_router_group, w_router_expert, b_router_expert,
           w_gate, w_up, w_down):
    depth = w_ada.shape[0]
    assert depth == 1
    bp, tp, d = x_prompt.shape
    bs, ts, _ = x_sample.shape
    n_p, n_s = bp * tp, bs * ts
    n_tok = n_p + n_s
    l = 0

    wi = w_in[l]
    qw, kw, vw, rest = wi[:, :d], wi[:, d:d + KV_COLS], wi[:, d + KV_COLS:d + 2 * KV_COLS], wi[:, d + 2 * KV_COLS:]
    w_in_b = jnp.concatenate([qw, rest, kw, vw], axis=1).astype(BF16)
    w_out_b = w_out[l].astype(BF16)
    w_r = jnp.concatenate([w_router_group[l],
                           jnp.transpose(w_router_expert[l], (1, 0, 2)).reshape(d, N_EXPERTS),
                           jnp.zeros((d, LANES - N_GROUPS - N_EXPERTS), F32)], axis=1)
    w_r_hi = lax.reduce_precision(w_r, exponent_bits=8, mantissa_bits=7)
    w_r = jnp.concatenate([w_r_hi.astype(BF16), (w_r - w_r_hi).astype(BF16)], axis=1)
    b_r = jnp.concatenate([b_router_group[l], b_router_expert[l].reshape(-1),
                           jnp.zeros((LANES - N_GROUPS - N_EXPERTS,), F32)]).reshape(1, LANES)
    knw = jnp.tile(k_norm_w[l], N_KV).reshape(1, KV_COLS)
    qsc = jnp.tile(q_norm_w[l] * (HEAD_DIM ** -0.5), N_KV).reshape(1, KV_COLS)
    n1w = norm1_w[l].reshape(1, d)
    n2w = norm2_w[l].reshape(1, d)

    mod = _ada(jnp.concatenate([c_prompt, c_sample], axis=0), w_ada[l], b_ada[l]).reshape(bp + bs, 6, d)
    mod_p, mod_s = mod[:bp], mod[bp:]

    proj_s = _inproj(x_sample, mod_s, n1w, w_in_b, ts)
    sink_tab = _sink_table(attn_sinks[l])
    bias_p = jnp.concatenate([_bias_table(rel_bias, CHUNK, UNIT_Q, False),
                              _bias_table(rel_bias, CHUNK, UNIT_Q, True)], axis=0)
    merged_p, k_p, v_p, c_p = _front(x_prompt, mod_p, n1w, w_in_b, knw, qsc, bias_p, sink_tab, conv_w[l],
                                     MIX_TILE, UNIT_Q)
    state = (state_attn_k[l].reshape(bs, WINDOW, KV_COLS), state_attn_v[l].reshape(bs, WINDOW, KV_COLS),
             state_conv[l])
    merged_s, k_s, v_s, c_s = _mixer(proj_s, knw, qsc, _bias_table(rel_bias, ts, ts, False), sink_tab, conv_w[l],
                                     state, ts, ts)

    x1_p, h2_p, eid_p, ew_p = _outproj(merged_p, x_prompt, mod_p, w_out_b, n2w, w_r, b_r, ROW_TILE)
    x1_s, h2_s, eid_s, ew_s = _outproj(merged_s, x_sample, mod_s, w_out_b, n2w, w_r, b_r, ts)

    dest, totals = _rank(jnp.concatenate([eid_p, eid_s], axis=0), EXPERT_BLOCK)
    dests = [dest[:n_p], dest[n_p:]]
    nblk = (totals[0, :N_EXPERTS] + EXPERT_BLOCK - 1) // EXPERT_BLOCK
    blk_end = jnp.cumsum(nblk)
    nb_max = -(-2 * n_tok // EXPERT_BLOCK) + N_EXPERTS
    block_e = jnp.minimum(jnp.sum(blk_end[None, :] <= jnp.arange(nb_max)[:, None], axis=1), N_EXPERTS - 1).astype(I32)
    n_valid = blk_end[-1:].astype(I32)

    xs = _dispatch_rows([h2_p, h2_s], dests, nb_max * EXPERT_BLOCK)
    ys = _experts(xs, block_e, n_valid, w_gate[l], w_up[l], w_down[l], EXPERT_BLOCK)
    (y0_p, y1_p), (y0_s, y1_s) = _collect_rows(ys, dests)

    y_p = _final(x1_p, y0_p, y1_p, ew_p, mod_p, ROW_TILE)
    y_s = _final(x1_s, y0_s, y1_s, ew_s, mod_s, ts)

    kv_shape = (1, -1, WINDOW, N_KV, HEAD_DIM)
    return (y_p, y_s, k_p.reshape(kv_shape), v_p.reshape(kv_shape), c_p[None],
            k_s.reshape(kv_shape), v_s.reshape(kv_shape), c_s[None])
```

```python
import functools
import math

import numpy as np
import jax
import jax.numpy as jnp
from jax import lax
from jax.experimental import pallas as pl
from jax.experimental.pallas import tpu as pltpu
from jax.experimental.pallas import tpu_sc as plsc

F32 = jnp.float32
BF16 = jnp.bfloat16
I32 = jnp.int32

D_MODEL = 1024
HEAD_DIM = 64
N_HEADS = 16
N_KV = 4
GROUP = 4
CHUNK = 64
WINDOW = 128
N_BUCKETS = 32
MAX_DISTANCE = 128
N_GROUPS = 8
EPG = 8
N_EXPERTS = 64
D_EXPERT = 512
RMS_EPS = 1e-6
KV_COLS = N_KV * HEAD_DIM
IN_COLS = 6 * D_MODEL + 2 * KV_COLS
HALF = D_MODEL // 2
LANES = 128

VMEM_LIMIT = 56 * 1024 * 1024
INPROJ_TN = 512
ROW_TILE = 512
MIX_TILE = 512
UNIT_Q = 2 * CHUNK
KEY_WIN = WINDOW + UNIT_Q
PROJ_CHUNK = 256
EXPERT_BLOCK = 256
RANK_TILE = 512
SC_CORES = 2
SC_SUBCORES = 16
SC_WORKERS = SC_CORES * SC_SUBCORES
SC_MAX_WINDOW = 128


def _cparams(sem):
    return pltpu.CompilerParams(dimension_semantics=sem, vmem_limit_bytes=VMEM_LIMIT)


def _split_bf16(a):
    hi = a.astype(BF16)
    lo = (a - hi.astype(F32)).astype(BF16)
    return hi, lo


def _dot3(a, b):
    ah, al = _split_bf16(a)
    bh, bl = _split_bf16(b)
    d = functools.partial(jnp.dot, preferred_element_type=F32)
    return d(ah, bh) + (d(ah, bl) + d(al, bh))


def _pack_pairs(y):
    a = lax.bitcast_convert_type(y[:, :HALF].astype(BF16).astype(F32), I32)
    b = lax.bitcast_convert_type(y[:, HALF:].astype(BF16).astype(F32), I32)
    return a | lax.shift_right_logical(b, jnp.int32(16))


def _unpack_pairs(w):
    a = lax.bitcast_convert_type(w & jnp.int32(-65536), F32)
    b = lax.bitcast_convert_type(lax.shift_left(w, jnp.int32(16)), F32)
    return a, b


def _ada_kernel(c_ref, w_ref, b_ref, o_ref):
    c = c_ref[...]
    s = c * jax.nn.sigmoid(c)
    o_ref[...] = _dot3(s, w_ref[...]) + b_ref[...]


def _ada(c_all, w_ada, b_ada):
    r, d = c_all.shape
    n = w_ada.shape[1]
    tn = 1024
    return pl.pallas_call(
        _ada_kernel,
        out_shape=jax.ShapeDtypeStruct((r, n), F32),
        grid=(n // tn,),
        in_specs=[pl.BlockSpec((r, d), lambda j: (0, 0)),
                  pl.BlockSpec((d, tn), lambda j: (0, j)),
                  pl.BlockSpec((1, tn), lambda j: (0, j))],
        out_specs=pl.BlockSpec((r, tn), lambda j: (0, j)),
        compiler_params=_cparams(("arbitrary",)),
        name="ada",
    )(c_all, w_ada, b_ada.reshape(1, n))


def _inproj_kernel(x_ref, mod_ref, nw_ref, w_ref, o_ref):
    x = x_ref[0]
    mod = mod_ref[0]
    h = x * lax.rsqrt(jnp.mean(x * x, axis=-1, keepdims=True) + RMS_EPS) * nw_ref[...]
    h = h * (1.0 + mod[1:2]) + mod[0:1]
    hb = h.astype(BF16)
    for j in range(IN_COLS // INPROJ_TN):
        sl = slice(j * INPROJ_TN, (j + 1) * INPROJ_TN)
        o_ref[0, :, sl] = jnp.dot(hb, w_ref[:, sl], preferred_element_type=F32).astype(BF16)


def _inproj(x, mod, nw, w_in_b, tm):
    b, t, d = x.shape
    return pl.pallas_call(
        _inproj_kernel,
        out_shape=jax.ShapeDtypeStruct((b, t, IN_COLS), BF16),
        grid=(b, t // tm),
        in_specs=[pl.BlockSpec((1, tm, d), lambda i, j: (i, j, 0)),
                  pl.BlockSpec((1, 6, d), lambda i, j: (i, 0, 0)),
                  pl.BlockSpec((1, d), lambda i, j: (0, 0)),
                  pl.BlockSpec((d, IN_COLS), lambda i, j: (0, 0), pipeline_mode=pl.Buffered(1))],
        out_specs=pl.BlockSpec((1, tm, IN_COLS), lambda i, j: (i, j, 0)),
        compiler_params=_cparams(("arbitrary", "arbitrary")),
        name="inproj",
    )(x, mod, nw, w_in_b)


def _head_inv_rms(xf, bd):
    hi, lo = _split_bf16(xf * xf)
    ssq = jnp.dot(hi, bd, preferred_element_type=F32) + jnp.dot(lo, bd, preferred_element_type=F32)
    return lax.rsqrt(ssq * (1.0 / HEAD_DIM) + RMS_EPS)


def _mixer_body(q_ref, k_ref, v_ref, bg_ref, c_ref, u_ref, ga_ref, gc_ref,
                knw_ref, qsc_ref, bd_ref, bias_ref, sink_ref, cw_ref,
                kpast_ref, vpast_ref, cpast_ref, upast_ref,
                merged_ref, knew_ref, vnew_ref, cnew_ref,
                kq_buf, vt_buf, attn_buf, *, tq, nq, stateful):
    hb = WINDOW
    pw = UNIT_Q
    t = pl.program_id(1)
    bd = bd_ref[...]

    k = k_ref[0].astype(F32)
    kn = k * _head_inv_rms(k, bd) * knw_ref[...]
    kq = (kn * qsc_ref[...]).astype(BF16)
    vb = v_ref[0]
    vt = vb.astype(F32).T.astype(BF16)

    if stateful:
        kp = kpast_ref[0]
        vp = vpast_ref[0]
        for kv in range(N_KV):
            kq_buf[kv, hb + tq:] = jnp.zeros((KEY_WIN - hb - tq, HEAD_DIM), BF16)
        vt_buf[:, hb + tq:] = jnp.zeros((KV_COLS, KEY_WIN - hb - tq), BF16)
        u_hist = jnp.concatenate([jnp.zeros((6, D_MODEL), F32), cpast_ref[0]], axis=0)
        knew_ref[0] = jnp.concatenate([kp[tq:], kn], axis=0)
        vnew_ref[0] = jnp.concatenate([vp[tq:], vb.astype(F32)], axis=0)
    else:
        kraw = kpast_ref[0].astype(F32)
        kp = kraw * _head_inv_rms(kraw, bd) * knw_ref[...]
        vp = vpast_ref[0].astype(F32)
        u_hist = jnp.where(t == 0, 0.0, cpast_ref[0].astype(F32) * upast_ref[0].astype(F32))
        knew_ref[0] = kn[tq - hb:]
        vnew_ref[0] = vb[tq - hb:].astype(F32)
    _fill_keys(kq_buf, vt_buf, kp, vp, kq, vt, qsc_ref, tq)
    q = q_ref[0]
    _attention_units(lambda kv: q[:, kv * KV_COLS:(kv + 1) * KV_COLS].astype(F32), kq_buf, vt_buf,
                     bias_ref, sink_ref, bd, attn_buf, tq, nq, None if stateful else t)
    cu = c_ref[0].astype(F32) * u_ref[0].astype(F32)
    conv, u_all = _short_conv(u_hist, cu, cw_ref[...], tq)
    cnew_ref[0] = u_all[tq + 6:tq + 8]
    merged = (jax.nn.sigmoid(ga_ref[0].astype(F32)) * attn_buf[...]
              + jax.nn.sigmoid(gc_ref[0].astype(F32)) * (bg_ref[0].astype(F32) * conv))
    merged_ref[0] = merged.astype(BF16)


def _fill_keys(kq_buf, vt_buf, kp, vp, kq, vt, qsc_ref, tq):
    hb = WINDOW
    kqp = (kp * qsc_ref[...]).astype(BF16)
    for kv in range(N_KV):
        kq_buf[kv, 0:hb] = kqp[:, kv * HEAD_DIM:(kv + 1) * HEAD_DIM]
        kq_buf[kv, hb:hb + tq] = kq[:, kv * HEAD_DIM:(kv + 1) * HEAD_DIM]
    vt_buf[:, 0:hb] = vp.T.astype(BF16)
    vt_buf[:, hb:hb + tq] = vt


def _attention_units(q_group, kq_buf, vt_buf, bias_ref, sink_ref, bd, attn_buf, tq, nq, t_first, after_unit=None):
    pw = UNIT_Q
    for kv in range(N_KV):
        qf = q_group(kv)
        qn = (qf * _head_inv_rms(qf, bd)).astype(BF16)
        for u in range(tq // nq):
            r0 = u * nq
            parts = [qn[r0:r0 + nq, g * HEAD_DIM:(g + 1) * HEAD_DIM] for g in range(GROUP)]
            if nq < pw:
                zpad = jnp.zeros((pw - nq, HEAD_DIM), BF16)
                parts = [x for p_ in parts for x in (p_, zpad)]
            qs = jnp.concatenate(parts, axis=0)
            kw = kq_buf[kv, r0:r0 + KEY_WIN]
            st = lax.dot_general(kw, qs, (((1,), (1,)), ((), ())), preferred_element_type=F32)
            if t_first is not None and u == 0:
                bias = jnp.where(t_first == 0, bias_ref[kv + N_KV], bias_ref[kv])
            else:
                bias = bias_ref[kv]
            st = st + bias
            sink = sink_ref[kv]
            m = jnp.maximum(jnp.max(st, axis=0, keepdims=True), sink)
            p = jnp.exp(st - m)
            den = jnp.sum(p, axis=0, keepdims=True) + jnp.exp(sink - m)
            ot = jnp.dot(vt_buf[kv * HEAD_DIM:(kv + 1) * HEAD_DIM, r0:r0 + KEY_WIN], p.astype(BF16),
                         preferred_element_type=F32) / den
            for gp in range(GROUP // 2):
                blk = jnp.concatenate([ot[:, (2 * gp) * pw:(2 * gp + 1) * pw],
                                       ot[:, (2 * gp + 1) * pw:(2 * gp + 2) * pw]], axis=0)
                c0 = (kv * GROUP + 2 * gp) * HEAD_DIM
                attn_buf[r0:r0 + nq, c0:c0 + 2 * HEAD_DIM] = blk.T[:nq]
            if after_unit is not None:
                after_unit(kv * (tq // nq) + u)


def _short_conv(u_hist, cu, cw, tq):
    u_all = jnp.concatenate([u_hist, cu], axis=0)
    conv = cw[0:1] * u_all[6:6 + tq] + cw[1:2] * u_all[7:7 + tq] + cw[2:3] * u_all[8:8 + tq]
    return conv, u_all


def _front_body(x_ref, xp_ref, mod_ref, nw_ref, w_ref, knw_ref, qsc_ref, bd_ref, bias_ref, sink_ref, cw_ref,
                merged_ref, knew_ref, vnew_ref, cnew_ref,
                kq_buf, vt_buf, u_buf, attn_buf, pbuf, *, tq, nq):
    hb = WINDOW
    d = D_MODEL
    t = pl.program_id(1)
    bd = bd_ref[...]
    mod = mod_ref[0]

    @pl.when(t == 0)
    def _():
        u_buf[0] = jnp.zeros((8, d), F32)

    def modnorm(x):
        h = x * lax.rsqrt(jnp.mean(x * x, axis=-1, keepdims=True) + RMS_EPS) * nw_ref[...]
        return (h * (1.0 + mod[1:2]) + mod[0:1]).astype(BF16)

    def proj(hrows, c0, width):
        return jnp.dot(hrows, w_ref[:, c0:c0 + width], preferred_element_type=F32)

    hbf = modnorm(x_ref[0])
    kv_t = proj(hbf, 6 * d, 2 * KV_COLS)
    kv_p = proj(modnorm(xp_ref[0]), 6 * d, 2 * KV_COLS)
    k, v = kv_t[:, :KV_COLS], kv_t[:, KV_COLS:]
    kn = k * _head_inv_rms(k, bd) * knw_ref[...]
    kq = (kn * qsc_ref[...]).astype(BF16)
    kraw = kv_p[:, :KV_COLS]
    kp = kraw * _head_inv_rms(kraw, bd) * knw_ref[...]
    knew_ref[0] = kn[tq - hb:]
    vnew_ref[0] = v[tq - hb:]
    _fill_keys(kq_buf, vt_buf, kp, kv_p[:, KV_COLS:], kq, v.T.astype(BF16), qsc_ref, tq)

    n_units = N_KV * (tq // nq)
    q_chunks = [[g * KV_COLS + c for c in range(0, KV_COLS, PROJ_CHUNK)] for g in range(N_KV)]
    rest_chunks = list(range(d, 6 * d, PROJ_CHUNK))

    def issue(c0):
        pbuf[:, c0:c0 + PROJ_CHUNK] = proj(hbf, c0, PROJ_CHUNK)

    for c0 in q_chunks[0]:
        issue(c0)
    n_rest = len(rest_chunks)

    def after_unit(i):
        kv, u = divmod(i, tq // nq)
        if u == 0 and kv + 1 < N_KV:
            for c0 in q_chunks[kv + 1]:
                issue(c0)
        for c0 in rest_chunks[i * n_rest // n_units:(i + 1) * n_rest // n_units]:
            issue(c0)

    _attention_units(lambda kv: pbuf[:, kv * KV_COLS:(kv + 1) * KV_COLS], kq_buf, vt_buf,
                     bias_ref, sink_ref, bd, attn_buf, tq, nq, t, after_unit)

    cu = pbuf[:, 2 * d:3 * d] * pbuf[:, 3 * d:4 * d]
    conv, u_all = _short_conv(u_buf[t % 2], cu, cw_ref[...], tq)
    cnew_ref[0] = u_all[tq + 6:tq + 8]
    u_buf[(t + 1) % 2] = u_all[tq:tq + 8]
    merged = (jax.nn.sigmoid(pbuf[:, 4 * d:5 * d]) * attn_buf[...]
              + jax.nn.sigmoid(pbuf[:, 5 * d:6 * d]) * (pbuf[:, d:2 * d] * conv))
    merged_ref[0] = merged.astype(BF16)


def _front(x, mod, nw, w_in_b, knw, qsc, bias_tab, sink_tab, conv_w, tq, nq):
    b, t, d = x.shape
    r = np.arange(KV_COLS) // HEAD_DIM
    bd = jnp.asarray((r[:, None] == r[None, :]).astype(np.float32), BF16)
    const2 = lambda shp: pl.BlockSpec(shp, lambda i, s: (0, 0))
    const3 = lambda shp: pl.BlockSpec(shp, lambda i, s: (0, 0, 0))
    per_b = lambda shp: pl.BlockSpec(shp, lambda i, s: (i, 0, 0))
    kw_ = tq // WINDOW
    in_specs = [pl.BlockSpec((1, tq, d), lambda i, s: (i, s, 0)),
                pl.BlockSpec((1, WINDOW, d), lambda i, s: (i, jnp.maximum(s * kw_ - 1, 0), 0)),
                per_b((1, 6, d)), const2((1, d)),
                pl.BlockSpec((d, IN_COLS), lambda i, s: (0, 0), pipeline_mode=pl.Buffered(1)),
                const2((1, KV_COLS)), const2((1, KV_COLS)), const2((KV_COLS, KV_COLS)),
                const3(bias_tab.shape), const3(sink_tab.shape), const2((3, d))]
    out_shape = (jax.ShapeDtypeStruct((b, t, d), BF16),
                 jax.ShapeDtypeStruct((b, WINDOW, KV_COLS), F32),
                 jax.ShapeDtypeStruct((b, WINDOW, KV_COLS), F32),
                 jax.ShapeDtypeStruct((b, 2, d), F32))
    out_specs = (pl.BlockSpec((1, tq, d), lambda i, s: (i, s, 0)),
                 per_b((1, WINDOW, KV_COLS)), per_b((1, WINDOW, KV_COLS)), per_b((1, 2, d)))
    return pl.pallas_call(
        functools.partial(_front_body, tq=tq, nq=nq),
        out_shape=out_shape,
        grid=(b, t // tq),
        in_specs=in_specs,
        out_specs=out_specs,
        scratch_shapes=[pltpu.VMEM((N_KV, WINDOW + tq, HEAD_DIM), BF16),
                        pltpu.VMEM((KV_COLS, WINDOW + tq), BF16),
                        pltpu.VMEM((2, 8, d), F32),
                        pltpu.VMEM((tq, d), F32),
                        pltpu.VMEM((tq, 6 * d), F32)],
        compiler_params=_cparams(("arbitrary", "arbitrary")),
        name="front",
    )(x, x, mod, nw, w_in_b, knw, qsc, bd, bias_tab, sink_tab, conv_w)


def _mixer(proj, knw, qsc, bias_tab, sink_tab, conv_w, state, tq, nq):
    b, t, _ = proj.shape
    d = D_MODEL
    stateful = state is not None
    key_rows = max(WINDOW + tq, KEY_WIN)
    r = np.arange(KV_COLS) // HEAD_DIM
    bd = jnp.asarray((r[:, None] == r[None, :]).astype(np.float32), BF16)
    wide = lambda j: pl.BlockSpec((1, tq, d), lambda i, s, j=j: (i, s, j))
    kvspec = lambda j: pl.BlockSpec((1, tq, KV_COLS), lambda i, s, j=j: (i, s, j))
    const2 = lambda shp: pl.BlockSpec(shp, lambda i, s: (0, 0))
    const3 = lambda shp: pl.BlockSpec(shp, lambda i, s: (0, 0, 0))
    per_b = lambda shp: pl.BlockSpec(shp, lambda i, s: (i, 0, 0))
    kvblk = 6 * d // KV_COLS
    if stateful:
        hist_specs = [per_b((1, WINDOW, KV_COLS)), per_b((1, WINDOW, KV_COLS)), per_b((1, 2, d)), per_b((1, 2, d))]
        hist_args = [state[0], state[1], state[2], state[2]]
    else:
        kw_ = tq // WINDOW
        prev_kv = lambda j: pl.BlockSpec((1, WINDOW, KV_COLS),
                                         lambda i, s, j=j: (i, jnp.maximum(s * kw_ - 1, 0), j))
        prev8 = lambda j: pl.BlockSpec((1, 8, d), lambda i, s, j=j: (i, jnp.maximum(s * (tq // 8) - 1, 0), j))
        hist_specs = [prev_kv(kvblk), prev_kv(kvblk + 1), prev8(2), prev8(3)]
        hist_args = [proj, proj, proj, proj]
    in_specs = [wide(0), kvspec(kvblk), kvspec(kvblk + 1), wide(1), wide(2), wide(3), wide(4), wide(5),
                const2((1, KV_COLS)), const2((1, KV_COLS)), const2((KV_COLS, KV_COLS)),
                const3(bias_tab.shape), const3(sink_tab.shape), const2((3, d))] + hist_specs
    out_shape = (jax.ShapeDtypeStruct((b, t, d), BF16),
                 jax.ShapeDtypeStruct((b, WINDOW, KV_COLS), F32),
                 jax.ShapeDtypeStruct((b, WINDOW, KV_COLS), F32),
                 jax.ShapeDtypeStruct((b, 2, d), F32))
    out_specs = (pl.BlockSpec((1, tq, d), lambda i, s: (i, s, 0)),
                 per_b((1, WINDOW, KV_COLS)), per_b((1, WINDOW, KV_COLS)), per_b((1, 2, d)))
    return pl.pallas_call(
        functools.partial(_mixer_body, tq=tq, nq=nq, stateful=stateful),
        out_shape=out_shape,
        grid=(b, t // tq),
        in_specs=in_specs,
        out_specs=out_specs,
        scratch_shapes=[pltpu.VMEM((N_KV, key_rows, HEAD_DIM), BF16),
                        pltpu.VMEM((KV_COLS, key_rows), BF16),
                        pltpu.VMEM((tq, d), F32)],
        compiler_params=_cparams(("arbitrary", "arbitrary")),
        name="mixer_state" if stateful else "mixer",
    )(proj, proj, proj, proj, proj, proj, proj, proj, knw, qsc, bd, bias_tab, sink_tab, conv_w, *hist_args)


def _route(logits):
    lane = lax.broadcasted_iota(I32, logits.shape, 1)
    neg = -jnp.inf
    big = jnp.int32(1 << 20)
    gl = jnp.where(lane < N_GROUPS, logits, neg)
    gmax = jnp.max(gl, axis=-1, keepdims=True)
    g_idx = jnp.min(jnp.where(gl == gmax, lane, big), axis=-1, keepdims=True)
    g_w = 1.0 / jnp.sum(jnp.exp(gl - gmax), axis=-1, keepdims=True)
    lo = N_GROUPS + g_idx * EPG
    el = jnp.where((lane >= lo) & (lane < lo + EPG), logits, neg)
    m1 = jnp.max(el, axis=-1, keepdims=True)
    i1 = jnp.min(jnp.where(el == m1, lane, big), axis=-1, keepdims=True)
    el2 = jnp.where(lane == i1, neg, el)
    m2 = jnp.max(el2, axis=-1, keepdims=True)
    i2 = jnp.min(jnp.where(el2 == m2, lane, big), axis=-1, keepdims=True)
    r = jnp.exp(m2 - m1)
    w1 = 1.0 / (1.0 + r)
    w2 = r / (1.0 + r)
    return i1 - N_GROUPS, i2 - N_GROUPS, g_w * w1, g_w * w2


def _outproj_kernel(m_ref, x_ref, mod_ref, wo_ref, nw_ref, wr_ref, br_ref,
                    x1_ref, h2_ref, eid_ref, ew_ref):
    mod = mod_ref[0]
    mix = jnp.dot(m_ref[0], wo_ref[...], preferred_element_type=F32)
    x1 = x_ref[0] + mod[2:3] * mix
    x1_ref[0] = x1
    h = x1 * lax.rsqrt(jnp.mean(x1 * x1, axis=-1, keepdims=True) + RMS_EPS) * nw_ref[...]
    h = h * (1.0 + mod[4:5]) + mod[3:4]
    h2_ref[...] = _pack_pairs(h)
    tm = h.shape[0]
    h_hi, h_lo = _split_bf16(h)
    prod = jnp.dot(jnp.concatenate([h_hi, h_lo], axis=0), wr_ref[...], preferred_element_type=F32)
    logits = prod[:tm, :LANES] + (prod[:tm, LANES:] + prod[tm:, :LANES]) + br_ref[...]
    e1, e2, w1, w2 = _route(logits)
    lane8 = lax.broadcasted_iota(I32, (h.shape[0], 8), 1)
    eid_ref[...] = jnp.where(lane8 == 0, e1, jnp.where(lane8 == 1, e2, 0))
    ew_ref[...] = jnp.where(lane8 == 0, w1, jnp.where(lane8 == 1, w2, 0.0))


def _outproj(merged, x, mod, w_out_b, nw, w_r, b_r, tm):
    b, t, d = x.shape
    nt = t // tm
    flat = lambda i, j: (i * nt + j, 0)
    return pl.pallas_call(
        _outproj_kernel,
        out_shape=(jax.ShapeDtypeStruct((b, t, d), F32),
                   jax.ShapeDtypeStruct((b * t, HALF), I32),
                   jax.ShapeDtypeStruct((b * t, 8), I32),
                   jax.ShapeDtypeStruct((b * t, 8), F32)),
        grid=(b, nt),
        in_specs=[pl.BlockSpec((1, tm, d), lambda i, j: (i, j, 0)),
                  pl.BlockSpec((1, tm, d), lambda i, j: (i, j, 0)),
                  pl.BlockSpec((1, 6, d), lambda i, j: (i, 0, 0)),
                  pl.BlockSpec((d, d), lambda i, j: (0, 0)),
                  pl.BlockSpec((1, d), lambda i, j: (0, 0)),
                  pl.BlockSpec((d, 2 * LANES), lambda i, j: (0, 0)),
                  pl.BlockSpec((1, LANES), lambda i, j: (0, 0))],
        out_specs=(pl.BlockSpec((1, tm, d), lambda i, j: (i, j, 0)),
                   pl.BlockSpec((tm, HALF), flat),
                   pl.BlockSpec((tm, 8), flat),
                   pl.BlockSpec((tm, 8), flat)),
        compiler_params=_cparams(("arbitrary", "arbitrary")),
        name="outproj",
    )(merged, x, mod, w_out_b, nw, w_r, b_r)


def _rank_kernel(eid_ref, tri_ref, upper_ref, dest_ref, tot_ref, cnt, starts, *, block):
    ph = pl.program_id(0)
    i = pl.program_id(1)
    tm = eid_ref.shape[0]
    lane = lax.broadcasted_iota(I32, (tm, LANES), 1)
    e0 = eid_ref[:, 0:1]
    e1 = eid_ref[:, 1:2]
    hot0 = lane == e0
    hot1 = lane == e1
    onehot = jnp.where(hot0 | hot1, 1.0, 0.0)
    colsum = jnp.sum(onehot, axis=0, keepdims=True)

    @pl.when((ph == 0) & (i == 0))
    def _():
        cnt[...] = jnp.zeros_like(cnt)

    @pl.when(ph == 0)
    def _():
        cnt[0:1] = cnt[0:1] + colsum

    @pl.when((ph == 1) & (i == 0))
    def _():
        tot = cnt[0:1]
        tot_ref[...] = tot.astype(I32)
        nblk = jnp.floor((tot + (block - 1)) * (1.0 / block))
        hi = jnp.floor(nblk * (1.0 / 16.0))
        lo = nblk - hi * 16.0
        up = upper_ref[...]
        excl = (jnp.dot(jnp.broadcast_to(hi, (8, LANES)).astype(BF16), up, preferred_element_type=F32) * 16.0
                + jnp.dot(jnp.broadcast_to(lo, (8, LANES)).astype(BF16), up, preferred_element_type=F32))
        starts[...] = excl * float(block)
        cnt[...] = jnp.zeros_like(cnt)

    @pl.when(ph == 1)
    def _():
        prefix = jnp.dot(tri_ref[...], onehot.astype(BF16), preferred_element_type=F32)
        pos = prefix + (starts[0:1] + cnt[0:1])
        d0 = jnp.sum(jnp.where(hot0, pos, 0.0), axis=-1, keepdims=True).astype(I32)
        d1 = jnp.sum(jnp.where(hot1, pos, 0.0), axis=-1, keepdims=True).astype(I32)
        lane8 = lax.broadcasted_iota(I32, (tm, 8), 1)
        dest_ref[...] = jnp.where(lane8 == 0, d0, jnp.where(lane8 == 1, d1, 0))
        cnt[0:1] = cnt[0:1] + colsum


def _rank(eid, block):
    n = eid.shape[0]
    tm = math.gcd(n, RANK_TILE)
    r = np.arange(tm)
    tri = jnp.asarray((r[:, None] > r[None, :]).astype(np.float32), BF16)
    l = np.arange(LANES)
    upper = jnp.asarray(((l[:, None] < l[None, :]) & (l[None, :] < N_EXPERTS)).astype(np.float32), BF16)
    return pl.pallas_call(
        functools.partial(_rank_kernel, block=block),
        out_shape=(jax.ShapeDtypeStruct((n, 8), I32), jax.ShapeDtypeStruct((1, LANES), I32)),
        grid=(2, n // tm),
        in_specs=[pl.BlockSpec((tm, 8), lambda p, i: (i, 0)),
                  pl.BlockSpec((tm, tm), lambda p, i: (0, 0)),
                  pl.BlockSpec((LANES, LANES), lambda p, i: (0, 0))],
        out_specs=(pl.BlockSpec((tm, 8), lambda p, i: (i * p, 0)),
                   pl.BlockSpec((1, LANES), lambda p, i: (0, 0))),
        scratch_shapes=[pltpu.VMEM((8, LANES), F32), pltpu.VMEM((8, LANES), F32)],
        compiler_params=_cparams(("arbitrary", "arbitrary")),
        name="rank",
    )(eid, tri, upper)


def _expert_kernel(be_ref, nv_ref, xs_ref, wg_ref, wu_ref, wd_ref, ys_ref, wg_s, wu_s, wd_s):
    b = pl.program_id(0)

    @pl.when(b < nv_ref[0])
    def _():
        prev = be_ref[jnp.maximum(b - 1, 0)]
        fresh = (b == 0) | (be_ref[b] != prev)

        @pl.when(fresh)
        def _():
            wg_s[...] = wg_ref[0].astype(BF16)
            wu_s[...] = wu_ref[0].astype(BF16)
            wd_s[...] = wd_ref[0].astype(BF16)

        a, c = _unpack_pairs(xs_ref[...])
        x = jnp.concatenate([a.astype(BF16), c.astype(BF16)], axis=1)
        g = jnp.dot(x, wg_s[...], preferred_element_type=F32)
        u = jnp.dot(x, wu_s[...], preferred_element_type=F32)
        hmid = (g * jax.nn.sigmoid(g) * u).astype(BF16)
        ys_ref[...] = _pack_pairs(jnp.dot(hmid, wd_s[...], preferred_element_type=F32))


def _experts(xs, block_e, n_valid, w_gate, w_up, w_down, block):
    n_rows = xs.shape[0]
    nb = n_rows // block
    rowblk = lambda b, be, nv: (jnp.minimum(b, nv[0] - 1), 0)
    wblk = lambda b, be, nv: (be[jnp.minimum(b, nv[0] - 1)], 0, 0)
    grid_spec = pltpu.PrefetchScalarGridSpec(
        num_scalar_prefetch=2,
        grid=(nb,),
        in_specs=[pl.BlockSpec((block, HALF), rowblk),
                  pl.BlockSpec((1, D_MODEL, D_EXPERT), wblk),
                  pl.BlockSpec((1, D_MODEL, D_EXPERT), wblk),
                  pl.BlockSpec((1, D_EXPERT, D_MODEL), wblk)],
        out_specs=pl.BlockSpec((block, HALF), rowblk),
        scratch_shapes=[pltpu.VMEM((D_MODEL, D_EXPERT), BF16),
                        pltpu.VMEM((D_MODEL, D_EXPERT), BF16),
                        pltpu.VMEM((D_EXPERT, D_MODEL), BF16)])
    return pl.pallas_call(
        _expert_kernel,
        out_shape=jax.ShapeDtypeStruct((n_rows, HALF), I32),
        grid_spec=grid_spec,
        compiler_params=_cparams(("arbitrary",)),
        name="experts",
    )(block_e, n_valid, xs, w_gate, w_up, w_down)


def _final_kernel(x1_ref, y0_ref, y1_ref, ew_ref, mod_ref, o_ref):
    a0, b0 = _unpack_pairs(y0_ref[...])
    a1, b1 = _unpack_pairs(y1_ref[...])
    w0 = ew_ref[:, 0:1]
    w1 = ew_ref[:, 1:2]
    moe = jnp.concatenate([w0 * a0 + w1 * a1, w0 * b0 + w1 * b1], axis=1)
    o_ref[0] = x1_ref[0] + mod_ref[0][5:6] * moe


def _final(x1, y0, y1, ew, mod, tm):
    b, t, d = x1.shape
    nt = t // tm
    flat = lambda i, j: (i * nt + j, 0)
    return pl.pallas_call(
        _final_kernel,
        out_shape=jax.ShapeDtypeStruct((b, t, d), F32),
        grid=(b, nt),
        in_specs=[pl.BlockSpec((1, tm, d), lambda i, j: (i, j, 0)),
                  pl.BlockSpec((tm, HALF), flat),
                  pl.BlockSpec((tm, HALF), flat),
                  pl.BlockSpec((tm, 8), flat),
                  pl.BlockSpec((1, 6, d), lambda i, j: (i, 0, 0))],
        out_specs=pl.BlockSpec((1, tm, d), lambda i, j: (i, j, 0)),
        compiler_params=_cparams(("arbitrary", "arbitrary")),
        name="final",
    )(x1, y0, y1, ew, mod)


def _sc_window(rows_per_worker):
    for w in range(SC_MAX_WINDOW, 7, -8):
        if rows_per_worker % w == 0:
            return w
    raise ValueError(f"no SparseCore window divides {rows_per_worker} rows per worker")


def _sc_split(idx):
    n = idx.shape[0]
    per = n // SC_WORKERS
    assert per * SC_WORKERS == n
    win = _sc_window(per)
    return idx.reshape(SC_WORKERS, per // win, win), per // win, win


def _sc_worker_id():
    return lax.axis_index("s") * SC_CORES + lax.axis_index("c")


def _dispatch_rows(h2_groups, dest_groups, n_rows):
    splits = [(_sc_split(d[:, 0]), _sc_split(d[:, 1])) for d in dest_groups]
    ng = len(h2_groups)
    scratch = []
    for (_, _, win), _ in splits:
        scratch += [pltpu.VMEM((win,), I32), pltpu.VMEM((win,), I32), pltpu.VMEM((win, HALF), I32)]

    @functools.partial(
        pl.kernel,
        mesh=plsc.VectorSubcoreMesh(core_axis_name="c", subcore_axis_name="s"),
        out_type=jax.ShapeDtypeStruct((n_rows, HALF), I32),
        scratch_types=scratch,
        name="sc_dispatch",
    )
    def k(*refs):
        x_refs, idx_refs, o_hbm, bufs = refs[:ng], refs[ng:3 * ng], refs[3 * ng], refs[3 * ng + 1:]
        wid = _sc_worker_id()
        for g in range(ng):
            (_, nwin, win), _ = splits[g]
            x_hbm, d0_hbm, d1_hbm = x_refs[g], idx_refs[2 * g], idx_refs[2 * g + 1]
            i0_v, i1_v, rows_v = bufs[3 * g:3 * g + 3]

            @pl.loop(0, nwin)
            def _(j, nwin=nwin, win=win, x_hbm=x_hbm, d0_hbm=d0_hbm, d1_hbm=d1_hbm,
                  i0_v=i0_v, i1_v=i1_v, rows_v=rows_v):
                base = pl.multiple_of((wid * nwin + j) * win, 8)
                pltpu.sync_copy(d0_hbm.at[wid, j], i0_v)
                pltpu.sync_copy(d1_hbm.at[wid, j], i1_v)
                pltpu.sync_copy(x_hbm.at[pl.ds(base, win)], rows_v)
                pltpu.sync_copy(rows_v, o_hbm.at[i0_v])
                pltpu.sync_copy(rows_v, o_hbm.at[i1_v])

    idx_args = []
    for (s0, s1) in splits:
        idx_args += [s0[0], s1[0]]
    return k(*h2_groups, *idx_args)


def _collect_rows(ys, dest_groups):
    splits = [(_sc_split(d[:, 0]), _sc_split(d[:, 1])) for d in dest_groups]
    ng = len(dest_groups)
    outs, scratch = [], []
    for d, ((_, _, win), _) in zip(dest_groups, splits):
        o = jax.ShapeDtypeStruct((d.shape[0], HALF), I32)
        outs += [o, o]
        scratch += [pltpu.VMEM((win,), I32), pltpu.VMEM((win, HALF), I32)]

    @functools.partial(
        pl.kernel,
        mesh=plsc.VectorSubcoreMesh(core_axis_name="c", subcore_axis_name="s"),
        out_type=tuple(outs),
        scratch_types=scratch,
        name="sc_collect",
    )
    def k(*refs):
        ys_hbm, idx_refs, out_refs, bufs = refs[0], refs[1:1 + 2 * ng], refs[1 + 2 * ng:1 + 4 * ng], refs[1 + 4 * ng:]
        wid = _sc_worker_id()
        for g in range(ng):
            (_, nwin, win), _ = splits[g]
            i_v, rows_v = bufs[2 * g:2 * g + 2]
            for kk in range(2):
                d_hbm, y_hbm = idx_refs[2 * g + kk], out_refs[2 * g + kk]

                @pl.loop(0, nwin)
                def _(j, nwin=nwin, win=win, d_hbm=d_hbm, y_hbm=y_hbm, i_v=i_v, rows_v=rows_v):
                    base = pl.multiple_of((wid * nwin + j) * win, 8)
                    pltpu.sync_copy(d_hbm.at[wid, j], i_v)
                    pltpu.sync_copy(ys_hbm.at[i_v], rows_v)
                    pltpu.sync_copy(rows_v, y_hbm.at[pl.ds(base, win)])

    idx_args = []
    for (s0, s1) in splits:
        idx_args += [s0[0], s1[0]]
    res = k(ys, *idx_args)
    return [(res[2 * g], res[2 * g + 1]) for g in range(ng)]


def _t5_bucket(rel):
    half = N_BUCKETS // 2
    max_exact = half // 2
    n = jnp.abs(rel)
    far = max_exact + (jnp.log(jnp.maximum(n, 1).astype(F32) / max_exact)
                       / math.log(MAX_DISTANCE / max_exact) * (half - max_exact)).astype(I32)
    far = jnp.minimum(far, half - 1)
    return jnp.where(rel > 0, half, 0) + jnp.where(n < max_exact, n, far)


def _bias_table(rel_bias, cq, nq, no_history):
    nk = WINDOW + cq
    j = jnp.arange(KEY_WIN)[:, None]
    c = jnp.arange(UNIT_Q)[None, :]
    jj = j - (c // cq) * cq
    valid = (jj >= 0) & (jj < nk) & (c < nq)
    if no_history:
        valid = valid & (j >= WINDOW)
    rel = jj - WINDOW - (c % cq)
    onehot = (_t5_bucket(rel)[:, :, None] == jnp.arange(N_BUCKETS)).astype(F32)
    bias = jnp.einsum("jcb,bh->jch", onehot, rel_bias.astype(F32), precision=lax.Precision.HIGHEST)
    bias = jnp.where(valid[:, :, None], bias, -jnp.inf)
    bias = jnp.transpose(bias.reshape(KEY_WIN, UNIT_Q, N_KV, GROUP), (2, 0, 3, 1))
    return bias.reshape(N_KV, KEY_WIN, GROUP * UNIT_Q)


def _sink_table(sinks):
    s = sinks.astype(F32).reshape(N_KV, 1, GROUP, 1)
    return jnp.broadcast_to(s, (N_KV, 1, GROUP, UNIT_Q)).reshape(N_KV, 1, GROUP * UNIT_Q)


def kernel(x_prompt, x_sample, state_attn_k, state_attn_v, state_conv, c_prompt, c_sample,
           rel_bias, w_ada, b_ada, norm1_w, w_in, q_norm_w, k_norm_w, attn_sinks, conv_w,
           w_out, norm2_w, w_router_group, b_router_group, w_router_expert, b_router_expert,
           w_gate, w_up, w_down):
    depth = w_ada.shape[0]
    assert depth == 1
    bp, tp, d = x_prompt.shape
    bs, ts, _ = x_sample.shape
    n_p, n_s = bp * tp, bs * ts
    n_tok = n_p + n_s
    l = 0

    wi = w_in[l]
    qw, kw, vw, rest = wi[:, :d], wi[:, d:d + KV_COLS], wi[:, d + KV_COLS:d + 2 * KV_COLS], wi[:, d + 2 * KV_COLS:]
    w_in_b = jnp.concatenate([qw, rest, kw, vw], axis=1).astype(BF16)
    w_out_b = w_out[l].astype(BF16)
    w_r = jnp.concatenate([w_router_group[l],
                           jnp.transpose(w_router_expert[l], (1, 0, 2)).reshape(d, N_EXPERTS),
                           jnp.zeros((d, LANES - N_GROUPS - N_EXPERTS), F32)], axis=1)
    w_r_hi = lax.reduce_precision(w_r, exponent_bits=8, mantissa_bits=7)
    w_r = jnp.concatenate([w_r_hi.astype(BF16), (w_r - w_r_hi).astype(BF16)], axis=1)
    b_r = jnp.concatenate([b_router_group[l], b_router_expert[l].reshape(-1),
                           jnp.zeros((LANES - N_GROUPS - N_EXPERTS,), F32)]).reshape(1, LANES)
    knw = jnp.tile(k_norm_w[l], N_KV).reshape(1, KV_COLS)
    qsc = jnp.tile(q_norm_w[l] * (HEAD_DIM ** -0.5), N_KV).reshape(1, KV_COLS)
    n1w = norm1_w[l].reshape(1, d)
    n2w = norm2_w[l].reshape(1, d)

    mod = _ada(jnp.concatenate([c_prompt, c_sample], axis=0), w_ada[l], b_ada[l]).reshape(bp + bs, 6, d)
    mod_p, mod_s = mod[:bp], mod[bp:]

    proj_s = _inproj(x_sample, mod_s, n1w, w_in_b, ts)
    sink_tab = _sink_table(attn_sinks[l])
    bias_p = jnp.concatenate([_bias_table(rel_bias, CHUNK, UNIT_Q, False),
                              _bias_table(rel_bias, CHUNK, UNIT_Q, True)], axis=0)
    merged_p, k_p, v_p, c_p = _front(x_prompt, mod_p, n1w, w_in_b, knw, qsc, bias_p, sink_tab, conv_w[l],
                                     MIX_TILE, UNIT_Q)
    state = (state_attn_k[l].reshape(bs, WINDOW, KV_COLS), state_attn_v[l].reshape(bs, WINDOW, KV_COLS),
             state_conv[l])
    merged_s, k_s, v_s, c_s = _mixer(proj_s, knw, qsc, _bias_table(rel_bias, ts, ts, False), sink_tab, conv_w[l],
                                     state, ts, ts)

    x1_p, h2_p, eid_p, ew_p = _outproj(merged_p, x_prompt, mod_p, w_out_b, n2w, w_r, b_r, ROW_TILE)
    x1_s, h2_s, eid_s, ew_s = _outproj(merged_s, x_sample, mod_s, w_out_b, n2w, w_r, b_r, ts)

    dest, totals = _rank(jnp.concatenate([eid_p, eid_s], axis=0), EXPERT_BLOCK)
    dests = [dest[:n_p], dest[n_p:]]
    nblk = (totals[0, :N_EXPERTS] + EXPERT_BLOCK - 1) // EXPERT_BLOCK
    blk_end = jnp.cumsum(nblk)
    nb_max = -(-2 * n_tok // EXPERT_BLOCK) + N_EXPERTS
    block_e = jnp.minimum(jnp.sum(blk_end[None, :] <= jnp.arange(nb_max)[:, None], axis=1), N_EXPERTS - 1).astype(I32)
    n_valid = blk_end[-1:].astype(I32)

    xs = _dispatch_rows([h2_p, h2_s], dests, nb_max * EXPERT_BLOCK)
    ys = _experts(xs, block_e, n_valid, w_gate[l], w_up[l], w_down[l], EXPERT_BLOCK)
    (y0_p, y1_p), (y0_s, y1_s) = _collect_rows(ys, dests)

    y_p = _final(x1_p, y0_p, y1_p, ew_p, mod_p, ROW_TILE)
    y_s = _final(x1_s, y0_s, y1_s, ew_s, mod_s, ts)

    kv_shape = (1, -1, WINDOW, N_KV, HEAD_DIM)
    return (y_p, y_s, k_p.reshape(kv_shape), v_p.reshape(kv_shape), c_p[None],
            k_s.reshape(kv_shape), v_s.reshape(kv_shape), c_s[None])
```

```python
import functools
import math

import numpy as np
import jax
import jax.numpy as jnp
from jax import lax
from jax.experimental import pallas as pl
from jax.experimental.pallas import tpu as pltpu
from jax.experimental.pallas import tpu_sc as plsc

F32 = jnp.float32
BF16 = jnp.bfloat16
I32 = jnp.int32

D_MODEL = 1024
HEAD_DIM = 64
N_HEADS = 16
N_KV = 4
GROUP = 4
CHUNK = 64
WINDOW = 128
N_BUCKETS = 32
MAX_DISTANCE = 128
N_GROUPS = 8
EPG = 8
N_EXPERTS = 64
D_EXPERT = 512
RMS_EPS = 1e-6
KV_COLS = N_KV * HEAD_DIM
IN_COLS = 6 * D_MODEL + 2 * KV_COLS
HALF = D_MODEL // 2
LANES = 128

VMEM_LIMIT = 56 * 1024 * 1024
INPROJ_TN = 512
ROW_TILE = 512
MIX_TILE = 512
UNIT_Q = 2 * CHUNK
KEY_WIN = WINDOW + UNIT_Q
PROJ_CHUNK = 256
EXPERT_BLOCK = 256
RANK_TILE = 512
SC_CORES = 2
SC_SUBCORES = 16
SC_WORKERS = SC_CORES * SC_SUBCORES
SC_MAX_WINDOW = 128


def _cparams(sem):
    return pltpu.CompilerParams(dimension_semantics=sem, vmem_limit_bytes=VMEM_LIMIT)


def _split_bf16(a):
    hi = a.astype(BF16)
    lo = (a - hi.astype(F32)).astype(BF16)
    return hi, lo


def _dot3(a, b):
    ah, al = _split_bf16(a)
    bh, bl = _split_bf16(b)
    d = functools.partial(jnp.dot, preferred_element_type=F32)
    return d(ah, bh) + (d(ah, bl) + d(al, bh))


def _pack_pairs(y):
    a = lax.bitcast_convert_type(y[:, :HALF].astype(BF16).astype(F32), I32)
    b = lax.bitcast_convert_type(y[:, HALF:].astype(BF16).astype(F32), I32)
    return a | lax.shift_right_logical(b, jnp.int32(16))


def _unpack_pairs(w):
    a = lax.bitcast_convert_type(w & jnp.int32(-65536), F32)
    b = lax.bitcast_convert_type(lax.shift_left(w, jnp.int32(16)), F32)
    return a, b


def _ada_kernel(c_ref, w_ref, b_ref, o_ref):
    c = c_ref[...]
    s = c * jax.nn.sigmoid(c)
    o_ref[...] = _dot3(s, w_ref[...]) + b_ref[...]


def _ada(c_all, w_ada, b_ada):
    r, d = c_all.shape
    n = w_ada.shape[1]
    tn = 1024
    return pl.pallas_call(
        _ada_kernel,
        out_shape=jax.ShapeDtypeStruct((r, n), F32),
        grid=(n // tn,),
        in_specs=[pl.BlockSpec((r, d), lambda j: (0, 0)),
                  pl.BlockSpec((d, tn), lambda j: (0, j)),
                  pl.BlockSpec((1, tn), lambda j: (0, j))],
        out_specs=pl.BlockSpec((r, tn), lambda j: (0, j)),
        compiler_params=_cparams(("arbitrary",)),
        name="ada",
    )(c_all, w_ada, b_ada.reshape(1, n))


def _inproj_kernel(x_ref, mod_ref, nw_ref, w_ref, o_ref):
    x = x_ref[0]
    mod = mod_ref[0]
    h = x * lax.rsqrt(jnp.mean(x * x, axis=-1, keepdims=True) + RMS_EPS) * nw_ref[...]
    h = h * (1.0 + mod[1:2]) + mod[0:1]
    hb = h.astype(BF16)
    for j in range(IN_COLS // INPROJ_TN):
        sl = slice(j * INPROJ_TN, (j + 1) * INPROJ_TN)
        o_ref[0, :, sl] = jnp.dot(hb, w_ref[:, sl], preferred_element_type=F32).astype(BF16)


def _inproj(x, mod, nw, w_in_b, tm):
    b, t, d = x.shape
    return pl.pallas_call(
        _inproj_kernel,
        out_shape=jax.ShapeDtypeStruct((b, t, IN_COLS), BF16),
        grid=(b, t // tm),
        in_specs=[pl.BlockSpec((1, tm, d), lambda i, j: (i, j, 0)),
                  pl.BlockSpec((1, 6, d), lambda i, j: (i, 0, 0)),
                  pl.BlockSpec((1, d), lambda i, j: (0, 0)),
                  pl.BlockSpec((d, IN_COLS), lambda i, j: (0, 0), pipeline_mode=pl.Buffered(1))],
        out_specs=pl.BlockSpec((1, tm, IN_COLS), lambda i, j: (i, j, 0)),
        compiler_params=_cparams(("arbitrary", "arbitrary")),
        name="inproj",
    )(x, mod, nw, w_in_b)


def _head_inv_rms(xf, bd):
    hi, lo = _split_bf16(xf * xf)
    ssq = jnp.dot(hi, bd, preferred_element_type=F32) + jnp.dot(lo, bd, preferred_element_type=F32)
    return lax.rsqrt(ssq * (1.0 / HEAD_DIM) + RMS_EPS)


def _mixer_body(q_ref, k_ref, v_ref, bg_ref, c_ref, u_ref, ga_ref, gc_ref,
                knw_ref, qsc_ref, bd_ref, bias_ref, sink_ref, cw_ref,
                kpast_ref, vpast_ref, cpast_ref, upast_ref,
                merged_ref, knew_ref, vnew_ref, cnew_ref,
                kq_buf, vt_buf, attn_buf, *, tq, nq, stateful):
    hb = WINDOW
    pw = UNIT_Q
    t = pl.program_id(1)
    bd = bd_ref[...]

    k = k_ref[0].astype(F32)
    kn = k * _head_inv_rms(k, bd) * knw_ref[...]
    kq = (kn * qsc_ref[...]).astype(BF16)
    vb = v_ref[0]
    vt = vb.astype(F32).T.astype(BF16)

    if stateful:
        kp = kpast_ref[0]
        vp = vpast_ref[0]
        for kv in range(N_KV):
            kq_buf[kv, hb + tq:] = jnp.zeros((KEY_WIN - hb - tq, HEAD_DIM), BF16)
        vt_buf[:, hb + tq:] = jnp.zeros((KV_COLS, KEY_WIN - hb - tq), BF16)
        u_hist = jnp.concatenate([jnp.zeros((6, D_MODEL), F32), cpast_ref[0]], axis=0)
        knew_ref[0] = jnp.concatenate([kp[tq:], kn], axis=0)
        vnew_ref[0] = jnp.concatenate([vp[tq:], vb.astype(F32)], axis=0)
    else:
        kraw = kpast_ref[0].astype(F32)
        kp = kraw * _head_inv_rms(kraw, bd) * knw_ref[...]
        vp = vpast_ref[0].astype(F32)
        u_hist = jnp.where(t == 0, 0.0, cpast_ref[0].astype(F32) * upast_ref[0].astype(F32))
        knew_ref[0] = kn[tq - hb:]
        vnew_ref[0] = vb[tq - hb:].astype(F32)
    _fill_keys(kq_buf, vt_buf, kp, vp, kq, vt, qsc_ref, tq)
    q = q_ref[0]
    _attention_units(lambda kv: q[:, kv * KV_COLS:(kv + 1) * KV_COLS].astype(F32), kq_buf, vt_buf,
                     bias_ref, sink_ref, bd, attn_buf, tq, nq, None if stateful else t)
    cu = c_ref[0].astype(F32) * u_ref[0].astype(F32)
    conv, u_all = _short_conv(u_hist, cu, cw_ref[...], tq)
    cnew_ref[0] = u_all[tq + 6:tq + 8]
    merged = (jax.nn.sigmoid(ga_ref[0].astype(F32)) * attn_buf[...]
              + jax.nn.sigmoid(gc_ref[0].astype(F32)) * (bg_ref[0].astype(F32) * conv))
    merged_ref[0] = merged.astype(BF16)


def _fill_keys(kq_buf, vt_buf, kp, vp, kq, vt, qsc_ref, tq):
    hb = WINDOW
    kqp = (kp * qsc_ref[...]).astype(BF16)
    for kv in range(N_KV):
        kq_buf[kv, 0:hb] = kqp[:, kv * HEAD_DIM:(kv + 1) * HEAD_DIM]
        kq_buf[kv, hb:hb + tq] = kq[:, kv * HEAD_DIM:(kv + 1) * HEAD_DIM]
    vt_buf[:, 0:hb] = vp.T.astype(BF16)
    vt_buf[:, hb:hb + tq] = vt


def _attention_units(q_group, kq_buf, vt_buf, bias_ref, sink_ref, bd, attn_buf, tq, nq, t_first, after_unit=None):
    pw = UNIT_Q
    for kv in range(N_KV):
        qf = q_group(kv)
        qn = (qf * _head_inv_rms(qf, bd)).astype(BF16)
        for u in range(tq // nq):
            r0 = u * nq
            parts = [qn[r0:r0 + nq, g * HEAD_DIM:(g + 1) * HEAD_DIM] for g in range(GROUP)]
            if nq < pw:
                zpad = jnp.zeros((pw - nq, HEAD_DIM), BF16)
                parts = [x for p_ in parts for x in (p_, zpad)]
            qs = jnp.concatenate(parts, axis=0)
            kw = kq_buf[kv, r0:r0 + KEY_WIN]
            st = lax.dot_general(kw, qs, (((1,), (1,)), ((), ())), preferred_element_type=F32)
            if t_first is not None and u == 0:
                bias = jnp.where(t_first == 0, bias_ref[kv + N_KV], bias_ref[kv])
            else:
                bias = bias_ref[kv]
            st = st + bias
            sink = sink_ref[kv]
            m = jnp.maximum(jnp.max(st, axis=0, keepdims=True), sink)
            p = jnp.exp(st - m)
            den = jnp.sum(p, axis=0, keepdims=True) + jnp.exp(sink - m)
            ot = jnp.dot(vt_buf[kv * HEAD_DIM:(kv + 1) * HEAD_DIM, r0:r0 + KEY_WIN], p.astype(BF16),
                         preferred_element_type=F32) / den
            for gp in range(GROUP // 2):
                blk = jnp.concatenate([ot[:, (2 * gp) * pw:(2 * gp + 1) * pw],
                                       ot[:, (2 * gp + 1) * pw:(2 * gp + 2) * pw]], axis=0)
                c0 = (kv * GROUP + 2 * gp) * HEAD_DIM
                attn_buf[r0:r0 + nq, c0:c0 + 2 * HEAD_DIM] = blk.T[:nq]
            if after_unit is not None:
                after_unit(kv * (tq // nq) + u)


def _short_conv(u_hist, cu, cw, tq):
    u_all = jnp.concatenate([u_hist, cu], axis=0)
    conv = cw[0:1] * u_all[6:6 + tq] + cw[1:2] * u_all[7:7 + tq] + cw[2:3] * u_all[8:8 + tq]
    return conv, u_all


def _front_body(x_ref, xp_ref, mod_ref, nw_ref, w_ref, knw_ref, qsc_ref, bd_ref, bias_ref, sink_ref, cw_ref,
                merged_ref, knew_ref, vnew_ref, cnew_ref,
                kq_buf, vt_buf, u_buf, attn_buf, pbuf, *, tq, nq):
    hb = WINDOW
    d = D_MODEL
    t = pl.program_id(1)
    bd = bd_ref[...]
    mod = mod_ref[0]

    @pl.when(t == 0)
    def _():
        u_buf[0] = jnp.zeros((8, d), F32)

    def modnorm(x):
        h = x * lax.rsqrt(jnp.mean(x * x, axis=-1, keepdims=True) + RMS_EPS) * nw_ref[...]
        return (h * (1.0 + mod[1:2]) + mod[0:1]).astype(BF16)

    def proj(hrows, c0, width):
        return jnp.dot(hrows, w_ref[:, c0:c0 + width], preferred_element_type=F32)

    hbf = modnorm(x_ref[0])
    kv_t = proj(hbf, 6 * d, 2 * KV_COLS)
    kv_p = proj(modnorm(xp_ref[0]), 6 * d, 2 * KV_COLS)
    k, v = kv_t[:, :KV_COLS], kv_t[:, KV_COLS:]
    kn = k * _head_inv_rms(k, bd) * knw_ref[...]
    kq = (kn * qsc_ref[...]).astype(BF16)
    kraw = kv_p[:, :KV_COLS]
    kp = kraw * _head_inv_rms(kraw, bd) * knw_ref[...]
    knew_ref[0] = kn[tq - hb:]
    vnew_ref[0] = v[tq - hb:]
    _fill_keys(kq_buf, vt_buf, kp, kv_p[:, KV_COLS:], kq, v.T.astype(BF16), qsc_ref, tq)

    n_units = N_KV * (tq // nq)
    q_chunks = [[g * KV_COLS + c for c in range(0, KV_COLS, PROJ_CHUNK)] for g in range(N_KV)]
    rest_chunks = list(range(d, 6 * d, PROJ_CHUNK))

    def issue(c0):
        pbuf[:, c0:c0 + PROJ_CHUNK] = proj(hbf, c0, PROJ_CHUNK)

    for c0 in q_chunks[0]:
        issue(c0)
    n_rest = len(rest_chunks)

    def after_unit(i):
        kv, u = divmod(i, tq // nq)
        if u == 0 and kv + 1 < N_KV:
            for c0 in q_chunks[kv + 1]:
                issue(c0)
        for c0 in rest_chunks[i * n_rest // n_units:(i + 1) * n_rest // n_units]:
            issue(c0)

    _attention_units(lambda kv: pbuf[:, kv * KV_COLS:(kv + 1) * KV_COLS], kq_buf, vt_buf,
                     bias_ref, sink_ref, bd, attn_buf, tq, nq, t, after_unit)

    cu = pbuf[:, 2 * d:3 * d] * pbuf[:, 3 * d:4 * d]
    conv, u_all = _short_conv(u_buf[t % 2], cu, cw_ref[...], tq)
    cnew_ref[0] = u_all[tq + 6:tq + 8]
    u_buf[(t + 1) % 2] = u_all[tq:tq + 8]
    merged = (jax.nn.sigmoid(pbuf[:, 4 * d:5 * d]) * attn_buf[...]
              + jax.nn.sigmoid(pbuf[:, 5 * d:6 * d]) * (pbuf[:, d:2 * d] * conv))
    merged_ref[0] = merged.astype(BF16)


def _front(x, mod, nw, w_in_b, knw, qsc, bias_tab, sink_tab, conv_w, tq, nq):
    b, t, d = x.shape
    r = np.arange(KV_COLS) // HEAD_DIM
    bd = jnp.asarray((r[:, None] == r[None, :]).astype(np.float32), BF16)
    const2 = lambda shp: pl.BlockSpec(shp, lambda i, s: (0, 0))
    const3 = lambda shp: pl.BlockSpec(shp, lambda i, s: (0, 0, 0))
    per_b = lambda shp: pl.BlockSpec(shp, lambda i, s: (i, 0, 0))
    kw_ = tq // WINDOW
    in_specs = [pl.BlockSpec((1, tq, d), lambda i, s: (i, s, 0)),
                pl.BlockSpec((1, WINDOW, d), lambda i, s: (i, jnp.maximum(s * kw_ - 1, 0), 0)),
                per_b((1, 6, d)), const2((1, d)),
                pl.BlockSpec((d, IN_COLS), lambda i, s: (0, 0), pipeline_mode=pl.Buffered(1)),
                const2((1, KV_COLS)), const2((1, KV_COLS)), const2((KV_COLS, KV_COLS)),
                const3(bias_tab.shape), const3(sink_tab.shape), const2((3, d))]
    out_shape = (jax.ShapeDtypeStruct((b, t, d), BF16),
                 jax.ShapeDtypeStruct((b, WINDOW, KV_COLS), F32),
                 jax.ShapeDtypeStruct((b, WINDOW, KV_COLS), F32),
                 jax.ShapeDtypeStruct((b, 2, d), F32))
    out_specs = (pl.BlockSpec((1, tq, d), lambda i, s: (i, s, 0)),
                 per_b((1, WINDOW, KV_COLS)), per_b((1, WINDOW, KV_COLS)), per_b((1, 2, d)))
    return pl.pallas_call(
        functools.partial(_front_body, tq=tq, nq=nq),
        out_shape=out_shape,
        grid=(b, t // tq),
        in_specs=in_specs,
        out_specs=out_specs,
        scratch_shapes=[pltpu.VMEM((N_KV, WINDOW + tq, HEAD_DIM), BF16),
                        pltpu.VMEM((KV_COLS, WINDOW + tq), BF16),
                        pltpu.VMEM((2, 8, d), F32),
                        pltpu.VMEM((tq, d), F32),
                        pltpu.VMEM((tq, 6 * d), F32)],
        compiler_params=_cparams(("arbitrary", "arbitrary")),
        name="front",
    )(x, x, mod, nw, w_in_b, knw, qsc, bd, bias_tab, sink_tab, conv_w)


def _mixer(proj, knw, qsc, bias_tab, sink_tab, conv_w, state, tq, nq):
    b, t, _ = proj.shape
    d = D_MODEL
    stateful = state is not None
    key_rows = max(WINDOW + tq, KEY_WIN)
    r = np.arange(KV_COLS) // HEAD_DIM
    bd = jnp.asarray((r[:, None] == r[None, :]).astype(np.float32), BF16)
    wide = lambda j: pl.BlockSpec((1, tq, d), lambda i, s, j=j: (i, s, j))
    kvspec = lambda j: pl.BlockSpec((1, tq, KV_COLS), lambda i, s, j=j: (i, s, j))
    const2 = lambda shp: pl.BlockSpec(shp, lambda i, s: (0, 0))
    const3 = lambda shp: pl.BlockSpec(shp, lambda i, s: (0, 0, 0))
    per_b = lambda shp: pl.BlockSpec(shp, lambda i, s: (i, 0, 0))
    kvblk = 6 * d // KV_COLS
    if stateful:
        hist_specs = [per_b((1, WINDOW, KV_COLS)), per_b((1, WINDOW, KV_COLS)), per_b((1, 2, d)), per_b((1, 2, d))]
        hist_args = [state[0], state[1], state[2], state[2]]
    else:
        kw_ = tq // WINDOW
        prev_kv = lambda j: pl.BlockSpec((1, WINDOW, KV_COLS),
                                         lambda i, s, j=j: (i, jnp.maximum(s * kw_ - 1, 0), j))
        prev8 = lambda j: pl.BlockSpec((1, 8, d), lambda i, s, j=j: (i, jnp.maximum(s * (tq // 8) - 1, 0), j))
        hist_specs = [prev_kv(kvblk), prev_kv(kvblk + 1), prev8(2), prev8(3)]
        hist_args = [proj, proj, proj, proj]
    in_specs = [wide(0), kvspec(kvblk), kvspec(kvblk + 1), wide(1), wide(2), wide(3), wide(4), wide(5),
                const2((1, KV_COLS)), const2((1, KV_COLS)), const2((KV_COLS, KV_COLS)),
                const3(bias_tab.shape), const3(sink_tab.shape), const2((3, d))] + hist_specs
    out_shape = (jax.ShapeDtypeStruct((b, t, d), BF16),
                 jax.ShapeDtypeStruct((b, WINDOW, KV_COLS), F32),
                 jax.ShapeDtypeStruct((b, WINDOW, KV_COLS), F32),
                 jax.ShapeDtypeStruct((b, 2, d), F32))
    out_specs = (pl.BlockSpec((1, tq, d), lambda i, s: (i, s, 0)),
                 per_b((1, WINDOW, KV_COLS)), per_b((1, WINDOW, KV_COLS)), per_b((1, 2, d)))
    return pl.pallas_call(
        functools.partial(_mixer_body, tq=tq, nq=nq, stateful=stateful),
        out_shape=out_shape,
        grid=(b, t // tq),
        in_specs=in_specs,
        out_specs=out_specs,
        scratch_shapes=[pltpu.VMEM((N_KV, key_rows, HEAD_DIM), BF16),
                        pltpu.VMEM((KV_COLS, key_rows), BF16),
                        pltpu.VMEM((tq, d), F32)],
        compiler_params=_cparams(("arbitrary", "arbitrary")),
        name="mixer_state" if stateful else "mixer",
    )(proj, proj, proj, proj, proj, proj, proj, proj, knw, qsc, bd, bias_tab, sink_tab, conv_w, *hist_args)


def _route(logits):
    lane = lax.broadcasted_iota(I32, logits.shape, 1)
    neg = -jnp.inf
    big = jnp.int32(1 << 20)
    gl = jnp.where(lane < N_GROUPS, logits, neg)
    gmax = jnp.max(gl, axis=-1, keepdims=True)
    g_idx = jnp.min(jnp.where(gl == gmax, lane, big), axis=-1, keepdims=True)
    g_w = 1.0 / jnp.sum(jnp.exp(gl - gmax), axis=-1, keepdims=True)
    lo = N_GROUPS + g_idx * EPG
    el = jnp.where((lane >= lo) & (lane < lo + EPG), logits, neg)
    m1 = jnp.max(el, axis=-1, keepdims=True)
    i1 = jnp.min(jnp.where(el == m1, lane, big), axis=-1, keepdims=True)
    el2 = jnp.where(lane == i1, neg, el)
    m2 = jnp.max(el2, axis=-1, keepdims=True)
    i2 = jnp.min(jnp.where(el2 == m2, lane, big), axis=-1, keepdims=True)
    r = jnp.exp(m2 - m1)
    w1 = 1.0 / (1.0 + r)
    w2 = r / (1.0 + r)
    return i1 - N_GROUPS, i2 - N_GROUPS, g_w * w1, g_w * w2


def _outproj_kernel(m_ref, x_ref, mod_ref, wo_ref, nw_ref, wr_ref, br_ref,
                    x1_ref, h2_ref, eid_ref, ew_ref):
    mod = mod_ref[0]
    mix = jnp.dot(m_ref[0], wo_ref[...], preferred_element_type=F32)
    x1 = x_ref[0] + mod[2:3] * mix
    x1_ref[0] = x1
    h = x1 * lax.rsqrt(jnp.mean(x1 * x1, axis=-1, keepdims=True) + RMS_EPS) * nw_ref[...]
    h = h * (1.0 + mod[4:5]) + mod[3:4]
    h2_ref[...] = _pack_pairs(h)
    tm = h.shape[0]
    h_hi, h_lo = _split_bf16(h)
    prod = jnp.dot(jnp.concatenate([h_hi, h_lo], axis=0), wr_ref[...], preferred_element_type=F32)
    logits = prod[:tm, :LANES] + (prod[:tm, LANES:] + prod[tm:, :LANES]) + br_ref[...]
    e1, e2, w1, w2 = _route(logits)
    lane8 = lax.broadcasted_iota(I32, (h.shape[0], 8), 1)
    eid_ref[...] = jnp.where(lane8 == 0, e1, jnp.where(lane8 == 1, e2, 0))
    ew_ref[...] = jnp.where(lane8 == 0, w1, jnp.where(lane8 == 1, w2, 0.0))


def _outproj(merged, x, mod, w_out_b, nw, w_r, b_r, tm):
    b, t, d = x.shape
    nt = t // tm
    flat = lambda i, j: (i * nt + j, 0)
    return pl.pallas_call(
        _outproj_kernel,
        out_shape=(jax.ShapeDtypeStruct((b, t, d), F32),
                   jax.ShapeDtypeStruct((b * t, HALF), I32),
                   jax.ShapeDtypeStruct((b * t, 8), I32),
                   jax.ShapeDtypeStruct((b * t, 8), F32)),
        grid=(b, nt),
        in_specs=[pl.BlockSpec((1, tm, d), lambda i, j: (i, j, 0)),
                  pl.BlockSpec((1, tm, d), lambda i, j: (i, j, 0)),
                  pl.BlockSpec((1, 6, d), lambda i, j: (i, 0, 0)),
                  pl.BlockSpec((d, d), lambda i, j: (0, 0)),
                  pl.BlockSpec((1, d), lambda i, j: (0, 0)),
                  pl.BlockSpec((d, 2 * LANES), lambda i, j: (0, 0)),
                  pl.BlockSpec((1, LANES), lambda i, j: (0, 0))],
        out_specs=(pl.BlockSpec((1, tm, d), lambda i, j: (i, j, 0)),
                   pl.BlockSpec((tm, HALF), flat),
                   pl.BlockSpec((tm, 8), flat),
                   pl.BlockSpec((tm, 8), flat)),
        compiler_params=_cparams(("arbitrary", "arbitrary")),
        name="outproj",
    )(merged, x, mod, w_out_b, nw, w_r, b_r)


def _rank_kernel(eid_ref, tri_ref, upper_ref, dest_ref, tot_ref, cnt, starts, *, block):
    ph = pl.program_id(0)
    i = pl.program_id(1)
    tm = eid_ref.shape[0]
    lane = lax.broadcasted_iota(I32, (tm, LANES), 1)
    e0 = eid_ref[:, 0:1]
    e1 = eid_ref[:, 1:2]
    hot0 = lane == e0
    hot1 = lane == e1
    onehot = jnp.where(hot0 | hot1, 1.0, 0.0)
    colsum = jnp.sum(onehot, axis=0, keepdims=True)

    @pl.when((ph == 0) & (i == 0))
    def _():
        cnt[...] = jnp.zeros_like(cnt)

    @pl.when(ph == 0)
    def _():
        cnt[0:1] = cnt[0:1] + colsum

    @pl.when((ph == 1) & (i == 0))
    def _():
        tot = cnt[0:1]
        tot_ref[...] = tot.astype(I32)
        nblk = jnp.floor((tot + (block - 1)) * (1.0 / block))
        hi = jnp.floor(nblk * (1.0 / 16.0))
        lo = nblk - hi * 16.0
        up = upper_ref[...]
        excl = (jnp.dot(jnp.broadcast_to(hi, (8, LANES)).astype(BF16), up, preferred_element_type=F32) * 16.0
                + jnp.dot(jnp.broadcast_to(lo, (8, LANES)).astype(BF16), up, preferred_element_type=F32))
        starts[...] = excl * float(block)
        cnt[...] = jnp.zeros_like(cnt)

    @pl.when(ph == 1)
    def _():
        prefix = jnp.dot(tri_ref[...], onehot.astype(BF16), preferred_element_type=F32)
        pos = prefix + (starts[0:1] + cnt[0:1])
        d0 = jnp.sum(jnp.where(hot0, pos, 0.0), axis=-1, keepdims=True).astype(I32)
        d1 = jnp.sum(jnp.where(hot1, pos, 0.0), axis=-1, keepdims=True).astype(I32)
        lane8 = lax.broadcasted_iota(I32, (tm, 8), 1)
        dest_ref[...] = jnp.where(lane8 == 0, d0, jnp.where(lane8 == 1, d1, 0))
        cnt[0:1] = cnt[0:1] + colsum


def _rank(eid, block):
    n = eid.shape[0]
    tm = math.gcd(n, RANK_TILE)
    r = np.arange(tm)
    tri = jnp.asarray((r[:, None] > r[None, :]).astype(np.float32), BF16)
    l = np.arange(LANES)
    upper = jnp.asarray(((l[:, None] < l[None, :]) & (l[None, :] < N_EXPERTS)).astype(np.float32), BF16)
    return pl.pallas_call(
        functools.partial(_rank_kernel, block=block),
        out_shape=(jax.ShapeDtypeStruct((n, 8), I32), jax.ShapeDtypeStruct((1, LANES), I32)),
        grid=(2, n // tm),
        in_specs=[pl.BlockSpec((tm, 8), lambda p, i: (i, 0)),
                  pl.BlockSpec((tm, tm), lambda p, i: (0, 0)),
                  pl.BlockSpec((LANES, LANES), lambda p, i: (0, 0))],
        out_specs=(pl.BlockSpec((tm, 8), lambda p, i: (i * p, 0)),
                   pl.BlockSpec((1, LANES), lambda p, i: (0, 0))),
        scratch_shapes=[pltpu.VMEM((8, LANES), F32), pltpu.VMEM((8, LANES), F32)],
        compiler_params=_cparams(("arbitrary", "arbitrary")),
        name="rank",
    )(eid, tri, upper)


def _expert_kernel(start_ref, nblk_ref, xs_hbm, wg_ref, wu_ref, wd_ref, ys_hbm,
                   xbuf, ybuf, wg_s, wu_s, wd_s, sem_in, sem_out, *, block):
    e = pl.program_id(0)
    n = nblk_ref[e]
    base = start_ref[e]

    def in_copy(i, slot):
        rows = pl.ds(pl.multiple_of((base + i) * block, block), block)
        return pltpu.make_async_copy(xs_hbm.at[rows], xbuf.at[slot], sem_in.at[slot])

    def out_copy(i, slot):
        rows = pl.ds(pl.multiple_of((base + i) * block, block), block)
        return pltpu.make_async_copy(ybuf.at[slot], ys_hbm.at[rows], sem_out.at[slot])

    @pl.when(n > 0)
    def _():
        in_copy(0, 0).start()
        wg_s[...] = wg_ref[0].astype(BF16)
        wu_s[...] = wu_ref[0].astype(BF16)
        wd_s[...] = wd_ref[0].astype(BF16)

        def body(i, carry):
            slot = i % 2
            in_copy(i, slot).wait()

            @pl.when(i + 1 < n)
            def _():
                in_copy(i + 1, 1 - slot).start()

            @pl.when(i >= 2)
            def _():
                out_copy(i - 2, slot).wait()

            a, c = _unpack_pairs(xbuf[slot])
            x = jnp.concatenate([a.astype(BF16), c.astype(BF16)], axis=1)
            g = jnp.dot(x, wg_s[...], preferred_element_type=F32)
            u = jnp.dot(x, wu_s[...], preferred_element_type=F32)
            hmid = (g * jax.nn.sigmoid(g) * u).astype(BF16)
            ybuf[slot] = _pack_pairs(jnp.dot(hmid, wd_s[...], preferred_element_type=F32))
            out_copy(i, slot).start()
            return carry

        lax.fori_loop(0, n, body, 0)

        @pl.when(n >= 2)
        def _():
            out_copy(n - 2, n % 2).wait()
        out_copy(n - 1, (n - 1) % 2).wait()


def _experts(xs, start_blk, nblk, w_gate, w_up, w_down, block):
    n_rows = xs.shape[0]
    wblk = lambda e, st, nb: (e, 0, 0)
    grid_spec = pltpu.PrefetchScalarGridSpec(
        num_scalar_prefetch=2,
        grid=(N_EXPERTS,),
        in_specs=[pl.BlockSpec(memory_space=pl.ANY),
                  pl.BlockSpec((1, D_MODEL, D_EXPERT), wblk),
                  pl.BlockSpec((1, D_MODEL, D_EXPERT), wblk),
                  pl.BlockSpec((1, D_EXPERT, D_MODEL), wblk)],
        out_specs=pl.BlockSpec(memory_space=pl.ANY),
        scratch_shapes=[pltpu.VMEM((2, block, HALF), I32),
                        pltpu.VMEM((2, block, HALF), I32),
                        pltpu.VMEM((D_MODEL, D_EXPERT), BF16),
                        pltpu.VMEM((D_MODEL, D_EXPERT), BF16),
                        pltpu.VMEM((D_EXPERT, D_MODEL), BF16),
                        pltpu.SemaphoreType.DMA((2,)),
                        pltpu.SemaphoreType.DMA((2,))])
    return pl.pallas_call(
        functools.partial(_expert_kernel, block=block),
        out_shape=jax.ShapeDtypeStruct((n_rows, HALF), I32),
        grid_spec=grid_spec,
        compiler_params=_cparams(("arbitrary",)),
        name="experts",
    )(start_blk, nblk, xs, w_gate, w_up, w_down)


def _final_kernel(x1_ref, y0_ref, y1_ref, ew_ref, mod_ref, o_ref):
    a0, b0 = _unpack_pairs(y0_ref[...])
    a1, b1 = _unpack_pairs(y1_ref[...])
    w0 = ew_ref[:, 0:1]
    w1 = ew_ref[:, 1:2]
    moe = jnp.concatenate([w0 * a0 + w1 * a1, w0 * b0 + w1 * b1], axis=1)
    o_ref[0] = x1_ref[0] + mod_ref[0][5:6] * moe


def _final(x1, y0, y1, ew, mod, tm):
    b, t, d = x1.shape
    nt = t // tm
    flat = lambda i, j: (i * nt + j, 0)
    return pl.pallas_call(
        _final_kernel,
        out_shape=jax.ShapeDtypeStruct((b, t, d), F32),
        grid=(b, nt),
        in_specs=[pl.BlockSpec((1, tm, d), lambda i, j: (i, j, 0)),
                  pl.BlockSpec((tm, HALF), flat),
                  pl.BlockSpec((tm, HALF), flat),
                  pl.BlockSpec((tm, 8), flat),
                  pl.BlockSpec((1, 6, d), lambda i, j: (i, 0, 0))],
        out_specs=pl.BlockSpec((1, tm, d), lambda i, j: (i, j, 0)),
        compiler_params=_cparams(("arbitrary", "arbitrary")),
        name="final",
    )(x1, y0, y1, ew, mod)


def _sc_window(rows_per_worker):
    for w in range(SC_MAX_WINDOW, 7, -8):
        if rows_per_worker % w == 0:
            return w
    raise ValueError(f"no SparseCore window divides {rows_per_worker} rows per worker")


def _sc_split(idx):
    n = idx.shape[0]
    per = n // SC_WORKERS
    assert per * SC_WORKERS == n
    win = _sc_window(per)
    return idx.reshape(SC_WORKERS, per // win, win), per // win, win


def _sc_worker_id():
    return lax.axis_index("s") * SC_CORES + lax.axis_index("c")


def _dispatch_rows(h2_groups, dest_groups, n_rows):
    splits = [(_sc_split(d[:, 0]), _sc_split(d[:, 1])) for d in dest_groups]
    ng = len(h2_groups)
    scratch = []
    for (_, _, win), _ in splits:
        scratch += [pltpu.VMEM((win,), I32), pltpu.VMEM((win,), I32), pltpu.VMEM((win, HALF), I32)]

    @functools.partial(
        pl.kernel,
        mesh=plsc.VectorSubcoreMesh(core_axis_name="c", subcore_axis_name="s"),
        out_type=jax.ShapeDtypeStruct((n_rows, HALF), I32),
        scratch_types=scratch,
        name="sc_dispatch",
    )
    def k(*refs):
        x_refs, idx_refs, o_hbm, bufs = refs[:ng], refs[ng:3 * ng], refs[3 * ng], refs[3 * ng + 1:]
        wid = _sc_worker_id()
        for g in range(ng):
            (_, nwin, win), _ = splits[g]
            x_hbm, d0_hbm, d1_hbm = x_refs[g], idx_refs[2 * g], idx_refs[2 * g + 1]
            i0_v, i1_v, rows_v = bufs[3 * g:3 * g + 3]

            @pl.loop(0, nwin)
            def _(j, nwin=nwin, win=win, x_hbm=x_hbm, d0_hbm=d0_hbm, d1_hbm=d1_hbm,
                  i0_v=i0_v, i1_v=i1_v, rows_v=rows_v):
                base = pl.multiple_of((wid * nwin + j) * win, 8)
                pltpu.sync_copy(d0_hbm.at[wid, j], i0_v)
                pltpu.sync_copy(d1_hbm.at[wid, j], i1_v)
                pltpu.sync_copy(x_hbm.at[pl.ds(base, win)], rows_v)
                pltpu.sync_copy(rows_v, o_hbm.at[i0_v])
                pltpu.sync_copy(rows_v, o_hbm.at[i1_v])

    idx_args = []
    for (s0, s1) in splits:
        idx_args += [s0[0], s1[0]]
    return k(*h2_groups, *idx_args)


def _collect_rows(ys, dest_groups):
    splits = [(_sc_split(d[:, 0]), _sc_split(d[:, 1])) for d in dest_groups]
    ng = len(dest_groups)
    outs, scratch = [], []
    for d, ((_, _, win), _) in zip(dest_groups, splits):
        o = jax.ShapeDtypeStruct((d.shape[0], HALF), I32)
        outs += [o, o]
        scratch += [pltpu.VMEM((win,), I32), pltpu.VMEM((win, HALF), I32)]

    @functools.partial(
        pl.kernel,
        mesh=plsc.VectorSubcoreMesh(core_axis_name="c", subcore_axis_name="s"),
        out_type=tuple(outs),
        scratch_types=scratch,
        name="sc_collect",
    )
    def k(*refs):
        ys_hbm, idx_refs, out_refs, bufs = refs[0], refs[1:1 + 2 * ng], refs[1 + 2 * ng:1 + 4 * ng], refs[1 + 4 * ng:]
        wid = _sc_worker_id()
        for g in range(ng):
            (_, nwin, win), _ = splits[g]
            i_v, rows_v = bufs[2 * g:2 * g + 2]
            for kk in range(2):
                d_hbm, y_hbm = idx_refs[2 * g + kk], out_refs[2 * g + kk]

                @pl.loop(0, nwin)
                def _(j, nwin=nwin, win=win, d_hbm=d_hbm, y_hbm=y_hbm, i_v=i_v, rows_v=rows_v):
                    base = pl.multiple_of((wid * nwin + j) * win, 8)
                    pltpu.sync_copy(d_hbm.at[wid, j], i_v)
                    pltpu.sync_copy(ys_hbm.at[i_v], rows_v)
                    pltpu.sync_copy(rows_v, y_hbm.at[pl.ds(base, win)])

    idx_args = []
    for (s0, s1) in splits:
        idx_args += [s0[0], s1[0]]
    res = k(ys, *idx_args)
    return [(res[2 * g], res[2 * g + 1]) for g in range(ng)]


def _t5_bucket(rel):
    half = N_BUCKETS // 2
    max_exact = half // 2
    n = jnp.abs(rel)
    far = max_exact + (jnp.log(jnp.maximum(n, 1).astype(F32) / max_exact)
                       / math.log(MAX_DISTANCE / max_exact) * (half - max_exact)).astype(I32)
    far = jnp.minimum(far, half - 1)
    return jnp.where(rel > 0, half, 0) + jnp.where(n < max_exact, n, far)


def _bias_table(rel_bias, cq, nq, no_history):
    nk = WINDOW + cq
    j = jnp.arange(KEY_WIN)[:, None]
    c = jnp.arange(UNIT_Q)[None, :]
    jj = j - (c // cq) * cq
    valid = (jj >= 0) & (jj < nk) & (c < nq)
    if no_history:
        valid = valid & (j >= WINDOW)
    rel = jj - WINDOW - (c % cq)
    onehot = (_t5_bucket(rel)[:, :, None] == jnp.arange(N_BUCKETS)).astype(F32)
    bias = jnp.einsum("jcb,bh->jch", onehot, rel_bias.astype(F32), precision=lax.Precision.HIGHEST)
    bias = jnp.where(valid[:, :, None], bias, -jnp.inf)
    bias = jnp.transpose(bias.reshape(KEY_WIN, UNIT_Q, N_KV, GROUP), (2, 0, 3, 1))
    return bias.reshape(N_KV, KEY_WIN, GROUP * UNIT_Q)


def _sink_table(sinks):
    s = sinks.astype(F32).reshape(N_KV, 1, GROUP, 1)
    return jnp.broadcast_to(s, (N_KV, 1, GROUP, UNIT_Q)).reshape(N_KV, 1, GROUP * UNIT_Q)


def kernel(x_prompt, x_sample, state_attn_k, state_attn_v, state_conv, c_prompt, c_sample,
           rel_bias, w_ada, b_ada, norm1_w, w_in, q_norm_w, k_norm_w, attn_sinks, conv_w,
           w_out, norm2_w, w_router_group, b_router_group, w_router_expert, b_router_expert,
           w_gate, w_up, w_down):
    depth = w_ada.shape[0]
    assert depth == 1
    bp, tp, d = x_prompt.shape
    bs, ts, _ = x_sample.shape
    n_p, n_s = bp * tp, bs * ts
    n_tok = n_p + n_s
    l = 0

    wi = w_in[l]
    qw, kw, vw, rest = wi[:, :d], wi[:, d:d + KV_COLS], wi[:, d + KV_COLS:d + 2 * KV_COLS], wi[:, d + 2 * KV_COLS:]
    w_in_b = jnp.concatenate([qw, rest, kw, vw], axis=1).astype(BF16)
    w_out_b = w_out[l].astype(BF16)
    w_r = jnp.concatenate([w_router_group[l],
                           jnp.transpose(w_router_expert[l], (1, 0, 2)).reshape(d, N_EXPERTS),
                           jnp.zeros((d, LANES - N_GROUPS - N_EXPERTS), F32)], axis=1)
    w_r_hi = lax.reduce_precision(w_r, exponent_bits=8, mantissa_bits=7)
    w_r = jnp.concatenate([w_r_hi.astype(BF16), (w_r - w_r_hi).astype(BF16)], axis=1)
    b_r = jnp.concatenate([b_router_group[l], b_router_expert[l].reshape(-1),
                           jnp.zeros((LANES - N_GROUPS - N_EXPERTS,), F32)]).reshape(1, LANES)
    knw = jnp.tile(k_norm_w[l], N_KV).reshape(1, KV_COLS)
    qsc = jnp.tile(q_norm_w[l] * (HEAD_DIM ** -0.5), N_KV).reshape(1, KV_COLS)
    n1w = norm1_w[l].reshape(1, d)
    n2w = norm2_w[l].reshape(1, d)

    mod = _ada(jnp.concatenate([c_prompt, c_sample], axis=0), w_ada[l], b_ada[l]).reshape(bp + bs, 6, d)
    mod_p, mod_s = mod[:bp], mod[bp:]

    proj_s = _inproj(x_sample, mod_s, n1w, w_in_b, ts)
    sink_tab = _sink_table(attn_sinks[l])
    bias_p = jnp.concatenate([_bias_table(rel_bias, CHUNK, UNIT_Q, False),
                              _bias_table(rel_bias, CHUNK, UNIT_Q, True)], axis=0)
    merged_p, k_p, v_p, c_p = _front(x_prompt, mod_p, n1w, w_in_b, knw, qsc, bias_p, sink_tab, conv_w[l],
                                     MIX_TILE, UNIT_Q)
    state = (state_attn_k[l].reshape(bs, WINDOW, KV_COLS), state_attn_v[l].reshape(bs, WINDOW, KV_COLS),
             state_conv[l])
    merged_s, k_s, v_s, c_s = _mixer(proj_s, knw, qsc, _bias_table(rel_bias, ts, ts, False), sink_tab, conv_w[l],
                                     state, ts, ts)

    x1_p, h2_p, eid_p, ew_p = _outproj(merged_p, x_prompt, mod_p, w_out_b, n2w, w_r, b_r, ROW_TILE)
    x1_s, h2_s, eid_s, ew_s = _outproj(merged_s, x_sample, mod_s, w_out_b, n2w, w_r, b_r, ts)

    dest, totals = _rank(jnp.concatenate([eid_p, eid_s], axis=0), EXPERT_BLOCK)
    dests = [dest[:n_p], dest[n_p:]]
    nblk = (totals[0, :N_EXPERTS] + EXPERT_BLOCK - 1) // EXPERT_BLOCK
    start_blk = (jnp.cumsum(nblk) - nblk).astype(I32)
    nb_max = -(-2 * n_tok // EXPERT_BLOCK) + N_EXPERTS

    xs = _dispatch_rows([h2_p, h2_s], dests, nb_max * EXPERT_BLOCK)
    ys = _experts(xs, start_blk, nblk.astype(I32), w_gate[l], w_up[l], w_down[l], EXPERT_BLOCK)
    (y0_p, y1_p), (y0_s, y1_s) = _collect_rows(ys, dests)

    y_p = _final(x1_p, y0_p, y1_p, ew_p, mod_p, ROW_TILE)
    y_s = _final(x1_s, y0_s, y1_s, ew_s, mod_s, ts)

    kv_shape = (1, -1, WINDOW, N_KV, HEAD_DIM)
    return (y_p, y_s, k_p.reshape(kv_shape), v_p.reshape(kv_shape), c_p[None],
            k_s.reshape(kv_shape), v_s.reshape(kv_shape), c_s[None])
```

```python
import functools
import math

import numpy as np
import jax
import jax.numpy as jnp
from jax import lax
from jax.experimental import pallas as pl
from jax.experimental.pallas import tpu as pltpu
from jax.experimental.pallas import tpu_sc as plsc

F32 = jnp.float32
BF16 = jnp.bfloat16
I32 = jnp.int32

D_MODEL = 1024
HEAD_DIM = 64
N_HEADS = 16
N_KV = 4
GROUP = 4
CHUNK = 64
WINDOW = 128
N_BUCKETS = 32
MAX_DISTANCE = 128
N_GROUPS = 8
EPG = 8
N_EXPERTS = 64
D_EXPERT = 512
RMS_EPS = 1e-6
KV_COLS = N_KV * HEAD_DIM
IN_COLS = 6 * D_MODEL + 2 * KV_COLS
HALF = D_MODEL // 2
LANES = 128

VMEM_LIMIT = 56 * 1024 * 1024
INPROJ_TN = 512
ROW_TILE = 512
MIX_TILE = 512
UNIT_Q = 2 * CHUNK
KEY_WIN = WINDOW + UNIT_Q
PROJ_CHUNK = 256
EXPERT_BLOCK = 256
RANK_TILE = 512
ROW_DMA_PRIORITY = 1
SC_CORES = 2
SC_SUBCORES = 16
SC_WORKERS = SC_CORES * SC_SUBCORES
SC_MAX_WINDOW = 128


def _cparams(sem):
    return pltpu.CompilerParams(dimension_semantics=sem, vmem_limit_bytes=VMEM_LIMIT)


def _split_bf16(a):
    hi = a.astype(BF16)
    lo = (a - hi.astype(F32)).astype(BF16)
    return hi, lo


def _dot3(a, b):
    ah, al = _split_bf16(a)
    bh, bl = _split_bf16(b)
    d = functools.partial(jnp.dot, preferred_element_type=F32)
    return d(ah, bh) + (d(ah, bl) + d(al, bh))


def _pack_pairs(y):
    a = lax.bitcast_convert_type(y[:, :HALF].astype(BF16).astype(F32), I32)
    b = lax.bitcast_convert_type(y[:, HALF:].astype(BF16).astype(F32), I32)
    return a | lax.shift_right_logical(b, jnp.int32(16))


def _unpack_pairs(w):
    a = lax.bitcast_convert_type(w & jnp.int32(-65536), F32)
    b = lax.bitcast_convert_type(lax.shift_left(w, jnp.int32(16)), F32)
    return a, b


def _ada_kernel(c_ref, w_ref, b_ref, o_ref):
    c = c_ref[...]
    s = c * jax.nn.sigmoid(c)
    o_ref[...] = _dot3(s, w_ref[...]) + b_ref[...]


def _ada(c_all, w_ada, b_ada):
    r, d = c_all.shape
    n = w_ada.shape[1]
    tn = 1024
    return pl.pallas_call(
        _ada_kernel,
        out_shape=jax.ShapeDtypeStruct((r, n), F32),
        grid=(n // tn,),
        in_specs=[pl.BlockSpec((r, d), lambda j: (0, 0)),
                  pl.BlockSpec((d, tn), lambda j: (0, j)),
                  pl.BlockSpec((1, tn), lambda j: (0, j))],
        out_specs=pl.BlockSpec((r, tn), lambda j: (0, j)),
        compiler_params=_cparams(("arbitrary",)),
        name="ada",
    )(c_all, w_ada, b_ada.reshape(1, n))


def _inproj_kernel(x_ref, mod_ref, nw_ref, w_ref, o_ref):
    x = x_ref[0]
    mod = mod_ref[0]
    h = x * lax.rsqrt(jnp.mean(x * x, axis=-1, keepdims=True) + RMS_EPS) * nw_ref[...]
    h = h * (1.0 + mod[1:2]) + mod[0:1]
    hb = h.astype(BF16)
    for j in range(IN_COLS // INPROJ_TN):
        sl = slice(j * INPROJ_TN, (j + 1) * INPROJ_TN)
        o_ref[0, :, sl] = jnp.dot(hb, w_ref[:, sl], preferred_element_type=F32).astype(BF16)


def _inproj(x, mod, nw, w_in_b, tm):
    b, t, d = x.shape
    return pl.pallas_call(
        _inproj_kernel,
        out_shape=jax.ShapeDtypeStruct((b, t, IN_COLS), BF16),
        grid=(b, t // tm),
        in_specs=[pl.BlockSpec((1, tm, d), lambda i, j: (i, j, 0)),
                  pl.BlockSpec((1, 6, d), lambda i, j: (i, 0, 0)),
                  pl.BlockSpec((1, d), lambda i, j: (0, 0)),
                  pl.BlockSpec((d, IN_COLS), lambda i, j: (0, 0), pipeline_mode=pl.Buffered(1))],
        out_specs=pl.BlockSpec((1, tm, IN_COLS), lambda i, j: (i, j, 0)),
        compiler_params=_cparams(("arbitrary", "arbitrary")),
        name="inproj",
    )(x, mod, nw, w_in_b)


def _head_inv_rms(xf, bd):
    hi, lo = _split_bf16(xf * xf)
    ssq = jnp.dot(hi, bd, preferred_element_type=F32) + jnp.dot(lo, bd, preferred_element_type=F32)
    return lax.rsqrt(ssq * (1.0 / HEAD_DIM) + RMS_EPS)


def _mixer_body(q_ref, k_ref, v_ref, bg_ref, c_ref, u_ref, ga_ref, gc_ref,
                knw_ref, qsc_ref, bd_ref, bias_ref, sink_ref, cw_ref,
                kpast_ref, vpast_ref, cpast_ref, upast_ref,
                merged_ref, knew_ref, vnew_ref, cnew_ref,
                kq_buf, vt_buf, attn_buf, *, tq, nq, stateful):
    hb = WINDOW
    pw = UNIT_Q
    t = pl.program_id(1)
    bd = bd_ref[...]

    k = k_ref[0].astype(F32)
    kn = k * _head_inv_rms(k, bd) * knw_ref[...]
    kq = (kn * qsc_ref[...]).astype(BF16)
    vb = v_ref[0]
    vt = vb.astype(F32).T.astype(BF16)

    if stateful:
        kp = kpast_ref[0]
        vp = vpast_ref[0]
        for kv in range(N_KV):
            kq_buf[kv, hb + tq:] = jnp.zeros((KEY_WIN - hb - tq, HEAD_DIM), BF16)
        vt_buf[:, hb + tq:] = jnp.zeros((KV_COLS, KEY_WIN - hb - tq), BF16)
        u_hist = jnp.concatenate([jnp.zeros((6, D_MODEL), F32), cpast_ref[0]], axis=0)
        knew_ref[0] = jnp.concatenate([kp[tq:], kn], axis=0)
        vnew_ref[0] = jnp.concatenate([vp[tq:], vb.astype(F32)], axis=0)
    else:
        kraw = kpast_ref[0].astype(F32)
        kp = kraw * _head_inv_rms(kraw, bd) * knw_ref[...]
        vp = vpast_ref[0].astype(F32)
        u_hist = jnp.where(t == 0, 0.0, cpast_ref[0].astype(F32) * upast_ref[0].astype(F32))
        knew_ref[0] = kn[tq - hb:]
        vnew_ref[0] = vb[tq - hb:].astype(F32)
    _fill_keys(kq_buf, vt_buf, kp, vp, kq, vt, qsc_ref, tq)
    q = q_ref[0]
    _attention_units(lambda kv: q[:, kv * KV_COLS:(kv + 1) * KV_COLS].astype(F32), kq_buf, vt_buf,
                     bias_ref, sink_ref, bd, attn_buf, tq, nq, None if stateful else t)
    cu = c_ref[0].astype(F32) * u_ref[0].astype(F32)
    conv, u_all = _short_conv(u_hist, cu, cw_ref[...], tq)
    cnew_ref[0] = u_all[tq + 6:tq + 8]
    merged = (jax.nn.sigmoid(ga_ref[0].astype(F32)) * attn_buf[...]
              + jax.nn.sigmoid(gc_ref[0].astype(F32)) * (bg_ref[0].astype(F32) * conv))
    merged_ref[0] = merged.astype(BF16)


def _fill_keys(kq_buf, vt_buf, kp, vp, kq, vt, qsc_ref, tq):
    hb = WINDOW
    kqp = (kp * qsc_ref[...]).astype(BF16)
    for kv in range(N_KV):
        kq_buf[kv, 0:hb] = kqp[:, kv * HEAD_DIM:(kv + 1) * HEAD_DIM]
        kq_buf[kv, hb:hb + tq] = kq[:, kv * HEAD_DIM:(kv + 1) * HEAD_DIM]
    vt_buf[:, 0:hb] = vp.T.astype(BF16)
    vt_buf[:, hb:hb + tq] = vt


def _attention_units(q_group, kq_buf, vt_buf, bias_ref, sink_ref, bd, attn_buf, tq, nq, t_first, after_unit=None):
    pw = UNIT_Q
    for kv in range(N_KV):
        qf = q_group(kv)
        qn = (qf * _head_inv_rms(qf, bd)).astype(BF16)
        for u in range(tq // nq):
            r0 = u * nq
            parts = [qn[r0:r0 + nq, g * HEAD_DIM:(g + 1) * HEAD_DIM] for g in range(GROUP)]
            if nq < pw:
                zpad = jnp.zeros((pw - nq, HEAD_DIM), BF16)
                parts = [x for p_ in parts for x in (p_, zpad)]
            qs = jnp.concatenate(parts, axis=0)
            kw = kq_buf[kv, r0:r0 + KEY_WIN]
            st = lax.dot_general(kw, qs, (((1,), (1,)), ((), ())), preferred_element_type=F32)
            if t_first is not None and u == 0:
                bias = jnp.where(t_first == 0, bias_ref[kv + N_KV], bias_ref[kv])
            else:
                bias = bias_ref[kv]
            st = st + bias
            sink = sink_ref[kv]
            m = jnp.maximum(jnp.max(st, axis=0, keepdims=True), sink)
            p = jnp.exp(st - m)
            den = jnp.sum(p, axis=0, keepdims=True) + jnp.exp(sink - m)
            ot = jnp.dot(vt_buf[kv * HEAD_DIM:(kv + 1) * HEAD_DIM, r0:r0 + KEY_WIN], p.astype(BF16),
                         preferred_element_type=F32) / den
            for gp in range(GROUP // 2):
                blk = jnp.concatenate([ot[:, (2 * gp) * pw:(2 * gp + 1) * pw],
                                       ot[:, (2 * gp + 1) * pw:(2 * gp + 2) * pw]], axis=0)
                c0 = (kv * GROUP + 2 * gp) * HEAD_DIM
                attn_buf[r0:r0 + nq, c0:c0 + 2 * HEAD_DIM] = blk.T[:nq]
            if after_unit is not None:
                after_unit(kv * (tq // nq) + u)


def _short_conv(u_hist, cu, cw, tq):
    u_all = jnp.concatenate([u_hist, cu], axis=0)
    conv = cw[0:1] * u_all[6:6 + tq] + cw[1:2] * u_all[7:7 + tq] + cw[2:3] * u_all[8:8 + tq]
    return conv, u_all


def _front_body(x_ref, xp_ref, mod_ref, nw_ref, w_ref, knw_ref, qsc_ref, bd_ref, bias_ref, sink_ref, cw_ref,
                merged_ref, knew_ref, vnew_ref, cnew_ref,
                kq_buf, vt_buf, u_buf, attn_buf, pbuf, *, tq, nq):
    hb = WINDOW
    d = D_MODEL
    t = pl.program_id(1)
    bd = bd_ref[...]
    mod = mod_ref[0]

    @pl.when(t == 0)
    def _():
        u_buf[0] = jnp.zeros((8, d), F32)

    def modnorm(x):
        h = x * lax.rsqrt(jnp.mean(x * x, axis=-1, keepdims=True) + RMS_EPS) * nw_ref[...]
        return (h * (1.0 + mod[1:2]) + mod[0:1]).astype(BF16)

    def proj(hrows, c0, width):
        return jnp.dot(hrows, w_ref[:, c0:c0 + width], preferred_element_type=F32)

    hbf = modnorm(x_ref[0])
    kv_t = proj(hbf, 6 * d, 2 * KV_COLS)
    kv_p = proj(modnorm(xp_ref[0]), 6 * d, 2 * KV_COLS)
    k, v = kv_t[:, :KV_COLS], kv_t[:, KV_COLS:]
    kn = k * _head_inv_rms(k, bd) * knw_ref[...]
    kq = (kn * qsc_ref[...]).astype(BF16)
    kraw = kv_p[:, :KV_COLS]
    kp = kraw * _head_inv_rms(kraw, bd) * knw_ref[...]
    knew_ref[0] = kn[tq - hb:]
    vnew_ref[0] = v[tq - hb:]
    _fill_keys(kq_buf, vt_buf, kp, kv_p[:, KV_COLS:], kq, v.T.astype(BF16), qsc_ref, tq)

    n_units = N_KV * (tq // nq)
    q_chunks = [[g * KV_COLS + c for c in range(0, KV_COLS, PROJ_CHUNK)] for g in range(N_KV)]
    rest_chunks = list(range(d, 6 * d, PROJ_CHUNK))

    def issue(c0):
        pbuf[:, c0:c0 + PROJ_CHUNK] = proj(hbf, c0, PROJ_CHUNK)

    for c0 in q_chunks[0]:
        issue(c0)
    n_rest = len(rest_chunks)

    def after_unit(i):
        kv, u = divmod(i, tq // nq)
        if u == 0 and kv + 1 < N_KV:
            for c0 in q_chunks[kv + 1]:
                issue(c0)
        for c0 in rest_chunks[i * n_rest // n_units:(i + 1) * n_rest // n_units]:
            issue(c0)

    _attention_units(lambda kv: pbuf[:, kv * KV_COLS:(kv + 1) * KV_COLS], kq_buf, vt_buf,
                     bias_ref, sink_ref, bd, attn_buf, tq, nq, t, after_unit)

    cu = pbuf[:, 2 * d:3 * d] * pbuf[:, 3 * d:4 * d]
    conv, u_all = _short_conv(u_buf[t % 2], cu, cw_ref[...], tq)
    cnew_ref[0] = u_all[tq + 6:tq + 8]
    u_buf[(t + 1) % 2] = u_all[tq:tq + 8]
    merged = (jax.nn.sigmoid(pbuf[:, 4 * d:5 * d]) * attn_buf[...]
              + jax.nn.sigmoid(pbuf[:, 5 * d:6 * d]) * (pbuf[:, d:2 * d] * conv))
    merged_ref[0] = merged.astype(BF16)


def _front(x, mod, nw, w_in_b, knw, qsc, bias_tab, sink_tab, conv_w, tq, nq):
    b, t, d = x.shape
    r = np.arange(KV_COLS) // HEAD_DIM
    bd = jnp.asarray((r[:, None] == r[None, :]).astype(np.float32), BF16)
    const2 = lambda shp: pl.BlockSpec(shp, lambda i, s: (0, 0))
    const3 = lambda shp: pl.BlockSpec(shp, lambda i, s: (0, 0, 0))
    per_b = lambda shp: pl.BlockSpec(shp, lambda i, s: (i, 0, 0))
    kw_ = tq // WINDOW
    in_specs = [pl.BlockSpec((1, tq, d), lambda i, s: (i, s, 0)),
                pl.BlockSpec((1, WINDOW, d), lambda i, s: (i, jnp.maximum(s * kw_ - 1, 0), 0)),
                per_b((1, 6, d)), const2((1, d)),
                pl.BlockSpec((d, IN_COLS), lambda i, s: (0, 0), pipeline_mode=pl.Buffered(1)),
                const2((1, KV_COLS)), const2((1, KV_COLS)), const2((KV_COLS, KV_COLS)),
                const3(bias_tab.shape), const3(sink_tab.shape), const2((3, d))]
    out_shape = (jax.ShapeDtypeStruct((b, t, d), BF16),
                 jax.ShapeDtypeStruct((b, WINDOW, KV_COLS), F32),
                 jax.ShapeDtypeStruct((b, WINDOW, KV_COLS), F32),
                 jax.ShapeDtypeStruct((b, 2, d), F32))
    out_specs = (pl.BlockSpec((1, tq, d), lambda i, s: (i, s, 0)),
                 per_b((1, WINDOW, KV_COLS)), per_b((1, WINDOW, KV_COLS)), per_b((1, 2, d)))
    return pl.pallas_call(
        functools.partial(_front_body, tq=tq, nq=nq),
        out_shape=out_shape,
        grid=(b, t // tq),
        in_specs=in_specs,
        out_specs=out_specs,
        scratch_shapes=[pltpu.VMEM((N_KV, WINDOW + tq, HEAD_DIM), BF16),
                        pltpu.VMEM((KV_COLS, WINDOW + tq), BF16),
                        pltpu.VMEM((2, 8, d), F32),
                        pltpu.VMEM((tq, d), F32),
                        pltpu.VMEM((tq, 6 * d), F32)],
        compiler_params=_cparams(("arbitrary", "arbitrary")),
        name="front",
    )(x, x, mod, nw, w_in_b, knw, qsc, bd, bias_tab, sink_tab, conv_w)


def _mixer(proj, knw, qsc, bias_tab, sink_tab, conv_w, state, tq, nq):
    b, t, _ = proj.shape
    d = D_MODEL
    stateful = state is not None
    key_rows = max(WINDOW + tq, KEY_WIN)
    r = np.arange(KV_COLS) // HEAD_DIM
    bd = jnp.asarray((r[:, None] == r[None, :]).astype(np.float32), BF16)
    wide = lambda j: pl.BlockSpec((1, tq, d), lambda i, s, j=j: (i, s, j))
    kvspec = lambda j: pl.BlockSpec((1, tq, KV_COLS), lambda i, s, j=j: (i, s, j))
    const2 = lambda shp: pl.BlockSpec(shp, lambda i, s: (0, 0))
    const3 = lambda shp: pl.BlockSpec(shp, lambda i, s: (0, 0, 0))
    per_b = lambda shp: pl.BlockSpec(shp, lambda i, s: (i, 0, 0))
    kvblk = 6 * d // KV_COLS
    if stateful:
        hist_specs = [per_b((1, WINDOW, KV_COLS)), per_b((1, WINDOW, KV_COLS)), per_b((1, 2, d)), per_b((1, 2, d))]
        hist_args = [state[0], state[1], state[2], state[2]]
    else:
        kw_ = tq // WINDOW
        prev_kv = lambda j: pl.BlockSpec((1, WINDOW, KV_COLS),
                                         lambda i, s, j=j: (i, jnp.maximum(s * kw_ - 1, 0), j))
        prev8 = lambda j: pl.BlockSpec((1, 8, d), lambda i, s, j=j: (i, jnp.maximum(s * (tq // 8) - 1, 0), j))
        hist_specs = [prev_kv(kvblk), prev_kv(kvblk + 1), prev8(2), prev8(3)]
        hist_args = [proj, proj, proj, proj]
    in_specs = [wide(0), kvspec(kvblk), kvspec(kvblk + 1), wide(1), wide(2), wide(3), wide(4), wide(5),
                const2((1, KV_COLS)), const2((1, KV_COLS)), const2((KV_COLS, KV_COLS)),
                const3(bias_tab.shape), const3(sink_tab.shape), const2((3, d))] + hist_specs
    out_shape = (jax.ShapeDtypeStruct((b, t, d), BF16),
                 jax.ShapeDtypeStruct((b, WINDOW, KV_COLS), F32),
                 jax.ShapeDtypeStruct((b, WINDOW, KV_COLS), F32),
                 jax.ShapeDtypeStruct((b, 2, d), F32))
    out_specs = (pl.BlockSpec((1, tq, d), lambda i, s: (i, s, 0)),
                 per_b((1, WINDOW, KV_COLS)), per_b((1, WINDOW, KV_COLS)), per_b((1, 2, d)))
    return pl.pallas_call(
        functools.partial(_mixer_body, tq=tq, nq=nq, stateful=stateful),
        out_shape=out_shape,
        grid=(b, t // tq),
        in_specs=in_specs,
        out_specs=out_specs,
        scratch_shapes=[pltpu.VMEM((N_KV, key_rows, HEAD_DIM), BF16),
                        pltpu.VMEM((KV_COLS, key_rows), BF16),
                        pltpu.VMEM((tq, d), F32)],
        compiler_params=_cparams(("arbitrary", "arbitrary")),
        name="mixer_state" if stateful else "mixer",
    )(proj, proj, proj, proj, proj, proj, proj, proj, knw, qsc, bd, bias_tab, sink_tab, conv_w, *hist_args)


def _route(logits):
    lane = lax.broadcasted_iota(I32, logits.shape, 1)
    neg = -jnp.inf
    big = jnp.int32(1 << 20)
    gl = jnp.where(lane < N_GROUPS, logits, neg)
    gmax = jnp.max(gl, axis=-1, keepdims=True)
    g_idx = jnp.min(jnp.where(gl == gmax, lane, big), axis=-1, keepdims=True)
    g_w = 1.0 / jnp.sum(jnp.exp(gl - gmax), axis=-1, keepdims=True)
    lo = N_GROUPS + g_idx * EPG
    el = jnp.where((lane >= lo) & (lane < lo + EPG), logits, neg)
    m1 = jnp.max(el, axis=-1, keepdims=True)
    i1 = jnp.min(jnp.where(el == m1, lane, big), axis=-1, keepdims=True)
    el2 = jnp.where(lane == i1, neg, el)
    m2 = jnp.max(el2, axis=-1, keepdims=True)
    i2 = jnp.min(jnp.where(el2 == m2, lane, big), axis=-1, keepdims=True)
    r = jnp.exp(m2 - m1)
    w1 = 1.0 / (1.0 + r)
    w2 = r / (1.0 + r)
    return i1 - N_GROUPS, i2 - N_GROUPS, g_w * w1, g_w * w2


def _outproj_kernel(m_ref, x_ref, mod_ref, wo_ref, nw_ref, wr_ref, br_ref,
                    x1_ref, h2_ref, eid_ref, ew_ref):
    mod = mod_ref[0]
    mix = jnp.dot(m_ref[0], wo_ref[...], preferred_element_type=F32)
    x1 = x_ref[0] + mod[2:3] * mix
    x1_ref[0] = x1
    h = x1 * lax.rsqrt(jnp.mean(x1 * x1, axis=-1, keepdims=True) + RMS_EPS) * nw_ref[...]
    h = h * (1.0 + mod[4:5]) + mod[3:4]
    h2_ref[...] = _pack_pairs(h)
    tm = h.shape[0]
    h_hi, h_lo = _split_bf16(h)
    prod = jnp.dot(jnp.concatenate([h_hi, h_lo], axis=0), wr_ref[...], preferred_element_type=F32)
    logits = prod[:tm, :LANES] + (prod[:tm, LANES:] + prod[tm:, :LANES]) + br_ref[...]
    e1, e2, w1, w2 = _route(logits)
    lane8 = lax.broadcasted_iota(I32, (h.shape[0], 8), 1)
    eid_ref[...] = jnp.where(lane8 == 0, e1, jnp.where(lane8 == 1, e2, 0))
    ew_ref[...] = jnp.where(lane8 == 0, w1, jnp.where(lane8 == 1, w2, 0.0))


def _outproj(merged, x, mod, w_out_b, nw, w_r, b_r, tm):
    b, t, d = x.shape
    nt = t // tm
    flat = lambda i, j: (i * nt + j, 0)
    return pl.pallas_call(
        _outproj_kernel,
        out_shape=(jax.ShapeDtypeStruct((b, t, d), F32),
                   jax.ShapeDtypeStruct((b * t, HALF), I32),
                   jax.ShapeDtypeStruct((b * t, 8), I32),
                   jax.ShapeDtypeStruct((b * t, 8), F32)),
        grid=(b, nt),
        in_specs=[pl.BlockSpec((1, tm, d), lambda i, j: (i, j, 0)),
                  pl.BlockSpec((1, tm, d), lambda i, j: (i, j, 0)),
                  pl.BlockSpec((1, 6, d), lambda i, j: (i, 0, 0)),
                  pl.BlockSpec((d, d), lambda i, j: (0, 0)),
                  pl.BlockSpec((1, d), lambda i, j: (0, 0)),
                  pl.BlockSpec((d, 2 * LANES), lambda i, j: (0, 0)),
                  pl.BlockSpec((1, LANES), lambda i, j: (0, 0))],
        out_specs=(pl.BlockSpec((1, tm, d), lambda i, j: (i, j, 0)),
                   pl.BlockSpec((tm, HALF), flat),
                   pl.BlockSpec((tm, 8), flat),
                   pl.BlockSpec((tm, 8), flat)),
        compiler_params=_cparams(("arbitrary", "arbitrary")),
        name="outproj",
    )(merged, x, mod, w_out_b, nw, w_r, b_r)


def _rank_kernel(eid_ref, tri_ref, upper_ref, dest_ref, tot_ref, cnt, starts, *, block):
    ph = pl.program_id(0)
    i = pl.program_id(1)
    tm = eid_ref.shape[0]
    lane = lax.broadcasted_iota(I32, (tm, LANES), 1)
    e0 = eid_ref[:, 0:1]
    e1 = eid_ref[:, 1:2]
    hot0 = lane == e0
    hot1 = lane == e1
    onehot = jnp.where(hot0 | hot1, 1.0, 0.0)
    colsum = jnp.sum(onehot, axis=0, keepdims=True)

    @pl.when((ph == 0) & (i == 0))
    def _():
        cnt[...] = jnp.zeros_like(cnt)

    @pl.when(ph == 0)
    def _():
        cnt[0:1] = cnt[0:1] + colsum

    @pl.when((ph == 1) & (i == 0))
    def _():
        tot = cnt[0:1]
        tot_ref[...] = tot.astype(I32)
        nblk = jnp.floor((tot + (block - 1)) * (1.0 / block))
        hi = jnp.floor(nblk * (1.0 / 16.0))
        lo = nblk - hi * 16.0
        up = upper_ref[...]
        excl = (jnp.dot(jnp.broadcast_to(hi, (8, LANES)).astype(BF16), up, preferred_element_type=F32) * 16.0
                + jnp.dot(jnp.broadcast_to(lo, (8, LANES)).astype(BF16), up, preferred_element_type=F32))
        starts[...] = excl * float(block)
        cnt[...] = jnp.zeros_like(cnt)

    @pl.when(ph == 1)
    def _():
        prefix = jnp.dot(tri_ref[...], onehot.astype(BF16), preferred_element_type=F32)
        pos = prefix + (starts[0:1] + cnt[0:1])
        d0 = jnp.sum(jnp.where(hot0, pos, 0.0), axis=-1, keepdims=True).astype(I32)
        d1 = jnp.sum(jnp.where(hot1, pos, 0.0), axis=-1, keepdims=True).astype(I32)
        lane8 = lax.broadcasted_iota(I32, (tm, 8), 1)
        dest_ref[...] = jnp.where(lane8 == 0, d0, jnp.where(lane8 == 1, d1, 0))
        cnt[0:1] = cnt[0:1] + colsum


def _rank(eid, block):
    n = eid.shape[0]
    tm = math.gcd(n, RANK_TILE)
    r = np.arange(tm)
    tri = jnp.asarray((r[:, None] > r[None, :]).astype(np.float32), BF16)
    l = np.arange(LANES)
    upper = jnp.asarray(((l[:, None] < l[None, :]) & (l[None, :] < N_EXPERTS)).astype(np.float32), BF16)
    return pl.pallas_call(
        functools.partial(_rank_kernel, block=block),
        out_shape=(jax.ShapeDtypeStruct((n, 8), I32), jax.ShapeDtypeStruct((1, LANES), I32)),
        grid=(2, n // tm),
        in_specs=[pl.BlockSpec((tm, 8), lambda p, i: (i, 0)),
                  pl.BlockSpec((tm, tm), lambda p, i: (0, 0)),
                  pl.BlockSpec((LANES, LANES), lambda p, i: (0, 0))],
        out_specs=(pl.BlockSpec((tm, 8), lambda p, i: (i * p, 0)),
                   pl.BlockSpec((1, LANES), lambda p, i: (0, 0))),
        scratch_shapes=[pltpu.VMEM((8, LANES), F32), pltpu.VMEM((8, LANES), F32)],
        compiler_params=_cparams(("arbitrary", "arbitrary")),
        name="rank",
    )(eid, tri, upper)


def _expert_kernel(start_ref, nblk_ref, xs_hbm, wg_ref, wu_ref, wd_ref, ys_hbm,
                   xbuf, ybuf, wg_s, wu_s, wd_s, sem_in, sem_out, *, block):
    e = pl.program_id(0)
    n = nblk_ref[e]
    base = start_ref[e]

    def in_copy(i, slot):
        rows = pl.ds(pl.multiple_of((base + i) * block, block), block)
        return pltpu.make_async_copy(xs_hbm.at[rows], xbuf.at[slot], sem_in.at[slot])

    def out_copy(i, slot):
        rows = pl.ds(pl.multiple_of((base + i) * block, block), block)
        return pltpu.make_async_copy(ybuf.at[slot], ys_hbm.at[rows], sem_out.at[slot])

    @pl.when(n > 0)
    def _():
        in_copy(0, 0).start(priority=ROW_DMA_PRIORITY)
        wg_s[...] = wg_ref[0].astype(BF16)
        wu_s[...] = wu_ref[0].astype(BF16)
        wd_s[...] = wd_ref[0].astype(BF16)

        def body(i, carry):
            slot = i % 2
            in_copy(i, slot).wait()

            @pl.when(i + 1 < n)
            def _():
                in_copy(i + 1, 1 - slot).start(priority=ROW_DMA_PRIORITY)

            @pl.when(i >= 2)
            def _():
                out_copy(i - 2, slot).wait()

            a, c = _unpack_pairs(xbuf[slot])
            x = jnp.concatenate([a.astype(BF16), c.astype(BF16)], axis=1)
            g = jnp.dot(x, wg_s[...], preferred_element_type=F32)
            u = jnp.dot(x, wu_s[...], preferred_element_type=F32)
            hmid = (g * jax.nn.sigmoid(g) * u).astype(BF16)
            ybuf[slot] = _pack_pairs(jnp.dot(hmid, wd_s[...], preferred_element_type=F32))
            out_copy(i, slot).start(priority=ROW_DMA_PRIORITY)
            return carry

        lax.fori_loop(0, n, body, 0)

        @pl.when(n >= 2)
        def _():
            out_copy(n - 2, n % 2).wait()
        out_copy(n - 1, (n - 1) % 2).wait()


def _experts(xs, start_blk, nblk, w_gate, w_up, w_down, block):
    n_rows = xs.shape[0]
    wblk = lambda e, st, nb: (e, 0, 0)
    grid_spec = pltpu.PrefetchScalarGridSpec(
        num_scalar_prefetch=2,
        grid=(N_EXPERTS,),
        in_specs=[pl.BlockSpec(memory_space=pl.ANY),
                  pl.BlockSpec((1, D_MODEL, D_EXPERT), wblk),
                  pl.BlockSpec((1, D_MODEL, D_EXPERT), wblk),
                  pl.BlockSpec((1, D_EXPERT, D_MODEL), wblk)],
        out_specs=pl.BlockSpec(memory_space=pl.ANY),
        scratch_shapes=[pltpu.VMEM((2, block, HALF), I32),
                        pltpu.VMEM((2, block, HALF), I32),
                        pltpu.VMEM((D_MODEL, D_EXPERT), BF16),
                        pltpu.VMEM((D_MODEL, D_EXPERT), BF16),
                        pltpu.VMEM((D_EXPERT, D_MODEL), BF16),
                        pltpu.SemaphoreType.DMA((2,)),
                        pltpu.SemaphoreType.DMA((2,))])
    return pl.pallas_call(
        functools.partial(_expert_kernel, block=block),
        out_shape=jax.ShapeDtypeStruct((n_rows, HALF), I32),
        grid_spec=grid_spec,
        compiler_params=_cparams(("arbitrary",)),
        name="experts",
    )(start_blk, nblk, xs, w_gate, w_up, w_down)


def _final_kernel(x1_ref, y0_ref, y1_ref, ew_ref, mod_ref, o_ref):
    a0, b0 = _unpack_pairs(y0_ref[...])
    a1, b1 = _unpack_pairs(y1_ref[...])
    w0 = ew_ref[:, 0:1]
    w1 = ew_ref[:, 1:2]
    moe = jnp.concatenate([w0 * a0 + w1 * a1, w0 * b0 + w1 * b1], axis=1)
    o_ref[0] = x1_ref[0] + mod_ref[0][5:6] * moe


def _final(x1, y0, y1, ew, mod, tm):
    b, t, d = x1.shape
    nt = t // tm
    flat = lambda i, j: (i * nt + j, 0)
    return pl.pallas_call(
        _final_kernel,
        out_shape=jax.ShapeDtypeStruct((b, t, d), F32),
        grid=(b, nt),
        in_specs=[pl.BlockSpec((1, tm, d), lambda i, j: (i, j, 0)),
                  pl.BlockSpec((tm, HALF), flat),
                  pl.BlockSpec((tm, HALF), flat),
                  pl.BlockSpec((tm, 8), flat),
                  pl.BlockSpec((1, 6, d), lambda i, j: (i, 0, 0))],
        out_specs=pl.BlockSpec((1, tm, d), lambda i, j: (i, j, 0)),
        compiler_params=_cparams(("arbitrary", "arbitrary")),
        name="final",
    )(x1, y0, y1, ew, mod)


def _sc_window(rows_per_worker):
    for w in range(SC_MAX_WINDOW, 7, -8):
        if rows_per_worker % w == 0:
            return w
    raise ValueError(f"no SparseCore window divides {rows_per_worker} rows per worker")


def _sc_split(idx):
    n = idx.shape[0]
    per = n // SC_WORKERS
    assert per * SC_WORKERS == n
    win = _sc_window(per)
    return idx.reshape(SC_WORKERS, per // win, win), per // win, win


def _sc_worker_id():
    return lax.axis_index("s") * SC_CORES + lax.axis_index("c")


def _dispatch_rows(h2_groups, dest_groups, n_rows):
    splits = [(_sc_split(d[:, 0]), _sc_split(d[:, 1])) for d in dest_groups]
    ng = len(h2_groups)
    scratch = []
    for (_, _, win), _ in splits:
        scratch += [pltpu.VMEM((win,), I32), pltpu.VMEM((win,), I32), pltpu.VMEM((win, HALF), I32)]

    @functools.partial(
        pl.kernel,
        mesh=plsc.VectorSubcoreMesh(core_axis_name="c", subcore_axis_name="s"),
        out_type=jax.ShapeDtypeStruct((n_rows, HALF), I32),
        scratch_types=scratch,
        name="sc_dispatch",
    )
    def k(*refs):
        x_refs, idx_refs, o_hbm, bufs = refs[:ng], refs[ng:3 * ng], refs[3 * ng], refs[3 * ng + 1:]
        wid = _sc_worker_id()
        for g in range(ng):
            (_, nwin, win), _ = splits[g]
            x_hbm, d0_hbm, d1_hbm = x_refs[g], idx_refs[2 * g], idx_refs[2 * g + 1]
            i0_v, i1_v, rows_v = bufs[3 * g:3 * g + 3]

            @pl.loop(0, nwin)
            def _(j, nwin=nwin, win=win, x_hbm=x_hbm, d0_hbm=d0_hbm, d1_hbm=d1_hbm,
                  i0_v=i0_v, i1_v=i1_v, rows_v=rows_v):
                base = pl.multiple_of((wid * nwin + j) * win, 8)
                pltpu.sync_copy(d0_hbm.at[wid, j], i0_v)
                pltpu.sync_copy(d1_hbm.at[wid, j], i1_v)
                pltpu.sync_copy(x_hbm.at[pl.ds(base, win)], rows_v)
                pltpu.sync_copy(rows_v, o_hbm.at[i0_v])
                pltpu.sync_copy(rows_v, o_hbm.at[i1_v])

    idx_args = []
    for (s0, s1) in splits:
        idx_args += [s0[0], s1[0]]
    return k(*h2_groups, *idx_args)


def _collect_rows(ys, dest_groups):
    splits = [(_sc_split(d[:, 0]), _sc_split(d[:, 1])) for d in dest_groups]
    ng = len(dest_groups)
    outs, scratch = [], []
    for d, ((_, _, win), _) in zip(dest_groups, splits):
        o = jax.ShapeDtypeStruct((d.shape[0], HALF), I32)
        outs += [o, o]
        scratch += [pltpu.VMEM((win,), I32), pltpu.VMEM((win, HALF), I32)]

    @functools.partial(
        pl.kernel,
        mesh=plsc.VectorSubcoreMesh(core_axis_name="c", subcore_axis_name="s"),
        out_type=tuple(outs),
        scratch_types=scratch,
        name="sc_collect",
    )
    def k(*refs):
        ys_hbm, idx_refs, out_refs, bufs = refs[0], refs[1:1 + 2 * ng], refs[1 + 2 * ng:1 + 4 * ng], refs[1 + 4 * ng:]
        wid = _sc_worker_id()
        for g in range(ng):
            (_, nwin, win), _ = splits[g]
            i_v, rows_v = bufs[2 * g:2 * g + 2]
            for kk in range(2):
                d_hbm, y_hbm = idx_refs[2 * g + kk], out_refs[2 * g + kk]

                @pl.loop(0, nwin)
                def _(j, nwin=nwin, win=win, d_hbm=d_hbm, y_hbm=y_hbm, i_v=i_v, rows_v=rows_v):
                    base = pl.multiple_of((wid * nwin + j) * win, 8)
                    pltpu.sync_copy(d_hbm.at[wid, j], i_v)
                    pltpu.sync_copy(ys_hbm.at[i_v], rows_v)
                    pltpu.sync_copy(rows_v, y_hbm.at[pl.ds(base, win)])

    idx_args = []
    for (s0, s1) in splits:
        idx_args += [s0[0], s1[0]]
    res = k(ys, *idx_args)
    return [(res[2 * g], res[2 * g + 1]) for g in range(ng)]


def _t5_bucket(rel):
    half = N_BUCKETS // 2
    max_exact = half // 2
    n = jnp.abs(rel)
    far = max_exact + (jnp.log(jnp.maximum(n, 1).astype(F32) / max_exact)
                       / math.log(MAX_DISTANCE / max_exact) * (half - max_exact)).astype(I32)
    far = jnp.minimum(far, half - 1)
    return jnp.where(rel > 0, half, 0) + jnp.where(n < max_exact, n, far)


def _bias_table(rel_bias, cq, nq, no_history):
    nk = WINDOW + cq
    j = jnp.arange(KEY_WIN)[:, None]
    c = jnp.arange(UNIT_Q)[None, :]
    jj = j - (c // cq) * cq
    valid = (jj >= 0) & (jj < nk) & (c < nq)
    if no_history:
        valid = valid & (j >= WINDOW)
    rel = jj - WINDOW - (c % cq)
    onehot = (_t5_bucket(rel)[:, :, None] == jnp.arange(N_BUCKETS)).astype(F32)
    bias = jnp.einsum("jcb,bh->jch", onehot, rel_bias.astype(F32), precision=lax.Precision.HIGHEST)
    bias = jnp.where(valid[:, :, None], bias, -jnp.inf)
    bias = jnp.transpose(bias.reshape(KEY_WIN, UNIT_Q, N_KV, GROUP), (2, 0, 3, 1))
    return bias.reshape(N_KV, KEY_WIN, GROUP * UNIT_Q)


def _sink_table(sinks):
    s = sinks.astype(F32).reshape(N_KV, 1, GROUP, 1)
    return jnp.broadcast_to(s, (N_KV, 1, GROUP, UNIT_Q)).reshape(N_KV, 1, GROUP * UNIT_Q)


def kernel(x_prompt, x_sample, state_attn_k, state_attn_v, state_conv, c_prompt, c_sample,
           rel_bias, w_ada, b_ada, norm1_w, w_in, q_norm_w, k_norm_w, attn_sinks, conv_w,
           w_out, norm2_w, w_router_group, b_router_group, w_router_expert, b_router_expert,
           w_gate, w_up, w_down):
    depth = w_ada.shape[0]
    assert depth == 1
    bp, tp, d = x_prompt.shape
    bs, ts, _ = x_sample.shape
    n_p, n_s = bp * tp, bs * ts
    n_tok = n_p + n_s
    l = 0

    wi = w_in[l]
    qw, kw, vw, rest = wi[:, :d], wi[:, d:d + KV_COLS], wi[:, d + KV_COLS:d + 2 * KV_COLS], wi[:, d + 2 * KV_COLS:]
    w_in_b = jnp.concatenate([qw, rest, kw, vw], axis=1).astype(BF16)
    w_out_b = w_out[l].astype(BF16)
    w_r = jnp.concatenate([w_router_group[l],
                           jnp.transpose(w_router_expert[l], (1, 0, 2)).reshape(d, N_EXPERTS),
                           jnp.zeros((d, LANES - N_GROUPS - N_EXPERTS), F32)], axis=1)
    w_r_hi = lax.reduce_precision(w_r, exponent_bits=8, mantissa_bits=7)
    w_r = jnp.concatenate([w_r_hi.astype(BF16), (w_r - w_r_hi).astype(BF16)], axis=1)
    b_r = jnp.concatenate([b_router_group[l], b_router_expert[l].reshape(-1),
                           jnp.zeros((LANES - N_GROUPS - N_EXPERTS,), F32)]).reshape(1, LANES)
    knw = jnp.tile(k_norm_w[l], N_KV).reshape(1, KV_COLS)
    qsc = jnp.tile(q_norm_w[l] * (HEAD_DIM ** -0.5), N_KV).reshape(1, KV_COLS)
    n1w = norm1_w[l].reshape(1, d)
    n2w = norm2_w[l].reshape(1, d)

    mod = _ada(jnp.concatenate([c_prompt, c_sample], axis=0), w_ada[l], b_ada[l]).reshape(bp + bs, 6, d)
    mod_p, mod_s = mod[:bp], mod[bp:]

    proj_s = _inproj(x_sample, mod_s, n1w, w_in_b, ts)
    sink_tab = _sink_table(attn_sinks[l])
    bias_p = jnp.concatenate([_bias_table(rel_bias, CHUNK, UNIT_Q, False),
                              _bias_table(rel_bias, CHUNK, UNIT_Q, True)], axis=0)
    merged_p, k_p, v_p, c_p = _front(x_prompt, mod_p, n1w, w_in_b, knw, qsc, bias_p, sink_tab, conv_w[l],
                                     MIX_TILE, UNIT_Q)
    state = (state_attn_k[l].reshape(bs, WINDOW, KV_COLS), state_attn_v[l].reshape(bs, WINDOW, KV_COLS),
             state_conv[l])
    merged_s, k_s, v_s, c_s = _mixer(proj_s, knw, qsc, _bias_table(rel_bias, ts, ts, False), sink_tab, conv_w[l],
                                     state, ts, ts)

    x1_p, h2_p, eid_p, ew_p = _outproj(merged_p, x_prompt, mod_p, w_out_b, n2w, w_r, b_r, ROW_TILE)
    x1_s, h2_s, eid_s, ew_s = _outproj(merged_s, x_sample, mod_s, w_out_b, n2w, w_r, b_r, ts)

    dest, totals = _rank(jnp.concatenate([eid_p, eid_s], axis=0), EXPERT_BLOCK)
    dests = [dest[:n_p], dest[n_p:]]
    nblk = (totals[0, :N_EXPERTS] + EXPERT_BLOCK - 1) // EXPERT_BLOCK
    start_blk = (jnp.cumsum(nblk) - nblk).astype(I32)
    nb_max = -(-2 * n_tok // EXPERT_BLOCK) + N_EXPERTS

    xs = _dispatch_rows([h2_p, h2_s], dests, nb_max * EXPERT_BLOCK)
    ys = _experts(xs, start_blk, nblk.astype(I32), w_gate[l], w_up[l], w_down[l], EXPERT_BLOCK)
    (y0_p, y1_p), (y0_s, y1_s) = _collect_rows(ys, dests)

    y_p = _final(x1_p, y0_p, y1_p, ew_p, mod_p, ROW_TILE)
    y_s = _final(x1_s, y0_s, y1_s, ew_s, mod_s, ts)

    kv_shape = (1, -1, WINDOW, N_KV, HEAD_DIM)
    return (y_p, y_s, k_p.reshape(kv_shape), v_p.reshape(kv_shape), c_p[None],
            k_s.reshape(kv_shape), v_s.reshape(kv_shape), c_s[None])
```

```python
import functools
import math

import numpy as np
import jax
import jax.numpy as jnp
from jax import lax
from jax.experimental import pallas as pl
from jax.experimental.pallas import tpu as pltpu
from jax.experimental.pallas import tpu_sc as plsc

F32 = jnp.float32
BF16 = jnp.bfloat16
I32 = jnp.int32

D_MODEL = 1024
HEAD_DIM = 64
N_HEADS = 16
N_KV = 4
GROUP = 4
CHUNK = 64
WINDOW = 128
N_BUCKETS = 32
MAX_DISTANCE = 128
N_GROUPS = 8
EPG = 8
N_EXPERTS = 64
D_EXPERT = 512
RMS_EPS = 1e-6
KV_COLS = N_KV * HEAD_DIM
IN_COLS = 6 * D_MODEL + 2 * KV_COLS
HALF = D_MODEL // 2
LANES = 128

VMEM_LIMIT = 56 * 1024 * 1024
INPROJ_TN = 512
ROW_TILE = 512
MIX_TILE = 512
UNIT_Q = 2 * CHUNK
KEY_WIN = WINDOW + UNIT_Q
PROJ_CHUNK = 256
EXPERT_BLOCK = 256
RANK_TILE = 512
SC_CORES = 2
SC_SUBCORES = 16
SC_WORKERS = SC_CORES * SC_SUBCORES
SC_MAX_WINDOW = 128


def _cparams(sem):
    return pltpu.CompilerParams(dimension_semantics=sem, vmem_limit_bytes=VMEM_LIMIT)


def _split_bf16(a):
    hi = a.astype(BF16)
    lo = (a - hi.astype(F32)).astype(BF16)
    return hi, lo


def _dot3(a, b):
    ah, al = _split_bf16(a)
    bh, bl = _split_bf16(b)
    d = functools.partial(jnp.dot, preferred_element_type=F32)
    return d(ah, bh) + (d(ah, bl) + d(al, bh))


def _pack_pairs(y):
    a = lax.bitcast_convert_type(y[:, :HALF].astype(BF16).astype(F32), I32)
    b = lax.bitcast_convert_type(y[:, HALF:].astype(BF16).astype(F32), I32)
    return a | lax.shift_right_logical(b, jnp.int32(16))


def _unpack_pairs(w):
    a = lax.bitcast_convert_type(w & jnp.int32(-65536), F32)
    b = lax.bitcast_convert_type(lax.shift_left(w, jnp.int32(16)), F32)
    return a, b


def _ada_kernel(c_ref, w_ref, b_ref, o_ref):
    c = c_ref[...]
    s = c * jax.nn.sigmoid(c)
    o_ref[...] = _dot3(s, w_ref[...]) + b_ref[...]


def _ada(c_all, w_ada, b_ada):
    r, d = c_all.shape
    n = w_ada.shape[1]
    tn = 1024
    return pl.pallas_call(
        _ada_kernel,
        out_shape=jax.ShapeDtypeStruct((r, n), F32),
        grid=(n // tn,),
        in_specs=[pl.BlockSpec((r, d), lambda j: (0, 0)),
                  pl.BlockSpec((d, tn), lambda j: (0, j)),
                  pl.BlockSpec((1, tn), lambda j: (0, j))],
        out_specs=pl.BlockSpec((r, tn), lambda j: (0, j)),
        compiler_params=_cparams(("arbitrary",)),
        name="ada",
    )(c_all, w_ada, b_ada.reshape(1, n))


def _inproj_kernel(x_ref, mod_ref, nw_ref, w_ref, o_ref):
    x = x_ref[0]
    mod = mod_ref[0]
    h = x * lax.rsqrt(jnp.mean(x * x, axis=-1, keepdims=True) + RMS_EPS) * nw_ref[...]
    h = h * (1.0 + mod[1:2]) + mod[0:1]
    hb = h.astype(BF16)
    for j in range(IN_COLS // INPROJ_TN):
        sl = slice(j * INPROJ_TN, (j + 1) * INPROJ_TN)
        o_ref[0, :, sl] = jnp.dot(hb, w_ref[:, sl], preferred_element_type=F32).astype(BF16)


def _inproj(x, mod, nw, w_in_b, tm):
    b, t, d = x.shape
    return pl.pallas_call(
        _inproj_kernel,
        out_shape=jax.ShapeDtypeStruct((b, t, IN_COLS), BF16),
        grid=(b, t // tm),
        in_specs=[pl.BlockSpec((1, tm, d), lambda i, j: (i, j, 0)),
                  pl.BlockSpec((1, 6, d), lambda i, j: (i, 0, 0)),
                  pl.BlockSpec((1, d), lambda i, j: (0, 0)),
                  pl.BlockSpec((d, IN_COLS), lambda i, j: (0, 0), pipeline_mode=pl.Buffered(1))],
        out_specs=pl.BlockSpec((1, tm, IN_COLS), lambda i, j: (i, j, 0)),
        compiler_params=_cparams(("arbitrary", "arbitrary")),
        name="inproj",
    )(x, mod, nw, w_in_b)


def _head_inv_rms(xf, bd):
    hi, lo = _split_bf16(xf * xf)
    ssq = jnp.dot(hi, bd, preferred_element_type=F32) + jnp.dot(lo, bd, preferred_element_type=F32)
    return lax.rsqrt(ssq * (1.0 / HEAD_DIM) + RMS_EPS)


def _mixer_body(q_ref, k_ref, v_ref, bg_ref, c_ref, u_ref, ga_ref, gc_ref,
                knw_ref, qsc_ref, bd_ref, bias_ref, sink_ref, cw_ref,
                kpast_ref, vpast_ref, cpast_ref, upast_ref,
                merged_ref, knew_ref, vnew_ref, cnew_ref,
                kq_buf, vt_buf, attn_buf, *, tq, nq, stateful):
    hb = WINDOW
    pw = UNIT_Q
    t = pl.program_id(1)
    bd = bd_ref[...]

    k = k_ref[0].astype(F32)
    kn = k * _head_inv_rms(k, bd) * knw_ref[...]
    kq = (kn * qsc_ref[...]).astype(BF16)
    vb = v_ref[0]
    vt = vb.astype(F32).T.astype(BF16)

    if stateful:
        kp = kpast_ref[0]
        vp = vpast_ref[0]
        for kv in range(N_KV):
            kq_buf[kv, hb + tq:] = jnp.zeros((KEY_WIN - hb - tq, HEAD_DIM), BF16)
        vt_buf[:, hb + tq:] = jnp.zeros((KV_COLS, KEY_WIN - hb - tq), BF16)
        u_hist = jnp.concatenate([jnp.zeros((6, D_MODEL), F32), cpast_ref[0]], axis=0)
        knew_ref[0] = jnp.concatenate([kp[tq:], kn], axis=0)
        vnew_ref[0] = jnp.concatenate([vp[tq:], vb.astype(F32)], axis=0)
    else:
        kraw = kpast_ref[0].astype(F32)
        kp = kraw * _head_inv_rms(kraw, bd) * knw_ref[...]
        vp = vpast_ref[0].astype(F32)
        u_hist = jnp.where(t == 0, 0.0, cpast_ref[0].astype(F32) * upast_ref[0].astype(F32))
        knew_ref[0] = kn[tq - hb:]
        vnew_ref[0] = vb[tq - hb:].astype(F32)
    _fill_keys(kq_buf, vt_buf, kp, vp, kq, vt, qsc_ref, tq)
    q = q_ref[0]
    _attention_units(lambda kv: q[:, kv * KV_COLS:(kv + 1) * KV_COLS].astype(F32), kq_buf, vt_buf,
                     bias_ref, sink_ref, bd, attn_buf, tq, nq, None if stateful else t)
    cu = c_ref[0].astype(F32) * u_ref[0].astype(F32)
    conv, u_all = _short_conv(u_hist, cu, cw_ref[...], tq)
    cnew_ref[0] = u_all[tq + 6:tq + 8]
    merged = (jax.nn.sigmoid(ga_ref[0].astype(F32)) * attn_buf[...]
              + jax.nn.sigmoid(gc_ref[0].astype(F32)) * (bg_ref[0].astype(F32) * conv))
    merged_ref[0] = merged.astype(BF16)


def _fill_keys(kq_buf, vt_buf, kp, vp, kq, vt, qsc_ref, tq):
    hb = WINDOW
    kqp = (kp * qsc_ref[...]).astype(BF16)
    for kv in range(N_KV):
        kq_buf[kv, 0:hb] = kqp[:, kv * HEAD_DIM:(kv + 1) * HEAD_DIM]
        kq_buf[kv, hb:hb + tq] = kq[:, kv * HEAD_DIM:(kv + 1) * HEAD_DIM]
    vt_buf[:, 0:hb] = vp.T.astype(BF16)
    vt_buf[:, hb:hb + tq] = vt


def _attention_units(q_group, kq_buf, vt_buf, bias_ref, sink_ref, bd, attn_buf, tq, nq, t_first, after_unit=None):
    pw = UNIT_Q
    for kv in range(N_KV):
        qf = q_group(kv)
        qn = (qf * _head_inv_rms(qf, bd)).astype(BF16)
        for u in range(tq // nq):
            r0 = u * nq
            parts = [qn[r0:r0 + nq, g * HEAD_DIM:(g + 1) * HEAD_DIM] for g in range(GROUP)]
            if nq < pw:
                zpad = jnp.zeros((pw - nq, HEAD_DIM), BF16)
                parts = [x for p_ in parts for x in (p_, zpad)]
            qs = jnp.concatenate(parts, axis=0)
            kw = kq_buf[kv, r0:r0 + KEY_WIN]
            st = lax.dot_general(kw, qs, (((1,), (1,)), ((), ())), preferred_element_type=F32)
            if t_first is not None and u == 0:
                bias = jnp.where(t_first == 0, bias_ref[kv + N_KV], bias_ref[kv])
            else:
                bias = bias_ref[kv]
            st = st + bias
            sink = sink_ref[kv]
            m = jnp.maximum(jnp.max(st, axis=0, keepdims=True), sink)
            p = jnp.exp(st - m)
            den = jnp.sum(p, axis=0, keepdims=True) + jnp.exp(sink - m)
            ot = jnp.dot(vt_buf[kv * HEAD_DIM:(kv + 1) * HEAD_DIM, r0:r0 + KEY_WIN], p.astype(BF16),
                         preferred_element_type=F32) / den
            for gp in range(GROUP // 2):
                blk = jnp.concatenate([ot[:, (2 * gp) * pw:(2 * gp + 1) * pw],
                                       ot[:, (2 * gp + 1) * pw:(2 * gp + 2) * pw]], axis=0)
                c0 = (kv * GROUP + 2 * gp) * HEAD_DIM
                attn_buf[r0:r0 + nq, c0:c0 + 2 * HEAD_DIM] = blk.T[:nq]
            if after_unit is not None:
                after_unit(kv * (tq // nq) + u)


def _short_conv(u_hist, cu, cw, tq):
    u_all = jnp.concatenate([u_hist, cu], axis=0)
    conv = cw[0:1] * u_all[6:6 + tq] + cw[1:2] * u_all[7:7 + tq] + cw[2:3] * u_all[8:8 + tq]
    return conv, u_all


def _front_body(x_ref, xp_ref, mod_ref, nw_ref, w_ref, knw_ref, qsc_ref, bd_ref, bias_ref, sink_ref, cw_ref,
                merged_ref, knew_ref, vnew_ref, cnew_ref,
                kq_buf, vt_buf, u_buf, attn_buf, pbuf, *, tq, nq):
    hb = WINDOW
    d = D_MODEL
    t = pl.program_id(1)
    bd = bd_ref[...]
    mod = mod_ref[0]

    @pl.when(t == 0)
    def _():
        u_buf[0] = jnp.zeros((8, d), F32)

    def modnorm(x):
        h = x * lax.rsqrt(jnp.mean(x * x, axis=-1, keepdims=True) + RMS_EPS) * nw_ref[...]
        return (h * (1.0 + mod[1:2]) + mod[0:1]).astype(BF16)

    def proj(hrows, c0, width):
        return jnp.dot(hrows, w_ref[:, c0:c0 + width], preferred_element_type=F32)

    hbf = modnorm(x_ref[0])
    kv_t = proj(hbf, 6 * d, 2 * KV_COLS)
    kv_p = proj(modnorm(xp_ref[0]), 6 * d, 2 * KV_COLS)
    k, v = kv_t[:, :KV_COLS], kv_t[:, KV_COLS:]
    kn = k * _head_inv_rms(k, bd) * knw_ref[...]
    kq = (kn * qsc_ref[...]).astype(BF16)
    kraw = kv_p[:, :KV_COLS]
    kp = kraw * _head_inv_rms(kraw, bd) * knw_ref[...]
    knew_ref[0] = kn[tq - hb:]
    vnew_ref[0] = v[tq - hb:]
    _fill_keys(kq_buf, vt_buf, kp, kv_p[:, KV_COLS:], kq, v.T.astype(BF16), qsc_ref, tq)

    n_units = N_KV * (tq // nq)
    q_chunks = [[g * KV_COLS + c for c in range(0, KV_COLS, PROJ_CHUNK)] for g in range(N_KV)]
    rest_chunks = list(range(d, 6 * d, PROJ_CHUNK))

    def issue(c0):
        pbuf[:, c0:c0 + PROJ_CHUNK] = proj(hbf, c0, PROJ_CHUNK)

    for c0 in q_chunks[0]:
        issue(c0)
    n_rest = len(rest_chunks)

    def after_unit(i):
        kv, u = divmod(i, tq // nq)
        if u == 0 and kv + 1 < N_KV:
            for c0 in q_chunks[kv + 1]:
                issue(c0)
        for c0 in rest_chunks[i * n_rest // n_units:(i + 1) * n_rest // n_units]:
            issue(c0)

    _attention_units(lambda kv: pbuf[:, kv * KV_COLS:(kv + 1) * KV_COLS], kq_buf, vt_buf,
                     bias_ref, sink_ref, bd, attn_buf, tq, nq, t, after_unit)

    cu = pbuf[:, 2 * d:3 * d] * pbuf[:, 3 * d:4 * d]
    conv, u_all = _short_conv(u_buf[t % 2], cu, cw_ref[...], tq)
    cnew_ref[0] = u_all[tq + 6:tq + 8]
    u_buf[(t + 1) % 2] = u_all[tq:tq + 8]
    merged = (jax.nn.sigmoid(pbuf[:, 4 * d:5 * d]) * attn_buf[...]
              + jax.nn.sigmoid(pbuf[:, 5 * d:6 * d]) * (pbuf[:, d:2 * d] * conv))
    merged_ref[0] = merged.astype(BF16)


def _front(x, mod, nw, w_in_b, knw, qsc, bias_tab, sink_tab, conv_w, tq, nq):
    b, t, d = x.shape
    r = np.arange(KV_COLS) // HEAD_DIM
    bd = jnp.asarray((r[:, None] == r[None, :]).astype(np.float32), BF16)
    const2 = lambda shp: pl.BlockSpec(shp, lambda i, s: (0, 0))
    const3 = lambda shp: pl.BlockSpec(shp, lambda i, s: (0, 0, 0))
    per_b = lambda shp: pl.BlockSpec(shp, lambda i, s: (i, 0, 0))
    kw_ = tq // WINDOW
    in_specs = [pl.BlockSpec((1, tq, d), lambda i, s: (i, s, 0)),
                pl.BlockSpec((1, WINDOW, d), lambda i, s: (i, jnp.maximum(s * kw_ - 1, 0), 0)),
                per_b((1, 6, d)), const2((1, d)),
                pl.BlockSpec((d, IN_COLS), lambda i, s: (0, 0), pipeline_mode=pl.Buffered(1)),
                const2((1, KV_COLS)), const2((1, KV_COLS)), const2((KV_COLS, KV_COLS)),
                const3(bias_tab.shape), const3(sink_tab.shape), const2((3, d))]
    out_shape = (jax.ShapeDtypeStruct((b, t, d), BF16),
                 jax.ShapeDtypeStruct((b, WINDOW, KV_COLS), F32),
                 jax.ShapeDtypeStruct((b, WINDOW, KV_COLS), F32),
                 jax.ShapeDtypeStruct((b, 2, d), F32))
    out_specs = (pl.BlockSpec((1, tq, d), lambda i, s: (i, s, 0)),
                 per_b((1, WINDOW, KV_COLS)), per_b((1, WINDOW, KV_COLS)), per_b((1, 2, d)))
    return pl.pallas_call(
        functools.partial(_front_body, tq=tq, nq=nq),
        out_shape=out_shape,
        grid=(b, t // tq),
        in_specs=in_specs,
        out_specs=out_specs,
        scratch_shapes=[pltpu.VMEM((N_KV, WINDOW + tq, HEAD_DIM), BF16),
                        pltpu.VMEM((KV_COLS, WINDOW + tq), BF16),
                        pltpu.VMEM((2, 8, d), F32),
                        pltpu.VMEM((tq, d), F32),
                        pltpu.VMEM((tq, 6 * d), F32)],
        compiler_params=_cparams(("arbitrary", "arbitrary")),
        name="front",
    )(x, x, mod, nw, w_in_b, knw, qsc, bd, bias_tab, sink_tab, conv_w)


def _mixer(proj, knw, qsc, bias_tab, sink_tab, conv_w, state, tq, nq):
    b, t, _ = proj.shape
    d = D_MODEL
    stateful = state is not None
    key_rows = max(WINDOW + tq, KEY_WIN)
    r = np.arange(KV_COLS) // HEAD_DIM
    bd = jnp.asarray((r[:, None] == r[None, :]).astype(np.float32), BF16)
    wide = lambda j: pl.BlockSpec((1, tq, d), lambda i, s, j=j: (i, s, j))
    kvspec = lambda j: pl.BlockSpec((1, tq, KV_COLS), lambda i, s, j=j: (i, s, j))
    const2 = lambda shp: pl.BlockSpec(shp, lambda i, s: (0, 0))
    const3 = lambda shp: pl.BlockSpec(shp, lambda i, s: (0, 0, 0))
    per_b = lambda shp: pl.BlockSpec(shp, lambda i, s: (i, 0, 0))
    kvblk = 6 * d // KV_COLS
    if stateful:
        hist_specs = [per_b((1, WINDOW, KV_COLS)), per_b((1, WINDOW, KV_COLS)), per_b((1, 2, d)), per_b((1, 2, d))]
        hist_args = [state[0], state[1], state[2], state[2]]
    else:
        kw_ = tq // WINDOW
        prev_kv = lambda j: pl.BlockSpec((1, WINDOW, KV_COLS),
                                         lambda i, s, j=j: (i, jnp.maximum(s * kw_ - 1, 0), j))
        prev8 = lambda j: pl.BlockSpec((1, 8, d), lambda i, s, j=j: (i, jnp.maximum(s * (tq // 8) - 1, 0), j))
        hist_specs = [prev_kv(kvblk), prev_kv(kvblk + 1), prev8(2), prev8(3)]
        hist_args = [proj, proj, proj, proj]
    in_specs = [wide(0), kvspec(kvblk), kvspec(kvblk + 1), wide(1), wide(2), wide(3), wide(4), wide(5),
                const2((1, KV_COLS)), const2((1, KV_COLS)), const2((KV_COLS, KV_COLS)),
                const3(bias_tab.shape), const3(sink_tab.shape), const2((3, d))] + hist_specs
    out_shape = (jax.ShapeDtypeStruct((b, t, d), BF16),
                 jax.ShapeDtypeStruct((b, WINDOW, KV_COLS), F32),
                 jax.ShapeDtypeStruct((b, WINDOW, KV_COLS), F32),
                 jax.ShapeDtypeStruct((b, 2, d), F32))
    out_specs = (pl.BlockSpec((1, tq, d), lambda i, s: (i, s, 0)),
                 per_b((1, WINDOW, KV_COLS)), per_b((1, WINDOW, KV_COLS)), per_b((1, 2, d)))
    return pl.pallas_call(
        functools.partial(_mixer_body, tq=tq, nq=nq, stateful=stateful),
        out_shape=out_shape,
        grid=(b, t // tq),
        in_specs=in_specs,
        out_specs=out_specs,
        scratch_shapes=[pltpu.VMEM((N_KV, key_rows, HEAD_DIM), BF16),
                        pltpu.VMEM((KV_COLS, key_rows), BF16),
                        pltpu.VMEM((tq, d), F32)],
        compiler_params=_cparams(("arbitrary", "arbitrary")),
        name="mixer_state" if stateful else "mixer",
    )(proj, proj, proj, proj, proj, proj, proj, proj, knw, qsc, bd, bias_tab, sink_tab, conv_w, *hist_args)


def _route(logits):
    lane = lax.broadcasted_iota(I32, logits.shape, 1)
    neg = -jnp.inf
    big = jnp.int32(1 << 20)
    gl = jnp.where(lane < N_GROUPS, logits, neg)
    gmax = jnp.max(gl, axis=-1, keepdims=True)
    g_idx = jnp.min(jnp.where(gl == gmax, lane, big), axis=-1, keepdims=True)
    g_w = 1.0 / jnp.sum(jnp.exp(gl - gmax), axis=-1, keepdims=True)
    lo = N_GROUPS + g_idx * EPG
    el = jnp.where((lane >= lo) & (lane < lo + EPG), logits, neg)
    m1 = jnp.max(el, axis=-1, keepdims=True)
    i1 = jnp.min(jnp.where(el == m1, lane, big), axis=-1, keepdims=True)
    el2 = jnp.where(lane == i1, neg, el)
    m2 = jnp.max(el2, axis=-1, keepdims=True)
    i2 = jnp.min(jnp.where(el2 == m2, lane, big), axis=-1, keepdims=True)
    r = jnp.exp(m2 - m1)
    w1 = 1.0 / (1.0 + r)
    w2 = r / (1.0 + r)
    return i1 - N_GROUPS, i2 - N_GROUPS, g_w * w1, g_w * w2


def _outproj_kernel(m_ref, x_ref, mod_ref, wo_ref, nw_ref, wr_ref, br_ref,
                    x1_ref, h2_ref, eid_ref, ew_ref):
    mod = mod_ref[0]
    mix = jnp.dot(m_ref[0], wo_ref[...], preferred_element_type=F32)
    x1 = x_ref[0] + mod[2:3] * mix
    x1_ref[0] = x1
    h = x1 * lax.rsqrt(jnp.mean(x1 * x1, axis=-1, keepdims=True) + RMS_EPS) * nw_ref[...]
    h = h * (1.0 + mod[4:5]) + mod[3:4]
    h2_ref[...] = _pack_pairs(h)
    tm = h.shape[0]
    h_hi, h_lo = _split_bf16(h)
    prod = jnp.dot(jnp.concatenate([h_hi, h_lo], axis=0), wr_ref[...], preferred_element_type=F32)
    logits = prod[:tm, :LANES] + (prod[:tm, LANES:] + prod[tm:, :LANES]) + br_ref[...]
    e1, e2, w1, w2 = _route(logits)
    lane8 = lax.broadcasted_iota(I32, (h.shape[0], 8), 1)
    eid_ref[...] = jnp.where(lane8 == 0, e1, jnp.where(lane8 == 1, e2, 0))
    ew_ref[...] = jnp.where(lane8 == 0, w1, jnp.where(lane8 == 1, w2, 0.0))


def _outproj(merged, x, mod, w_out_b, nw, w_r, b_r, tm):
    b, t, d = x.shape
    nt = t // tm
    flat = lambda i, j: (i * nt + j, 0)
    return pl.pallas_call(
        _outproj_kernel,
        out_shape=(jax.ShapeDtypeStruct((b, t, d), F32),
                   jax.ShapeDtypeStruct((b * t, HALF), I32),
                   jax.ShapeDtypeStruct((b * t, 8), I32),
                   jax.ShapeDtypeStruct((b * t, 8), F32)),
        grid=(b, nt),
        in_specs=[pl.BlockSpec((1, tm, d), lambda i, j: (i, j, 0)),
                  pl.BlockSpec((1, tm, d), lambda i, j: (i, j, 0)),
                  pl.BlockSpec((1, 6, d), lambda i, j: (i, 0, 0)),
                  pl.BlockSpec((d, d), lambda i, j: (0, 0)),
                  pl.BlockSpec((1, d), lambda i, j: (0, 0)),
                  pl.BlockSpec((d, 2 * LANES), lambda i, j: (0, 0)),
                  pl.BlockSpec((1, LANES), lambda i, j: (0, 0))],
        out_specs=(pl.BlockSpec((1, tm, d), lambda i, j: (i, j, 0)),
                   pl.BlockSpec((tm, HALF), flat),
                   pl.BlockSpec((tm, 8), flat),
                   pl.BlockSpec((tm, 8), flat)),
        compiler_params=_cparams(("arbitrary", "arbitrary")),
        name="outproj",
    )(merged, x, mod, w_out_b, nw, w_r, b_r)


def _rank_kernel(eid_ref, tri_ref, upper_ref, dest_ref, tot_ref, cnt, starts, *, block):
    ph = pl.program_id(0)
    i = pl.program_id(1)
    tm = eid_ref.shape[0]
    lane = lax.broadcasted_iota(I32, (tm, LANES), 1)
    e0 = eid_ref[:, 0:1]
    e1 = eid_ref[:, 1:2]
    hot0 = lane == e0
    hot1 = lane == e1
    onehot = jnp.where(hot0 | hot1, 1.0, 0.0)
    colsum = jnp.sum(onehot, axis=0, keepdims=True)

    @pl.when((ph == 0) & (i == 0))
    def _():
        cnt[...] = jnp.zeros_like(cnt)

    @pl.when(ph == 0)
    def _():
        cnt[0:1] = cnt[0:1] + colsum

    @pl.when((ph == 1) & (i == 0))
    def _():
        tot = cnt[0:1]
        tot_ref[...] = tot.astype(I32)
        nblk = jnp.floor((tot + (block - 1)) * (1.0 / block))
        hi = jnp.floor(nblk * (1.0 / 16.0))
        lo = nblk - hi * 16.0
        up = upper_ref[...]
        excl = (jnp.dot(jnp.broadcast_to(hi, (8, LANES)).astype(BF16), up, preferred_element_type=F32) * 16.0
                + jnp.dot(jnp.broadcast_to(lo, (8, LANES)).astype(BF16), up, preferred_element_type=F32))
        starts[...] = excl * float(block)
        cnt[...] = jnp.zeros_like(cnt)

    @pl.when(ph == 1)
    def _():
        prefix = jnp.dot(tri_ref[...], onehot.astype(BF16), preferred_element_type=F32)
        pos = prefix + (starts[0:1] + cnt[0:1])
        d0 = jnp.sum(jnp.where(hot0, pos, 0.0), axis=-1, keepdims=True).astype(I32)
        d1 = jnp.sum(jnp.where(hot1, pos, 0.0), axis=-1, keepdims=True).astype(I32)
        lane8 = lax.broadcasted_iota(I32, (tm, 8), 1)
        dest_ref[...] = jnp.where(lane8 == 0, d0, jnp.where(lane8 == 1, d1, 0))
        cnt[0:1] = cnt[0:1] + colsum


def _rank(eid, block):
    n = eid.shape[0]
    tm = math.gcd(n, RANK_TILE)
    r = np.arange(tm)
    tri = jnp.asarray((r[:, None] > r[None, :]).astype(np.float32), BF16)
    l = np.arange(LANES)
    upper = jnp.asarray(((l[:, None] < l[None, :]) & (l[None, :] < N_EXPERTS)).astype(np.float32), BF16)
    return pl.pallas_call(
        functools.partial(_rank_kernel, block=block),
        out_shape=(jax.ShapeDtypeStruct((n, 8), I32), jax.ShapeDtypeStruct((1, LANES), I32)),
        grid=(2, n // tm),
        in_specs=[pl.BlockSpec((tm, 8), lambda p, i: (i, 0)),
                  pl.BlockSpec((tm, tm), lambda p, i: (0, 0)),
                  pl.BlockSpec((LANES, LANES), lambda p, i: (0, 0))],
        out_specs=(pl.BlockSpec((tm, 8), lambda p, i: (i * p, 0)),
                   pl.BlockSpec((1, LANES), lambda p, i: (0, 0))),
        scratch_shapes=[pltpu.VMEM((8, LANES), F32), pltpu.VMEM((8, LANES), F32)],
        compiler_params=_cparams(("arbitrary", "arbitrary")),
        name="rank",
    )(eid, tri, upper)


def _expert_kernel(start_ref, nblk_ref, xs_hbm, wg_ref, wu_ref, wd_ref, ys_hbm,
                   xbuf, ybuf, wg_s, wu_s, wd_s, sem_in, sem_out, *, block):
    e = pl.program_id(0)
    n = nblk_ref[e]
    base = start_ref[e]
    total = start_ref[N_EXPERTS - 1] + nblk_ref[N_EXPERTS - 1]

    def in_copy(g):
        rows = pl.ds(pl.multiple_of(g * block, block), block)
        return pltpu.make_async_copy(xs_hbm.at[rows], xbuf.at[g % 2], sem_in.at[g % 2])

    def out_copy(g):
        rows = pl.ds(pl.multiple_of(g * block, block), block)
        return pltpu.make_async_copy(ybuf.at[g % 2], ys_hbm.at[rows], sem_out.at[g % 2])

    @pl.when((e == 0) & (total > 0))
    def _():
        in_copy(0).start()

    @pl.when(n > 0)
    def _():
        wg_s[...] = wg_ref[0].astype(BF16)
        wu_s[...] = wu_ref[0].astype(BF16)
        wd_s[...] = wd_ref[0].astype(BF16)

        def body(i, carry):
            g = base + i
            slot = g % 2
            in_copy(g).wait()

            @pl.when(g + 1 < total)
            def _():
                in_copy(g + 1).start()

            @pl.when(g >= 2)
            def _():
                out_copy(g - 2).wait()

            a, c = _unpack_pairs(xbuf[slot])
            x = jnp.concatenate([a.astype(BF16), c.astype(BF16)], axis=1)
            gate = jnp.dot(x, wg_s[...], preferred_element_type=F32)
            up = jnp.dot(x, wu_s[...], preferred_element_type=F32)
            hmid = (gate * jax.nn.sigmoid(gate) * up).astype(BF16)
            ybuf[slot] = _pack_pairs(jnp.dot(hmid, wd_s[...], preferred_element_type=F32))
            out_copy(g).start()
            return carry

        lax.fori_loop(0, n, body, 0)

    @pl.when(e == N_EXPERTS - 1)
    def _():
        @pl.when(total >= 2)
        def _():
            out_copy(total - 2).wait()

        @pl.when(total >= 1)
        def _():
            out_copy(total - 1).wait()


def _experts(xs, start_blk, nblk, w_gate, w_up, w_down, block):
    n_rows = xs.shape[0]
    wblk = lambda e, st, nb: (e, 0, 0)
    grid_spec = pltpu.PrefetchScalarGridSpec(
        num_scalar_prefetch=2,
        grid=(N_EXPERTS,),
        in_specs=[pl.BlockSpec(memory_space=pl.ANY),
                  pl.BlockSpec((1, D_MODEL, D_EXPERT), wblk),
                  pl.BlockSpec((1, D_MODEL, D_EXPERT), wblk),
                  pl.BlockSpec((1, D_EXPERT, D_MODEL), wblk)],
        out_specs=pl.BlockSpec(memory_space=pl.ANY),
        scratch_shapes=[pltpu.VMEM((2, block, HALF), I32),
                        pltpu.VMEM((2, block, HALF), I32),
                        pltpu.VMEM((D_MODEL, D_EXPERT), BF16),
                        pltpu.VMEM((D_MODEL, D_EXPERT), BF16),
                        pltpu.VMEM((D_EXPERT, D_MODEL), BF16),
                        pltpu.SemaphoreType.DMA((2,)),
                        pltpu.SemaphoreType.DMA((2,))])
    return pl.pallas_call(
        functools.partial(_expert_kernel, block=block),
        out_shape=jax.ShapeDtypeStruct((n_rows, HALF), I32),
        grid_spec=grid_spec,
        compiler_params=_cparams(("arbitrary",)),
        name="experts",
    )(start_blk, nblk, xs, w_gate, w_up, w_down)


def _final_kernel(x1_ref, y0_ref, y1_ref, ew_ref, mod_ref, o_ref):
    a0, b0 = _unpack_pairs(y0_ref[...])
    a1, b1 = _unpack_pairs(y1_ref[...])
    w0 = ew_ref[:, 0:1]
    w1 = ew_ref[:, 1:2]
    moe = jnp.concatenate([w0 * a0 + w1 * a1, w0 * b0 + w1 * b1], axis=1)
    o_ref[0] = x1_ref[0] + mod_ref[0][5:6] * moe


def _final(x1, y0, y1, ew, mod, tm):
    b, t, d = x1.shape
    nt = t // tm
    flat = lambda i, j: (i * nt + j, 0)
    return pl.pallas_call(
        _final_kernel,
        out_shape=jax.ShapeDtypeStruct((b, t, d), F32),
        grid=(b, nt),
        in_specs=[pl.BlockSpec((1, tm, d), lambda i, j: (i, j, 0)),
                  pl.BlockSpec((tm, HALF), flat),
                  pl.BlockSpec((tm, HALF), flat),
                  pl.BlockSpec((tm, 8), flat),
                  pl.BlockSpec((1, 6, d), lambda i, j: (i, 0, 0))],
        out_specs=pl.BlockSpec((1, tm, d), lambda i, j: (i, j, 0)),
        compiler_params=_cparams(("arbitrary", "arbitrary")),
        name="final",
    )(x1, y0, y1, ew, mod)


def _sc_window(rows_per_worker):
    for w in range(SC_MAX_WINDOW, 7, -8):
        if rows_per_worker % w == 0:
            return w
    raise ValueError(f"no SparseCore window divides {rows_per_worker} rows per worker")


def _sc_split(idx):
    n = idx.shape[0]
    per = n // SC_WORKERS
    assert per * SC_WORKERS == n
    win = _sc_window(per)
    return idx.reshape(SC_WORKERS, per // win, win), per // win, win


def _sc_worker_id():
    return lax.axis_index("s") * SC_CORES + lax.axis_index("c")


def _dispatch_rows(h2_groups, dest_groups, n_rows):
    splits = [(_sc_split(d[:, 0]), _sc_split(d[:, 1])) for d in dest_groups]
    ng = len(h2_groups)
    scratch = []
    for (_, _, win), _ in splits:
        scratch += [pltpu.VMEM((win,), I32), pltpu.VMEM((win,), I32), pltpu.VMEM((win, HALF), I32)]

    @functools.partial(
        pl.kernel,
        mesh=plsc.VectorSubcoreMesh(core_axis_name="c", subcore_axis_name="s"),
        out_type=jax.ShapeDtypeStruct((n_rows, HALF), I32),
        scratch_types=scratch,
        name="sc_dispatch",
    )
    def k(*refs):
        x_refs, idx_refs, o_hbm, bufs = refs[:ng], refs[ng:3 * ng], refs[3 * ng], refs[3 * ng + 1:]
        wid = _sc_worker_id()
        for g in range(ng):
            (_, nwin, win), _ = splits[g]
            x_hbm, d0_hbm, d1_hbm = x_refs[g], idx_refs[2 * g], idx_refs[2 * g + 1]
            i0_v, i1_v, rows_v = bufs[3 * g:3 * g + 3]

            @pl.loop(0, nwin)
            def _(j, nwin=nwin, win=win, x_hbm=x_hbm, d0_hbm=d0_hbm, d1_hbm=d1_hbm,
                  i0_v=i0_v, i1_v=i1_v, rows_v=rows_v):
                base = pl.multiple_of((wid * nwin + j) * win, 8)
                pltpu.sync_copy(d0_hbm.at[wid, j], i0_v)
                pltpu.sync_copy(d1_hbm.at[wid, j], i1_v)
                pltpu.sync_copy(x_hbm.at[pl.ds(base, win)], rows_v)
                pltpu.sync_copy(rows_v, o_hbm.at[i0_v])
                pltpu.sync_copy(rows_v, o_hbm.at[i1_v])

    idx_args = []
    for (s0, s1) in splits:
        idx_args += [s0[0], s1[0]]
    return k(*h2_groups, *idx_args)


def _collect_rows(ys, dest_groups):
    splits = [(_sc_split(d[:, 0]), _sc_split(d[:, 1])) for d in dest_groups]
    ng = len(dest_groups)
    outs, scratch = [], []
    for d, ((_, _, win), _) in zip(dest_groups, splits):
        o = jax.ShapeDtypeStruct((d.shape[0], HALF), I32)
        outs += [o, o]
        scratch += [pltpu.VMEM((win,), I32), pltpu.VMEM((win, HALF), I32)]

    @functools.partial(
        pl.kernel,
        mesh=plsc.VectorSubcoreMesh(core_axis_name="c", subcore_axis_name="s"),
        out_type=tuple(outs),
        scratch_types=scratch,
        name="sc_collect",
    )
    def k(*refs):
        ys_hbm, idx_refs, out_refs, bufs = refs[0], refs[1:1 + 2 * ng], refs[1 + 2 * ng:1 + 4 * ng], refs[1 + 4 * ng:]
        wid = _sc_worker_id()
        for g in range(ng):
            (_, nwin, win), _ = splits[g]
            i_v, rows_v = bufs[2 * g:2 * g + 2]
            for kk in range(2):
                d_hbm, y_hbm = idx_refs[2 * g + kk], out_refs[2 * g + kk]

                @pl.loop(0, nwin)
                def _(j, nwin=nwin, win=win, d_hbm=d_hbm, y_hbm=y_hbm, i_v=i_v, rows_v=rows_v):
                    base = pl.multiple_of((wid * nwin + j) * win, 8)
                    pltpu.sync_copy(d_hbm.at[wid, j], i_v)
                    pltpu.sync_copy(ys_hbm.at[i_v], rows_v)
                    pltpu.sync_copy(rows_v, y_hbm.at[pl.ds(base, win)])

    idx_args = []
    for (s0, s1) in splits:
        idx_args += [s0[0], s1[0]]
    res = k(ys, *idx_args)
    return [(res[2 * g], res[2 * g + 1]) for g in range(ng)]


def _t5_bucket(rel):
    half = N_BUCKETS // 2
    max_exact = half // 2
    n = jnp.abs(rel)
    far = max_exact + (jnp.log(jnp.maximum(n, 1).astype(F32) / max_exact)
                       / math.log(MAX_DISTANCE / max_exact) * (half - max_exact)).astype(I32)
    far = jnp.minimum(far, half - 1)
    return jnp.where(rel > 0, half, 0) + jnp.where(n < max_exact, n, far)


def _bias_table(rel_bias, cq, nq, no_history):
    nk = WINDOW + cq
    j = jnp.arange(KEY_WIN)[:, None]
    c = jnp.arange(UNIT_Q)[None, :]
    jj = j - (c // cq) * cq
    valid = (jj >= 0) & (jj < nk) & (c < nq)
    if no_history:
        valid = valid & (j >= WINDOW)
    rel = jj - WINDOW - (c % cq)
    onehot = (_t5_bucket(rel)[:, :, None] == jnp.arange(N_BUCKETS)).astype(F32)
    bias = jnp.einsum("jcb,bh->jch", onehot, rel_bias.astype(F32), precision=lax.Precision.HIGHEST)
    bias = jnp.where(valid[:, :, None], bias, -jnp.inf)
    bias = jnp.transpose(bias.reshape(KEY_WIN, UNIT_Q, N_KV, GROUP), (2, 0, 3, 1))
    return bias.reshape(N_KV, KEY_WIN, GROUP * UNIT_Q)


def _sink_table(sinks):
    s = sinks.astype(F32).reshape(N_KV, 1, GROUP, 1)
    return jnp.broadcast_to(s, (N_KV, 1, GROUP, UNIT_Q)).reshape(N_KV, 1, GROUP * UNIT_Q)


def kernel(x_prompt, x_sample, state_attn_k, state_attn_v, state_conv, c_prompt, c_sample,
           rel_bias, w_ada, b_ada, norm1_w, w_in, q_norm_w, k_norm_w, attn_sinks, conv_w,
           w_out, norm2_w, w_router_group, b_router_group, w_router_expert, b_router_expert,
           w_gate, w_up, w_down):
    depth = w_ada.shape[0]
    assert depth == 1
    bp, tp, d = x_prompt.shape
    bs, ts, _ = x_sample.shape
    n_p, n_s = bp * tp, bs * ts
    n_tok = n_p + n_s
    l = 0

    wi = w_in[l]
    qw, kw, vw, rest = wi[:, :d], wi[:, d:d + KV_COLS], wi[:, d + KV_COLS:d + 2 * KV_COLS], wi[:, d + 2 * KV_COLS:]
    w_in_b = jnp.concatenate([qw, rest, kw, vw], axis=1).astype(BF16)
    w_out_b = w_out[l].astype(BF16)
    w_r = jnp.concatenate([w_router_group[l],
                           jnp.transpose(w_router_expert[l], (1, 0, 2)).reshape(d, N_EXPERTS),
                           jnp.zeros((d, LANES - N_GROUPS - N_EXPERTS), F32)], axis=1)
    w_r_hi = lax.reduce_precision(w_r, exponent_bits=8, mantissa_bits=7)
    w_r = jnp.concatenate([w_r_hi.astype(BF16), (w_r - w_r_hi).astype(BF16)], axis=1)
    b_r = jnp.concatenate([b_router_group[l], b_router_expert[l].reshape(-1),
                           jnp.zeros((LANES - N_GROUPS - N_EXPERTS,), F32)]).reshape(1, LANES)
    knw = jnp.tile(k_norm_w[l], N_KV).reshape(1, KV_COLS)
    qsc = jnp.tile(q_norm_w[l] * (HEAD_DIM ** -0.5), N_KV).reshape(1, KV_COLS)
    n1w = norm1_w[l].reshape(1, d)
    n2w = norm2_w[l].reshape(1, d)

    mod = _ada(jnp.concatenate([c_prompt, c_sample], axis=0), w_ada[l], b_ada[l]).reshape(bp + bs, 6, d)
    mod_p, mod_s = mod[:bp], mod[bp:]

    proj_s = _inproj(x_sample, mod_s, n1w, w_in_b, ts)
    sink_tab = _sink_table(attn_sinks[l])
    bias_p = jnp.concatenate([_bias_table(rel_bias, CHUNK, UNIT_Q, False),
                              _bias_table(rel_bias, CHUNK, UNIT_Q, True)], axis=0)
    merged_p, k_p, v_p, c_p = _front(x_prompt, mod_p, n1w, w_in_b, knw, qsc, bias_p, sink_tab, conv_w[l],
                                     MIX_TILE, UNIT_Q)
    state = (state_attn_k[l].reshape(bs, WINDOW, KV_COLS), state_attn_v[l].reshape(bs, WINDOW, KV_COLS),
             state_conv[l])
    merged_s, k_s, v_s, c_s = _mixer(proj_s, knw, qsc, _bias_table(rel_bias, ts, ts, False), sink_tab, conv_w[l],
                                     state, ts, ts)

    x1_p, h2_p, eid_p, ew_p = _outproj(merged_p, x_prompt, mod_p, w_out_b, n2w, w_r, b_r, ROW_TILE)
    x1_s, h2_s, eid_s, ew_s = _outproj(merged_s, x_sample, mod_s, w_out_b, n2w, w_r, b_r, ts)

    dest, totals = _rank(jnp.concatenate([eid_p, eid_s], axis=0), EXPERT_BLOCK)
    dests = [dest[:n_p], dest[n_p:]]
    nblk = (totals[0, :N_EXPERTS] + EXPERT_BLOCK - 1) // EXPERT_BLOCK
    start_blk = (jnp.cumsum(nblk) - nblk).astype(I32)
    nb_max = -(-2 * n_tok // EXPERT_BLOCK) + N_EXPERTS

    xs = _dispatch_rows([h2_p, h2_s], dests, nb_max * EXPERT_BLOCK)
    ys = _experts(xs, start_blk, nblk.astype(I32), w_gate[l], w_up[l], w_down[l], EXPERT_BLOCK)
    (y0_p, y1_p), (y0_s, y1_s) = _collect_rows(ys, dests)

    y_p = _final(x1_p, y0_p, y1_p, ew_p, mod_p, ROW_TILE)
    y_s = _final(x1_s, y0_s, y1_s, ew_s, mod_s, ts)

    kv_shape = (1, -1, WINDOW, N_KV, HEAD_DIM)
    return (y_p, y_s, k_p.reshape(kv_shape), v_p.reshape(kv_shape), c_p[None],
            k_s.reshape(kv_shape), v_s.reshape(kv_shape), c_s[None])
```

```python
import functools
import math

import numpy as np
import jax
import jax.numpy as jnp
from jax import lax
from jax.experimental import pallas as pl
from jax.experimental.pallas import tpu as pltpu
from jax.experimental.pallas import tpu_sc as plsc

F32 = jnp.float32
BF16 = jnp.bfloat16
I32 = jnp.int32

D_MODEL = 1024
HEAD_DIM = 64
N_HEADS = 16
N_KV = 4
GROUP = 4
CHUNK = 64
WINDOW = 128
N_BUCKETS = 32
MAX_DISTANCE = 128
N_GROUPS = 8
EPG = 8
N_EXPERTS = 64
D_EXPERT = 512
RMS_EPS = 1e-6
KV_COLS = N_KV * HEAD_DIM
IN_COLS = 6 * D_MODEL + 2 * KV_COLS
HALF = D_MODEL // 2
LANES = 128

VMEM_LIMIT = 56 * 1024 * 1024
INPROJ_TN = 512
ROW_TILE = 512
MIX_TILE = 512
UNIT_Q = 2 * CHUNK
KEY_WIN = WINDOW + UNIT_Q
PROJ_CHUNK = 256
EXPERT_BLOCK = 256
RANK_TILE = 512
EXPERT_BUFS = 4
SC_CORES = 2
SC_SUBCORES = 16
SC_WORKERS = SC_CORES * SC_SUBCORES
SC_MAX_WINDOW = 128


def _cparams(sem):
    return pltpu.CompilerParams(dimension_semantics=sem, vmem_limit_bytes=VMEM_LIMIT)


def _split_bf16(a):
    hi = a.astype(BF16)
    lo = (a - hi.astype(F32)).astype(BF16)
    return hi, lo


def _dot3(a, b):
    ah, al = _split_bf16(a)
    bh, bl = _split_bf16(b)
    d = functools.partial(jnp.dot, preferred_element_type=F32)
    return d(ah, bh) + (d(ah, bl) + d(al, bh))


def _pack_pairs(y):
    a = lax.bitcast_convert_type(y[:, :HALF].astype(BF16).astype(F32), I32)
    b = lax.bitcast_convert_type(y[:, HALF:].astype(BF16).astype(F32), I32)
    return a | lax.shift_right_logical(b, jnp.int32(16))


def _unpack_pairs(w):
    a = lax.bitcast_convert_type(w & jnp.int32(-65536), F32)
    b = lax.bitcast_convert_type(lax.shift_left(w, jnp.int32(16)), F32)
    return a, b


def _ada_kernel(c_ref, w_ref, b_ref, o_ref):
    c = c_ref[...]
    s = c * jax.nn.sigmoid(c)
    o_ref[...] = _dot3(s, w_ref[...]) + b_ref[...]


def _ada(c_all, w_ada, b_ada):
    r, d = c_all.shape
    n = w_ada.shape[1]
    tn = 1024
    return pl.pallas_call(
        _ada_kernel,
        out_shape=jax.ShapeDtypeStruct((r, n), F32),
        grid=(n // tn,),
        in_specs=[pl.BlockSpec((r, d), lambda j: (0, 0)),
                  pl.BlockSpec((d, tn), lambda j: (0, j)),
                  pl.BlockSpec((1, tn), lambda j: (0, j))],
        out_specs=pl.BlockSpec((r, tn), lambda j: (0, j)),
        compiler_params=_cparams(("arbitrary",)),
        name="ada",
    )(c_all, w_ada, b_ada.reshape(1, n))


def _inproj_kernel(x_ref, mod_ref, nw_ref, w_ref, o_ref):
    x = x_ref[0]
    mod = mod_ref[0]
    h = x * lax.rsqrt(jnp.mean(x * x, axis=-1, keepdims=True) + RMS_EPS) * nw_ref[...]
    h = h * (1.0 + mod[1:2]) + mod[0:1]
    hb = h.astype(BF16)
    for j in range(IN_COLS // INPROJ_TN):
        sl = slice(j * INPROJ_TN, (j + 1) * INPROJ_TN)
        o_ref[0, :, sl] = jnp.dot(hb, w_ref[:, sl], preferred_element_type=F32).astype(BF16)


def _inproj(x, mod, nw, w_in_b, tm):
    b, t, d = x.shape
    return pl.pallas_call(
        _inproj_kernel,
        out_shape=jax.ShapeDtypeStruct((b, t, IN_COLS), BF16),
        grid=(b, t // tm),
        in_specs=[pl.BlockSpec((1, tm, d), lambda i, j: (i, j, 0)),
                  pl.BlockSpec((1, 6, d), lambda i, j: (i, 0, 0)),
                  pl.BlockSpec((1, d), lambda i, j: (0, 0)),
                  pl.BlockSpec((d, IN_COLS), lambda i, j: (0, 0), pipeline_mode=pl.Buffered(1))],
        out_specs=pl.BlockSpec((1, tm, IN_COLS), lambda i, j: (i, j, 0)),
        compiler_params=_cparams(("arbitrary", "arbitrary")),
        name="inproj",
    )(x, mod, nw, w_in_b)


def _head_inv_rms(xf, bd):
    hi, lo = _split_bf16(xf * xf)
    ssq = jnp.dot(hi, bd, preferred_element_type=F32) + jnp.dot(lo, bd, preferred_element_type=F32)
    return lax.rsqrt(ssq * (1.0 / HEAD_DIM) + RMS_EPS)


def _mixer_body(q_ref, k_ref, v_ref, bg_ref, c_ref, u_ref, ga_ref, gc_ref,
                knw_ref, qsc_ref, bd_ref, bias_ref, sink_ref, cw_ref,
                kpast_ref, vpast_ref, cpast_ref, upast_ref,
                merged_ref, knew_ref, vnew_ref, cnew_ref,
                kq_buf, vt_buf, attn_buf, *, tq, nq, stateful):
    hb = WINDOW
    pw = UNIT_Q
    t = pl.program_id(1)
    bd = bd_ref[...]

    k = k_ref[0].astype(F32)
    kn = k * _head_inv_rms(k, bd) * knw_ref[...]
    kq = (kn * qsc_ref[...]).astype(BF16)
    vb = v_ref[0]
    vt = vb.astype(F32).T.astype(BF16)

    if stateful:
        kp = kpast_ref[0]
        vp = vpast_ref[0]
        for kv in range(N_KV):
            kq_buf[kv, hb + tq:] = jnp.zeros((KEY_WIN - hb - tq, HEAD_DIM), BF16)
        vt_buf[:, hb + tq:] = jnp.zeros((KV_COLS, KEY_WIN - hb - tq), BF16)
        u_hist = jnp.concatenate([jnp.zeros((6, D_MODEL), F32), cpast_ref[0]], axis=0)
        knew_ref[0] = jnp.concatenate([kp[tq:], kn], axis=0)
        vnew_ref[0] = jnp.concatenate([vp[tq:], vb.astype(F32)], axis=0)
    else:
        kraw = kpast_ref[0].astype(F32)
        kp = kraw * _head_inv_rms(kraw, bd) * knw_ref[...]
        vp = vpast_ref[0].astype(F32)
        u_hist = jnp.where(t == 0, 0.0, cpast_ref[0].astype(F32) * upast_ref[0].astype(F32))
        knew_ref[0] = kn[tq - hb:]
        vnew_ref[0] = vb[tq - hb:].astype(F32)
    _fill_keys(kq_buf, vt_buf, kp, vp, kq, vt, qsc_ref, tq)
    q = q_ref[0]
    _attention_units(lambda kv: q[:, kv * KV_COLS:(kv + 1) * KV_COLS].astype(F32), kq_buf, vt_buf,
                     bias_ref, sink_ref, bd, attn_buf, tq, nq, None if stateful else t)
    cu = c_ref[0].astype(F32) * u_ref[0].astype(F32)
    conv, u_all = _short_conv(u_hist, cu, cw_ref[...], tq)
    cnew_ref[0] = u_all[tq + 6:tq + 8]
    merged = (jax.nn.sigmoid(ga_ref[0].astype(F32)) * attn_buf[...]
              + jax.nn.sigmoid(gc_ref[0].astype(F32)) * (bg_ref[0].astype(F32) * conv))
    merged_ref[0] = merged.astype(BF16)


def _fill_keys(kq_buf, vt_buf, kp, vp, kq, vt, qsc_ref, tq):
    hb = WINDOW
    kqp = (kp * qsc_ref[...]).astype(BF16)
    for kv in range(N_KV):
        kq_buf[kv, 0:hb] = kqp[:, kv * HEAD_DIM:(kv + 1) * HEAD_DIM]
        kq_buf[kv, hb:hb + tq] = kq[:, kv * HEAD_DIM:(kv + 1) * HEAD_DIM]
    vt_buf[:, 0:hb] = vp.T.astype(BF16)
    vt_buf[:, hb:hb + tq] = vt


def _attention_units(q_group, kq_buf, vt_buf, bias_ref, sink_ref, bd, attn_buf, tq, nq, t_first, after_unit=None):
    pw = UNIT_Q
    for kv in range(N_KV):
        qf = q_group(kv)
        qn = (qf * _head_inv_rms(qf, bd)).astype(BF16)
        for u in range(tq // nq):
            r0 = u * nq
            parts = [qn[r0:r0 + nq, g * HEAD_DIM:(g + 1) * HEAD_DIM] for g in range(GROUP)]
            if nq < pw:
                zpad = jnp.zeros((pw - nq, HEAD_DIM), BF16)
                parts = [x for p_ in parts for x in (p_, zpad)]
            qs = jnp.concatenate(parts, axis=0)
            kw = kq_buf[kv, r0:r0 + KEY_WIN]
            st = lax.dot_general(kw, qs, (((1,), (1,)), ((), ())), preferred_element_type=F32)
            if t_first is not None and u == 0:
                bias = jnp.where(t_first == 0, bias_ref[kv + N_KV], bias_ref[kv])
            else:
                bias = bias_ref[kv]
            st = st + bias
            sink = sink_ref[kv]
            m = jnp.maximum(jnp.max(st, axis=0, keepdims=True), sink)
            p = jnp.exp(st - m)
            den = jnp.sum(p, axis=0, keepdims=True) + jnp.exp(sink - m)
            ot = jnp.dot(vt_buf[kv * HEAD_DIM:(kv + 1) * HEAD_DIM, r0:r0 + KEY_WIN], p.astype(BF16),
                         preferred_element_type=F32) / den
            for gp in range(GROUP // 2):
                blk = jnp.concatenate([ot[:, (2 * gp) * pw:(2 * gp + 1) * pw],
                                       ot[:, (2 * gp + 1) * pw:(2 * gp + 2) * pw]], axis=0)
                c0 = (kv * GROUP + 2 * gp) * HEAD_DIM
                attn_buf[r0:r0 + nq, c0:c0 + 2 * HEAD_DIM] = blk.T[:nq]
            if after_unit is not None:
                after_unit(kv * (tq // nq) + u)


def _short_conv(u_hist, cu, cw, tq):
    u_all = jnp.concatenate([u_hist, cu], axis=0)
    conv = cw[0:1] * u_all[6:6 + tq] + cw[1:2] * u_all[7:7 + tq] + cw[2:3] * u_all[8:8 + tq]
    return conv, u_all


def _front_body(x_ref, xp_ref, mod_ref, nw_ref, w_ref, knw_ref, qsc_ref, bd_ref, bias_ref, sink_ref, cw_ref,
                merged_ref, knew_ref, vnew_ref, cnew_ref,
                kq_buf, vt_buf, u_buf, attn_buf, pbuf, *, tq, nq):
    hb = WINDOW
    d = D_MODEL
    t = pl.program_id(1)
    bd = bd_ref[...]
    mod = mod_ref[0]

    @pl.when(t == 0)
    def _():
        u_buf[0] = jnp.zeros((8, d), F32)

    def modnorm(x):
        h = x * lax.rsqrt(jnp.mean(x * x, axis=-1, keepdims=True) + RMS_EPS) * nw_ref[...]
        return (h * (1.0 + mod[1:2]) + mod[0:1]).astype(BF16)

    def proj(hrows, c0, width):
        return jnp.dot(hrows, w_ref[:, c0:c0 + width], preferred_element_type=F32)

    hbf = modnorm(x_ref[0])
    kv_t = proj(hbf, 6 * d, 2 * KV_COLS)
    kv_p = proj(modnorm(xp_ref[0]), 6 * d, 2 * KV_COLS)
    k, v = kv_t[:, :KV_COLS], kv_t[:, KV_COLS:]
    kn = k * _head_inv_rms(k, bd) * knw_ref[...]
    kq = (kn * qsc_ref[...]).astype(BF16)
    kraw = kv_p[:, :KV_COLS]
    kp = kraw * _head_inv_rms(kraw, bd) * knw_ref[...]
    knew_ref[0] = kn[tq - hb:]
    vnew_ref[0] = v[tq - hb:]
    _fill_keys(kq_buf, vt_buf, kp, kv_p[:, KV_COLS:], kq, v.T.astype(BF16), qsc_ref, tq)

    n_units = N_KV * (tq // nq)
    q_chunks = [[g * KV_COLS + c for c in range(0, KV_COLS, PROJ_CHUNK)] for g in range(N_KV)]
    rest_chunks = list(range(d, 6 * d, PROJ_CHUNK))

    def issue(c0):
        pbuf[:, c0:c0 + PROJ_CHUNK] = proj(hbf, c0, PROJ_CHUNK)

    for c0 in q_chunks[0]:
        issue(c0)
    n_rest = len(rest_chunks)

    def after_unit(i):
        kv, u = divmod(i, tq // nq)
        if u == 0 and kv + 1 < N_KV:
            for c0 in q_chunks[kv + 1]:
                issue(c0)
        for c0 in rest_chunks[i * n_rest // n_units:(i + 1) * n_rest // n_units]:
            issue(c0)

    _attention_units(lambda kv: pbuf[:, kv * KV_COLS:(kv + 1) * KV_COLS], kq_buf, vt_buf,
                     bias_ref, sink_ref, bd, attn_buf, tq, nq, t, after_unit)

    cu = pbuf[:, 2 * d:3 * d] * pbuf[:, 3 * d:4 * d]
    conv, u_all = _short_conv(u_buf[t % 2], cu, cw_ref[...], tq)
    cnew_ref[0] = u_all[tq + 6:tq + 8]
    u_buf[(t + 1) % 2] = u_all[tq:tq + 8]
    merged = (jax.nn.sigmoid(pbuf[:, 4 * d:5 * d]) * attn_buf[...]
              + jax.nn.sigmoid(pbuf[:, 5 * d:6 * d]) * (pbuf[:, d:2 * d] * conv))
    merged_ref[0] = merged.astype(BF16)


def _front(x, mod, nw, w_in_b, knw, qsc, bias_tab, sink_tab, conv_w, tq, nq):
    b, t, d = x.shape
    r = np.arange(KV_COLS) // HEAD_DIM
    bd = jnp.asarray((r[:, None] == r[None, :]).astype(np.float32), BF16)
    const2 = lambda shp: pl.BlockSpec(shp, lambda i, s: (0, 0))
    const3 = lambda shp: pl.BlockSpec(shp, lambda i, s: (0, 0, 0))
    per_b = lambda shp: pl.BlockSpec(shp, lambda i, s: (i, 0, 0))
    kw_ = tq // WINDOW
    in_specs = [pl.BlockSpec((1, tq, d), lambda i, s: (i, s, 0)),
                pl.BlockSpec((1, WINDOW, d), lambda i, s: (i, jnp.maximum(s * kw_ - 1, 0), 0)),
                per_b((1, 6, d)), const2((1, d)),
                pl.BlockSpec((d, IN_COLS), lambda i, s: (0, 0), pipeline_mode=pl.Buffered(1)),
                const2((1, KV_COLS)), const2((1, KV_COLS)), const2((KV_COLS, KV_COLS)),
                const3(bias_tab.shape), const3(sink_tab.shape), const2((3, d))]
    out_shape = (jax.ShapeDtypeStruct((b, t, d), BF16),
                 jax.ShapeDtypeStruct((b, WINDOW, KV_COLS), F32),
                 jax.ShapeDtypeStruct((b, WINDOW, KV_COLS), F32),
                 jax.ShapeDtypeStruct((b, 2, d), F32))
    out_specs = (pl.BlockSpec((1, tq, d), lambda i, s: (i, s, 0)),
                 per_b((1, WINDOW, KV_COLS)), per_b((1, WINDOW, KV_COLS)), per_b((1, 2, d)))
    return pl.pallas_call(
        functools.partial(_front_body, tq=tq, nq=nq),
        out_shape=out_shape,
        grid=(b, t // tq),
        in_specs=in_specs,
        out_specs=out_specs,
        scratch_shapes=[pltpu.VMEM((N_KV, WINDOW + tq, HEAD_DIM), BF16),
                        pltpu.VMEM((KV_COLS, WINDOW + tq), BF16),
                        pltpu.VMEM((2, 8, d), F32),
                        pltpu.VMEM((tq, d), F32),
                        pltpu.VMEM((tq, 6 * d), F32)],
        compiler_params=_cparams(("arbitrary", "arbitrary")),
        name="front",
    )(x, x, mod, nw, w_in_b, knw, qsc, bd, bias_tab, sink_tab, conv_w)


def _mixer(proj, knw, qsc, bias_tab, sink_tab, conv_w, state, tq, nq):
    b, t, _ = proj.shape
    d = D_MODEL
    stateful = state is not None
    key_rows = max(WINDOW + tq, KEY_WIN)
    r = np.arange(KV_COLS) // HEAD_DIM
    bd = jnp.asarray((r[:, None] == r[None, :]).astype(np.float32), BF16)
    wide = lambda j: pl.BlockSpec((1, tq, d), lambda i, s, j=j: (i, s, j))
    kvspec = lambda j: pl.BlockSpec((1, tq, KV_COLS), lambda i, s, j=j: (i, s, j))
    const2 = lambda shp: pl.BlockSpec(shp, lambda i, s: (0, 0))
    const3 = lambda shp: pl.BlockSpec(shp, lambda i, s: (0, 0, 0))
    per_b = lambda shp: pl.BlockSpec(shp, lambda i, s: (i, 0, 0))
    kvblk = 6 * d // KV_COLS
    if stateful:
        hist_specs = [per_b((1, WINDOW, KV_COLS)), per_b((1, WINDOW, KV_COLS)), per_b((1, 2, d)), per_b((1, 2, d))]
        hist_args = [state[0], state[1], state[2], state[2]]
    else:
        kw_ = tq // WINDOW
        prev_kv = lambda j: pl.BlockSpec((1, WINDOW, KV_COLS),
                                         lambda i, s, j=j: (i, jnp.maximum(s * kw_ - 1, 0), j))
        prev8 = lambda j: pl.BlockSpec((1, 8, d), lambda i, s, j=j: (i, jnp.maximum(s * (tq // 8) - 1, 0), j))
        hist_specs = [prev_kv(kvblk), prev_kv(kvblk + 1), prev8(2), prev8(3)]
        hist_args = [proj, proj, proj, proj]
    in_specs = [wide(0), kvspec(kvblk), kvspec(kvblk + 1), wide(1), wide(2), wide(3), wide(4), wide(5),
                const2((1, KV_COLS)), const2((1, KV_COLS)), const2((KV_COLS, KV_COLS)),
                const3(bias_tab.shape), const3(sink_tab.shape), const2((3, d))] + hist_specs
    out_shape = (jax.ShapeDtypeStruct((b, t, d), BF16),
                 jax.ShapeDtypeStruct((b, WINDOW, KV_COLS), F32),
                 jax.ShapeDtypeStruct((b, WINDOW, KV_COLS), F32),
                 jax.ShapeDtypeStruct((b, 2, d), F32))
    out_specs = (pl.BlockSpec((1, tq, d), lambda i, s: (i, s, 0)),
                 per_b((1, WINDOW, KV_COLS)), per_b((1, WINDOW, KV_COLS)), per_b((1, 2, d)))
    return pl.pallas_call(
        functools.partial(_mixer_body, tq=tq, nq=nq, stateful=stateful),
        out_shape=out_shape,
        grid=(b, t // tq),
        in_specs=in_specs,
        out_specs=out_specs,
        scratch_shapes=[pltpu.VMEM((N_KV, key_rows, HEAD_DIM), BF16),
                        pltpu.VMEM((KV_COLS, key_rows), BF16),
                        pltpu.VMEM((tq, d), F32)],
        compiler_params=_cparams(("arbitrary", "arbitrary")),
        name="mixer_state" if stateful else "mixer",
    )(proj, proj, proj, proj, proj, proj, proj, proj, knw, qsc, bd, bias_tab, sink_tab, conv_w, *hist_args)


def _route(logits):
    lane = lax.broadcasted_iota(I32, logits.shape, 1)
    neg = -jnp.inf
    big = jnp.int32(1 << 20)
    gl = jnp.where(lane < N_GROUPS, logits, neg)
    gmax = jnp.max(gl, axis=-1, keepdims=True)
    g_idx = jnp.min(jnp.where(gl == gmax, lane, big), axis=-1, keepdims=True)
    g_w = 1.0 / jnp.sum(jnp.exp(gl - gmax), axis=-1, keepdims=True)
    lo = N_GROUPS + g_idx * EPG
    el = jnp.where((lane >= lo) & (lane < lo + EPG), logits, neg)
    m1 = jnp.max(el, axis=-1, keepdims=True)
    i1 = jnp.min(jnp.where(el == m1, lane, big), axis=-1, keepdims=True)
    el2 = jnp.where(lane == i1, neg, el)
    m2 = jnp.max(el2, axis=-1, keepdims=True)
    i2 = jnp.min(jnp.where(el2 == m2, lane, big), axis=-1, keepdims=True)
    r = jnp.exp(m2 - m1)
    w1 = 1.0 / (1.0 + r)
    w2 = r / (1.0 + r)
    return i1 - N_GROUPS, i2 - N_GROUPS, g_w * w1, g_w * w2


def _outproj_kernel(m_ref, x_ref, mod_ref, wo_ref, nw_ref, wr_ref, br_ref,
                    x1_ref, h2_ref, eid_ref, ew_ref):
    mod = mod_ref[0]
    mix = jnp.dot(m_ref[0], wo_ref[...], preferred_element_type=F32)
    x1 = x_ref[0] + mod[2:3] * mix
    x1_ref[0] = x1
    h = x1 * lax.rsqrt(jnp.mean(x1 * x1, axis=-1, keepdims=True) + RMS_EPS) * nw_ref[...]
    h = h * (1.0 + mod[4:5]) + mod[3:4]
    h2_ref[...] = _pack_pairs(h)
    tm = h.shape[0]
    h_hi, h_lo = _split_bf16(h)
    prod = jnp.dot(jnp.concatenate([h_hi, h_lo], axis=0), wr_ref[...], preferred_element_type=F32)
    logits = prod[:tm, :LANES] + (prod[:tm, LANES:] + prod[tm:, :LANES]) + br_ref[...]
    e1, e2, w1, w2 = _route(logits)
    lane8 = lax.broadcasted_iota(I32, (h.shape[0], 8), 1)
    eid_ref[...] = jnp.where(lane8 == 0, e1, jnp.where(lane8 == 1, e2, 0))
    ew_ref[...] = jnp.where(lane8 == 0, w1, jnp.where(lane8 == 1, w2, 0.0))


def _outproj(merged, x, mod, w_out_b, nw, w_r, b_r, tm):
    b, t, d = x.shape
    nt = t // tm
    flat = lambda i, j: (i * nt + j, 0)
    return pl.pallas_call(
        _outproj_kernel,
        out_shape=(jax.ShapeDtypeStruct((b, t, d), F32),
                   jax.ShapeDtypeStruct((b * t, HALF), I32),
                   jax.ShapeDtypeStruct((b * t, 8), I32),
                   jax.ShapeDtypeStruct((b * t, 8), F32)),
        grid=(b, nt),
        in_specs=[pl.BlockSpec((1, tm, d), lambda i, j: (i, j, 0)),
                  pl.BlockSpec((1, tm, d), lambda i, j: (i, j, 0)),
                  pl.BlockSpec((1, 6, d), lambda i, j: (i, 0, 0)),
                  pl.BlockSpec((d, d), lambda i, j: (0, 0)),
                  pl.BlockSpec((1, d), lambda i, j: (0, 0)),
                  pl.BlockSpec((d, 2 * LANES), lambda i, j: (0, 0)),
                  pl.BlockSpec((1, LANES), lambda i, j: (0, 0))],
        out_specs=(pl.BlockSpec((1, tm, d), lambda i, j: (i, j, 0)),
                   pl.BlockSpec((tm, HALF), flat),
                   pl.BlockSpec((tm, 8), flat),
                   pl.BlockSpec((tm, 8), flat)),
        compiler_params=_cparams(("arbitrary", "arbitrary")),
        name="outproj",
    )(merged, x, mod, w_out_b, nw, w_r, b_r)


def _rank_kernel(eid_ref, tri_ref, upper_ref, dest_ref, tot_ref, cnt, starts, *, block):
    ph = pl.program_id(0)
    i = pl.program_id(1)
    tm = eid_ref.shape[0]
    lane = lax.broadcasted_iota(I32, (tm, LANES), 1)
    e0 = eid_ref[:, 0:1]
    e1 = eid_ref[:, 1:2]
    hot0 = lane == e0
    hot1 = lane == e1
    onehot = jnp.where(hot0 | hot1, 1.0, 0.0)
    colsum = jnp.sum(onehot, axis=0, keepdims=True)

    @pl.when((ph == 0) & (i == 0))
    def _():
        cnt[...] = jnp.zeros_like(cnt)

    @pl.when(ph == 0)
    def _():
        cnt[0:1] = cnt[0:1] + colsum

    @pl.when((ph == 1) & (i == 0))
    def _():
        tot = cnt[0:1]
        tot_ref[...] = tot.astype(I32)
        nblk = jnp.floor((tot + (block - 1)) * (1.0 / block))
        hi = jnp.floor(nblk * (1.0 / 16.0))
        lo = nblk - hi * 16.0
        up = upper_ref[...]
        excl = (jnp.dot(jnp.broadcast_to(hi, (8, LANES)).astype(BF16), up, preferred_element_type=F32) * 16.0
                + jnp.dot(jnp.broadcast_to(lo, (8, LANES)).astype(BF16), up, preferred_element_type=F32))
        starts[...] = excl * float(block)
        cnt[...] = jnp.zeros_like(cnt)

    @pl.when(ph == 1)
    def _():
        prefix = jnp.dot(tri_ref[...], onehot.astype(BF16), preferred_element_type=F32)
        pos = prefix + (starts[0:1] + cnt[0:1])
        d0 = jnp.sum(jnp.where(hot0, pos, 0.0), axis=-1, keepdims=True).astype(I32)
        d1 = jnp.sum(jnp.where(hot1, pos, 0.0), axis=-1, keepdims=True).astype(I32)
        lane8 = lax.broadcasted_iota(I32, (tm, 8), 1)
        dest_ref[...] = jnp.where(lane8 == 0, d0, jnp.where(lane8 == 1, d1, 0))
        cnt[0:1] = cnt[0:1] + colsum


def _rank(eid, block):
    n = eid.shape[0]
    tm = math.gcd(n, RANK_TILE)
    r = np.arange(tm)
    tri = jnp.asarray((r[:, None] > r[None, :]).astype(np.float32), BF16)
    l = np.arange(LANES)
    upper = jnp.asarray(((l[:, None] < l[None, :]) & (l[None, :] < N_EXPERTS)).astype(np.float32), BF16)
    return pl.pallas_call(
        functools.partial(_rank_kernel, block=block),
        out_shape=(jax.ShapeDtypeStruct((n, 8), I32), jax.ShapeDtypeStruct((1, LANES), I32)),
        grid=(2, n // tm),
        in_specs=[pl.BlockSpec((tm, 8), lambda p, i: (i, 0)),
                  pl.BlockSpec((tm, tm), lambda p, i: (0, 0)),
                  pl.BlockSpec((LANES, LANES), lambda p, i: (0, 0))],
        out_specs=(pl.BlockSpec((tm, 8), lambda p, i: (i * p, 0)),
                   pl.BlockSpec((1, LANES), lambda p, i: (0, 0))),
        scratch_shapes=[pltpu.VMEM((8, LANES), F32), pltpu.VMEM((8, LANES), F32)],
        compiler_params=_cparams(("arbitrary", "arbitrary")),
        name="rank",
    )(eid, tri, upper)


def _expert_kernel(start_ref, nblk_ref, xs_hbm, wg_ref, wu_ref, wd_ref, ys_hbm,
                   xbuf, ybuf, wg_s, wu_s, wd_s, sem_in, sem_out, *, block):
    nbuf = EXPERT_BUFS
    e = pl.program_id(0)
    n = nblk_ref[e]
    base = start_ref[e]
    total = start_ref[N_EXPERTS - 1] + nblk_ref[N_EXPERTS - 1]

    def in_copy(g):
        rows = pl.ds(pl.multiple_of(g * block, block), block)
        return pltpu.make_async_copy(xs_hbm.at[rows], xbuf.at[g % nbuf], sem_in.at[g % nbuf])

    def out_copy(g):
        rows = pl.ds(pl.multiple_of(g * block, block), block)
        return pltpu.make_async_copy(ybuf.at[g % nbuf], ys_hbm.at[rows], sem_out.at[g % nbuf])

    @pl.when(e == 0)
    def _():
        for g0 in range(nbuf - 1):
            @pl.when(g0 < total)
            def _(g0=g0):
                in_copy(g0).start()

    @pl.when(n > 0)
    def _():
        wg_s[...] = wg_ref[0].astype(BF16)
        wu_s[...] = wu_ref[0].astype(BF16)
        wd_s[...] = wd_ref[0].astype(BF16)

        def body(i, carry):
            g = base + i
            slot = g % nbuf
            in_copy(g).wait()

            @pl.when(g + nbuf - 1 < total)
            def _():
                in_copy(g + nbuf - 1).start()

            @pl.when(g >= nbuf)
            def _():
                out_copy(g - nbuf).wait()

            a, c = _unpack_pairs(xbuf[slot])
            x = jnp.concatenate([a.astype(BF16), c.astype(BF16)], axis=1)
            gate = jnp.dot(x, wg_s[...], preferred_element_type=F32)
            up = jnp.dot(x, wu_s[...], preferred_element_type=F32)
            hmid = (gate * jax.nn.sigmoid(gate) * up).astype(BF16)
            ybuf[slot] = _pack_pairs(jnp.dot(hmid, wd_s[...], preferred_element_type=F32))
            out_copy(g).start()
            return carry

        lax.fori_loop(0, n, body, 0)

    @pl.when(e == N_EXPERTS - 1)
    def _():
        for back in range(nbuf, 0, -1):
            @pl.when(total >= back)
            def _(back=back):
                out_copy(total - back).wait()


def _experts(xs, start_blk, nblk, w_gate, w_up, w_down, block):
    n_rows = xs.shape[0]
    wblk = lambda e, st, nb: (e, 0, 0)
    grid_spec = pltpu.PrefetchScalarGridSpec(
        num_scalar_prefetch=2,
        grid=(N_EXPERTS,),
        in_specs=[pl.BlockSpec(memory_space=pl.ANY),
                  pl.BlockSpec((1, D_MODEL, D_EXPERT), wblk),
                  pl.BlockSpec((1, D_MODEL, D_EXPERT), wblk),
                  pl.BlockSpec((1, D_EXPERT, D_MODEL), wblk)],
        out_specs=pl.BlockSpec(memory_space=pl.ANY),
        scratch_shapes=[pltpu.VMEM((EXPERT_BUFS, block, HALF), I32),
                        pltpu.VMEM((EXPERT_BUFS, block, HALF), I32),
                        pltpu.VMEM((D_MODEL, D_EXPERT), BF16),
                        pltpu.VMEM((D_MODEL, D_EXPERT), BF16),
                        pltpu.VMEM((D_EXPERT, D_MODEL), BF16),
                        pltpu.SemaphoreType.DMA((EXPERT_BUFS,)),
                        pltpu.SemaphoreType.DMA((EXPERT_BUFS,))])
    return pl.pallas_call(
        functools.partial(_expert_kernel, block=block),
        out_shape=jax.ShapeDtypeStruct((n_rows, HALF), I32),
        grid_spec=grid_spec,
        compiler_params=_cparams(("arbitrary",)),
        name="experts",
    )(start_blk, nblk, xs, w_gate, w_up, w_down)


def _final_kernel(x1_ref, y0_ref, y1_ref, ew_ref, mod_ref, o_ref):
    a0, b0 = _unpack_pairs(y0_ref[...])
    a1, b1 = _unpack_pairs(y1_ref[...])
    w0 = ew_ref[:, 0:1]
    w1 = ew_ref[:, 1:2]
    moe = jnp.concatenate([w0 * a0 + w1 * a1, w0 * b0 + w1 * b1], axis=1)
    o_ref[0] = x1_ref[0] + mod_ref[0][5:6] * moe


def _final(x1, y0, y1, ew, mod, tm):
    b, t, d = x1.shape
    nt = t // tm
    flat = lambda i, j: (i * nt + j, 0)
    return pl.pallas_call(
        _final_kernel,
        out_shape=jax.ShapeDtypeStruct((b, t, d), F32),
        grid=(b, nt),
        in_specs=[pl.BlockSpec((1, tm, d), lambda i, j: (i, j, 0)),
                  pl.BlockSpec((tm, HALF), flat),
                  pl.BlockSpec((tm, HALF), flat),
                  pl.BlockSpec((tm, 8), flat),
                  pl.BlockSpec((1, 6, d), lambda i, j: (i, 0, 0))],
        out_specs=pl.BlockSpec((1, tm, d), lambda i, j: (i, j, 0)),
        compiler_params=_cparams(("arbitrary", "arbitrary")),
        name="final",
    )(x1, y0, y1, ew, mod)


def _sc_window(rows_per_worker):
    for w in range(SC_MAX_WINDOW, 7, -8):
        if rows_per_worker % w == 0:
            return w
    raise ValueError(f"no SparseCore window divides {rows_per_worker} rows per worker")


def _sc_split(idx):
    n = idx.shape[0]
    per = n // SC_WORKERS
    assert per * SC_WORKERS == n
    win = _sc_window(per)
    return idx.reshape(SC_WORKERS, per // win, win), per // win, win


def _sc_worker_id():
    return lax.axis_index("s") * SC_CORES + lax.axis_index("c")


def _dispatch_rows(h2_groups, dest_groups, n_rows):
    splits = [(_sc_split(d[:, 0]), _sc_split(d[:, 1])) for d in dest_groups]
    ng = len(h2_groups)
    scratch = []
    for (_, _, win), _ in splits:
        scratch += [pltpu.VMEM((win,), I32), pltpu.VMEM((win,), I32), pltpu.VMEM((win, HALF), I32)]

    @functools.partial(
        pl.kernel,
        mesh=plsc.VectorSubcoreMesh(core_axis_name="c", subcore_axis_name="s"),
        out_type=jax.ShapeDtypeStruct((n_rows, HALF), I32),
        scratch_types=scratch,
        name="sc_dispatch",
    )
    def k(*refs):
        x_refs, idx_refs, o_hbm, bufs = refs[:ng], refs[ng:3 * ng], refs[3 * ng], refs[3 * ng + 1:]
        wid = _sc_worker_id()
        for g in range(ng):
            (_, nwin, win), _ = splits[g]
            x_hbm, d0_hbm, d1_hbm = x_refs[g], idx_refs[2 * g], idx_refs[2 * g + 1]
            i0_v, i1_v, rows_v = bufs[3 * g:3 * g + 3]

            @pl.loop(0, nwin)
            def _(j, nwin=nwin, win=win, x_hbm=x_hbm, d0_hbm=d0_hbm, d1_hbm=d1_hbm,
                  i0_v=i0_v, i1_v=i1_v, rows_v=rows_v):
                base = pl.multiple_of((wid * nwin + j) * win, 8)
                pltpu.sync_copy(d0_hbm.at[wid, j], i0_v)
                pltpu.sync_copy(d1_hbm.at[wid, j], i1_v)
                pltpu.sync_copy(x_hbm.at[pl.ds(base, win)], rows_v)
                pltpu.sync_copy(rows_v, o_hbm.at[i0_v])
                pltpu.sync_copy(rows_v, o_hbm.at[i1_v])

    idx_args = []
    for (s0, s1) in splits:
        idx_args += [s0[0], s1[0]]
    return k(*h2_groups, *idx_args)


def _collect_rows(ys, dest_groups):
    splits = [(_sc_split(d[:, 0]), _sc_split(d[:, 1])) for d in dest_groups]
    ng = len(dest_groups)
    outs, scratch = [], []
    for d, ((_, _, win), _) in zip(dest_groups, splits):
        o = jax.ShapeDtypeStruct((d.shape[0], HALF), I32)
        outs += [o, o]
        scratch += [pltpu.VMEM((win,), I32), pltpu.VMEM((win, HALF), I32)]

    @functools.partial(
        pl.kernel,
        mesh=plsc.VectorSubcoreMesh(core_axis_name="c", subcore_axis_name="s"),
        out_type=tuple(outs),
        scratch_types=scratch,
        name="sc_collect",
    )
    def k(*refs):
        ys_hbm, idx_refs, out_refs, bufs = refs[0], refs[1:1 + 2 * ng], refs[1 + 2 * ng:1 + 4 * ng], refs[1 + 4 * ng:]
        wid = _sc_worker_id()
        for g in range(ng):
            (_, nwin, win), _ = splits[g]
            i_v, rows_v = bufs[2 * g:2 * g + 2]
            for kk in range(2):
                d_hbm, y_hbm = idx_refs[2 * g + kk], out_refs[2 * g + kk]

                @pl.loop(0, nwin)
                def _(j, nwin=nwin, win=win, d_hbm=d_hbm, y_hbm=y_hbm, i_v=i_v, rows_v=rows_v):
                    base = pl.multiple_of((wid * nwin + j) * win, 8)
                    pltpu.sync_copy(d_hbm.at[wid, j], i_v)
                    pltpu.sync_copy(ys_hbm.at[i_v], rows_v)
                    pltpu.sync_copy(rows_v, y_hbm.at[pl.ds(base, win)])

    idx_args = []
    for (s0, s1) in splits:
        idx_args += [s0[0], s1[0]]
    res = k(ys, *idx_args)
    return [(res[2 * g], res[2 * g + 1]) for g in range(ng)]


def _t5_bucket(rel):
    half = N_BUCKETS // 2
    max_exact = half // 2
    n = jnp.abs(rel)
    far = max_exact + (jnp.log(jnp.maximum(n, 1).astype(F32) / max_exact)
                       / math.log(MAX_DISTANCE / max_exact) * (half - max_exact)).astype(I32)
    far = jnp.minimum(far, half - 1)
    return jnp.where(rel > 0, half, 0) + jnp.where(n < max_exact, n, far)


def _bias_table(rel_bias, cq, nq, no_history):
    nk = WINDOW + cq
    j = jnp.arange(KEY_WIN)[:, None]
    c = jnp.arange(UNIT_Q)[None, :]
    jj = j - (c // cq) * cq
    valid = (jj >= 0) & (jj < nk) & (c < nq)
    if no_history:
        valid = valid & (j >= WINDOW)
    rel = jj - WINDOW - (c % cq)
    onehot = (_t5_bucket(rel)[:, :, None] == jnp.arange(N_BUCKETS)).astype(F32)
    bias = jnp.einsum("jcb,bh->jch", onehot, rel_bias.astype(F32), precision=lax.Precision.HIGHEST)
    bias = jnp.where(valid[:, :, None], bias, -jnp.inf)
    bias = jnp.transpose(bias.reshape(KEY_WIN, UNIT_Q, N_KV, GROUP), (2, 0, 3, 1))
    return bias.reshape(N_KV, KEY_WIN, GROUP * UNIT_Q)


def _sink_table(sinks):
    s = sinks.astype(F32).reshape(N_KV, 1, GROUP, 1)
    return jnp.broadcast_to(s, (N_KV, 1, GROUP, UNIT_Q)).reshape(N_KV, 1, GROUP * UNIT_Q)


def kernel(x_prompt, x_sample, state_attn_k, state_attn_v, state_conv, c_prompt, c_sample,
           rel_bias, w_ada, b_ada, norm1_w, w_in, q_norm_w, k_norm_w, attn_sinks, conv_w,
           w_out, norm2_w, w_router_group, b_router_group, w_router_expert, b_router_expert,
           w_gate, w_up, w_down):
    depth = w_ada.shape[0]
    assert depth == 1
    bp, tp, d = x_prompt.shape
    bs, ts, _ = x_sample.shape
    n_p, n_s = bp * tp, bs * ts
    n_tok = n_p + n_s
    l = 0

    wi = w_in[l]
    qw, kw, vw, rest = wi[:, :d], wi[:, d:d + KV_COLS], wi[:, d + KV_COLS:d + 2 * KV_COLS], wi[:, d + 2 * KV_COLS:]
    w_in_b = jnp.concatenate([qw, rest, kw, vw], axis=1).astype(BF16)
    w_out_b = w_out[l].astype(BF16)
    w_r = jnp.concatenate([w_router_group[l],
                           jnp.transpose(w_router_expert[l], (1, 0, 2)).reshape(d, N_EXPERTS),
                           jnp.zeros((d, LANES - N_GROUPS - N_EXPERTS), F32)], axis=1)
    w_r_hi = lax.reduce_precision(w_r, exponent_bits=8, mantissa_bits=7)
    w_r = jnp.concatenate([w_r_hi.astype(BF16), (w_r - w_r_hi).astype(BF16)], axis=1)
    b_r = jnp.concatenate([b_router_group[l], b_router_expert[l].reshape(-1),
                           jnp.zeros((LANES - N_GROUPS - N_EXPERTS,), F32)]).reshape(1, LANES)
    knw = jnp.tile(k_norm_w[l], N_KV).reshape(1, KV_COLS)
    qsc = jnp.tile(q_norm_w[l] * (HEAD_DIM ** -0.5), N_KV).reshape(1, KV_COLS)
    n1w = norm1_w[l].reshape(1, d)
    n2w = norm2_w[l].reshape(1, d)

    mod = _ada(jnp.concatenate([c_prompt, c_sample], axis=0), w_ada[l], b_ada[l]).reshape(bp + bs, 6, d)
    mod_p, mod_s = mod[:bp], mod[bp:]

    proj_s = _inproj(x_sample, mod_s, n1w, w_in_b, ts)
    sink_tab = _sink_table(attn_sinks[l])
    bias_p = jnp.concatenate([_bias_table(rel_bias, CHUNK, UNIT_Q, False),
                              _bias_table(rel_bias, CHUNK, UNIT_Q, True)], axis=0)
    merged_p, k_p, v_p, c_p = _front(x_prompt, mod_p, n1w, w_in_b, knw, qsc, bias_p, sink_tab, conv_w[l],
                                     MIX_TILE, UNIT_Q)
    state = (state_attn_k[l].reshape(bs, WINDOW, KV_COLS), state_attn_v[l].reshape(bs, WINDOW, KV_COLS),
             state_conv[l])
    merged_s, k_s, v_s, c_s = _mixer(proj_s, knw, qsc, _bias_table(rel_bias, ts, ts, False), sink_tab, conv_w[l],
                                     state, ts, ts)

    x1_p, h2_p, eid_p, ew_p = _outproj(merged_p, x_prompt, mod_p, w_out_b, n2w, w_r, b_r, ROW_TILE)
    x1_s, h2_s, eid_s, ew_s = _outproj(merged_s, x_sample, mod_s, w_out_b, n2w, w_r, b_r, ts)

    dest, totals = _rank(jnp.concatenate([eid_p, eid_s], axis=0), EXPERT_BLOCK)
    dests = [dest[:n_p], dest[n_p:]]
    nblk = (totals[0, :N_EXPERTS] + EXPERT_BLOCK - 1) // EXPERT_BLOCK
    start_blk = (jnp.cumsum(nblk) - nblk).astype(I32)
    nb_max = -(-2 * n_tok // EXPERT_BLOCK) + N_EXPERTS

    xs = _dispatch_rows([h2_p, h2_s], dests, nb_max * EXPERT_BLOCK)
    ys = _experts(xs, start_blk, nblk.astype(I32), w_gate[l], w_up[l], w_down[l], EXPERT_BLOCK)
    (y0_p, y1_p), (y0_s, y1_s) = _collect_rows(ys, dests)

    y_p = _final(x1_p, y0_p, y1_p, ew_p, mod_p, ROW_TILE)
    y_s = _final(x1_s, y0_s, y1_s, ew_s, mod_s, ts)

    kv_shape = (1, -1, WINDOW, N_KV, HEAD_DIM)
    return (y_p, y_s, k_p.reshape(kv_shape), v_p.reshape(kv_shape), c_p[None],
            k_s.reshape(kv_shape), v_s.reshape(kv_shape), c_s[None])
```

```python
import functools
import math

import numpy as np
import jax
import jax.numpy as jnp
from jax import lax
from jax.experimental import pallas as pl
from jax.experimental.pallas import tpu as pltpu
from jax.experimental.pallas import tpu_sc as plsc

F32 = jnp.float32
BF16 = jnp.bfloat16
I32 = jnp.int32

D_MODEL = 1024
HEAD_DIM = 64
N_HEADS = 16
N_KV = 4
GROUP = 4
CHUNK = 64
WINDOW = 128
N_BUCKETS = 32
MAX_DISTANCE = 128
N_GROUPS = 8
EPG = 8
N_EXPERTS = 64
D_EXPERT = 512
RMS_EPS = 1e-6
KV_COLS = N_KV * HEAD_DIM
IN_COLS = 6 * D_MODEL + 2 * KV_COLS
HALF = D_MODEL // 2
LANES = 128

VMEM_LIMIT = 56 * 1024 * 1024
INPROJ_TN = 512
ROW_TILE = 512
MIX_TILE = 512
UNIT_Q = 2 * CHUNK
KEY_WIN = WINDOW + UNIT_Q
PROJ_CHUNK = 256
EXPERT_BLOCK = 256
EXPERT_BUFS = 4
SC_CORES = 2
SC_SUBCORES = 16
SC_WORKERS = SC_CORES * SC_SUBCORES
SC_MAX_WINDOW = 128


def _cparams(sem):
    return pltpu.CompilerParams(dimension_semantics=sem, vmem_limit_bytes=VMEM_LIMIT)


def _split_bf16(a):
    hi = a.astype(BF16)
    lo = (a - hi.astype(F32)).astype(BF16)
    return hi, lo


def _dot3(a, b):
    ah, al = _split_bf16(a)
    bh, bl = _split_bf16(b)
    d = functools.partial(jnp.dot, preferred_element_type=F32)
    return d(ah, bh) + (d(ah, bl) + d(al, bh))


def _pack_pairs(y):
    a = lax.bitcast_convert_type(y[:, :HALF].astype(BF16).astype(F32), I32)
    b = lax.bitcast_convert_type(y[:, HALF:].astype(BF16).astype(F32), I32)
    return a | lax.shift_right_logical(b, jnp.int32(16))


def _unpack_pairs(w):
    a = lax.bitcast_convert_type(w & jnp.int32(-65536), F32)
    b = lax.bitcast_convert_type(lax.shift_left(w, jnp.int32(16)), F32)
    return a, b


def _ada_kernel(c_ref, w_ref, b_ref, o_ref):
    c = c_ref[...]
    s = c * jax.nn.sigmoid(c)
    o_ref[...] = _dot3(s, w_ref[...]) + b_ref[...]


def _ada(c_all, w_ada, b_ada):
    r, d = c_all.shape
    n = w_ada.shape[1]
    tn = 1024
    return pl.pallas_call(
        _ada_kernel,
        out_shape=jax.ShapeDtypeStruct((r, n), F32),
        grid=(n // tn,),
        in_specs=[pl.BlockSpec((r, d), lambda j: (0, 0)),
                  pl.BlockSpec((d, tn), lambda j: (0, j)),
                  pl.BlockSpec((1, tn), lambda j: (0, j))],
        out_specs=pl.BlockSpec((r, tn), lambda j: (0, j)),
        compiler_params=_cparams(("arbitrary",)),
        name="ada",
    )(c_all, w_ada, b_ada.reshape(1, n))


def _inproj_kernel(x_ref, mod_ref, nw_ref, w_ref, o_ref):
    x = x_ref[0]
    mod = mod_ref[0]
    h = x * lax.rsqrt(jnp.mean(x * x, axis=-1, keepdims=True) + RMS_EPS) * nw_ref[...]
    h = h * (1.0 + mod[1]) + mod[0]
    hb = h.astype(BF16)
    for j in range(IN_COLS // INPROJ_TN):
        sl = slice(j * INPROJ_TN, (j + 1) * INPROJ_TN)
        o_ref[0, :, sl] = jnp.dot(hb, w_ref[:, sl], preferred_element_type=F32).astype(BF16)


def _inproj(x, mod4, nw, w_in_b, tm):
    b, t, d = x.shape
    mr = mod4.shape[2]
    assert mr == 1 or (mr == t and tm == t)
    return pl.pallas_call(
        _inproj_kernel,
        out_shape=jax.ShapeDtypeStruct((b, t, IN_COLS), BF16),
        grid=(b, t // tm),
        in_specs=[pl.BlockSpec((1, tm, d), lambda i, j: (i, j, 0)),
                  pl.BlockSpec((1, 6, mr, d), lambda i, j: (i, 0, 0, 0)),
                  pl.BlockSpec((1, d), lambda i, j: (0, 0)),
                  pl.BlockSpec((d, IN_COLS), lambda i, j: (0, 0), pipeline_mode=pl.Buffered(1))],
        out_specs=pl.BlockSpec((1, tm, IN_COLS), lambda i, j: (i, j, 0)),
        compiler_params=_cparams(("arbitrary", "arbitrary")),
        name="inproj",
    )(x, mod4, nw, w_in_b)


def _head_inv_rms(xf, bd):
    hi, lo = _split_bf16(xf * xf)
    ssq = jnp.dot(hi, bd, preferred_element_type=F32) + jnp.dot(lo, bd, preferred_element_type=F32)
    return lax.rsqrt(ssq * (1.0 / HEAD_DIM) + RMS_EPS)


def _mixer_body(q_ref, k_ref, v_ref, bg_ref, c_ref, u_ref, ga_ref, gc_ref,
                knw_ref, qsc_ref, bd_ref, bias_ref, sink_ref, cw_ref,
                kpast_ref, vpast_ref, cpast_ref, upast_ref,
                merged_ref, knew_ref, vnew_ref, cnew_ref,
                kq_buf, vt_buf, attn_buf, *, tq, nq, stateful):
    hb = WINDOW
    pw = UNIT_Q
    t = pl.program_id(1)
    bd = bd_ref[...]

    k = k_ref[0].astype(F32)
    kn = k * _head_inv_rms(k, bd) * knw_ref[...]
    kq = (kn * qsc_ref[...]).astype(BF16)
    vb = v_ref[0]
    vt = vb.astype(F32).T.astype(BF16)

    if stateful:
        kp = kpast_ref[0]
        vp = vpast_ref[0]
        for kv in range(N_KV):
            kq_buf[kv, hb + tq:] = jnp.zeros((KEY_WIN - hb - tq, HEAD_DIM), BF16)
        vt_buf[:, hb + tq:] = jnp.zeros((KV_COLS, KEY_WIN - hb - tq), BF16)
        u_hist = jnp.concatenate([jnp.zeros((6, D_MODEL), F32), cpast_ref[0]], axis=0)
        knew_ref[0] = jnp.concatenate([kp[tq:], kn], axis=0)
        vnew_ref[0] = jnp.concatenate([vp[tq:], vb.astype(F32)], axis=0)
    else:
        kraw = kpast_ref[0].astype(F32)
        kp = kraw * _head_inv_rms(kraw, bd) * knw_ref[...]
        vp = vpast_ref[0].astype(F32)
        u_hist = jnp.where(t == 0, 0.0, cpast_ref[0].astype(F32) * upast_ref[0].astype(F32))
        knew_ref[0] = kn[tq - hb:]
        vnew_ref[0] = vb[tq - hb:].astype(F32)
    _fill_keys(kq_buf, vt_buf, kp, vp, kq, vt, qsc_ref, tq)
    q = q_ref[0]
    _attention_units(lambda kv: q[:, kv * KV_COLS:(kv + 1) * KV_COLS].astype(F32), kq_buf, vt_buf,
                     bias_ref, sink_ref, bd, attn_buf, tq, nq, None if stateful else t)
    cu = c_ref[0].astype(F32) * u_ref[0].astype(F32)
    conv, u_all = _short_conv(u_hist, cu, cw_ref[...], tq)
    cnew_ref[0] = u_all[tq + 6:tq + 8]
    merged = (jax.nn.sigmoid(ga_ref[0].astype(F32)) * attn_buf[...]
              + jax.nn.sigmoid(gc_ref[0].astype(F32)) * (bg_ref[0].astype(F32) * conv))
    merged_ref[0] = merged.astype(BF16)


def _fill_keys(kq_buf, vt_buf, kp, vp, kq, vt, qsc_ref, tq):
    hb = WINDOW
    kqp = (kp * qsc_ref[...]).astype(BF16)
    for kv in range(N_KV):
        kq_buf[kv, 0:hb] = kqp[:, kv * HEAD_DIM:(kv + 1) * HEAD_DIM]
        kq_buf[kv, hb:hb + tq] = kq[:, kv * HEAD_DIM:(kv + 1) * HEAD_DIM]
    vt_buf[:, 0:hb] = vp.T.astype(BF16)
    vt_buf[:, hb:hb + tq] = vt


def _attention_units(q_group, kq_buf, vt_buf, bias_ref, sink_ref, bd, attn_buf, tq, nq, t_first, after_unit=None):
    pw = UNIT_Q
    for kv in range(N_KV):
        qf = q_group(kv)
        qn = (qf * _head_inv_rms(qf, bd)).astype(BF16)
        for u in range(tq // nq):
            r0 = u * nq
            parts = [qn[r0:r0 + nq, g * HEAD_DIM:(g + 1) * HEAD_DIM] for g in range(GROUP)]
            if nq < pw:
                zpad = jnp.zeros((pw - nq, HEAD_DIM), BF16)
                parts = [x for p_ in parts for x in (p_, zpad)]
            qs = jnp.concatenate(parts, axis=0)
            kw = kq_buf[kv, r0:r0 + KEY_WIN]
            st = lax.dot_general(kw, qs, (((1,), (1,)), ((), ())), preferred_element_type=F32)
            if t_first is not None and u == 0:
                bias = jnp.where(t_first == 0, bias_ref[kv + N_KV], bias_ref[kv])
            else:
                bias = bias_ref[kv]
            st = st + bias
            sink = sink_ref[kv]
            m = jnp.maximum(jnp.max(st, axis=0, keepdims=True), sink)
            p = jnp.exp(st - m)
            den = jnp.sum(p, axis=0, keepdims=True) + jnp.exp(sink - m)
            ot = jnp.dot(vt_buf[kv * HEAD_DIM:(kv + 1) * HEAD_DIM, r0:r0 + KEY_WIN], p.astype(BF16),
                         preferred_element_type=F32) / den
            for gp in range(GROUP // 2):
                blk = jnp.concatenate([ot[:, (2 * gp) * pw:(2 * gp + 1) * pw],
                                       ot[:, (2 * gp + 1) * pw:(2 * gp + 2) * pw]], axis=0)
                c0 = (kv * GROUP + 2 * gp) * HEAD_DIM
                attn_buf[r0:r0 + nq, c0:c0 + 2 * HEAD_DIM] = blk.T[:nq]
            if after_unit is not None:
                after_unit(kv * (tq // nq) + u)


def _short_conv(u_hist, cu, cw, tq):
    u_all = jnp.concatenate([u_hist, cu], axis=0)
    conv = cw[0:1] * u_all[6:6 + tq] + cw[1:2] * u_all[7:7 + tq] + cw[2:3] * u_all[8:8 + tq]
    return conv, u_all


def _front_body(x_ref, xp_ref, mod_ref, nw_ref, w_ref, knw_ref, qsc_ref, bd_ref, bias_ref, sink_ref, cw_ref,
                merged_ref, knew_ref, vnew_ref, cnew_ref,
                kq_buf, vt_buf, u_buf, attn_buf, pbuf, *, tq, nq):
    hb = WINDOW
    d = D_MODEL
    t = pl.program_id(1)
    bd = bd_ref[...]
    mod = mod_ref[0]

    @pl.when(t == 0)
    def _():
        u_buf[0] = jnp.zeros((8, d), F32)

    def modnorm(x):
        h = x * lax.rsqrt(jnp.mean(x * x, axis=-1, keepdims=True) + RMS_EPS) * nw_ref[...]
        return (h * (1.0 + mod[1:2]) + mod[0:1]).astype(BF16)

    def proj(hrows, c0, width):
        return jnp.dot(hrows, w_ref[:, c0:c0 + width], preferred_element_type=F32)

    hbf = modnorm(x_ref[0])
    kv_t = proj(hbf, 6 * d, 2 * KV_COLS)
    kv_p = proj(modnorm(xp_ref[0]), 6 * d, 2 * KV_COLS)
    k, v = kv_t[:, :KV_COLS], kv_t[:, KV_COLS:]
    kn = k * _head_inv_rms(k, bd) * knw_ref[...]
    kq = (kn * qsc_ref[...]).astype(BF16)
    kraw = kv_p[:, :KV_COLS]
    kp = kraw * _head_inv_rms(kraw, bd) * knw_ref[...]
    knew_ref[0] = kn[tq - hb:]
    vnew_ref[0] = v[tq - hb:]
    _fill_keys(kq_buf, vt_buf, kp, kv_p[:, KV_COLS:], kq, v.T.astype(BF16), qsc_ref, tq)

    n_units = N_KV * (tq // nq)
    q_chunks = [[g * KV_COLS + c for c in range(0, KV_COLS, PROJ_CHUNK)] for g in range(N_KV)]
    rest_chunks = list(range(d, 6 * d, PROJ_CHUNK))

    def issue(c0):
        pbuf[:, c0:c0 + PROJ_CHUNK] = proj(hbf, c0, PROJ_CHUNK)

    for c0 in q_chunks[0]:
        issue(c0)
    n_rest = len(rest_chunks)

    def after_unit(i):
        kv, u = divmod(i, tq // nq)
        if u == 0 and kv + 1 < N_KV:
            for c0 in q_chunks[kv + 1]:
                issue(c0)
        for c0 in rest_chunks[i * n_rest // n_units:(i + 1) * n_rest // n_units]:
            issue(c0)

    _attention_units(lambda kv: pbuf[:, kv * KV_COLS:(kv + 1) * KV_COLS], kq_buf, vt_buf,
                     bias_ref, sink_ref, bd, attn_buf, tq, nq, t, after_unit)

    cu = pbuf[:, 2 * d:3 * d] * pbuf[:, 3 * d:4 * d]
    conv, u_all = _short_conv(u_buf[t % 2], cu, cw_ref[...], tq)
    cnew_ref[0] = u_all[tq + 6:tq + 8]
    u_buf[(t + 1) % 2] = u_all[tq:tq + 8]
    merged = (jax.nn.sigmoid(pbuf[:, 4 * d:5 * d]) * attn_buf[...]
              + jax.nn.sigmoid(pbuf[:, 5 * d:6 * d]) * (pbuf[:, d:2 * d] * conv))
    merged_ref[0] = merged.astype(BF16)


def _front(x, mod, nw, w_in_b, knw, qsc, bias_tab, sink_tab, conv_w, tq, nq):
    b, t, d = x.shape
    r = np.arange(KV_COLS) // HEAD_DIM
    bd = jnp.asarray((r[:, None] == r[None, :]).astype(np.float32), BF16)
    const2 = lambda shp: pl.BlockSpec(shp, lambda i, s: (0, 0))
    const3 = lambda shp: pl.BlockSpec(shp, lambda i, s: (0, 0, 0))
    per_b = lambda shp: pl.BlockSpec(shp, lambda i, s: (i, 0, 0))
    kw_ = tq // WINDOW
    in_specs = [pl.BlockSpec((1, tq, d), lambda i, s: (i, s, 0)),
                pl.BlockSpec((1, WINDOW, d), lambda i, s: (i, jnp.maximum(s * kw_ - 1, 0), 0)),
                per_b((1, 6, d)), const2((1, d)),
                pl.BlockSpec((d, IN_COLS), lambda i, s: (0, 0), pipeline_mode=pl.Buffered(1)),
                const2((1, KV_COLS)), const2((1, KV_COLS)), const2((KV_COLS, KV_COLS)),
                const3(bias_tab.shape), const3(sink_tab.shape), const2((3, d))]
    out_shape = (jax.ShapeDtypeStruct((b, t, d), BF16),
                 jax.ShapeDtypeStruct((b, WINDOW, KV_COLS), F32),
                 jax.ShapeDtypeStruct((b, WINDOW, KV_COLS), F32),
                 jax.ShapeDtypeStruct((b, 2, d), F32))
    out_specs = (pl.BlockSpec((1, tq, d), lambda i, s: (i, s, 0)),
                 per_b((1, WINDOW, KV_COLS)), per_b((1, WINDOW, KV_COLS)), per_b((1, 2, d)))
    return pl.pallas_call(
        functools.partial(_front_body, tq=tq, nq=nq),
        out_shape=out_shape,
        grid=(b, t // tq),
        in_specs=in_specs,
        out_specs=out_specs,
        scratch_shapes=[pltpu.VMEM((N_KV, WINDOW + tq, HEAD_DIM), BF16),
                        pltpu.VMEM((KV_COLS, WINDOW + tq), BF16),
                        pltpu.VMEM((2, 8, d), F32),
                        pltpu.VMEM((tq, d), F32),
                        pltpu.VMEM((tq, 6 * d), F32)],
        compiler_params=_cparams(("arbitrary", "arbitrary")),
        name="front",
    )(x, x, mod, nw, w_in_b, knw, qsc, bd, bias_tab, sink_tab, conv_w)


def _mixer(proj, knw, qsc, bias_tab, sink_tab, conv_w, state, tq, nq):
    b, t, _ = proj.shape
    d = D_MODEL
    stateful = state is not None
    key_rows = max(WINDOW + tq, KEY_WIN)
    r = np.arange(KV_COLS) // HEAD_DIM
    bd = jnp.asarray((r[:, None] == r[None, :]).astype(np.float32), BF16)
    wide = lambda j: pl.BlockSpec((1, tq, d), lambda i, s, j=j: (i, s, j))
    kvspec = lambda j: pl.BlockSpec((1, tq, KV_COLS), lambda i, s, j=j: (i, s, j))
    const2 = lambda shp: pl.BlockSpec(shp, lambda i, s: (0, 0))
    const3 = lambda shp: pl.BlockSpec(shp, lambda i, s: (0, 0, 0))
    per_b = lambda shp: pl.BlockSpec(shp, lambda i, s: (i, 0, 0))
    kvblk = 6 * d // KV_COLS
    if stateful:
        hist_specs = [per_b((1, WINDOW, KV_COLS)), per_b((1, WINDOW, KV_COLS)), per_b((1, 2, d)), per_b((1, 2, d))]
        hist_args = [state[0], state[1], state[2], state[2]]
    else:
        kw_ = tq // WINDOW
        prev_kv = lambda j: pl.BlockSpec((1, WINDOW, KV_COLS),
                                         lambda i, s, j=j: (i, jnp.maximum(s * kw_ - 1, 0), j))
        prev8 = lambda j: pl.BlockSpec((1, 8, d), lambda i, s, j=j: (i, jnp.maximum(s * (tq // 8) - 1, 0), j))
        hist_specs = [prev_kv(kvblk), prev_kv(kvblk + 1), prev8(2), prev8(3)]
        hist_args = [proj, proj, proj, proj]
    in_specs = [wide(0), kvspec(kvblk), kvspec(kvblk + 1), wide(1), wide(2), wide(3), wide(4), wide(5),
                const2((1, KV_COLS)), const2((1, KV_COLS)), const2((KV_COLS, KV_COLS)),
                const3(bias_tab.shape), const3(sink_tab.shape), const2((3, d))] + hist_specs
    out_shape = (jax.ShapeDtypeStruct((b, t, d), BF16),
                 jax.ShapeDtypeStruct((b, WINDOW, KV_COLS), F32),
                 jax.ShapeDtypeStruct((b, WINDOW, KV_COLS), F32),
                 jax.ShapeDtypeStruct((b, 2, d), F32))
    out_specs = (pl.BlockSpec((1, tq, d), lambda i, s: (i, s, 0)),
                 per_b((1, WINDOW, KV_COLS)), per_b((1, WINDOW, KV_COLS)), per_b((1, 2, d)))
    return pl.pallas_call(
        functools.partial(_mixer_body, tq=tq, nq=nq, stateful=stateful),
        out_shape=out_shape,
        grid=(b, t // tq),
        in_specs=in_specs,
        out_specs=out_specs,
        scratch_shapes=[pltpu.VMEM((N_KV, key_rows, HEAD_DIM), BF16),
                        pltpu.VMEM((KV_COLS, key_rows), BF16),
                        pltpu.VMEM((tq, d), F32)],
        compiler_params=_cparams(("arbitrary", "arbitrary")),
        name="mixer_state" if stateful else "mixer",
    )(proj, proj, proj, proj, proj, proj, proj, proj, knw, qsc, bd, bias_tab, sink_tab, conv_w, *hist_args)


def _route(logits):
    lane = lax.broadcasted_iota(I32, logits.shape, 1)
    neg = -jnp.inf
    big = jnp.int32(1 << 20)
    gl = jnp.where(lane < N_GROUPS, logits, neg)
    gmax = jnp.max(gl, axis=-1, keepdims=True)
    g_idx = jnp.min(jnp.where(gl == gmax, lane, big), axis=-1, keepdims=True)
    g_w = 1.0 / jnp.sum(jnp.exp(gl - gmax), axis=-1, keepdims=True)
    lo = N_GROUPS + g_idx * EPG
    el = jnp.where((lane >= lo) & (lane < lo + EPG), logits, neg)
    m1 = jnp.max(el, axis=-1, keepdims=True)
    i1 = jnp.min(jnp.where(el == m1, lane, big), axis=-1, keepdims=True)
    el2 = jnp.where(lane == i1, neg, el)
    m2 = jnp.max(el2, axis=-1, keepdims=True)
    i2 = jnp.min(jnp.where(el2 == m2, lane, big), axis=-1, keepdims=True)
    r = jnp.exp(m2 - m1)
    w1 = 1.0 / (1.0 + r)
    w2 = r / (1.0 + r)
    return i1 - N_GROUPS, i2 - N_GROUPS, g_w * w1, g_w * w2


def _outproj_kernel(m_ref, x_ref, mod_ref, wo_ref, nw_ref, wr_ref, br_ref,
                    x1_ref, h2_ref, e01_ref, ew_ref):
    mod = mod_ref[0]
    mix = jnp.dot(m_ref[0], wo_ref[...], preferred_element_type=F32)
    x1 = x_ref[0] + mod[2] * mix
    x1_ref[0] = x1
    h = x1 * lax.rsqrt(jnp.mean(x1 * x1, axis=-1, keepdims=True) + RMS_EPS) * nw_ref[...]
    h = h * (1.0 + mod[4]) + mod[3]
    h2_ref[...] = _pack_pairs(h)
    tm = h.shape[0]
    h_hi, h_lo = _split_bf16(h)
    prod = jnp.dot(jnp.concatenate([h_hi, h_lo], axis=0), wr_ref[...], preferred_element_type=F32)
    logits = prod[:tm, :LANES] + (prod[:tm, LANES:] + prod[tm:, :LANES]) + br_ref[...]
    e1, e2, w1, w2 = _route(logits)
    e01_ref[0] = jnp.concatenate([_col_to_row(e1.astype(F32)), _col_to_row(e2.astype(F32))], axis=0).astype(I32)
    lane8 = lax.broadcasted_iota(I32, (h.shape[0], 8), 1)
    ew_ref[...] = jnp.where(lane8 == 0, w1, jnp.where(lane8 == 1, w2, 0.0))


def _outproj(merged, x, mod4, w_out_b, nw, w_r, b_r, tm):
    b, t, d = x.shape
    nt = t // tm
    mr = mod4.shape[2]
    assert mr == 1 or (mr == t and nt == 1)
    flat = lambda i, j: (i * nt + j, 0)
    return pl.pallas_call(
        _outproj_kernel,
        out_shape=(jax.ShapeDtypeStruct((b, t, d), F32),
                   jax.ShapeDtypeStruct((b * t, HALF), I32),
                   jax.ShapeDtypeStruct((b * nt, 2, tm), I32),
                   jax.ShapeDtypeStruct((b * t, 8), F32)),
        grid=(b, nt),
        in_specs=[pl.BlockSpec((1, tm, d), lambda i, j: (i, j, 0)),
                  pl.BlockSpec((1, tm, d), lambda i, j: (i, j, 0)),
                  pl.BlockSpec((1, 6, mr, d), lambda i, j: (i, 0, 0, 0)),
                  pl.BlockSpec((d, d), lambda i, j: (0, 0)),
                  pl.BlockSpec((1, d), lambda i, j: (0, 0)),
                  pl.BlockSpec((d, 2 * LANES), lambda i, j: (0, 0)),
                  pl.BlockSpec((1, LANES), lambda i, j: (0, 0))],
        out_specs=(pl.BlockSpec((1, tm, d), lambda i, j: (i, j, 0)),
                   pl.BlockSpec((tm, HALF), flat),
                   pl.BlockSpec((1, 2, tm), lambda i, j: (i * nt + j, 0, 0)),
                   pl.BlockSpec((tm, 8), flat)),
        compiler_params=_cparams(("arbitrary", "arbitrary")),
        name="outproj",
    )(merged, x, mod4, w_out_b, nw, w_r, b_r)


def _col_to_row(col):
    eye = lax.broadcasted_iota(I32, (LANES, LANES), 0) == lax.broadcasted_iota(I32, (LANES, LANES), 1)
    parts = [jnp.sum(jnp.where(eye, col[r * LANES:(r + 1) * LANES], 0.0), axis=0, keepdims=True)
             for r in range(col.shape[0] // LANES)]
    return jnp.concatenate(parts, axis=1)


def _rank_kernel(e_ref, tri_ref, low_ref, d_ref, tot_ref, *, block):
    n_sub, _, t = e_ref.shape
    sub = lax.broadcasted_iota(I32, (LANES, t), 0)

    def hots(s):
        e = e_ref[s]
        return sub == e[0:1], sub == e[1:2]

    def count(s, cnt):
        h0, h1 = hots(s)
        return cnt + jnp.sum(jnp.where(h0 | h1, 1.0, 0.0), axis=1, keepdims=True)

    cnt = lax.fori_loop(0, n_sub, count, jnp.zeros((LANES, 1), F32))
    tot_ref[...] = _col_to_row(cnt).astype(I32)
    nblk = jnp.floor((cnt + (block - 1)) * (1.0 / block))
    hi = jnp.floor(nblk * (1.0 / 16.0))
    lo = nblk - hi * 16.0
    low = low_ref[...]
    bcast = lambda c: jnp.broadcast_to(c, (LANES, LANES)).astype(BF16)
    excl = (jnp.dot(low, bcast(hi), preferred_element_type=F32) * 16.0
            + jnp.dot(low, bcast(lo), preferred_element_type=F32))
    starts = excl[:, 0:1] * float(block)

    def place(s, running):
        h0, h1 = hots(s)
        onehot = jnp.where(h0 | h1, 1.0, 0.0)
        prefix = jnp.dot(onehot.astype(BF16), tri_ref[...], preferred_element_type=F32)
        pos = prefix + running
        d0 = jnp.sum(jnp.where(h0, pos, 0.0), axis=0, keepdims=True)
        d1 = jnp.sum(jnp.where(h1, pos, 0.0), axis=0, keepdims=True)
        d_ref[s] = jnp.concatenate([d0, d1], axis=0).astype(I32)
        return running + jnp.sum(onehot, axis=1, keepdims=True)

    lax.fori_loop(0, n_sub, place, starts)


def _rank(e01, block):
    n_sub, _, t = e01.shape
    r = np.arange(t)
    tri = jnp.asarray((r[:, None] < r[None, :]).astype(np.float32), BF16)
    l = np.arange(LANES)
    low = jnp.asarray((l[None, :] < l[:, None]).astype(np.float32), BF16)
    return pl.pallas_call(
        functools.partial(_rank_kernel, block=block),
        out_shape=(jax.ShapeDtypeStruct((n_sub, 2, t), I32), jax.ShapeDtypeStruct((1, LANES), I32)),
        compiler_params=pltpu.CompilerParams(vmem_limit_bytes=VMEM_LIMIT),
        name="rank",
    )(e01, tri, low)


def _expert_kernel(start_ref, nblk_ref, xs_hbm, wg_ref, wu_ref, wd_ref, ys_hbm,
                   xbuf, ybuf, wg_s, wu_s, wd_s, sem_in, sem_out, *, block):
    nbuf = EXPERT_BUFS
    e = pl.program_id(0)
    n = nblk_ref[e]
    base = start_ref[e]
    total = start_ref[N_EXPERTS - 1] + nblk_ref[N_EXPERTS - 1]

    def in_copy(g):
        rows = pl.ds(pl.multiple_of(g * block, block), block)
        return pltpu.make_async_copy(xs_hbm.at[rows], xbuf.at[g % nbuf], sem_in.at[g % nbuf])

    def out_copy(g):
        rows = pl.ds(pl.multiple_of(g * block, block), block)
        return pltpu.make_async_copy(ybuf.at[g % nbuf], ys_hbm.at[rows], sem_out.at[g % nbuf])

    @pl.when(e == 0)
    def _():
        for g0 in range(nbuf - 1):
            @pl.when(g0 < total)
            def _(g0=g0):
                in_copy(g0).start()

    @pl.when(n > 0)
    def _():
        wg_s[...] = wg_ref[0].astype(BF16)
        wu_s[...] = wu_ref[0].astype(BF16)
        wd_s[...] = wd_ref[0].astype(BF16)

        def body(i, carry):
            g = base + i
            slot = g % nbuf
            in_copy(g).wait()

            @pl.when(g + nbuf - 1 < total)
            def _():
                in_copy(g + nbuf - 1).start()

            @pl.when(g >= nbuf)
            def _():
                out_copy(g - nbuf).wait()

            a, c = _unpack_pairs(xbuf[slot])
            x = jnp.concatenate([a.astype(BF16), c.astype(BF16)], axis=1)
            gate = jnp.dot(x, wg_s[...], preferred_element_type=F32)
            up = jnp.dot(x, wu_s[...], preferred_element_type=F32)
            hmid = (gate * jax.nn.sigmoid(gate) * up).astype(BF16)
            ybuf[slot] = _pack_pairs(jnp.dot(hmid, wd_s[...], preferred_element_type=F32))
            out_copy(g).start()
            return carry

        lax.fori_loop(0, n, body, 0)

    @pl.when(e == N_EXPERTS - 1)
    def _():
        for back in range(nbuf, 0, -1):
            @pl.when(total >= back)
            def _(back=back):
                out_copy(total - back).wait()


def _experts(xs, start_blk, nblk, w_gate, w_up, w_down, block):
    n_rows = xs.shape[0]
    wblk = lambda e, st, nb: (e, 0, 0)
    grid_spec = pltpu.PrefetchScalarGridSpec(
        num_scalar_prefetch=2,
        grid=(N_EXPERTS,),
        in_specs=[pl.BlockSpec(memory_space=pl.ANY),
                  pl.BlockSpec((1, D_MODEL, D_EXPERT), wblk),
                  pl.BlockSpec((1, D_MODEL, D_EXPERT), wblk),
                  pl.BlockSpec((1, D_EXPERT, D_MODEL), wblk)],
        out_specs=pl.BlockSpec(memory_space=pl.ANY),
        scratch_shapes=[pltpu.VMEM((EXPERT_BUFS, block, HALF), I32),
                        pltpu.VMEM((EXPERT_BUFS, block, HALF), I32),
                        pltpu.VMEM((D_MODEL, D_EXPERT), BF16),
                        pltpu.VMEM((D_MODEL, D_EXPERT), BF16),
                        pltpu.VMEM((D_EXPERT, D_MODEL), BF16),
                        pltpu.SemaphoreType.DMA((EXPERT_BUFS,)),
                        pltpu.SemaphoreType.DMA((EXPERT_BUFS,))])
    return pl.pallas_call(
        functools.partial(_expert_kernel, block=block),
        out_shape=jax.ShapeDtypeStruct((n_rows, HALF), I32),
        grid_spec=grid_spec,
        compiler_params=_cparams(("arbitrary",)),
        name="experts",
    )(start_blk, nblk, xs, w_gate, w_up, w_down)


def _final_kernel(x1_ref, y0_ref, y1_ref, ew_ref, mod_ref, o_ref):
    a0, b0 = _unpack_pairs(y0_ref[...])
    a1, b1 = _unpack_pairs(y1_ref[...])
    w0 = ew_ref[:, 0:1]
    w1 = ew_ref[:, 1:2]
    moe = jnp.concatenate([w0 * a0 + w1 * a1, w0 * b0 + w1 * b1], axis=1)
    o_ref[0] = x1_ref[0] + mod_ref[0][5:6] * moe


def _final(x1, y0, y1, ew, mod, tm):
    b, t, d = x1.shape
    nt = t // tm
    flat = lambda i, j: (i * nt + j, 0)
    return pl.pallas_call(
        _final_kernel,
        out_shape=jax.ShapeDtypeStruct((b, t, d), F32),
        grid=(b, nt),
        in_specs=[pl.BlockSpec((1, tm, d), lambda i, j: (i, j, 0)),
                  pl.BlockSpec((tm, HALF), flat),
                  pl.BlockSpec((tm, HALF), flat),
                  pl.BlockSpec((tm, 8), flat),
                  pl.BlockSpec((1, 6, d), lambda i, j: (i, 0, 0))],
        out_specs=pl.BlockSpec((1, tm, d), lambda i, j: (i, j, 0)),
        compiler_params=_cparams(("arbitrary", "arbitrary")),
        name="final",
    )(x1, y0, y1, ew, mod)


def _sc_window(rows_per_worker):
    for w in range(SC_MAX_WINDOW, 7, -8):
        if rows_per_worker % w == 0:
            return w
    raise ValueError(f"no SparseCore window divides {rows_per_worker} rows per worker")


def _sc_split(idx):
    n = idx.shape[0]
    per = n // SC_WORKERS
    assert per * SC_WORKERS == n
    win = _sc_window(per)
    return idx.reshape(SC_WORKERS, per // win, win), per // win, win


def _sc_worker_id():
    return lax.axis_index("s") * SC_CORES + lax.axis_index("c")


def _dispatch_rows(h2_groups, dest_groups, n_rows):
    splits = [(_sc_split(d0), _sc_split(d1)) for d0, d1 in dest_groups]
    ng = len(h2_groups)
    scratch = []
    for (_, _, win), _ in splits:
        scratch += [pltpu.VMEM((win,), I32), pltpu.VMEM((win,), I32), pltpu.VMEM((win, HALF), I32)]

    @functools.partial(
        pl.kernel,
        mesh=plsc.VectorSubcoreMesh(core_axis_name="c", subcore_axis_name="s"),
        out_type=jax.ShapeDtypeStruct((n_rows, HALF), I32),
        scratch_types=scratch,
        name="sc_dispatch",
    )
    def k(*refs):
        x_refs, idx_refs, o_hbm, bufs = refs[:ng], refs[ng:3 * ng], refs[3 * ng], refs[3 * ng + 1:]
        wid = _sc_worker_id()
        for g in range(ng):
            (_, nwin, win), _ = splits[g]
            x_hbm, d0_hbm, d1_hbm = x_refs[g], idx_refs[2 * g], idx_refs[2 * g + 1]
            i0_v, i1_v, rows_v = bufs[3 * g:3 * g + 3]

            @pl.loop(0, nwin)
            def _(j, nwin=nwin, win=win, x_hbm=x_hbm, d0_hbm=d0_hbm, d1_hbm=d1_hbm,
                  i0_v=i0_v, i1_v=i1_v, rows_v=rows_v):
                base = pl.multiple_of((wid * nwin + j) * win, 8)
                pltpu.sync_copy(d0_hbm.at[wid, j], i0_v)
                pltpu.sync_copy(d1_hbm.at[wid, j], i1_v)
                pltpu.sync_copy(x_hbm.at[pl.ds(base, win)], rows_v)
                pltpu.sync_copy(rows_v, o_hbm.at[i0_v])
                pltpu.sync_copy(rows_v, o_hbm.at[i1_v])

    idx_args = []
    for (s0, s1) in splits:
        idx_args += [s0[0], s1[0]]
    return k(*h2_groups, *idx_args)


def _collect_rows(ys, dest_groups):
    splits = [(_sc_split(d0), _sc_split(d1)) for d0, d1 in dest_groups]
    ng = len(dest_groups)
    outs, scratch = [], []
    for (d0, _), ((_, _, win), _) in zip(dest_groups, splits):
        o = jax.ShapeDtypeStruct((d0.shape[0], HALF), I32)
        outs += [o, o]
        scratch += [pltpu.VMEM((win,), I32), pltpu.VMEM((win, HALF), I32)]

    @functools.partial(
        pl.kernel,
        mesh=plsc.VectorSubcoreMesh(core_axis_name="c", subcore_axis_name="s"),
        out_type=tuple(outs),
        scratch_types=scratch,
        name="sc_collect",
    )
    def k(*refs):
        ys_hbm, idx_refs, out_refs, bufs = refs[0], refs[1:1 + 2 * ng], refs[1 + 2 * ng:1 + 4 * ng], refs[1 + 4 * ng:]
        wid = _sc_worker_id()
        for g in range(ng):
            (_, nwin, win), _ = splits[g]
            i_v, rows_v = bufs[2 * g:2 * g + 2]
            for kk in range(2):
                d_hbm, y_hbm = idx_refs[2 * g + kk], out_refs[2 * g + kk]

                @pl.loop(0, nwin)
                def _(j, nwin=nwin, win=win, d_hbm=d_hbm, y_hbm=y_hbm, i_v=i_v, rows_v=rows_v):
                    base = pl.multiple_of((wid * nwin + j) * win, 8)
                    pltpu.sync_copy(d_hbm.at[wid, j], i_v)
                    pltpu.sync_copy(ys_hbm.at[i_v], rows_v)
                    pltpu.sync_copy(rows_v, y_hbm.at[pl.ds(base, win)])

    idx_args = []
    for (s0, s1) in splits:
        idx_args += [s0[0], s1[0]]
    res = k(ys, *idx_args)
    return [(res[2 * g], res[2 * g + 1]) for g in range(ng)]


def _t5_bucket(rel):
    half = N_BUCKETS // 2
    max_exact = half // 2
    n = jnp.abs(rel)
    far = max_exact + (jnp.log(jnp.maximum(n, 1).astype(F32) / max_exact)
                       / math.log(MAX_DISTANCE / max_exact) * (half - max_exact)).astype(I32)
    far = jnp.minimum(far, half - 1)
    return jnp.where(rel > 0, half, 0) + jnp.where(n < max_exact, n, far)


def _bias_table(rel_bias, cq, nq, no_history):
    nk = WINDOW + cq
    j = jnp.arange(KEY_WIN)[:, None]
    c = jnp.arange(UNIT_Q)[None, :]
    jj = j - (c // cq) * cq
    valid = (jj >= 0) & (jj < nk) & (c < nq)
    if no_history:
        valid = valid & (j >= WINDOW)
    rel = jj - WINDOW - (c % cq)
    onehot = (_t5_bucket(rel)[:, :, None] == jnp.arange(N_BUCKETS)).astype(F32)
    bias = jnp.einsum("jcb,bh->jch", onehot, rel_bias.astype(F32), precision=lax.Precision.HIGHEST)
    bias = jnp.where(valid[:, :, None], bias, -jnp.inf)
    bias = jnp.transpose(bias.reshape(KEY_WIN, UNIT_Q, N_KV, GROUP), (2, 0, 3, 1))
    return bias.reshape(N_KV, KEY_WIN, GROUP * UNIT_Q)


def _sink_table(sinks):
    s = sinks.astype(F32).reshape(N_KV, 1, GROUP, 1)
    return jnp.broadcast_to(s, (N_KV, 1, GROUP, UNIT_Q)).reshape(N_KV, 1, GROUP * UNIT_Q)


def kernel(x_prompt, x_sample, state_attn_k, state_attn_v, state_conv, c_prompt, c_sample,
           rel_bias, w_ada, b_ada, norm1_w, w_in, q_norm_w, k_norm_w, attn_sinks, conv_w,
           w_out, norm2_w, w_router_group, b_router_group, w_router_expert, b_router_expert,
           w_gate, w_up, w_down):
    depth = w_ada.shape[0]
    assert depth == 1
    bp, tp, d = x_prompt.shape
    bs, ts, _ = x_sample.shape
    n_p, n_s = bp * tp, bs * ts
    n_tok = n_p + n_s
    l = 0

    wi = w_in[l]
    qw, kw, vw, rest = wi[:, :d], wi[:, d:d + KV_COLS], wi[:, d + KV_COLS:d + 2 * KV_COLS], wi[:, d + 2 * KV_COLS:]
    w_in_b = jnp.concatenate([qw, rest, kw, vw], axis=1).astype(BF16)
    w_out_b = w_out[l].astype(BF16)
    w_r = jnp.concatenate([w_router_group[l],
                           jnp.transpose(w_router_expert[l], (1, 0, 2)).reshape(d, N_EXPERTS),
                           jnp.zeros((d, LANES - N_GROUPS - N_EXPERTS), F32)], axis=1)
    w_r_hi = lax.reduce_precision(w_r, exponent_bits=8, mantissa_bits=7)
    w_r = jnp.concatenate([w_r_hi.astype(BF16), (w_r - w_r_hi).astype(BF16)], axis=1)
    b_r = jnp.concatenate([b_router_group[l], b_router_expert[l].reshape(-1),
                           jnp.zeros((LANES - N_GROUPS - N_EXPERTS,), F32)]).reshape(1, LANES)
    knw = jnp.tile(k_norm_w[l], N_KV).reshape(1, KV_COLS)
    qsc = jnp.tile(q_norm_w[l] * (HEAD_DIM ** -0.5), N_KV).reshape(1, KV_COLS)
    n1w = norm1_w[l].reshape(1, d)
    n2w = norm2_w[l].reshape(1, d)

    mod = _ada(jnp.concatenate([c_prompt, c_sample], axis=0), w_ada[l], b_ada[l]).reshape(bp + bs, 6, d)
    mod_p, mod_s = mod[:bp], mod[bp:]

    xs_rows = x_sample.reshape(1, n_s, d)
    mod4_p = mod_p[:, :, None, :]
    mod4_s = jnp.repeat(jnp.transpose(mod_s, (1, 0, 2)), ts, axis=1)[None]
    proj_s = _inproj(xs_rows, mod4_s, n1w, w_in_b, n_s).reshape(bs, ts, IN_COLS)
    sink_tab = _sink_table(attn_sinks[l])
    bias_p = jnp.concatenate([_bias_table(rel_bias, CHUNK, UNIT_Q, False),
                              _bias_table(rel_bias, CHUNK, UNIT_Q, True)], axis=0)
    merged_p, k_p, v_p, c_p = _front(x_prompt, mod_p, n1w, w_in_b, knw, qsc, bias_p, sink_tab, conv_w[l],
                                     MIX_TILE, UNIT_Q)
    state = (state_attn_k[l].reshape(bs, WINDOW, KV_COLS), state_attn_v[l].reshape(bs, WINDOW, KV_COLS),
             state_conv[l])
    merged_s, k_s, v_s, c_s = _mixer(proj_s, knw, qsc, _bias_table(rel_bias, ts, ts, False), sink_tab, conv_w[l],
                                     state, ts, ts)

    x1_p, h2_p, e01_p, ew_p = _outproj(merged_p, x_prompt, mod4_p, w_out_b, n2w, w_r, b_r, ROW_TILE)
    x1_s, h2_s, e01_s, ew_s = _outproj(merged_s.reshape(1, n_s, d), xs_rows, mod4_s, w_out_b, n2w, w_r, b_r, n_s)
    x1_s = x1_s.reshape(bs, ts, d)

    assert n_s == ROW_TILE
    d01, totals = _rank(jnp.concatenate([e01_p, e01_s], axis=0), EXPERT_BLOCK)
    n_sub_p = n_p // ROW_TILE
    dests = [(d01[:n_sub_p, 0].reshape(-1), d01[:n_sub_p, 1].reshape(-1)),
             (d01[n_sub_p:, 0].reshape(-1), d01[n_sub_p:, 1].reshape(-1))]
    nblk = (totals[0, :N_EXPERTS] + EXPERT_BLOCK - 1) // EXPERT_BLOCK
    start_blk = (jnp.cumsum(nblk) - nblk).astype(I32)
    nb_max = -(-2 * n_tok // EXPERT_BLOCK) + N_EXPERTS

    xs = _dispatch_rows([h2_p, h2_s], dests, nb_max * EXPERT_BLOCK)
    ys = _experts(xs, start_blk, nblk.astype(I32), w_gate[l], w_up[l], w_down[l], EXPERT_BLOCK)
    (y0_p, y1_p), (y0_s, y1_s) = _collect_rows(ys, dests)

    y_p = _final(x1_p, y0_p, y1_p, ew_p, mod_p, ROW_TILE)
    y_s = _final(x1_s, y0_s, y1_s, ew_s, mod_s, ts)

    kv_shape = (1, -1, WINDOW, N_KV, HEAD_DIM)
    return (y_p, y_s, k_p.reshape(kv_shape), v_p.reshape(kv_shape), c_p[None],
            k_s.reshape(kv_shape), v_s.reshape(kv_shape), c_s[None])
```

```python
import functools
import math

import numpy as np
import jax
import jax.numpy as jnp
from jax import lax
from jax.experimental import pallas as pl
from jax.experimental.pallas import tpu as pltpu
from jax.experimental.pallas import tpu_sc as plsc

F32 = jnp.float32
BF16 = jnp.bfloat16
I32 = jnp.int32

D_MODEL = 1024
HEAD_DIM = 64
N_HEADS = 16
N_KV = 4
GROUP = 4
CHUNK = 64
WINDOW = 128
N_BUCKETS = 32
MAX_DISTANCE = 128
N_GROUPS = 8
EPG = 8
N_EXPERTS = 64
D_EXPERT = 512
RMS_EPS = 1e-6
KV_COLS = N_KV * HEAD_DIM
IN_COLS = 6 * D_MODEL + 2 * KV_COLS
HALF = D_MODEL // 2
LANES = 128

VMEM_LIMIT = 56 * 1024 * 1024
INPROJ_TN = 512
ROW_TILE = 512
MIX_TILE = 512
UNIT_Q = 2 * CHUNK
KEY_WIN = WINDOW + UNIT_Q
PROJ_CHUNK = 256
EPILOGUE_LAG = 2
EXPERT_BLOCK = 256
EXPERT_BUFS = 4
SC_CORES = 2
SC_SUBCORES = 16
SC_WORKERS = SC_CORES * SC_SUBCORES
SC_MAX_WINDOW = 128


def _cparams(sem):
    return pltpu.CompilerParams(dimension_semantics=sem, vmem_limit_bytes=VMEM_LIMIT)


def _split_bf16(a):
    hi = a.astype(BF16)
    lo = (a - hi.astype(F32)).astype(BF16)
    return hi, lo


def _dot3(a, b):
    ah, al = _split_bf16(a)
    bh, bl = _split_bf16(b)
    d = functools.partial(jnp.dot, preferred_element_type=F32)
    return d(ah, bh) + (d(ah, bl) + d(al, bh))


def _sigmoid(x):
    return 0.5 * jnp.tanh(0.5 * x) + 0.5


def _pack_pairs(y):
    a = lax.bitcast_convert_type(y[:, :HALF].astype(BF16).astype(F32), I32)
    b = lax.bitcast_convert_type(y[:, HALF:].astype(BF16).astype(F32), I32)
    return a | lax.shift_right_logical(b, jnp.int32(16))


def _unpack_pairs(w):
    a = lax.bitcast_convert_type(w & jnp.int32(-65536), F32)
    b = lax.bitcast_convert_type(lax.shift_left(w, jnp.int32(16)), F32)
    return a, b


def _ada_kernel(c_ref, w_ref, b_ref, o_ref):
    c = c_ref[...]
    s = c * jax.nn.sigmoid(c)
    o_ref[...] = _dot3(s, w_ref[...]) + b_ref[...]


def _ada(c_all, w_ada, b_ada):
    r, d = c_all.shape
    n = w_ada.shape[1]
    tn = 1024
    return pl.pallas_call(
        _ada_kernel,
        out_shape=jax.ShapeDtypeStruct((r, n), F32),
        grid=(n // tn,),
        in_specs=[pl.BlockSpec((r, d), lambda j: (0, 0)),
                  pl.BlockSpec((d, tn), lambda j: (0, j)),
                  pl.BlockSpec((1, tn), lambda j: (0, j))],
        out_specs=pl.BlockSpec((r, tn), lambda j: (0, j)),
        compiler_params=_cparams(("arbitrary",)),
        name="ada",
    )(c_all, w_ada, b_ada.reshape(1, n))


def _inproj_kernel(x_ref, mod_ref, nw_ref, w_ref, o_ref):
    x = x_ref[0]
    mod = mod_ref[0]
    h = x * lax.rsqrt(jnp.mean(x * x, axis=-1, keepdims=True) + RMS_EPS) * nw_ref[...]
    h = h * (1.0 + mod[1]) + mod[0]
    hb = h.astype(BF16)
    for j in range(IN_COLS // INPROJ_TN):
        sl = slice(j * INPROJ_TN, (j + 1) * INPROJ_TN)
        o_ref[0, :, sl] = jnp.dot(hb, w_ref[:, sl], preferred_element_type=F32).astype(BF16)


def _inproj(x, mod4, nw, w_in_b, tm):
    b, t, d = x.shape
    mr = mod4.shape[2]
    assert mr == 1 or (mr == t and tm == t)
    return pl.pallas_call(
        _inproj_kernel,
        out_shape=jax.ShapeDtypeStruct((b, t, IN_COLS), BF16),
        grid=(b, t // tm),
        in_specs=[pl.BlockSpec((1, tm, d), lambda i, j: (i, j, 0)),
                  pl.BlockSpec((1, 6, mr, d), lambda i, j: (i, 0, 0, 0)),
                  pl.BlockSpec((1, d), lambda i, j: (0, 0)),
                  pl.BlockSpec((d, IN_COLS), lambda i, j: (0, 0), pipeline_mode=pl.Buffered(1))],
        out_specs=pl.BlockSpec((1, tm, IN_COLS), lambda i, j: (i, j, 0)),
        compiler_params=_cparams(("arbitrary", "arbitrary")),
        name="inproj",
    )(x, mod4, nw, w_in_b)


def _head_inv_rms(xf, bd):
    hi, lo = _split_bf16(xf * xf)
    ssq = jnp.dot(hi, bd, preferred_element_type=F32) + jnp.dot(lo, bd, preferred_element_type=F32)
    return lax.rsqrt(ssq * (1.0 / HEAD_DIM) + RMS_EPS)


def _mixer_body(q_ref, k_ref, v_ref, bg_ref, c_ref, u_ref, ga_ref, gc_ref,
                knw_ref, qsc_ref, bd_ref, bias_ref, sink_ref, cw_ref,
                kpast_ref, vpast_ref, cpast_ref, upast_ref,
                merged_ref, knew_ref, vnew_ref, cnew_ref,
                kq_buf, vt_buf, attn_buf, *, tq, nq, stateful):
    hb = WINDOW
    pw = UNIT_Q
    t = pl.program_id(1)
    bd = bd_ref[...]

    k = k_ref[0].astype(F32)
    kn = k * _head_inv_rms(k, bd) * knw_ref[...]
    kq = (kn * qsc_ref[...]).astype(BF16)
    vb = v_ref[0]
    vt = vb.astype(F32).T.astype(BF16)

    if stateful:
        kp = kpast_ref[0]
        vp = vpast_ref[0]
        for kv in range(N_KV):
            kq_buf[kv, hb + tq:] = jnp.zeros((KEY_WIN - hb - tq, HEAD_DIM), BF16)
        vt_buf[:, hb + tq:] = jnp.zeros((KV_COLS, KEY_WIN - hb - tq), BF16)
        u_hist = jnp.concatenate([jnp.zeros((6, D_MODEL), F32), cpast_ref[0]], axis=0)
        knew_ref[0] = jnp.concatenate([kp[tq:], kn], axis=0)
        vnew_ref[0] = jnp.concatenate([vp[tq:], vb.astype(F32)], axis=0)
    else:
        kraw = kpast_ref[0].astype(F32)
        kp = kraw * _head_inv_rms(kraw, bd) * knw_ref[...]
        vp = vpast_ref[0].astype(F32)
        u_hist = jnp.where(t == 0, 0.0, cpast_ref[0].astype(F32) * upast_ref[0].astype(F32))
        knew_ref[0] = kn[tq - hb:]
        vnew_ref[0] = vb[tq - hb:].astype(F32)
    _fill_keys(kq_buf, vt_buf, kp, vp, kq, vt, qsc_ref, tq)
    q = q_ref[0]
    _attention_units(lambda kv: q[:, kv * KV_COLS:(kv + 1) * KV_COLS].astype(F32), kq_buf, vt_buf,
                     bias_ref, sink_ref, bd, attn_buf, tq, nq, None if stateful else t)
    cu = c_ref[0].astype(F32) * u_ref[0].astype(F32)
    conv, u_all = _short_conv(u_hist, cu, cw_ref[...], tq)
    cnew_ref[0] = u_all[tq + 6:tq + 8]
    merged = (_sigmoid(ga_ref[0].astype(F32)) * attn_buf[...]
              + _sigmoid(gc_ref[0].astype(F32)) * (bg_ref[0].astype(F32) * conv))
    merged_ref[0] = merged.astype(BF16)


def _fill_keys(kq_buf, vt_buf, kp, vp, kq, vt, qsc_ref, tq):
    hb = WINDOW
    kqp = (kp * qsc_ref[...]).astype(BF16)
    for kv in range(N_KV):
        kq_buf[kv, 0:hb] = kqp[:, kv * HEAD_DIM:(kv + 1) * HEAD_DIM]
        kq_buf[kv, hb:hb + tq] = kq[:, kv * HEAD_DIM:(kv + 1) * HEAD_DIM]
    vt_buf[:, 0:hb] = vp.T.astype(BF16)
    vt_buf[:, hb:hb + tq] = vt


def _attention_units(q_group, kq_buf, vt_buf, bias_ref, sink_ref, bd, attn_buf, tq, nq, t_first, after_unit=None):
    pw = UNIT_Q
    for kv in range(N_KV):
        qf = q_group(kv)
        qn = (qf * _head_inv_rms(qf, bd)).astype(BF16)
        for u in range(tq // nq):
            r0 = u * nq
            parts = [qn[r0:r0 + nq, g * HEAD_DIM:(g + 1) * HEAD_DIM] for g in range(GROUP)]
            if nq < pw:
                zpad = jnp.zeros((pw - nq, HEAD_DIM), BF16)
                parts = [x for p_ in parts for x in (p_, zpad)]
            qs = jnp.concatenate(parts, axis=0)
            kw = kq_buf[kv, r0:r0 + KEY_WIN]
            st = lax.dot_general(kw, qs, (((1,), (1,)), ((), ())), preferred_element_type=F32)
            if t_first is not None and u == 0:
                bias = jnp.where(t_first == 0, bias_ref[kv + N_KV], bias_ref[kv])
            else:
                bias = bias_ref[kv]
            st = st + bias
            sink = sink_ref[kv]
            m = jnp.maximum(jnp.max(st, axis=0, keepdims=True), sink)
            p = jnp.exp(st - m)
            den = jnp.sum(p, axis=0, keepdims=True) + jnp.exp(sink - m)
            ot = jnp.dot(vt_buf[kv * HEAD_DIM:(kv + 1) * HEAD_DIM, r0:r0 + KEY_WIN], p.astype(BF16),
                         preferred_element_type=F32) / den
            for gp in range(GROUP // 2):
                blk = jnp.concatenate([ot[:, (2 * gp) * pw:(2 * gp + 1) * pw],
                                       ot[:, (2 * gp + 1) * pw:(2 * gp + 2) * pw]], axis=0)
                c0 = (kv * GROUP + 2 * gp) * HEAD_DIM
                attn_buf[r0:r0 + nq, c0:c0 + 2 * HEAD_DIM] = blk.T[:nq]
            if after_unit is not None:
                after_unit(kv * (tq // nq) + u)


def _short_conv(u_hist, cu, cw, tq):
    u_all = jnp.concatenate([u_hist, cu], axis=0)
    conv = cw[0:1] * u_all[6:6 + tq] + cw[1:2] * u_all[7:7 + tq] + cw[2:3] * u_all[8:8 + tq]
    return conv, u_all


def _front_body(x_ref, xp_ref, mod_ref, nw_ref, w_ref, knw_ref, qsc_ref, bd_ref, bias_ref, sink_ref, cw_ref,
                merged_ref, knew_ref, vnew_ref, cnew_ref,
                kq_buf, vt_buf, u_buf, attn_buf, pbuf, *, tq, nq):
    hb = WINDOW
    d = D_MODEL
    t = pl.program_id(1)
    bd = bd_ref[...]
    mod = mod_ref[0]

    @pl.when(t == 0)
    def _():
        u_buf[0] = jnp.zeros((8, d), F32)

    def modnorm(x):
        h = x * lax.rsqrt(jnp.mean(x * x, axis=-1, keepdims=True) + RMS_EPS) * nw_ref[...]
        return (h * (1.0 + mod[1:2]) + mod[0:1]).astype(BF16)

    def proj(hrows, c0, width):
        return jnp.dot(hrows, w_ref[:, c0:c0 + width], preferred_element_type=F32)

    hbf = modnorm(x_ref[0])
    kv_t = proj(hbf, 6 * d, 2 * KV_COLS)
    kv_p = proj(modnorm(xp_ref[0]), 6 * d, 2 * KV_COLS)
    k, v = kv_t[:, :KV_COLS], kv_t[:, KV_COLS:]
    kn = k * _head_inv_rms(k, bd) * knw_ref[...]
    kq = (kn * qsc_ref[...]).astype(BF16)
    kraw = kv_p[:, :KV_COLS]
    kp = kraw * _head_inv_rms(kraw, bd) * knw_ref[...]
    knew_ref[0] = kn[tq - hb:]
    vnew_ref[0] = v[tq - hb:]
    _fill_keys(kq_buf, vt_buf, kp, kv_p[:, KV_COLS:], kq, v.T.astype(BF16), qsc_ref, tq)

    n_units = N_KV * (tq // nq)
    n_cb = d // PROJ_CHUNK
    q_chunks = [[g * KV_COLS + c for c in range(0, KV_COLS, PROJ_CHUNK)] for g in range(N_KV)]
    block_chunks = [[k * d + cb * PROJ_CHUNK for k in range(1, 6)] for cb in range(n_cb)]
    rest_chunks = [c0 for cb in range(n_cb - EPILOGUE_LAG) for c0 in block_chunks[cb]]

    def issue(c0):
        pbuf[:, c0:c0 + PROJ_CHUNK] = proj(hbf, c0, PROJ_CHUNK)

    for c0 in q_chunks[0]:
        issue(c0)
    n_rest = len(rest_chunks)

    def after_unit(i):
        kv, u = divmod(i, tq // nq)
        if u == 0 and kv + 1 < N_KV:
            for c0 in q_chunks[kv + 1]:
                issue(c0)
        for c0 in rest_chunks[i * n_rest // n_units:(i + 1) * n_rest // n_units]:
            issue(c0)

    _attention_units(lambda kv: pbuf[:, kv * KV_COLS:(kv + 1) * KV_COLS], kq_buf, vt_buf,
                     bias_ref, sink_ref, bd, attn_buf, tq, nq, t, after_unit)

    for cb in range(n_cb):
        if cb + n_cb - EPILOGUE_LAG < n_cb:
            for c0 in block_chunks[cb + n_cb - EPILOGUE_LAG]:
                issue(c0)
        sl = slice(cb * PROJ_CHUNK, (cb + 1) * PROJ_CHUNK)
        part = lambda k, cb=cb: pbuf[:, k * d + cb * PROJ_CHUNK:k * d + (cb + 1) * PROJ_CHUNK]
        conv, u_all = _short_conv(u_buf[t % 2, :, sl], part(2) * part(3), cw_ref[:, sl], tq)
        cnew_ref[0, :, sl] = u_all[tq + 6:tq + 8]
        u_buf[(t + 1) % 2, :, sl] = u_all[tq:tq + 8]
        merged = _sigmoid(part(4)) * attn_buf[:, sl] + _sigmoid(part(5)) * (part(1) * conv)
        merged_ref[0, :, sl] = merged.astype(BF16)


def _front(x, mod, nw, w_in_b, knw, qsc, bias_tab, sink_tab, conv_w, tq, nq):
    b, t, d = x.shape
    r = np.arange(KV_COLS) // HEAD_DIM
    bd = jnp.asarray((r[:, None] == r[None, :]).astype(np.float32), BF16)
    const2 = lambda shp: pl.BlockSpec(shp, lambda i, s: (0, 0))
    const3 = lambda shp: pl.BlockSpec(shp, lambda i, s: (0, 0, 0))
    per_b = lambda shp: pl.BlockSpec(shp, lambda i, s: (i, 0, 0))
    kw_ = tq // WINDOW
    in_specs = [pl.BlockSpec((1, tq, d), lambda i, s: (i, s, 0)),
                pl.BlockSpec((1, WINDOW, d), lambda i, s: (i, jnp.maximum(s * kw_ - 1, 0), 0)),
                per_b((1, 6, d)), const2((1, d)),
                pl.BlockSpec((d, IN_COLS), lambda i, s: (0, 0), pipeline_mode=pl.Buffered(1)),
                const2((1, KV_COLS)), const2((1, KV_COLS)), const2((KV_COLS, KV_COLS)),
                const3(bias_tab.shape), const3(sink_tab.shape), const2((3, d))]
    out_shape = (jax.ShapeDtypeStruct((b, t, d), BF16),
                 jax.ShapeDtypeStruct((b, WINDOW, KV_COLS), F32),
                 jax.ShapeDtypeStruct((b, WINDOW, KV_COLS), F32),
                 jax.ShapeDtypeStruct((b, 2, d), F32))
    out_specs = (pl.BlockSpec((1, tq, d), lambda i, s: (i, s, 0)),
                 per_b((1, WINDOW, KV_COLS)), per_b((1, WINDOW, KV_COLS)), per_b((1, 2, d)))
    return pl.pallas_call(
        functools.partial(_front_body, tq=tq, nq=nq),
        out_shape=out_shape,
        grid=(b, t // tq),
        in_specs=in_specs,
        out_specs=out_specs,
        scratch_shapes=[pltpu.VMEM((N_KV, WINDOW + tq, HEAD_DIM), BF16),
                        pltpu.VMEM((KV_COLS, WINDOW + tq), BF16),
                        pltpu.VMEM((2, 8, d), F32),
                        pltpu.VMEM((tq, d), F32),
                        pltpu.VMEM((tq, 6 * d), F32)],
        compiler_params=_cparams(("arbitrary", "arbitrary")),
        name="front",
    )(x, x, mod, nw, w_in_b, knw, qsc, bd, bias_tab, sink_tab, conv_w)


def _mixer(proj, knw, qsc, bias_tab, sink_tab, conv_w, state, tq, nq):
    b, t, _ = proj.shape
    d = D_MODEL
    stateful = state is not None
    key_rows = max(WINDOW + tq, KEY_WIN)
    r = np.arange(KV_COLS) // HEAD_DIM
    bd = jnp.asarray((r[:, None] == r[None, :]).astype(np.float32), BF16)
    wide = lambda j: pl.BlockSpec((1, tq, d), lambda i, s, j=j: (i, s, j))
    kvspec = lambda j: pl.BlockSpec((1, tq, KV_COLS), lambda i, s, j=j: (i, s, j))
    const2 = lambda shp: pl.BlockSpec(shp, lambda i, s: (0, 0))
    const3 = lambda shp: pl.BlockSpec(shp, lambda i, s: (0, 0, 0))
    per_b = lambda shp: pl.BlockSpec(shp, lambda i, s: (i, 0, 0))
    kvblk = 6 * d // KV_COLS
    if stateful:
        hist_specs = [per_b((1, WINDOW, KV_COLS)), per_b((1, WINDOW, KV_COLS)), per_b((1, 2, d)), per_b((1, 2, d))]
        hist_args = [state[0], state[1], state[2], state[2]]
    else:
        kw_ = tq // WINDOW
        prev_kv = lambda j: pl.BlockSpec((1, WINDOW, KV_COLS),
                                         lambda i, s, j=j: (i, jnp.maximum(s * kw_ - 1, 0), j))
        prev8 = lambda j: pl.BlockSpec((1, 8, d), lambda i, s, j=j: (i, jnp.maximum(s * (tq // 8) - 1, 0), j))
        hist_specs = [prev_kv(kvblk), prev_kv(kvblk + 1), prev8(2), prev8(3)]
        hist_args = [proj, proj, proj, proj]
    in_specs = [wide(0), kvspec(kvblk), kvspec(kvblk + 1), wide(1), wide(2), wide(3), wide(4), wide(5),
                const2((1, KV_COLS)), const2((1, KV_COLS)), const2((KV_COLS, KV_COLS)),
                const3(bias_tab.shape), const3(sink_tab.shape), const2((3, d))] + hist_specs
    out_shape = (jax.ShapeDtypeStruct((b, t, d), BF16),
                 jax.ShapeDtypeStruct((b, WINDOW, KV_COLS), F32),
                 jax.ShapeDtypeStruct((b, WINDOW, KV_COLS), F32),
                 jax.ShapeDtypeStruct((b, 2, d), F32))
    out_specs = (pl.BlockSpec((1, tq, d), lambda i, s: (i, s, 0)),
                 per_b((1, WINDOW, KV_COLS)), per_b((1, WINDOW, KV_COLS)), per_b((1, 2, d)))
    return pl.pallas_call(
        functools.partial(_mixer_body, tq=tq, nq=nq, stateful=stateful),
        out_shape=out_shape,
        grid=(b, t // tq),
        in_specs=in_specs,
        out_specs=out_specs,
        scratch_shapes=[pltpu.VMEM((N_KV, key_rows, HEAD_DIM), BF16),
                        pltpu.VMEM((KV_COLS, key_rows), BF16),
                        pltpu.VMEM((tq, d), F32)],
        compiler_params=_cparams(("arbitrary", "arbitrary")),
        name="mixer_state" if stateful else "mixer",
    )(proj, proj, proj, proj, proj, proj, proj, proj, knw, qsc, bd, bias_tab, sink_tab, conv_w, *hist_args)


def _route(logits):
    lane = lax.broadcasted_iota(I32, logits.shape, 1).astype(F32)
    neg = -jnp.inf
    big = float(1 << 20)
    gl = jnp.where(lane < N_GROUPS, logits, neg)
    gmax = jnp.max(gl, axis=-1, keepdims=True)
    g_idx = jnp.min(jnp.where(gl == gmax, lane, big), axis=-1, keepdims=True)
    g_w = 1.0 / jnp.sum(jnp.exp(gl - gmax), axis=-1, keepdims=True)
    lo = N_GROUPS + g_idx * EPG
    el = jnp.where((lane >= lo) & (lane < lo + EPG), logits, neg)
    m1 = jnp.max(el, axis=-1, keepdims=True)
    i1 = jnp.min(jnp.where(el == m1, lane, big), axis=-1, keepdims=True)
    el2 = jnp.where(lane == i1, neg, el)
    m2 = jnp.max(el2, axis=-1, keepdims=True)
    i2 = jnp.min(jnp.where(el2 == m2, lane, big), axis=-1, keepdims=True)
    r = jnp.exp(m2 - m1)
    w1 = 1.0 / (1.0 + r)
    w2 = r / (1.0 + r)
    return i1 - N_GROUPS, i2 - N_GROUPS, g_w * w1, g_w * w2


def _outproj_kernel(m_ref, x_ref, mod_ref, wo_ref, nw_ref, wr_ref, br_ref,
                    x1_ref, h2_ref, e01_ref, ew_ref):
    mod = mod_ref[0]
    mix = jnp.dot(m_ref[0], wo_ref[...], preferred_element_type=F32)
    x1 = x_ref[0] + mod[2] * mix
    x1_ref[0] = x1
    h = x1 * lax.rsqrt(jnp.mean(x1 * x1, axis=-1, keepdims=True) + RMS_EPS) * nw_ref[...]
    h = h * (1.0 + mod[4]) + mod[3]
    h2_ref[...] = _pack_pairs(h)
    tm = h.shape[0]
    h_hi, h_lo = _split_bf16(h)
    prod = jnp.dot(jnp.concatenate([h_hi, h_lo], axis=0), wr_ref[...], preferred_element_type=F32)
    logits = prod[:tm, :LANES] + (prod[:tm, LANES:] + prod[tm:, :LANES]) + br_ref[...]
    e1, e2, w1, w2 = _route(logits)
    e01_ref[0] = jnp.concatenate([_col_to_row(e1), _col_to_row(e2)], axis=0).astype(I32)
    lane8 = lax.broadcasted_iota(I32, (h.shape[0], 8), 1)
    ew_ref[...] = jnp.where(lane8 == 0, w1, jnp.where(lane8 == 1, w2, 0.0))


def _outproj(merged, x, mod4, w_out_b, nw, w_r, b_r, tm):
    b, t, d = x.shape
    nt = t // tm
    mr = mod4.shape[2]
    assert mr == 1 or (mr == t and nt == 1)
    flat = lambda i, j: (i * nt + j, 0)
    return pl.pallas_call(
        _outproj_kernel,
        out_shape=(jax.ShapeDtypeStruct((b, t, d), F32),
                   jax.ShapeDtypeStruct((b * t, HALF), I32),
                   jax.ShapeDtypeStruct((b * nt, 2, tm), I32),
                   jax.ShapeDtypeStruct((b * t, 8), F32)),
        grid=(b, nt),
        in_specs=[pl.BlockSpec((1, tm, d), lambda i, j: (i, j, 0)),
                  pl.BlockSpec((1, tm, d), lambda i, j: (i, j, 0)),
                  pl.BlockSpec((1, 6, mr, d), lambda i, j: (i, 0, 0, 0)),
                  pl.BlockSpec((d, d), lambda i, j: (0, 0)),
                  pl.BlockSpec((1, d), lambda i, j: (0, 0)),
                  pl.BlockSpec((d, 2 * LANES), lambda i, j: (0, 0)),
                  pl.BlockSpec((1, LANES), lambda i, j: (0, 0))],
        out_specs=(pl.BlockSpec((1, tm, d), lambda i, j: (i, j, 0)),
                   pl.BlockSpec((tm, HALF), flat),
                   pl.BlockSpec((1, 2, tm), lambda i, j: (i * nt + j, 0, 0)),
                   pl.BlockSpec((tm, 8), flat)),
        compiler_params=_cparams(("arbitrary", "arbitrary")),
        name="outproj",
    )(merged, x, mod4, w_out_b, nw, w_r, b_r)


def _col_to_row(col):
    eye = lax.broadcasted_iota(I32, (LANES, LANES), 0) == lax.broadcasted_iota(I32, (LANES, LANES), 1)
    parts = [jnp.sum(jnp.where(eye, col[r * LANES:(r + 1) * LANES], 0.0), axis=0, keepdims=True)
             for r in range(col.shape[0] // LANES)]
    return jnp.concatenate(parts, axis=1)


def _rank_kernel(e_ref, tri_ref, low_ref, d_ref, tot_ref, *, block):
    n_sub, _, t = e_ref.shape
    sub = lax.broadcasted_iota(I32, (LANES, t), 0)

    def hots(s):
        e = e_ref[s]
        return sub == e[0:1], sub == e[1:2]

    def count(s, cnt):
        h0, h1 = hots(s)
        return cnt + jnp.sum(jnp.where(h0 | h1, 1.0, 0.0), axis=1, keepdims=True)

    cnt = lax.fori_loop(0, n_sub, count, jnp.zeros((LANES, 1), F32))
    tot_ref[...] = _col_to_row(cnt).astype(I32)
    nblk = jnp.floor((cnt + (block - 1)) * (1.0 / block))
    hi = jnp.floor(nblk * (1.0 / 16.0))
    lo = nblk - hi * 16.0
    low = low_ref[...]
    bcast = lambda c: jnp.broadcast_to(c, (LANES, LANES)).astype(BF16)
    excl = (jnp.dot(low, bcast(hi), preferred_element_type=F32) * 16.0
            + jnp.dot(low, bcast(lo), preferred_element_type=F32))
    starts = excl[:, 0:1] * float(block)

    def place(s, running):
        h0, h1 = hots(s)
        onehot = jnp.where(h0 | h1, 1.0, 0.0)
        prefix = jnp.dot(onehot.astype(BF16), tri_ref[...], preferred_element_type=F32)
        pos = prefix + running
        d0 = jnp.sum(jnp.where(h0, pos, 0.0), axis=0, keepdims=True)
        d1 = jnp.sum(jnp.where(h1, pos, 0.0), axis=0, keepdims=True)
        d_ref[s] = jnp.concatenate([d0, d1], axis=0).astype(I32)
        return running + jnp.sum(onehot, axis=1, keepdims=True)

    lax.fori_loop(0, n_sub, place, starts)


def _rank(e01, block):
    n_sub, _, t = e01.shape
    r = np.arange(t)
    tri = jnp.asarray((r[:, None] < r[None, :]).astype(np.float32), BF16)
    l = np.arange(LANES)
    low = jnp.asarray((l[None, :] < l[:, None]).astype(np.float32), BF16)
    return pl.pallas_call(
        functools.partial(_rank_kernel, block=block),
        out_shape=(jax.ShapeDtypeStruct((n_sub, 2, t), I32), jax.ShapeDtypeStruct((1, LANES), I32)),
        compiler_params=pltpu.CompilerParams(vmem_limit_bytes=VMEM_LIMIT),
        name="rank",
    )(e01, tri, low)


def _expert_kernel(start_ref, nblk_ref, xs_hbm, wg_ref, wu_ref, wd_ref, ys_hbm,
                   xbuf, ybuf, wg_s, wu_s, wd_s, sem_in, sem_out, *, block):
    nbuf = EXPERT_BUFS
    e = pl.program_id(0)
    n = nblk_ref[e]
    base = start_ref[e]
    total = start_ref[N_EXPERTS - 1] + nblk_ref[N_EXPERTS - 1]

    def in_copy(g):
        rows = pl.ds(pl.multiple_of(g * block, block), block)
        return pltpu.make_async_copy(xs_hbm.at[rows], xbuf.at[g % nbuf], sem_in.at[g % nbuf])

    def out_copy(g):
        rows = pl.ds(pl.multiple_of(g * block, block), block)
        return pltpu.make_async_copy(ybuf.at[g % nbuf], ys_hbm.at[rows], sem_out.at[g % nbuf])

    @pl.when(e == 0)
    def _():
        for g0 in range(nbuf - 1):
            @pl.when(g0 < total)
            def _(g0=g0):
                in_copy(g0).start()

    @pl.when(n > 0)
    def _():
        wg_s[...] = wg_ref[0].astype(BF16)
        wu_s[...] = wu_ref[0].astype(BF16)
        wd_s[...] = wd_ref[0].astype(BF16)

        def body(i, carry):
            g = base + i
            slot = g % nbuf
            in_copy(g).wait()

            @pl.when(g + nbuf - 1 < total)
            def _():
                in_copy(g + nbuf - 1).start()

            @pl.when(g >= nbuf)
            def _():
                out_copy(g - nbuf).wait()

            a, c = _unpack_pairs(xbuf[slot])
            x = jnp.concatenate([a.astype(BF16), c.astype(BF16)], axis=1)
            gate = jnp.dot(x, wg_s[...], preferred_element_type=F32)
            up = jnp.dot(x, wu_s[...], preferred_element_type=F32)
            hmid = (gate * _sigmoid(gate) * up).astype(BF16)
            ybuf[slot] = _pack_pairs(jnp.dot(hmid, wd_s[...], preferred_element_type=F32))
            out_copy(g).start()
            return carry

        lax.fori_loop(0, n, body, 0)

    @pl.when(e == N_EXPERTS - 1)
    def _():
        for back in range(nbuf, 0, -1):
            @pl.when(total >= back)
            def _(back=back):
                out_copy(total - back).wait()


def _experts(xs, start_blk, nblk, w_gate, w_up, w_down, block):
    n_rows = xs.shape[0]
    wblk = lambda e, st, nb: (e, 0, 0)
    grid_spec = pltpu.PrefetchScalarGridSpec(
        num_scalar_prefetch=2,
        grid=(N_EXPERTS,),
        in_specs=[pl.BlockSpec(memory_space=pl.ANY),
                  pl.BlockSpec((1, D_MODEL, D_EXPERT), wblk),
                  pl.BlockSpec((1, D_MODEL, D_EXPERT), wblk),
                  pl.BlockSpec((1, D_EXPERT, D_MODEL), wblk)],
        out_specs=pl.BlockSpec(memory_space=pl.ANY),
        scratch_shapes=[pltpu.VMEM((EXPERT_BUFS, block, HALF), I32),
                        pltpu.VMEM((EXPERT_BUFS, block, HALF), I32),
                        pltpu.VMEM((D_MODEL, D_EXPERT), BF16),
                        pltpu.VMEM((D_MODEL, D_EXPERT), BF16),
                        pltpu.VMEM((D_EXPERT, D_MODEL), BF16),
                        pltpu.SemaphoreType.DMA((EXPERT_BUFS,)),
                        pltpu.SemaphoreType.DMA((EXPERT_BUFS,))])
    return pl.pallas_call(
        functools.partial(_expert_kernel, block=block),
        out_shape=jax.ShapeDtypeStruct((n_rows, HALF), I32),
        grid_spec=grid_spec,
        compiler_params=_cparams(("arbitrary",)),
        name="experts",
    )(start_blk, nblk, xs, w_gate, w_up, w_down)


def _final_kernel(x1_ref, y0_ref, y1_ref, ew_ref, mod_ref, o_ref):
    a0, b0 = _unpack_pairs(y0_ref[...])
    a1, b1 = _unpack_pairs(y1_ref[...])
    w0 = ew_ref[:, 0:1]
    w1 = ew_ref[:, 1:2]
    moe = jnp.concatenate([w0 * a0 + w1 * a1, w0 * b0 + w1 * b1], axis=1)
    o_ref[0] = x1_ref[0] + mod_ref[0][5:6] * moe


def _final(x1, y0, y1, ew, mod, tm):
    b, t, d = x1.shape
    nt = t // tm
    flat = lambda i, j: (i * nt + j, 0)
    return pl.pallas_call(
        _final_kernel,
        out_shape=jax.ShapeDtypeStruct((b, t, d), F32),
        grid=(b, nt),
        in_specs=[pl.BlockSpec((1, tm, d), lambda i, j: (i, j, 0)),
                  pl.BlockSpec((tm, HALF), flat),
                  pl.BlockSpec((tm, HALF), flat),
                  pl.BlockSpec((tm, 8), flat),
                  pl.BlockSpec((1, 6, d), lambda i, j: (i, 0, 0))],
        out_specs=pl.BlockSpec((1, tm, d), lambda i, j: (i, j, 0)),
        compiler_params=_cparams(("arbitrary", "arbitrary")),
        name="final",
    )(x1, y0, y1, ew, mod)


def _sc_window(rows_per_worker):
    for w in range(SC_MAX_WINDOW, 7, -8):
        if rows_per_worker % w == 0:
            return w
    raise ValueError(f"no SparseCore window divides {rows_per_worker} rows per worker")


def _sc_split(idx):
    n = idx.shape[0]
    per = n // SC_WORKERS
    assert per * SC_WORKERS == n
    win = _sc_window(per)
    return idx.reshape(SC_WORKERS, per // win, win), per // win, win


def _sc_worker_id():
    return lax.axis_index("s") * SC_CORES + lax.axis_index("c")


def _dispatch_rows(h2_groups, dest_groups, n_rows):
    splits = [(_sc_split(d0), _sc_split(d1)) for d0, d1 in dest_groups]
    ng = len(h2_groups)
    scratch = []
    for (_, _, win), _ in splits:
        scratch += [pltpu.VMEM((win,), I32), pltpu.VMEM((win,), I32), pltpu.VMEM((win, HALF), I32)]

    @functools.partial(
        pl.kernel,
        mesh=plsc.VectorSubcoreMesh(core_axis_name="c", subcore_axis_name="s"),
        out_type=jax.ShapeDtypeStruct((n_rows, HALF), I32),
        scratch_types=scratch,
        name="sc_dispatch",
    )
    def k(*refs):
        x_refs, idx_refs, o_hbm, bufs = refs[:ng], refs[ng:3 * ng], refs[3 * ng], refs[3 * ng + 1:]
        wid = _sc_worker_id()
        for g in range(ng):
            (_, nwin, win), _ = splits[g]
            x_hbm, d0_hbm, d1_hbm = x_refs[g], idx_refs[2 * g], idx_refs[2 * g + 1]
            i0_v, i1_v, rows_v = bufs[3 * g:3 * g + 3]

            @pl.loop(0, nwin)
            def _(j, nwin=nwin, win=win, x_hbm=x_hbm, d0_hbm=d0_hbm, d1_hbm=d1_hbm,
                  i0_v=i0_v, i1_v=i1_v, rows_v=rows_v):
                base = pl.multiple_of((wid * nwin + j) * win, 8)
                pltpu.sync_copy(d0_hbm.at[wid, j], i0_v)
                pltpu.sync_copy(d1_hbm.at[wid, j], i1_v)
                pltpu.sync_copy(x_hbm.at[pl.ds(base, win)], rows_v)
                pltpu.sync_copy(rows_v, o_hbm.at[i0_v])
                pltpu.sync_copy(rows_v, o_hbm.at[i1_v])

    idx_args = []
    for (s0, s1) in splits:
        idx_args += [s0[0], s1[0]]
    return k(*h2_groups, *idx_args)


def _collect_rows(ys, dest_groups):
    splits = [(_sc_split(d0), _sc_split(d1)) for d0, d1 in dest_groups]
    ng = len(dest_groups)
    outs, scratch = [], []
    for (d0, _), ((_, _, win), _) in zip(dest_groups, splits):
        o = jax.ShapeDtypeStruct((d0.shape[0], HALF), I32)
        outs += [o, o]
        scratch += [pltpu.VMEM((win,), I32), pltpu.VMEM((win, HALF), I32)]

    @functools.partial(
        pl.kernel,
        mesh=plsc.VectorSubcoreMesh(core_axis_name="c", subcore_axis_name="s"),
        out_type=tuple(outs),
        scratch_types=scratch,
        name="sc_collect",
    )
    def k(*refs):
        ys_hbm, idx_refs, out_refs, bufs = refs[0], refs[1:1 + 2 * ng], refs[1 + 2 * ng:1 + 4 * ng], refs[1 + 4 * ng:]
        wid = _sc_worker_id()
        for g in range(ng):
            (_, nwin, win), _ = splits[g]
            i_v, rows_v = bufs[2 * g:2 * g + 2]
            for kk in range(2):
                d_hbm, y_hbm = idx_refs[2 * g + kk], out_refs[2 * g + kk]

                @pl.loop(0, nwin)
                def _(j, nwin=nwin, win=win, d_hbm=d_hbm, y_hbm=y_hbm, i_v=i_v, rows_v=rows_v):
                    base = pl.multiple_of((wid * nwin + j) * win, 8)
                    pltpu.sync_copy(d_hbm.at[wid, j], i_v)
                    pltpu.sync_copy(ys_hbm.at[i_v], rows_v)
                    pltpu.sync_copy(rows_v, y_hbm.at[pl.ds(base, win)])

    idx_args = []
    for (s0, s1) in splits:
        idx_args += [s0[0], s1[0]]
    res = k(ys, *idx_args)
    return [(res[2 * g], res[2 * g + 1]) for g in range(ng)]


def _t5_bucket(rel):
    half = N_BUCKETS // 2
    max_exact = half // 2
    n = jnp.abs(rel)
    far = max_exact + (jnp.log(jnp.maximum(n, 1).astype(F32) / max_exact)
                       / math.log(MAX_DISTANCE / max_exact) * (half - max_exact)).astype(I32)
    far = jnp.minimum(far, half - 1)
    return jnp.where(rel > 0, half, 0) + jnp.where(n < max_exact, n, far)


def _bias_table(rel_bias, cq, nq, no_history):
    nk = WINDOW + cq
    j = jnp.arange(KEY_WIN)[:, None]
    c = jnp.arange(UNIT_Q)[None, :]
    jj = j - (c // cq) * cq
    valid = (jj >= 0) & (jj < nk) & (c < nq)
    if no_history:
        valid = valid & (j >= WINDOW)
    rel = jj - WINDOW - (c % cq)
    onehot = (_t5_bucket(rel)[:, :, None] == jnp.arange(N_BUCKETS)).astype(F32)
    bias = jnp.einsum("jcb,bh->jch", onehot, rel_bias.astype(F32), precision=lax.Precision.HIGHEST)
    bias = jnp.where(valid[:, :, None], bias, -jnp.inf)
    bias = jnp.transpose(bias.reshape(KEY_WIN, UNIT_Q, N_KV, GROUP), (2, 0, 3, 1))
    return bias.reshape(N_KV, KEY_WIN, GROUP * UNIT_Q)


def _sink_table(sinks):
    s = sinks.astype(F32).reshape(N_KV, 1, GROUP, 1)
    return jnp.broadcast_to(s, (N_KV, 1, GROUP, UNIT_Q)).reshape(N_KV, 1, GROUP * UNIT_Q)


def kernel(x_prompt, x_sample, state_attn_k, state_attn_v, state_conv, c_prompt, c_sample,
           rel_bias, w_ada, b_ada, norm1_w, w_in, q_norm_w, k_norm_w, attn_sinks, conv_w,
           w_out, norm2_w, w_router_group, b_router_group, w_router_expert, b_router_expert,
           w_gate, w_up, w_down):
    depth = w_ada.shape[0]
    assert depth == 1
    bp, tp, d = x_prompt.shape
    bs, ts, _ = x_sample.shape
    n_p, n_s = bp * tp, bs * ts
    n_tok = n_p + n_s
    l = 0

    wi = w_in[l]
    qw, kw, vw, rest = wi[:, :d], wi[:, d:d + KV_COLS], wi[:, d + KV_COLS:d + 2 * KV_COLS], wi[:, d + 2 * KV_COLS:]
    w_in_b = jnp.concatenate([qw, rest, kw, vw], axis=1).astype(BF16)
    w_out_b = w_out[l].astype(BF16)
    w_r = jnp.concatenate([w_router_group[l],
                           jnp.transpose(w_router_expert[l], (1, 0, 2)).reshape(d, N_EXPERTS),
                           jnp.zeros((d, LANES - N_GROUPS - N_EXPERTS), F32)], axis=1)
    w_r_hi = lax.reduce_precision(w_r, exponent_bits=8, mantissa_bits=7)
    w_r = jnp.concatenate([w_r_hi.astype(BF16), (w_r - w_r_hi).astype(BF16)], axis=1)
    b_r = jnp.concatenate([b_router_group[l], b_router_expert[l].reshape(-1),
                           jnp.zeros((LANES - N_GROUPS - N_EXPERTS,), F32)]).reshape(1, LANES)
    knw = jnp.tile(k_norm_w[l], N_KV).reshape(1, KV_COLS)
    qsc = jnp.tile(q_norm_w[l] * (HEAD_DIM ** -0.5), N_KV).reshape(1, KV_COLS)
    n1w = norm1_w[l].reshape(1, d)
    n2w = norm2_w[l].reshape(1, d)

    mod = _ada(jnp.concatenate([c_prompt, c_sample], axis=0), w_ada[l], b_ada[l]).reshape(bp + bs, 6, d)
    mod_p, mod_s = mod[:bp], mod[bp:]

    xs_rows = x_sample.reshape(1, n_s, d)
    mod4_p = mod_p[:, :, None, :]
    mod4_s = jnp.repeat(jnp.transpose(mod_s, (1, 0, 2)), ts, axis=1)[None]
    proj_s = _inproj(xs_rows, mod4_s, n1w, w_in_b, n_s).reshape(bs, ts, IN_COLS)
    sink_tab = _sink_table(attn_sinks[l])
    bias_p = jnp.concatenate([_bias_table(rel_bias, CHUNK, UNIT_Q, False),
                              _bias_table(rel_bias, CHUNK, UNIT_Q, True)], axis=0)
    merged_p, k_p, v_p, c_p = _front(x_prompt, mod_p, n1w, w_in_b, knw, qsc, bias_p, sink_tab, conv_w[l],
                                     MIX_TILE, UNIT_Q)
    state = (state_attn_k[l].reshape(bs, WINDOW, KV_COLS), state_attn_v[l].reshape(bs, WINDOW, KV_COLS),
             state_conv[l])
    merged_s, k_s, v_s, c_s = _mixer(proj_s, knw, qsc, _bias_table(rel_bias, ts, ts, False), sink_tab, conv_w[l],
                                     state, ts, ts)

    x1_p, h2_p, e01_p, ew_p = _outproj(merged_p, x_prompt, mod4_p, w_out_b, n2w, w_r, b_r, ROW_TILE)
    x1_s, h2_s, e01_s, ew_s = _outproj(merged_s.reshape(1, n_s, d), xs_rows, mod4_s, w_out_b, n2w, w_r, b_r, n_s)
    x1_s = x1_s.reshape(bs, ts, d)

    assert n_s == ROW_TILE
    d01, totals = _rank(jnp.concatenate([e01_p, e01_s], axis=0), EXPERT_BLOCK)
    n_sub_p = n_p // ROW_TILE
    dests = [(d01[:n_sub_p, 0].reshape(-1), d01[:n_sub_p, 1].reshape(-1)),
             (d01[n_sub_p:, 0].reshape(-1), d01[n_sub_p:, 1].reshape(-1))]
    nblk = (totals[0, :N_EXPERTS] + EXPERT_BLOCK - 1) // EXPERT_BLOCK
    start_blk = (jnp.cumsum(nblk) - nblk).astype(I32)
    nb_max = -(-2 * n_tok // EXPERT_BLOCK) + N_EXPERTS

    xs = _dispatch_rows([h2_p, h2_s], dests, nb_max * EXPERT_BLOCK)
    ys = _experts(xs, start_blk, nblk.astype(I32), w_gate[l], w_up[l], w_down[l], EXPERT_BLOCK)
    (y0_p, y1_p), (y0_s, y1_s) = _collect_rows(ys, dests)

    y_p = _final(x1_p, y0_p, y1_p, ew_p, mod_p, ROW_TILE)
    y_s = _final(x1_s, y0_s, y1_s, ew_s, mod_s, ts)

    kv_shape = (1, -1, WINDOW, N_KV, HEAD_DIM)
    return (y_p, y_s, k_p.reshape(kv_shape), v_p.reshape(kv_shape), c_p[None],
            k_s.reshape(kv_shape), v_s.reshape(kv_shape), c_s[None])
```

```python
import functools
import math

import numpy as np
import jax
import jax.numpy as jnp
from jax import lax
from jax.experimental import pallas as pl
from jax.experimental.pallas import tpu as pltpu
from jax.experimental.pallas import tpu_sc as plsc

F32 = jnp.float32
BF16 = jnp.bfloat16
I32 = jnp.int32

D_MODEL = 1024
HEAD_DIM = 64
N_HEADS = 16
N_KV = 4
GROUP = 4
CHUNK = 64
WINDOW = 128
N_BUCKETS = 32
MAX_DISTANCE = 128
N_GROUPS = 8
EPG = 8
N_EXPERTS = 64
D_EXPERT = 512
RMS_EPS = 1e-6
LOG2E = math.log2(math.e)
SUM_ROWS = 16
KV_COLS = N_KV * HEAD_DIM
IN_COLS = 6 * D_MODEL + 2 * KV_COLS
HALF = D_MODEL // 2
LANES = 128

VMEM_LIMIT = 56 * 1024 * 1024
INPROJ_TN = 512
ROW_TILE = 512
MIX_TILE = 512
UNIT_Q = 2 * CHUNK
KEY_WIN = WINDOW + UNIT_Q
PROJ_CHUNK = 256
EXPERT_BLOCK = 256
EXPERT_BUFS = 4
SC_CORES = 2
SC_SUBCORES = 16
SC_WORKERS = SC_CORES * SC_SUBCORES
SC_MAX_WINDOW = 128


def _cparams(sem):
    return pltpu.CompilerParams(dimension_semantics=sem, vmem_limit_bytes=VMEM_LIMIT)


def _split_bf16(a):
    hi = a.astype(BF16)
    lo = (a - hi.astype(F32)).astype(BF16)
    return hi, lo


def _dot3(a, b):
    ah, al = _split_bf16(a)
    bh, bl = _split_bf16(b)
    d = functools.partial(jnp.dot, preferred_element_type=F32)
    return d(ah, bh) + (d(ah, bl) + d(al, bh))


def _sigmoid(x):
    return 0.5 * jnp.tanh(0.5 * x) + 0.5


def _pack_pairs(y):
    a = lax.bitcast_convert_type(y[:, :HALF].astype(BF16).astype(F32), I32)
    b = lax.bitcast_convert_type(y[:, HALF:].astype(BF16).astype(F32), I32)
    return a | lax.shift_right_logical(b, jnp.int32(16))


def _unpack_pairs(w):
    a = lax.bitcast_convert_type(w & jnp.int32(-65536), F32)
    b = lax.bitcast_convert_type(lax.shift_left(w, jnp.int32(16)), F32)
    return a, b


def _ada_kernel(c_ref, w_ref, b_ref, o_ref):
    c = c_ref[...]
    s = c * jax.nn.sigmoid(c)
    o_ref[...] = _dot3(s, w_ref[...]) + b_ref[...]


def _ada(c_all, w_ada, b_ada):
    r, d = c_all.shape
    n = w_ada.shape[1]
    tn = 1024
    return pl.pallas_call(
        _ada_kernel,
        out_shape=jax.ShapeDtypeStruct((r, n), F32),
        grid=(n // tn,),
        in_specs=[pl.BlockSpec((r, d), lambda j: (0, 0)),
                  pl.BlockSpec((d, tn), lambda j: (0, j)),
                  pl.BlockSpec((1, tn), lambda j: (0, j))],
        out_specs=pl.BlockSpec((r, tn), lambda j: (0, j)),
        compiler_params=_cparams(("arbitrary",)),
        name="ada",
    )(c_all, w_ada, b_ada.reshape(1, n))


def _inproj_kernel(x_ref, mod_ref, nw_ref, w_ref, o_ref):
    x = x_ref[0]
    mod = mod_ref[0]
    h = x * lax.rsqrt(jnp.mean(x * x, axis=-1, keepdims=True) + RMS_EPS) * nw_ref[...]
    h = h * (1.0 + mod[1]) + mod[0]
    hb = h.astype(BF16)
    for j in range(IN_COLS // INPROJ_TN):
        sl = slice(j * INPROJ_TN, (j + 1) * INPROJ_TN)
        o_ref[0, :, sl] = jnp.dot(hb, w_ref[:, sl], preferred_element_type=F32).astype(BF16)


def _inproj(x, mod4, nw, w_in_b, tm):
    b, t, d = x.shape
    mr = mod4.shape[2]
    assert mr == 1 or (mr == t and tm == t)
    return pl.pallas_call(
        _inproj_kernel,
        out_shape=jax.ShapeDtypeStruct((b, t, IN_COLS), BF16),
        grid=(b, t // tm),
        in_specs=[pl.BlockSpec((1, tm, d), lambda i, j: (i, j, 0)),
                  pl.BlockSpec((1, 6, mr, d), lambda i, j: (i, 0, 0, 0)),
                  pl.BlockSpec((1, d), lambda i, j: (0, 0)),
                  pl.BlockSpec((d, IN_COLS), lambda i, j: (0, 0), pipeline_mode=pl.Buffered(1))],
        out_specs=pl.BlockSpec((1, tm, IN_COLS), lambda i, j: (i, j, 0)),
        compiler_params=_cparams(("arbitrary", "arbitrary")),
        name="inproj",
    )(x, mod4, nw, w_in_b)


def _head_inv_rms(xf, bd, two_pass=True):
    sq = xf * xf
    if two_pass:
        hi, lo = _split_bf16(sq)
        ssq = jnp.dot(hi, bd, preferred_element_type=F32) + jnp.dot(lo, bd, preferred_element_type=F32)
    else:
        ssq = jnp.dot(sq.astype(BF16), bd, preferred_element_type=F32)
    return lax.rsqrt(ssq * (1.0 / HEAD_DIM) + RMS_EPS)


def _mixer_body(q_ref, k_ref, v_ref, bg_ref, c_ref, u_ref, ga_ref, gc_ref,
                knw_ref, qsc_ref, bd_ref, bias_ref, sink_ref, cw_ref,
                kpast_ref, vpast_ref, cpast_ref, upast_ref,
                merged_ref, knew_ref, vnew_ref, cnew_ref,
                kq_buf, vt_buf, attn_buf, *, tq, nq, stateful):
    hb = WINDOW
    pw = UNIT_Q
    t = pl.program_id(1)
    bd = bd_ref[...]

    k = k_ref[0].astype(F32)
    kn = k * _head_inv_rms(k, bd) * knw_ref[...]
    kq = (kn * qsc_ref[...]).astype(BF16)
    vb = v_ref[0]
    vt = vb.astype(F32).T.astype(BF16)

    if stateful:
        kp = kpast_ref[0]
        vp = vpast_ref[0]
        for kv in range(N_KV):
            kq_buf[kv, hb + tq:] = jnp.zeros((KEY_WIN - hb - tq, HEAD_DIM), BF16)
        vt_buf[:, hb + tq:] = jnp.zeros((KV_COLS, KEY_WIN - hb - tq), BF16)
        u_hist = jnp.concatenate([jnp.zeros((6, D_MODEL), F32), cpast_ref[0]], axis=0)
        knew_ref[0] = jnp.concatenate([kp[tq:], kn], axis=0)
        vnew_ref[0] = jnp.concatenate([vp[tq:], vb.astype(F32)], axis=0)
    else:
        kraw = kpast_ref[0].astype(F32)
        kp = kraw * _head_inv_rms(kraw, bd) * knw_ref[...]
        vp = vpast_ref[0].astype(F32)
        u_hist = jnp.where(t == 0, 0.0, cpast_ref[0].astype(F32) * upast_ref[0].astype(F32))
        knew_ref[0] = kn[tq - hb:]
        vnew_ref[0] = vb[tq - hb:].astype(F32)
    _fill_keys(kq_buf, vt_buf, kp, vp, kq, vt, qsc_ref, tq)
    q = q_ref[0]
    _attention_units(lambda kv: q[:, kv * KV_COLS:(kv + 1) * KV_COLS].astype(F32), kq_buf, vt_buf,
                     bias_ref, sink_ref, bd, attn_buf, tq, nq, None if stateful else t)
    cu = c_ref[0].astype(F32) * u_ref[0].astype(F32)
    conv, u_all = _short_conv(u_hist, cu, cw_ref[...], tq)
    cnew_ref[0] = u_all[tq + 6:tq + 8]
    merged = (_sigmoid(ga_ref[0].astype(F32)) * attn_buf[...]
              + _sigmoid(gc_ref[0].astype(F32)) * (bg_ref[0].astype(F32) * conv))
    merged_ref[0] = merged.astype(BF16)


def _fill_keys(kq_buf, vt_buf, kp, vp, kq, vt, qsc_ref, tq):
    hb = WINDOW
    kqp = (kp * qsc_ref[...]).astype(BF16)
    for kv in range(N_KV):
        kq_buf[kv, 0:hb] = kqp[:, kv * HEAD_DIM:(kv + 1) * HEAD_DIM]
        kq_buf[kv, hb:hb + tq] = kq[:, kv * HEAD_DIM:(kv + 1) * HEAD_DIM]
    vt_buf[:, 0:hb] = vp.T.astype(BF16)
    vt_buf[:, hb:hb + tq] = vt


def _attention_units(q_group, kq_buf, vt_buf, bias_ref, sink_ref, bd, attn_buf, tq, nq, t_first, after_unit=None):
    pw = UNIT_Q
    ones_rows = jnp.ones((SUM_ROWS, KEY_WIN), BF16)
    for kv in range(N_KV):
        qf = q_group(kv)
        qn = (qf * _head_inv_rms(qf, bd, two_pass=False)).astype(BF16)
        for u in range(tq // nq):
            r0 = u * nq
            parts = [qn[r0:r0 + nq, g * HEAD_DIM:(g + 1) * HEAD_DIM] for g in range(GROUP)]
            if nq < pw:
                zpad = jnp.zeros((pw - nq, HEAD_DIM), BF16)
                parts = [x for p_ in parts for x in (p_, zpad)]
            qs = jnp.concatenate(parts, axis=0)
            kw = kq_buf[kv, r0:r0 + KEY_WIN]
            st = lax.dot_general(kw, qs, (((1,), (1,)), ((), ())), preferred_element_type=F32)
            if t_first is not None and u == 0:
                bias = jnp.where(t_first == 0, bias_ref[kv + N_KV], bias_ref[kv])
            else:
                bias = bias_ref[kv]
            st = st + bias
            sink = sink_ref[kv]
            m = jnp.maximum(jnp.max(st, axis=0, keepdims=True), sink)
            p = jnp.exp2((st - m).astype(BF16))
            vt1 = jnp.concatenate([vt_buf[kv * HEAD_DIM:(kv + 1) * HEAD_DIM, r0:r0 + KEY_WIN], ones_rows], axis=0)
            pv = jnp.dot(vt1, p, preferred_element_type=F32)
            den = pv[HEAD_DIM:HEAD_DIM + 1] + jnp.exp2(sink - m)
            ot = pv[:HEAD_DIM] / den
            for gp in range(GROUP // 2):
                blk = jnp.concatenate([ot[:, (2 * gp) * pw:(2 * gp + 1) * pw],
                                       ot[:, (2 * gp + 1) * pw:(2 * gp + 2) * pw]], axis=0)
                c0 = (kv * GROUP + 2 * gp) * HEAD_DIM
                attn_buf[r0:r0 + nq, c0:c0 + 2 * HEAD_DIM] = blk.T[:nq]
            if after_unit is not None:
                after_unit(kv * (tq // nq) + u)


def _short_conv(u_hist, cu, cw, tq):
    u_all = jnp.concatenate([u_hist, cu], axis=0)
    conv = cw[0:1] * u_all[6:6 + tq] + cw[1:2] * u_all[7:7 + tq] + cw[2:3] * u_all[8:8 + tq]
    return conv, u_all


def _front_body(x_ref, xp_ref, mod_ref, nw_ref, w_ref, knw_ref, qsc_ref, bd_ref, bias_ref, sink_ref, cw_ref,
                merged_ref, knew_ref, vnew_ref, cnew_ref,
                kq_buf, vt_buf, u_buf, attn_buf, pbuf, *, tq, nq):
    hb = WINDOW
    d = D_MODEL
    t = pl.program_id(1)
    bd = bd_ref[...]
    mod = mod_ref[0]

    @pl.when(t == 0)
    def _():
        u_buf[0] = jnp.zeros((8, d), F32)

    def modnorm(x):
        h = x * lax.rsqrt(jnp.mean(x * x, axis=-1, keepdims=True) + RMS_EPS) * nw_ref[...]
        return (h * (1.0 + mod[1:2]) + mod[0:1]).astype(BF16)

    def proj(hrows, c0, width):
        return jnp.dot(hrows, w_ref[:, c0:c0 + width], preferred_element_type=F32)

    hbf = modnorm(x_ref[0])
    kv_t = proj(hbf, 6 * d, 2 * KV_COLS)
    kv_p = proj(modnorm(xp_ref[0]), 6 * d, 2 * KV_COLS)
    k, v = kv_t[:, :KV_COLS], kv_t[:, KV_COLS:]
    kn = k * _head_inv_rms(k, bd) * knw_ref[...]
    kq = (kn * qsc_ref[...]).astype(BF16)
    kraw = kv_p[:, :KV_COLS]
    kp = kraw * _head_inv_rms(kraw, bd) * knw_ref[...]
    knew_ref[0] = kn[tq - hb:]
    vnew_ref[0] = v[tq - hb:]
    _fill_keys(kq_buf, vt_buf, kp, kv_p[:, KV_COLS:], kq, v.T.astype(BF16), qsc_ref, tq)

    n_units = N_KV * (tq // nq)
    q_chunks = [[g * KV_COLS + c for c in range(0, KV_COLS, PROJ_CHUNK)] for g in range(N_KV)]
    rest_chunks = list(range(d, 6 * d, PROJ_CHUNK))

    def issue(c0):
        pbuf[:, c0:c0 + PROJ_CHUNK] = proj(hbf, c0, PROJ_CHUNK)

    for c0 in q_chunks[0]:
        issue(c0)
    n_rest = len(rest_chunks)

    def after_unit(i):
        kv, u = divmod(i, tq // nq)
        if u == 0 and kv + 1 < N_KV:
            for c0 in q_chunks[kv + 1]:
                issue(c0)
        for c0 in rest_chunks[i * n_rest // n_units:(i + 1) * n_rest // n_units]:
            issue(c0)

    _attention_units(lambda kv: pbuf[:, kv * KV_COLS:(kv + 1) * KV_COLS], kq_buf, vt_buf,
                     bias_ref, sink_ref, bd, attn_buf, tq, nq, t, after_unit)

    cu = pbuf[:, 2 * d:3 * d] * pbuf[:, 3 * d:4 * d]
    conv, u_all = _short_conv(u_buf[t % 2], cu, cw_ref[...], tq)
    cnew_ref[0] = u_all[tq + 6:tq + 8]
    u_buf[(t + 1) % 2] = u_all[tq:tq + 8]
    merged = (_sigmoid(pbuf[:, 4 * d:5 * d]) * attn_buf[...]
              + _sigmoid(pbuf[:, 5 * d:6 * d]) * (pbuf[:, d:2 * d] * conv))
    merged_ref[0] = merged.astype(BF16)


def _front(x, mod, nw, w_in_b, knw, qsc, bias_tab, sink_tab, conv_w, tq, nq):
    b, t, d = x.shape
    r = np.arange(KV_COLS) // HEAD_DIM
    bd = jnp.asarray((r[:, None] == r[None, :]).astype(np.float32), BF16)
    const2 = lambda shp: pl.BlockSpec(shp, lambda i, s: (0, 0))
    const3 = lambda shp: pl.BlockSpec(shp, lambda i, s: (0, 0, 0))
    per_b = lambda shp: pl.BlockSpec(shp, lambda i, s: (i, 0, 0))
    kw_ = tq // WINDOW
    in_specs = [pl.BlockSpec((1, tq, d), lambda i, s: (i, s, 0)),
                pl.BlockSpec((1, WINDOW, d), lambda i, s: (i, jnp.maximum(s * kw_ - 1, 0), 0)),
                per_b((1, 6, d)), const2((1, d)),
                pl.BlockSpec((d, IN_COLS), lambda i, s: (0, 0), pipeline_mode=pl.Buffered(1)),
                const2((1, KV_COLS)), const2((1, KV_COLS)), const2((KV_COLS, KV_COLS)),
                const3(bias_tab.shape), const3(sink_tab.shape), const2((3, d))]
    out_shape = (jax.ShapeDtypeStruct((b, t, d), BF16),
                 jax.ShapeDtypeStruct((b, WINDOW, KV_COLS), F32),
                 jax.ShapeDtypeStruct((b, WINDOW, KV_COLS), F32),
                 jax.ShapeDtypeStruct((b, 2, d), F32))
    out_specs = (pl.BlockSpec((1, tq, d), lambda i, s: (i, s, 0)),
                 per_b((1, WINDOW, KV_COLS)), per_b((1, WINDOW, KV_COLS)), per_b((1, 2, d)))
    return pl.pallas_call(
        functools.partial(_front_body, tq=tq, nq=nq),
        out_shape=out_shape,
        grid=(b, t // tq),
        in_specs=in_specs,
        out_specs=out_specs,
        scratch_shapes=[pltpu.VMEM((N_KV, WINDOW + tq, HEAD_DIM), BF16),
                        pltpu.VMEM((KV_COLS, WINDOW + tq), BF16),
                        pltpu.VMEM((2, 8, d), F32),
                        pltpu.VMEM((tq, d), F32),
                        pltpu.VMEM((tq, 6 * d), F32)],
        compiler_params=_cparams(("arbitrary", "arbitrary")),
        name="front",
    )(x, x, mod, nw, w_in_b, knw, qsc, bd, bias_tab, sink_tab, conv_w)


def _mixer(proj, knw, qsc, bias_tab, sink_tab, conv_w, state, tq, nq):
    b, t, _ = proj.shape
    d = D_MODEL
    stateful = state is not None
    key_rows = max(WINDOW + tq, KEY_WIN)
    r = np.arange(KV_COLS) // HEAD_DIM
    bd = jnp.asarray((r[:, None] == r[None, :]).astype(np.float32), BF16)
    wide = lambda j: pl.BlockSpec((1, tq, d), lambda i, s, j=j: (i, s, j))
    kvspec = lambda j: pl.BlockSpec((1, tq, KV_COLS), lambda i, s, j=j: (i, s, j))
    const2 = lambda shp: pl.BlockSpec(shp, lambda i, s: (0, 0))
    const3 = lambda shp: pl.BlockSpec(shp, lambda i, s: (0, 0, 0))
    per_b = lambda shp: pl.BlockSpec(shp, lambda i, s: (i, 0, 0))
    kvblk = 6 * d // KV_COLS
    if stateful:
        hist_specs = [per_b((1, WINDOW, KV_COLS)), per_b((1, WINDOW, KV_COLS)), per_b((1, 2, d)), per_b((1, 2, d))]
        hist_args = [state[0], state[1], state[2], state[2]]
    else:
        kw_ = tq // WINDOW
        prev_kv = lambda j: pl.BlockSpec((1, WINDOW, KV_COLS),
                                         lambda i, s, j=j: (i, jnp.maximum(s * kw_ - 1, 0), j))
        prev8 = lambda j: pl.BlockSpec((1, 8, d), lambda i, s, j=j: (i, jnp.maximum(s * (tq // 8) - 1, 0), j))
        hist_specs = [prev_kv(kvblk), prev_kv(kvblk + 1), prev8(2), prev8(3)]
        hist_args = [proj, proj, proj, proj]
    in_specs = [wide(0), kvspec(kvblk), kvspec(kvblk + 1), wide(1), wide(2), wide(3), wide(4), wide(5),
                const2((1, KV_COLS)), const2((1, KV_COLS)), const2((KV_COLS, KV_COLS)),
                const3(bias_tab.shape), const3(sink_tab.shape), const2((3, d))] + hist_specs
    out_shape = (jax.ShapeDtypeStruct((b, t, d), BF16),
                 jax.ShapeDtypeStruct((b, WINDOW, KV_COLS), F32),
                 jax.ShapeDtypeStruct((b, WINDOW, KV_COLS), F32),
                 jax.ShapeDtypeStruct((b, 2, d), F32))
    out_specs = (pl.BlockSpec((1, tq, d), lambda i, s: (i, s, 0)),
                 per_b((1, WINDOW, KV_COLS)), per_b((1, WINDOW, KV_COLS)), per_b((1, 2, d)))
    return pl.pallas_call(
        functools.partial(_mixer_body, tq=tq, nq=nq, stateful=stateful),
        out_shape=out_shape,
        grid=(b, t // tq),
        in_specs=in_specs,
        out_specs=out_specs,
        scratch_shapes=[pltpu.VMEM((N_KV, key_rows, HEAD_DIM), BF16),
                        pltpu.VMEM((KV_COLS, key_rows), BF16),
                        pltpu.VMEM((tq, d), F32)],
        compiler_params=_cparams(("arbitrary", "arbitrary")),
        name="mixer_state" if stateful else "mixer",
    )(proj, proj, proj, proj, proj, proj, proj, proj, knw, qsc, bd, bias_tab, sink_tab, conv_w, *hist_args)


def _route(logits):
    lane = lax.broadcasted_iota(I32, logits.shape, 1).astype(F32)
    neg = -jnp.inf
    big = float(1 << 20)
    gl = jnp.where(lane < N_GROUPS, logits, neg)
    gmax = jnp.max(gl, axis=-1, keepdims=True)
    g_idx = jnp.min(jnp.where(gl == gmax, lane, big), axis=-1, keepdims=True)
    g_w = 1.0 / jnp.sum(jnp.exp(gl - gmax), axis=-1, keepdims=True)
    lo = N_GROUPS + g_idx * EPG
    el = jnp.where((lane >= lo) & (lane < lo + EPG), logits, neg)
    m1 = jnp.max(el, axis=-1, keepdims=True)
    i1 = jnp.min(jnp.where(el == m1, lane, big), axis=-1, keepdims=True)
    el2 = jnp.where(lane == i1, neg, el)
    m2 = jnp.max(el2, axis=-1, keepdims=True)
    i2 = jnp.min(jnp.where(el2 == m2, lane, big), axis=-1, keepdims=True)
    r = jnp.exp(m2 - m1)
    w1 = 1.0 / (1.0 + r)
    w2 = r / (1.0 + r)
    return i1 - N_GROUPS, i2 - N_GROUPS, g_w * w1, g_w * w2


def _outproj_kernel(m_ref, x_ref, mod_ref, wo_ref, nw_ref, wr_ref, br_ref,
                    x1_ref, h2_ref, e01_ref, ew_ref):
    mod = mod_ref[0]
    mix = jnp.dot(m_ref[0], wo_ref[...], preferred_element_type=F32)
    x1 = x_ref[0] + mod[2] * mix
    x1_ref[0] = x1
    h = x1 * lax.rsqrt(jnp.mean(x1 * x1, axis=-1, keepdims=True) + RMS_EPS) * nw_ref[...]
    h = h * (1.0 + mod[4]) + mod[3]
    h2_ref[...] = _pack_pairs(h)
    tm = h.shape[0]
    h_hi, h_lo = _split_bf16(h)
    prod = jnp.dot(jnp.concatenate([h_hi, h_lo], axis=0), wr_ref[...], preferred_element_type=F32)
    logits = prod[:tm, :LANES] + (prod[:tm, LANES:] + prod[tm:, :LANES]) + br_ref[...]
    e1, e2, w1, w2 = _route(logits)
    e01_ref[0] = jnp.concatenate([_col_to_row(e1), _col_to_row(e2)], axis=0).astype(I32)
    lane8 = lax.broadcasted_iota(I32, (h.shape[0], 8), 1)
    ew_ref[...] = jnp.where(lane8 == 0, w1, jnp.where(lane8 == 1, w2, 0.0))


def _outproj(merged, x, mod4, w_out_b, nw, w_r, b_r, tm):
    b, t, d = x.shape
    nt = t // tm
    mr = mod4.shape[2]
    assert mr == 1 or (mr == t and nt == 1)
    flat = lambda i, j: (i * nt + j, 0)
    return pl.pallas_call(
        _outproj_kernel,
        out_shape=(jax.ShapeDtypeStruct((b, t, d), F32),
                   jax.ShapeDtypeStruct((b * t, HALF), I32),
                   jax.ShapeDtypeStruct((b * nt, 2, tm), I32),
                   jax.ShapeDtypeStruct((b * t, 8), F32)),
        grid=(b, nt),
        in_specs=[pl.BlockSpec((1, tm, d), lambda i, j: (i, j, 0)),
                  pl.BlockSpec((1, tm, d), lambda i, j: (i, j, 0)),
                  pl.BlockSpec((1, 6, mr, d), lambda i, j: (i, 0, 0, 0)),
                  pl.BlockSpec((d, d), lambda i, j: (0, 0)),
                  pl.BlockSpec((1, d), lambda i, j: (0, 0)),
                  pl.BlockSpec((d, 2 * LANES), lambda i, j: (0, 0)),
                  pl.BlockSpec((1, LANES), lambda i, j: (0, 0))],
        out_specs=(pl.BlockSpec((1, tm, d), lambda i, j: (i, j, 0)),
                   pl.BlockSpec((tm, HALF), flat),
                   pl.BlockSpec((1, 2, tm), lambda i, j: (i * nt + j, 0, 0)),
                   pl.BlockSpec((tm, 8), flat)),
        compiler_params=_cparams(("arbitrary", "arbitrary")),
        name="outproj",
    )(merged, x, mod4, w_out_b, nw, w_r, b_r)


def _col_to_row(col):
    eye = lax.broadcasted_iota(I32, (LANES, LANES), 0) == lax.broadcasted_iota(I32, (LANES, LANES), 1)
    parts = [jnp.sum(jnp.where(eye, col[r * LANES:(r + 1) * LANES], 0.0), axis=0, keepdims=True)
             for r in range(col.shape[0] // LANES)]
    return jnp.concatenate(parts, axis=1)


def _rank_kernel(e_ref, tri_ref, low_ref, d_ref, tot_ref, *, block):
    n_sub, _, t = e_ref.shape
    sub = lax.broadcasted_iota(I32, (LANES, t), 0)

    def hots(s):
        e = e_ref[s]
        return sub == e[0:1], sub == e[1:2]

    def count(s, cnt):
        h0, h1 = hots(s)
        return cnt + jnp.sum(jnp.where(h0 | h1, 1.0, 0.0), axis=1, keepdims=True)

    cnt = lax.fori_loop(0, n_sub, count, jnp.zeros((LANES, 1), F32))
    tot_ref[...] = _col_to_row(cnt).astype(I32)
    nblk = jnp.floor((cnt + (block - 1)) * (1.0 / block))
    hi = jnp.floor(nblk * (1.0 / 16.0))
    lo = nblk - hi * 16.0
    low = low_ref[...]
    bcast = lambda c: jnp.broadcast_to(c, (LANES, LANES)).astype(BF16)
    excl = (jnp.dot(low, bcast(hi), preferred_element_type=F32) * 16.0
            + jnp.dot(low, bcast(lo), preferred_element_type=F32))
    starts = excl[:, 0:1] * float(block)

    def place(s, running):
        h0, h1 = hots(s)
        onehot = jnp.where(h0 | h1, 1.0, 0.0)
        prefix = jnp.dot(onehot.astype(BF16), tri_ref[...], preferred_element_type=F32)
        pos = prefix + running
        d0 = jnp.sum(jnp.where(h0, pos, 0.0), axis=0, keepdims=True)
        d1 = jnp.sum(jnp.where(h1, pos, 0.0), axis=0, keepdims=True)
        d_ref[s] = jnp.concatenate([d0, d1], axis=0).astype(I32)
        return running + jnp.sum(onehot, axis=1, keepdims=True)

    lax.fori_loop(0, n_sub, place, starts)


def _rank(e01, block):
    n_sub, _, t = e01.shape
    r = np.arange(t)
    tri = jnp.asarray((r[:, None] < r[None, :]).astype(np.float32), BF16)
    l = np.arange(LANES)
    low = jnp.asarray((l[None, :] < l[:, None]).astype(np.float32), BF16)
    return pl.pallas_call(
        functools.partial(_rank_kernel, block=block),
        out_shape=(jax.ShapeDtypeStruct((n_sub, 2, t), I32), jax.ShapeDtypeStruct((1, LANES), I32)),
        compiler_params=pltpu.CompilerParams(vmem_limit_bytes=VMEM_LIMIT),
        name="rank",
    )(e01, tri, low)


def _expert_kernel(start_ref, nblk_ref, xs_hbm, wg_ref, wu_ref, wd_ref, ys_hbm,
                   xbuf, ybuf, wg_s, wu_s, wd_s, sem_in, sem_out, *, block):
    nbuf = EXPERT_BUFS
    e = pl.program_id(0)
    n = nblk_ref[e]
    base = start_ref[e]
    total = start_ref[N_EXPERTS - 1] + nblk_ref[N_EXPERTS - 1]

    def in_copy(g):
        rows = pl.ds(pl.multiple_of(g * block, block), block)
        return pltpu.make_async_copy(xs_hbm.at[rows], xbuf.at[g % nbuf], sem_in.at[g % nbuf])

    def out_copy(g):
        rows = pl.ds(pl.multiple_of(g * block, block), block)
        return pltpu.make_async_copy(ybuf.at[g % nbuf], ys_hbm.at[rows], sem_out.at[g % nbuf])

    @pl.when(e == 0)
    def _():
        for g0 in range(nbuf - 1):
            @pl.when(g0 < total)
            def _(g0=g0):
                in_copy(g0).start()

    @pl.when(n > 0)
    def _():
        wg_s[...] = wg_ref[0].astype(BF16)
        wu_s[...] = wu_ref[0].astype(BF16)
        wd_s[...] = wd_ref[0].astype(BF16)

        def body(i, carry):
            g = base + i
            slot = g % nbuf
            in_copy(g).wait()

            @pl.when(g + nbuf - 1 < total)
            def _():
                in_copy(g + nbuf - 1).start()

            @pl.when(g >= nbuf)
            def _():
                out_copy(g - nbuf).wait()

            a, c = _unpack_pairs(xbuf[slot])
            x = jnp.concatenate([a.astype(BF16), c.astype(BF16)], axis=1)
            gate = jnp.dot(x, wg_s[...], preferred_element_type=F32)
            up = jnp.dot(x, wu_s[...], preferred_element_type=F32)
            hmid = (gate * _sigmoid(gate) * up).astype(BF16)
            ybuf[slot] = _pack_pairs(jnp.dot(hmid, wd_s[...], preferred_element_type=F32))
            out_copy(g).start()
            return carry

        lax.fori_loop(0, n, body, 0)

    @pl.when(e == N_EXPERTS - 1)
    def _():
        for back in range(nbuf, 0, -1):
            @pl.when(total >= back)
            def _(back=back):
                out_copy(total - back).wait()


def _experts(xs, start_blk, nblk, w_gate, w_up, w_down, block):
    n_rows = xs.shape[0]
    wblk = lambda e, st, nb: (e, 0, 0)
    grid_spec = pltpu.PrefetchScalarGridSpec(
        num_scalar_prefetch=2,
        grid=(N_EXPERTS,),
        in_specs=[pl.BlockSpec(memory_space=pl.ANY),
                  pl.BlockSpec((1, D_MODEL, D_EXPERT), wblk),
                  pl.BlockSpec((1, D_MODEL, D_EXPERT), wblk),
                  pl.BlockSpec((1, D_EXPERT, D_MODEL), wblk)],
        out_specs=pl.BlockSpec(memory_space=pl.ANY),
        scratch_shapes=[pltpu.VMEM((EXPERT_BUFS, block, HALF), I32),
                        pltpu.VMEM((EXPERT_BUFS, block, HALF), I32),
                        pltpu.VMEM((D_MODEL, D_EXPERT), BF16),
                        pltpu.VMEM((D_MODEL, D_EXPERT), BF16),
                        pltpu.VMEM((D_EXPERT, D_MODEL), BF16),
                        pltpu.SemaphoreType.DMA((EXPERT_BUFS,)),
                        pltpu.SemaphoreType.DMA((EXPERT_BUFS,))])
    return pl.pallas_call(
        functools.partial(_expert_kernel, block=block),
        out_shape=jax.ShapeDtypeStruct((n_rows, HALF), I32),
        grid_spec=grid_spec,
        compiler_params=_cparams(("arbitrary",)),
        name="experts",
    )(start_blk, nblk, xs, w_gate, w_up, w_down)


def _final_kernel(x1_ref, y0_ref, y1_ref, ew_ref, mod_ref, o_ref):
    a0, b0 = _unpack_pairs(y0_ref[...])
    a1, b1 = _unpack_pairs(y1_ref[...])
    w0 = ew_ref[:, 0:1]
    w1 = ew_ref[:, 1:2]
    moe = jnp.concatenate([w0 * a0 + w1 * a1, w0 * b0 + w1 * b1], axis=1)
    o_ref[0] = x1_ref[0] + mod_ref[0][5:6] * moe


def _final(x1, y0, y1, ew, mod, tm):
    b, t, d = x1.shape
    nt = t // tm
    flat = lambda i, j: (i * nt + j, 0)
    return pl.pallas_call(
        _final_kernel,
        out_shape=jax.ShapeDtypeStruct((b, t, d), F32),
        grid=(b, nt),
        in_specs=[pl.BlockSpec((1, tm, d), lambda i, j: (i, j, 0)),
                  pl.BlockSpec((tm, HALF), flat),
                  pl.BlockSpec((tm, HALF), flat),
                  pl.BlockSpec((tm, 8), flat),
                  pl.BlockSpec((1, 6, d), lambda i, j: (i, 0, 0))],
        out_specs=pl.BlockSpec((1, tm, d), lambda i, j: (i, j, 0)),
        compiler_params=_cparams(("arbitrary", "arbitrary")),
        name="final",
    )(x1, y0, y1, ew, mod)


def _sc_window(rows_per_worker):
    for w in range(SC_MAX_WINDOW, 7, -8):
        if rows_per_worker % w == 0:
            return w
    raise ValueError(f"no SparseCore window divides {rows_per_worker} rows per worker")


def _sc_split(idx):
    n = idx.shape[0]
    per = n // SC_WORKERS
    assert per * SC_WORKERS == n
    win = _sc_window(per)
    return idx.reshape(SC_WORKERS, per // win, win), per // win, win


def _sc_worker_id():
    return lax.axis_index("s") * SC_CORES + lax.axis_index("c")


def _dispatch_rows(h2_groups, dest_groups, n_rows):
    splits = [(_sc_split(d0), _sc_split(d1)) for d0, d1 in dest_groups]
    ng = len(h2_groups)
    scratch = []
    for (_, _, win), _ in splits:
        scratch += [pltpu.VMEM((win,), I32), pltpu.VMEM((win,), I32), pltpu.VMEM((win, HALF), I32)]

    @functools.partial(
        pl.kernel,
        mesh=plsc.VectorSubcoreMesh(core_axis_name="c", subcore_axis_name="s"),
        out_type=jax.ShapeDtypeStruct((n_rows, HALF), I32),
        scratch_types=scratch,
        name="sc_dispatch",
    )
    def k(*refs):
        x_refs, idx_refs, o_hbm, bufs = refs[:ng], refs[ng:3 * ng], refs[3 * ng], refs[3 * ng + 1:]
        wid = _sc_worker_id()
        for g in range(ng):
            (_, nwin, win), _ = splits[g]
            x_hbm, d0_hbm, d1_hbm = x_refs[g], idx_refs[2 * g], idx_refs[2 * g + 1]
            i0_v, i1_v, rows_v = bufs[3 * g:3 * g + 3]

            @pl.loop(0, nwin)
            def _(j, nwin=nwin, win=win, x_hbm=x_hbm, d0_hbm=d0_hbm, d1_hbm=d1_hbm,
                  i0_v=i0_v, i1_v=i1_v, rows_v=rows_v):
                base = pl.multiple_of((wid * nwin + j) * win, 8)
                pltpu.sync_copy(d0_hbm.at[wid, j], i0_v)
                pltpu.sync_copy(d1_hbm.at[wid, j], i1_v)
                pltpu.sync_copy(x_hbm.at[pl.ds(base, win)], rows_v)
                pltpu.sync_copy(rows_v, o_hbm.at[i0_v])
                pltpu.sync_copy(rows_v, o_hbm.at[i1_v])

    idx_args = []
    for (s0, s1) in splits:
        idx_args += [s0[0], s1[0]]
    return k(*h2_groups, *idx_args)


def _collect_rows(ys, dest_groups):
    splits = [(_sc_split(d0), _sc_split(d1)) for d0, d1 in dest_groups]
    ng = len(dest_groups)
    outs, scratch = [], []
    for (d0, _), ((_, _, win), _) in zip(dest_groups, splits):
        o = jax.ShapeDtypeStruct((d0.shape[0], HALF), I32)
        outs += [o, o]
        scratch += [pltpu.VMEM((win,), I32), pltpu.VMEM((win, HALF), I32)]

    @functools.partial(
        pl.kernel,
        mesh=plsc.VectorSubcoreMesh(core_axis_name="c", subcore_axis_name="s"),
        out_type=tuple(outs),
        scratch_types=scratch,
        name="sc_collect",
    )
    def k(*refs):
        ys_hbm, idx_refs, out_refs, bufs = refs[0], refs[1:1 + 2 * ng], refs[1 + 2 * ng:1 + 4 * ng], refs[1 + 4 * ng:]
        wid = _sc_worker_id()
        for g in range(ng):
            (_, nwin, win), _ = splits[g]
            i_v, rows_v = bufs[2 * g:2 * g + 2]
            for kk in range(2):
                d_hbm, y_hbm = idx_refs[2 * g + kk], out_refs[2 * g + kk]

                @pl.loop(0, nwin)
                def _(j, nwin=nwin, win=win, d_hbm=d_hbm, y_hbm=y_hbm, i_v=i_v, rows_v=rows_v):
                    base = pl.multiple_of((wid * nwin + j) * win, 8)
                    pltpu.sync_copy(d_hbm.at[wid, j], i_v)
                    pltpu.sync_copy(ys_hbm.at[i_v], rows_v)
                    pltpu.sync_copy(rows_v, y_hbm.at[pl.ds(base, win)])

    idx_args = []
    for (s0, s1) in splits:
        idx_args += [s0[0], s1[0]]
    res = k(ys, *idx_args)
    return [(res[2 * g], res[2 * g + 1]) for g in range(ng)]


def _t5_bucket(rel):
    half = N_BUCKETS // 2
    max_exact = half // 2
    n = jnp.abs(rel)
    far = max_exact + (jnp.log(jnp.maximum(n, 1).astype(F32) / max_exact)
                       / math.log(MAX_DISTANCE / max_exact) * (half - max_exact)).astype(I32)
    far = jnp.minimum(far, half - 1)
    return jnp.where(rel > 0, half, 0) + jnp.where(n < max_exact, n, far)


def _bias_table(rel_bias, cq, nq, no_history):
    nk = WINDOW + cq
    j = jnp.arange(KEY_WIN)[:, None]
    c = jnp.arange(UNIT_Q)[None, :]
    jj = j - (c // cq) * cq
    valid = (jj >= 0) & (jj < nk) & (c < nq)
    if no_history:
        valid = valid & (j >= WINDOW)
    rel = jj - WINDOW - (c % cq)
    onehot = (_t5_bucket(rel)[:, :, None] == jnp.arange(N_BUCKETS)).astype(F32)
    bias = jnp.einsum("jcb,bh->jch", onehot, rel_bias.astype(F32), precision=lax.Precision.HIGHEST)
    bias = jnp.where(valid[:, :, None], bias * LOG2E, -jnp.inf)
    bias = jnp.transpose(bias.reshape(KEY_WIN, UNIT_Q, N_KV, GROUP), (2, 0, 3, 1))
    return bias.reshape(N_KV, KEY_WIN, GROUP * UNIT_Q)


def _sink_table(sinks):
    s = sinks.astype(F32).reshape(N_KV, 1, GROUP, 1)
    return jnp.broadcast_to(s * LOG2E, (N_KV, 1, GROUP, UNIT_Q)).reshape(N_KV, 1, GROUP * UNIT_Q)


def kernel(x_prompt, x_sample, state_attn_k, state_attn_v, state_conv, c_prompt, c_sample,
           rel_bias, w_ada, b_ada, norm1_w, w_in, q_norm_w, k_norm_w, attn_sinks, conv_w,
           w_out, norm2_w, w_router_group, b_router_group, w_router_expert, b_router_expert,
           w_gate, w_up, w_down):
    depth = w_ada.shape[0]
    assert depth == 1
    bp, tp, d = x_prompt.shape
    bs, ts, _ = x_sample.shape
    n_p, n_s = bp * tp, bs * ts
    n_tok = n_p + n_s
    l = 0

    wi = w_in[l]
    qw, kw, vw, rest = wi[:, :d], wi[:, d:d + KV_COLS], wi[:, d + KV_COLS:d + 2 * KV_COLS], wi[:, d + 2 * KV_COLS:]
    w_in_b = jnp.concatenate([qw, rest, kw, vw], axis=1).astype(BF16)
    w_out_b = w_out[l].astype(BF16)
    w_r = jnp.concatenate([w_router_group[l],
                           jnp.transpose(w_router_expert[l], (1, 0, 2)).reshape(d, N_EXPERTS),
                           jnp.zeros((d, LANES - N_GROUPS - N_EXPERTS), F32)], axis=1)
    w_r_hi = lax.reduce_precision(w_r, exponent_bits=8, mantissa_bits=7)
    w_r = jnp.concatenate([w_r_hi.astype(BF16), (w_r - w_r_hi).astype(BF16)], axis=1)
    b_r = jnp.concatenate([b_router_group[l], b_router_expert[l].reshape(-1),
                           jnp.zeros((LANES - N_GROUPS - N_EXPERTS,), F32)]).reshape(1, LANES)
    knw = jnp.tile(k_norm_w[l], N_KV).reshape(1, KV_COLS)
    qsc = jnp.tile(q_norm_w[l] * (HEAD_DIM ** -0.5 * LOG2E), N_KV).reshape(1, KV_COLS)
    n1w = norm1_w[l].reshape(1, d)
    n2w = norm2_w[l].reshape(1, d)

    mod = _ada(jnp.concatenate([c_prompt, c_sample], axis=0), w_ada[l], b_ada[l]).reshape(bp + bs, 6, d)
    mod_p, mod_s = mod[:bp], mod[bp:]

    xs_rows = x_sample.reshape(1, n_s, d)
    mod4_p = mod_p[:, :, None, :]
    mod4_s = jnp.repeat(jnp.transpose(mod_s, (1, 0, 2)), ts, axis=1)[None]
    proj_s = _inproj(xs_rows, mod4_s, n1w, w_in_b, n_s).reshape(bs, ts, IN_COLS)
    sink_tab = _sink_table(attn_sinks[l])
    bias_p = jnp.concatenate([_bias_table(rel_bias, CHUNK, UNIT_Q, False),
                              _bias_table(rel_bias, CHUNK, UNIT_Q, True)], axis=0)
    merged_p, k_p, v_p, c_p = _front(x_prompt, mod_p, n1w, w_in_b, knw, qsc, bias_p, sink_tab, conv_w[l],
                                     MIX_TILE, UNIT_Q)
    state = (state_attn_k[l].reshape(bs, WINDOW, KV_COLS), state_attn_v[l].reshape(bs, WINDOW, KV_COLS),
             state_conv[l])
    merged_s, k_s, v_s, c_s = _mixer(proj_s, knw, qsc, _bias_table(rel_bias, ts, ts, False), sink_tab, conv_w[l],
                                     state, ts, ts)

    x1_p, h2_p, e01_p, ew_p = _outproj(merged_p, x_prompt, mod4_p, w_out_b, n2w, w_r, b_r, ROW_TILE)
    x1_s, h2_s, e01_s, ew_s = _outproj(merged_s.reshape(1, n_s, d), xs_rows, mod4_s, w_out_b, n2w, w_r, b_r, n_s)
    x1_s = x1_s.reshape(bs, ts, d)

    assert n_s == ROW_TILE
    d01, totals = _rank(jnp.concatenate([e01_p, e01_s], axis=0), EXPERT_BLOCK)
    n_sub_p = n_p // ROW_TILE
    dests = [(d01[:n_sub_p, 0].reshape(-1), d01[:n_sub_p, 1].reshape(-1)),
             (d01[n_sub_p:, 0].reshape(-1), d01[n_sub_p:, 1].reshape(-1))]
    nblk = (totals[0, :N_EXPERTS] + EXPERT_BLOCK - 1) // EXPERT_BLOCK
    start_blk = (jnp.cumsum(nblk) - nblk).astype(I32)
    nb_max = -(-2 * n_tok // EXPERT_BLOCK) + N_EXPERTS

    xs = _dispatch_rows([h2_p, h2_s], dests, nb_max * EXPERT_BLOCK)
    ys = _experts(xs, start_blk, nblk.astype(I32), w_gate[l], w_up[l], w_down[l], EXPERT_BLOCK)
    (y0_p, y1_p), (y0_s, y1_s) = _collect_rows(ys, dests)

    y_p = _final(x1_p, y0_p, y1_p, ew_p, mod_p, ROW_TILE)
    y_s = _final(x1_s, y0_s, y1_s, ew_s, mod_s, ts)

    kv_shape = (1, -1, WINDOW, N_KV, HEAD_DIM)
    return (y_p, y_s, k_p.reshape(kv_shape), v_p.reshape(kv_shape), c_p[None],
            k_s.reshape(kv_shape), v_s.reshape(kv_shape), c_s[None])
```

```python
import functools
import math

import numpy as np
import jax
import jax.numpy as jnp
from jax import lax
from jax.experimental import pallas as pl
from jax.experimental.pallas import tpu as pltpu
from jax.experimental.pallas import tpu_sc as plsc

F32 = jnp.float32
BF16 = jnp.bfloat16
I32 = jnp.int32

D_MODEL = 1024
HEAD_DIM = 64
N_HEADS = 16
N_KV = 4
GROUP = 4
CHUNK = 64
WINDOW = 128
N_BUCKETS = 32
MAX_DISTANCE = 128
N_GROUPS = 8
EPG = 8
N_EXPERTS = 64
D_EXPERT = 512
RMS_EPS = 1e-6
LOG2E = math.log2(math.e)
SUM_ROWS = 16
KV_COLS = N_KV * HEAD_DIM
IN_COLS = 6 * D_MODEL + 2 * KV_COLS
HALF = D_MODEL // 2
LANES = 128

VMEM_LIMIT = 56 * 1024 * 1024
INPROJ_TN = 512
ROW_TILE = 512
MIX_TILE = 512
UNIT_Q = 2 * CHUNK
KEY_WIN = WINDOW + UNIT_Q
PROJ_CHUNK = 256
EXPERT_BLOCK = 512
EXPERT_BUFS = 4
SC_CORES = 2
SC_SUBCORES = 16
SC_WORKERS = SC_CORES * SC_SUBCORES
SC_MAX_WINDOW = 128


def _cparams(sem):
    return pltpu.CompilerParams(dimension_semantics=sem, vmem_limit_bytes=VMEM_LIMIT)


def _split_bf16(a):
    hi = a.astype(BF16)
    lo = (a - hi.astype(F32)).astype(BF16)
    return hi, lo


def _dot3(a, b):
    ah, al = _split_bf16(a)
    bh, bl = _split_bf16(b)
    d = functools.partial(jnp.dot, preferred_element_type=F32)
    return d(ah, bh) + (d(ah, bl) + d(al, bh))


def _sigmoid(x):
    return 0.5 * jnp.tanh(0.5 * x) + 0.5


def _pack_pairs(y):
    a = lax.bitcast_convert_type(y[:, :HALF].astype(BF16).astype(F32), I32)
    b = lax.bitcast_convert_type(y[:, HALF:].astype(BF16).astype(F32), I32)
    return a | lax.shift_right_logical(b, jnp.int32(16))


def _unpack_pairs(w):
    a = lax.bitcast_convert_type(w & jnp.int32(-65536), F32)
    b = lax.bitcast_convert_type(lax.shift_left(w, jnp.int32(16)), F32)
    return a, b


def _ada_kernel(c_ref, w_ref, b_ref, o_ref):
    c = c_ref[...]
    s = c * jax.nn.sigmoid(c)
    o_ref[...] = _dot3(s, w_ref[...]) + b_ref[...]


def _ada(c_all, w_ada, b_ada):
    r, d = c_all.shape
    n = w_ada.shape[1]
    tn = 1024
    return pl.pallas_call(
        _ada_kernel,
        out_shape=jax.ShapeDtypeStruct((r, n), F32),
        grid=(n // tn,),
        in_specs=[pl.BlockSpec((r, d), lambda j: (0, 0)),
                  pl.BlockSpec((d, tn), lambda j: (0, j)),
                  pl.BlockSpec((1, tn), lambda j: (0, j))],
        out_specs=pl.BlockSpec((r, tn), lambda j: (0, j)),
        compiler_params=_cparams(("arbitrary",)),
        name="ada",
    )(c_all, w_ada, b_ada.reshape(1, n))


def _inproj_kernel(x_ref, mod_ref, nw_ref, w_ref, o_ref):
    x = x_ref[0]
    mod = mod_ref[0]
    h = x * lax.rsqrt(jnp.mean(x * x, axis=-1, keepdims=True) + RMS_EPS) * nw_ref[...]
    h = h * (1.0 + mod[1]) + mod[0]
    hb = h.astype(BF16)
    for j in range(IN_COLS // INPROJ_TN):
        sl = slice(j * INPROJ_TN, (j + 1) * INPROJ_TN)
        o_ref[0, :, sl] = jnp.dot(hb, w_ref[:, sl], preferred_element_type=F32).astype(BF16)


def _inproj(x, mod4, nw, w_in_b, tm):
    b, t, d = x.shape
    mr = mod4.shape[2]
    assert mr == 1 or (mr == t and tm == t)
    return pl.pallas_call(
        _inproj_kernel,
        out_shape=jax.ShapeDtypeStruct((b, t, IN_COLS), BF16),
        grid=(b, t // tm),
        in_specs=[pl.BlockSpec((1, tm, d), lambda i, j: (i, j, 0)),
                  pl.BlockSpec((1, 6, mr, d), lambda i, j: (i, 0, 0, 0)),
                  pl.BlockSpec((1, d), lambda i, j: (0, 0)),
                  pl.BlockSpec((d, IN_COLS), lambda i, j: (0, 0), pipeline_mode=pl.Buffered(1))],
        out_specs=pl.BlockSpec((1, tm, IN_COLS), lambda i, j: (i, j, 0)),
        compiler_params=_cparams(("arbitrary", "arbitrary")),
        name="inproj",
    )(x, mod4, nw, w_in_b)


def _head_inv_rms(xf, bd, two_pass=True):
    sq = xf * xf
    if two_pass:
        hi, lo = _split_bf16(sq)
        ssq = jnp.dot(hi, bd, preferred_element_type=F32) + jnp.dot(lo, bd, preferred_element_type=F32)
    else:
        ssq = jnp.dot(sq.astype(BF16), bd, preferred_element_type=F32)
    return lax.rsqrt(ssq * (1.0 / HEAD_DIM) + RMS_EPS)


def _mixer_body(q_ref, k_ref, v_ref, bg_ref, c_ref, u_ref, ga_ref, gc_ref,
                knw_ref, qsc_ref, bd_ref, bias_ref, sink_ref, cw_ref,
                kpast_ref, vpast_ref, cpast_ref, upast_ref,
                merged_ref, knew_ref, vnew_ref, cnew_ref,
                kq_buf, vt_buf, attn_buf, *, tq, nq, stateful):
    hb = WINDOW
    pw = UNIT_Q
    t = pl.program_id(1)
    bd = bd_ref[...]

    k = k_ref[0].astype(F32)
    kn = k * _head_inv_rms(k, bd) * knw_ref[...]
    kq = (kn * qsc_ref[...]).astype(BF16)
    vb = v_ref[0]
    vt = vb.astype(F32).T.astype(BF16)

    if stateful:
        kp = kpast_ref[0]
        vp = vpast_ref[0]
        for kv in range(N_KV):
            kq_buf[kv, hb + tq:] = jnp.zeros((KEY_WIN - hb - tq, HEAD_DIM), BF16)
        vt_buf[:, hb + tq:] = jnp.zeros((KV_COLS, KEY_WIN - hb - tq), BF16)
        u_hist = jnp.concatenate([jnp.zeros((6, D_MODEL), F32), cpast_ref[0]], axis=0)
        knew_ref[0] = jnp.concatenate([kp[tq:], kn], axis=0)
        vnew_ref[0] = jnp.concatenate([vp[tq:], vb.astype(F32)], axis=0)
    else:
        kraw = kpast_ref[0].astype(F32)
        kp = kraw * _head_inv_rms(kraw, bd) * knw_ref[...]
        vp = vpast_ref[0].astype(F32)
        u_hist = jnp.where(t == 0, 0.0, cpast_ref[0].astype(F32) * upast_ref[0].astype(F32))
        knew_ref[0] = kn[tq - hb:]
        vnew_ref[0] = vb[tq - hb:].astype(F32)
    _fill_keys(kq_buf, vt_buf, kp, vp, kq, vt, qsc_ref, tq)
    q = q_ref[0]
    _attention_units(lambda kv: q[:, kv * KV_COLS:(kv + 1) * KV_COLS].astype(F32), kq_buf, vt_buf,
                     bias_ref, sink_ref, bd, attn_buf, tq, nq, None if stateful else t)
    cu = c_ref[0].astype(F32) * u_ref[0].astype(F32)
    conv, u_all = _short_conv(u_hist, cu, cw_ref[...], tq)
    cnew_ref[0] = u_all[tq + 6:tq + 8]
    merged = (_sigmoid(ga_ref[0].astype(F32)) * attn_buf[...]
              + _sigmoid(gc_ref[0].astype(F32)) * (bg_ref[0].astype(F32) * conv))
    merged_ref[0] = merged.astype(BF16)


def _fill_keys(kq_buf, vt_buf, kp, vp, kq, vt, qsc_ref, tq):
    hb = WINDOW
    kqp = (kp * qsc_ref[...]).astype(BF16)
    for kv in range(N_KV):
        kq_buf[kv, 0:hb] = kqp[:, kv * HEAD_DIM:(kv + 1) * HEAD_DIM]
        kq_buf[kv, hb:hb + tq] = kq[:, kv * HEAD_DIM:(kv + 1) * HEAD_DIM]
    vt_buf[:, 0:hb] = vp.T.astype(BF16)
    vt_buf[:, hb:hb + tq] = vt


def _attention_units(q_group, kq_buf, vt_buf, bias_ref, sink_ref, bd, attn_buf, tq, nq, t_first, after_unit=None):
    pw = UNIT_Q
    ones_rows = jnp.ones((SUM_ROWS, KEY_WIN), BF16)
    for kv in range(N_KV):
        qf = q_group(kv)
        qn = (qf * _head_inv_rms(qf, bd, two_pass=False)).astype(BF16)
        for u in range(tq // nq):
            r0 = u * nq
            parts = [qn[r0:r0 + nq, g * HEAD_DIM:(g + 1) * HEAD_DIM] for g in range(GROUP)]
            if nq < pw:
                zpad = jnp.zeros((pw - nq, HEAD_DIM), BF16)
                parts = [x for p_ in parts for x in (p_, zpad)]
            qs = jnp.concatenate(parts, axis=0)
            kw = kq_buf[kv, r0:r0 + KEY_WIN]
            st = lax.dot_general(kw, qs, (((1,), (1,)), ((), ())), preferred_element_type=F32)
            if t_first is not None and u == 0:
                bias = jnp.where(t_first == 0, bias_ref[kv + N_KV], bias_ref[kv])
            else:
                bias = bias_ref[kv]
            st = st + bias
            sink = sink_ref[kv]
            m = jnp.maximum(jnp.max(st, axis=0, keepdims=True), sink)
            p = jnp.exp2((st - m).astype(BF16))
            vt1 = jnp.concatenate([vt_buf[kv * HEAD_DIM:(kv + 1) * HEAD_DIM, r0:r0 + KEY_WIN], ones_rows], axis=0)
            pv = jnp.dot(vt1, p, preferred_element_type=F32)
            den = pv[HEAD_DIM:HEAD_DIM + 1] + jnp.exp2(sink - m)
            ot = pv[:HEAD_DIM] / den
            for gp in range(GROUP // 2):
                blk = jnp.concatenate([ot[:, (2 * gp) * pw:(2 * gp + 1) * pw],
                                       ot[:, (2 * gp + 1) * pw:(2 * gp + 2) * pw]], axis=0)
                c0 = (kv * GROUP + 2 * gp) * HEAD_DIM
                attn_buf[r0:r0 + nq, c0:c0 + 2 * HEAD_DIM] = blk.T[:nq]
            if after_unit is not None:
                after_unit(kv * (tq // nq) + u)


def _short_conv(u_hist, cu, cw, tq):
    u_all = jnp.concatenate([u_hist, cu], axis=0)
    conv = cw[0:1] * u_all[6:6 + tq] + cw[1:2] * u_all[7:7 + tq] + cw[2:3] * u_all[8:8 + tq]
    return conv, u_all


def _front_body(x_ref, xp_ref, mod_ref, nw_ref, w_ref, knw_ref, qsc_ref, bd_ref, bias_ref, sink_ref, cw_ref,
                merged_ref, knew_ref, vnew_ref, cnew_ref,
                kq_buf, vt_buf, u_buf, attn_buf, pbuf, *, tq, nq):
    hb = WINDOW
    d = D_MODEL
    t = pl.program_id(1)
    bd = bd_ref[...]
    mod = mod_ref[0]

    @pl.when(t == 0)
    def _():
        u_buf[0] = jnp.zeros((8, d), F32)

    def modnorm(x):
        h = x * lax.rsqrt(jnp.mean(x * x, axis=-1, keepdims=True) + RMS_EPS) * nw_ref[...]
        return (h * (1.0 + mod[1:2]) + mod[0:1]).astype(BF16)

    def proj(hrows, c0, width):
        return jnp.dot(hrows, w_ref[:, c0:c0 + width], preferred_element_type=F32)

    hbf = modnorm(x_ref[0])
    kv_t = proj(hbf, 6 * d, 2 * KV_COLS)
    kv_p = proj(modnorm(xp_ref[0]), 6 * d, 2 * KV_COLS)
    k, v = kv_t[:, :KV_COLS], kv_t[:, KV_COLS:]
    kn = k * _head_inv_rms(k, bd) * knw_ref[...]
    kq = (kn * qsc_ref[...]).astype(BF16)
    kraw = kv_p[:, :KV_COLS]
    kp = kraw * _head_inv_rms(kraw, bd) * knw_ref[...]
    knew_ref[0] = kn[tq - hb:]
    vnew_ref[0] = v[tq - hb:]
    _fill_keys(kq_buf, vt_buf, kp, kv_p[:, KV_COLS:], kq, v.T.astype(BF16), qsc_ref, tq)

    n_units = N_KV * (tq // nq)
    q_chunks = [[g * KV_COLS + c for c in range(0, KV_COLS, PROJ_CHUNK)] for g in range(N_KV)]
    rest_chunks = list(range(d, 6 * d, PROJ_CHUNK))

    def issue(c0):
        pbuf[:, c0:c0 + PROJ_CHUNK] = proj(hbf, c0, PROJ_CHUNK)

    for c0 in q_chunks[0]:
        issue(c0)
    n_rest = len(rest_chunks)

    def after_unit(i):
        kv, u = divmod(i, tq // nq)
        if u == 0 and kv + 1 < N_KV:
            for c0 in q_chunks[kv + 1]:
                issue(c0)
        for c0 in rest_chunks[i * n_rest // n_units:(i + 1) * n_rest // n_units]:
            issue(c0)

    _attention_units(lambda kv: pbuf[:, kv * KV_COLS:(kv + 1) * KV_COLS], kq_buf, vt_buf,
                     bias_ref, sink_ref, bd, attn_buf, tq, nq, t, after_unit)

    cu = pbuf[:, 2 * d:3 * d] * pbuf[:, 3 * d:4 * d]
    conv, u_all = _short_conv(u_buf[t % 2], cu, cw_ref[...], tq)
    cnew_ref[0] = u_all[tq + 6:tq + 8]
    u_buf[(t + 1) % 2] = u_all[tq:tq + 8]
    merged = (_sigmoid(pbuf[:, 4 * d:5 * d]) * attn_buf[...]
              + _sigmoid(pbuf[:, 5 * d:6 * d]) * (pbuf[:, d:2 * d] * conv))
    merged_ref[0] = merged.astype(BF16)


def _front(x, mod, nw, w_in_b, knw, qsc, bias_tab, sink_tab, conv_w, tq, nq):
    b, t, d = x.shape
    r = np.arange(KV_COLS) // HEAD_DIM
    bd = jnp.asarray((r[:, None] == r[None, :]).astype(np.float32), BF16)
    const2 = lambda shp: pl.BlockSpec(shp, lambda i, s: (0, 0))
    const3 = lambda shp: pl.BlockSpec(shp, lambda i, s: (0, 0, 0))
    per_b = lambda shp: pl.BlockSpec(shp, lambda i, s: (i, 0, 0))
    kw_ = tq // WINDOW
    in_specs = [pl.BlockSpec((1, tq, d), lambda i, s: (i, s, 0)),
                pl.BlockSpec((1, WINDOW, d), lambda i, s: (i, jnp.maximum(s * kw_ - 1, 0), 0)),
                per_b((1, 6, d)), const2((1, d)),
                pl.BlockSpec((d, IN_COLS), lambda i, s: (0, 0), pipeline_mode=pl.Buffered(1)),
                const2((1, KV_COLS)), const2((1, KV_COLS)), const2((KV_COLS, KV_COLS)),
                const3(bias_tab.shape), const3(sink_tab.shape), const2((3, d))]
    out_shape = (jax.ShapeDtypeStruct((b, t, d), BF16),
                 jax.ShapeDtypeStruct((b, WINDOW, KV_COLS), F32),
                 jax.ShapeDtypeStruct((b, WINDOW, KV_COLS), F32),
                 jax.ShapeDtypeStruct((b, 2, d), F32))
    out_specs = (pl.BlockSpec((1, tq, d), lambda i, s: (i, s, 0)),
                 per_b((1, WINDOW, KV_COLS)), per_b((1, WINDOW, KV_COLS)), per_b((1, 2, d)))
    return pl.pallas_call(
        functools.partial(_front_body, tq=tq, nq=nq),
        out_shape=out_shape,
        grid=(b, t // tq),
        in_specs=in_specs,
        out_specs=out_specs,
        scratch_shapes=[pltpu.VMEM((N_KV, WINDOW + tq, HEAD_DIM), BF16),
                        pltpu.VMEM((KV_COLS, WINDOW + tq), BF16),
                        pltpu.VMEM((2, 8, d), F32),
                        pltpu.VMEM((tq, d), F32),
                        pltpu.VMEM((tq, 6 * d), F32)],
        compiler_params=_cparams(("arbitrary", "arbitrary")),
        name="front",
    )(x, x, mod, nw, w_in_b, knw, qsc, bd, bias_tab, sink_tab, conv_w)


def _mixer(proj, knw, qsc, bias_tab, sink_tab, conv_w, state, tq, nq):
    b, t, _ = proj.shape
    d = D_MODEL
    stateful = state is not None
    key_rows = max(WINDOW + tq, KEY_WIN)
    r = np.arange(KV_COLS) // HEAD_DIM
    bd = jnp.asarray((r[:, None] == r[None, :]).astype(np.float32), BF16)
    wide = lambda j: pl.BlockSpec((1, tq, d), lambda i, s, j=j: (i, s, j))
    kvspec = lambda j: pl.BlockSpec((1, tq, KV_COLS), lambda i, s, j=j: (i, s, j))
    const2 = lambda shp: pl.BlockSpec(shp, lambda i, s: (0, 0))
    const3 = lambda shp: pl.BlockSpec(shp, lambda i, s: (0, 0, 0))
    per_b = lambda shp: pl.BlockSpec(shp, lambda i, s: (i, 0, 0))
    kvblk = 6 * d // KV_COLS
    if stateful:
        hist_specs = [per_b((1, WINDOW, KV_COLS)), per_b((1, WINDOW, KV_COLS)), per_b((1, 2, d)), per_b((1, 2, d))]
        hist_args = [state[0], state[1], state[2], state[2]]
    else:
        kw_ = tq // WINDOW
        prev_kv = lambda j: pl.BlockSpec((1, WINDOW, KV_COLS),
                                         lambda i, s, j=j: (i, jnp.maximum(s * kw_ - 1, 0), j))
        prev8 = lambda j: pl.BlockSpec((1, 8, d), lambda i, s, j=j: (i, jnp.maximum(s * (tq // 8) - 1, 0), j))
        hist_specs = [prev_kv(kvblk), prev_kv(kvblk + 1), prev8(2), prev8(3)]
        hist_args = [proj, proj, proj, proj]
    in_specs = [wide(0), kvspec(kvblk), kvspec(kvblk + 1), wide(1), wide(2), wide(3), wide(4), wide(5),
                const2((1, KV_COLS)), const2((1, KV_COLS)), const2((KV_COLS, KV_COLS)),
                const3(bias_tab.shape), const3(sink_tab.shape), const2((3, d))] + hist_specs
    out_shape = (jax.ShapeDtypeStruct((b, t, d), BF16),
                 jax.ShapeDtypeStruct((b, WINDOW, KV_COLS), F32),
                 jax.ShapeDtypeStruct((b, WINDOW, KV_COLS), F32),
                 jax.ShapeDtypeStruct((b, 2, d), F32))
    out_specs = (pl.BlockSpec((1, tq, d), lambda i, s: (i, s, 0)),
                 per_b((1, WINDOW, KV_COLS)), per_b((1, WINDOW, KV_COLS)), per_b((1, 2, d)))
    return pl.pallas_call(
        functools.partial(_mixer_body, tq=tq, nq=nq, stateful=stateful),
        out_shape=out_shape,
        grid=(b, t // tq),
        in_specs=in_specs,
        out_specs=out_specs,
        scratch_shapes=[pltpu.VMEM((N_KV, key_rows, HEAD_DIM), BF16),
                        pltpu.VMEM((KV_COLS, key_rows), BF16),
                        pltpu.VMEM((tq, d), F32)],
        compiler_params=_cparams(("arbitrary", "arbitrary")),
        name="mixer_state" if stateful else "mixer",
    )(proj, proj, proj, proj, proj, proj, proj, proj, knw, qsc, bd, bias_tab, sink_tab, conv_w, *hist_args)


def _route(logits):
    lane = lax.broadcasted_iota(I32, logits.shape, 1).astype(F32)
    neg = -jnp.inf
    big = float(1 << 20)
    gl = jnp.where(lane < N_GROUPS, logits, neg)
    gmax = jnp.max(gl, axis=-1, keepdims=True)
    g_idx = jnp.min(jnp.where(gl == gmax, lane, big), axis=-1, keepdims=True)
    g_w = 1.0 / jnp.sum(jnp.exp(gl - gmax), axis=-1, keepdims=True)
    lo = N_GROUPS + g_idx * EPG
    el = jnp.where((lane >= lo) & (lane < lo + EPG), logits, neg)
    m1 = jnp.max(el, axis=-1, keepdims=True)
    i1 = jnp.min(jnp.where(el == m1, lane, big), axis=-1, keepdims=True)
    el2 = jnp.where(lane == i1, neg, el)
    m2 = jnp.max(el2, axis=-1, keepdims=True)
    i2 = jnp.min(jnp.where(el2 == m2, lane, big), axis=-1, keepdims=True)
    r = jnp.exp(m2 - m1)
    w1 = 1.0 / (1.0 + r)
    w2 = r / (1.0 + r)
    return i1 - N_GROUPS, i2 - N_GROUPS, g_w * w1, g_w * w2


def _outproj_kernel(m_ref, x_ref, mod_ref, wo_ref, nw_ref, wr_ref, br_ref,
                    x1_ref, h2_ref, e01_ref, ew_ref):
    mod = mod_ref[0]
    mix = jnp.dot(m_ref[0], wo_ref[...], preferred_element_type=F32)
    x1 = x_ref[0] + mod[2] * mix
    x1_ref[0] = x1
    h = x1 * lax.rsqrt(jnp.mean(x1 * x1, axis=-1, keepdims=True) + RMS_EPS) * nw_ref[...]
    h = h * (1.0 + mod[4]) + mod[3]
    h2_ref[...] = _pack_pairs(h)
    tm = h.shape[0]
    h_hi, h_lo = _split_bf16(h)
    prod = jnp.dot(jnp.concatenate([h_hi, h_lo], axis=0), wr_ref[...], preferred_element_type=F32)
    logits = prod[:tm, :LANES] + (prod[:tm, LANES:] + prod[tm:, :LANES]) + br_ref[...]
    e1, e2, w1, w2 = _route(logits)
    e01_ref[0] = jnp.concatenate([_col_to_row(e1), _col_to_row(e2)], axis=0).astype(I32)
    lane8 = lax.broadcasted_iota(I32, (h.shape[0], 8), 1)
    ew_ref[...] = jnp.where(lane8 == 0, w1, jnp.where(lane8 == 1, w2, 0.0))


def _outproj(merged, x, mod4, w_out_b, nw, w_r, b_r, tm):
    b, t, d = x.shape
    nt = t // tm
    mr = mod4.shape[2]
    assert mr == 1 or (mr == t and nt == 1)
    flat = lambda i, j: (i * nt + j, 0)
    return pl.pallas_call(
        _outproj_kernel,
        out_shape=(jax.ShapeDtypeStruct((b, t, d), F32),
                   jax.ShapeDtypeStruct((b * t, HALF), I32),
                   jax.ShapeDtypeStruct((b * nt, 2, tm), I32),
                   jax.ShapeDtypeStruct((b * t, 8), F32)),
        grid=(b, nt),
        in_specs=[pl.BlockSpec((1, tm, d), lambda i, j: (i, j, 0)),
                  pl.BlockSpec((1, tm, d), lambda i, j: (i, j, 0)),
                  pl.BlockSpec((1, 6, mr, d), lambda i, j: (i, 0, 0, 0)),
                  pl.BlockSpec((d, d), lambda i, j: (0, 0)),
                  pl.BlockSpec((1, d), lambda i, j: (0, 0)),
                  pl.BlockSpec((d, 2 * LANES), lambda i, j: (0, 0)),
                  pl.BlockSpec((1, LANES), lambda i, j: (0, 0))],
        out_specs=(pl.BlockSpec((1, tm, d), lambda i, j: (i, j, 0)),
                   pl.BlockSpec((tm, HALF), flat),
                   pl.BlockSpec((1, 2, tm), lambda i, j: (i * nt + j, 0, 0)),
                   pl.BlockSpec((tm, 8), flat)),
        compiler_params=_cparams(("arbitrary", "arbitrary")),
        name="outproj",
    )(merged, x, mod4, w_out_b, nw, w_r, b_r)


def _col_to_row(col):
    eye = lax.broadcasted_iota(I32, (LANES, LANES), 0) == lax.broadcasted_iota(I32, (LANES, LANES), 1)
    parts = [jnp.sum(jnp.where(eye, col[r * LANES:(r + 1) * LANES], 0.0), axis=0, keepdims=True)
             for r in range(col.shape[0] // LANES)]
    return jnp.concatenate(parts, axis=1)


def _rank_kernel(e_ref, tri_ref, low_ref, d_ref, tot_ref, *, block):
    n_sub, _, t = e_ref.shape
    sub = lax.broadcasted_iota(I32, (LANES, t), 0)

    def hots(s):
        e = e_ref[s]
        return sub == e[0:1], sub == e[1:2]

    def count(s, cnt):
        h0, h1 = hots(s)
        return cnt + jnp.sum(jnp.where(h0 | h1, 1.0, 0.0), axis=1, keepdims=True)

    cnt = lax.fori_loop(0, n_sub, count, jnp.zeros((LANES, 1), F32))
    tot_ref[...] = _col_to_row(cnt).astype(I32)
    nblk = jnp.floor((cnt + (block - 1)) * (1.0 / block))
    hi = jnp.floor(nblk * (1.0 / 16.0))
    lo = nblk - hi * 16.0
    low = low_ref[...]
    bcast = lambda c: jnp.broadcast_to(c, (LANES, LANES)).astype(BF16)
    excl = (jnp.dot(low, bcast(hi), preferred_element_type=F32) * 16.0
            + jnp.dot(low, bcast(lo), preferred_element_type=F32))
    starts = excl[:, 0:1] * float(block)

    def place(s, running):
        h0, h1 = hots(s)
        onehot = jnp.where(h0 | h1, 1.0, 0.0)
        prefix = jnp.dot(onehot.astype(BF16), tri_ref[...], preferred_element_type=F32)
        pos = prefix + running
        d0 = jnp.sum(jnp.where(h0, pos, 0.0), axis=0, keepdims=True)
        d1 = jnp.sum(jnp.where(h1, pos, 0.0), axis=0, keepdims=True)
        d_ref[s] = jnp.concatenate([d0, d1], axis=0).astype(I32)
        return running + jnp.sum(onehot, axis=1, keepdims=True)

    lax.fori_loop(0, n_sub, place, starts)


def _rank(e01, block):
    n_sub, _, t = e01.shape
    r = np.arange(t)
    tri = jnp.asarray((r[:, None] < r[None, :]).astype(np.float32), BF16)
    l = np.arange(LANES)
    low = jnp.asarray((l[None, :] < l[:, None]).astype(np.float32), BF16)
    return pl.pallas_call(
        functools.partial(_rank_kernel, block=block),
        out_shape=(jax.ShapeDtypeStruct((n_sub, 2, t), I32), jax.ShapeDtypeStruct((1, LANES), I32)),
        compiler_params=pltpu.CompilerParams(vmem_limit_bytes=VMEM_LIMIT),
        name="rank",
    )(e01, tri, low)


def _expert_kernel(start_ref, nblk_ref, xs_hbm, wg_ref, wu_ref, wd_ref, ys_hbm,
                   xbuf, ybuf, wg_s, wu_s, wd_s, sem_in, sem_out, *, block):
    nbuf = EXPERT_BUFS
    e = pl.program_id(0)
    n = nblk_ref[e]
    base = start_ref[e]
    total = start_ref[N_EXPERTS - 1] + nblk_ref[N_EXPERTS - 1]

    def in_copy(g):
        rows = pl.ds(pl.multiple_of(g * block, block), block)
        return pltpu.make_async_copy(xs_hbm.at[rows], xbuf.at[g % nbuf], sem_in.at[g % nbuf])

    def out_copy(g):
        rows = pl.ds(pl.multiple_of(g * block, block), block)
        return pltpu.make_async_copy(ybuf.at[g % nbuf], ys_hbm.at[rows], sem_out.at[g % nbuf])

    @pl.when(e == 0)
    def _():
        for g0 in range(nbuf - 1):
            @pl.when(g0 < total)
            def _(g0=g0):
                in_copy(g0).start()

    @pl.when(n > 0)
    def _():
        wg_s[...] = wg_ref[0].astype(BF16)
        wu_s[...] = wu_ref[0].astype(BF16)
        wd_s[...] = wd_ref[0].astype(BF16)

        def body(i, carry):
            g = base + i
            slot = g % nbuf
            in_copy(g).wait()

            @pl.when(g + nbuf - 1 < total)
            def _():
                in_copy(g + nbuf - 1).start()

            @pl.when(g >= nbuf)
            def _():
                out_copy(g - nbuf).wait()

            a, c = _unpack_pairs(xbuf[slot])
            x = jnp.concatenate([a.astype(BF16), c.astype(BF16)], axis=1)
            gate = jnp.dot(x, wg_s[...], preferred_element_type=F32)
            up = jnp.dot(x, wu_s[...], preferred_element_type=F32)
            hmid = (gate * _sigmoid(gate) * up).astype(BF16)
            ybuf[slot] = _pack_pairs(jnp.dot(hmid, wd_s[...], preferred_element_type=F32))
            out_copy(g).start()
            return carry

        lax.fori_loop(0, n, body, 0)

    @pl.when(e == N_EXPERTS - 1)
    def _():
        for back in range(nbuf, 0, -1):
            @pl.when(total >= back)
            def _(back=back):
                out_copy(total - back).wait()


def _experts(xs, start_blk, nblk, w_gate, w_up, w_down, block):
    n_rows = xs.shape[0]
    wblk = lambda e, st, nb: (e, 0, 0)
    grid_spec = pltpu.PrefetchScalarGridSpec(
        num_scalar_prefetch=2,
        grid=(N_EXPERTS,),
        in_specs=[pl.BlockSpec(memory_space=pl.ANY),
                  pl.BlockSpec((1, D_MODEL, D_EXPERT), wblk),
                  pl.BlockSpec((1, D_MODEL, D_EXPERT), wblk),
                  pl.BlockSpec((1, D_EXPERT, D_MODEL), wblk)],
        out_specs=pl.BlockSpec(memory_space=pl.ANY),
        scratch_shapes=[pltpu.VMEM((EXPERT_BUFS, block, HALF), I32),
                        pltpu.VMEM((EXPERT_BUFS, block, HALF), I32),
                        pltpu.VMEM((D_MODEL, D_EXPERT), BF16),
                        pltpu.VMEM((D_MODEL, D_EXPERT), BF16),
                        pltpu.VMEM((D_EXPERT, D_MODEL), BF16),
                        pltpu.SemaphoreType.DMA((EXPERT_BUFS,)),
                        pltpu.SemaphoreType.DMA((EXPERT_BUFS,))])
    return pl.pallas_call(
        functools.partial(_expert_kernel, block=block),
        out_shape=jax.ShapeDtypeStruct((n_rows, HALF), I32),
        grid_spec=grid_spec,
        compiler_params=_cparams(("arbitrary",)),
        name="experts",
    )(start_blk, nblk, xs, w_gate, w_up, w_down)


def _final_kernel(x1_ref, y0_ref, y1_ref, ew_ref, mod_ref, o_ref):
    a0, b0 = _unpack_pairs(y0_ref[...])
    a1, b1 = _unpack_pairs(y1_ref[...])
    w0 = ew_ref[:, 0:1]
    w1 = ew_ref[:, 1:2]
    moe = jnp.concatenate([w0 * a0 + w1 * a1, w0 * b0 + w1 * b1], axis=1)
    o_ref[0] = x1_ref[0] + mod_ref[0][5:6] * moe


def _final(x1, y0, y1, ew, mod, tm):
    b, t, d = x1.shape
    nt = t // tm
    flat = lambda i, j: (i * nt + j, 0)
    return pl.pallas_call(
        _final_kernel,
        out_shape=jax.ShapeDtypeStruct((b, t, d), F32),
        grid=(b, nt),
        in_specs=[pl.BlockSpec((1, tm, d), lambda i, j: (i, j, 0)),
                  pl.BlockSpec((tm, HALF), flat),
                  pl.BlockSpec((tm, HALF), flat),
                  pl.BlockSpec((tm, 8), flat),
                  pl.BlockSpec((1, 6, d), lambda i, j: (i, 0, 0))],
        out_specs=pl.BlockSpec((1, tm, d), lambda i, j: (i, j, 0)),
        compiler_params=_cparams(("arbitrary", "arbitrary")),
        name="final",
    )(x1, y0, y1, ew, mod)


def _sc_window(rows_per_worker):
    for w in range(SC_MAX_WINDOW, 7, -8):
        if rows_per_worker % w == 0:
            return w
    raise ValueError(f"no SparseCore window divides {rows_per_worker} rows per worker")


def _sc_split(idx):
    n = idx.shape[0]
    per = n // SC_WORKERS
    assert per * SC_WORKERS == n
    win = _sc_window(per)
    return idx.reshape(SC_WORKERS, per // win, win), per // win, win


def _sc_worker_id():
    return lax.axis_index("s") * SC_CORES + lax.axis_index("c")


def _dispatch_rows(h2_groups, dest_groups, n_rows):
    splits = [(_sc_split(d0), _sc_split(d1)) for d0, d1 in dest_groups]
    ng = len(h2_groups)
    scratch = []
    for (_, _, win), _ in splits:
        scratch += [pltpu.VMEM((win,), I32), pltpu.VMEM((win,), I32), pltpu.VMEM((win, HALF), I32)]

    @functools.partial(
        pl.kernel,
        mesh=plsc.VectorSubcoreMesh(core_axis_name="c", subcore_axis_name="s"),
        out_type=jax.ShapeDtypeStruct((n_rows, HALF), I32),
        scratch_types=scratch,
        name="sc_dispatch",
    )
    def k(*refs):
        x_refs, idx_refs, o_hbm, bufs = refs[:ng], refs[ng:3 * ng], refs[3 * ng], refs[3 * ng + 1:]
        wid = _sc_worker_id()
        for g in range(ng):
            (_, nwin, win), _ = splits[g]
            x_hbm, d0_hbm, d1_hbm = x_refs[g], idx_refs[2 * g], idx_refs[2 * g + 1]
            i0_v, i1_v, rows_v = bufs[3 * g:3 * g + 3]

            @pl.loop(0, nwin)
            def _(j, nwin=nwin, win=win, x_hbm=x_hbm, d0_hbm=d0_hbm, d1_hbm=d1_hbm,
                  i0_v=i0_v, i1_v=i1_v, rows_v=rows_v):
                base = pl.multiple_of((wid * nwin + j) * win, 8)
                pltpu.sync_copy(d0_hbm.at[wid, j], i0_v)
                pltpu.sync_copy(d1_hbm.at[wid, j], i1_v)
                pltpu.sync_copy(x_hbm.at[pl.ds(base, win)], rows_v)
                pltpu.sync_copy(rows_v, o_hbm.at[i0_v])
                pltpu.sync_copy(rows_v, o_hbm.at[i1_v])

    idx_args = []
    for (s0, s1) in splits:
        idx_args += [s0[0], s1[0]]
    return k(*h2_groups, *idx_args)


def _collect_rows(ys, dest_groups):
    splits = [(_sc_split(d0), _sc_split(d1)) for d0, d1 in dest_groups]
    ng = len(dest_groups)
    outs, scratch = [], []
    for (d0, _), ((_, _, win), _) in zip(dest_groups, splits):
        o = jax.ShapeDtypeStruct((d0.shape[0], HALF), I32)
        outs += [o, o]
        scratch += [pltpu.VMEM((win,), I32), pltpu.VMEM((win, HALF), I32)]

    @functools.partial(
        pl.kernel,
        mesh=plsc.VectorSubcoreMesh(core_axis_name="c", subcore_axis_name="s"),
        out_type=tuple(outs),
        scratch_types=scratch,
        name="sc_collect",
    )
    def k(*refs):
        ys_hbm, idx_refs, out_refs, bufs = refs[0], refs[1:1 + 2 * ng], refs[1 + 2 * ng:1 + 4 * ng], refs[1 + 4 * ng:]
        wid = _sc_worker_id()
        for g in range(ng):
            (_, nwin, win), _ = splits[g]
            i_v, rows_v = bufs[2 * g:2 * g + 2]
            for kk in range(2):
                d_hbm, y_hbm = idx_refs[2 * g + kk], out_refs[2 * g + kk]

                @pl.loop(0, nwin)
                def _(j, nwin=nwin, win=win, d_hbm=d_hbm, y_hbm=y_hbm, i_v=i_v, rows_v=rows_v):
                    base = pl.multiple_of((wid * nwin + j) * win, 8)
                    pltpu.sync_copy(d_hbm.at[wid, j], i_v)
                    pltpu.sync_copy(ys_hbm.at[i_v], rows_v)
                    pltpu.sync_copy(rows_v, y_hbm.at[pl.ds(base, win)])

    idx_args = []
    for (s0, s1) in splits:
        idx_args += [s0[0], s1[0]]
    res = k(ys, *idx_args)
    return [(res[2 * g], res[2 * g + 1]) for g in range(ng)]


def _t5_bucket(rel):
    half = N_BUCKETS // 2
    max_exact = half // 2
    n = jnp.abs(rel)
    far = max_exact + (jnp.log(jnp.maximum(n, 1).astype(F32) / max_exact)
                       / math.log(MAX_DISTANCE / max_exact) * (half - max_exact)).astype(I32)
    far = jnp.minimum(far, half - 1)
    return jnp.where(rel > 0, half, 0) + jnp.where(n < max_exact, n, far)


def _bias_table(rel_bias, cq, nq, no_history):
    nk = WINDOW + cq
    j = jnp.arange(KEY_WIN)[:, None]
    c = jnp.arange(UNIT_Q)[None, :]
    jj = j - (c // cq) * cq
    valid = (jj >= 0) & (jj < nk) & (c < nq)
    if no_history:
        valid = valid & (j >= WINDOW)
    rel = jj - WINDOW - (c % cq)
    onehot = (_t5_bucket(rel)[:, :, None] == jnp.arange(N_BUCKETS)).astype(F32)
    bias = jnp.einsum("jcb,bh->jch", onehot, rel_bias.astype(F32), precision=lax.Precision.HIGHEST)
    bias = jnp.where(valid[:, :, None], bias * LOG2E, -jnp.inf)
    bias = jnp.transpose(bias.reshape(KEY_WIN, UNIT_Q, N_KV, GROUP), (2, 0, 3, 1))
    return bias.reshape(N_KV, KEY_WIN, GROUP * UNIT_Q)


def _sink_table(sinks):
    s = sinks.astype(F32).reshape(N_KV, 1, GROUP, 1)
    return jnp.broadcast_to(s * LOG2E, (N_KV, 1, GROUP, UNIT_Q)).reshape(N_KV, 1, GROUP * UNIT_Q)


def kernel(x_prompt, x_sample, state_attn_k, state_attn_v, state_conv, c_prompt, c_sample,
           rel_bias, w_ada, b_ada, norm1_w, w_in, q_norm_w, k_norm_w, attn_sinks, conv_w,
           w_out, norm2_w, w_router_group, b_router_group, w_router_expert, b_router_expert,
           w_gate, w_up, w_down):
    depth = w_ada.shape[0]
    assert depth == 1
    bp, tp, d = x_prompt.shape
    bs, ts, _ = x_sample.shape
    n_p, n_s = bp * tp, bs * ts
    n_tok = n_p + n_s
    l = 0

    wi = w_in[l]
    qw, kw, vw, rest = wi[:, :d], wi[:, d:d + KV_COLS], wi[:, d + KV_COLS:d + 2 * KV_COLS], wi[:, d + 2 * KV_COLS:]
    w_in_b = jnp.concatenate([qw, rest, kw, vw], axis=1).astype(BF16)
    w_out_b = w_out[l].astype(BF16)
    w_r = jnp.concatenate([w_router_group[l],
                           jnp.transpose(w_router_expert[l], (1, 0, 2)).reshape(d, N_EXPERTS),
                           jnp.zeros((d, LANES - N_GROUPS - N_EXPERTS), F32)], axis=1)
    w_r_hi = lax.reduce_precision(w_r, exponent_bits=8, mantissa_bits=7)
    w_r = jnp.concatenate([w_r_hi.astype(BF16), (w_r - w_r_hi).astype(BF16)], axis=1)
    b_r = jnp.concatenate([b_router_group[l], b_router_expert[l].reshape(-1),
                           jnp.zeros((LANES - N_GROUPS - N_EXPERTS,), F32)]).reshape(1, LANES)
    knw = jnp.tile(k_norm_w[l], N_KV).reshape(1, KV_COLS)
    qsc = jnp.tile(q_norm_w[l] * (HEAD_DIM ** -0.5 * LOG2E), N_KV).reshape(1, KV_COLS)
    n1w = norm1_w[l].reshape(1, d)
    n2w = norm2_w[l].reshape(1, d)

    mod = _ada(jnp.concatenate([c_prompt, c_sample], axis=0), w_ada[l], b_ada[l]).reshape(bp + bs, 6, d)
    mod_p, mod_s = mod[:bp], mod[bp:]

    xs_rows = x_sample.reshape(1, n_s, d)
    mod4_p = mod_p[:, :, None, :]
    mod4_s = jnp.repeat(jnp.transpose(mod_s, (1, 0, 2)), ts, axis=1)[None]
    proj_s = _inproj(xs_rows, mod4_s, n1w, w_in_b, n_s).reshape(bs, ts, IN_COLS)
    sink_tab = _sink_table(attn_sinks[l])
    bias_p = jnp.concatenate([_bias_table(rel_bias, CHUNK, UNIT_Q, False),
                              _bias_table(rel_bias, CHUNK, UNIT_Q, True)], axis=0)
    merged_p, k_p, v_p, c_p = _front(x_prompt, mod_p, n1w, w_in_b, knw, qsc, bias_p, sink_tab, conv_w[l],
                                     MIX_TILE, UNIT_Q)
    state = (state_attn_k[l].reshape(bs, WINDOW, KV_COLS), state_attn_v[l].reshape(bs, WINDOW, KV_COLS),
             state_conv[l])
    merged_s, k_s, v_s, c_s = _mixer(proj_s, knw, qsc, _bias_table(rel_bias, ts, ts, False), sink_tab, conv_w[l],
                                     state, ts, ts)

    x1_p, h2_p, e01_p, ew_p = _outproj(merged_p, x_prompt, mod4_p, w_out_b, n2w, w_r, b_r, ROW_TILE)
    x1_s, h2_s, e01_s, ew_s = _outproj(merged_s.reshape(1, n_s, d), xs_rows, mod4_s, w_out_b, n2w, w_r, b_r, n_s)
    x1_s = x1_s.reshape(bs, ts, d)

    assert n_s == ROW_TILE
    d01, totals = _rank(jnp.concatenate([e01_p, e01_s], axis=0), EXPERT_BLOCK)
    n_sub_p = n_p // ROW_TILE
    dests = [(d01[:n_sub_p, 0].reshape(-1), d01[:n_sub_p, 1].reshape(-1)),
             (d01[n_sub_p:, 0].reshape(-1), d01[n_sub_p:, 1].reshape(-1))]
    nblk = (totals[0, :N_EXPERTS] + EXPERT_BLOCK - 1) // EXPERT_BLOCK
    start_blk = (jnp.cumsum(nblk) - nblk).astype(I32)
    nb_max = -(-2 * n_tok // EXPERT_BLOCK) + N_EXPERTS

    xs = _dispatch_rows([h2_p, h2_s], dests, nb_max * EXPERT_BLOCK)
    ys = _experts(xs, start_blk, nblk.astype(I32), w_gate[l], w_up[l], w_down[l], EXPERT_BLOCK)
    (y0_p, y1_p), (y0_s, y1_s) = _collect_rows(ys, dests)

    y_p = _final(x1_p, y0_p, y1_p, ew_p, mod_p, ROW_TILE)
    y_s = _final(x1_s, y0_s, y1_s, ew_s, mod_s, ts)

    kv_shape = (1, -1, WINDOW, N_KV, HEAD_DIM)
    return (y_p, y_s, k_p.reshape(kv_shape), v_p.reshape(kv_shape), c_p[None],
            k_s.reshape(kv_shape), v_s.reshape(kv_shape), c_s[None])
```

```python
import functools
import math

import numpy as np
import jax
import jax.numpy as jnp
from jax import lax
from jax.experimental import pallas as pl
from jax.experimental.pallas import tpu as pltpu
from jax.experimental.pallas import tpu_sc as plsc

F32 = jnp.float32
BF16 = jnp.bfloat16
I32 = jnp.int32

D_MODEL = 1024
HEAD_DIM = 64
N_HEADS = 16
N_KV = 4
GROUP = 4
CHUNK = 64
WINDOW = 128
N_BUCKETS = 32
MAX_DISTANCE = 128
N_GROUPS = 8
EPG = 8
N_EXPERTS = 64
D_EXPERT = 512
RMS_EPS = 1e-6
LOG2E = math.log2(math.e)
SUM_ROWS = 16
KV_COLS = N_KV * HEAD_DIM
IN_COLS = 6 * D_MODEL + 2 * KV_COLS
HALF = D_MODEL // 2
LANES = 128

VMEM_LIMIT = 56 * 1024 * 1024
INPROJ_TN = 512
ROW_TILE = 512
MIX_TILE = 512
UNIT_Q = 2 * CHUNK
KEY_WIN = WINDOW + UNIT_Q
PROJ_CHUNK = 256
EXPERT_BLOCK = 512
COLLECT_PARTS = 4
EXPERT_BUFS = 4
SC_CORES = 2
SC_SUBCORES = 16
SC_WORKERS = SC_CORES * SC_SUBCORES
SC_MAX_WINDOW = 128


def _cparams(sem):
    return pltpu.CompilerParams(dimension_semantics=sem, vmem_limit_bytes=VMEM_LIMIT)


def _split_bf16(a):
    hi = a.astype(BF16)
    lo = (a - hi.astype(F32)).astype(BF16)
    return hi, lo


def _dot3(a, b):
    ah, al = _split_bf16(a)
    bh, bl = _split_bf16(b)
    d = functools.partial(jnp.dot, preferred_element_type=F32)
    return d(ah, bh) + (d(ah, bl) + d(al, bh))


def _sigmoid(x):
    return 0.5 * jnp.tanh(0.5 * x) + 0.5


def _pack_pairs(y):
    a = lax.bitcast_convert_type(y[:, :HALF].astype(BF16).astype(F32), I32)
    b = lax.bitcast_convert_type(y[:, HALF:].astype(BF16).astype(F32), I32)
    return a | lax.shift_right_logical(b, jnp.int32(16))


def _unpack_pairs(w):
    a = lax.bitcast_convert_type(w & jnp.int32(-65536), F32)
    b = lax.bitcast_convert_type(lax.shift_left(w, jnp.int32(16)), F32)
    return a, b


def _ada_kernel(c_ref, w_ref, b_ref, o_ref):
    c = c_ref[...]
    s = c * jax.nn.sigmoid(c)
    o_ref[...] = _dot3(s, w_ref[...]) + b_ref[...]


def _ada(c_all, w_ada, b_ada):
    r, d = c_all.shape
    n = w_ada.shape[1]
    tn = 1024
    return pl.pallas_call(
        _ada_kernel,
        out_shape=jax.ShapeDtypeStruct((r, n), F32),
        grid=(n // tn,),
        in_specs=[pl.BlockSpec((r, d), lambda j: (0, 0)),
                  pl.BlockSpec((d, tn), lambda j: (0, j)),
                  pl.BlockSpec((1, tn), lambda j: (0, j))],
        out_specs=pl.BlockSpec((r, tn), lambda j: (0, j)),
        compiler_params=_cparams(("arbitrary",)),
        name="ada",
    )(c_all, w_ada, b_ada.reshape(1, n))


def _inproj_kernel(x_ref, mod_ref, nw_ref, w_ref, o_ref):
    x = x_ref[0]
    mod = mod_ref[0]
    h = x * lax.rsqrt(jnp.mean(x * x, axis=-1, keepdims=True) + RMS_EPS) * nw_ref[...]
    h = h * (1.0 + mod[1]) + mod[0]
    hb = h.astype(BF16)
    for j in range(IN_COLS // INPROJ_TN):
        sl = slice(j * INPROJ_TN, (j + 1) * INPROJ_TN)
        o_ref[0, :, sl] = jnp.dot(hb, w_ref[:, sl], preferred_element_type=F32).astype(BF16)


def _inproj(x, mod4, nw, w_in_b, tm):
    b, t, d = x.shape
    mr = mod4.shape[2]
    assert mr == 1 or (mr == t and tm == t)
    return pl.pallas_call(
        _inproj_kernel,
        out_shape=jax.ShapeDtypeStruct((b, t, IN_COLS), BF16),
        grid=(b, t // tm),
        in_specs=[pl.BlockSpec((1, tm, d), lambda i, j: (i, j, 0)),
                  pl.BlockSpec((1, 6, mr, d), lambda i, j: (i, 0, 0, 0)),
                  pl.BlockSpec((1, d), lambda i, j: (0, 0)),
                  pl.BlockSpec((d, IN_COLS), lambda i, j: (0, 0), pipeline_mode=pl.Buffered(1))],
        out_specs=pl.BlockSpec((1, tm, IN_COLS), lambda i, j: (i, j, 0)),
        compiler_params=_cparams(("arbitrary", "arbitrary")),
        name="inproj",
    )(x, mod4, nw, w_in_b)


def _head_inv_rms(xf, bd, two_pass=True):
    sq = xf * xf
    if two_pass:
        hi, lo = _split_bf16(sq)
        ssq = jnp.dot(hi, bd, preferred_element_type=F32) + jnp.dot(lo, bd, preferred_element_type=F32)
    else:
        ssq = jnp.dot(sq.astype(BF16), bd, preferred_element_type=F32)
    return lax.rsqrt(ssq * (1.0 / HEAD_DIM) + RMS_EPS)


def _mixer_body(q_ref, k_ref, v_ref, bg_ref, c_ref, u_ref, ga_ref, gc_ref,
                knw_ref, qsc_ref, bd_ref, bias_ref, sink_ref, cw_ref,
                kpast_ref, vpast_ref, cpast_ref, upast_ref,
                merged_ref, knew_ref, vnew_ref, cnew_ref,
                kq_buf, vt_buf, attn_buf, *, tq, nq, stateful):
    hb = WINDOW
    pw = UNIT_Q
    t = pl.program_id(1)
    bd = bd_ref[...]

    k = k_ref[0].astype(F32)
    kn = k * _head_inv_rms(k, bd) * knw_ref[...]
    kq = (kn * qsc_ref[...]).astype(BF16)
    vb = v_ref[0]
    vt = vb.astype(F32).T.astype(BF16)

    if stateful:
        kp = kpast_ref[0]
        vp = vpast_ref[0]
        for kv in range(N_KV):
            kq_buf[kv, hb + tq:] = jnp.zeros((KEY_WIN - hb - tq, HEAD_DIM), BF16)
        vt_buf[:, hb + tq:] = jnp.zeros((KV_COLS, KEY_WIN - hb - tq), BF16)
        u_hist = jnp.concatenate([jnp.zeros((6, D_MODEL), F32), cpast_ref[0]], axis=0)
        knew_ref[0] = jnp.concatenate([kp[tq:], kn], axis=0)
        vnew_ref[0] = jnp.concatenate([vp[tq:], vb.astype(F32)], axis=0)
    else:
        kraw = kpast_ref[0].astype(F32)
        kp = kraw * _head_inv_rms(kraw, bd) * knw_ref[...]
        vp = vpast_ref[0].astype(F32)
        u_hist = jnp.where(t == 0, 0.0, cpast_ref[0].astype(F32) * upast_ref[0].astype(F32))
        knew_ref[0] = kn[tq - hb:]
        vnew_ref[0] = vb[tq - hb:].astype(F32)
    _fill_keys(kq_buf, vt_buf, kp, vp, kq, vt, qsc_ref, tq)
    q = q_ref[0]
    _attention_units(lambda kv: q[:, kv * KV_COLS:(kv + 1) * KV_COLS].astype(F32), kq_buf, vt_buf,
                     bias_ref, sink_ref, bd, attn_buf, tq, nq, None if stateful else t)
    cu = c_ref[0].astype(F32) * u_ref[0].astype(F32)
    conv, u_all = _short_conv(u_hist, cu, cw_ref[...], tq)
    cnew_ref[0] = u_all[tq + 6:tq + 8]
    merged = (_sigmoid(ga_ref[0].astype(F32)) * attn_buf[...]
              + _sigmoid(gc_ref[0].astype(F32)) * (bg_ref[0].astype(F32) * conv))
    merged_ref[0] = merged.astype(BF16)


def _fill_keys(kq_buf, vt_buf, kp, vp, kq, vt, qsc_ref, tq):
    hb = WINDOW
    kqp = (kp * qsc_ref[...]).astype(BF16)
    for kv in range(N_KV):
        kq_buf[kv, 0:hb] = kqp[:, kv * HEAD_DIM:(kv + 1) * HEAD_DIM]
        kq_buf[kv, hb:hb + tq] = kq[:, kv * HEAD_DIM:(kv + 1) * HEAD_DIM]
    vt_buf[:, 0:hb] = vp.T.astype(BF16)
    vt_buf[:, hb:hb + tq] = vt


def _attention_units(q_group, kq_buf, vt_buf, bias_ref, sink_ref, bd, attn_buf, tq, nq, t_first, after_unit=None):
    pw = UNIT_Q
    ones_rows = jnp.ones((SUM_ROWS, KEY_WIN), BF16)
    for kv in range(N_KV):
        qf = q_group(kv)
        qn = (qf * _head_inv_rms(qf, bd, two_pass=False)).astype(BF16)
        for u in range(tq // nq):
            r0 = u * nq
            parts = [qn[r0:r0 + nq, g * HEAD_DIM:(g + 1) * HEAD_DIM] for g in range(GROUP)]
            if nq < pw:
                zpad = jnp.zeros((pw - nq, HEAD_DIM), BF16)
                parts = [x for p_ in parts for x in (p_, zpad)]
            qs = jnp.concatenate(parts, axis=0)
            kw = kq_buf[kv, r0:r0 + KEY_WIN]
            st = lax.dot_general(kw, qs, (((1,), (1,)), ((), ())), preferred_element_type=F32)
            if t_first is not None and u == 0:
                bias = jnp.where(t_first == 0, bias_ref[kv + N_KV], bias_ref[kv])
            else:
                bias = bias_ref[kv]
            st = st + bias
            sink = sink_ref[kv]
            m = jnp.maximum(jnp.max(st, axis=0, keepdims=True), sink)
            p = jnp.exp2((st - m).astype(BF16))
            vt1 = jnp.concatenate([vt_buf[kv * HEAD_DIM:(kv + 1) * HEAD_DIM, r0:r0 + KEY_WIN], ones_rows], axis=0)
            pv = jnp.dot(vt1, p, preferred_element_type=F32)
            den = pv[HEAD_DIM:HEAD_DIM + 1] + jnp.exp2(sink - m)
            ot = pv[:HEAD_DIM] / den
            for gp in range(GROUP // 2):
                blk = jnp.concatenate([ot[:, (2 * gp) * pw:(2 * gp + 1) * pw],
                                       ot[:, (2 * gp + 1) * pw:(2 * gp + 2) * pw]], axis=0)
                c0 = (kv * GROUP + 2 * gp) * HEAD_DIM
                attn_buf[r0:r0 + nq, c0:c0 + 2 * HEAD_DIM] = blk.T[:nq]
            if after_unit is not None:
                after_unit(kv * (tq // nq) + u)


def _short_conv(u_hist, cu, cw, tq):
    u_all = jnp.concatenate([u_hist, cu], axis=0)
    conv = cw[0:1] * u_all[6:6 + tq] + cw[1:2] * u_all[7:7 + tq] + cw[2:3] * u_all[8:8 + tq]
    return conv, u_all


def _front_body(x_ref, xp_ref, mod_ref, nw_ref, w_ref, knw_ref, qsc_ref, bd_ref, bias_ref, sink_ref, cw_ref,
                merged_ref, knew_ref, vnew_ref, cnew_ref,
                kq_buf, vt_buf, u_buf, attn_buf, pbuf, *, tq, nq):
    hb = WINDOW
    d = D_MODEL
    t = pl.program_id(1)
    bd = bd_ref[...]
    mod = mod_ref[0]

    @pl.when(t == 0)
    def _():
        u_buf[0] = jnp.zeros((8, d), F32)

    def modnorm(x):
        h = x * lax.rsqrt(jnp.mean(x * x, axis=-1, keepdims=True) + RMS_EPS) * nw_ref[...]
        return (h * (1.0 + mod[1:2]) + mod[0:1]).astype(BF16)

    def proj(hrows, c0, width):
        return jnp.dot(hrows, w_ref[:, c0:c0 + width], preferred_element_type=F32)

    hbf = modnorm(x_ref[0])
    kv_t = proj(hbf, 6 * d, 2 * KV_COLS)
    kv_p = proj(modnorm(xp_ref[0]), 6 * d, 2 * KV_COLS)
    k, v = kv_t[:, :KV_COLS], kv_t[:, KV_COLS:]
    kn = k * _head_inv_rms(k, bd) * knw_ref[...]
    kq = (kn * qsc_ref[...]).astype(BF16)
    kraw = kv_p[:, :KV_COLS]
    kp = kraw * _head_inv_rms(kraw, bd) * knw_ref[...]
    knew_ref[0] = kn[tq - hb:]
    vnew_ref[0] = v[tq - hb:]
    _fill_keys(kq_buf, vt_buf, kp, kv_p[:, KV_COLS:], kq, v.T.astype(BF16), qsc_ref, tq)

    n_units = N_KV * (tq // nq)
    q_chunks = [[g * KV_COLS + c for c in range(0, KV_COLS, PROJ_CHUNK)] for g in range(N_KV)]
    rest_chunks = list(range(d, 6 * d, PROJ_CHUNK))

    def issue(c0):
        pbuf[:, c0:c0 + PROJ_CHUNK] = proj(hbf, c0, PROJ_CHUNK)

    for c0 in q_chunks[0]:
        issue(c0)
    n_rest = len(rest_chunks)

    def after_unit(i):
        kv, u = divmod(i, tq // nq)
        if u == 0 and kv + 1 < N_KV:
            for c0 in q_chunks[kv + 1]:
                issue(c0)
        for c0 in rest_chunks[i * n_rest // n_units:(i + 1) * n_rest // n_units]:
            issue(c0)

    _attention_units(lambda kv: pbuf[:, kv * KV_COLS:(kv + 1) * KV_COLS], kq_buf, vt_buf,
                     bias_ref, sink_ref, bd, attn_buf, tq, nq, t, after_unit)

    cu = pbuf[:, 2 * d:3 * d] * pbuf[:, 3 * d:4 * d]
    conv, u_all = _short_conv(u_buf[t % 2], cu, cw_ref[...], tq)
    cnew_ref[0] = u_all[tq + 6:tq + 8]
    u_buf[(t + 1) % 2] = u_all[tq:tq + 8]
    merged = (_sigmoid(pbuf[:, 4 * d:5 * d]) * attn_buf[...]
              + _sigmoid(pbuf[:, 5 * d:6 * d]) * (pbuf[:, d:2 * d] * conv))
    merged_ref[0] = merged.astype(BF16)


def _front(x, mod, nw, w_in_b, knw, qsc, bias_tab, sink_tab, conv_w, tq, nq):
    b, t, d = x.shape
    r = np.arange(KV_COLS) // HEAD_DIM
    bd = jnp.asarray((r[:, None] == r[None, :]).astype(np.float32), BF16)
    const2 = lambda shp: pl.BlockSpec(shp, lambda i, s: (0, 0))
    const3 = lambda shp: pl.BlockSpec(shp, lambda i, s: (0, 0, 0))
    per_b = lambda shp: pl.BlockSpec(shp, lambda i, s: (i, 0, 0))
    kw_ = tq // WINDOW
    in_specs = [pl.BlockSpec((1, tq, d), lambda i, s: (i, s, 0)),
                pl.BlockSpec((1, WINDOW, d), lambda i, s: (i, jnp.maximum(s * kw_ - 1, 0), 0)),
                per_b((1, 6, d)), const2((1, d)),
                pl.BlockSpec((d, IN_COLS), lambda i, s: (0, 0), pipeline_mode=pl.Buffered(1)),
                const2((1, KV_COLS)), const2((1, KV_COLS)), const2((KV_COLS, KV_COLS)),
                const3(bias_tab.shape), const3(sink_tab.shape), const2((3, d))]
    out_shape = (jax.ShapeDtypeStruct((b, t, d), BF16),
                 jax.ShapeDtypeStruct((b, WINDOW, KV_COLS), F32),
                 jax.ShapeDtypeStruct((b, WINDOW, KV_COLS), F32),
                 jax.ShapeDtypeStruct((b, 2, d), F32))
    out_specs = (pl.BlockSpec((1, tq, d), lambda i, s: (i, s, 0)),
                 per_b((1, WINDOW, KV_COLS)), per_b((1, WINDOW, KV_COLS)), per_b((1, 2, d)))
    return pl.pallas_call(
        functools.partial(_front_body, tq=tq, nq=nq),
        out_shape=out_shape,
        grid=(b, t // tq),
        in_specs=in_specs,
        out_specs=out_specs,
        scratch_shapes=[pltpu.VMEM((N_KV, WINDOW + tq, HEAD_DIM), BF16),
                        pltpu.VMEM((KV_COLS, WINDOW + tq), BF16),
                        pltpu.VMEM((2, 8, d), F32),
                        pltpu.VMEM((tq, d), F32),
                        pltpu.VMEM((tq, 6 * d), F32)],
        compiler_params=_cparams(("arbitrary", "arbitrary")),
        name="front",
    )(x, x, mod, nw, w_in_b, knw, qsc, bd, bias_tab, sink_tab, conv_w)


def _mixer(proj, knw, qsc, bias_tab, sink_tab, conv_w, state, tq, nq):
    b, t, _ = proj.shape
    d = D_MODEL
    stateful = state is not None
    key_rows = max(WINDOW + tq, KEY_WIN)
    r = np.arange(KV_COLS) // HEAD_DIM
    bd = jnp.asarray((r[:, None] == r[None, :]).astype(np.float32), BF16)
    wide = lambda j: pl.BlockSpec((1, tq, d), lambda i, s, j=j: (i, s, j))
    kvspec = lambda j: pl.BlockSpec((1, tq, KV_COLS), lambda i, s, j=j: (i, s, j))
    const2 = lambda shp: pl.BlockSpec(shp, lambda i, s: (0, 0))
    const3 = lambda shp: pl.BlockSpec(shp, lambda i, s: (0, 0, 0))
    per_b = lambda shp: pl.BlockSpec(shp, lambda i, s: (i, 0, 0))
    kvblk = 6 * d // KV_COLS
    if stateful:
        hist_specs = [per_b((1, WINDOW, KV_COLS)), per_b((1, WINDOW, KV_COLS)), per_b((1, 2, d)), per_b((1, 2, d))]
        hist_args = [state[0], state[1], state[2], state[2]]
    else:
        kw_ = tq // WINDOW
        prev_kv = lambda j: pl.BlockSpec((1, WINDOW, KV_COLS),
                                         lambda i, s, j=j: (i, jnp.maximum(s * kw_ - 1, 0), j))
        prev8 = lambda j: pl.BlockSpec((1, 8, d), lambda i, s, j=j: (i, jnp.maximum(s * (tq // 8) - 1, 0), j))
        hist_specs = [prev_kv(kvblk), prev_kv(kvblk + 1), prev8(2), prev8(3)]
        hist_args = [proj, proj, proj, proj]
    in_specs = [wide(0), kvspec(kvblk), kvspec(kvblk + 1), wide(1), wide(2), wide(3), wide(4), wide(5),
                const2((1, KV_COLS)), const2((1, KV_COLS)), const2((KV_COLS, KV_COLS)),
                const3(bias_tab.shape), const3(sink_tab.shape), const2((3, d))] + hist_specs
    out_shape = (jax.ShapeDtypeStruct((b, t, d), BF16),
                 jax.ShapeDtypeStruct((b, WINDOW, KV_COLS), F32),
                 jax.ShapeDtypeStruct((b, WINDOW, KV_COLS), F32),
                 jax.ShapeDtypeStruct((b, 2, d), F32))
    out_specs = (pl.BlockSpec((1, tq, d), lambda i, s: (i, s, 0)),
                 per_b((1, WINDOW, KV_COLS)), per_b((1, WINDOW, KV_COLS)), per_b((1, 2, d)))
    return pl.pallas_call(
        functools.partial(_mixer_body, tq=tq, nq=nq, stateful=stateful),
        out_shape=out_shape,
        grid=(b, t // tq),
        in_specs=in_specs,
        out_specs=out_specs,
        scratch_shapes=[pltpu.VMEM((N_KV, key_rows, HEAD_DIM), BF16),
                        pltpu.VMEM((KV_COLS, key_rows), BF16),
                        pltpu.VMEM((tq, d), F32)],
        compiler_params=_cparams(("arbitrary", "arbitrary")),
        name="mixer_state" if stateful else "mixer",
    )(proj, proj, proj, proj, proj, proj, proj, proj, knw, qsc, bd, bias_tab, sink_tab, conv_w, *hist_args)


def _route(logits):
    lane = lax.broadcasted_iota(I32, logits.shape, 1).astype(F32)
    neg = -jnp.inf
    big = float(1 << 20)
    gl = jnp.where(lane < N_GROUPS, logits, neg)
    gmax = jnp.max(gl, axis=-1, keepdims=True)
    g_idx = jnp.min(jnp.where(gl == gmax, lane, big), axis=-1, keepdims=True)
    g_w = 1.0 / jnp.sum(jnp.exp(gl - gmax), axis=-1, keepdims=True)
    lo = N_GROUPS + g_idx * EPG
    el = jnp.where((lane >= lo) & (lane < lo + EPG), logits, neg)
    m1 = jnp.max(el, axis=-1, keepdims=True)
    i1 = jnp.min(jnp.where(el == m1, lane, big), axis=-1, keepdims=True)
    el2 = jnp.where(lane == i1, neg, el)
    m2 = jnp.max(el2, axis=-1, keepdims=True)
    i2 = jnp.min(jnp.where(el2 == m2, lane, big), axis=-1, keepdims=True)
    r = jnp.exp(m2 - m1)
    w1 = 1.0 / (1.0 + r)
    w2 = r / (1.0 + r)
    return i1 - N_GROUPS, i2 - N_GROUPS, g_w * w1, g_w * w2


def _outproj_kernel(m_ref, x_ref, mod_ref, wo_ref, nw_ref, wr_ref, br_ref,
                    x1_ref, h2_ref, e01_ref, ew_ref):
    mod = mod_ref[0]
    mix = jnp.dot(m_ref[0], wo_ref[...], preferred_element_type=F32)
    x1 = x_ref[0] + mod[2] * mix
    x1_ref[0] = x1
    h = x1 * lax.rsqrt(jnp.mean(x1 * x1, axis=-1, keepdims=True) + RMS_EPS) * nw_ref[...]
    h = h * (1.0 + mod[4]) + mod[3]
    h2_ref[...] = _pack_pairs(h)
    tm = h.shape[0]
    h_hi, h_lo = _split_bf16(h)
    prod = jnp.dot(jnp.concatenate([h_hi, h_lo], axis=0), wr_ref[...], preferred_element_type=F32)
    logits = prod[:tm, :LANES] + (prod[:tm, LANES:] + prod[tm:, :LANES]) + br_ref[...]
    e1, e2, w1, w2 = _route(logits)
    e01_ref[0] = jnp.concatenate([_col_to_row(e1), _col_to_row(e2)], axis=0).astype(I32)
    lane8 = lax.broadcasted_iota(I32, (h.shape[0], 8), 1)
    ew_ref[...] = jnp.where(lane8 == 0, w1, jnp.where(lane8 == 1, w2, 0.0))


def _outproj(merged, x, mod4, w_out_b, nw, w_r, b_r, tm):
    b, t, d = x.shape
    nt = t // tm
    mr = mod4.shape[2]
    assert mr == 1 or (mr == t and nt == 1)
    flat = lambda i, j: (i * nt + j, 0)
    return pl.pallas_call(
        _outproj_kernel,
        out_shape=(jax.ShapeDtypeStruct((b, t, d), F32),
                   jax.ShapeDtypeStruct((b * t, HALF), I32),
                   jax.ShapeDtypeStruct((b * nt, 2, tm), I32),
                   jax.ShapeDtypeStruct((b * t, 8), F32)),
        grid=(b, nt),
        in_specs=[pl.BlockSpec((1, tm, d), lambda i, j: (i, j, 0)),
                  pl.BlockSpec((1, tm, d), lambda i, j: (i, j, 0)),
                  pl.BlockSpec((1, 6, mr, d), lambda i, j: (i, 0, 0, 0)),
                  pl.BlockSpec((d, d), lambda i, j: (0, 0)),
                  pl.BlockSpec((1, d), lambda i, j: (0, 0)),
                  pl.BlockSpec((d, 2 * LANES), lambda i, j: (0, 0)),
                  pl.BlockSpec((1, LANES), lambda i, j: (0, 0))],
        out_specs=(pl.BlockSpec((1, tm, d), lambda i, j: (i, j, 0)),
                   pl.BlockSpec((tm, HALF), flat),
                   pl.BlockSpec((1, 2, tm), lambda i, j: (i * nt + j, 0, 0)),
                   pl.BlockSpec((tm, 8), flat)),
        compiler_params=_cparams(("arbitrary", "arbitrary")),
        name="outproj",
    )(merged, x, mod4, w_out_b, nw, w_r, b_r)


def _col_to_row(col):
    eye = lax.broadcasted_iota(I32, (LANES, LANES), 0) == lax.broadcasted_iota(I32, (LANES, LANES), 1)
    parts = [jnp.sum(jnp.where(eye, col[r * LANES:(r + 1) * LANES], 0.0), axis=0, keepdims=True)
             for r in range(col.shape[0] // LANES)]
    return jnp.concatenate(parts, axis=1)


def _rank_kernel(e_ref, tri_ref, low_ref, d_ref, tot_ref, *, block):
    n_sub, _, t = e_ref.shape
    sub = lax.broadcasted_iota(I32, (LANES, t), 0)

    def hots(s):
        e = e_ref[s]
        return sub == e[0:1], sub == e[1:2]

    def count(s, cnt):
        h0, h1 = hots(s)
        return cnt + jnp.sum(jnp.where(h0 | h1, 1.0, 0.0), axis=1, keepdims=True)

    cnt = lax.fori_loop(0, n_sub, count, jnp.zeros((LANES, 1), F32))
    tot_ref[...] = _col_to_row(cnt).astype(I32)
    nblk = jnp.floor((cnt + (block - 1)) * (1.0 / block))
    hi = jnp.floor(nblk * (1.0 / 16.0))
    lo = nblk - hi * 16.0
    low = low_ref[...]
    bcast = lambda c: jnp.broadcast_to(c, (LANES, LANES)).astype(BF16)
    excl = (jnp.dot(low, bcast(hi), preferred_element_type=F32) * 16.0
            + jnp.dot(low, bcast(lo), preferred_element_type=F32))
    starts = excl[:, 0:1] * float(block)

    def place(s, running):
        h0, h1 = hots(s)
        onehot = jnp.where(h0 | h1, 1.0, 0.0)
        prefix = jnp.dot(onehot.astype(BF16), tri_ref[...], preferred_element_type=F32)
        pos = prefix + running
        d0 = jnp.sum(jnp.where(h0, pos, 0.0), axis=0, keepdims=True)
        d1 = jnp.sum(jnp.where(h1, pos, 0.0), axis=0, keepdims=True)
        d_ref[s] = jnp.concatenate([d0, d1], axis=0).astype(I32)
        return running + jnp.sum(onehot, axis=1, keepdims=True)

    lax.fori_loop(0, n_sub, place, starts)


def _rank(e01, block):
    n_sub, _, t = e01.shape
    r = np.arange(t)
    tri = jnp.asarray((r[:, None] < r[None, :]).astype(np.float32), BF16)
    l = np.arange(LANES)
    low = jnp.asarray((l[None, :] < l[:, None]).astype(np.float32), BF16)
    return pl.pallas_call(
        functools.partial(_rank_kernel, block=block),
        out_shape=(jax.ShapeDtypeStruct((n_sub, 2, t), I32), jax.ShapeDtypeStruct((1, LANES), I32)),
        compiler_params=pltpu.CompilerParams(vmem_limit_bytes=VMEM_LIMIT),
        name="rank",
    )(e01, tri, low)


def _expert_kernel(start_ref, nblk_ref, xs_hbm, wg_ref, wu_ref, wd_ref, ys_hbm,
                   xbuf, ybuf, wg_s, wu_s, wd_s, sem_in, sem_out, *, block):
    nbuf = EXPERT_BUFS
    e = pl.program_id(0)
    n = nblk_ref[e]
    base = start_ref[e]
    total = start_ref[N_EXPERTS - 1] + nblk_ref[N_EXPERTS - 1]

    def in_copy(g):
        rows = pl.ds(pl.multiple_of(g * block, block), block)
        return pltpu.make_async_copy(xs_hbm.at[rows], xbuf.at[g % nbuf], sem_in.at[g % nbuf])

    def out_copy(g):
        rows = pl.ds(pl.multiple_of(g * block, block), block)
        return pltpu.make_async_copy(ybuf.at[g % nbuf], ys_hbm.at[rows], sem_out.at[g % nbuf])

    @pl.when(e == 0)
    def _():
        for g0 in range(nbuf - 1):
            @pl.when(g0 < total)
            def _(g0=g0):
                in_copy(g0).start()

    @pl.when(n > 0)
    def _():
        wg_s[...] = wg_ref[0].astype(BF16)
        wu_s[...] = wu_ref[0].astype(BF16)
        wd_s[...] = wd_ref[0].astype(BF16)

        def body(i, carry):
            g = base + i
            slot = g % nbuf
            in_copy(g).wait()

            @pl.when(g + nbuf - 1 < total)
            def _():
                in_copy(g + nbuf - 1).start()

            @pl.when(g >= nbuf)
            def _():
                out_copy(g - nbuf).wait()

            a, c = _unpack_pairs(xbuf[slot])
            x = jnp.concatenate([a.astype(BF16), c.astype(BF16)], axis=1)
            gate = jnp.dot(x, wg_s[...], preferred_element_type=F32)
            up = jnp.dot(x, wu_s[...], preferred_element_type=F32)
            hmid = (gate * _sigmoid(gate) * up).astype(BF16)
            ybuf[slot] = _pack_pairs(jnp.dot(hmid, wd_s[...], preferred_element_type=F32))
            out_copy(g).start()
            return carry

        lax.fori_loop(0, n, body, 0)

    @pl.when(e == N_EXPERTS - 1)
    def _():
        for back in range(nbuf, 0, -1):
            @pl.when(total >= back)
            def _(back=back):
                out_copy(total - back).wait()


def _experts(xs, start_blk, nblk, w_gate, w_up, w_down, block):
    n_rows = xs.shape[0]
    wblk = lambda e, st, nb: (e, 0, 0)
    grid_spec = pltpu.PrefetchScalarGridSpec(
        num_scalar_prefetch=2,
        grid=(N_EXPERTS,),
        in_specs=[pl.BlockSpec(memory_space=pl.ANY),
                  pl.BlockSpec((1, D_MODEL, D_EXPERT), wblk),
                  pl.BlockSpec((1, D_MODEL, D_EXPERT), wblk),
                  pl.BlockSpec((1, D_EXPERT, D_MODEL), wblk)],
        out_specs=pl.BlockSpec(memory_space=pl.ANY),
        scratch_shapes=[pltpu.VMEM((EXPERT_BUFS, block, HALF), I32),
                        pltpu.VMEM((EXPERT_BUFS, block, HALF), I32),
                        pltpu.VMEM((D_MODEL, D_EXPERT), BF16),
                        pltpu.VMEM((D_MODEL, D_EXPERT), BF16),
                        pltpu.VMEM((D_EXPERT, D_MODEL), BF16),
                        pltpu.SemaphoreType.DMA((EXPERT_BUFS,)),
                        pltpu.SemaphoreType.DMA((EXPERT_BUFS,))])
    return pl.pallas_call(
        functools.partial(_expert_kernel, block=block),
        out_shape=jax.ShapeDtypeStruct((n_rows, HALF), I32),
        grid_spec=grid_spec,
        compiler_params=_cparams(("arbitrary",)),
        name="experts",
    )(start_blk, nblk, xs, w_gate, w_up, w_down)


def _final_kernel(x1_ref, y0_ref, y1_ref, ew_ref, mod_ref, o_ref):
    a0, b0 = _unpack_pairs(y0_ref[...])
    a1, b1 = _unpack_pairs(y1_ref[...])
    w0 = ew_ref[:, 0:1]
    w1 = ew_ref[:, 1:2]
    moe = jnp.concatenate([w0 * a0 + w1 * a1, w0 * b0 + w1 * b1], axis=1)
    o_ref[0] = x1_ref[0] + mod_ref[0][5:6] * moe


def _final(x1, y0, y1, ew, mod, tm, b0=0, nb=None, y_prev=None):
    b, t, d = x1.shape
    nb = b if nb is None else nb
    nt = t // tm
    local = lambda i, j: (i * nt + j, 0)
    glob = lambda i, j: ((i + b0) * nt + j, 0)
    rows3 = lambda i, j: (i + b0, j, 0)
    in_specs = [pl.BlockSpec((1, tm, d), rows3),
                pl.BlockSpec((tm, HALF), local),
                pl.BlockSpec((tm, HALF), local),
                pl.BlockSpec((tm, 8), glob),
                pl.BlockSpec((1, 6, d), lambda i, j: (i + b0, 0, 0))]
    args = [x1, y0, y1, ew, mod]
    aliases = {}
    kern = _final_kernel
    if y_prev is not None:
        in_specs.append(pl.BlockSpec(memory_space=pl.ANY))
        args.append(y_prev)
        aliases = {5: 0}
        kern = lambda *refs: _final_kernel(*refs[:5], refs[6])
    return pl.pallas_call(
        kern,
        out_shape=jax.ShapeDtypeStruct((b, t, d), F32),
        grid=(nb, nt),
        in_specs=in_specs,
        out_specs=pl.BlockSpec((1, tm, d), rows3),
        input_output_aliases=aliases,
        compiler_params=_cparams(("arbitrary", "arbitrary")),
        name="final",
    )(*args)


def _sc_window(rows_per_worker):
    for w in range(SC_MAX_WINDOW, 7, -8):
        if rows_per_worker % w == 0:
            return w
    raise ValueError(f"no SparseCore window divides {rows_per_worker} rows per worker")


def _sc_split(idx):
    n = idx.shape[0]
    per = n // SC_WORKERS
    assert per * SC_WORKERS == n
    win = _sc_window(per)
    return idx.reshape(SC_WORKERS, per // win, win), per // win, win


def _sc_worker_id():
    return lax.axis_index("s") * SC_CORES + lax.axis_index("c")


def _dispatch_rows(h2_groups, dest_groups, n_rows):
    splits = [(_sc_split(d0), _sc_split(d1)) for d0, d1 in dest_groups]
    ng = len(h2_groups)
    scratch = []
    for (_, _, win), _ in splits:
        scratch += [pltpu.VMEM((win,), I32), pltpu.VMEM((win,), I32), pltpu.VMEM((win, HALF), I32)]

    @functools.partial(
        pl.kernel,
        mesh=plsc.VectorSubcoreMesh(core_axis_name="c", subcore_axis_name="s"),
        out_type=jax.ShapeDtypeStruct((n_rows, HALF), I32),
        scratch_types=scratch,
        name="sc_dispatch",
    )
    def k(*refs):
        x_refs, idx_refs, o_hbm, bufs = refs[:ng], refs[ng:3 * ng], refs[3 * ng], refs[3 * ng + 1:]
        wid = _sc_worker_id()
        for g in range(ng):
            (_, nwin, win), _ = splits[g]
            x_hbm, d0_hbm, d1_hbm = x_refs[g], idx_refs[2 * g], idx_refs[2 * g + 1]
            i0_v, i1_v, rows_v = bufs[3 * g:3 * g + 3]

            @pl.loop(0, nwin)
            def _(j, nwin=nwin, win=win, x_hbm=x_hbm, d0_hbm=d0_hbm, d1_hbm=d1_hbm,
                  i0_v=i0_v, i1_v=i1_v, rows_v=rows_v):
                base = pl.multiple_of((wid * nwin + j) * win, 8)
                pltpu.sync_copy(d0_hbm.at[wid, j], i0_v)
                pltpu.sync_copy(d1_hbm.at[wid, j], i1_v)
                pltpu.sync_copy(x_hbm.at[pl.ds(base, win)], rows_v)
                pltpu.sync_copy(rows_v, o_hbm.at[i0_v])
                pltpu.sync_copy(rows_v, o_hbm.at[i1_v])

    idx_args = []
    for (s0, s1) in splits:
        idx_args += [s0[0], s1[0]]
    return k(*h2_groups, *idx_args)


def _collect_rows(ys, dest_groups):
    splits = [(_sc_split(d0), _sc_split(d1)) for d0, d1 in dest_groups]
    ng = len(dest_groups)
    outs, scratch = [], []
    for (d0, _), ((_, _, win), _) in zip(dest_groups, splits):
        o = jax.ShapeDtypeStruct((d0.shape[0], HALF), I32)
        outs += [o, o]
        scratch += [pltpu.VMEM((win,), I32), pltpu.VMEM((win, HALF), I32)]

    @functools.partial(
        pl.kernel,
        mesh=plsc.VectorSubcoreMesh(core_axis_name="c", subcore_axis_name="s"),
        out_type=tuple(outs),
        scratch_types=scratch,
        name="sc_collect",
    )
    def k(*refs):
        ys_hbm, idx_refs, out_refs, bufs = refs[0], refs[1:1 + 2 * ng], refs[1 + 2 * ng:1 + 4 * ng], refs[1 + 4 * ng:]
        wid = _sc_worker_id()
        for g in range(ng):
            (_, nwin, win), _ = splits[g]
            i_v, rows_v = bufs[2 * g:2 * g + 2]
            for kk in range(2):
                d_hbm, y_hbm = idx_refs[2 * g + kk], out_refs[2 * g + kk]

                @pl.loop(0, nwin)
                def _(j, nwin=nwin, win=win, d_hbm=d_hbm, y_hbm=y_hbm, i_v=i_v, rows_v=rows_v):
                    base = pl.multiple_of((wid * nwin + j) * win, 8)
                    pltpu.sync_copy(d_hbm.at[wid, j], i_v)
                    pltpu.sync_copy(ys_hbm.at[i_v], rows_v)
                    pltpu.sync_copy(rows_v, y_hbm.at[pl.ds(base, win)])

    idx_args = []
    for (s0, s1) in splits:
        idx_args += [s0[0], s1[0]]
    res = k(ys, *idx_args)
    return [(res[2 * g], res[2 * g + 1]) for g in range(ng)]


def _t5_bucket(rel):
    half = N_BUCKETS // 2
    max_exact = half // 2
    n = jnp.abs(rel)
    far = max_exact + (jnp.log(jnp.maximum(n, 1).astype(F32) / max_exact)
                       / math.log(MAX_DISTANCE / max_exact) * (half - max_exact)).astype(I32)
    far = jnp.minimum(far, half - 1)
    return jnp.where(rel > 0, half, 0) + jnp.where(n < max_exact, n, far)


def _bias_table(rel_bias, cq, nq, no_history):
    nk = WINDOW + cq
    j = jnp.arange(KEY_WIN)[:, None]
    c = jnp.arange(UNIT_Q)[None, :]
    jj = j - (c // cq) * cq
    valid = (jj >= 0) & (jj < nk) & (c < nq)
    if no_history:
        valid = valid & (j >= WINDOW)
    rel = jj - WINDOW - (c % cq)
    onehot = (_t5_bucket(rel)[:, :, None] == jnp.arange(N_BUCKETS)).astype(F32)
    bias = jnp.einsum("jcb,bh->jch", onehot, rel_bias.astype(F32), precision=lax.Precision.HIGHEST)
    bias = jnp.where(valid[:, :, None], bias * LOG2E, -jnp.inf)
    bias = jnp.transpose(bias.reshape(KEY_WIN, UNIT_Q, N_KV, GROUP), (2, 0, 3, 1))
    return bias.reshape(N_KV, KEY_WIN, GROUP * UNIT_Q)


def _sink_table(sinks):
    s = sinks.astype(F32).reshape(N_KV, 1, GROUP, 1)
    return jnp.broadcast_to(s * LOG2E, (N_KV, 1, GROUP, UNIT_Q)).reshape(N_KV, 1, GROUP * UNIT_Q)


def kernel(x_prompt, x_sample, state_attn_k, state_attn_v, state_conv, c_prompt, c_sample,
           rel_bias, w_ada, b_ada, norm1_w, w_in, q_norm_w, k_norm_w, attn_sinks, conv_w,
           w_out, norm2_w, w_router_group, b_router_group, w_router_expert, b_router_expert,
           w_gate, w_up, w_down):
    depth = w_ada.shape[0]
    assert depth == 1
    bp, tp, d = x_prompt.shape
    bs, ts, _ = x_sample.shape
    n_p, n_s = bp * tp, bs * ts
    n_tok = n_p + n_s
    l = 0

    wi = w_in[l]
    qw, kw, vw, rest = wi[:, :d], wi[:, d:d + KV_COLS], wi[:, d + KV_COLS:d + 2 * KV_COLS], wi[:, d + 2 * KV_COLS:]
    w_in_b = jnp.concatenate([qw, rest, kw, vw], axis=1).astype(BF16)
    w_out_b = w_out[l].astype(BF16)
    w_r = jnp.concatenate([w_router_group[l],
                           jnp.transpose(w_router_expert[l], (1, 0, 2)).reshape(d, N_EXPERTS),
                           jnp.zeros((d, LANES - N_GROUPS - N_EXPERTS), F32)], axis=1)
    w_r_hi = lax.reduce_precision(w_r, exponent_bits=8, mantissa_bits=7)
    w_r = jnp.concatenate([w_r_hi.astype(BF16), (w_r - w_r_hi).astype(BF16)], axis=1)
    b_r = jnp.concatenate([b_router_group[l], b_router_expert[l].reshape(-1),
                           jnp.zeros((LANES - N_GROUPS - N_EXPERTS,), F32)]).reshape(1, LANES)
    knw = jnp.tile(k_norm_w[l], N_KV).reshape(1, KV_COLS)
    qsc = jnp.tile(q_norm_w[l] * (HEAD_DIM ** -0.5 * LOG2E), N_KV).reshape(1, KV_COLS)
    n1w = norm1_w[l].reshape(1, d)
    n2w = norm2_w[l].reshape(1, d)

    mod = _ada(jnp.concatenate([c_prompt, c_sample], axis=0), w_ada[l], b_ada[l]).reshape(bp + bs, 6, d)
    mod_p, mod_s = mod[:bp], mod[bp:]

    xs_rows = x_sample.reshape(1, n_s, d)
    mod4_p = mod_p[:, :, None, :]
    mod4_s = jnp.repeat(jnp.transpose(mod_s, (1, 0, 2)), ts, axis=1)[None]
    proj_s = _inproj(xs_rows, mod4_s, n1w, w_in_b, n_s).reshape(bs, ts, IN_COLS)
    sink_tab = _sink_table(attn_sinks[l])
    bias_p = jnp.concatenate([_bias_table(rel_bias, CHUNK, UNIT_Q, False),
                              _bias_table(rel_bias, CHUNK, UNIT_Q, True)], axis=0)
    merged_p, k_p, v_p, c_p = _front(x_prompt, mod_p, n1w, w_in_b, knw, qsc, bias_p, sink_tab, conv_w[l],
                                     MIX_TILE, UNIT_Q)
    state = (state_attn_k[l].reshape(bs, WINDOW, KV_COLS), state_attn_v[l].reshape(bs, WINDOW, KV_COLS),
             state_conv[l])
    merged_s, k_s, v_s, c_s = _mixer(proj_s, knw, qsc, _bias_table(rel_bias, ts, ts, False), sink_tab, conv_w[l],
                                     state, ts, ts)

    x1_p, h2_p, e01_p, ew_p = _outproj(merged_p, x_prompt, mod4_p, w_out_b, n2w, w_r, b_r, ROW_TILE)
    x1_s, h2_s, e01_s, ew_s = _outproj(merged_s.reshape(1, n_s, d), xs_rows, mod4_s, w_out_b, n2w, w_r, b_r, n_s)
    x1_s = x1_s.reshape(bs, ts, d)

    assert n_s == ROW_TILE
    d01, totals = _rank(jnp.concatenate([e01_p, e01_s], axis=0), EXPERT_BLOCK)
    n_sub_p = n_p // ROW_TILE
    dests = [(d01[:n_sub_p, 0].reshape(-1), d01[:n_sub_p, 1].reshape(-1)),
             (d01[n_sub_p:, 0].reshape(-1), d01[n_sub_p:, 1].reshape(-1))]
    nblk = (totals[0, :N_EXPERTS] + EXPERT_BLOCK - 1) // EXPERT_BLOCK
    start_blk = (jnp.cumsum(nblk) - nblk).astype(I32)
    nb_max = -(-2 * n_tok // EXPERT_BLOCK) + N_EXPERTS

    xs = _dispatch_rows([h2_p, h2_s], dests, nb_max * EXPERT_BLOCK)
    ys = _experts(xs, start_blk, nblk.astype(I32), w_gate[l], w_up[l], w_down[l], EXPERT_BLOCK)
    (d0_p, d1_p), dest_s = dests
    nbp = bp // COLLECT_PARTS
    rows = nbp * tp
    y_p = None
    for part in range(COLLECT_PARTS):
        sl = slice(part * rows, (part + 1) * rows)
        groups = [(d0_p[sl], d1_p[sl])] + ([dest_s] if part == 0 else [])
        got = _collect_rows(ys, groups)
        if part == 0:
            y0_s, y1_s = got[1]
        y_p = _final(x1_p, got[0][0], got[0][1], ew_p, mod_p, ROW_TILE, part * nbp, nbp, y_p)
    y_s = _final(x1_s, y0_s, y1_s, ew_s, mod_s, ts)

    kv_shape = (1, -1, WINDOW, N_KV, HEAD_DIM)
    return (y_p, y_s, k_p.reshape(kv_shape), v_p.reshape(kv_shape), c_p[None],
            k_s.reshape(kv_shape), v_s.reshape(kv_shape), c_s[None])
```

```python
import functools
import math

import numpy as np
import jax
import jax.numpy as jnp
from jax import lax
from jax.experimental import pallas as pl
from jax.experimental.pallas import tpu as pltpu
from jax.experimental.pallas import tpu_sc as plsc

F32 = jnp.float32
BF16 = jnp.bfloat16
I32 = jnp.int32

D_MODEL = 1024
HEAD_DIM = 64
N_HEADS = 16
N_KV = 4
GROUP = 4
CHUNK = 64
WINDOW = 128
N_BUCKETS = 32
MAX_DISTANCE = 128
N_GROUPS = 8
EPG = 8
N_EXPERTS = 64
D_EXPERT = 512
RMS_EPS = 1e-6
LOG2E = math.log2(math.e)
SUM_ROWS = 16
KV_COLS = N_KV * HEAD_DIM
IN_COLS = 6 * D_MODEL + 2 * KV_COLS
HALF = D_MODEL // 2
LANES = 128

VMEM_LIMIT = 56 * 1024 * 1024
INPROJ_TN = 512
ROW_TILE = 512
MIX_TILE = 512
UNIT_Q = 2 * CHUNK
KEY_WIN = WINDOW + UNIT_Q
PROJ_CHUNK = 256
EXPERT_BLOCK = 512
COLLECT_PARTS = 4
EXPERT_BUFS = 4
SC_CORES = 2
SC_SUBCORES = 16
SC_WORKERS = SC_CORES * SC_SUBCORES
SC_MAX_WINDOW = 128


def _cparams(sem):
    return pltpu.CompilerParams(dimension_semantics=sem, vmem_limit_bytes=VMEM_LIMIT)


def _split_bf16(a):
    hi = a.astype(BF16)
    lo = (a - hi.astype(F32)).astype(BF16)
    return hi, lo


def _dot3(a, b):
    ah, al = _split_bf16(a)
    bh, bl = _split_bf16(b)
    d = functools.partial(jnp.dot, preferred_element_type=F32)
    return d(ah, bh) + (d(ah, bl) + d(al, bh))


def _sigmoid(x):
    return 0.5 * jnp.tanh(0.5 * x) + 0.5


def _pack_pairs(y):
    a = lax.bitcast_convert_type(y[:, :HALF].astype(BF16).astype(F32), I32)
    b = lax.bitcast_convert_type(y[:, HALF:].astype(BF16).astype(F32), I32)
    return a | lax.shift_right_logical(b, jnp.int32(16))


def _unpack_pairs(w):
    a = lax.bitcast_convert_type(w & jnp.int32(-65536), F32)
    b = lax.bitcast_convert_type(lax.shift_left(w, jnp.int32(16)), F32)
    return a, b


def _ada_kernel(c_ref, w_ref, b_ref, o_ref):
    c = c_ref[...]
    s = c * jax.nn.sigmoid(c)
    o_ref[...] = _dot3(s, w_ref[...]) + b_ref[...]


def _ada(c_all, w_ada, b_ada):
    r, d = c_all.shape
    n = w_ada.shape[1]
    tn = 1024
    return pl.pallas_call(
        _ada_kernel,
        out_shape=jax.ShapeDtypeStruct((r, n), F32),
        grid=(n // tn,),
        in_specs=[pl.BlockSpec((r, d), lambda j: (0, 0)),
                  pl.BlockSpec((d, tn), lambda j: (0, j)),
                  pl.BlockSpec((1, tn), lambda j: (0, j))],
        out_specs=pl.BlockSpec((r, tn), lambda j: (0, j)),
        compiler_params=_cparams(("arbitrary",)),
        name="ada",
    )(c_all, w_ada, b_ada.reshape(1, n))


def _inproj_kernel(x_ref, mod_ref, nw_ref, w_ref, o_ref):
    x = x_ref[0]
    mod = mod_ref[0]
    h = x * lax.rsqrt(jnp.mean(x * x, axis=-1, keepdims=True) + RMS_EPS) * nw_ref[...]
    h = h * (1.0 + mod[1]) + mod[0]
    hb = h.astype(BF16)
    for j in range(IN_COLS // INPROJ_TN):
        sl = slice(j * INPROJ_TN, (j + 1) * INPROJ_TN)
        o_ref[0, :, sl] = jnp.dot(hb, w_ref[:, sl], preferred_element_type=F32).astype(BF16)


def _inproj(x, mod4, nw, w_in_b, tm):
    b, t, d = x.shape
    mr = mod4.shape[2]
    assert mr == 1 or (mr == t and tm == t)
    return pl.pallas_call(
        _inproj_kernel,
        out_shape=jax.ShapeDtypeStruct((b, t, IN_COLS), BF16),
        grid=(b, t // tm),
        in_specs=[pl.BlockSpec((1, tm, d), lambda i, j: (i, j, 0)),
                  pl.BlockSpec((1, 6, mr, d), lambda i, j: (i, 0, 0, 0)),
                  pl.BlockSpec((1, d), lambda i, j: (0, 0)),
                  pl.BlockSpec((d, IN_COLS), lambda i, j: (0, 0), pipeline_mode=pl.Buffered(1))],
        out_specs=pl.BlockSpec((1, tm, IN_COLS), lambda i, j: (i, j, 0)),
        compiler_params=_cparams(("arbitrary", "arbitrary")),
        name="inproj",
    )(x, mod4, nw, w_in_b)


def _head_inv_rms(xf, bd, two_pass=True):
    sq = xf * xf
    if two_pass:
        hi, lo = _split_bf16(sq)
        ssq = jnp.dot(hi, bd, preferred_element_type=F32) + jnp.dot(lo, bd, preferred_element_type=F32)
    else:
        ssq = jnp.dot(sq.astype(BF16), bd, preferred_element_type=F32)
    return lax.rsqrt(ssq * (1.0 / HEAD_DIM) + RMS_EPS)


def _mixer_body(q_ref, k_ref, v_ref, bg_ref, c_ref, u_ref, ga_ref, gc_ref,
                knw_ref, qsc_ref, bd_ref, bias_ref, sink_ref, cw_ref,
                kpast_ref, vpast_ref, cpast_ref, upast_ref,
                merged_ref, knew_ref, vnew_ref, cnew_ref,
                kq_buf, vt_buf, attn_buf, *, tq, nq, stateful):
    hb = WINDOW
    pw = UNIT_Q
    t = pl.program_id(1)
    bd = bd_ref[...]

    k = k_ref[0].astype(F32)
    kn = k * _head_inv_rms(k, bd) * knw_ref[...]
    kq = (kn * qsc_ref[...]).astype(BF16)
    vb = v_ref[0]
    vt = vb.astype(F32).T.astype(BF16)

    if stateful:
        kp = kpast_ref[0]
        vp = vpast_ref[0]
        for kv in range(N_KV):
            kq_buf[kv, hb + tq:] = jnp.zeros((KEY_WIN - hb - tq, HEAD_DIM), BF16)
        vt_buf[:, hb + tq:] = jnp.zeros((KV_COLS, KEY_WIN - hb - tq), BF16)
        u_hist = jnp.concatenate([jnp.zeros((6, D_MODEL), F32), cpast_ref[0]], axis=0)
        knew_ref[0] = jnp.concatenate([kp[tq:], kn], axis=0)
        vnew_ref[0] = jnp.concatenate([vp[tq:], vb.astype(F32)], axis=0)
    else:
        kraw = kpast_ref[0].astype(F32)
        kp = kraw * _head_inv_rms(kraw, bd) * knw_ref[...]
        vp = vpast_ref[0].astype(F32)
        u_hist = jnp.where(t == 0, 0.0, cpast_ref[0].astype(F32) * upast_ref[0].astype(F32))
        knew_ref[0] = kn[tq - hb:]
        vnew_ref[0] = vb[tq - hb:].astype(F32)
    _fill_keys(kq_buf, vt_buf, kp, vp, kq, vt, qsc_ref, tq)
    q = q_ref[0]
    _attention_units(lambda kv: q[:, kv * KV_COLS:(kv + 1) * KV_COLS].astype(F32), kq_buf, vt_buf,
                     bias_ref, sink_ref, bd, attn_buf, tq, nq, None if stateful else t)
    cu = c_ref[0].astype(F32) * u_ref[0].astype(F32)
    conv, u_all = _short_conv(u_hist, cu, cw_ref[...], tq)
    cnew_ref[0] = u_all[tq + 6:tq + 8]
    merged = (_sigmoid(ga_ref[0].astype(F32)) * attn_buf[...]
              + _sigmoid(gc_ref[0].astype(F32)) * (bg_ref[0].astype(F32) * conv))
    merged_ref[0] = merged.astype(BF16)


def _fill_keys(kq_buf, vt_buf, kp, vp, kq, vt, qsc_ref, tq):
    hb = WINDOW
    kqp = (kp * qsc_ref[...]).astype(BF16)
    for kv in range(N_KV):
        kq_buf[kv, 0:hb] = kqp[:, kv * HEAD_DIM:(kv + 1) * HEAD_DIM]
        kq_buf[kv, hb:hb + tq] = kq[:, kv * HEAD_DIM:(kv + 1) * HEAD_DIM]
    vt_buf[:, 0:hb] = vp.T.astype(BF16)
    vt_buf[:, hb:hb + tq] = vt


def _attention_units(q_group, kq_buf, vt_buf, bias_ref, sink_ref, bd, attn_buf, tq, nq, t_first, after_unit=None):
    pw = UNIT_Q
    n_u = tq // nq
    ones_rows = jnp.ones((SUM_ROWS, KEY_WIN), BF16)
    qn_cache = {}

    def scores(i):
        kv, u = divmod(i, n_u)
        if kv not in qn_cache:
            qf = q_group(kv)
            qn_cache[kv] = (qf * _head_inv_rms(qf, bd, two_pass=False)).astype(BF16)
        qn = qn_cache[kv]
        r0 = u * nq
        parts = [qn[r0:r0 + nq, g * HEAD_DIM:(g + 1) * HEAD_DIM] for g in range(GROUP)]
        if nq < pw:
            zpad = jnp.zeros((pw - nq, HEAD_DIM), BF16)
            parts = [x for p_ in parts for x in (p_, zpad)]
        qs = jnp.concatenate(parts, axis=0)
        kw = kq_buf[kv, r0:r0 + KEY_WIN]
        return lax.dot_general(kw, qs, (((1,), (1,)), ((), ())), preferred_element_type=F32)

    def finish(i, st):
        kv, u = divmod(i, n_u)
        r0 = u * nq
        if t_first is not None and u == 0:
            bias = jnp.where(t_first == 0, bias_ref[kv + N_KV], bias_ref[kv])
        else:
            bias = bias_ref[kv]
        st = st + bias
        sink = sink_ref[kv]
        m = jnp.maximum(jnp.max(st, axis=0, keepdims=True), sink)
        p = jnp.exp2((st - m).astype(BF16))
        vt1 = jnp.concatenate([vt_buf[kv * HEAD_DIM:(kv + 1) * HEAD_DIM, r0:r0 + KEY_WIN], ones_rows], axis=0)
        pv = jnp.dot(vt1, p, preferred_element_type=F32)
        den = pv[HEAD_DIM:HEAD_DIM + 1] + jnp.exp2(sink - m)
        ot = pv[:HEAD_DIM] / den
        for gp in range(GROUP // 2):
            blk = jnp.concatenate([ot[:, (2 * gp) * pw:(2 * gp + 1) * pw],
                                   ot[:, (2 * gp + 1) * pw:(2 * gp + 2) * pw]], axis=0)
            c0 = (kv * GROUP + 2 * gp) * HEAD_DIM
            attn_buf[r0:r0 + nq, c0:c0 + 2 * HEAD_DIM] = blk.T[:nq]

    n_units = N_KV * n_u
    st = scores(0)
    for i in range(n_units):
        if after_unit is not None:
            after_unit(i)
        st_next = scores(i + 1) if i + 1 < n_units else None
        finish(i, st)
        st = st_next


def _short_conv(u_hist, cu, cw, tq):
    u_all = jnp.concatenate([u_hist, cu], axis=0)
    conv = cw[0:1] * u_all[6:6 + tq] + cw[1:2] * u_all[7:7 + tq] + cw[2:3] * u_all[8:8 + tq]
    return conv, u_all


def _front_body(x_ref, xp_ref, mod_ref, nw_ref, w_ref, knw_ref, qsc_ref, bd_ref, bias_ref, sink_ref, cw_ref,
                merged_ref, knew_ref, vnew_ref, cnew_ref,
                kq_buf, vt_buf, u_buf, attn_buf, pbuf, *, tq, nq):
    hb = WINDOW
    d = D_MODEL
    t = pl.program_id(1)
    bd = bd_ref[...]
    mod = mod_ref[0]

    @pl.when(t == 0)
    def _():
        u_buf[0] = jnp.zeros((8, d), F32)

    def modnorm(x):
        h = x * lax.rsqrt(jnp.mean(x * x, axis=-1, keepdims=True) + RMS_EPS) * nw_ref[...]
        return (h * (1.0 + mod[1:2]) + mod[0:1]).astype(BF16)

    def proj(hrows, c0, width):
        return jnp.dot(hrows, w_ref[:, c0:c0 + width], preferred_element_type=F32)

    hbf = modnorm(x_ref[0])
    kv_t = proj(hbf, 6 * d, 2 * KV_COLS)
    kv_p = proj(modnorm(xp_ref[0]), 6 * d, 2 * KV_COLS)
    k, v = kv_t[:, :KV_COLS], kv_t[:, KV_COLS:]
    kn = k * _head_inv_rms(k, bd) * knw_ref[...]
    kq = (kn * qsc_ref[...]).astype(BF16)
    kraw = kv_p[:, :KV_COLS]
    kp = kraw * _head_inv_rms(kraw, bd) * knw_ref[...]
    knew_ref[0] = kn[tq - hb:]
    vnew_ref[0] = v[tq - hb:]
    _fill_keys(kq_buf, vt_buf, kp, kv_p[:, KV_COLS:], kq, v.T.astype(BF16), qsc_ref, tq)

    n_units = N_KV * (tq // nq)
    rest_chunks = list(range(d, 6 * d, PROJ_CHUNK))

    def issue(c0, width):
        pbuf[:, c0:c0 + width] = proj(hbf, c0, width)

    issue(0, KV_COLS)
    n_rest = len(rest_chunks)

    def after_unit(i):
        kv, u = divmod(i, tq // nq)
        if u == 0 and kv + 1 < N_KV:
            issue((kv + 1) * KV_COLS, KV_COLS)
        for c0 in rest_chunks[i * n_rest // n_units:(i + 1) * n_rest // n_units]:
            issue(c0, PROJ_CHUNK)

    _attention_units(lambda kv: pbuf[:, kv * KV_COLS:(kv + 1) * KV_COLS], kq_buf, vt_buf,
                     bias_ref, sink_ref, bd, attn_buf, tq, nq, t, after_unit)

    cu = pbuf[:, 2 * d:3 * d] * pbuf[:, 3 * d:4 * d]
    conv, u_all = _short_conv(u_buf[t % 2], cu, cw_ref[...], tq)
    cnew_ref[0] = u_all[tq + 6:tq + 8]
    u_buf[(t + 1) % 2] = u_all[tq:tq + 8]
    merged = (_sigmoid(pbuf[:, 4 * d:5 * d]) * attn_buf[...]
              + _sigmoid(pbuf[:, 5 * d:6 * d]) * (pbuf[:, d:2 * d] * conv))
    merged_ref[0] = merged.astype(BF16)


def _front(x, mod, nw, w_in_b, knw, qsc, bias_tab, sink_tab, conv_w, tq, nq):
    b, t, d = x.shape
    r = np.arange(KV_COLS) // HEAD_DIM
    bd = jnp.asarray((r[:, None] == r[None, :]).astype(np.float32), BF16)
    const2 = lambda shp: pl.BlockSpec(shp, lambda i, s: (0, 0))
    const3 = lambda shp: pl.BlockSpec(shp, lambda i, s: (0, 0, 0))
    per_b = lambda shp: pl.BlockSpec(shp, lambda i, s: (i, 0, 0))
    kw_ = tq // WINDOW
    in_specs = [pl.BlockSpec((1, tq, d), lambda i, s: (i, s, 0)),
                pl.BlockSpec((1, WINDOW, d), lambda i, s: (i, jnp.maximum(s * kw_ - 1, 0), 0)),
                per_b((1, 6, d)), const2((1, d)),
                pl.BlockSpec((d, IN_COLS), lambda i, s: (0, 0), pipeline_mode=pl.Buffered(1)),
                const2((1, KV_COLS)), const2((1, KV_COLS)), const2((KV_COLS, KV_COLS)),
                const3(bias_tab.shape), const3(sink_tab.shape), const2((3, d))]
    out_shape = (jax.ShapeDtypeStruct((b, t, d), BF16),
                 jax.ShapeDtypeStruct((b, WINDOW, KV_COLS), F32),
                 jax.ShapeDtypeStruct((b, WINDOW, KV_COLS), F32),
                 jax.ShapeDtypeStruct((b, 2, d), F32))
    out_specs = (pl.BlockSpec((1, tq, d), lambda i, s: (i, s, 0)),
                 per_b((1, WINDOW, KV_COLS)), per_b((1, WINDOW, KV_COLS)), per_b((1, 2, d)))
    return pl.pallas_call(
        functools.partial(_front_body, tq=tq, nq=nq),
        out_shape=out_shape,
        grid=(b, t // tq),
        in_specs=in_specs,
        out_specs=out_specs,
        scratch_shapes=[pltpu.VMEM((N_KV, WINDOW + tq, HEAD_DIM), BF16),
                        pltpu.VMEM((KV_COLS, WINDOW + tq), BF16),
                        pltpu.VMEM((2, 8, d), F32),
                        pltpu.VMEM((tq, d), F32),
                        pltpu.VMEM((tq, 6 * d), F32)],
        compiler_params=_cparams(("arbitrary", "arbitrary")),
        name="front",
    )(x, x, mod, nw, w_in_b, knw, qsc, bd, bias_tab, sink_tab, conv_w)


def _mixer(proj, knw, qsc, bias_tab, sink_tab, conv_w, state, tq, nq):
    b, t, _ = proj.shape
    d = D_MODEL
    stateful = state is not None
    key_rows = max(WINDOW + tq, KEY_WIN)
    r = np.arange(KV_COLS) // HEAD_DIM
    bd = jnp.asarray((r[:, None] == r[None, :]).astype(np.float32), BF16)
    wide = lambda j: pl.BlockSpec((1, tq, d), lambda i, s, j=j: (i, s, j))
    kvspec = lambda j: pl.BlockSpec((1, tq, KV_COLS), lambda i, s, j=j: (i, s, j))
    const2 = lambda shp: pl.BlockSpec(shp, lambda i, s: (0, 0))
    const3 = lambda shp: pl.BlockSpec(shp, lambda i, s: (0, 0, 0))
    per_b = lambda shp: pl.BlockSpec(shp, lambda i, s: (i, 0, 0))
    kvblk = 6 * d // KV_COLS
    if stateful:
        hist_specs = [per_b((1, WINDOW, KV_COLS)), per_b((1, WINDOW, KV_COLS)), per_b((1, 2, d)), per_b((1, 2, d))]
        hist_args = [state[0], state[1], state[2], state[2]]
    else:
        kw_ = tq // WINDOW
        prev_kv = lambda j: pl.BlockSpec((1, WINDOW, KV_COLS),
                                         lambda i, s, j=j: (i, jnp.maximum(s * kw_ - 1, 0), j))
        prev8 = lambda j: pl.BlockSpec((1, 8, d), lambda i, s, j=j: (i, jnp.maximum(s * (tq // 8) - 1, 0), j))
        hist_specs = [prev_kv(kvblk), prev_kv(kvblk + 1), prev8(2), prev8(3)]
        hist_args = [proj, proj, proj, proj]
    in_specs = [wide(0), kvspec(kvblk), kvspec(kvblk + 1), wide(1), wide(2), wide(3), wide(4), wide(5),
                const2((1, KV_COLS)), const2((1, KV_COLS)), const2((KV_COLS, KV_COLS)),
                const3(bias_tab.shape), const3(sink_tab.shape), const2((3, d))] + hist_specs
    out_shape = (jax.ShapeDtypeStruct((b, t, d), BF16),
                 jax.ShapeDtypeStruct((b, WINDOW, KV_COLS), F32),
                 jax.ShapeDtypeStruct((b, WINDOW, KV_COLS), F32),
                 jax.ShapeDtypeStruct((b, 2, d), F32))
    out_specs = (pl.BlockSpec((1, tq, d), lambda i, s: (i, s, 0)),
                 per_b((1, WINDOW, KV_COLS)), per_b((1, WINDOW, KV_COLS)), per_b((1, 2, d)))
    return pl.pallas_call(
        functools.partial(_mixer_body, tq=tq, nq=nq, stateful=stateful),
        out_shape=out_shape,
        grid=(b, t // tq),
        in_specs=in_specs,
        out_specs=out_specs,
        scratch_shapes=[pltpu.VMEM((N_KV, key_rows, HEAD_DIM), BF16),
                        pltpu.VMEM((KV_COLS, key_rows), BF16),
                        pltpu.VMEM((tq, d), F32)],
        compiler_params=_cparams(("arbitrary", "arbitrary")),
        name="mixer_state" if stateful else "mixer",
    )(proj, proj, proj, proj, proj, proj, proj, proj, knw, qsc, bd, bias_tab, sink_tab, conv_w, *hist_args)


def _route(logits):
    lane = lax.broadcasted_iota(I32, logits.shape, 1).astype(F32)
    neg = -jnp.inf
    big = float(1 << 20)
    gl = jnp.where(lane < N_GROUPS, logits, neg)
    gmax = jnp.max(gl, axis=-1, keepdims=True)
    g_idx = jnp.min(jnp.where(gl == gmax, lane, big), axis=-1, keepdims=True)
    g_w = 1.0 / jnp.sum(jnp.exp(gl - gmax), axis=-1, keepdims=True)
    lo = N_GROUPS + g_idx * EPG
    el = jnp.where((lane >= lo) & (lane < lo + EPG), logits, neg)
    m1 = jnp.max(el, axis=-1, keepdims=True)
    i1 = jnp.min(jnp.where(el == m1, lane, big), axis=-1, keepdims=True)
    el2 = jnp.where(lane == i1, neg, el)
    m2 = jnp.max(el2, axis=-1, keepdims=True)
    i2 = jnp.min(jnp.where(el2 == m2, lane, big), axis=-1, keepdims=True)
    r = jnp.exp(m2 - m1)
    w1 = 1.0 / (1.0 + r)
    w2 = r / (1.0 + r)
    return i1 - N_GROUPS, i2 - N_GROUPS, g_w * w1, g_w * w2


def _outproj_kernel(m_ref, x_ref, mod_ref, wo_ref, nw_ref, wr_ref, br_ref,
                    x1_ref, h2_ref, e01_ref, ew_ref):
    mod = mod_ref[0]
    mix = jnp.dot(m_ref[0], wo_ref[...], preferred_element_type=F32)
    x1 = x_ref[0] + mod[2] * mix
    x1_ref[0] = x1
    h = x1 * lax.rsqrt(jnp.mean(x1 * x1, axis=-1, keepdims=True) + RMS_EPS) * nw_ref[...]
    h = h * (1.0 + mod[4]) + mod[3]
    h2_ref[...] = _pack_pairs(h)
    tm = h.shape[0]
    h_hi, h_lo = _split_bf16(h)
    prod = jnp.dot(jnp.concatenate([h_hi, h_lo], axis=0), wr_ref[...], preferred_element_type=F32)
    logits = prod[:tm, :LANES] + (prod[:tm, LANES:] + prod[tm:, :LANES]) + br_ref[...]
    e1, e2, w1, w2 = _route(logits)
    e01_ref[0] = jnp.concatenate([_col_to_row(e1), _col_to_row(e2)], axis=0).astype(I32)
    lane8 = lax.broadcasted_iota(I32, (h.shape[0], 8), 1)
    ew_ref[...] = jnp.where(lane8 == 0, w1, jnp.where(lane8 == 1, w2, 0.0))


def _outproj(merged, x, mod4, w_out_b, nw, w_r, b_r, tm):
    b, t, d = x.shape
    nt = t // tm
    mr = mod4.shape[2]
    assert mr == 1 or (mr == t and nt == 1)
    flat = lambda i, j: (i * nt + j, 0)
    return pl.pallas_call(
        _outproj_kernel,
        out_shape=(jax.ShapeDtypeStruct((b, t, d), F32),
                   jax.ShapeDtypeStruct((b * t, HALF), I32),
                   jax.ShapeDtypeStruct((b * nt, 2, tm), I32),
                   jax.ShapeDtypeStruct((b * t, 8), F32)),
        grid=(b, nt),
        in_specs=[pl.BlockSpec((1, tm, d), lambda i, j: (i, j, 0)),
                  pl.BlockSpec((1, tm, d), lambda i, j: (i, j, 0)),
                  pl.BlockSpec((1, 6, mr, d), lambda i, j: (i, 0, 0, 0)),
                  pl.BlockSpec((d, d), lambda i, j: (0, 0)),
                  pl.BlockSpec((1, d), lambda i, j: (0, 0)),
                  pl.BlockSpec((d, 2 * LANES), lambda i, j: (0, 0)),
                  pl.BlockSpec((1, LANES), lambda i, j: (0, 0))],
        out_specs=(pl.BlockSpec((1, tm, d), lambda i, j: (i, j, 0)),
                   pl.BlockSpec((tm, HALF), flat),
                   pl.BlockSpec((1, 2, tm), lambda i, j: (i * nt + j, 0, 0)),
                   pl.BlockSpec((tm, 8), flat)),
        compiler_params=_cparams(("arbitrary", "arbitrary")),
        name="outproj",
    )(merged, x, mod4, w_out_b, nw, w_r, b_r)


def _col_to_row(col):
    eye = lax.broadcasted_iota(I32, (LANES, LANES), 0) == lax.broadcasted_iota(I32, (LANES, LANES), 1)
    parts = [jnp.sum(jnp.where(eye, col[r * LANES:(r + 1) * LANES], 0.0), axis=0, keepdims=True)
             for r in range(col.shape[0] // LANES)]
    return jnp.concatenate(parts, axis=1)


def _rank_kernel(e_ref, tri_ref, low_ref, d_ref, tot_ref, *, block):
    n_sub, _, t = e_ref.shape
    sub = lax.broadcasted_iota(I32, (LANES, t), 0)

    def hots(s):
        e = e_ref[s]
        return sub == e[0:1], sub == e[1:2]

    def count(s, cnt):
        h0, h1 = hots(s)
        return cnt + jnp.sum(jnp.where(h0 | h1, 1.0, 0.0), axis=1, keepdims=True)

    cnt = lax.fori_loop(0, n_sub, count, jnp.zeros((LANES, 1), F32))
    tot_ref[...] = _col_to_row(cnt).astype(I32)
    nblk = jnp.floor((cnt + (block - 1)) * (1.0 / block))
    hi = jnp.floor(nblk * (1.0 / 16.0))
    lo = nblk - hi * 16.0
    low = low_ref[...]
    bcast = lambda c: jnp.broadcast_to(c, (LANES, LANES)).astype(BF16)
    excl = (jnp.dot(low, bcast(hi), preferred_element_type=F32) * 16.0
            + jnp.dot(low, bcast(lo), preferred_element_type=F32))
    starts = excl[:, 0:1] * float(block)

    def place(s, running):
        h0, h1 = hots(s)
        onehot = jnp.where(h0 | h1, 1.0, 0.0)
        prefix = jnp.dot(onehot.astype(BF16), tri_ref[...], preferred_element_type=F32)
        pos = prefix + running
        d0 = jnp.sum(jnp.where(h0, pos, 0.0), axis=0, keepdims=True)
        d1 = jnp.sum(jnp.where(h1, pos, 0.0), axis=0, keepdims=True)
        d_ref[s] = jnp.concatenate([d0, d1], axis=0).astype(I32)
        return running + jnp.sum(onehot, axis=1, keepdims=True)

    lax.fori_loop(0, n_sub, place, starts)


def _rank(e01, block):
    n_sub, _, t = e01.shape
    r = np.arange(t)
    tri = jnp.asarray((r[:, None] < r[None, :]).astype(np.float32), BF16)
    l = np.arange(LANES)
    low = jnp.asarray((l[None, :] < l[:, None]).astype(np.float32), BF16)
    return pl.pallas_call(
        functools.partial(_rank_kernel, block=block),
        out_shape=(jax.ShapeDtypeStruct((n_sub, 2, t), I32), jax.ShapeDtypeStruct((1, LANES), I32)),
        compiler_params=pltpu.CompilerParams(vmem_limit_bytes=VMEM_LIMIT),
        name="rank",
    )(e01, tri, low)


def _expert_kernel(start_ref, nblk_ref, xs_hbm, wg_ref, wu_ref, wd_ref, ys_hbm,
                   xbuf, ybuf, wg_s, wu_s, wd_s, sem_in, sem_out, *, block):
    nbuf = EXPERT_BUFS
    e = pl.program_id(0)
    n = nblk_ref[e]
    base = start_ref[e]
    total = start_ref[N_EXPERTS - 1] + nblk_ref[N_EXPERTS - 1]

    def in_copy(g):
        rows = pl.ds(pl.multiple_of(g * block, block), block)
        return pltpu.make_async_copy(xs_hbm.at[rows], xbuf.at[g % nbuf], sem_in.at[g % nbuf])

    def out_copy(g):
        rows = pl.ds(pl.multiple_of(g * block, block), block)
        return pltpu.make_async_copy(ybuf.at[g % nbuf], ys_hbm.at[rows], sem_out.at[g % nbuf])

    @pl.when(e == 0)
    def _():
        for g0 in range(nbuf - 1):
            @pl.when(g0 < total)
            def _(g0=g0):
                in_copy(g0).start()

    @pl.when(n > 0)
    def _():
        wg_s[...] = wg_ref[0].astype(BF16)
        wu_s[...] = wu_ref[0].astype(BF16)
        wd_s[...] = wd_ref[0].astype(BF16)

        def body(i, carry):
            g = base + i
            slot = g % nbuf
            in_copy(g).wait()

            @pl.when(g + nbuf - 1 < total)
            def _():
                in_copy(g + nbuf - 1).start()

            @pl.when(g >= nbuf)
            def _():
                out_copy(g - nbuf).wait()

            a, c = _unpack_pairs(xbuf[slot])
            x = jnp.concatenate([a.astype(BF16), c.astype(BF16)], axis=1)
            gate = jnp.dot(x, wg_s[...], preferred_element_type=F32)
            up = jnp.dot(x, wu_s[...], preferred_element_type=F32)
            hmid = (gate * _sigmoid(gate) * up).astype(BF16)
            ybuf[slot] = _pack_pairs(jnp.dot(hmid, wd_s[...], preferred_element_type=F32))
            out_copy(g).start()
            return carry

        lax.fori_loop(0, n, body, 0)

    @pl.when(e == N_EXPERTS - 1)
    def _():
        for back in range(nbuf, 0, -1):
            @pl.when(total >= back)
            def _(back=back):
                out_copy(total - back).wait()


def _experts(xs, start_blk, nblk, w_gate, w_up, w_down, block):
    n_rows = xs.shape[0]
    wblk = lambda e, st, nb: (e, 0, 0)
    grid_spec = pltpu.PrefetchScalarGridSpec(
        num_scalar_prefetch=2,
        grid=(N_EXPERTS,),
        in_specs=[pl.BlockSpec(memory_space=pl.ANY),
                  pl.BlockSpec((1, D_MODEL, D_EXPERT), wblk),
                  pl.BlockSpec((1, D_MODEL, D_EXPERT), wblk),
                  pl.BlockSpec((1, D_EXPERT, D_MODEL), wblk)],
        out_specs=pl.BlockSpec(memory_space=pl.ANY),
        scratch_shapes=[pltpu.VMEM((EXPERT_BUFS, block, HALF), I32),
                        pltpu.VMEM((EXPERT_BUFS, block, HALF), I32),
                        pltpu.VMEM((D_MODEL, D_EXPERT), BF16),
                        pltpu.VMEM((D_MODEL, D_EXPERT), BF16),
                        pltpu.VMEM((D_EXPERT, D_MODEL), BF16),
                        pltpu.SemaphoreType.DMA((EXPERT_BUFS,)),
                        pltpu.SemaphoreType.DMA((EXPERT_BUFS,))])
    return pl.pallas_call(
        functools.partial(_expert_kernel, block=block),
        out_shape=jax.ShapeDtypeStruct((n_rows, HALF), I32),
        grid_spec=grid_spec,
        compiler_params=_cparams(("arbitrary",)),
        name="experts",
    )(start_blk, nblk, xs, w_gate, w_up, w_down)


def _final_kernel(x1_ref, y0_ref, y1_ref, ew_ref, mod_ref, o_ref):
    a0, b0 = _unpack_pairs(y0_ref[...])
    a1, b1 = _unpack_pairs(y1_ref[...])
    w0 = ew_ref[:, 0:1]
    w1 = ew_ref[:, 1:2]
    moe = jnp.concatenate([w0 * a0 + w1 * a1, w0 * b0 + w1 * b1], axis=1)
    o_ref[0] = x1_ref[0] + mod_ref[0][5:6] * moe


def _final(x1, y0, y1, ew, mod, tm, b0=0, nb=None, y_prev=None):
    b, t, d = x1.shape
    nb = b if nb is None else nb
    nt = t // tm
    local = lambda i, j: (i * nt + j, 0)
    glob = lambda i, j: ((i + b0) * nt + j, 0)
    rows3 = lambda i, j: (i + b0, j, 0)
    in_specs = [pl.BlockSpec((1, tm, d), rows3),
                pl.BlockSpec((tm, HALF), local),
                pl.BlockSpec((tm, HALF), local),
                pl.BlockSpec((tm, 8), glob),
                pl.BlockSpec((1, 6, d), lambda i, j: (i + b0, 0, 0))]
    args = [x1, y0, y1, ew, mod]
    aliases = {}
    kern = _final_kernel
    if y_prev is not None:
        in_specs.append(pl.BlockSpec(memory_space=pl.ANY))
        args.append(y_prev)
        aliases = {5: 0}
        kern = lambda *refs: _final_kernel(*refs[:5], refs[6])
    return pl.pallas_call(
        kern,
        out_shape=jax.ShapeDtypeStruct((b, t, d), F32),
        grid=(nb, nt),
        in_specs=in_specs,
        out_specs=pl.BlockSpec((1, tm, d), rows3),
        input_output_aliases=aliases,
        compiler_params=_cparams(("arbitrary", "arbitrary")),
        name="final",
    )(*args)


def _sc_window(rows_per_worker):
    for w in range(SC_MAX_WINDOW, 7, -8):
        if rows_per_worker % w == 0:
            return w
    raise ValueError(f"no SparseCore window divides {rows_per_worker} rows per worker")


def _sc_split(idx):
    n = idx.shape[0]
    per = n // SC_WORKERS
    assert per * SC_WORKERS == n
    win = _sc_window(per)
    return idx.reshape(SC_WORKERS, per // win, win), per // win, win


def _sc_worker_id():
    return lax.axis_index("s") * SC_CORES + lax.axis_index("c")


def _dispatch_rows(h2_groups, dest_groups, n_rows):
    splits = [(_sc_split(d0), _sc_split(d1)) for d0, d1 in dest_groups]
    ng = len(h2_groups)
    scratch = []
    for (_, _, win), _ in splits:
        scratch += [pltpu.VMEM((win,), I32), pltpu.VMEM((win,), I32), pltpu.VMEM((win, HALF), I32)]

    @functools.partial(
        pl.kernel,
        mesh=plsc.VectorSubcoreMesh(core_axis_name="c", subcore_axis_name="s"),
        out_type=jax.ShapeDtypeStruct((n_rows, HALF), I32),
        scratch_types=scratch,
        name="sc_dispatch",
    )
    def k(*refs):
        x_refs, idx_refs, o_hbm, bufs = refs[:ng], refs[ng:3 * ng], refs[3 * ng], refs[3 * ng + 1:]
        wid = _sc_worker_id()
        for g in range(ng):
            (_, nwin, win), _ = splits[g]
            x_hbm, d0_hbm, d1_hbm = x_refs[g], idx_refs[2 * g], idx_refs[2 * g + 1]
            i0_v, i1_v, rows_v = bufs[3 * g:3 * g + 3]

            @pl.loop(0, nwin)
            def _(j, nwin=nwin, win=win, x_hbm=x_hbm, d0_hbm=d0_hbm, d1_hbm=d1_hbm,
                  i0_v=i0_v, i1_v=i1_v, rows_v=rows_v):
                base = pl.multiple_of((wid * nwin + j) * win, 8)
                pltpu.sync_copy(d0_hbm.at[wid, j], i0_v)
                pltpu.sync_copy(d1_hbm.at[wid, j], i1_v)
                pltpu.sync_copy(x_hbm.at[pl.ds(base, win)], rows_v)
                pltpu.sync_copy(rows_v, o_hbm.at[i0_v])
                pltpu.sync_copy(rows_v, o_hbm.at[i1_v])

    idx_args = []
    for (s0, s1) in splits:
        idx_args += [s0[0], s1[0]]
    return k(*h2_groups, *idx_args)


def _collect_rows(ys, dest_groups):
    splits = [(_sc_split(d0), _sc_split(d1)) for d0, d1 in dest_groups]
    ng = len(dest_groups)
    outs, scratch = [], []
    for (d0, _), ((_, _, win), _) in zip(dest_groups, splits):
        o = jax.ShapeDtypeStruct((d0.shape[0], HALF), I32)
        outs += [o, o]
        scratch += [pltpu.VMEM((win,), I32), pltpu.VMEM((win, HALF), I32)]

    @functools.partial(
        pl.kernel,
        mesh=plsc.VectorSubcoreMesh(core_axis_name="c", subcore_axis_name="s"),
        out_type=tuple(outs),
        scratch_types=scratch,
        name="sc_collect",
    )
    def k(*refs):
        ys_hbm, idx_refs, out_refs, bufs = refs[0], refs[1:1 + 2 * ng], refs[1 + 2 * ng:1 + 4 * ng], refs[1 + 4 * ng:]
        wid = _sc_worker_id()
        for g in range(ng):
            (_, nwin, win), _ = splits[g]
            i_v, rows_v = bufs[2 * g:2 * g + 2]
            for kk in range(2):
                d_hbm, y_hbm = idx_refs[2 * g + kk], out_refs[2 * g + kk]

                @pl.loop(0, nwin)
                def _(j, nwin=nwin, win=win, d_hbm=d_hbm, y_hbm=y_hbm, i_v=i_v, rows_v=rows_v):
                    base = pl.multiple_of((wid * nwin + j) * win, 8)
                    pltpu.sync_copy(d_hbm.at[wid, j], i_v)
                    pltpu.sync_copy(ys_hbm.at[i_v], rows_v)
                    pltpu.sync_copy(rows_v, y_hbm.at[pl.ds(base, win)])

    idx_args = []
    for (s0, s1) in splits:
        idx_args += [s0[0], s1[0]]
    res = k(ys, *idx_args)
    return [(res[2 * g], res[2 * g + 1]) for g in range(ng)]


def _t5_bucket(rel):
    half = N_BUCKETS // 2
    max_exact = half // 2
    n = jnp.abs(rel)
    far = max_exact + (jnp.log(jnp.maximum(n, 1).astype(F32) / max_exact)
                       / math.log(MAX_DISTANCE / max_exact) * (half - max_exact)).astype(I32)
    far = jnp.minimum(far, half - 1)
    return jnp.where(rel > 0, half, 0) + jnp.where(n < max_exact, n, far)


def _bias_table(rel_bias, cq, nq, no_history):
    nk = WINDOW + cq
    j = jnp.arange(KEY_WIN)[:, None]
    c = jnp.arange(UNIT_Q)[None, :]
    jj = j - (c // cq) * cq
    valid = (jj >= 0) & (jj < nk) & (c < nq)
    if no_history:
        valid = valid & (j >= WINDOW)
    rel = jj - WINDOW - (c % cq)
    onehot = (_t5_bucket(rel)[:, :, None] == jnp.arange(N_BUCKETS)).astype(F32)
    bias = jnp.einsum("jcb,bh->jch", onehot, rel_bias.astype(F32), precision=lax.Precision.HIGHEST)
    bias = jnp.where(valid[:, :, None], bias * LOG2E, -jnp.inf)
    bias = jnp.transpose(bias.reshape(KEY_WIN, UNIT_Q, N_KV, GROUP), (2, 0, 3, 1))
    return bias.reshape(N_KV, KEY_WIN, GROUP * UNIT_Q)


def _sink_table(sinks):
    s = sinks.astype(F32).reshape(N_KV, 1, GROUP, 1)
    return jnp.broadcast_to(s * LOG2E, (N_KV, 1, GROUP, UNIT_Q)).reshape(N_KV, 1, GROUP * UNIT_Q)


def kernel(x_prompt, x_sample, state_attn_k, state_attn_v, state_conv, c_prompt, c_sample,
           rel_bias, w_ada, b_ada, norm1_w, w_in, q_norm_w, k_norm_w, attn_sinks, conv_w,
           w_out, norm2_w, w_router_group, b_router_group, w_router_expert, b_router_expert,
           w_gate, w_up, w_down):
    depth = w_ada.shape[0]
    assert depth == 1
    bp, tp, d = x_prompt.shape
    bs, ts, _ = x_sample.shape
    n_p, n_s = bp * tp, bs * ts
    n_tok = n_p + n_s
    l = 0

    wi = w_in[l]
    qw, kw, vw, rest = wi[:, :d], wi[:, d:d + KV_COLS], wi[:, d + KV_COLS:d + 2 * KV_COLS], wi[:, d + 2 * KV_COLS:]
    w_in_b = jnp.concatenate([qw, rest, kw, vw], axis=1).astype(BF16)
    w_out_b = w_out[l].astype(BF16)
    w_r = jnp.concatenate([w_router_group[l],
                           jnp.transpose(w_router_expert[l], (1, 0, 2)).reshape(d, N_EXPERTS),
                           jnp.zeros((d, LANES - N_GROUPS - N_EXPERTS), F32)], axis=1)
    w_r_hi = lax.reduce_precision(w_r, exponent_bits=8, mantissa_bits=7)
    w_r = jnp.concatenate([w_r_hi.astype(BF16), (w_r - w_r_hi).astype(BF16)], axis=1)
    b_r = jnp.concatenate([b_router_group[l], b_router_expert[l].reshape(-1),
                           jnp.zeros((LANES - N_GROUPS - N_EXPERTS,), F32)]).reshape(1, LANES)
    knw = jnp.tile(k_norm_w[l], N_KV).reshape(1, KV_COLS)
    qsc = jnp.tile(q_norm_w[l] * (HEAD_DIM ** -0.5 * LOG2E), N_KV).reshape(1, KV_COLS)
    n1w = norm1_w[l].reshape(1, d)
    n2w = norm2_w[l].reshape(1, d)

    mod = _ada(jnp.concatenate([c_prompt, c_sample], axis=0), w_ada[l], b_ada[l]).reshape(bp + bs, 6, d)
    mod_p, mod_s = mod[:bp], mod[bp:]

    xs_rows = x_sample.reshape(1, n_s, d)
    mod4_p = mod_p[:, :, None, :]
    mod4_s = jnp.repeat(jnp.transpose(mod_s, (1, 0, 2)), ts, axis=1)[None]
    proj_s = _inproj(xs_rows, mod4_s, n1w, w_in_b, n_s).reshape(bs, ts, IN_COLS)
    sink_tab = _sink_table(attn_sinks[l])
    bias_p = jnp.concatenate([_bias_table(rel_bias, CHUNK, UNIT_Q, False),
                              _bias_table(rel_bias, CHUNK, UNIT_Q, True)], axis=0)
    merged_p, k_p, v_p, c_p = _front(x_prompt, mod_p, n1w, w_in_b, knw, qsc, bias_p, sink_tab, conv_w[l],
                                     MIX_TILE, UNIT_Q)
    state = (state_attn_k[l].reshape(bs, WINDOW, KV_COLS), state_attn_v[l].reshape(bs, WINDOW, KV_COLS),
             state_conv[l])
    merged_s, k_s, v_s, c_s = _mixer(proj_s, knw, qsc, _bias_table(rel_bias, ts, ts, False), sink_tab, conv_w[l],
                                     state, ts, ts)

    x1_p, h2_p, e01_p, ew_p = _outproj(merged_p, x_prompt, mod4_p, w_out_b, n2w, w_r, b_r, ROW_TILE)
    x1_s, h2_s, e01_s, ew_s = _outproj(merged_s.reshape(1, n_s, d), xs_rows, mod4_s, w_out_b, n2w, w_r, b_r, n_s)
    x1_s = x1_s.reshape(bs, ts, d)

    assert n_s == ROW_TILE
    d01, totals = _rank(jnp.concatenate([e01_p, e01_s], axis=0), EXPERT_BLOCK)
    n_sub_p = n_p // ROW_TILE
    dests = [(d01[:n_sub_p, 0].reshape(-1), d01[:n_sub_p, 1].reshape(-1)),
             (d01[n_sub_p:, 0].reshape(-1), d01[n_sub_p:, 1].reshape(-1))]
    nblk = (totals[0, :N_EXPERTS] + EXPERT_BLOCK - 1) // EXPERT_BLOCK
    start_blk = (jnp.cumsum(nblk) - nblk).astype(I32)
    nb_max = -(-2 * n_tok // EXPERT_BLOCK) + N_EXPERTS

    xs = _dispatch_rows([h2_p, h2_s], dests, nb_max * EXPERT_BLOCK)
    ys = _experts(xs, start_blk, nblk.astype(I32), w_gate[l], w_up[l], w_down[l], EXPERT_BLOCK)
    (d0_p, d1_p), dest_s = dests
    assert bp % COLLECT_PARTS == 0
    nbp = bp // COLLECT_PARTS
    rows = nbp * tp
    y_p = None
    for part in range(COLLECT_PARTS):
        sl = slice(part * rows, (part + 1) * rows)
        groups = [(d0_p[sl], d1_p[sl])] + ([dest_s] if part == 0 else [])
        got = _collect_rows(ys, groups)
        if part == 0:
            y0_s, y1_s = got[1]
        y_p = _final(x1_p, got[0][0], got[0][1], ew_p, mod_p, ROW_TILE, part * nbp, nbp, y_p)
    y_s = _final(x1_s, y0_s, y1_s, ew_s, mod_s, ts)

    kv_shape = (1, -1, WINDOW, N_KV, HEAD_DIM)
    return (y_p, y_s, k_p.reshape(kv_shape), v_p.reshape(kv_shape), c_p[None],
            k_s.reshape(kv_shape), v_s.reshape(kv_shape), c_s[None])
```

```python
import functools
import math

import numpy as np
import jax
import jax.numpy as jnp
from jax import lax
from jax.experimental import pallas as pl
from jax.experimental.pallas import tpu as pltpu
from jax.experimental.pallas import tpu_sc as plsc

F32 = jnp.float32
BF16 = jnp.bfloat16
I32 = jnp.int32

D_MODEL = 1024
HEAD_DIM = 64
N_HEADS = 16
N_KV = 4
GROUP = 4
CHUNK = 64
WINDOW = 128
N_BUCKETS = 32
MAX_DISTANCE = 128
N_GROUPS = 8
EPG = 8
N_EXPERTS = 64
D_EXPERT = 512
RMS_EPS = 1e-6
LOG2E = math.log2(math.e)
SCORE_LOOKAHEAD = 1
PROLOGUE_CHUNKS = 4
SUM_ROWS = 16
KV_COLS = N_KV * HEAD_DIM
IN_COLS = 6 * D_MODEL + 2 * KV_COLS
HALF = D_MODEL // 2
LANES = 128

VMEM_LIMIT = 56 * 1024 * 1024
INPROJ_TN = 512
ROW_TILE = 512
MIX_TILE = 512
UNIT_Q = 2 * CHUNK
KEY_WIN = WINDOW + UNIT_Q
PROJ_CHUNK = 256
EXPERT_BLOCK = 512
COLLECT_PARTS = 4
EXPERT_BUFS = 4
SC_CORES = 2
SC_SUBCORES = 16
SC_WORKERS = SC_CORES * SC_SUBCORES
SC_MAX_WINDOW = 128


def _cparams(sem):
    return pltpu.CompilerParams(dimension_semantics=sem, vmem_limit_bytes=VMEM_LIMIT)


def _split_bf16(a):
    hi = a.astype(BF16)
    lo = (a - hi.astype(F32)).astype(BF16)
    return hi, lo


def _dot3(a, b):
    ah, al = _split_bf16(a)
    bh, bl = _split_bf16(b)
    d = functools.partial(jnp.dot, preferred_element_type=F32)
    return d(ah, bh) + (d(ah, bl) + d(al, bh))


def _sigmoid(x):
    return 0.5 * jnp.tanh(0.5 * x) + 0.5


def _pack_pairs(y):
    a = lax.bitcast_convert_type(y[:, :HALF].astype(BF16).astype(F32), I32)
    b = lax.bitcast_convert_type(y[:, HALF:].astype(BF16).astype(F32), I32)
    return a | lax.shift_right_logical(b, jnp.int32(16))


def _unpack_pairs(w):
    a = lax.bitcast_convert_type(w & jnp.int32(-65536), F32)
    b = lax.bitcast_convert_type(lax.shift_left(w, jnp.int32(16)), F32)
    return a, b


def _ada_kernel(c_ref, w_ref, b_ref, o_ref):
    c = c_ref[...]
    s = c * jax.nn.sigmoid(c)
    o_ref[...] = _dot3(s, w_ref[...]) + b_ref[...]


def _ada(c_all, w_ada, b_ada):
    r, d = c_all.shape
    n = w_ada.shape[1]
    tn = 1024
    return pl.pallas_call(
        _ada_kernel,
        out_shape=jax.ShapeDtypeStruct((r, n), F32),
        grid=(n // tn,),
        in_specs=[pl.BlockSpec((r, d), lambda j: (0, 0)),
                  pl.BlockSpec((d, tn), lambda j: (0, j)),
                  pl.BlockSpec((1, tn), lambda j: (0, j))],
        out_specs=pl.BlockSpec((r, tn), lambda j: (0, j)),
        compiler_params=_cparams(("arbitrary",)),
        name="ada",
    )(c_all, w_ada, b_ada.reshape(1, n))


def _inproj_kernel(x_ref, mod_ref, nw_ref, w_ref, o_ref):
    x = x_ref[0]
    mod = mod_ref[0]
    h = x * lax.rsqrt(jnp.mean(x * x, axis=-1, keepdims=True) + RMS_EPS) * nw_ref[...]
    h = h * (1.0 + mod[1]) + mod[0]
    hb = h.astype(BF16)
    for j in range(IN_COLS // INPROJ_TN):
        sl = slice(j * INPROJ_TN, (j + 1) * INPROJ_TN)
        o_ref[0, :, sl] = jnp.dot(hb, w_ref[:, sl], preferred_element_type=F32).astype(BF16)


def _inproj(x, mod4, nw, w_in_b, tm):
    b, t, d = x.shape
    mr = mod4.shape[2]
    assert mr == 1 or (mr == t and tm == t)
    return pl.pallas_call(
        _inproj_kernel,
        out_shape=jax.ShapeDtypeStruct((b, t, IN_COLS), BF16),
        grid=(b, t // tm),
        in_specs=[pl.BlockSpec((1, tm, d), lambda i, j: (i, j, 0)),
                  pl.BlockSpec((1, 6, mr, d), lambda i, j: (i, 0, 0, 0)),
                  pl.BlockSpec((1, d), lambda i, j: (0, 0)),
                  pl.BlockSpec((d, IN_COLS), lambda i, j: (0, 0), pipeline_mode=pl.Buffered(1))],
        out_specs=pl.BlockSpec((1, tm, IN_COLS), lambda i, j: (i, j, 0)),
        compiler_params=_cparams(("arbitrary", "arbitrary")),
        name="inproj",
    )(x, mod4, nw, w_in_b)


def _head_inv_rms(xf, bd, two_pass=True):
    sq = xf * xf
    if two_pass:
        hi, lo = _split_bf16(sq)
        ssq = jnp.dot(hi, bd, preferred_element_type=F32) + jnp.dot(lo, bd, preferred_element_type=F32)
    else:
        ssq = jnp.dot(sq.astype(BF16), bd, preferred_element_type=F32)
    return lax.rsqrt(ssq * (1.0 / HEAD_DIM) + RMS_EPS)


def _mixer_body(q_ref, k_ref, v_ref, bg_ref, c_ref, u_ref, ga_ref, gc_ref,
                knw_ref, qsc_ref, bd_ref, bias_ref, sink_ref, cw_ref,
                kpast_ref, vpast_ref, cpast_ref, upast_ref,
                merged_ref, knew_ref, vnew_ref, cnew_ref,
                kq_buf, vt_buf, attn_buf, *, tq, nq, stateful):
    hb = WINDOW
    pw = UNIT_Q
    t = pl.program_id(1)
    bd = bd_ref[...]

    k = k_ref[0].astype(F32)
    kn = k * _head_inv_rms(k, bd) * knw_ref[...]
    kq = (kn * qsc_ref[...]).astype(BF16)
    vb = v_ref[0]
    vt = vb.astype(F32).T.astype(BF16)

    if stateful:
        kp = kpast_ref[0]
        vp = vpast_ref[0]
        for kv in range(N_KV):
            kq_buf[kv, hb + tq:] = jnp.zeros((KEY_WIN - hb - tq, HEAD_DIM), BF16)
        vt_buf[:, hb + tq:] = jnp.zeros((KV_COLS, KEY_WIN - hb - tq), BF16)
        u_hist = jnp.concatenate([jnp.zeros((6, D_MODEL), F32), cpast_ref[0]], axis=0)
        knew_ref[0] = jnp.concatenate([kp[tq:], kn], axis=0)
        vnew_ref[0] = jnp.concatenate([vp[tq:], vb.astype(F32)], axis=0)
    else:
        kraw = kpast_ref[0].astype(F32)
        kp = kraw * _head_inv_rms(kraw, bd) * knw_ref[...]
        vp = vpast_ref[0].astype(F32)
        u_hist = jnp.where(t == 0, 0.0, cpast_ref[0].astype(F32) * upast_ref[0].astype(F32))
        knew_ref[0] = kn[tq - hb:]
        vnew_ref[0] = vb[tq - hb:].astype(F32)
    _fill_keys(kq_buf, vt_buf, kp, vp, kq, vt, qsc_ref, tq)
    q = q_ref[0]
    _attention_units(lambda kv: q[:, kv * KV_COLS:(kv + 1) * KV_COLS].astype(F32), kq_buf, vt_buf,
                     bias_ref, sink_ref, bd, attn_buf, tq, nq, None if stateful else t)
    cu = c_ref[0].astype(F32) * u_ref[0].astype(F32)
    conv, u_all = _short_conv(u_hist, cu, cw_ref[...], tq)
    cnew_ref[0] = u_all[tq + 6:tq + 8]
    merged = (_sigmoid(ga_ref[0].astype(F32)) * attn_buf[...]
              + _sigmoid(gc_ref[0].astype(F32)) * (bg_ref[0].astype(F32) * conv))
    merged_ref[0] = merged.astype(BF16)


def _fill_keys(kq_buf, vt_buf, kp, vp, kq, vt, qsc_ref, tq):
    hb = WINDOW
    kqp = (kp * qsc_ref[...]).astype(BF16)
    for kv in range(N_KV):
        kq_buf[kv, 0:hb] = kqp[:, kv * HEAD_DIM:(kv + 1) * HEAD_DIM]
        kq_buf[kv, hb:hb + tq] = kq[:, kv * HEAD_DIM:(kv + 1) * HEAD_DIM]
    vt_buf[:, 0:hb] = vp.T.astype(BF16)
    vt_buf[:, hb:hb + tq] = vt


def _attention_units(q_group, kq_buf, vt_buf, bias_ref, sink_ref, bd, attn_buf, tq, nq, t_first, after_unit=None):
    pw = UNIT_Q
    n_u = tq // nq
    ones_rows = jnp.ones((SUM_ROWS, KEY_WIN), BF16)
    qn_cache = {}

    def scores(i):
        kv, u = divmod(i, n_u)
        if kv not in qn_cache:
            qf = q_group(kv)
            qn_cache[kv] = (qf * _head_inv_rms(qf, bd, two_pass=False)).astype(BF16)
        qn = qn_cache[kv]
        r0 = u * nq
        parts = [qn[r0:r0 + nq, g * HEAD_DIM:(g + 1) * HEAD_DIM] for g in range(GROUP)]
        if nq < pw:
            zpad = jnp.zeros((pw - nq, HEAD_DIM), BF16)
            parts = [x for p_ in parts for x in (p_, zpad)]
        qs = jnp.concatenate(parts, axis=0)
        kw = kq_buf[kv, r0:r0 + KEY_WIN]
        return lax.dot_general(kw, qs, (((1,), (1,)), ((), ())), preferred_element_type=F32)

    def softmax(i, st):
        kv, u = divmod(i, n_u)
        if t_first is not None and u == 0:
            bias = jnp.where(t_first == 0, bias_ref[kv + N_KV], bias_ref[kv])
        else:
            bias = bias_ref[kv]
        st = st + bias
        sink = sink_ref[kv]
        m = jnp.maximum(jnp.max(st, axis=0, keepdims=True), sink)
        return jnp.exp2((st - m).astype(BF16)), jnp.exp2(sink - m)

    def values(i, p, sink_p):
        kv, u = divmod(i, n_u)
        r0 = u * nq
        vt1 = jnp.concatenate([vt_buf[kv * HEAD_DIM:(kv + 1) * HEAD_DIM, r0:r0 + KEY_WIN], ones_rows], axis=0)
        pv = jnp.dot(vt1, p, preferred_element_type=F32)
        ot = pv[:HEAD_DIM] / (pv[HEAD_DIM:HEAD_DIM + 1] + sink_p)
        for gp in range(GROUP // 2):
            blk = jnp.concatenate([ot[:, (2 * gp) * pw:(2 * gp + 1) * pw],
                                   ot[:, (2 * gp + 1) * pw:(2 * gp + 2) * pw]], axis=0)
            c0 = (kv * GROUP + 2 * gp) * HEAD_DIM
            attn_buf[r0:r0 + nq, c0:c0 + 2 * HEAD_DIM] = blk.T[:nq]

    n_units = N_KV * n_u
    ahead = min(SCORE_LOOKAHEAD, n_units)
    queue = [scores(j) for j in range(ahead)]
    pending = None
    for i in range(n_units):
        if i + ahead < n_units:
            queue.append(scores(i + ahead))
        if after_unit is not None:
            after_unit(i)
        probs = softmax(i, queue.pop(0))
        if pending is not None:
            values(i - 1, *pending)
        pending = probs
    values(n_units - 1, *pending)


def _short_conv(u_hist, cu, cw, tq):
    u_all = jnp.concatenate([u_hist, cu], axis=0)
    conv = cw[0:1] * u_all[6:6 + tq] + cw[1:2] * u_all[7:7 + tq] + cw[2:3] * u_all[8:8 + tq]
    return conv, u_all


def _front_body(x_ref, xp_ref, mod_ref, nw_ref, w_ref, knw_ref, qsc_ref, bd_ref, bias_ref, sink_ref, cw_ref,
                merged_ref, knew_ref, vnew_ref, cnew_ref,
                kq_buf, vt_buf, u_buf, attn_buf, pbuf, *, tq, nq):
    hb = WINDOW
    d = D_MODEL
    t = pl.program_id(1)
    bd = bd_ref[...]
    mod = mod_ref[0]

    @pl.when(t == 0)
    def _():
        u_buf[0] = jnp.zeros((8, d), F32)

    def modnorm(x):
        h = x * lax.rsqrt(jnp.mean(x * x, axis=-1, keepdims=True) + RMS_EPS) * nw_ref[...]
        return (h * (1.0 + mod[1:2]) + mod[0:1]).astype(BF16)

    def proj(hrows, c0, width):
        return jnp.dot(hrows, w_ref[:, c0:c0 + width], preferred_element_type=F32)

    kv_p = proj(modnorm(xp_ref[0]), 6 * d, 2 * KV_COLS)
    hbf = modnorm(x_ref[0])
    kv_t = proj(hbf, 6 * d, 2 * KV_COLS)

    assert tq // nq >= 2
    n_units = N_KV * (tq // nq)
    rest_chunks = list(range(d, 6 * d, PROJ_CHUNK))

    def issue(c0, width):
        pbuf[:, c0:c0 + width] = proj(hbf, c0, width)

    issue(0, KV_COLS)
    for c0 in rest_chunks[:PROLOGUE_CHUNKS]:
        issue(c0, PROJ_CHUNK)
    rest_chunks = rest_chunks[PROLOGUE_CHUNKS:]
    n_rest = len(rest_chunks)

    k, v = kv_t[:, :KV_COLS], kv_t[:, KV_COLS:]
    kn = k * _head_inv_rms(k, bd) * knw_ref[...]
    kq = (kn * qsc_ref[...]).astype(BF16)
    kraw = kv_p[:, :KV_COLS]
    kp = kraw * _head_inv_rms(kraw, bd) * knw_ref[...]
    knew_ref[0] = kn[tq - hb:]
    vnew_ref[0] = v[tq - hb:]
    _fill_keys(kq_buf, vt_buf, kp, kv_p[:, KV_COLS:], kq, v.T.astype(BF16), qsc_ref, tq)

    def after_unit(i):
        kv, u = divmod(i, tq // nq)
        if u == 0 and kv + 1 < N_KV:
            issue((kv + 1) * KV_COLS, KV_COLS)
        for c0 in rest_chunks[i * n_rest // n_units:(i + 1) * n_rest // n_units]:
            issue(c0, PROJ_CHUNK)

    _attention_units(lambda kv: pbuf[:, kv * KV_COLS:(kv + 1) * KV_COLS], kq_buf, vt_buf,
                     bias_ref, sink_ref, bd, attn_buf, tq, nq, t, after_unit)

    cu = pbuf[:, 2 * d:3 * d] * pbuf[:, 3 * d:4 * d]
    conv, u_all = _short_conv(u_buf[t % 2], cu, cw_ref[...], tq)
    cnew_ref[0] = u_all[tq + 6:tq + 8]
    u_buf[(t + 1) % 2] = u_all[tq:tq + 8]
    merged = (_sigmoid(pbuf[:, 4 * d:5 * d]) * attn_buf[...]
              + _sigmoid(pbuf[:, 5 * d:6 * d]) * (pbuf[:, d:2 * d] * conv))
    merged_ref[0] = merged.astype(BF16)


def _front(x, mod, nw, w_in_b, knw, qsc, bias_tab, sink_tab, conv_w, tq, nq):
    b, t, d = x.shape
    r = np.arange(KV_COLS) // HEAD_DIM
    bd = jnp.asarray((r[:, None] == r[None, :]).astype(np.float32), BF16)
    const2 = lambda shp: pl.BlockSpec(shp, lambda i, s: (0, 0))
    const3 = lambda shp: pl.BlockSpec(shp, lambda i, s: (0, 0, 0))
    per_b = lambda shp: pl.BlockSpec(shp, lambda i, s: (i, 0, 0))
    kw_ = tq // WINDOW
    in_specs = [pl.BlockSpec((1, tq, d), lambda i, s: (i, s, 0)),
                pl.BlockSpec((1, WINDOW, d), lambda i, s: (i, jnp.maximum(s * kw_ - 1, 0), 0)),
                per_b((1, 6, d)), const2((1, d)),
                pl.BlockSpec((d, IN_COLS), lambda i, s: (0, 0), pipeline_mode=pl.Buffered(1)),
                const2((1, KV_COLS)), const2((1, KV_COLS)), const2((KV_COLS, KV_COLS)),
                const3(bias_tab.shape), const3(sink_tab.shape), const2((3, d))]
    out_shape = (jax.ShapeDtypeStruct((b, t, d), BF16),
                 jax.ShapeDtypeStruct((b, WINDOW, KV_COLS), F32),
                 jax.ShapeDtypeStruct((b, WINDOW, KV_COLS), F32),
                 jax.ShapeDtypeStruct((b, 2, d), F32))
    out_specs = (pl.BlockSpec((1, tq, d), lambda i, s: (i, s, 0)),
                 per_b((1, WINDOW, KV_COLS)), per_b((1, WINDOW, KV_COLS)), per_b((1, 2, d)))
    return pl.pallas_call(
        functools.partial(_front_body, tq=tq, nq=nq),
        out_shape=out_shape,
        grid=(b, t // tq),
        in_specs=in_specs,
        out_specs=out_specs,
        scratch_shapes=[pltpu.VMEM((N_KV, WINDOW + tq, HEAD_DIM), BF16),
                        pltpu.VMEM((KV_COLS, WINDOW + tq), BF16),
                        pltpu.VMEM((2, 8, d), F32),
                        pltpu.VMEM((tq, d), F32),
                        pltpu.VMEM((tq, 6 * d), F32)],
        compiler_params=_cparams(("arbitrary", "arbitrary")),
        name="front",
    )(x, x, mod, nw, w_in_b, knw, qsc, bd, bias_tab, sink_tab, conv_w)


def _mixer(proj, knw, qsc, bias_tab, sink_tab, conv_w, state, tq, nq):
    b, t, _ = proj.shape
    d = D_MODEL
    stateful = state is not None
    key_rows = max(WINDOW + tq, KEY_WIN)
    r = np.arange(KV_COLS) // HEAD_DIM
    bd = jnp.asarray((r[:, None] == r[None, :]).astype(np.float32), BF16)
    wide = lambda j: pl.BlockSpec((1, tq, d), lambda i, s, j=j: (i, s, j))
    kvspec = lambda j: pl.BlockSpec((1, tq, KV_COLS), lambda i, s, j=j: (i, s, j))
    const2 = lambda shp: pl.BlockSpec(shp, lambda i, s: (0, 0))
    const3 = lambda shp: pl.BlockSpec(shp, lambda i, s: (0, 0, 0))
    per_b = lambda shp: pl.BlockSpec(shp, lambda i, s: (i, 0, 0))
    kvblk = 6 * d // KV_COLS
    if stateful:
        hist_specs = [per_b((1, WINDOW, KV_COLS)), per_b((1, WINDOW, KV_COLS)), per_b((1, 2, d)), per_b((1, 2, d))]
        hist_args = [state[0], state[1], state[2], state[2]]
    else:
        kw_ = tq // WINDOW
        prev_kv = lambda j: pl.BlockSpec((1, WINDOW, KV_COLS),
                                         lambda i, s, j=j: (i, jnp.maximum(s * kw_ - 1, 0), j))
        prev8 = lambda j: pl.BlockSpec((1, 8, d), lambda i, s, j=j: (i, jnp.maximum(s * (tq // 8) - 1, 0), j))
        hist_specs = [prev_kv(kvblk), prev_kv(kvblk + 1), prev8(2), prev8(3)]
        hist_args = [proj, proj, proj, proj]
    in_specs = [wide(0), kvspec(kvblk), kvspec(kvblk + 1), wide(1), wide(2), wide(3), wide(4), wide(5),
                const2((1, KV_COLS)), const2((1, KV_COLS)), const2((KV_COLS, KV_COLS)),
                const3(bias_tab.shape), const3(sink_tab.shape), const2((3, d))] + hist_specs
    out_shape = (jax.ShapeDtypeStruct((b, t, d), BF16),
                 jax.ShapeDtypeStruct((b, WINDOW, KV_COLS), F32),
                 jax.ShapeDtypeStruct((b, WINDOW, KV_COLS), F32),
                 jax.ShapeDtypeStruct((b, 2, d), F32))
    out_specs = (pl.BlockSpec((1, tq, d), lambda i, s: (i, s, 0)),
                 per_b((1, WINDOW, KV_COLS)), per_b((1, WINDOW, KV_COLS)), per_b((1, 2, d)))
    return pl.pallas_call(
        functools.partial(_mixer_body, tq=tq, nq=nq, stateful=stateful),
        out_shape=out_shape,
        grid=(b, t // tq),
        in_specs=in_specs,
        out_specs=out_specs,
        scratch_shapes=[pltpu.VMEM((N_KV, key_rows, HEAD_DIM), BF16),
                        pltpu.VMEM((KV_COLS, key_rows), BF16),
                        pltpu.VMEM((tq, d), F32)],
        compiler_params=_cparams(("arbitrary", "arbitrary")),
        name="mixer_state" if stateful else "mixer",
    )(proj, proj, proj, proj, proj, proj, proj, proj, knw, qsc, bd, bias_tab, sink_tab, conv_w, *hist_args)


def _route(logits):
    lane = lax.broadcasted_iota(I32, logits.shape, 1).astype(F32)
    neg = -jnp.inf
    big = float(1 << 20)
    gl = jnp.where(lane < N_GROUPS, logits, neg)
    gmax = jnp.max(gl, axis=-1, keepdims=True)
    g_idx = jnp.min(jnp.where(gl == gmax, lane, big), axis=-1, keepdims=True)
    g_w = 1.0 / jnp.sum(jnp.exp(gl - gmax), axis=-1, keepdims=True)
    lo = N_GROUPS + g_idx * EPG
    el = jnp.where((lane >= lo) & (lane < lo + EPG), logits, neg)
    m1 = jnp.max(el, axis=-1, keepdims=True)
    i1 = jnp.min(jnp.where(el == m1, lane, big), axis=-1, keepdims=True)
    el2 = jnp.where(lane == i1, neg, el)
    m2 = jnp.max(el2, axis=-1, keepdims=True)
    i2 = jnp.min(jnp.where(el2 == m2, lane, big), axis=-1, keepdims=True)
    r = jnp.exp(m2 - m1)
    w1 = 1.0 / (1.0 + r)
    w2 = r / (1.0 + r)
    return i1 - N_GROUPS, i2 - N_GROUPS, g_w * w1, g_w * w2


def _outproj_kernel(m_ref, x_ref, mod_ref, wo_ref, nw_ref, wr_ref, br_ref,
                    x1_ref, h2_ref, e01_ref, ew_ref):
    mod = mod_ref[0]
    mix = jnp.dot(m_ref[0], wo_ref[...], preferred_element_type=F32)
    x1 = x_ref[0] + mod[2] * mix
    x1_ref[0] = x1
    h = x1 * lax.rsqrt(jnp.mean(x1 * x1, axis=-1, keepdims=True) + RMS_EPS) * nw_ref[...]
    h = h * (1.0 + mod[4]) + mod[3]
    h2_ref[...] = _pack_pairs(h)
    tm = h.shape[0]
    h_hi, h_lo = _split_bf16(h)
    prod = jnp.dot(jnp.concatenate([h_hi, h_lo], axis=0), wr_ref[...], preferred_element_type=F32)
    logits = prod[:tm, :LANES] + (prod[:tm, LANES:] + prod[tm:, :LANES]) + br_ref[...]
    e1, e2, w1, w2 = _route(logits)
    e01_ref[0] = jnp.concatenate([_col_to_row(e1), _col_to_row(e2)], axis=0).astype(I32)
    lane8 = lax.broadcasted_iota(I32, (h.shape[0], 8), 1)
    ew_ref[...] = jnp.where(lane8 == 0, w1, jnp.where(lane8 == 1, w2, 0.0))


def _outproj(merged, x, mod4, w_out_b, nw, w_r, b_r, tm):
    b, t, d = x.shape
    nt = t // tm
    mr = mod4.shape[2]
    assert mr == 1 or (mr == t and nt == 1)
    flat = lambda i, j: (i * nt + j, 0)
    return pl.pallas_call(
        _outproj_kernel,
        out_shape=(jax.ShapeDtypeStruct((b, t, d), F32),
                   jax.ShapeDtypeStruct((b * t, HALF), I32),
                   jax.ShapeDtypeStruct((b * nt, 2, tm), I32),
                   jax.ShapeDtypeStruct((b * t, 8), F32)),
        grid=(b, nt),
        in_specs=[pl.BlockSpec((1, tm, d), lambda i, j: (i, j, 0)),
                  pl.BlockSpec((1, tm, d), lambda i, j: (i, j, 0)),
                  pl.BlockSpec((1, 6, mr, d), lambda i, j: (i, 0, 0, 0)),
                  pl.BlockSpec((d, d), lambda i, j: (0, 0)),
                  pl.BlockSpec((1, d), lambda i, j: (0, 0)),
                  pl.BlockSpec((d, 2 * LANES), lambda i, j: (0, 0)),
                  pl.BlockSpec((1, LANES), lambda i, j: (0, 0))],
        out_specs=(pl.BlockSpec((1, tm, d), lambda i, j: (i, j, 0)),
                   pl.BlockSpec((tm, HALF), flat),
                   pl.BlockSpec((1, 2, tm), lambda i, j: (i * nt + j, 0, 0)),
                   pl.BlockSpec((tm, 8), flat)),
        compiler_params=_cparams(("arbitrary", "arbitrary")),
        name="outproj",
    )(merged, x, mod4, w_out_b, nw, w_r, b_r)


def _col_to_row(col):
    eye = lax.broadcasted_iota(I32, (LANES, LANES), 0) == lax.broadcasted_iota(I32, (LANES, LANES), 1)
    parts = [jnp.sum(jnp.where(eye, col[r * LANES:(r + 1) * LANES], 0.0), axis=0, keepdims=True)
             for r in range(col.shape[0] // LANES)]
    return jnp.concatenate(parts, axis=1)


def _rank_kernel(e_ref, tri_ref, low_ref, d_ref, tot_ref, *, block):
    n_sub, _, t = e_ref.shape
    sub = lax.broadcasted_iota(I32, (LANES, t), 0)

    def hots(s):
        e = e_ref[s]
        return sub == e[0:1], sub == e[1:2]

    def count(s, cnt):
        h0, h1 = hots(s)
        return cnt + jnp.sum(jnp.where(h0 | h1, 1.0, 0.0), axis=1, keepdims=True)

    cnt = lax.fori_loop(0, n_sub, count, jnp.zeros((LANES, 1), F32))
    tot_ref[...] = _col_to_row(cnt).astype(I32)
    nblk = jnp.floor((cnt + (block - 1)) * (1.0 / block))
    hi = jnp.floor(nblk * (1.0 / 16.0))
    lo = nblk - hi * 16.0
    low = low_ref[...]
    bcast = lambda c: jnp.broadcast_to(c, (LANES, LANES)).astype(BF16)
    excl = (jnp.dot(low, bcast(hi), preferred_element_type=F32) * 16.0
            + jnp.dot(low, bcast(lo), preferred_element_type=F32))
    starts = excl[:, 0:1] * float(block)

    def place(s, running):
        h0, h1 = hots(s)
        onehot = jnp.where(h0 | h1, 1.0, 0.0)
        prefix = jnp.dot(onehot.astype(BF16), tri_ref[...], preferred_element_type=F32)
        pos = prefix + running
        d0 = jnp.sum(jnp.where(h0, pos, 0.0), axis=0, keepdims=True)
        d1 = jnp.sum(jnp.where(h1, pos, 0.0), axis=0, keepdims=True)
        d_ref[s] = jnp.concatenate([d0, d1], axis=0).astype(I32)
        return running + jnp.sum(onehot, axis=1, keepdims=True)

    lax.fori_loop(0, n_sub, place, starts)


def _rank(e01, block):
    n_sub, _, t = e01.shape
    r = np.arange(t)
    tri = jnp.asarray((r[:, None] < r[None, :]).astype(np.float32), BF16)
    l = np.arange(LANES)
    low = jnp.asarray((l[None, :] < l[:, None]).astype(np.float32), BF16)
    return pl.pallas_call(
        functools.partial(_rank_kernel, block=block),
        out_shape=(jax.ShapeDtypeStruct((n_sub, 2, t), I32), jax.ShapeDtypeStruct((1, LANES), I32)),
        compiler_params=pltpu.CompilerParams(vmem_limit_bytes=VMEM_LIMIT),
        name="rank",
    )(e01, tri, low)


def _expert_kernel(start_ref, nblk_ref, xs_hbm, wg_ref, wu_ref, wd_ref, ys_hbm,
                   xbuf, ybuf, wg_s, wu_s, wd_s, sem_in, sem_out, *, block):
    nbuf = EXPERT_BUFS
    e = pl.program_id(0)
    n = nblk_ref[e]
    base = start_ref[e]
    total = start_ref[N_EXPERTS - 1] + nblk_ref[N_EXPERTS - 1]

    def in_copy(g):
        rows = pl.ds(pl.multiple_of(g * block, block), block)
        return pltpu.make_async_copy(xs_hbm.at[rows], xbuf.at[g % nbuf], sem_in.at[g % nbuf])

    def out_copy(g):
        rows = pl.ds(pl.multiple_of(g * block, block), block)
        return pltpu.make_async_copy(ybuf.at[g % nbuf], ys_hbm.at[rows], sem_out.at[g % nbuf])

    @pl.when(e == 0)
    def _():
        for g0 in range(nbuf - 1):
            @pl.when(g0 < total)
            def _(g0=g0):
                in_copy(g0).start()

    @pl.when(n > 0)
    def _():
        wg_s[...] = wg_ref[0].astype(BF16)
        wu_s[...] = wu_ref[0].astype(BF16)
        wd_s[...] = wd_ref[0].astype(BF16)

        def body(i, carry):
            g = base + i
            slot = g % nbuf
            in_copy(g).wait()

            @pl.when(g + nbuf - 1 < total)
            def _():
                in_copy(g + nbuf - 1).start()

            @pl.when(g >= nbuf)
            def _():
                out_copy(g - nbuf).wait()

            a, c = _unpack_pairs(xbuf[slot])
            x = jnp.concatenate([a.astype(BF16), c.astype(BF16)], axis=1)
            gate = jnp.dot(x, wg_s[...], preferred_element_type=F32)
            up = jnp.dot(x, wu_s[...], preferred_element_type=F32)
            hmid = (gate * _sigmoid(gate) * up).astype(BF16)
            ybuf[slot] = _pack_pairs(jnp.dot(hmid, wd_s[...], preferred_element_type=F32))
            out_copy(g).start()
            return carry

        lax.fori_loop(0, n, body, 0)

    @pl.when(e == N_EXPERTS - 1)
    def _():
        for back in range(nbuf, 0, -1):
            @pl.when(total >= back)
            def _(back=back):
                out_copy(total - back).wait()


def _experts(xs, start_blk, nblk, w_gate, w_up, w_down, block):
    n_rows = xs.shape[0]
    wblk = lambda e, st, nb: (e, 0, 0)
    grid_spec = pltpu.PrefetchScalarGridSpec(
        num_scalar_prefetch=2,
        grid=(N_EXPERTS,),
        in_specs=[pl.BlockSpec(memory_space=pl.ANY),
                  pl.BlockSpec((1, D_MODEL, D_EXPERT), wblk),
                  pl.BlockSpec((1, D_MODEL, D_EXPERT), wblk),
                  pl.BlockSpec((1, D_EXPERT, D_MODEL), wblk)],
        out_specs=pl.BlockSpec(memory_space=pl.ANY),
        scratch_shapes=[pltpu.VMEM((EXPERT_BUFS, block, HALF), I32),
                        pltpu.VMEM((EXPERT_BUFS, block, HALF), I32),
                        pltpu.VMEM((D_MODEL, D_EXPERT), BF16),
                        pltpu.VMEM((D_MODEL, D_EXPERT), BF16),
                        pltpu.VMEM((D_EXPERT, D_MODEL), BF16),
                        pltpu.SemaphoreType.DMA((EXPERT_BUFS,)),
                        pltpu.SemaphoreType.DMA((EXPERT_BUFS,))])
    return pl.pallas_call(
        functools.partial(_expert_kernel, block=block),
        out_shape=jax.ShapeDtypeStruct((n_rows, HALF), I32),
        grid_spec=grid_spec,
        compiler_params=_cparams(("arbitrary",)),
        name="experts",
    )(start_blk, nblk, xs, w_gate, w_up, w_down)


def _final_kernel(x1_ref, y0_ref, y1_ref, ew_ref, mod_ref, o_ref):
    a0, b0 = _unpack_pairs(y0_ref[...])
    a1, b1 = _unpack_pairs(y1_ref[...])
    w0 = ew_ref[:, 0:1]
    w1 = ew_ref[:, 1:2]
    moe = jnp.concatenate([w0 * a0 + w1 * a1, w0 * b0 + w1 * b1], axis=1)
    o_ref[0] = x1_ref[0] + mod_ref[0][5:6] * moe


def _final(x1, y0, y1, ew, mod, tm, b0=0, nb=None, y_prev=None):
    b, t, d = x1.shape
    nb = b if nb is None else nb
    nt = t // tm
    local = lambda i, j: (i * nt + j, 0)
    glob = lambda i, j: ((i + b0) * nt + j, 0)
    rows3 = lambda i, j: (i + b0, j, 0)
    in_specs = [pl.BlockSpec((1, tm, d), rows3),
                pl.BlockSpec((tm, HALF), local),
                pl.BlockSpec((tm, HALF), local),
                pl.BlockSpec((tm, 8), glob),
                pl.BlockSpec((1, 6, d), lambda i, j: (i + b0, 0, 0))]
    args = [x1, y0, y1, ew, mod]
    aliases = {}
    kern = _final_kernel
    if y_prev is not None:
        in_specs.append(pl.BlockSpec(memory_space=pl.ANY))
        args.append(y_prev)
        aliases = {5: 0}
        kern = lambda *refs: _final_kernel(*refs[:5], refs[6])
    return pl.pallas_call(
        kern,
        out_shape=jax.ShapeDtypeStruct((b, t, d), F32),
        grid=(nb, nt),
        in_specs=in_specs,
        out_specs=pl.BlockSpec((1, tm, d), rows3),
        input_output_aliases=aliases,
        compiler_params=_cparams(("arbitrary", "arbitrary")),
        name="final",
    )(*args)


def _sc_window(rows_per_worker):
    for w in range(SC_MAX_WINDOW, 7, -8):
        if rows_per_worker % w == 0:
            return w
    raise ValueError(f"no SparseCore window divides {rows_per_worker} rows per worker")


def _sc_split(idx):
    n = idx.shape[0]
    per = n // SC_WORKERS
    assert per * SC_WORKERS == n
    win = _sc_window(per)
    return idx.reshape(SC_WORKERS, per // win, win), per // win, win


def _sc_worker_id():
    return lax.axis_index("s") * SC_CORES + lax.axis_index("c")


def _dispatch_rows(h2_groups, dest_groups, n_rows):
    splits = [(_sc_split(d0), _sc_split(d1)) for d0, d1 in dest_groups]
    ng = len(h2_groups)
    scratch = []
    for (_, _, win), _ in splits:
        scratch += [pltpu.VMEM((win,), I32), pltpu.VMEM((win,), I32), pltpu.VMEM((win, HALF), I32)]

    @functools.partial(
        pl.kernel,
        mesh=plsc.VectorSubcoreMesh(core_axis_name="c", subcore_axis_name="s"),
        out_type=jax.ShapeDtypeStruct((n_rows, HALF), I32),
        scratch_types=scratch,
        name="sc_dispatch",
    )
    def k(*refs):
        x_refs, idx_refs, o_hbm, bufs = refs[:ng], refs[ng:3 * ng], refs[3 * ng], refs[3 * ng + 1:]
        wid = _sc_worker_id()
        for g in range(ng):
            (_, nwin, win), _ = splits[g]
            x_hbm, d0_hbm, d1_hbm = x_refs[g], idx_refs[2 * g], idx_refs[2 * g + 1]
            i0_v, i1_v, rows_v = bufs[3 * g:3 * g + 3]

            @pl.loop(0, nwin)
            def _(j, nwin=nwin, win=win, x_hbm=x_hbm, d0_hbm=d0_hbm, d1_hbm=d1_hbm,
                  i0_v=i0_v, i1_v=i1_v, rows_v=rows_v):
                base = pl.multiple_of((wid * nwin + j) * win, 8)
                pltpu.sync_copy(d0_hbm.at[wid, j], i0_v)
                pltpu.sync_copy(d1_hbm.at[wid, j], i1_v)
                pltpu.sync_copy(x_hbm.at[pl.ds(base, win)], rows_v)
                pltpu.sync_copy(rows_v, o_hbm.at[i0_v])
                pltpu.sync_copy(rows_v, o_hbm.at[i1_v])

    idx_args = []
    for (s0, s1) in splits:
        idx_args += [s0[0], s1[0]]
    return k(*h2_groups, *idx_args)


def _collect_rows(ys, dest_groups):
    splits = [(_sc_split(d0), _sc_split(d1)) for d0, d1 in dest_groups]
    ng = len(dest_groups)
    outs, scratch = [], []
    for (d0, _), ((_, _, win), _) in zip(dest_groups, splits):
        o = jax.ShapeDtypeStruct((d0.shape[0], HALF), I32)
        outs += [o, o]
        scratch += [pltpu.VMEM((win,), I32), pltpu.VMEM((win, HALF), I32)]

    @functools.partial(
        pl.kernel,
        mesh=plsc.VectorSubcoreMesh(core_axis_name="c", subcore_axis_name="s"),
        out_type=tuple(outs),
        scratch_types=scratch,
        name="sc_collect",
    )
    def k(*refs):
        ys_hbm, idx_refs, out_refs, bufs = refs[0], refs[1:1 + 2 * ng], refs[1 + 2 * ng:1 + 4 * ng], refs[1 + 4 * ng:]
        wid = _sc_worker_id()
        for g in range(ng):
            (_, nwin, win), _ = splits[g]
            i_v, rows_v = bufs[2 * g:2 * g + 2]
            for kk in range(2):
                d_hbm, y_hbm = idx_refs[2 * g + kk], out_refs[2 * g + kk]

                @pl.loop(0, nwin)
                def _(j, nwin=nwin, win=win, d_hbm=d_hbm, y_hbm=y_hbm, i_v=i_v, rows_v=rows_v):
                    base = pl.multiple_of((wid * nwin + j) * win, 8)
                    pltpu.sync_copy(d_hbm.at[wid, j], i_v)
                    pltpu.sync_copy(ys_hbm.at[i_v], rows_v)
                    pltpu.sync_copy(rows_v, y_hbm.at[pl.ds(base, win)])

    idx_args = []
    for (s0, s1) in splits:
        idx_args += [s0[0], s1[0]]
    res = k(ys, *idx_args)
    return [(res[2 * g], res[2 * g + 1]) for g in range(ng)]


def _t5_bucket(rel):
    half = N_BUCKETS // 2
    max_exact = half // 2
    n = jnp.abs(rel)
    far = max_exact + (jnp.log(jnp.maximum(n, 1).astype(F32) / max_exact)
                       / math.log(MAX_DISTANCE / max_exact) * (half - max_exact)).astype(I32)
    far = jnp.minimum(far, half - 1)
    return jnp.where(rel > 0, half, 0) + jnp.where(n < max_exact, n, far)


def _bias_table(rel_bias, cq, nq, no_history):
    nk = WINDOW + cq
    j = jnp.arange(KEY_WIN)[:, None]
    c = jnp.arange(UNIT_Q)[None, :]
    jj = j - (c // cq) * cq
    valid = (jj >= 0) & (jj < nk) & (c < nq)
    if no_history:
        valid = valid & (j >= WINDOW)
    rel = jj - WINDOW - (c % cq)
    onehot = (_t5_bucket(rel)[:, :, None] == jnp.arange(N_BUCKETS)).astype(F32)
    bias = jnp.einsum("jcb,bh->jch", onehot, rel_bias.astype(F32), precision=lax.Precision.HIGHEST)
    bias = jnp.where(valid[:, :, None], bias * LOG2E, -jnp.inf)
    bias = jnp.transpose(bias.reshape(KEY_WIN, UNIT_Q, N_KV, GROUP), (2, 0, 3, 1))
    return bias.reshape(N_KV, KEY_WIN, GROUP * UNIT_Q)


def _sink_table(sinks):
    s = sinks.astype(F32).reshape(N_KV, 1, GROUP, 1)
    return jnp.broadcast_to(s * LOG2E, (N_KV, 1, GROUP, UNIT_Q)).reshape(N_KV, 1, GROUP * UNIT_Q)


def kernel(x_prompt, x_sample, state_attn_k, state_attn_v, state_conv, c_prompt, c_sample,
           rel_bias, w_ada, b_ada, norm1_w, w_in, q_norm_w, k_norm_w, attn_sinks, conv_w,
           w_out, norm2_w, w_router_group, b_router_group, w_router_expert, b_router_expert,
           w_gate, w_up, w_down):
    depth = w_ada.shape[0]
    assert depth == 1
    bp, tp, d = x_prompt.shape
    bs, ts, _ = x_sample.shape
    n_p, n_s = bp * tp, bs * ts
    n_tok = n_p + n_s
    l = 0

    wi = w_in[l]
    qw, kw, vw, rest = wi[:, :d], wi[:, d:d + KV_COLS], wi[:, d + KV_COLS:d + 2 * KV_COLS], wi[:, d + 2 * KV_COLS:]
    w_in_b = jnp.concatenate([qw, rest, kw, vw], axis=1).astype(BF16)
    w_out_b = w_out[l].astype(BF16)
    w_r = jnp.concatenate([w_router_group[l],
                           jnp.transpose(w_router_expert[l], (1, 0, 2)).reshape(d, N_EXPERTS),
                           jnp.zeros((d, LANES - N_GROUPS - N_EXPERTS), F32)], axis=1)
    w_r_hi = lax.reduce_precision(w_r, exponent_bits=8, mantissa_bits=7)
    w_r = jnp.concatenate([w_r_hi.astype(BF16), (w_r - w_r_hi).astype(BF16)], axis=1)
    b_r = jnp.concatenate([b_router_group[l], b_router_expert[l].reshape(-1),
                           jnp.zeros((LANES - N_GROUPS - N_EXPERTS,), F32)]).reshape(1, LANES)
    knw = jnp.tile(k_norm_w[l], N_KV).reshape(1, KV_COLS)
    qsc = jnp.tile(q_norm_w[l] * (HEAD_DIM ** -0.5 * LOG2E), N_KV).reshape(1, KV_COLS)
    n1w = norm1_w[l].reshape(1, d)
    n2w = norm2_w[l].reshape(1, d)

    mod = _ada(jnp.concatenate([c_prompt, c_sample], axis=0), w_ada[l], b_ada[l]).reshape(bp + bs, 6, d)
    mod_p, mod_s = mod[:bp], mod[bp:]

    xs_rows = x_sample.reshape(1, n_s, d)
    mod4_p = mod_p[:, :, None, :]
    mod4_s = jnp.repeat(jnp.transpose(mod_s, (1, 0, 2)), ts, axis=1)[None]
    proj_s = _inproj(xs_rows, mod4_s, n1w, w_in_b, n_s).reshape(bs, ts, IN_COLS)
    sink_tab = _sink_table(attn_sinks[l])
    bias_p = jnp.concatenate([_bias_table(rel_bias, CHUNK, UNIT_Q, False),
                              _bias_table(rel_bias, CHUNK, UNIT_Q, True)], axis=0)
    merged_p, k_p, v_p, c_p = _front(x_prompt, mod_p, n1w, w_in_b, knw, qsc, bias_p, sink_tab, conv_w[l],
                                     MIX_TILE, UNIT_Q)
    state = (state_attn_k[l].reshape(bs, WINDOW, KV_COLS), state_attn_v[l].reshape(bs, WINDOW, KV_COLS),
             state_conv[l])
    merged_s, k_s, v_s, c_s = _mixer(proj_s, knw, qsc, _bias_table(rel_bias, ts, ts, False), sink_tab, conv_w[l],
                                     state, ts, ts)

    x1_p, h2_p, e01_p, ew_p = _outproj(merged_p, x_prompt, mod4_p, w_out_b, n2w, w_r, b_r, ROW_TILE)
    x1_s, h2_s, e01_s, ew_s = _outproj(merged_s.reshape(1, n_s, d), xs_rows, mod4_s, w_out_b, n2w, w_r, b_r, n_s)
    x1_s = x1_s.reshape(bs, ts, d)

    assert n_s == ROW_TILE
    d01, totals = _rank(jnp.concatenate([e01_p, e01_s], axis=0), EXPERT_BLOCK)
    n_sub_p = n_p // ROW_TILE
    dests = [(d01[:n_sub_p, 0].reshape(-1), d01[:n_sub_p, 1].reshape(-1)),
             (d01[n_sub_p:, 0].reshape(-1), d01[n_sub_p:, 1].reshape(-1))]
    nblk = (totals[0, :N_EXPERTS] + EXPERT_BLOCK - 1) // EXPERT_BLOCK
    start_blk = (jnp.cumsum(nblk) - nblk).astype(I32)
    nb_max = -(-2 * n_tok // EXPERT_BLOCK) + N_EXPERTS

    xs = _dispatch_rows([h2_p, h2_s], dests, nb_max * EXPERT_BLOCK)
    ys = _experts(xs, start_blk, nblk.astype(I32), w_gate[l], w_up[l], w_down[l], EXPERT_BLOCK)
    (d0_p, d1_p), dest_s = dests
    assert bp % COLLECT_PARTS == 0
    nbp = bp // COLLECT_PARTS
    rows = nbp * tp
    y_p = None
    for part in range(COLLECT_PARTS):
        sl = slice(part * rows, (part + 1) * rows)
        groups = [(d0_p[sl], d1_p[sl])] + ([dest_s] if part == 0 else [])
        got = _collect_rows(ys, groups)
        if part == 0:
            y0_s, y1_s = got[1]
        y_p = _final(x1_p, got[0][0], got[0][1], ew_p, mod_p, ROW_TILE, part * nbp, nbp, y_p)
    y_s = _final(x1_s, y0_s, y1_s, ew_s, mod_s, ts)

    kv_shape = (1, -1, WINDOW, N_KV, HEAD_DIM)
    return (y_p, y_s, k_p.reshape(kv_shape), v_p.reshape(kv_shape), c_p[None],
            k_s.reshape(kv_shape), v_s.reshape(kv_shape), c_s[None])
```

```python
import functools
import math

import numpy as np
import jax
import jax.numpy as jnp
from jax import lax
from jax.experimental import pallas as pl
from jax.experimental.pallas import tpu as pltpu
from jax.experimental.pallas import tpu_sc as plsc

F32 = jnp.float32
BF16 = jnp.bfloat16
I32 = jnp.int32

D_MODEL = 1024
HEAD_DIM = 64
N_HEADS = 16
N_KV = 4
GROUP = 4
CHUNK = 64
WINDOW = 128
N_BUCKETS = 32
MAX_DISTANCE = 128
N_GROUPS = 8
EPG = 8
N_EXPERTS = 64
D_EXPERT = 512
RMS_EPS = 1e-6
LOG2E = math.log2(math.e)
SCORE_LOOKAHEAD = 1
PROLOGUE_CHUNKS = 4
SUM_ROWS = 16
KV_COLS = N_KV * HEAD_DIM
IN_COLS = 6 * D_MODEL + 2 * KV_COLS
HALF = D_MODEL // 2
LANES = 128

VMEM_LIMIT = 56 * 1024 * 1024
INPROJ_TN = 512
ROW_TILE = 512
OUT_TILE = 1024
MIX_TILE = 512
UNIT_Q = 2 * CHUNK
KEY_WIN = WINDOW + UNIT_Q
PROJ_CHUNK = 256
EXPERT_BLOCK = 512
COLLECT_PARTS = 4
EXPERT_BUFS = 4
SC_CORES = 2
SC_SUBCORES = 16
SC_WORKERS = SC_CORES * SC_SUBCORES
SC_MAX_WINDOW = 128


def _cparams(sem):
    return pltpu.CompilerParams(dimension_semantics=sem, vmem_limit_bytes=VMEM_LIMIT)


def _split_bf16(a):
    hi = a.astype(BF16)
    lo = (a - hi.astype(F32)).astype(BF16)
    return hi, lo


def _dot3(a, b):
    ah, al = _split_bf16(a)
    bh, bl = _split_bf16(b)
    d = functools.partial(jnp.dot, preferred_element_type=F32)
    return d(ah, bh) + (d(ah, bl) + d(al, bh))


def _sigmoid(x):
    return 0.5 * jnp.tanh(0.5 * x) + 0.5


def _pack_pairs(y):
    a = lax.bitcast_convert_type(y[:, :HALF].astype(BF16).astype(F32), I32)
    b = lax.bitcast_convert_type(y[:, HALF:].astype(BF16).astype(F32), I32)
    return a | lax.shift_right_logical(b, jnp.int32(16))


def _unpack_pairs(w):
    a = lax.bitcast_convert_type(w & jnp.int32(-65536), F32)
    b = lax.bitcast_convert_type(lax.shift_left(w, jnp.int32(16)), F32)
    return a, b


def _ada_kernel(c_ref, w_ref, b_ref, o_ref):
    c = c_ref[...]
    s = c * jax.nn.sigmoid(c)
    o_ref[...] = _dot3(s, w_ref[...]) + b_ref[...]


def _ada(c_all, w_ada, b_ada):
    r, d = c_all.shape
    n = w_ada.shape[1]
    tn = 1024
    return pl.pallas_call(
        _ada_kernel,
        out_shape=jax.ShapeDtypeStruct((r, n), F32),
        grid=(n // tn,),
        in_specs=[pl.BlockSpec((r, d), lambda j: (0, 0)),
                  pl.BlockSpec((d, tn), lambda j: (0, j)),
                  pl.BlockSpec((1, tn), lambda j: (0, j))],
        out_specs=pl.BlockSpec((r, tn), lambda j: (0, j)),
        compiler_params=_cparams(("arbitrary",)),
        name="ada",
    )(c_all, w_ada, b_ada.reshape(1, n))


def _inproj_kernel(x_ref, mod_ref, nw_ref, w_ref, o_ref):
    x = x_ref[0]
    mod = mod_ref[0]
    h = x * lax.rsqrt(jnp.mean(x * x, axis=-1, keepdims=True) + RMS_EPS) * nw_ref[...]
    h = h * (1.0 + mod[1]) + mod[0]
    hb = h.astype(BF16)
    for j in range(IN_COLS // INPROJ_TN):
        sl = slice(j * INPROJ_TN, (j + 1) * INPROJ_TN)
        o_ref[0, :, sl] = jnp.dot(hb, w_ref[:, sl], preferred_element_type=F32).astype(BF16)


def _inproj(x, mod4, nw, w_in_b, tm):
    b, t, d = x.shape
    mr = mod4.shape[2]
    assert mr == 1 or (mr == t and tm == t)
    return pl.pallas_call(
        _inproj_kernel,
        out_shape=jax.ShapeDtypeStruct((b, t, IN_COLS), BF16),
        grid=(b, t // tm),
        in_specs=[pl.BlockSpec((1, tm, d), lambda i, j: (i, j, 0)),
                  pl.BlockSpec((1, 6, mr, d), lambda i, j: (i, 0, 0, 0)),
                  pl.BlockSpec((1, d), lambda i, j: (0, 0)),
                  pl.BlockSpec((d, IN_COLS), lambda i, j: (0, 0), pipeline_mode=pl.Buffered(1))],
        out_specs=pl.BlockSpec((1, tm, IN_COLS), lambda i, j: (i, j, 0)),
        compiler_params=_cparams(("arbitrary", "arbitrary")),
        name="inproj",
    )(x, mod4, nw, w_in_b)


def _head_inv_rms(xf, bd, two_pass=True):
    sq = xf * xf
    if two_pass:
        hi, lo = _split_bf16(sq)
        ssq = jnp.dot(hi, bd, preferred_element_type=F32) + jnp.dot(lo, bd, preferred_element_type=F32)
    else:
        ssq = jnp.dot(sq.astype(BF16), bd, preferred_element_type=F32)
    return lax.rsqrt(ssq * (1.0 / HEAD_DIM) + RMS_EPS)


def _mixer_body(q_ref, k_ref, v_ref, bg_ref, c_ref, u_ref, ga_ref, gc_ref,
                knw_ref, qsc_ref, bd_ref, bias_ref, sink_ref, cw_ref,
                kpast_ref, vpast_ref, cpast_ref, upast_ref,
                merged_ref, knew_ref, vnew_ref, cnew_ref,
                kq_buf, vt_buf, attn_buf, *, tq, nq, stateful):
    hb = WINDOW
    pw = UNIT_Q
    t = pl.program_id(1)
    bd = bd_ref[...]

    k = k_ref[0].astype(F32)
    kn = k * _head_inv_rms(k, bd) * knw_ref[...]
    kq = (kn * qsc_ref[...]).astype(BF16)
    vb = v_ref[0]
    vt = vb.astype(F32).T.astype(BF16)

    if stateful:
        kp = kpast_ref[0]
        vp = vpast_ref[0]
        for kv in range(N_KV):
            kq_buf[kv, hb + tq:] = jnp.zeros((KEY_WIN - hb - tq, HEAD_DIM), BF16)
        vt_buf[:, hb + tq:] = jnp.zeros((KV_COLS, KEY_WIN - hb - tq), BF16)
        u_hist = jnp.concatenate([jnp.zeros((6, D_MODEL), F32), cpast_ref[0]], axis=0)
        knew_ref[0] = jnp.concatenate([kp[tq:], kn], axis=0)
        vnew_ref[0] = jnp.concatenate([vp[tq:], vb.astype(F32)], axis=0)
    else:
        kraw = kpast_ref[0].astype(F32)
        kp = kraw * _head_inv_rms(kraw, bd) * knw_ref[...]
        vp = vpast_ref[0].astype(F32)
        u_hist = jnp.where(t == 0, 0.0, cpast_ref[0].astype(F32) * upast_ref[0].astype(F32))
        knew_ref[0] = kn[tq - hb:]
        vnew_ref[0] = vb[tq - hb:].astype(F32)
    _fill_keys(kq_buf, vt_buf, kp, vp, kq, vt, qsc_ref, tq)
    q = q_ref[0]
    _attention_units(lambda kv: q[:, kv * KV_COLS:(kv + 1) * KV_COLS].astype(F32), kq_buf, vt_buf,
                     bias_ref, sink_ref, bd, attn_buf, tq, nq, None if stateful else t)
    cu = c_ref[0].astype(F32) * u_ref[0].astype(F32)
    conv, u_all = _short_conv(u_hist, cu, cw_ref[...], tq)
    cnew_ref[0] = u_all[tq + 6:tq + 8]
    merged = (_sigmoid(ga_ref[0].astype(F32)) * attn_buf[...]
              + _sigmoid(gc_ref[0].astype(F32)) * (bg_ref[0].astype(F32) * conv))
    merged_ref[0] = merged.astype(BF16)


def _fill_keys(kq_buf, vt_buf, kp, vp, kq, vt, qsc_ref, tq):
    hb = WINDOW
    kqp = (kp * qsc_ref[...]).astype(BF16)
    for kv in range(N_KV):
        kq_buf[kv, 0:hb] = kqp[:, kv * HEAD_DIM:(kv + 1) * HEAD_DIM]
        kq_buf[kv, hb:hb + tq] = kq[:, kv * HEAD_DIM:(kv + 1) * HEAD_DIM]
    vt_buf[:, 0:hb] = vp.T.astype(BF16)
    vt_buf[:, hb:hb + tq] = vt


def _attention_units(q_group, kq_buf, vt_buf, bias_ref, sink_ref, bd, attn_buf, tq, nq, t_first, after_unit=None):
    pw = UNIT_Q
    n_u = tq // nq
    ones_rows = jnp.ones((SUM_ROWS, KEY_WIN), BF16)
    qn_cache = {}

    def scores(i):
        kv, u = divmod(i, n_u)
        if kv not in qn_cache:
            qf = q_group(kv)
            qn_cache[kv] = (qf * _head_inv_rms(qf, bd, two_pass=False)).astype(BF16)
        qn = qn_cache[kv]
        r0 = u * nq
        parts = [qn[r0:r0 + nq, g * HEAD_DIM:(g + 1) * HEAD_DIM] for g in range(GROUP)]
        if nq < pw:
            zpad = jnp.zeros((pw - nq, HEAD_DIM), BF16)
            parts = [x for p_ in parts for x in (p_, zpad)]
        qs = jnp.concatenate(parts, axis=0)
        kw = kq_buf[kv, r0:r0 + KEY_WIN]
        return lax.dot_general(kw, qs, (((1,), (1,)), ((), ())), preferred_element_type=F32)

    def softmax(i, st):
        kv, u = divmod(i, n_u)
        if t_first is not None and u == 0:
            bias = jnp.where(t_first == 0, bias_ref[kv + N_KV], bias_ref[kv])
        else:
            bias = bias_ref[kv]
        st = st + bias
        sink = sink_ref[kv]
        m = jnp.maximum(jnp.max(st, axis=0, keepdims=True), sink)
        return jnp.exp2((st - m).astype(BF16)), jnp.exp2(sink - m)

    def values(i, p, sink_p):
        kv, u = divmod(i, n_u)
        r0 = u * nq
        vt1 = jnp.concatenate([vt_buf[kv * HEAD_DIM:(kv + 1) * HEAD_DIM, r0:r0 + KEY_WIN], ones_rows], axis=0)
        pv = jnp.dot(vt1, p, preferred_element_type=F32)
        ot = pv[:HEAD_DIM] / (pv[HEAD_DIM:HEAD_DIM + 1] + sink_p)
        for gp in range(GROUP // 2):
            blk = jnp.concatenate([ot[:, (2 * gp) * pw:(2 * gp + 1) * pw],
                                   ot[:, (2 * gp + 1) * pw:(2 * gp + 2) * pw]], axis=0)
            c0 = (kv * GROUP + 2 * gp) * HEAD_DIM
            attn_buf[r0:r0 + nq, c0:c0 + 2 * HEAD_DIM] = blk.T[:nq]

    n_units = N_KV * n_u
    ahead = min(SCORE_LOOKAHEAD, n_units)
    queue = [scores(j) for j in range(ahead)]
    pending = None
    for i in range(n_units):
        if i + ahead < n_units:
            queue.append(scores(i + ahead))
        if after_unit is not None:
            after_unit(i)
        probs = softmax(i, queue.pop(0))
        if pending is not None:
            values(i - 1, *pending)
        pending = probs
    values(n_units - 1, *pending)


def _short_conv(u_hist, cu, cw, tq):
    u_all = jnp.concatenate([u_hist, cu], axis=0)
    conv = cw[0:1] * u_all[6:6 + tq] + cw[1:2] * u_all[7:7 + tq] + cw[2:3] * u_all[8:8 + tq]
    return conv, u_all


def _front_body(x_ref, xp_ref, mod_ref, nw_ref, w_ref, knw_ref, qsc_ref, bd_ref, bias_ref, sink_ref, cw_ref,
                merged_ref, knew_ref, vnew_ref, cnew_ref,
                kq_buf, vt_buf, u_buf, attn_buf, pbuf, *, tq, nq):
    hb = WINDOW
    d = D_MODEL
    t = pl.program_id(1)
    bd = bd_ref[...]
    mod = mod_ref[0]

    @pl.when(t == 0)
    def _():
        u_buf[0] = jnp.zeros((8, d), F32)

    def modnorm(x):
        h = x * lax.rsqrt(jnp.mean(x * x, axis=-1, keepdims=True) + RMS_EPS) * nw_ref[...]
        return (h * (1.0 + mod[1:2]) + mod[0:1]).astype(BF16)

    def proj(hrows, c0, width):
        return jnp.dot(hrows, w_ref[:, c0:c0 + width], preferred_element_type=F32)

    kv_p = proj(modnorm(xp_ref[0]), 6 * d, 2 * KV_COLS)
    hbf = modnorm(x_ref[0])
    kv_t = proj(hbf, 6 * d, 2 * KV_COLS)

    assert tq // nq >= 2
    n_units = N_KV * (tq // nq)
    rest_chunks = list(range(d, 6 * d, PROJ_CHUNK))

    def issue(c0, width):
        pbuf[:, c0:c0 + width] = proj(hbf, c0, width)

    issue(0, KV_COLS)
    for c0 in rest_chunks[:PROLOGUE_CHUNKS]:
        issue(c0, PROJ_CHUNK)
    rest_chunks = rest_chunks[PROLOGUE_CHUNKS:]
    n_rest = len(rest_chunks)

    k, v = kv_t[:, :KV_COLS], kv_t[:, KV_COLS:]
    kn = k * _head_inv_rms(k, bd) * knw_ref[...]
    kq = (kn * qsc_ref[...]).astype(BF16)
    kraw = kv_p[:, :KV_COLS]
    kp = kraw * _head_inv_rms(kraw, bd) * knw_ref[...]
    knew_ref[0] = kn[tq - hb:]
    vnew_ref[0] = v[tq - hb:]
    _fill_keys(kq_buf, vt_buf, kp, kv_p[:, KV_COLS:], kq, v.T.astype(BF16), qsc_ref, tq)

    def after_unit(i):
        kv, u = divmod(i, tq // nq)
        if u == 0 and kv + 1 < N_KV:
            issue((kv + 1) * KV_COLS, KV_COLS)
        for c0 in rest_chunks[i * n_rest // n_units:(i + 1) * n_rest // n_units]:
            issue(c0, PROJ_CHUNK)

    _attention_units(lambda kv: pbuf[:, kv * KV_COLS:(kv + 1) * KV_COLS], kq_buf, vt_buf,
                     bias_ref, sink_ref, bd, attn_buf, tq, nq, t, after_unit)

    cu = pbuf[:, 2 * d:3 * d] * pbuf[:, 3 * d:4 * d]
    conv, u_all = _short_conv(u_buf[t % 2], cu, cw_ref[...], tq)
    cnew_ref[0] = u_all[tq + 6:tq + 8]
    u_buf[(t + 1) % 2] = u_all[tq:tq + 8]
    merged = (_sigmoid(pbuf[:, 4 * d:5 * d]) * attn_buf[...]
              + _sigmoid(pbuf[:, 5 * d:6 * d]) * (pbuf[:, d:2 * d] * conv))
    merged_ref[0] = merged.astype(BF16)


def _front(x, mod, nw, w_in_b, knw, qsc, bias_tab, sink_tab, conv_w, tq, nq):
    b, t, d = x.shape
    r = np.arange(KV_COLS) // HEAD_DIM
    bd = jnp.asarray((r[:, None] == r[None, :]).astype(np.float32), BF16)
    const2 = lambda shp: pl.BlockSpec(shp, lambda i, s: (0, 0))
    const3 = lambda shp: pl.BlockSpec(shp, lambda i, s: (0, 0, 0))
    per_b = lambda shp: pl.BlockSpec(shp, lambda i, s: (i, 0, 0))
    kw_ = tq // WINDOW
    in_specs = [pl.BlockSpec((1, tq, d), lambda i, s: (i, s, 0)),
                pl.BlockSpec((1, WINDOW, d), lambda i, s: (i, jnp.maximum(s * kw_ - 1, 0), 0)),
                per_b((1, 6, d)), const2((1, d)),
                pl.BlockSpec((d, IN_COLS), lambda i, s: (0, 0), pipeline_mode=pl.Buffered(1)),
                const2((1, KV_COLS)), const2((1, KV_COLS)), const2((KV_COLS, KV_COLS)),
                const3(bias_tab.shape), const3(sink_tab.shape), const2((3, d))]
    out_shape = (jax.ShapeDtypeStruct((b, t, d), BF16),
                 jax.ShapeDtypeStruct((b, WINDOW, KV_COLS), F32),
                 jax.ShapeDtypeStruct((b, WINDOW, KV_COLS), F32),
                 jax.ShapeDtypeStruct((b, 2, d), F32))
    out_specs = (pl.BlockSpec((1, tq, d), lambda i, s: (i, s, 0)),
                 per_b((1, WINDOW, KV_COLS)), per_b((1, WINDOW, KV_COLS)), per_b((1, 2, d)))
    return pl.pallas_call(
        functools.partial(_front_body, tq=tq, nq=nq),
        out_shape=out_shape,
        grid=(b, t // tq),
        in_specs=in_specs,
        out_specs=out_specs,
        scratch_shapes=[pltpu.VMEM((N_KV, WINDOW + tq, HEAD_DIM), BF16),
                        pltpu.VMEM((KV_COLS, WINDOW + tq), BF16),
                        pltpu.VMEM((2, 8, d), F32),
                        pltpu.VMEM((tq, d), F32),
                        pltpu.VMEM((tq, 6 * d), F32)],
        compiler_params=_cparams(("arbitrary", "arbitrary")),
        name="front",
    )(x, x, mod, nw, w_in_b, knw, qsc, bd, bias_tab, sink_tab, conv_w)


def _mixer(proj, knw, qsc, bias_tab, sink_tab, conv_w, state, tq, nq):
    b, t, _ = proj.shape
    d = D_MODEL
    stateful = state is not None
    key_rows = max(WINDOW + tq, KEY_WIN)
    r = np.arange(KV_COLS) // HEAD_DIM
    bd = jnp.asarray((r[:, None] == r[None, :]).astype(np.float32), BF16)
    wide = lambda j: pl.BlockSpec((1, tq, d), lambda i, s, j=j: (i, s, j))
    kvspec = lambda j: pl.BlockSpec((1, tq, KV_COLS), lambda i, s, j=j: (i, s, j))
    const2 = lambda shp: pl.BlockSpec(shp, lambda i, s: (0, 0))
    const3 = lambda shp: pl.BlockSpec(shp, lambda i, s: (0, 0, 0))
    per_b = lambda shp: pl.BlockSpec(shp, lambda i, s: (i, 0, 0))
    kvblk = 6 * d // KV_COLS
    if stateful:
        hist_specs = [per_b((1, WINDOW, KV_COLS)), per_b((1, WINDOW, KV_COLS)), per_b((1, 2, d)), per_b((1, 2, d))]
        hist_args = [state[0], state[1], state[2], state[2]]
    else:
        kw_ = tq // WINDOW
        prev_kv = lambda j: pl.BlockSpec((1, WINDOW, KV_COLS),
                                         lambda i, s, j=j: (i, jnp.maximum(s * kw_ - 1, 0), j))
        prev8 = lambda j: pl.BlockSpec((1, 8, d), lambda i, s, j=j: (i, jnp.maximum(s * (tq // 8) - 1, 0), j))
        hist_specs = [prev_kv(kvblk), prev_kv(kvblk + 1), prev8(2), prev8(3)]
        hist_args = [proj, proj, proj, proj]
    in_specs = [wide(0), kvspec(kvblk), kvspec(kvblk + 1), wide(1), wide(2), wide(3), wide(4), wide(5),
                const2((1, KV_COLS)), const2((1, KV_COLS)), const2((KV_COLS, KV_COLS)),
                const3(bias_tab.shape), const3(sink_tab.shape), const2((3, d))] + hist_specs
    out_shape = (jax.ShapeDtypeStruct((b, t, d), BF16),
                 jax.ShapeDtypeStruct((b, WINDOW, KV_COLS), F32),
                 jax.ShapeDtypeStruct((b, WINDOW, KV_COLS), F32),
                 jax.ShapeDtypeStruct((b, 2, d), F32))
    out_specs = (pl.BlockSpec((1, tq, d), lambda i, s: (i, s, 0)),
                 per_b((1, WINDOW, KV_COLS)), per_b((1, WINDOW, KV_COLS)), per_b((1, 2, d)))
    return pl.pallas_call(
        functools.partial(_mixer_body, tq=tq, nq=nq, stateful=stateful),
        out_shape=out_shape,
        grid=(b, t // tq),
        in_specs=in_specs,
        out_specs=out_specs,
        scratch_shapes=[pltpu.VMEM((N_KV, key_rows, HEAD_DIM), BF16),
                        pltpu.VMEM((KV_COLS, key_rows), BF16),
                        pltpu.VMEM((tq, d), F32)],
        compiler_params=_cparams(("arbitrary", "arbitrary")),
        name="mixer_state" if stateful else "mixer",
    )(proj, proj, proj, proj, proj, proj, proj, proj, knw, qsc, bd, bias_tab, sink_tab, conv_w, *hist_args)


def _route(logits):
    lane = lax.broadcasted_iota(I32, logits.shape, 1).astype(F32)
    neg = -jnp.inf
    big = float(1 << 20)
    gl = jnp.where(lane < N_GROUPS, logits, neg)
    gmax = jnp.max(gl, axis=-1, keepdims=True)
    g_idx = jnp.min(jnp.where(gl == gmax, lane, big), axis=-1, keepdims=True)
    g_w = 1.0 / jnp.sum(jnp.exp(gl - gmax), axis=-1, keepdims=True)
    lo = N_GROUPS + g_idx * EPG
    el = jnp.where((lane >= lo) & (lane < lo + EPG), logits, neg)
    m1 = jnp.max(el, axis=-1, keepdims=True)
    i1 = jnp.min(jnp.where(el == m1, lane, big), axis=-1, keepdims=True)
    el2 = jnp.where(lane == i1, neg, el)
    m2 = jnp.max(el2, axis=-1, keepdims=True)
    i2 = jnp.min(jnp.where(el2 == m2, lane, big), axis=-1, keepdims=True)
    r = jnp.exp(m2 - m1)
    w1 = 1.0 / (1.0 + r)
    w2 = r / (1.0 + r)
    return i1 - N_GROUPS, i2 - N_GROUPS, g_w * w1, g_w * w2


def _outproj_kernel(m_ref, x_ref, mod_ref, wo_ref, nw_ref, wr_ref, br_ref,
                    x1_ref, h2_ref, e01_ref, ew_ref):
    mod = mod_ref[0]
    sub = ROW_TILE
    n_sub = x_ref.shape[1] // sub
    rows = lambda s: slice(s * sub, (s + 1) * sub)
    modrow = lambda k, s: mod[k] if mod.shape[1] == 1 else mod[k, rows(s)]

    mix = [jnp.dot(m_ref[0, rows(s)], wo_ref[...], preferred_element_type=F32) for s in range(n_sub)]
    prod = []
    lane8 = lax.broadcasted_iota(I32, (sub, 8), 1)

    def route(s):
        logits = prod[s][:sub, :LANES] + (prod[s][:sub, LANES:] + prod[s][sub:, :LANES]) + br_ref[...]
        e1, e2, w1, w2 = _route(logits)
        e01_ref[s] = jnp.concatenate([_col_to_row(e1), _col_to_row(e2)], axis=0).astype(I32)
        ew_ref[rows(s)] = jnp.where(lane8 == 0, w1, jnp.where(lane8 == 1, w2, 0.0))

    for s in range(n_sub):
        x1 = x_ref[0, rows(s)] + modrow(2, s) * mix[s]
        x1_ref[0, rows(s)] = x1
        h = x1 * lax.rsqrt(jnp.mean(x1 * x1, axis=-1, keepdims=True) + RMS_EPS) * nw_ref[...]
        h = h * (1.0 + modrow(4, s)) + modrow(3, s)
        h2_ref[rows(s)] = _pack_pairs(h)
        h_hi, h_lo = _split_bf16(h)
        prod.append(jnp.dot(jnp.concatenate([h_hi, h_lo], axis=0), wr_ref[...], preferred_element_type=F32))
        if s > 0:
            route(s - 1)
    route(n_sub - 1)


def _outproj(merged, x, mod4, w_out_b, nw, w_r, b_r, tm):
    b, t, d = x.shape
    nt = t // tm
    mr = mod4.shape[2]
    assert mr == 1 or (mr == t and nt == 1)
    n_sub = tm // ROW_TILE
    assert n_sub * ROW_TILE == tm
    flat = lambda i, j: (i * nt + j, 0)
    return pl.pallas_call(
        _outproj_kernel,
        out_shape=(jax.ShapeDtypeStruct((b, t, d), F32),
                   jax.ShapeDtypeStruct((b * t, HALF), I32),
                   jax.ShapeDtypeStruct((b * nt * n_sub, 2, ROW_TILE), I32),
                   jax.ShapeDtypeStruct((b * t, 8), F32)),
        grid=(b, nt),
        in_specs=[pl.BlockSpec((1, tm, d), lambda i, j: (i, j, 0)),
                  pl.BlockSpec((1, tm, d), lambda i, j: (i, j, 0)),
                  pl.BlockSpec((1, 6, mr, d), lambda i, j: (i, 0, 0, 0)),
                  pl.BlockSpec((d, d), lambda i, j: (0, 0)),
                  pl.BlockSpec((1, d), lambda i, j: (0, 0)),
                  pl.BlockSpec((d, 2 * LANES), lambda i, j: (0, 0)),
                  pl.BlockSpec((1, LANES), lambda i, j: (0, 0))],
        out_specs=(pl.BlockSpec((1, tm, d), lambda i, j: (i, j, 0)),
                   pl.BlockSpec((tm, HALF), flat),
                   pl.BlockSpec((n_sub, 2, ROW_TILE), lambda i, j: (i * nt + j, 0, 0)),
                   pl.BlockSpec((tm, 8), flat)),
        compiler_params=_cparams(("arbitrary", "arbitrary")),
        name="outproj",
    )(merged, x, mod4, w_out_b, nw, w_r, b_r)


def _col_to_row(col):
    eye = lax.broadcasted_iota(I32, (LANES, LANES), 0) == lax.broadcasted_iota(I32, (LANES, LANES), 1)
    parts = [jnp.sum(jnp.where(eye, col[r * LANES:(r + 1) * LANES], 0.0), axis=0, keepdims=True)
             for r in range(col.shape[0] // LANES)]
    return jnp.concatenate(parts, axis=1)


def _rank_kernel(e_ref, tri_ref, low_ref, d_ref, tot_ref, *, block):
    n_sub, _, t = e_ref.shape
    sub = lax.broadcasted_iota(I32, (LANES, t), 0)

    def hots(s):
        e = e_ref[s]
        return sub == e[0:1], sub == e[1:2]

    def count(s, cnt):
        h0, h1 = hots(s)
        return cnt + jnp.sum(jnp.where(h0 | h1, 1.0, 0.0), axis=1, keepdims=True)

    cnt = lax.fori_loop(0, n_sub, count, jnp.zeros((LANES, 1), F32))
    tot_ref[...] = _col_to_row(cnt).astype(I32)
    nblk = jnp.floor((cnt + (block - 1)) * (1.0 / block))
    hi = jnp.floor(nblk * (1.0 / 16.0))
    lo = nblk - hi * 16.0
    low = low_ref[...]
    bcast = lambda c: jnp.broadcast_to(c, (LANES, LANES)).astype(BF16)
    excl = (jnp.dot(low, bcast(hi), preferred_element_type=F32) * 16.0
            + jnp.dot(low, bcast(lo), preferred_element_type=F32))
    starts = excl[:, 0:1] * float(block)

    def place(s, running):
        h0, h1 = hots(s)
        onehot = jnp.where(h0 | h1, 1.0, 0.0)
        prefix = jnp.dot(onehot.astype(BF16), tri_ref[...], preferred_element_type=F32)
        pos = prefix + running
        d0 = jnp.sum(jnp.where(h0, pos, 0.0), axis=0, keepdims=True)
        d1 = jnp.sum(jnp.where(h1, pos, 0.0), axis=0, keepdims=True)
        d_ref[s] = jnp.concatenate([d0, d1], axis=0).astype(I32)
        return running + jnp.sum(onehot, axis=1, keepdims=True)

    lax.fori_loop(0, n_sub, place, starts)


def _rank(e01, block):
    n_sub, _, t = e01.shape
    r = np.arange(t)
    tri = jnp.asarray((r[:, None] < r[None, :]).astype(np.float32), BF16)
    l = np.arange(LANES)
    low = jnp.asarray((l[None, :] < l[:, None]).astype(np.float32), BF16)
    return pl.pallas_call(
        functools.partial(_rank_kernel, block=block),
        out_shape=(jax.ShapeDtypeStruct((n_sub, 2, t), I32), jax.ShapeDtypeStruct((1, LANES), I32)),
        compiler_params=pltpu.CompilerParams(vmem_limit_bytes=VMEM_LIMIT),
        name="rank",
    )(e01, tri, low)


def _expert_kernel(start_ref, nblk_ref, xs_hbm, wg_ref, wu_ref, wd_ref, ys_hbm,
                   xbuf, ybuf, wg_s, wu_s, wd_s, sem_in, sem_out, *, block):
    nbuf = EXPERT_BUFS
    e = pl.program_id(0)
    n = nblk_ref[e]
    base = start_ref[e]
    total = start_ref[N_EXPERTS - 1] + nblk_ref[N_EXPERTS - 1]

    def in_copy(g):
        rows = pl.ds(pl.multiple_of(g * block, block), block)
        return pltpu.make_async_copy(xs_hbm.at[rows], xbuf.at[g % nbuf], sem_in.at[g % nbuf])

    def out_copy(g):
        rows = pl.ds(pl.multiple_of(g * block, block), block)
        return pltpu.make_async_copy(ybuf.at[g % nbuf], ys_hbm.at[rows], sem_out.at[g % nbuf])

    @pl.when(e == 0)
    def _():
        for g0 in range(nbuf - 1):
            @pl.when(g0 < total)
            def _(g0=g0):
                in_copy(g0).start()

    @pl.when(n > 0)
    def _():
        wg_s[...] = wg_ref[0].astype(BF16)
        wu_s[...] = wu_ref[0].astype(BF16)
        wd_s[...] = wd_ref[0].astype(BF16)

        def body(i, carry):
            g = base + i
            slot = g % nbuf
            in_copy(g).wait()

            @pl.when(g + nbuf - 1 < total)
            def _():
                in_copy(g + nbuf - 1).start()

            @pl.when(g >= nbuf)
            def _():
                out_copy(g - nbuf).wait()

            a, c = _unpack_pairs(xbuf[slot])
            x = jnp.concatenate([a.astype(BF16), c.astype(BF16)], axis=1)
            gate = jnp.dot(x, wg_s[...], preferred_element_type=F32)
            up = jnp.dot(x, wu_s[...], preferred_element_type=F32)
            hmid = (gate * _sigmoid(gate) * up).astype(BF16)
            ybuf[slot] = _pack_pairs(jnp.dot(hmid, wd_s[...], preferred_element_type=F32))
            out_copy(g).start()
            return carry

        lax.fori_loop(0, n, body, 0)

    @pl.when(e == N_EXPERTS - 1)
    def _():
        for back in range(nbuf, 0, -1):
            @pl.when(total >= back)
            def _(back=back):
                out_copy(total - back).wait()


def _experts(xs, start_blk, nblk, w_gate, w_up, w_down, block):
    n_rows = xs.shape[0]
    wblk = lambda e, st, nb: (e, 0, 0)
    grid_spec = pltpu.PrefetchScalarGridSpec(
        num_scalar_prefetch=2,
        grid=(N_EXPERTS,),
        in_specs=[pl.BlockSpec(memory_space=pl.ANY),
                  pl.BlockSpec((1, D_MODEL, D_EXPERT), wblk),
                  pl.BlockSpec((1, D_MODEL, D_EXPERT), wblk),
                  pl.BlockSpec((1, D_EXPERT, D_MODEL), wblk)],
        out_specs=pl.BlockSpec(memory_space=pl.ANY),
        scratch_shapes=[pltpu.VMEM((EXPERT_BUFS, block, HALF), I32),
                        pltpu.VMEM((EXPERT_BUFS, block, HALF), I32),
                        pltpu.VMEM((D_MODEL, D_EXPERT), BF16),
                        pltpu.VMEM((D_MODEL, D_EXPERT), BF16),
                        pltpu.VMEM((D_EXPERT, D_MODEL), BF16),
                        pltpu.SemaphoreType.DMA((EXPERT_BUFS,)),
                        pltpu.SemaphoreType.DMA((EXPERT_BUFS,))])
    return pl.pallas_call(
        functools.partial(_expert_kernel, block=block),
        out_shape=jax.ShapeDtypeStruct((n_rows, HALF), I32),
        grid_spec=grid_spec,
        compiler_params=_cparams(("arbitrary",)),
        name="experts",
    )(start_blk, nblk, xs, w_gate, w_up, w_down)


def _final_kernel(x1_ref, y0_ref, y1_ref, ew_ref, mod_ref, o_ref):
    a0, b0 = _unpack_pairs(y0_ref[...])
    a1, b1 = _unpack_pairs(y1_ref[...])
    w0 = ew_ref[:, 0:1]
    w1 = ew_ref[:, 1:2]
    moe = jnp.concatenate([w0 * a0 + w1 * a1, w0 * b0 + w1 * b1], axis=1)
    o_ref[0] = x1_ref[0] + mod_ref[0][5:6] * moe


def _final(x1, y0, y1, ew, mod, tm, b0=0, nb=None, y_prev=None):
    b, t, d = x1.shape
    nb = b if nb is None else nb
    nt = t // tm
    local = lambda i, j: (i * nt + j, 0)
    glob = lambda i, j: ((i + b0) * nt + j, 0)
    rows3 = lambda i, j: (i + b0, j, 0)
    in_specs = [pl.BlockSpec((1, tm, d), rows3),
                pl.BlockSpec((tm, HALF), local),
                pl.BlockSpec((tm, HALF), local),
                pl.BlockSpec((tm, 8), glob),
                pl.BlockSpec((1, 6, d), lambda i, j: (i + b0, 0, 0))]
    args = [x1, y0, y1, ew, mod]
    aliases = {}
    kern = _final_kernel
    if y_prev is not None:
        in_specs.append(pl.BlockSpec(memory_space=pl.ANY))
        args.append(y_prev)
        aliases = {5: 0}
        kern = lambda *refs: _final_kernel(*refs[:5], refs[6])
    return pl.pallas_call(
        kern,
        out_shape=jax.ShapeDtypeStruct((b, t, d), F32),
        grid=(nb, nt),
        in_specs=in_specs,
        out_specs=pl.BlockSpec((1, tm, d), rows3),
        input_output_aliases=aliases,
        compiler_params=_cparams(("arbitrary", "arbitrary")),
        name="final",
    )(*args)


def _sc_window(rows_per_worker):
    for w in range(SC_MAX_WINDOW, 7, -8):
        if rows_per_worker % w == 0:
            return w
    raise ValueError(f"no SparseCore window divides {rows_per_worker} rows per worker")


def _sc_split(idx):
    n = idx.shape[0]
    per = n // SC_WORKERS
    assert per * SC_WORKERS == n
    win = _sc_window(per)
    return idx.reshape(SC_WORKERS, per // win, win), per // win, win


def _sc_worker_id():
    return lax.axis_index("s") * SC_CORES + lax.axis_index("c")


def _dispatch_rows(h2_groups, dest_groups, n_rows):
    splits = [(_sc_split(d0), _sc_split(d1)) for d0, d1 in dest_groups]
    ng = len(h2_groups)
    scratch = []
    for (_, _, win), _ in splits:
        scratch += [pltpu.VMEM((win,), I32), pltpu.VMEM((win,), I32), pltpu.VMEM((win, HALF), I32)]

    @functools.partial(
        pl.kernel,
        mesh=plsc.VectorSubcoreMesh(core_axis_name="c", subcore_axis_name="s"),
        out_type=jax.ShapeDtypeStruct((n_rows, HALF), I32),
        scratch_types=scratch,
        name="sc_dispatch",
    )
    def k(*refs):
        x_refs, idx_refs, o_hbm, bufs = refs[:ng], refs[ng:3 * ng], refs[3 * ng], refs[3 * ng + 1:]
        wid = _sc_worker_id()
        for g in range(ng):
            (_, nwin, win), _ = splits[g]
            x_hbm, d0_hbm, d1_hbm = x_refs[g], idx_refs[2 * g], idx_refs[2 * g + 1]
            i0_v, i1_v, rows_v = bufs[3 * g:3 * g + 3]

            @pl.loop(0, nwin)
            def _(j, nwin=nwin, win=win, x_hbm=x_hbm, d0_hbm=d0_hbm, d1_hbm=d1_hbm,
                  i0_v=i0_v, i1_v=i1_v, rows_v=rows_v):
                base = pl.multiple_of((wid * nwin + j) * win, 8)
                pltpu.sync_copy(d0_hbm.at[wid, j], i0_v)
                pltpu.sync_copy(d1_hbm.at[wid, j], i1_v)
                pltpu.sync_copy(x_hbm.at[pl.ds(base, win)], rows_v)
                pltpu.sync_copy(rows_v, o_hbm.at[i0_v])
                pltpu.sync_copy(rows_v, o_hbm.at[i1_v])

    idx_args = []
    for (s0, s1) in splits:
        idx_args += [s0[0], s1[0]]
    return k(*h2_groups, *idx_args)


def _collect_rows(ys, dest_groups):
    splits = [(_sc_split(d0), _sc_split(d1)) for d0, d1 in dest_groups]
    ng = len(dest_groups)
    outs, scratch = [], []
    for (d0, _), ((_, _, win), _) in zip(dest_groups, splits):
        o = jax.ShapeDtypeStruct((d0.shape[0], HALF), I32)
        outs += [o, o]
        scratch += [pltpu.VMEM((win,), I32), pltpu.VMEM((win, HALF), I32)]

    @functools.partial(
        pl.kernel,
        mesh=plsc.VectorSubcoreMesh(core_axis_name="c", subcore_axis_name="s"),
        out_type=tuple(outs),
        scratch_types=scratch,
        name="sc_collect",
    )
    def k(*refs):
        ys_hbm, idx_refs, out_refs, bufs = refs[0], refs[1:1 + 2 * ng], refs[1 + 2 * ng:1 + 4 * ng], refs[1 + 4 * ng:]
        wid = _sc_worker_id()
        for g in range(ng):
            (_, nwin, win), _ = splits[g]
            i_v, rows_v = bufs[2 * g:2 * g + 2]
            for kk in range(2):
                d_hbm, y_hbm = idx_refs[2 * g + kk], out_refs[2 * g + kk]

                @pl.loop(0, nwin)
                def _(j, nwin=nwin, win=win, d_hbm=d_hbm, y_hbm=y_hbm, i_v=i_v, rows_v=rows_v):
                    base = pl.multiple_of((wid * nwin + j) * win, 8)
                    pltpu.sync_copy(d_hbm.at[wid, j], i_v)
                    pltpu.sync_copy(ys_hbm.at[i_v], rows_v)
                    pltpu.sync_copy(rows_v, y_hbm.at[pl.ds(base, win)])

    idx_args = []
    for (s0, s1) in splits:
        idx_args += [s0[0], s1[0]]
    res = k(ys, *idx_args)
    return [(res[2 * g], res[2 * g + 1]) for g in range(ng)]


def _t5_bucket(rel):
    half = N_BUCKETS // 2
    max_exact = half // 2
    n = jnp.abs(rel)
    far = max_exact + (jnp.log(jnp.maximum(n, 1).astype(F32) / max_exact)
                       / math.log(MAX_DISTANCE / max_exact) * (half - max_exact)).astype(I32)
    far = jnp.minimum(far, half - 1)
    return jnp.where(rel > 0, half, 0) + jnp.where(n < max_exact, n, far)


def _bias_table(rel_bias, cq, nq, no_history):
    nk = WINDOW + cq
    j = jnp.arange(KEY_WIN)[:, None]
    c = jnp.arange(UNIT_Q)[None, :]
    jj = j - (c // cq) * cq
    valid = (jj >= 0) & (jj < nk) & (c < nq)
    if no_history:
        valid = valid & (j >= WINDOW)
    rel = jj - WINDOW - (c % cq)
    onehot = (_t5_bucket(rel)[:, :, None] == jnp.arange(N_BUCKETS)).astype(F32)
    bias = jnp.einsum("jcb,bh->jch", onehot, rel_bias.astype(F32), precision=lax.Precision.HIGHEST)
    bias = jnp.where(valid[:, :, None], bias * LOG2E, -jnp.inf)
    bias = jnp.transpose(bias.reshape(KEY_WIN, UNIT_Q, N_KV, GROUP), (2, 0, 3, 1))
    return bias.reshape(N_KV, KEY_WIN, GROUP * UNIT_Q)


def _sink_table(sinks):
    s = sinks.astype(F32).reshape(N_KV, 1, GROUP, 1)
    return jnp.broadcast_to(s * LOG2E, (N_KV, 1, GROUP, UNIT_Q)).reshape(N_KV, 1, GROUP * UNIT_Q)


def kernel(x_prompt, x_sample, state_attn_k, state_attn_v, state_conv, c_prompt, c_sample,
           rel_bias, w_ada, b_ada, norm1_w, w_in, q_norm_w, k_norm_w, attn_sinks, conv_w,
           w_out, norm2_w, w_router_group, b_router_group, w_router_expert, b_router_expert,
           w_gate, w_up, w_down):
    depth = w_ada.shape[0]
    assert depth == 1
    bp, tp, d = x_prompt.shape
    bs, ts, _ = x_sample.shape
    n_p, n_s = bp * tp, bs * ts
    n_tok = n_p + n_s
    l = 0

    wi = w_in[l]
    qw, kw, vw, rest = wi[:, :d], wi[:, d:d + KV_COLS], wi[:, d + KV_COLS:d + 2 * KV_COLS], wi[:, d + 2 * KV_COLS:]
    w_in_b = jnp.concatenate([qw, rest, kw, vw], axis=1).astype(BF16)
    w_out_b = w_out[l].astype(BF16)
    w_r = jnp.concatenate([w_router_group[l],
                           jnp.transpose(w_router_expert[l], (1, 0, 2)).reshape(d, N_EXPERTS),
                           jnp.zeros((d, LANES - N_GROUPS - N_EXPERTS), F32)], axis=1)
    w_r_hi = lax.reduce_precision(w_r, exponent_bits=8, mantissa_bits=7)
    w_r = jnp.concatenate([w_r_hi.astype(BF16), (w_r - w_r_hi).astype(BF16)], axis=1)
    b_r = jnp.concatenate([b_router_group[l], b_router_expert[l].reshape(-1),
                           jnp.zeros((LANES - N_GROUPS - N_EXPERTS,), F32)]).reshape(1, LANES)
    knw = jnp.tile(k_norm_w[l], N_KV).reshape(1, KV_COLS)
    qsc = jnp.tile(q_norm_w[l] * (HEAD_DIM ** -0.5 * LOG2E), N_KV).reshape(1, KV_COLS)
    n1w = norm1_w[l].reshape(1, d)
    n2w = norm2_w[l].reshape(1, d)

    mod = _ada(jnp.concatenate([c_prompt, c_sample], axis=0), w_ada[l], b_ada[l]).reshape(bp + bs, 6, d)
    mod_p, mod_s = mod[:bp], mod[bp:]

    xs_rows = x_sample.reshape(1, n_s, d)
    mod4_p = mod_p[:, :, None, :]
    mod4_s = jnp.repeat(jnp.transpose(mod_s, (1, 0, 2)), ts, axis=1)[None]
    proj_s = _inproj(xs_rows, mod4_s, n1w, w_in_b, n_s).reshape(bs, ts, IN_COLS)
    sink_tab = _sink_table(attn_sinks[l])
    bias_p = jnp.concatenate([_bias_table(rel_bias, CHUNK, UNIT_Q, False),
                              _bias_table(rel_bias, CHUNK, UNIT_Q, True)], axis=0)
    merged_p, k_p, v_p, c_p = _front(x_prompt, mod_p, n1w, w_in_b, knw, qsc, bias_p, sink_tab, conv_w[l],
                                     MIX_TILE, UNIT_Q)
    state = (state_attn_k[l].reshape(bs, WINDOW, KV_COLS), state_attn_v[l].reshape(bs, WINDOW, KV_COLS),
             state_conv[l])
    merged_s, k_s, v_s, c_s = _mixer(proj_s, knw, qsc, _bias_table(rel_bias, ts, ts, False), sink_tab, conv_w[l],
                                     state, ts, ts)

    x1_p, h2_p, e01_p, ew_p = _outproj(merged_p, x_prompt, mod4_p, w_out_b, n2w, w_r, b_r, OUT_TILE)
    x1_s, h2_s, e01_s, ew_s = _outproj(merged_s.reshape(1, n_s, d), xs_rows, mod4_s, w_out_b, n2w, w_r, b_r, n_s)
    x1_s = x1_s.reshape(bs, ts, d)

    assert n_s == ROW_TILE
    d01, totals = _rank(jnp.concatenate([e01_p, e01_s], axis=0), EXPERT_BLOCK)
    n_sub_p = n_p // ROW_TILE
    dests = [(d01[:n_sub_p, 0].reshape(-1), d01[:n_sub_p, 1].reshape(-1)),
             (d01[n_sub_p:, 0].reshape(-1), d01[n_sub_p:, 1].reshape(-1))]
    nblk = (totals[0, :N_EXPERTS] + EXPERT_BLOCK - 1) // EXPERT_BLOCK
    start_blk = (jnp.cumsum(nblk) - nblk).astype(I32)
    nb_max = -(-2 * n_tok // EXPERT_BLOCK) + N_EXPERTS

    xs = _dispatch_rows([h2_p, h2_s], dests, nb_max * EXPERT_BLOCK)
    ys = _experts(xs, start_blk, nblk.astype(I32), w_gate[l], w_up[l], w_down[l], EXPERT_BLOCK)
    (d0_p, d1_p), dest_s = dests
    assert bp % COLLECT_PARTS == 0
    nbp = bp // COLLECT_PARTS
    rows = nbp * tp
    y_p = None
    for part in range(COLLECT_PARTS):
        sl = slice(part * rows, (part + 1) * rows)
        groups = [(d0_p[sl], d1_p[sl])] + ([dest_s] if part == 0 else [])
        got = _collect_rows(ys, groups)
        if part == 0:
            y0_s, y1_s = got[1]
        y_p = _final(x1_p, got[0][0], got[0][1], ew_p, mod_p, ROW_TILE, part * nbp, nbp, y_p)
    y_s = _final(x1_s, y0_s, y1_s, ew_s, mod_s, ts)

    kv_shape = (1, -1, WINDOW, N_KV, HEAD_DIM)
    return (y_p, y_s, k_p.reshape(kv_shape), v_p.reshape(kv_shape), c_p[None],
            k_s.reshape(kv_shape), v_s.reshape(kv_shape), c_s[None])
```

```python
import functools
import math

import numpy as np
import jax
import jax.numpy as jnp
from jax import lax
from jax.experimental import pallas as pl
from jax.experimental.pallas import tpu as pltpu
from jax.experimental.pallas import tpu_sc as plsc

F32 = jnp.float32
BF16 = jnp.bfloat16
I32 = jnp.int32

D_MODEL = 1024
HEAD_DIM = 64
N_HEADS = 16
N_KV = 4
GROUP = 4
CHUNK = 64
WINDOW = 128
N_BUCKETS = 32
MAX_DISTANCE = 128
N_GROUPS = 8
EPG = 8
N_EXPERTS = 64
D_EXPERT = 512
RMS_EPS = 1e-6
LOG2E = math.log2(math.e)
SCORE_LOOKAHEAD = 1
PROLOGUE_CHUNKS = 4
SUM_ROWS = 16
KV_COLS = N_KV * HEAD_DIM
IN_COLS = 6 * D_MODEL + 2 * KV_COLS
HALF = D_MODEL // 2
LANES = 128

VMEM_LIMIT = 56 * 1024 * 1024
ADA_TN = 2048
INPROJ_TN = 512
ROW_TILE = 512
OUT_TILE = 1024
MIX_TILE = 512
UNIT_Q = 2 * CHUNK
KEY_WIN = WINDOW + UNIT_Q
PROJ_CHUNK = 256
EXPERT_BLOCK = 512
COLLECT_PARTS = 4
EXPERT_BUFS = 4
SC_CORES = 2
SC_SUBCORES = 16
SC_WORKERS = SC_CORES * SC_SUBCORES
SC_MAX_WINDOW = 128


def _cparams(sem):
    return pltpu.CompilerParams(dimension_semantics=sem, vmem_limit_bytes=VMEM_LIMIT)


def _split_bf16(a):
    hi = a.astype(BF16)
    lo = (a - hi.astype(F32)).astype(BF16)
    return hi, lo


def _dot3(a, b):
    ah, al = _split_bf16(a)
    bh, bl = _split_bf16(b)
    d = functools.partial(jnp.dot, preferred_element_type=F32)
    return d(ah, bh) + (d(ah, bl) + d(al, bh))


def _sigmoid(x):
    return 0.5 * jnp.tanh(0.5 * x) + 0.5


def _pack_pairs(y):
    a = lax.bitcast_convert_type(y[:, :HALF].astype(BF16).astype(F32), I32)
    b = lax.bitcast_convert_type(y[:, HALF:].astype(BF16).astype(F32), I32)
    return a | lax.shift_right_logical(b, jnp.int32(16))


def _unpack_pairs(w):
    a = lax.bitcast_convert_type(w & jnp.int32(-65536), F32)
    b = lax.bitcast_convert_type(lax.shift_left(w, jnp.int32(16)), F32)
    return a, b


def _ada_kernel(c_ref, w_ref, b_ref, o_ref):
    c = c_ref[...]
    s = c * jax.nn.sigmoid(c)
    o_ref[...] = _dot3(s, w_ref[...]) + b_ref[...]


def _ada(c_all, w_ada, b_ada):
    r, d = c_all.shape
    n = w_ada.shape[1]
    tn = ADA_TN
    return pl.pallas_call(
        _ada_kernel,
        out_shape=jax.ShapeDtypeStruct((r, n), F32),
        grid=(n // tn,),
        in_specs=[pl.BlockSpec((r, d), lambda j: (0, 0)),
                  pl.BlockSpec((d, tn), lambda j: (0, j)),
                  pl.BlockSpec((1, tn), lambda j: (0, j))],
        out_specs=pl.BlockSpec((r, tn), lambda j: (0, j)),
        compiler_params=_cparams(("arbitrary",)),
        name="ada",
    )(c_all, w_ada, b_ada.reshape(1, n))


def _inproj_kernel(x_ref, mod_ref, nw_ref, w_ref, o_ref):
    x = x_ref[0]
    mod = mod_ref[0]
    h = x * lax.rsqrt(jnp.mean(x * x, axis=-1, keepdims=True) + RMS_EPS) * nw_ref[...]
    h = h * (1.0 + mod[1]) + mod[0]
    hb = h.astype(BF16)
    for j in range(IN_COLS // INPROJ_TN):
        sl = slice(j * INPROJ_TN, (j + 1) * INPROJ_TN)
        o_ref[0, :, sl] = jnp.dot(hb, w_ref[:, sl], preferred_element_type=F32).astype(BF16)


def _inproj(x, mod4, nw, w_in_b, tm):
    b, t, d = x.shape
    mr = mod4.shape[2]
    assert mr == 1 or (mr == t and tm == t)
    return pl.pallas_call(
        _inproj_kernel,
        out_shape=jax.ShapeDtypeStruct((b, t, IN_COLS), BF16),
        grid=(b, t // tm),
        in_specs=[pl.BlockSpec((1, tm, d), lambda i, j: (i, j, 0)),
                  pl.BlockSpec((1, 6, mr, d), lambda i, j: (i, 0, 0, 0)),
                  pl.BlockSpec((1, d), lambda i, j: (0, 0)),
                  pl.BlockSpec((d, IN_COLS), lambda i, j: (0, 0), pipeline_mode=pl.Buffered(1))],
        out_specs=pl.BlockSpec((1, tm, IN_COLS), lambda i, j: (i, j, 0)),
        compiler_params=_cparams(("arbitrary", "arbitrary")),
        name="inproj",
    )(x, mod4, nw, w_in_b)


def _head_inv_rms(xf, bd, two_pass=True):
    sq = xf * xf
    if two_pass:
        hi, lo = _split_bf16(sq)
        ssq = jnp.dot(hi, bd, preferred_element_type=F32) + jnp.dot(lo, bd, preferred_element_type=F32)
    else:
        ssq = jnp.dot(sq.astype(BF16), bd, preferred_element_type=F32)
    return lax.rsqrt(ssq * (1.0 / HEAD_DIM) + RMS_EPS)


def _mixer_body(q_ref, k_ref, v_ref, bg_ref, c_ref, u_ref, ga_ref, gc_ref,
                knw_ref, qsc_ref, bd_ref, bias_ref, sink_ref, cw_ref,
                kpast_ref, vpast_ref, cpast_ref, upast_ref,
                merged_ref, knew_ref, vnew_ref, cnew_ref,
                kq_buf, vt_buf, attn_buf, *, tq, nq, stateful):
    hb = WINDOW
    pw = UNIT_Q
    t = pl.program_id(1)
    bd = bd_ref[...]

    k = k_ref[0].astype(F32)
    kn = k * _head_inv_rms(k, bd) * knw_ref[...]
    kq = (kn * qsc_ref[...]).astype(BF16)
    vb = v_ref[0]
    vt = vb.astype(F32).T.astype(BF16)

    if stateful:
        kp = kpast_ref[0]
        vp = vpast_ref[0]
        for kv in range(N_KV):
            kq_buf[kv, hb + tq:] = jnp.zeros((KEY_WIN - hb - tq, HEAD_DIM), BF16)
        vt_buf[:, hb + tq:] = jnp.zeros((KV_COLS, KEY_WIN - hb - tq), BF16)
        u_hist = jnp.concatenate([jnp.zeros((6, D_MODEL), F32), cpast_ref[0]], axis=0)
        knew_ref[0] = jnp.concatenate([kp[tq:], kn], axis=0)
        vnew_ref[0] = jnp.concatenate([vp[tq:], vb.astype(F32)], axis=0)
    else:
        kraw = kpast_ref[0].astype(F32)
        kp = kraw * _head_inv_rms(kraw, bd) * knw_ref[...]
        vp = vpast_ref[0].astype(F32)
        u_hist = jnp.where(t == 0, 0.0, cpast_ref[0].astype(F32) * upast_ref[0].astype(F32))
        knew_ref[0] = kn[tq - hb:]
        vnew_ref[0] = vb[tq - hb:].astype(F32)
    _fill_keys(kq_buf, vt_buf, kp, vp, kq, vt, qsc_ref, tq)
    q = q_ref[0]
    _attention_units(lambda kv: q[:, kv * KV_COLS:(kv + 1) * KV_COLS].astype(F32), kq_buf, vt_buf,
                     bias_ref, sink_ref, bd, attn_buf, tq, nq, None if stateful else t)
    cu = c_ref[0].astype(F32) * u_ref[0].astype(F32)
    conv, u_all = _short_conv(u_hist, cu, cw_ref[...], tq)
    cnew_ref[0] = u_all[tq + 6:tq + 8]
    merged = (_sigmoid(ga_ref[0].astype(F32)) * attn_buf[...]
              + _sigmoid(gc_ref[0].astype(F32)) * (bg_ref[0].astype(F32) * conv))
    merged_ref[0] = merged.astype(BF16)


def _fill_keys(kq_buf, vt_buf, kp, vp, kq, vt, qsc_ref, tq):
    hb = WINDOW
    kqp = (kp * qsc_ref[...]).astype(BF16)
    for kv in range(N_KV):
        kq_buf[kv, 0:hb] = kqp[:, kv * HEAD_DIM:(kv + 1) * HEAD_DIM]
        kq_buf[kv, hb:hb + tq] = kq[:, kv * HEAD_DIM:(kv + 1) * HEAD_DIM]
    vt_buf[:, 0:hb] = vp.T.astype(BF16)
    vt_buf[:, hb:hb + tq] = vt


def _attention_units(q_group, kq_buf, vt_buf, bias_ref, sink_ref, bd, attn_buf, tq, nq, t_first, after_unit=None):
    pw = UNIT_Q
    n_u = tq // nq
    ones_rows = jnp.ones((SUM_ROWS, KEY_WIN), BF16)
    qn_cache = {}

    def scores(i):
        kv, u = divmod(i, n_u)
        if kv not in qn_cache:
            qf = q_group(kv)
            qn_cache[kv] = (qf * _head_inv_rms(qf, bd, two_pass=False)).astype(BF16)
        qn = qn_cache[kv]
        r0 = u * nq
        parts = [qn[r0:r0 + nq, g * HEAD_DIM:(g + 1) * HEAD_DIM] for g in range(GROUP)]
        if nq < pw:
            zpad = jnp.zeros((pw - nq, HEAD_DIM), BF16)
            parts = [x for p_ in parts for x in (p_, zpad)]
        qs = jnp.concatenate(parts, axis=0)
        kw = kq_buf[kv, r0:r0 + KEY_WIN]
        return lax.dot_general(kw, qs, (((1,), (1,)), ((), ())), preferred_element_type=F32)

    def softmax(i, st):
        kv, u = divmod(i, n_u)
        if t_first is not None and u == 0:
            bias = jnp.where(t_first == 0, bias_ref[kv + N_KV], bias_ref[kv])
        else:
            bias = bias_ref[kv]
        st = st + bias
        sink = sink_ref[kv]
        m = jnp.maximum(jnp.max(st, axis=0, keepdims=True), sink)
        return jnp.exp2((st - m).astype(BF16)), jnp.exp2(sink - m)

    def values(i, p, sink_p):
        kv, u = divmod(i, n_u)
        r0 = u * nq
        vt1 = jnp.concatenate([vt_buf[kv * HEAD_DIM:(kv + 1) * HEAD_DIM, r0:r0 + KEY_WIN], ones_rows], axis=0)
        pv = jnp.dot(vt1, p, preferred_element_type=F32)
        ot = pv[:HEAD_DIM] / (pv[HEAD_DIM:HEAD_DIM + 1] + sink_p)
        for gp in range(GROUP // 2):
            blk = jnp.concatenate([ot[:, (2 * gp) * pw:(2 * gp + 1) * pw],
                                   ot[:, (2 * gp + 1) * pw:(2 * gp + 2) * pw]], axis=0)
            c0 = (kv * GROUP + 2 * gp) * HEAD_DIM
            attn_buf[r0:r0 + nq, c0:c0 + 2 * HEAD_DIM] = blk.T[:nq]

    n_units = N_KV * n_u
    ahead = min(SCORE_LOOKAHEAD, n_units)
    queue = [scores(j) for j in range(ahead)]
    pending = None
    for i in range(n_units):
        if i + ahead < n_units:
            queue.append(scores(i + ahead))
        if after_unit is not None:
            after_unit(i)
        probs = softmax(i, queue.pop(0))
        if pending is not None:
            values(i - 1, *pending)
        pending = probs
    values(n_units - 1, *pending)


def _short_conv(u_hist, cu, cw, tq):
    u_all = jnp.concatenate([u_hist, cu], axis=0)
    conv = cw[0:1] * u_all[6:6 + tq] + cw[1:2] * u_all[7:7 + tq] + cw[2:3] * u_all[8:8 + tq]
    return conv, u_all


def _front_body(x_ref, xp_ref, mod_ref, nw_ref, w_ref, knw_ref, qsc_ref, bd_ref, bias_ref, sink_ref, cw_ref,
                merged_ref, knew_ref, vnew_ref, cnew_ref,
                kq_buf, vt_buf, u_buf, attn_buf, pbuf, *, tq, nq):
    hb = WINDOW
    d = D_MODEL
    t = pl.program_id(1)
    bd = bd_ref[...]
    mod = mod_ref[0]

    @pl.when(t == 0)
    def _():
        u_buf[0] = jnp.zeros((8, d), F32)

    def modnorm(x):
        h = x * lax.rsqrt(jnp.mean(x * x, axis=-1, keepdims=True) + RMS_EPS) * nw_ref[...]
        return (h * (1.0 + mod[1:2]) + mod[0:1]).astype(BF16)

    def proj(hrows, c0, width):
        return jnp.dot(hrows, w_ref[:, c0:c0 + width], preferred_element_type=F32)

    kv_p = proj(modnorm(xp_ref[0]), 6 * d, 2 * KV_COLS)
    hbf = modnorm(x_ref[0])
    kv_t = proj(hbf, 6 * d, 2 * KV_COLS)

    assert tq // nq >= 2
    n_units = N_KV * (tq // nq)
    rest_chunks = list(range(d, 6 * d, PROJ_CHUNK))

    def issue(c0, width):
        pbuf[:, c0:c0 + width] = proj(hbf, c0, width)

    issue(0, KV_COLS)
    for c0 in rest_chunks[:PROLOGUE_CHUNKS]:
        issue(c0, PROJ_CHUNK)
    rest_chunks = rest_chunks[PROLOGUE_CHUNKS:]
    n_rest = len(rest_chunks)

    k, v = kv_t[:, :KV_COLS], kv_t[:, KV_COLS:]
    kn = k * _head_inv_rms(k, bd) * knw_ref[...]
    kq = (kn * qsc_ref[...]).astype(BF16)
    kraw = kv_p[:, :KV_COLS]
    kp = kraw * _head_inv_rms(kraw, bd) * knw_ref[...]
    knew_ref[0] = kn[tq - hb:]
    vnew_ref[0] = v[tq - hb:]
    _fill_keys(kq_buf, vt_buf, kp, kv_p[:, KV_COLS:], kq, v.T.astype(BF16), qsc_ref, tq)

    def after_unit(i):
        kv, u = divmod(i, tq // nq)
        if u == 0 and kv + 1 < N_KV:
            issue((kv + 1) * KV_COLS, KV_COLS)
        for c0 in rest_chunks[i * n_rest // n_units:(i + 1) * n_rest // n_units]:
            issue(c0, PROJ_CHUNK)

    _attention_units(lambda kv: pbuf[:, kv * KV_COLS:(kv + 1) * KV_COLS], kq_buf, vt_buf,
                     bias_ref, sink_ref, bd, attn_buf, tq, nq, t, after_unit)

    cu = pbuf[:, 2 * d:3 * d] * pbuf[:, 3 * d:4 * d]
    conv, u_all = _short_conv(u_buf[t % 2], cu, cw_ref[...], tq)
    cnew_ref[0] = u_all[tq + 6:tq + 8]
    u_buf[(t + 1) % 2] = u_all[tq:tq + 8]
    merged = (_sigmoid(pbuf[:, 4 * d:5 * d]) * attn_buf[...]
              + _sigmoid(pbuf[:, 5 * d:6 * d]) * (pbuf[:, d:2 * d] * conv))
    merged_ref[0] = merged.astype(BF16)


def _front(x, mod, nw, w_in_b, knw, qsc, bias_tab, sink_tab, conv_w, tq, nq):
    b, t, d = x.shape
    r = np.arange(KV_COLS) // HEAD_DIM
    bd = jnp.asarray((r[:, None] == r[None, :]).astype(np.float32), BF16)
    const2 = lambda shp: pl.BlockSpec(shp, lambda i, s: (0, 0))
    const3 = lambda shp: pl.BlockSpec(shp, lambda i, s: (0, 0, 0))
    per_b = lambda shp: pl.BlockSpec(shp, lambda i, s: (i, 0, 0))
    kw_ = tq // WINDOW
    in_specs = [pl.BlockSpec((1, tq, d), lambda i, s: (i, s, 0)),
                pl.BlockSpec((1, WINDOW, d), lambda i, s: (i, jnp.maximum(s * kw_ - 1, 0), 0)),
                per_b((1, 6, d)), const2((1, d)),
                pl.BlockSpec((d, IN_COLS), lambda i, s: (0, 0), pipeline_mode=pl.Buffered(1)),
                const2((1, KV_COLS)), const2((1, KV_COLS)), const2((KV_COLS, KV_COLS)),
                const3(bias_tab.shape), const3(sink_tab.shape), const2((3, d))]
    out_shape = (jax.ShapeDtypeStruct((b, t, d), BF16),
                 jax.ShapeDtypeStruct((b, WINDOW, KV_COLS), F32),
                 jax.ShapeDtypeStruct((b, WINDOW, KV_COLS), F32),
                 jax.ShapeDtypeStruct((b, 2, d), F32))
    out_specs = (pl.BlockSpec((1, tq, d), lambda i, s: (i, s, 0)),
                 per_b((1, WINDOW, KV_COLS)), per_b((1, WINDOW, KV_COLS)), per_b((1, 2, d)))
    return pl.pallas_call(
        functools.partial(_front_body, tq=tq, nq=nq),
        out_shape=out_shape,
        grid=(b, t // tq),
        in_specs=in_specs,
        out_specs=out_specs,
        scratch_shapes=[pltpu.VMEM((N_KV, WINDOW + tq, HEAD_DIM), BF16),
                        pltpu.VMEM((KV_COLS, WINDOW + tq), BF16),
                        pltpu.VMEM((2, 8, d), F32),
                        pltpu.VMEM((tq, d), F32),
                        pltpu.VMEM((tq, 6 * d), F32)],
        compiler_params=_cparams(("arbitrary", "arbitrary")),
        name="front",
    )(x, x, mod, nw, w_in_b, knw, qsc, bd, bias_tab, sink_tab, conv_w)


def _mixer(proj, knw, qsc, bias_tab, sink_tab, conv_w, state, tq, nq):
    b, t, _ = proj.shape
    d = D_MODEL
    stateful = state is not None
    key_rows = max(WINDOW + tq, KEY_WIN)
    r = np.arange(KV_COLS) // HEAD_DIM
    bd = jnp.asarray((r[:, None] == r[None, :]).astype(np.float32), BF16)
    wide = lambda j: pl.BlockSpec((1, tq, d), lambda i, s, j=j: (i, s, j))
    kvspec = lambda j: pl.BlockSpec((1, tq, KV_COLS), lambda i, s, j=j: (i, s, j))
    const2 = lambda shp: pl.BlockSpec(shp, lambda i, s: (0, 0))
    const3 = lambda shp: pl.BlockSpec(shp, lambda i, s: (0, 0, 0))
    per_b = lambda shp: pl.BlockSpec(shp, lambda i, s: (i, 0, 0))
    kvblk = 6 * d // KV_COLS
    if stateful:
        hist_specs = [per_b((1, WINDOW, KV_COLS)), per_b((1, WINDOW, KV_COLS)), per_b((1, 2, d)), per_b((1, 2, d))]
        hist_args = [state[0], state[1], state[2], state[2]]
    else:
        kw_ = tq // WINDOW
        prev_kv = lambda j: pl.BlockSpec((1, WINDOW, KV_COLS),
                                         lambda i, s, j=j: (i, jnp.maximum(s * kw_ - 1, 0), j))
        prev8 = lambda j: pl.BlockSpec((1, 8, d), lambda i, s, j=j: (i, jnp.maximum(s * (tq // 8) - 1, 0), j))
        hist_specs = [prev_kv(kvblk), prev_kv(kvblk + 1), prev8(2), prev8(3)]
        hist_args = [proj, proj, proj, proj]
    in_specs = [wide(0), kvspec(kvblk), kvspec(kvblk + 1), wide(1), wide(2), wide(3), wide(4), wide(5),
                const2((1, KV_COLS)), const2((1, KV_COLS)), const2((KV_COLS, KV_COLS)),
                const3(bias_tab.shape), const3(sink_tab.shape), const2((3, d))] + hist_specs
    out_shape = (jax.ShapeDtypeStruct((b, t, d), BF16),
                 jax.ShapeDtypeStruct((b, WINDOW, KV_COLS), F32),
                 jax.ShapeDtypeStruct((b, WINDOW, KV_COLS), F32),
                 jax.ShapeDtypeStruct((b, 2, d), F32))
    out_specs = (pl.BlockSpec((1, tq, d), lambda i, s: (i, s, 0)),
                 per_b((1, WINDOW, KV_COLS)), per_b((1, WINDOW, KV_COLS)), per_b((1, 2, d)))
    return pl.pallas_call(
        functools.partial(_mixer_body, tq=tq, nq=nq, stateful=stateful),
        out_shape=out_shape,
        grid=(b, t // tq),
        in_specs=in_specs,
        out_specs=out_specs,
        scratch_shapes=[pltpu.VMEM((N_KV, key_rows, HEAD_DIM), BF16),
                        pltpu.VMEM((KV_COLS, key_rows), BF16),
                        pltpu.VMEM((tq, d), F32)],
        compiler_params=_cparams(("arbitrary", "arbitrary")),
        name="mixer_state" if stateful else "mixer",
    )(proj, proj, proj, proj, proj, proj, proj, proj, knw, qsc, bd, bias_tab, sink_tab, conv_w, *hist_args)


def _route(logits):
    lane = lax.broadcasted_iota(I32, logits.shape, 1).astype(F32)
    neg = -jnp.inf
    big = float(1 << 20)
    gl = jnp.where(lane < N_GROUPS, logits, neg)
    gmax = jnp.max(gl, axis=-1, keepdims=True)
    g_idx = jnp.min(jnp.where(gl == gmax, lane, big), axis=-1, keepdims=True)
    g_w = 1.0 / jnp.sum(jnp.exp(gl - gmax), axis=-1, keepdims=True)
    lo = N_GROUPS + g_idx * EPG
    el = jnp.where((lane >= lo) & (lane < lo + EPG), logits, neg)
    m1 = jnp.max(el, axis=-1, keepdims=True)
    i1 = jnp.min(jnp.where(el == m1, lane, big), axis=-1, keepdims=True)
    el2 = jnp.where(lane == i1, neg, el)
    m2 = jnp.max(el2, axis=-1, keepdims=True)
    i2 = jnp.min(jnp.where(el2 == m2, lane, big), axis=-1, keepdims=True)
    r = jnp.exp(m2 - m1)
    w1 = 1.0 / (1.0 + r)
    w2 = r / (1.0 + r)
    return i1 - N_GROUPS, i2 - N_GROUPS, g_w * w1, g_w * w2


def _outproj_kernel(m_ref, x_ref, mod_ref, wo_ref, nw_ref, wr_ref, br_ref,
                    x1_ref, h2_ref, e01_ref, ew_ref):
    mod = mod_ref[0]
    sub = ROW_TILE
    n_sub = x_ref.shape[1] // sub
    rows = lambda s: slice(s * sub, (s + 1) * sub)
    modrow = lambda k, s: mod[k] if mod.shape[1] == 1 else mod[k, rows(s)]

    mix = [jnp.dot(m_ref[0, rows(s)], wo_ref[...], preferred_element_type=F32) for s in range(n_sub)]
    prod = []
    lane8 = lax.broadcasted_iota(I32, (sub, 8), 1)

    def route(s):
        logits = prod[s][:sub, :LANES] + (prod[s][:sub, LANES:] + prod[s][sub:, :LANES]) + br_ref[...]
        e1, e2, w1, w2 = _route(logits)
        e01_ref[s] = jnp.concatenate([_col_to_row(e1), _col_to_row(e2)], axis=0).astype(I32)
        ew_ref[rows(s)] = jnp.where(lane8 == 0, w1, jnp.where(lane8 == 1, w2, 0.0))

    for s in range(n_sub):
        x1 = x_ref[0, rows(s)] + modrow(2, s) * mix[s]
        x1_ref[0, rows(s)] = x1
        h = x1 * lax.rsqrt(jnp.mean(x1 * x1, axis=-1, keepdims=True) + RMS_EPS) * nw_ref[...]
        h = h * (1.0 + modrow(4, s)) + modrow(3, s)
        h2_ref[rows(s)] = _pack_pairs(h)
        h_hi, h_lo = _split_bf16(h)
        prod.append(jnp.dot(jnp.concatenate([h_hi, h_lo], axis=0), wr_ref[...], preferred_element_type=F32))
        if s > 0:
            route(s - 1)
    route(n_sub - 1)


def _outproj(merged, x, mod4, w_out_b, nw, w_r, b_r, tm):
    b, t, d = x.shape
    nt = t // tm
    mr = mod4.shape[2]
    assert mr == 1 or (mr == t and nt == 1)
    n_sub = tm // ROW_TILE
    assert n_sub * ROW_TILE == tm
    flat = lambda i, j: (i * nt + j, 0)
    return pl.pallas_call(
        _outproj_kernel,
        out_shape=(jax.ShapeDtypeStruct((b, t, d), F32),
                   jax.ShapeDtypeStruct((b * t, HALF), I32),
                   jax.ShapeDtypeStruct((b * nt * n_sub, 2, ROW_TILE), I32),
                   jax.ShapeDtypeStruct((b * t, 8), F32)),
        grid=(b, nt),
        in_specs=[pl.BlockSpec((1, tm, d), lambda i, j: (i, j, 0)),
                  pl.BlockSpec((1, tm, d), lambda i, j: (i, j, 0)),
                  pl.BlockSpec((1, 6, mr, d), lambda i, j: (i, 0, 0, 0)),
                  pl.BlockSpec((d, d), lambda i, j: (0, 0)),
                  pl.BlockSpec((1, d), lambda i, j: (0, 0)),
                  pl.BlockSpec((d, 2 * LANES), lambda i, j: (0, 0)),
                  pl.BlockSpec((1, LANES), lambda i, j: (0, 0))],
        out_specs=(pl.BlockSpec((1, tm, d), lambda i, j: (i, j, 0)),
                   pl.BlockSpec((tm, HALF), flat),
                   pl.BlockSpec((n_sub, 2, ROW_TILE), lambda i, j: (i * nt + j, 0, 0)),
                   pl.BlockSpec((tm, 8), flat)),
        compiler_params=_cparams(("arbitrary", "arbitrary")),
        name="outproj",
    )(merged, x, mod4, w_out_b, nw, w_r, b_r)


def _col_to_row(col):
    eye = lax.broadcasted_iota(I32, (LANES, LANES), 0) == lax.broadcasted_iota(I32, (LANES, LANES), 1)
    parts = [jnp.sum(jnp.where(eye, col[r * LANES:(r + 1) * LANES], 0.0), axis=0, keepdims=True)
             for r in range(col.shape[0] // LANES)]
    return jnp.concatenate(parts, axis=1)


def _rank_kernel(e_ref, tri_ref, low_ref, d_ref, tot_ref, *, block):
    n_sub, _, t = e_ref.shape
    sub = lax.broadcasted_iota(I32, (LANES, t), 0)

    def hots(s):
        e = e_ref[s]
        return sub == e[0:1], sub == e[1:2]

    def count(s, cnt):
        h0, h1 = hots(s)
        return cnt + jnp.sum(jnp.where(h0 | h1, 1.0, 0.0), axis=1, keepdims=True)

    cnt = lax.fori_loop(0, n_sub, count, jnp.zeros((LANES, 1), F32))
    tot_ref[...] = _col_to_row(cnt).astype(I32)
    nblk = jnp.floor((cnt + (block - 1)) * (1.0 / block))
    hi = jnp.floor(nblk * (1.0 / 16.0))
    lo = nblk - hi * 16.0
    low = low_ref[...]
    bcast = lambda c: jnp.broadcast_to(c, (LANES, LANES)).astype(BF16)
    excl = (jnp.dot(low, bcast(hi), preferred_element_type=F32) * 16.0
            + jnp.dot(low, bcast(lo), preferred_element_type=F32))
    starts = excl[:, 0:1] * float(block)

    def place(s, running):
        h0, h1 = hots(s)
        onehot = jnp.where(h0 | h1, 1.0, 0.0)
        prefix = jnp.dot(onehot.astype(BF16), tri_ref[...], preferred_element_type=F32)
        pos = prefix + running
        d0 = jnp.sum(jnp.where(h0, pos, 0.0), axis=0, keepdims=True)
        d1 = jnp.sum(jnp.where(h1, pos, 0.0), axis=0, keepdims=True)
        d_ref[s] = jnp.concatenate([d0, d1], axis=0).astype(I32)
        return running + jnp.sum(onehot, axis=1, keepdims=True)

    lax.fori_loop(0, n_sub, place, starts)


def _rank(e01, block):
    n_sub, _, t = e01.shape
    r = np.arange(t)
    tri = jnp.asarray((r[:, None] < r[None, :]).astype(np.float32), BF16)
    l = np.arange(LANES)
    low = jnp.asarray((l[None, :] < l[:, None]).astype(np.float32), BF16)
    return pl.pallas_call(
        functools.partial(_rank_kernel, block=block),
        out_shape=(jax.ShapeDtypeStruct((n_sub, 2, t), I32), jax.ShapeDtypeStruct((1, LANES), I32)),
        compiler_params=pltpu.CompilerParams(vmem_limit_bytes=VMEM_LIMIT),
        name="rank",
    )(e01, tri, low)


def _expert_kernel(start_ref, nblk_ref, xs_hbm, wg_ref, wu_ref, wd_ref, ys_hbm,
                   xbuf, ybuf, wg_s, wu_s, wd_s, sem_in, sem_out, *, block):
    nbuf = EXPERT_BUFS
    e = pl.program_id(0)
    n = nblk_ref[e]
    base = start_ref[e]
    total = start_ref[N_EXPERTS - 1] + nblk_ref[N_EXPERTS - 1]

    def in_copy(g):
        rows = pl.ds(pl.multiple_of(g * block, block), block)
        return pltpu.make_async_copy(xs_hbm.at[rows], xbuf.at[g % nbuf], sem_in.at[g % nbuf])

    def out_copy(g):
        rows = pl.ds(pl.multiple_of(g * block, block), block)
        return pltpu.make_async_copy(ybuf.at[g % nbuf], ys_hbm.at[rows], sem_out.at[g % nbuf])

    @pl.when(e == 0)
    def _():
        for g0 in range(nbuf - 1):
            @pl.when(g0 < total)
            def _(g0=g0):
                in_copy(g0).start()

    @pl.when(n > 0)
    def _():
        wg_s[...] = wg_ref[0].astype(BF16)
        wu_s[...] = wu_ref[0].astype(BF16)
        wd_s[...] = wd_ref[0].astype(BF16)

        def body(i, carry):
            g = base + i
            slot = g % nbuf
            in_copy(g).wait()

            @pl.when(g + nbuf - 1 < total)
            def _():
                in_copy(g + nbuf - 1).start()

            @pl.when(g >= nbuf)
            def _():
                out_copy(g - nbuf).wait()

            a, c = _unpack_pairs(xbuf[slot])
            x = jnp.concatenate([a.astype(BF16), c.astype(BF16)], axis=1)
            gate = jnp.dot(x, wg_s[...], preferred_element_type=F32)
            up = jnp.dot(x, wu_s[...], preferred_element_type=F32)
            hmid = (gate * _sigmoid(gate) * up).astype(BF16)
            ybuf[slot] = _pack_pairs(jnp.dot(hmid, wd_s[...], preferred_element_type=F32))
            out_copy(g).start()
            return carry

        lax.fori_loop(0, n, body, 0)

    @pl.when(e == N_EXPERTS - 1)
    def _():
        for back in range(nbuf, 0, -1):
            @pl.when(total >= back)
            def _(back=back):
                out_copy(total - back).wait()


def _experts(xs, start_blk, nblk, w_gate, w_up, w_down, block):
    n_rows = xs.shape[0]
    wblk = lambda e, st, nb: (e, 0, 0)
    grid_spec = pltpu.PrefetchScalarGridSpec(
        num_scalar_prefetch=2,
        grid=(N_EXPERTS,),
        in_specs=[pl.BlockSpec(memory_space=pl.ANY),
                  pl.BlockSpec((1, D_MODEL, D_EXPERT), wblk),
                  pl.BlockSpec((1, D_MODEL, D_EXPERT), wblk),
                  pl.BlockSpec((1, D_EXPERT, D_MODEL), wblk)],
        out_specs=pl.BlockSpec(memory_space=pl.ANY),
        scratch_shapes=[pltpu.VMEM((EXPERT_BUFS, block, HALF), I32),
                        pltpu.VMEM((EXPERT_BUFS, block, HALF), I32),
                        pltpu.VMEM((D_MODEL, D_EXPERT), BF16),
                        pltpu.VMEM((D_MODEL, D_EXPERT), BF16),
                        pltpu.VMEM((D_EXPERT, D_MODEL), BF16),
                        pltpu.SemaphoreType.DMA((EXPERT_BUFS,)),
                        pltpu.SemaphoreType.DMA((EXPERT_BUFS,))])
    return pl.pallas_call(
        functools.partial(_expert_kernel, block=block),
        out_shape=jax.ShapeDtypeStruct((n_rows, HALF), I32),
        grid_spec=grid_spec,
        compiler_params=_cparams(("arbitrary",)),
        name="experts",
    )(start_blk, nblk, xs, w_gate, w_up, w_down)


def _final_kernel(x1_ref, y0_ref, y1_ref, ew_ref, mod_ref, o_ref):
    a0, b0 = _unpack_pairs(y0_ref[...])
    a1, b1 = _unpack_pairs(y1_ref[...])
    w0 = ew_ref[:, 0:1]
    w1 = ew_ref[:, 1:2]
    moe = jnp.concatenate([w0 * a0 + w1 * a1, w0 * b0 + w1 * b1], axis=1)
    o_ref[0] = x1_ref[0] + mod_ref[0][5:6] * moe


def _final(x1, y0, y1, ew, mod, tm, b0=0, nb=None, y_prev=None):
    b, t, d = x1.shape
    nb = b if nb is None else nb
    nt = t // tm
    local = lambda i, j: (i * nt + j, 0)
    glob = lambda i, j: ((i + b0) * nt + j, 0)
    rows3 = lambda i, j: (i + b0, j, 0)
    in_specs = [pl.BlockSpec((1, tm, d), rows3),
                pl.BlockSpec((tm, HALF), local),
                pl.BlockSpec((tm, HALF), local),
                pl.BlockSpec((tm, 8), glob),
                pl.BlockSpec((1, 6, d), lambda i, j: (i + b0, 0, 0))]
    args = [x1, y0, y1, ew, mod]
    aliases = {}
    kern = _final_kernel
    if y_prev is not None:
        in_specs.append(pl.BlockSpec(memory_space=pl.ANY))
        args.append(y_prev)
        aliases = {5: 0}
        kern = lambda *refs: _final_kernel(*refs[:5], refs[6])
    return pl.pallas_call(
        kern,
        out_shape=jax.ShapeDtypeStruct((b, t, d), F32),
        grid=(nb, nt),
        in_specs=in_specs,
        out_specs=pl.BlockSpec((1, tm, d), rows3),
        input_output_aliases=aliases,
        compiler_params=_cparams(("arbitrary", "arbitrary")),
        name="final",
    )(*args)


def _sc_window(rows_per_worker):
    for w in range(SC_MAX_WINDOW, 7, -8):
        if rows_per_worker % w == 0:
            return w
    raise ValueError(f"no SparseCore window divides {rows_per_worker} rows per worker")


def _sc_split(idx):
    n = idx.shape[0]
    per = n // SC_WORKERS
    assert per * SC_WORKERS == n
    win = _sc_window(per)
    return idx.reshape(SC_WORKERS, per // win, win), per // win, win


def _sc_worker_id():
    return lax.axis_index("s") * SC_CORES + lax.axis_index("c")


def _dispatch_rows(h2_groups, dest_groups, n_rows):
    splits = [(_sc_split(d0), _sc_split(d1)) for d0, d1 in dest_groups]
    ng = len(h2_groups)
    scratch = []
    for (_, _, win), _ in splits:
        scratch += [pltpu.VMEM((win,), I32), pltpu.VMEM((win,), I32), pltpu.VMEM((win, HALF), I32)]

    @functools.partial(
        pl.kernel,
        mesh=plsc.VectorSubcoreMesh(core_axis_name="c", subcore_axis_name="s"),
        out_type=jax.ShapeDtypeStruct((n_rows, HALF), I32),
        scratch_types=scratch,
        name="sc_dispatch",
    )
    def k(*refs):
        x_refs, idx_refs, o_hbm, bufs = refs[:ng], refs[ng:3 * ng], refs[3 * ng], refs[3 * ng + 1:]
        wid = _sc_worker_id()
        for g in range(ng):
            (_, nwin, win), _ = splits[g]
            x_hbm, d0_hbm, d1_hbm = x_refs[g], idx_refs[2 * g], idx_refs[2 * g + 1]
            i0_v, i1_v, rows_v = bufs[3 * g:3 * g + 3]

            @pl.loop(0, nwin)
            def _(j, nwin=nwin, win=win, x_hbm=x_hbm, d0_hbm=d0_hbm, d1_hbm=d1_hbm,
                  i0_v=i0_v, i1_v=i1_v, rows_v=rows_v):
                base = pl.multiple_of((wid * nwin + j) * win, 8)
                pltpu.sync_copy(d0_hbm.at[wid, j], i0_v)
                pltpu.sync_copy(d1_hbm.at[wid, j], i1_v)
                pltpu.sync_copy(x_hbm.at[pl.ds(base, win)], rows_v)
                pltpu.sync_copy(rows_v, o_hbm.at[i0_v])
                pltpu.sync_copy(rows_v, o_hbm.at[i1_v])

    idx_args = []
    for (s0, s1) in splits:
        idx_args += [s0[0], s1[0]]
    return k(*h2_groups, *idx_args)


def _collect_rows(ys, dest_groups):
    splits = [(_sc_split(d0), _sc_split(d1)) for d0, d1 in dest_groups]
    ng = len(dest_groups)
    outs, scratch = [], []
    for (d0, _), ((_, _, win), _) in zip(dest_groups, splits):
        o = jax.ShapeDtypeStruct((d0.shape[0], HALF), I32)
        outs += [o, o]
        scratch += [pltpu.VMEM((win,), I32), pltpu.VMEM((win, HALF), I32)]

    @functools.partial(
        pl.kernel,
        mesh=plsc.VectorSubcoreMesh(core_axis_name="c", subcore_axis_name="s"),
        out_type=tuple(outs),
        scratch_types=scratch,
        name="sc_collect",
    )
    def k(*refs):
        ys_hbm, idx_refs, out_refs, bufs = refs[0], refs[1:1 + 2 * ng], refs[1 + 2 * ng:1 + 4 * ng], refs[1 + 4 * ng:]
        wid = _sc_worker_id()
        for g in range(ng):
            (_, nwin, win), _ = splits[g]
            i_v, rows_v = bufs[2 * g:2 * g + 2]
            for kk in range(2):
                d_hbm, y_hbm = idx_refs[2 * g + kk], out_refs[2 * g + kk]

                @pl.loop(0, nwin)
                def _(j, nwin=nwin, win=win, d_hbm=d_hbm, y_hbm=y_hbm, i_v=i_v, rows_v=rows_v):
                    base = pl.multiple_of((wid * nwin + j) * win, 8)
                    pltpu.sync_copy(d_hbm.at[wid, j], i_v)
                    pltpu.sync_copy(ys_hbm.at[i_v], rows_v)
                    pltpu.sync_copy(rows_v, y_hbm.at[pl.ds(base, win)])

    idx_args = []
    for (s0, s1) in splits:
        idx_args += [s0[0], s1[0]]
    res = k(ys, *idx_args)
    return [(res[2 * g], res[2 * g + 1]) for g in range(ng)]


def _t5_bucket(rel):
    half = N_BUCKETS // 2
    max_exact = half // 2
    n = jnp.abs(rel)
    far = max_exact + (jnp.log(jnp.maximum(n, 1).astype(F32) / max_exact)
                       / math.log(MAX_DISTANCE / max_exact) * (half - max_exact)).astype(I32)
    far = jnp.minimum(far, half - 1)
    return jnp.where(rel > 0, half, 0) + jnp.where(n < max_exact, n, far)


def _bias_table(rel_bias, cq, nq, with_no_history):
    nk = WINDOW + cq
    j = jnp.arange(KEY_WIN)[:, None]
    c = jnp.arange(UNIT_Q)[None, :]
    jj = j - (c // cq) * cq
    valid = (jj >= 0) & (jj < nk) & (c < nq)
    rel = jj - WINDOW - (c % cq)
    onehot = (_t5_bucket(rel)[:, :, None] == jnp.arange(N_BUCKETS)).astype(F32)
    vals = jnp.einsum("jcb,bh->jch", onehot, rel_bias.astype(F32), precision=lax.Precision.HIGHEST) * LOG2E

    def table(mask):
        b = jnp.where(mask[:, :, None], vals, -jnp.inf)
        b = jnp.transpose(b.reshape(KEY_WIN, UNIT_Q, N_KV, GROUP), (2, 0, 3, 1))
        return b.reshape(N_KV, KEY_WIN, GROUP * UNIT_Q)

    if not with_no_history:
        return table(valid)
    return jnp.concatenate([table(valid), table(valid & (j >= WINDOW))], axis=0)


def _sink_table(sinks):
    s = sinks.astype(F32).reshape(N_KV, 1, GROUP, 1)
    return jnp.broadcast_to(s * LOG2E, (N_KV, 1, GROUP, UNIT_Q)).reshape(N_KV, 1, GROUP * UNIT_Q)


def kernel(x_prompt, x_sample, state_attn_k, state_attn_v, state_conv, c_prompt, c_sample,
           rel_bias, w_ada, b_ada, norm1_w, w_in, q_norm_w, k_norm_w, attn_sinks, conv_w,
           w_out, norm2_w, w_router_group, b_router_group, w_router_expert, b_router_expert,
           w_gate, w_up, w_down):
    depth = w_ada.shape[0]
    assert depth == 1
    bp, tp, d = x_prompt.shape
    bs, ts, _ = x_sample.shape
    n_p, n_s = bp * tp, bs * ts
    n_tok = n_p + n_s
    l = 0

    wi = w_in[l]
    qw, kw, vw, rest = wi[:, :d], wi[:, d:d + KV_COLS], wi[:, d + KV_COLS:d + 2 * KV_COLS], wi[:, d + 2 * KV_COLS:]
    w_in_b = jnp.concatenate([qw, rest, kw, vw], axis=1).astype(BF16)
    w_out_b = w_out[l].astype(BF16)
    w_r = jnp.concatenate([w_router_group[l],
                           jnp.transpose(w_router_expert[l], (1, 0, 2)).reshape(d, N_EXPERTS),
                           jnp.zeros((d, LANES - N_GROUPS - N_EXPERTS), F32)], axis=1)
    w_r_hi = lax.reduce_precision(w_r, exponent_bits=8, mantissa_bits=7)
    w_r = jnp.concatenate([w_r_hi.astype(BF16), (w_r - w_r_hi).astype(BF16)], axis=1)
    b_r = jnp.concatenate([b_router_group[l], b_router_expert[l].reshape(-1),
                           jnp.zeros((LANES - N_GROUPS - N_EXPERTS,), F32)]).reshape(1, LANES)
    knw = jnp.tile(k_norm_w[l], N_KV).reshape(1, KV_COLS)
    qsc = jnp.tile(q_norm_w[l] * (HEAD_DIM ** -0.5 * LOG2E), N_KV).reshape(1, KV_COLS)
    n1w = norm1_w[l].reshape(1, d)
    n2w = norm2_w[l].reshape(1, d)

    mod = _ada(jnp.concatenate([c_prompt, c_sample], axis=0), w_ada[l], b_ada[l]).reshape(bp + bs, 6, d)
    mod_p, mod_s = mod[:bp], mod[bp:]

    xs_rows = x_sample.reshape(1, n_s, d)
    mod4_p = mod_p[:, :, None, :]
    mod4_s = jnp.repeat(jnp.transpose(mod_s, (1, 0, 2)), ts, axis=1)[None]
    proj_s = _inproj(xs_rows, mod4_s, n1w, w_in_b, n_s).reshape(bs, ts, IN_COLS)
    sink_tab = _sink_table(attn_sinks[l])
    bias_p = _bias_table(rel_bias, CHUNK, UNIT_Q, True)
    merged_p, k_p, v_p, c_p = _front(x_prompt, mod_p, n1w, w_in_b, knw, qsc, bias_p, sink_tab, conv_w[l],
                                     MIX_TILE, UNIT_Q)
    state = (state_attn_k[l].reshape(bs, WINDOW, KV_COLS), state_attn_v[l].reshape(bs, WINDOW, KV_COLS),
             state_conv[l])
    merged_s, k_s, v_s, c_s = _mixer(proj_s, knw, qsc, _bias_table(rel_bias, ts, ts, False), sink_tab, conv_w[l],
                                     state, ts, ts)

    x1_p, h2_p, e01_p, ew_p = _outproj(merged_p, x_prompt, mod4_p, w_out_b, n2w, w_r, b_r, OUT_TILE)
    x1_s, h2_s, e01_s, ew_s = _outproj(merged_s.reshape(1, n_s, d), xs_rows, mod4_s, w_out_b, n2w, w_r, b_r, n_s)
    x1_s = x1_s.reshape(bs, ts, d)

    assert n_s == ROW_TILE
    d01, totals = _rank(jnp.concatenate([e01_p, e01_s], axis=0), EXPERT_BLOCK)
    n_sub_p = n_p // ROW_TILE
    dests = [(d01[:n_sub_p, 0].reshape(-1), d01[:n_sub_p, 1].reshape(-1)),
             (d01[n_sub_p:, 0].reshape(-1), d01[n_sub_p:, 1].reshape(-1))]
    nblk = (totals[0, :N_EXPERTS] + EXPERT_BLOCK - 1) // EXPERT_BLOCK
    start_blk = (jnp.cumsum(nblk) - nblk).astype(I32)
    nb_max = -(-2 * n_tok // EXPERT_BLOCK) + N_EXPERTS

    xs = _dispatch_rows([h2_p, h2_s], dests, nb_max * EXPERT_BLOCK)
    ys = _experts(xs, start_blk, nblk.astype(I32), w_gate[l], w_up[l], w_down[l], EXPERT_BLOCK)
    (d0_p, d1_p), dest_s = dests
    assert bp % COLLECT_PARTS == 0
    nbp = bp // COLLECT_PARTS
    rows = nbp * tp
    y_p = None
    for part in range(COLLECT_PARTS):
        sl = slice(part * rows, (part + 1) * rows)
        groups = [(d0_p[sl], d1_p[sl])] + ([dest_s] if part == 0 else [])
        got = _collect_rows(ys, groups)
        if part == 0:
            y0_s, y1_s = got[1]
        y_p = _final(x1_p, got[0][0], got[0][1], ew_p, mod_p, ROW_TILE, part * nbp, nbp, y_p)
    y_s = _final(x1_s, y0_s, y1_s, ew_s, mod_s, ts)

    kv_shape = (1, -1, WINDOW, N_KV, HEAD_DIM)
    return (y_p, y_s, k_p.reshape(kv_shape), v_p.reshape(kv_shape), c_p[None],
            k_s.reshape(kv_shape), v_s.reshape(kv_shape), c_s[None])
```

```python
import functools
import math

import numpy as np
import jax
import jax.numpy as jnp
from jax import lax
from jax.experimental import pallas as pl
from jax.experimental.pallas import tpu as pltpu
from jax.experimental.pallas import tpu_sc as plsc

F32 = jnp.float32
BF16 = jnp.bfloat16
I32 = jnp.int32

D_MODEL = 1024
HEAD_DIM = 64
N_HEADS = 16
N_KV = 4
GROUP = 4
CHUNK = 64
WINDOW = 128
N_BUCKETS = 32
MAX_DISTANCE = 128
N_GROUPS = 8
EPG = 8
N_EXPERTS = 64
D_EXPERT = 512
RMS_EPS = 1e-6
LOG2E = math.log2(math.e)
SCORE_LOOKAHEAD = 1
PROLOGUE_CHUNKS = 4
SUM_ROWS = 16
KV_COLS = N_KV * HEAD_DIM
IN_COLS = 6 * D_MODEL + 2 * KV_COLS
HALF = D_MODEL // 2
LANES = 128

VMEM_LIMIT = 56 * 1024 * 1024
ADA_TN = 2048
INPROJ_TN = 512
ROW_TILE = 512
OUT_TILE = 1024
MIX_TILE = 512
UNIT_Q = 2 * CHUNK
KEY_WIN = WINDOW + UNIT_Q
PROJ_CHUNK = 256
EXPERT_BLOCK = 512
COLLECT_PARTS = 4
EXPERT_BUFS = 4
SC_CORES = 2
SC_SUBCORES = 16
SC_WORKERS = SC_CORES * SC_SUBCORES
SC_MAX_WINDOW = 128


def _cparams(sem):
    return pltpu.CompilerParams(dimension_semantics=sem, vmem_limit_bytes=VMEM_LIMIT)


def _split_bf16(a):
    hi = a.astype(BF16)
    lo = (a - hi.astype(F32)).astype(BF16)
    return hi, lo


def _dot3(a, b):
    ah, al = _split_bf16(a)
    bh, bl = _split_bf16(b)
    d = functools.partial(jnp.dot, preferred_element_type=F32)
    return d(ah, bh) + (d(ah, bl) + d(al, bh))


def _sigmoid(x):
    return 0.5 * jnp.tanh(0.5 * x) + 0.5


def _pack_pairs(y):
    a = lax.bitcast_convert_type(y[:, :HALF].astype(BF16).astype(F32), I32)
    b = lax.bitcast_convert_type(y[:, HALF:].astype(BF16).astype(F32), I32)
    return a | lax.shift_right_logical(b, jnp.int32(16))


def _unpack_pairs(w):
    a = lax.bitcast_convert_type(w & jnp.int32(-65536), F32)
    b = lax.bitcast_convert_type(lax.shift_left(w, jnp.int32(16)), F32)
    return a, b


def _ada_kernel(c_ref, w_ref, b_ref, o_ref):
    c = c_ref[...]
    s = c * jax.nn.sigmoid(c)
    o_ref[...] = _dot3(s, w_ref[...]) + b_ref[...]


def _ada(c_all, w_ada, b_ada):
    r, d = c_all.shape
    n = w_ada.shape[1]
    tn = ADA_TN
    return pl.pallas_call(
        _ada_kernel,
        out_shape=jax.ShapeDtypeStruct((r, n), F32),
        grid=(n // tn,),
        in_specs=[pl.BlockSpec((r, d), lambda j: (0, 0)),
                  pl.BlockSpec((d, tn), lambda j: (0, j)),
                  pl.BlockSpec((1, tn), lambda j: (0, j))],
        out_specs=pl.BlockSpec((r, tn), lambda j: (0, j)),
        compiler_params=_cparams(("arbitrary",)),
        name="ada",
    )(c_all, w_ada, b_ada.reshape(1, n))


def _inproj_kernel(x_ref, mod_ref, nw_ref, w_ref, o_ref):
    x = x_ref[0]
    mod = mod_ref[0]
    h = x * lax.rsqrt(jnp.mean(x * x, axis=-1, keepdims=True) + RMS_EPS) * nw_ref[...]
    h = h * (1.0 + mod[1]) + mod[0]
    hb = h.astype(BF16)
    for j in range(IN_COLS // INPROJ_TN):
        sl = slice(j * INPROJ_TN, (j + 1) * INPROJ_TN)
        o_ref[0, :, sl] = jnp.dot(hb, w_ref[:, sl], preferred_element_type=F32).astype(BF16)


def _inproj(x, mod4, nw, w_in_b, tm):
    b, t, d = x.shape
    mr = mod4.shape[2]
    assert mr == 1 or (mr == t and tm == t)
    return pl.pallas_call(
        _inproj_kernel,
        out_shape=jax.ShapeDtypeStruct((b, t, IN_COLS), BF16),
        grid=(b, t // tm),
        in_specs=[pl.BlockSpec((1, tm, d), lambda i, j: (i, j, 0)),
                  pl.BlockSpec((1, 6, mr, d), lambda i, j: (i, 0, 0, 0)),
                  pl.BlockSpec((1, d), lambda i, j: (0, 0)),
                  pl.BlockSpec((d, IN_COLS), lambda i, j: (0, 0), pipeline_mode=pl.Buffered(1))],
        out_specs=pl.BlockSpec((1, tm, IN_COLS), lambda i, j: (i, j, 0)),
        compiler_params=_cparams(("arbitrary", "arbitrary")),
        name="inproj",
    )(x, mod4, nw, w_in_b)


def _head_inv_rms(xf, bd, two_pass=True):
    sq = xf * xf
    if two_pass:
        hi, lo = _split_bf16(sq)
        ssq = jnp.dot(hi, bd, preferred_element_type=F32) + jnp.dot(lo, bd, preferred_element_type=F32)
    else:
        ssq = jnp.dot(sq.astype(BF16), bd, preferred_element_type=F32)
    return lax.rsqrt(ssq * (1.0 / HEAD_DIM) + RMS_EPS)


def _mixer_body(q_ref, k_ref, v_ref, bg_ref, c_ref, u_ref, ga_ref, gc_ref,
                knw_ref, qsc_ref, bd_ref, bias_ref, sink_ref, cw_ref,
                kpast_ref, vpast_ref, cpast_ref,
                merged_ref, knew_ref, vnew_ref, cnew_ref,
                kq_buf, vt_buf, attn_buf, *, tq, nq):
    hb = WINDOW
    bd = bd_ref[...]

    k = k_ref[0].astype(F32)
    kn = k * _head_inv_rms(k, bd) * knw_ref[...]
    kq = (kn * qsc_ref[...]).astype(BF16)
    vb = v_ref[0]
    vt = vb.astype(F32).T.astype(BF16)

    kp = kpast_ref[0]
    vp = vpast_ref[0]
    for kv in range(N_KV):
        kq_buf[kv, hb + tq:] = jnp.zeros((KEY_WIN - hb - tq, HEAD_DIM), BF16)
    vt_buf[:, hb + tq:] = jnp.zeros((KV_COLS, KEY_WIN - hb - tq), BF16)
    u_hist = jnp.concatenate([jnp.zeros((6, D_MODEL), F32), cpast_ref[0]], axis=0)
    knew_ref[0] = jnp.concatenate([kp[tq:], kn], axis=0)
    vnew_ref[0] = jnp.concatenate([vp[tq:], vb.astype(F32)], axis=0)
    _fill_keys(kq_buf, vt_buf, kp, vp, kq, vt, qsc_ref, tq)
    q = q_ref[0]
    _attention_units(lambda kv: q[:, kv * KV_COLS:(kv + 1) * KV_COLS].astype(F32), kq_buf, vt_buf,
                     bias_ref, sink_ref, bd, attn_buf, tq, nq, None)
    cu = c_ref[0].astype(F32) * u_ref[0].astype(F32)
    conv, u_all = _short_conv(u_hist, cu, cw_ref[...], tq)
    cnew_ref[0] = u_all[tq + 6:tq + 8]
    merged = (_sigmoid(ga_ref[0].astype(F32)) * attn_buf[...]
              + _sigmoid(gc_ref[0].astype(F32)) * (bg_ref[0].astype(F32) * conv))
    merged_ref[0] = merged.astype(BF16)


def _fill_keys(kq_buf, vt_buf, kp, vp, kq, vt, qsc_ref, tq):
    hb = WINDOW
    kqp = (kp * qsc_ref[...]).astype(BF16)
    for kv in range(N_KV):
        kq_buf[kv, 0:hb] = kqp[:, kv * HEAD_DIM:(kv + 1) * HEAD_DIM]
        kq_buf[kv, hb:hb + tq] = kq[:, kv * HEAD_DIM:(kv + 1) * HEAD_DIM]
    vt_buf[:, 0:hb] = vp.T.astype(BF16)
    vt_buf[:, hb:hb + tq] = vt


def _attention_units(q_group, kq_buf, vt_buf, bias_ref, sink_ref, bd, attn_buf, tq, nq, t_first, after_unit=None):
    pw = UNIT_Q
    n_u = tq // nq
    ones_rows = jnp.ones((SUM_ROWS, KEY_WIN), BF16)
    qn_cache = {}

    def scores(i):
        kv, u = divmod(i, n_u)
        if kv not in qn_cache:
            qf = q_group(kv)
            qn_cache[kv] = (qf * _head_inv_rms(qf, bd, two_pass=False)).astype(BF16)
        qn = qn_cache[kv]
        r0 = u * nq
        parts = [qn[r0:r0 + nq, g * HEAD_DIM:(g + 1) * HEAD_DIM] for g in range(GROUP)]
        if nq < pw:
            zpad = jnp.zeros((pw - nq, HEAD_DIM), BF16)
            parts = [x for p_ in parts for x in (p_, zpad)]
        qs = jnp.concatenate(parts, axis=0)
        kw = kq_buf[kv, r0:r0 + KEY_WIN]
        return lax.dot_general(kw, qs, (((1,), (1,)), ((), ())), preferred_element_type=F32)

    def softmax(i, st):
        kv, u = divmod(i, n_u)
        if t_first is not None and u == 0:
            bias = jnp.where(t_first == 0, bias_ref[kv + N_KV], bias_ref[kv])
        else:
            bias = bias_ref[kv]
        st = st + bias
        sink = sink_ref[kv]
        m = jnp.maximum(jnp.max(st, axis=0, keepdims=True), sink)
        return jnp.exp2((st - m).astype(BF16)), jnp.exp2(sink - m)

    def values(i, p, sink_p):
        kv, u = divmod(i, n_u)
        r0 = u * nq
        vt1 = jnp.concatenate([vt_buf[kv * HEAD_DIM:(kv + 1) * HEAD_DIM, r0:r0 + KEY_WIN], ones_rows], axis=0)
        pv = jnp.dot(vt1, p, preferred_element_type=F32)
        ot = pv[:HEAD_DIM] / (pv[HEAD_DIM:HEAD_DIM + 1] + sink_p)
        for gp in range(GROUP // 2):
            blk = jnp.concatenate([ot[:, (2 * gp) * pw:(2 * gp + 1) * pw],
                                   ot[:, (2 * gp + 1) * pw:(2 * gp + 2) * pw]], axis=0)
            c0 = (kv * GROUP + 2 * gp) * HEAD_DIM
            attn_buf[r0:r0 + nq, c0:c0 + 2 * HEAD_DIM] = blk.T[:nq]

    n_units = N_KV * n_u
    ahead = min(SCORE_LOOKAHEAD, n_units)
    queue = [scores(j) for j in range(ahead)]
    pending = None
    for i in range(n_units):
        if i + ahead < n_units:
            queue.append(scores(i + ahead))
        if after_unit is not None:
            after_unit(i)
        probs = softmax(i, queue.pop(0))
        if pending is not None:
            values(i - 1, *pending)
        pending = probs
    values(n_units - 1, *pending)


def _short_conv(u_hist, cu, cw, tq):
    u_all = jnp.concatenate([u_hist, cu], axis=0)
    conv = cw[0:1] * u_all[6:6 + tq] + cw[1:2] * u_all[7:7 + tq] + cw[2:3] * u_all[8:8 + tq]
    return conv, u_all


def _front_body(x_ref, xp_ref, mod_ref, nw_ref, w_ref, knw_ref, qsc_ref, bd_ref, bias_ref, sink_ref, cw_ref,
                merged_ref, knew_ref, vnew_ref, cnew_ref,
                kq_buf, vt_buf, u_buf, attn_buf, pbuf, *, tq, nq):
    hb = WINDOW
    d = D_MODEL
    t = pl.program_id(1)
    bd = bd_ref[...]
    mod = mod_ref[0]

    @pl.when(t == 0)
    def _():
        u_buf[0] = jnp.zeros((8, d), F32)

    def modnorm(x):
        h = x * lax.rsqrt(jnp.mean(x * x, axis=-1, keepdims=True) + RMS_EPS) * nw_ref[...]
        return (h * (1.0 + mod[1:2]) + mod[0:1]).astype(BF16)

    def proj(hrows, c0, width):
        return jnp.dot(hrows, w_ref[:, c0:c0 + width], preferred_element_type=F32)

    kv_p = proj(modnorm(xp_ref[0]), 6 * d, 2 * KV_COLS)
    hbf = modnorm(x_ref[0])
    kv_t = proj(hbf, 6 * d, 2 * KV_COLS)

    assert tq // nq >= 2
    n_units = N_KV * (tq // nq)
    rest_chunks = list(range(d, 6 * d, PROJ_CHUNK))

    def issue(c0, width):
        pbuf[:, c0:c0 + width] = proj(hbf, c0, width)

    issue(0, KV_COLS)
    for c0 in rest_chunks[:PROLOGUE_CHUNKS]:
        issue(c0, PROJ_CHUNK)
    rest_chunks = rest_chunks[PROLOGUE_CHUNKS:]
    n_rest = len(rest_chunks)

    k, v = kv_t[:, :KV_COLS], kv_t[:, KV_COLS:]
    kn = k * _head_inv_rms(k, bd) * knw_ref[...]
    kq = (kn * qsc_ref[...]).astype(BF16)
    kraw = kv_p[:, :KV_COLS]
    kp = kraw * _head_inv_rms(kraw, bd) * knw_ref[...]
    knew_ref[0] = kn[tq - hb:]
    vnew_ref[0] = v[tq - hb:]
    _fill_keys(kq_buf, vt_buf, kp, kv_p[:, KV_COLS:], kq, v.T.astype(BF16), qsc_ref, tq)

    def after_unit(i):
        kv, u = divmod(i, tq // nq)
        if u == 0 and kv + 1 < N_KV:
            issue((kv + 1) * KV_COLS, KV_COLS)
        for c0 in rest_chunks[i * n_rest // n_units:(i + 1) * n_rest // n_units]:
            issue(c0, PROJ_CHUNK)

    _attention_units(lambda kv: pbuf[:, kv * KV_COLS:(kv + 1) * KV_COLS], kq_buf, vt_buf,
                     bias_ref, sink_ref, bd, attn_buf, tq, nq, t, after_unit)

    cu = pbuf[:, 2 * d:3 * d] * pbuf[:, 3 * d:4 * d]
    conv, u_all = _short_conv(u_buf[t % 2], cu, cw_ref[...], tq)
    cnew_ref[0] = u_all[tq + 6:tq + 8]
    u_buf[(t + 1) % 2] = u_all[tq:tq + 8]
    merged = (_sigmoid(pbuf[:, 4 * d:5 * d]) * attn_buf[...]
              + _sigmoid(pbuf[:, 5 * d:6 * d]) * (pbuf[:, d:2 * d] * conv))
    merged_ref[0] = merged.astype(BF16)


def _front(x, mod, nw, w_in_b, knw, qsc, bias_tab, sink_tab, conv_w, tq, nq):
    b, t, d = x.shape
    r = np.arange(KV_COLS) // HEAD_DIM
    bd = jnp.asarray((r[:, None] == r[None, :]).astype(np.float32), BF16)
    const2 = lambda shp: pl.BlockSpec(shp, lambda i, s: (0, 0))
    const3 = lambda shp: pl.BlockSpec(shp, lambda i, s: (0, 0, 0))
    per_b = lambda shp: pl.BlockSpec(shp, lambda i, s: (i, 0, 0))
    kw_ = tq // WINDOW
    in_specs = [pl.BlockSpec((1, tq, d), lambda i, s: (i, s, 0)),
                pl.BlockSpec((1, WINDOW, d), lambda i, s: (i, jnp.maximum(s * kw_ - 1, 0), 0)),
                per_b((1, 6, d)), const2((1, d)),
                pl.BlockSpec((d, IN_COLS), lambda i, s: (0, 0), pipeline_mode=pl.Buffered(1)),
                const2((1, KV_COLS)), const2((1, KV_COLS)), const2((KV_COLS, KV_COLS)),
                const3(bias_tab.shape), const3(sink_tab.shape), const2((3, d))]
    out_shape = (jax.ShapeDtypeStruct((b, t, d), BF16),
                 jax.ShapeDtypeStruct((b, WINDOW, KV_COLS), F32),
                 jax.ShapeDtypeStruct((b, WINDOW, KV_COLS), F32),
                 jax.ShapeDtypeStruct((b, 2, d), F32))
    out_specs = (pl.BlockSpec((1, tq, d), lambda i, s: (i, s, 0)),
                 per_b((1, WINDOW, KV_COLS)), per_b((1, WINDOW, KV_COLS)), per_b((1, 2, d)))
    return pl.pallas_call(
        functools.partial(_front_body, tq=tq, nq=nq),
        out_shape=out_shape,
        grid=(b, t // tq),
        in_specs=in_specs,
        out_specs=out_specs,
        scratch_shapes=[pltpu.VMEM((N_KV, WINDOW + tq, HEAD_DIM), BF16),
                        pltpu.VMEM((KV_COLS, WINDOW + tq), BF16),
                        pltpu.VMEM((2, 8, d), F32),
                        pltpu.VMEM((tq, d), F32),
                        pltpu.VMEM((tq, 6 * d), F32)],
        compiler_params=_cparams(("arbitrary", "arbitrary")),
        name="front",
    )(x, x, mod, nw, w_in_b, knw, qsc, bd, bias_tab, sink_tab, conv_w)


def _mixer(proj, knw, qsc, bias_tab, sink_tab, conv_w, state, tq, nq):
    b, t, _ = proj.shape
    d = D_MODEL
    key_rows = max(WINDOW + tq, KEY_WIN)
    r = np.arange(KV_COLS) // HEAD_DIM
    bd = jnp.asarray((r[:, None] == r[None, :]).astype(np.float32), BF16)
    wide = lambda j: pl.BlockSpec((1, tq, d), lambda i, s, j=j: (i, s, j))
    kvspec = lambda j: pl.BlockSpec((1, tq, KV_COLS), lambda i, s, j=j: (i, s, j))
    const2 = lambda shp: pl.BlockSpec(shp, lambda i, s: (0, 0))
    const3 = lambda shp: pl.BlockSpec(shp, lambda i, s: (0, 0, 0))
    per_b = lambda shp: pl.BlockSpec(shp, lambda i, s: (i, 0, 0))
    kvblk = 6 * d // KV_COLS
    in_specs = [wide(0), kvspec(kvblk), kvspec(kvblk + 1), wide(1), wide(2), wide(3), wide(4), wide(5),
                const2((1, KV_COLS)), const2((1, KV_COLS)), const2((KV_COLS, KV_COLS)),
                const3(bias_tab.shape), const3(sink_tab.shape), const2((3, d)),
                per_b((1, WINDOW, KV_COLS)), per_b((1, WINDOW, KV_COLS)), per_b((1, 2, d))]
    out_shape = (jax.ShapeDtypeStruct((b, t, d), BF16),
                 jax.ShapeDtypeStruct((b, WINDOW, KV_COLS), F32),
                 jax.ShapeDtypeStruct((b, WINDOW, KV_COLS), F32),
                 jax.ShapeDtypeStruct((b, 2, d), F32))
    out_specs = (pl.BlockSpec((1, tq, d), lambda i, s: (i, s, 0)),
                 per_b((1, WINDOW, KV_COLS)), per_b((1, WINDOW, KV_COLS)), per_b((1, 2, d)))
    return pl.pallas_call(
        functools.partial(_mixer_body, tq=tq, nq=nq),
        out_shape=out_shape,
        grid=(b, t // tq),
        in_specs=in_specs,
        out_specs=out_specs,
        scratch_shapes=[pltpu.VMEM((N_KV, key_rows, HEAD_DIM), BF16),
                        pltpu.VMEM((KV_COLS, key_rows), BF16),
                        pltpu.VMEM((tq, d), F32)],
        compiler_params=_cparams(("arbitrary", "arbitrary")),
        name="mixer_state",
    )(proj, proj, proj, proj, proj, proj, proj, proj, knw, qsc, bd, bias_tab, sink_tab, conv_w, *state)


def _route(logits):
    lane = lax.broadcasted_iota(I32, logits.shape, 1).astype(F32)
    neg = -jnp.inf
    big = float(1 << 20)
    gl = jnp.where(lane < N_GROUPS, logits, neg)
    gmax = jnp.max(gl, axis=-1, keepdims=True)
    g_idx = jnp.min(jnp.where(gl == gmax, lane, big), axis=-1, keepdims=True)
    g_w = 1.0 / jnp.sum(jnp.exp(gl - gmax), axis=-1, keepdims=True)
    lo = N_GROUPS + g_idx * EPG
    el = jnp.where((lane >= lo) & (lane < lo + EPG), logits, neg)
    m1 = jnp.max(el, axis=-1, keepdims=True)
    i1 = jnp.min(jnp.where(el == m1, lane, big), axis=-1, keepdims=True)
    el2 = jnp.where(lane == i1, neg, el)
    m2 = jnp.max(el2, axis=-1, keepdims=True)
    i2 = jnp.min(jnp.where(el2 == m2, lane, big), axis=-1, keepdims=True)
    r = jnp.exp(m2 - m1)
    w1 = 1.0 / (1.0 + r)
    w2 = r / (1.0 + r)
    return i1 - N_GROUPS, i2 - N_GROUPS, g_w * w1, g_w * w2


def _outproj_kernel(m_ref, x_ref, mod_ref, wo_ref, nw_ref, wr_ref, br_ref,
                    x1_ref, h2_ref, e01_ref, ew_ref):
    mod = mod_ref[0]
    sub = ROW_TILE
    n_sub = x_ref.shape[1] // sub
    rows = lambda s: slice(s * sub, (s + 1) * sub)
    modrow = lambda k, s: mod[k] if mod.shape[1] == 1 else mod[k, rows(s)]

    mix = [jnp.dot(m_ref[0, rows(s)], wo_ref[...], preferred_element_type=F32) for s in range(n_sub)]
    prod = []
    lane8 = lax.broadcasted_iota(I32, (sub, 8), 1)

    def route(s):
        logits = prod[s][:sub, :LANES] + (prod[s][:sub, LANES:] + prod[s][sub:, :LANES]) + br_ref[...]
        e1, e2, w1, w2 = _route(logits)
        e01_ref[s] = jnp.concatenate([_col_to_row(e1), _col_to_row(e2)], axis=0).astype(I32)
        ew_ref[rows(s)] = jnp.where(lane8 == 0, w1, jnp.where(lane8 == 1, w2, 0.0))

    for s in range(n_sub):
        x1 = x_ref[0, rows(s)] + modrow(2, s) * mix[s]
        x1_ref[0, rows(s)] = x1
        h = x1 * lax.rsqrt(jnp.mean(x1 * x1, axis=-1, keepdims=True) + RMS_EPS) * nw_ref[...]
        h = h * (1.0 + modrow(4, s)) + modrow(3, s)
        h2_ref[rows(s)] = _pack_pairs(h)
        h_hi, h_lo = _split_bf16(h)
        prod.append(jnp.dot(jnp.concatenate([h_hi, h_lo], axis=0), wr_ref[...], preferred_element_type=F32))
        if s > 0:
            route(s - 1)
    route(n_sub - 1)


def _outproj(merged, x, mod4, w_out_b, nw, w_r, b_r, tm):
    b, t, d = x.shape
    nt = t // tm
    assert nt * tm == t
    mr = mod4.shape[2]
    assert mr == 1 or (mr == t and nt == 1)
    n_sub = tm // ROW_TILE
    assert n_sub * ROW_TILE == tm
    flat = lambda i, j: (i * nt + j, 0)
    return pl.pallas_call(
        _outproj_kernel,
        out_shape=(jax.ShapeDtypeStruct((b, t, d), F32),
                   jax.ShapeDtypeStruct((b * t, HALF), I32),
                   jax.ShapeDtypeStruct((b * nt * n_sub, 2, ROW_TILE), I32),
                   jax.ShapeDtypeStruct((b * t, 8), F32)),
        grid=(b, nt),
        in_specs=[pl.BlockSpec((1, tm, d), lambda i, j: (i, j, 0)),
                  pl.BlockSpec((1, tm, d), lambda i, j: (i, j, 0)),
                  pl.BlockSpec((1, 6, mr, d), lambda i, j: (i, 0, 0, 0)),
                  pl.BlockSpec((d, d), lambda i, j: (0, 0)),
                  pl.BlockSpec((1, d), lambda i, j: (0, 0)),
                  pl.BlockSpec((d, 2 * LANES), lambda i, j: (0, 0)),
                  pl.BlockSpec((1, LANES), lambda i, j: (0, 0))],
        out_specs=(pl.BlockSpec((1, tm, d), lambda i, j: (i, j, 0)),
                   pl.BlockSpec((tm, HALF), flat),
                   pl.BlockSpec((n_sub, 2, ROW_TILE), lambda i, j: (i * nt + j, 0, 0)),
                   pl.BlockSpec((tm, 8), flat)),
        compiler_params=_cparams(("arbitrary", "arbitrary")),
        name="outproj",
    )(merged, x, mod4, w_out_b, nw, w_r, b_r)


def _col_to_row(col):
    eye = lax.broadcasted_iota(I32, (LANES, LANES), 0) == lax.broadcasted_iota(I32, (LANES, LANES), 1)
    parts = [jnp.sum(jnp.where(eye, col[r * LANES:(r + 1) * LANES], 0.0), axis=0, keepdims=True)
             for r in range(col.shape[0] // LANES)]
    return jnp.concatenate(parts, axis=1)


def _rank_kernel(e_ref, tri_ref, low_ref, d_ref, tot_ref, *, block):
    n_sub, _, t = e_ref.shape
    sub = lax.broadcasted_iota(I32, (LANES, t), 0)

    def hots(s):
        e = e_ref[s]
        return sub == e[0:1], sub == e[1:2]

    def count(s, cnt):
        h0, h1 = hots(s)
        return cnt + jnp.sum(jnp.where(h0 | h1, 1.0, 0.0), axis=1, keepdims=True)

    cnt = lax.fori_loop(0, n_sub, count, jnp.zeros((LANES, 1), F32))
    tot_ref[...] = _col_to_row(cnt).astype(I32)
    nblk = jnp.floor((cnt + (block - 1)) * (1.0 / block))
    hi = jnp.floor(nblk * (1.0 / 16.0))
    lo = nblk - hi * 16.0
    low = low_ref[...]
    bcast = lambda c: jnp.broadcast_to(c, (LANES, LANES)).astype(BF16)
    excl = (jnp.dot(low, bcast(hi), preferred_element_type=F32) * 16.0
            + jnp.dot(low, bcast(lo), preferred_element_type=F32))
    starts = excl[:, 0:1] * float(block)

    def place(s, running):
        h0, h1 = hots(s)
        onehot = jnp.where(h0 | h1, 1.0, 0.0)
        prefix = jnp.dot(onehot.astype(BF16), tri_ref[...], preferred_element_type=F32)
        pos = prefix + running
        d0 = jnp.sum(jnp.where(h0, pos, 0.0), axis=0, keepdims=True)
        d1 = jnp.sum(jnp.where(h1, pos, 0.0), axis=0, keepdims=True)
        d_ref[s] = jnp.concatenate([d0, d1], axis=0).astype(I32)
        return running + jnp.sum(onehot, axis=1, keepdims=True)

    lax.fori_loop(0, n_sub, place, starts)


def _rank(e01, block):
    n_sub, _, t = e01.shape
    r = np.arange(t)
    tri = jnp.asarray((r[:, None] < r[None, :]).astype(np.float32), BF16)
    l = np.arange(LANES)
    low = jnp.asarray((l[None, :] < l[:, None]).astype(np.float32), BF16)
    return pl.pallas_call(
        functools.partial(_rank_kernel, block=block),
        out_shape=(jax.ShapeDtypeStruct((n_sub, 2, t), I32), jax.ShapeDtypeStruct((1, LANES), I32)),
        compiler_params=pltpu.CompilerParams(vmem_limit_bytes=VMEM_LIMIT),
        name="rank",
    )(e01, tri, low)


def _expert_kernel(start_ref, nblk_ref, xs_hbm, wg_ref, wu_ref, wd_ref, ys_hbm,
                   xbuf, ybuf, wg_s, wu_s, wd_s, sem_in, sem_out, *, block):
    nbuf = EXPERT_BUFS
    e = pl.program_id(0)
    n = nblk_ref[e]
    base = start_ref[e]
    total = start_ref[N_EXPERTS - 1] + nblk_ref[N_EXPERTS - 1]

    def in_copy(g):
        rows = pl.ds(pl.multiple_of(g * block, block), block)
        return pltpu.make_async_copy(xs_hbm.at[rows], xbuf.at[g % nbuf], sem_in.at[g % nbuf])

    def out_copy(g):
        rows = pl.ds(pl.multiple_of(g * block, block), block)
        return pltpu.make_async_copy(ybuf.at[g % nbuf], ys_hbm.at[rows], sem_out.at[g % nbuf])

    @pl.when(e == 0)
    def _():
        for g0 in range(nbuf - 1):
            @pl.when(g0 < total)
            def _(g0=g0):
                in_copy(g0).start()

    @pl.when(n > 0)
    def _():
        wg_s[...] = wg_ref[0].astype(BF16)
        wu_s[...] = wu_ref[0].astype(BF16)
        wd_s[...] = wd_ref[0].astype(BF16)

        def body(i, carry):
            g = base + i
            slot = g % nbuf
            in_copy(g).wait()

            @pl.when(g + nbuf - 1 < total)
            def _():
                in_copy(g + nbuf - 1).start()

            @pl.when(g >= nbuf)
            def _():
                out_copy(g - nbuf).wait()

            a, c = _unpack_pairs(xbuf[slot])
            x = jnp.concatenate([a.astype(BF16), c.astype(BF16)], axis=1)
            gate = jnp.dot(x, wg_s[...], preferred_element_type=F32)
            up = jnp.dot(x, wu_s[...], preferred_element_type=F32)
            hmid = (gate * _sigmoid(gate) * up).astype(BF16)
            ybuf[slot] = _pack_pairs(jnp.dot(hmid, wd_s[...], preferred_element_type=F32))
            out_copy(g).start()
            return carry

        lax.fori_loop(0, n, body, 0)

    @pl.when(e == N_EXPERTS - 1)
    def _():
        for back in range(nbuf, 0, -1):
            @pl.when(total >= back)
            def _(back=back):
                out_copy(total - back).wait()


def _experts(xs, start_blk, nblk, w_gate, w_up, w_down, block):
    n_rows = xs.shape[0]
    wblk = lambda e, st, nb: (e, 0, 0)
    grid_spec = pltpu.PrefetchScalarGridSpec(
        num_scalar_prefetch=2,
        grid=(N_EXPERTS,),
        in_specs=[pl.BlockSpec(memory_space=pl.ANY),
                  pl.BlockSpec((1, D_MODEL, D_EXPERT), wblk),
                  pl.BlockSpec((1, D_MODEL, D_EXPERT), wblk),
                  pl.BlockSpec((1, D_EXPERT, D_MODEL), wblk)],
        out_specs=pl.BlockSpec(memory_space=pl.ANY),
        scratch_shapes=[pltpu.VMEM((EXPERT_BUFS, block, HALF), I32),
                        pltpu.VMEM((EXPERT_BUFS, block, HALF), I32),
                        pltpu.VMEM((D_MODEL, D_EXPERT), BF16),
                        pltpu.VMEM((D_MODEL, D_EXPERT), BF16),
                        pltpu.VMEM((D_EXPERT, D_MODEL), BF16),
                        pltpu.SemaphoreType.DMA((EXPERT_BUFS,)),
                        pltpu.SemaphoreType.DMA((EXPERT_BUFS,))])
    return pl.pallas_call(
        functools.partial(_expert_kernel, block=block),
        out_shape=jax.ShapeDtypeStruct((n_rows, HALF), I32),
        grid_spec=grid_spec,
        compiler_params=_cparams(("arbitrary",)),
        name="experts",
    )(start_blk, nblk, xs, w_gate, w_up, w_down)


def _final_kernel(x1_ref, y0_ref, y1_ref, ew_ref, mod_ref, o_ref):
    a0, b0 = _unpack_pairs(y0_ref[...])
    a1, b1 = _unpack_pairs(y1_ref[...])
    w0 = ew_ref[:, 0:1]
    w1 = ew_ref[:, 1:2]
    moe = jnp.concatenate([w0 * a0 + w1 * a1, w0 * b0 + w1 * b1], axis=1)
    o_ref[0] = x1_ref[0] + mod_ref[0][5:6] * moe


def _final(x1, y0, y1, ew, mod, tm, b0=0, nb=None, y_prev=None):
    b, t, d = x1.shape
    nb = b if nb is None else nb
    nt = t // tm
    local = lambda i, j: (i * nt + j, 0)
    glob = lambda i, j: ((i + b0) * nt + j, 0)
    rows3 = lambda i, j: (i + b0, j, 0)
    in_specs = [pl.BlockSpec((1, tm, d), rows3),
                pl.BlockSpec((tm, HALF), local),
                pl.BlockSpec((tm, HALF), local),
                pl.BlockSpec((tm, 8), glob),
                pl.BlockSpec((1, 6, d), lambda i, j: (i + b0, 0, 0))]
    args = [x1, y0, y1, ew, mod]
    aliases = {}
    kern = _final_kernel
    if y_prev is not None:
        in_specs.append(pl.BlockSpec(memory_space=pl.ANY))
        args.append(y_prev)
        aliases = {5: 0}
        kern = lambda *refs: _final_kernel(*refs[:5], refs[6])
    return pl.pallas_call(
        kern,
        out_shape=jax.ShapeDtypeStruct((b, t, d), F32),
        grid=(nb, nt),
        in_specs=in_specs,
        out_specs=pl.BlockSpec((1, tm, d), rows3),
        input_output_aliases=aliases,
        compiler_params=_cparams(("arbitrary", "arbitrary")),
        name="final",
    )(*args)


def _sc_window(rows_per_worker):
    for w in range(SC_MAX_WINDOW, 7, -8):
        if rows_per_worker % w == 0:
            return w
    raise ValueError(f"no SparseCore window divides {rows_per_worker} rows per worker")


def _sc_split(idx):
    n = idx.shape[0]
    per = n // SC_WORKERS
    assert per * SC_WORKERS == n
    win = _sc_window(per)
    return idx.reshape(SC_WORKERS, per // win, win), per // win, win


def _sc_worker_id():
    return lax.axis_index("s") * SC_CORES + lax.axis_index("c")


def _dispatch_rows(h2_groups, dest_groups, n_rows):
    splits = [(_sc_split(d0), _sc_split(d1)) for d0, d1 in dest_groups]
    ng = len(h2_groups)
    scratch = []
    for (_, _, win), _ in splits:
        scratch += [pltpu.VMEM((win,), I32), pltpu.VMEM((win,), I32), pltpu.VMEM((win, HALF), I32)]

    @functools.partial(
        pl.kernel,
        mesh=plsc.VectorSubcoreMesh(core_axis_name="c", subcore_axis_name="s"),
        out_type=jax.ShapeDtypeStruct((n_rows, HALF), I32),
        scratch_types=scratch,
        name="sc_dispatch",
    )
    def k(*refs):
        x_refs, idx_refs, o_hbm, bufs = refs[:ng], refs[ng:3 * ng], refs[3 * ng], refs[3 * ng + 1:]
        wid = _sc_worker_id()
        for g in range(ng):
            (_, nwin, win), _ = splits[g]
            x_hbm, d0_hbm, d1_hbm = x_refs[g], idx_refs[2 * g], idx_refs[2 * g + 1]
            i0_v, i1_v, rows_v = bufs[3 * g:3 * g + 3]

            @pl.loop(0, nwin)
            def _(j, nwin=nwin, win=win, x_hbm=x_hbm, d0_hbm=d0_hbm, d1_hbm=d1_hbm,
                  i0_v=i0_v, i1_v=i1_v, rows_v=rows_v):
                base = pl.multiple_of((wid * nwin + j) * win, 8)
                pltpu.sync_copy(d0_hbm.at[wid, j], i0_v)
                pltpu.sync_copy(d1_hbm.at[wid, j], i1_v)
                pltpu.sync_copy(x_hbm.at[pl.ds(base, win)], rows_v)
                pltpu.sync_copy(rows_v, o_hbm.at[i0_v])
                pltpu.sync_copy(rows_v, o_hbm.at[i1_v])

    idx_args = []
    for (s0, s1) in splits:
        idx_args += [s0[0], s1[0]]
    return k(*h2_groups, *idx_args)


def _collect_rows(ys, dest_groups):
    splits = [(_sc_split(d0), _sc_split(d1)) for d0, d1 in dest_groups]
    ng = len(dest_groups)
    outs, scratch = [], []
    for (d0, _), ((_, _, win), _) in zip(dest_groups, splits):
        o = jax.ShapeDtypeStruct((d0.shape[0], HALF), I32)
        outs += [o, o]
        scratch += [pltpu.VMEM((win,), I32), pltpu.VMEM((win, HALF), I32)]

    @functools.partial(
        pl.kernel,
        mesh=plsc.VectorSubcoreMesh(core_axis_name="c", subcore_axis_name="s"),
        out_type=tuple(outs),
        scratch_types=scratch,
        name="sc_collect",
    )
    def k(*refs):
        ys_hbm, idx_refs, out_refs, bufs = refs[0], refs[1:1 + 2 * ng], refs[1 + 2 * ng:1 + 4 * ng], refs[1 + 4 * ng:]
        wid = _sc_worker_id()
        for g in range(ng):
            (_, nwin, win), _ = splits[g]
            i_v, rows_v = bufs[2 * g:2 * g + 2]
            for kk in range(2):
                d_hbm, y_hbm = idx_refs[2 * g + kk], out_refs[2 * g + kk]

                @pl.loop(0, nwin)
                def _(j, nwin=nwin, win=win, d_hbm=d_hbm, y_hbm=y_hbm, i_v=i_v, rows_v=rows_v):
                    base = pl.multiple_of((wid * nwin + j) * win, 8)
                    pltpu.sync_copy(d_hbm.at[wid, j], i_v)
                    pltpu.sync_copy(ys_hbm.at[i_v], rows_v)
                    pltpu.sync_copy(rows_v, y_hbm.at[pl.ds(base, win)])

    idx_args = []
    for (s0, s1) in splits:
        idx_args += [s0[0], s1[0]]
    res = k(ys, *idx_args)
    return [(res[2 * g], res[2 * g + 1]) for g in range(ng)]


def _t5_bucket(rel):
    half = N_BUCKETS // 2
    max_exact = half // 2
    n = jnp.abs(rel)
    far = max_exact + (jnp.log(jnp.maximum(n, 1).astype(F32) / max_exact)
                       / math.log(MAX_DISTANCE / max_exact) * (half - max_exact)).astype(I32)
    far = jnp.minimum(far, half - 1)
    return jnp.where(rel > 0, half, 0) + jnp.where(n < max_exact, n, far)


def _bias_table(rel_bias, cq, nq, with_no_history):
    nk = WINDOW + cq
    j = jnp.arange(KEY_WIN)[:, None]
    c = jnp.arange(UNIT_Q)[None, :]
    jj = j - (c // cq) * cq
    valid = (jj >= 0) & (jj < nk) & (c < nq)
    rel = jj - WINDOW - (c % cq)
    onehot = (_t5_bucket(rel)[:, :, None] == jnp.arange(N_BUCKETS)).astype(F32)
    vals = jnp.einsum("jcb,bh->jch", onehot, rel_bias.astype(F32), precision=lax.Precision.HIGHEST) * LOG2E

    def table(mask):
        b = jnp.where(mask[:, :, None], vals, -jnp.inf)
        b = jnp.transpose(b.reshape(KEY_WIN, UNIT_Q, N_KV, GROUP), (2, 0, 3, 1))
        return b.reshape(N_KV, KEY_WIN, GROUP * UNIT_Q)

    if not with_no_history:
        return table(valid)
    return jnp.concatenate([table(valid), table(valid & (j >= WINDOW))], axis=0)


def _sink_table(sinks):
    s = sinks.astype(F32).reshape(N_KV, 1, GROUP, 1)
    return jnp.broadcast_to(s * LOG2E, (N_KV, 1, GROUP, UNIT_Q)).reshape(N_KV, 1, GROUP * UNIT_Q)


def kernel(x_prompt, x_sample, state_attn_k, state_attn_v, state_conv, c_prompt, c_sample,
           rel_bias, w_ada, b_ada, norm1_w, w_in, q_norm_w, k_norm_w, attn_sinks, conv_w,
           w_out, norm2_w, w_router_group, b_router_group, w_router_expert, b_router_expert,
           w_gate, w_up, w_down):
    depth = w_ada.shape[0]
    assert depth == 1
    bp, tp, d = x_prompt.shape
    bs, ts, _ = x_sample.shape
    n_p, n_s = bp * tp, bs * ts
    n_tok = n_p + n_s
    l = 0

    wi = w_in[l]
    qw, kw, vw, rest = wi[:, :d], wi[:, d:d + KV_COLS], wi[:, d + KV_COLS:d + 2 * KV_COLS], wi[:, d + 2 * KV_COLS:]
    w_in_b = jnp.concatenate([qw, rest, kw, vw], axis=1).astype(BF16)
    w_out_b = w_out[l].astype(BF16)
    w_r = jnp.concatenate([w_router_group[l],
                           jnp.transpose(w_router_expert[l], (1, 0, 2)).reshape(d, N_EXPERTS),
                           jnp.zeros((d, LANES - N_GROUPS - N_EXPERTS), F32)], axis=1)
    w_r_hi = lax.reduce_precision(w_r, exponent_bits=8, mantissa_bits=7)
    w_r = jnp.concatenate([w_r_hi.astype(BF16), (w_r - w_r_hi).astype(BF16)], axis=1)
    b_r = jnp.concatenate([b_router_group[l], b_router_expert[l].reshape(-1),
                           jnp.zeros((LANES - N_GROUPS - N_EXPERTS,), F32)]).reshape(1, LANES)
    knw = jnp.tile(k_norm_w[l], N_KV).reshape(1, KV_COLS)
    qsc = jnp.tile(q_norm_w[l] * (HEAD_DIM ** -0.5 * LOG2E), N_KV).reshape(1, KV_COLS)
    n1w = norm1_w[l].reshape(1, d)
    n2w = norm2_w[l].reshape(1, d)

    mod = _ada(jnp.concatenate([c_prompt, c_sample], axis=0), w_ada[l], b_ada[l]).reshape(bp + bs, 6, d)
    mod_p, mod_s = mod[:bp], mod[bp:]

    xs_rows = x_sample.reshape(1, n_s, d)
    mod4_p = mod_p[:, :, None, :]
    mod4_s = jnp.repeat(jnp.transpose(mod_s, (1, 0, 2)), ts, axis=1)[None]
    proj_s = _inproj(xs_rows, mod4_s, n1w, w_in_b, n_s).reshape(bs, ts, IN_COLS)
    sink_tab = _sink_table(attn_sinks[l])
    bias_p = _bias_table(rel_bias, CHUNK, UNIT_Q, True)
    merged_p, k_p, v_p, c_p = _front(x_prompt, mod_p, n1w, w_in_b, knw, qsc, bias_p, sink_tab, conv_w[l],
                                     MIX_TILE, UNIT_Q)
    state = (state_attn_k[l].reshape(bs, WINDOW, KV_COLS), state_attn_v[l].reshape(bs, WINDOW, KV_COLS),
             state_conv[l])
    merged_s, k_s, v_s, c_s = _mixer(proj_s, knw, qsc, _bias_table(rel_bias, ts, ts, False), sink_tab, conv_w[l],
                                     state, ts, ts)

    x1_p, h2_p, e01_p, ew_p = _outproj(merged_p, x_prompt, mod4_p, w_out_b, n2w, w_r, b_r, OUT_TILE)
    x1_s, h2_s, e01_s, ew_s = _outproj(merged_s.reshape(1, n_s, d), xs_rows, mod4_s, w_out_b, n2w, w_r, b_r, n_s)
    x1_s = x1_s.reshape(bs, ts, d)

    assert n_s == ROW_TILE
    d01, totals = _rank(jnp.concatenate([e01_p, e01_s], axis=0), EXPERT_BLOCK)
    n_sub_p = n_p // ROW_TILE
    dests = [(d01[:n_sub_p, 0].reshape(-1), d01[:n_sub_p, 1].reshape(-1)),
             (d01[n_sub_p:, 0].reshape(-1), d01[n_sub_p:, 1].reshape(-1))]
    nblk = (totals[0, :N_EXPERTS] + EXPERT_BLOCK - 1) // EXPERT_BLOCK
    start_blk = (jnp.cumsum(nblk) - nblk).astype(I32)
    nb_max = -(-2 * n_tok // EXPERT_BLOCK) + N_EXPERTS

    xs = _dispatch_rows([h2_p, h2_s], dests, nb_max * EXPERT_BLOCK)
    ys = _experts(xs, start_blk, nblk.astype(I32), w_gate[l], w_up[l], w_down[l], EXPERT_BLOCK)
    (d0_p, d1_p), dest_s = dests
    assert bp % COLLECT_PARTS == 0
    nbp = bp // COLLECT_PARTS
    rows = nbp * tp
    y_p = None
    for part in range(COLLECT_PARTS):
        sl = slice(part * rows, (part + 1) * rows)
        groups = [(d0_p[sl], d1_p[sl])] + ([dest_s] if part == 0 else [])
        got = _collect_rows(ys, groups)
        if part == 0:
            y0_s, y1_s = got[1]
        y_p = _final(x1_p, got[0][0], got[0][1], ew_p, mod_p, ROW_TILE, part * nbp, nbp, y_p)
    y_s = _final(x1_s, y0_s, y1_s, ew_s, mod_s, ts)

    kv_shape = (1, -1, WINDOW, N_KV, HEAD_DIM)
    return (y_p, y_s, k_p.reshape(kv_shape), v_p.reshape(kv_shape), c_p[None],
            k_s.reshape(kv_shape), v_s.reshape(kv_shape), c_s[None])
```

```python
import functools
import math

import numpy as np
import jax
import jax.numpy as jnp
from jax import lax
from jax.experimental import pallas as pl
from jax.experimental.pallas import tpu as pltpu
from jax.experimental.pallas import tpu_sc as plsc

F32 = jnp.float32
BF16 = jnp.bfloat16
I32 = jnp.int32

D_MODEL = 1024
HEAD_DIM = 64
N_HEADS = 16
N_KV = 4
GROUP = 4
CHUNK = 64
WINDOW = 128
N_BUCKETS = 32
MAX_DISTANCE = 128
N_GROUPS = 8
EPG = 8
N_EXPERTS = 64
D_EXPERT = 512
RMS_EPS = 1e-6
LOG2E = math.log2(math.e)
SCORE_LOOKAHEAD = 1
PROLOGUE_CHUNKS = 4
SUM_ROWS = 16
KV_COLS = N_KV * HEAD_DIM
IN_COLS = 6 * D_MODEL + 2 * KV_COLS
HALF = D_MODEL // 2
LANES = 128

VMEM_LIMIT = 56 * 1024 * 1024
ADA_TN = 2048
INPROJ_TN = 512
ROW_TILE = 512
OUT_TILE = 1024
MIX_TILE = 512
UNIT_Q = 2 * CHUNK
KEY_WIN = WINDOW + UNIT_Q
PROJ_CHUNK = 256
EXPERT_BLOCK = 512
RANK_UNROLL = 5
COLLECT_PARTS = 4
EXPERT_BUFS = 4
SC_CORES = 2
SC_SUBCORES = 16
SC_WORKERS = SC_CORES * SC_SUBCORES
SC_MAX_WINDOW = 128


def _cparams(sem):
    return pltpu.CompilerParams(dimension_semantics=sem, vmem_limit_bytes=VMEM_LIMIT)


def _split_bf16(a):
    hi = a.astype(BF16)
    lo = (a - hi.astype(F32)).astype(BF16)
    return hi, lo


def _dot3(a, b):
    ah, al = _split_bf16(a)
    bh, bl = _split_bf16(b)
    d = functools.partial(jnp.dot, preferred_element_type=F32)
    return d(ah, bh) + (d(ah, bl) + d(al, bh))


def _sigmoid(x):
    return 0.5 * jnp.tanh(0.5 * x) + 0.5


def _pack_pairs(y):
    a = lax.bitcast_convert_type(y[:, :HALF].astype(BF16).astype(F32), I32)
    b = lax.bitcast_convert_type(y[:, HALF:].astype(BF16).astype(F32), I32)
    return a | lax.shift_right_logical(b, jnp.int32(16))


def _unpack_pairs(w):
    a = lax.bitcast_convert_type(w & jnp.int32(-65536), F32)
    b = lax.bitcast_convert_type(lax.shift_left(w, jnp.int32(16)), F32)
    return a, b


def _ada_kernel(c_ref, w_ref, b_ref, o_ref):
    c = c_ref[...]
    s = c * jax.nn.sigmoid(c)
    o_ref[...] = _dot3(s, w_ref[...]) + b_ref[...]


def _ada(c_all, w_ada, b_ada):
    r, d = c_all.shape
    n = w_ada.shape[1]
    tn = ADA_TN
    return pl.pallas_call(
        _ada_kernel,
        out_shape=jax.ShapeDtypeStruct((r, n), F32),
        grid=(n // tn,),
        in_specs=[pl.BlockSpec((r, d), lambda j: (0, 0)),
                  pl.BlockSpec((d, tn), lambda j: (0, j)),
                  pl.BlockSpec((1, tn), lambda j: (0, j))],
        out_specs=pl.BlockSpec((r, tn), lambda j: (0, j)),
        compiler_params=_cparams(("arbitrary",)),
        name="ada",
    )(c_all, w_ada, b_ada.reshape(1, n))


def _inproj_kernel(x_ref, mod_ref, nw_ref, w_ref, o_ref):
    x = x_ref[0]
    mod = mod_ref[0]
    h = x * lax.rsqrt(jnp.mean(x * x, axis=-1, keepdims=True) + RMS_EPS) * nw_ref[...]
    h = h * (1.0 + mod[1]) + mod[0]
    hb = h.astype(BF16)
    for j in range(IN_COLS // INPROJ_TN):
        sl = slice(j * INPROJ_TN, (j + 1) * INPROJ_TN)
        o_ref[0, :, sl] = jnp.dot(hb, w_ref[:, sl], preferred_element_type=F32).astype(BF16)


def _inproj(x, mod4, nw, w_in_b, tm):
    b, t, d = x.shape
    mr = mod4.shape[2]
    assert mr == 1 or (mr == t and tm == t)
    return pl.pallas_call(
        _inproj_kernel,
        out_shape=jax.ShapeDtypeStruct((b, t, IN_COLS), BF16),
        grid=(b, t // tm),
        in_specs=[pl.BlockSpec((1, tm, d), lambda i, j: (i, j, 0)),
                  pl.BlockSpec((1, 6, mr, d), lambda i, j: (i, 0, 0, 0)),
                  pl.BlockSpec((1, d), lambda i, j: (0, 0)),
                  pl.BlockSpec((d, IN_COLS), lambda i, j: (0, 0), pipeline_mode=pl.Buffered(1))],
        out_specs=pl.BlockSpec((1, tm, IN_COLS), lambda i, j: (i, j, 0)),
        compiler_params=_cparams(("arbitrary", "arbitrary")),
        name="inproj",
    )(x, mod4, nw, w_in_b)


def _head_inv_rms(xf, bd, two_pass=True):
    sq = xf * xf
    if two_pass:
        hi, lo = _split_bf16(sq)
        ssq = jnp.dot(hi, bd, preferred_element_type=F32) + jnp.dot(lo, bd, preferred_element_type=F32)
    else:
        ssq = jnp.dot(sq.astype(BF16), bd, preferred_element_type=F32)
    return lax.rsqrt(ssq * (1.0 / HEAD_DIM) + RMS_EPS)


def _mixer_body(q_ref, k_ref, v_ref, bg_ref, c_ref, u_ref, ga_ref, gc_ref,
                knw_ref, qsc_ref, bd_ref, bias_ref, sink_ref, cw_ref,
                kpast_ref, vpast_ref, cpast_ref,
                merged_ref, knew_ref, vnew_ref, cnew_ref,
                kq_buf, vt_buf, attn_buf, *, tq, nq):
    hb = WINDOW
    bd = bd_ref[...]

    k = k_ref[0].astype(F32)
    kn = k * _head_inv_rms(k, bd) * knw_ref[...]
    kq = (kn * qsc_ref[...]).astype(BF16)
    vb = v_ref[0]
    vt = vb.astype(F32).T.astype(BF16)

    kp = kpast_ref[0]
    vp = vpast_ref[0]
    for kv in range(N_KV):
        kq_buf[kv, hb + tq:] = jnp.zeros((KEY_WIN - hb - tq, HEAD_DIM), BF16)
    vt_buf[:, hb + tq:] = jnp.zeros((KV_COLS, KEY_WIN - hb - tq), BF16)
    u_hist = jnp.concatenate([jnp.zeros((6, D_MODEL), F32), cpast_ref[0]], axis=0)
    knew_ref[0] = jnp.concatenate([kp[tq:], kn], axis=0)
    vnew_ref[0] = jnp.concatenate([vp[tq:], vb.astype(F32)], axis=0)
    _fill_keys(kq_buf, vt_buf, kp, vp, kq, vt, qsc_ref, tq)
    q = q_ref[0]
    _attention_units(lambda kv: q[:, kv * KV_COLS:(kv + 1) * KV_COLS].astype(F32), kq_buf, vt_buf,
                     bias_ref, sink_ref, bd, attn_buf, tq, nq, None)
    cu = c_ref[0].astype(F32) * u_ref[0].astype(F32)
    conv, u_all = _short_conv(u_hist, cu, cw_ref[...], tq)
    cnew_ref[0] = u_all[tq + 6:tq + 8]
    merged = (_sigmoid(ga_ref[0].astype(F32)) * attn_buf[...]
              + _sigmoid(gc_ref[0].astype(F32)) * (bg_ref[0].astype(F32) * conv))
    merged_ref[0] = merged.astype(BF16)


def _fill_keys(kq_buf, vt_buf, kp, vp, kq, vt, qsc_ref, tq):
    hb = WINDOW
    kqp = (kp * qsc_ref[...]).astype(BF16)
    for kv in range(N_KV):
        kq_buf[kv, 0:hb] = kqp[:, kv * HEAD_DIM:(kv + 1) * HEAD_DIM]
        kq_buf[kv, hb:hb + tq] = kq[:, kv * HEAD_DIM:(kv + 1) * HEAD_DIM]
    vt_buf[:, 0:hb] = vp.T.astype(BF16)
    vt_buf[:, hb:hb + tq] = vt


def _attention_units(q_group, kq_buf, vt_buf, bias_ref, sink_ref, bd, attn_buf, tq, nq, t_first, after_unit=None):
    pw = UNIT_Q
    n_u = tq // nq
    ones_rows = jnp.ones((SUM_ROWS, KEY_WIN), BF16)
    qn_cache = {}

    def scores(i):
        kv, u = divmod(i, n_u)
        if kv not in qn_cache:
            qf = q_group(kv)
            qn_cache[kv] = (qf * _head_inv_rms(qf, bd, two_pass=False)).astype(BF16)
        qn = qn_cache[kv]
        r0 = u * nq
        parts = [qn[r0:r0 + nq, g * HEAD_DIM:(g + 1) * HEAD_DIM] for g in range(GROUP)]
        if nq < pw:
            zpad = jnp.zeros((pw - nq, HEAD_DIM), BF16)
            parts = [x for p_ in parts for x in (p_, zpad)]
        qs = jnp.concatenate(parts, axis=0)
        kw = kq_buf[kv, r0:r0 + KEY_WIN]
        return lax.dot_general(kw, qs, (((1,), (1,)), ((), ())), preferred_element_type=F32)

    def softmax(i, st):
        kv, u = divmod(i, n_u)
        if t_first is not None and u == 0:
            bias = jnp.where(t_first == 0, bias_ref[kv + N_KV], bias_ref[kv])
        else:
            bias = bias_ref[kv]
        st = st + bias
        sink = sink_ref[kv]
        m = jnp.maximum(jnp.max(st, axis=0, keepdims=True), sink)
        return jnp.exp2((st - m).astype(BF16)), jnp.exp2(sink - m)

    def values(i, p, sink_p):
        kv, u = divmod(i, n_u)
        r0 = u * nq
        vt1 = jnp.concatenate([vt_buf[kv * HEAD_DIM:(kv + 1) * HEAD_DIM, r0:r0 + KEY_WIN], ones_rows], axis=0)
        pv = jnp.dot(vt1, p, preferred_element_type=F32)
        ot = pv[:HEAD_DIM] / (pv[HEAD_DIM:HEAD_DIM + 1] + sink_p)
        for gp in range(GROUP // 2):
            blk = jnp.concatenate([ot[:, (2 * gp) * pw:(2 * gp + 1) * pw],
                                   ot[:, (2 * gp + 1) * pw:(2 * gp + 2) * pw]], axis=0)
            c0 = (kv * GROUP + 2 * gp) * HEAD_DIM
            attn_buf[r0:r0 + nq, c0:c0 + 2 * HEAD_DIM] = blk.T[:nq]

    n_units = N_KV * n_u
    ahead = min(SCORE_LOOKAHEAD, n_units)
    queue = [scores(j) for j in range(ahead)]
    pending = None
    for i in range(n_units):
        if i + ahead < n_units:
            queue.append(scores(i + ahead))
        if after_unit is not None:
            after_unit(i)
        probs = softmax(i, queue.pop(0))
        if pending is not None:
            values(i - 1, *pending)
        pending = probs
    values(n_units - 1, *pending)


def _short_conv(u_hist, cu, cw, tq):
    u_all = jnp.concatenate([u_hist, cu], axis=0)
    conv = cw[0:1] * u_all[6:6 + tq] + cw[1:2] * u_all[7:7 + tq] + cw[2:3] * u_all[8:8 + tq]
    return conv, u_all


def _front_body(x_ref, xp_ref, mod_ref, nw_ref, w_ref, knw_ref, qsc_ref, bd_ref, bias_ref, sink_ref, cw_ref,
                merged_ref, knew_ref, vnew_ref, cnew_ref,
                kq_buf, vt_buf, u_buf, attn_buf, pbuf, *, tq, nq):
    hb = WINDOW
    d = D_MODEL
    t = pl.program_id(1)
    bd = bd_ref[...]
    mod = mod_ref[0]

    @pl.when(t == 0)
    def _():
        u_buf[0] = jnp.zeros((8, d), F32)

    def modnorm(x):
        h = x * lax.rsqrt(jnp.mean(x * x, axis=-1, keepdims=True) + RMS_EPS) * nw_ref[...]
        return (h * (1.0 + mod[1:2]) + mod[0:1]).astype(BF16)

    def proj(hrows, c0, width):
        return jnp.dot(hrows, w_ref[:, c0:c0 + width], preferred_element_type=F32)

    kv_p = proj(modnorm(xp_ref[0]), 6 * d, 2 * KV_COLS)
    hbf = modnorm(x_ref[0])
    kv_t = proj(hbf, 6 * d, 2 * KV_COLS)

    assert tq // nq >= 2
    n_units = N_KV * (tq // nq)
    rest_chunks = list(range(d, 6 * d, PROJ_CHUNK))

    def issue(c0, width):
        pbuf[:, c0:c0 + width] = proj(hbf, c0, width)

    issue(0, KV_COLS)
    for c0 in rest_chunks[:PROLOGUE_CHUNKS]:
        issue(c0, PROJ_CHUNK)
    rest_chunks = rest_chunks[PROLOGUE_CHUNKS:]
    n_rest = len(rest_chunks)

    k, v = kv_t[:, :KV_COLS], kv_t[:, KV_COLS:]
    kn = k * _head_inv_rms(k, bd) * knw_ref[...]
    kq = (kn * qsc_ref[...]).astype(BF16)
    kraw = kv_p[:, :KV_COLS]
    kp = kraw * _head_inv_rms(kraw, bd) * knw_ref[...]
    knew_ref[0] = kn[tq - hb:]
    vnew_ref[0] = v[tq - hb:]
    _fill_keys(kq_buf, vt_buf, kp, kv_p[:, KV_COLS:], kq, v.T.astype(BF16), qsc_ref, tq)

    def after_unit(i):
        kv, u = divmod(i, tq // nq)
        if u == 0 and kv + 1 < N_KV:
            issue((kv + 1) * KV_COLS, KV_COLS)
        for c0 in rest_chunks[i * n_rest // n_units:(i + 1) * n_rest // n_units]:
            issue(c0, PROJ_CHUNK)

    _attention_units(lambda kv: pbuf[:, kv * KV_COLS:(kv + 1) * KV_COLS], kq_buf, vt_buf,
                     bias_ref, sink_ref, bd, attn_buf, tq, nq, t, after_unit)

    cu = pbuf[:, 2 * d:3 * d] * pbuf[:, 3 * d:4 * d]
    conv, u_all = _short_conv(u_buf[t % 2], cu, cw_ref[...], tq)
    cnew_ref[0] = u_all[tq + 6:tq + 8]
    u_buf[(t + 1) % 2] = u_all[tq:tq + 8]
    merged = (_sigmoid(pbuf[:, 4 * d:5 * d]) * attn_buf[...]
              + _sigmoid(pbuf[:, 5 * d:6 * d]) * (pbuf[:, d:2 * d] * conv))
    merged_ref[0] = merged.astype(BF16)


def _front(x, mod, nw, w_in_b, knw, qsc, bias_tab, sink_tab, conv_w, tq, nq):
    b, t, d = x.shape
    r = np.arange(KV_COLS) // HEAD_DIM
    bd = jnp.asarray((r[:, None] == r[None, :]).astype(np.float32), BF16)
    const2 = lambda shp: pl.BlockSpec(shp, lambda i, s: (0, 0))
    const3 = lambda shp: pl.BlockSpec(shp, lambda i, s: (0, 0, 0))
    per_b = lambda shp: pl.BlockSpec(shp, lambda i, s: (i, 0, 0))
    kw_ = tq // WINDOW
    in_specs = [pl.BlockSpec((1, tq, d), lambda i, s: (i, s, 0)),
                pl.BlockSpec((1, WINDOW, d), lambda i, s: (i, jnp.maximum(s * kw_ - 1, 0), 0)),
                per_b((1, 6, d)), const2((1, d)),
                pl.BlockSpec((d, IN_COLS), lambda i, s: (0, 0), pipeline_mode=pl.Buffered(1)),
                const2((1, KV_COLS)), const2((1, KV_COLS)), const2((KV_COLS, KV_COLS)),
                const3(bias_tab.shape), const3(sink_tab.shape), const2((3, d))]
    out_shape = (jax.ShapeDtypeStruct((b, t, d), BF16),
                 jax.ShapeDtypeStruct((b, WINDOW, KV_COLS), F32),
                 jax.ShapeDtypeStruct((b, WINDOW, KV_COLS), F32),
                 jax.ShapeDtypeStruct((b, 2, d), F32))
    out_specs = (pl.BlockSpec((1, tq, d), lambda i, s: (i, s, 0)),
                 per_b((1, WINDOW, KV_COLS)), per_b((1, WINDOW, KV_COLS)), per_b((1, 2, d)))
    return pl.pallas_call(
        functools.partial(_front_body, tq=tq, nq=nq),
        out_shape=out_shape,
        grid=(b, t // tq),
        in_specs=in_specs,
        out_specs=out_specs,
        scratch_shapes=[pltpu.VMEM((N_KV, WINDOW + tq, HEAD_DIM), BF16),
                        pltpu.VMEM((KV_COLS, WINDOW + tq), BF16),
                        pltpu.VMEM((2, 8, d), F32),
                        pltpu.VMEM((tq, d), F32),
                        pltpu.VMEM((tq, 6 * d), F32)],
        compiler_params=_cparams(("arbitrary", "arbitrary")),
        name="front",
    )(x, x, mod, nw, w_in_b, knw, qsc, bd, bias_tab, sink_tab, conv_w)


def _mixer(proj, knw, qsc, bias_tab, sink_tab, conv_w, state, tq, nq):
    b, t, _ = proj.shape
    d = D_MODEL
    key_rows = max(WINDOW + tq, KEY_WIN)
    r = np.arange(KV_COLS) // HEAD_DIM
    bd = jnp.asarray((r[:, None] == r[None, :]).astype(np.float32), BF16)
    wide = lambda j: pl.BlockSpec((1, tq, d), lambda i, s, j=j: (i, s, j))
    kvspec = lambda j: pl.BlockSpec((1, tq, KV_COLS), lambda i, s, j=j: (i, s, j))
    const2 = lambda shp: pl.BlockSpec(shp, lambda i, s: (0, 0))
    const3 = lambda shp: pl.BlockSpec(shp, lambda i, s: (0, 0, 0))
    per_b = lambda shp: pl.BlockSpec(shp, lambda i, s: (i, 0, 0))
    kvblk = 6 * d // KV_COLS
    in_specs = [wide(0), kvspec(kvblk), kvspec(kvblk + 1), wide(1), wide(2), wide(3), wide(4), wide(5),
                const2((1, KV_COLS)), const2((1, KV_COLS)), const2((KV_COLS, KV_COLS)),
                const3(bias_tab.shape), const3(sink_tab.shape), const2((3, d)),
                per_b((1, WINDOW, KV_COLS)), per_b((1, WINDOW, KV_COLS)), per_b((1, 2, d))]
    out_shape = (jax.ShapeDtypeStruct((b, t, d), BF16),
                 jax.ShapeDtypeStruct((b, WINDOW, KV_COLS), F32),
                 jax.ShapeDtypeStruct((b, WINDOW, KV_COLS), F32),
                 jax.ShapeDtypeStruct((b, 2, d), F32))
    out_specs = (pl.BlockSpec((1, tq, d), lambda i, s: (i, s, 0)),
                 per_b((1, WINDOW, KV_COLS)), per_b((1, WINDOW, KV_COLS)), per_b((1, 2, d)))
    return pl.pallas_call(
        functools.partial(_mixer_body, tq=tq, nq=nq),
        out_shape=out_shape,
        grid=(b, t // tq),
        in_specs=in_specs,
        out_specs=out_specs,
        scratch_shapes=[pltpu.VMEM((N_KV, key_rows, HEAD_DIM), BF16),
                        pltpu.VMEM((KV_COLS, key_rows), BF16),
                        pltpu.VMEM((tq, d), F32)],
        compiler_params=_cparams(("arbitrary", "arbitrary")),
        name="mixer_state",
    )(proj, proj, proj, proj, proj, proj, proj, proj, knw, qsc, bd, bias_tab, sink_tab, conv_w, *state)


def _route(logits):
    lane = lax.broadcasted_iota(I32, logits.shape, 1).astype(F32)
    neg = -jnp.inf
    big = float(1 << 20)
    gl = jnp.where(lane < N_GROUPS, logits, neg)
    gmax = jnp.max(gl, axis=-1, keepdims=True)
    g_idx = jnp.min(jnp.where(gl == gmax, lane, big), axis=-1, keepdims=True)
    g_w = 1.0 / jnp.sum(jnp.exp(gl - gmax), axis=-1, keepdims=True)
    lo = N_GROUPS + g_idx * EPG
    el = jnp.where((lane >= lo) & (lane < lo + EPG), logits, neg)
    m1 = jnp.max(el, axis=-1, keepdims=True)
    i1 = jnp.min(jnp.where(el == m1, lane, big), axis=-1, keepdims=True)
    el2 = jnp.where(lane == i1, neg, el)
    m2 = jnp.max(el2, axis=-1, keepdims=True)
    i2 = jnp.min(jnp.where(el2 == m2, lane, big), axis=-1, keepdims=True)
    r = jnp.exp(m2 - m1)
    w1 = 1.0 / (1.0 + r)
    w2 = r / (1.0 + r)
    return i1 - N_GROUPS, i2 - N_GROUPS, g_w * w1, g_w * w2


def _outproj_kernel(m_ref, x_ref, mod_ref, wo_ref, nw_ref, wr_ref, br_ref,
                    x1_ref, h2_ref, e01_ref, ew_ref):
    mod = mod_ref[0]
    sub = ROW_TILE
    n_sub = x_ref.shape[1] // sub
    rows = lambda s: slice(s * sub, (s + 1) * sub)
    modrow = lambda k, s: mod[k] if mod.shape[1] == 1 else mod[k, rows(s)]

    mix = [jnp.dot(m_ref[0, rows(s)], wo_ref[...], preferred_element_type=F32) for s in range(n_sub)]
    prod = []
    lane8 = lax.broadcasted_iota(I32, (sub, 8), 1)

    def route(s):
        logits = prod[s][:sub, :LANES] + (prod[s][:sub, LANES:] + prod[s][sub:, :LANES]) + br_ref[...]
        e1, e2, w1, w2 = _route(logits)
        e01_ref[s] = jnp.concatenate([_col_to_row(e1), _col_to_row(e2)], axis=0).astype(I32)
        ew_ref[rows(s)] = jnp.where(lane8 == 0, w1, jnp.where(lane8 == 1, w2, 0.0))

    for s in range(n_sub):
        x1 = x_ref[0, rows(s)] + modrow(2, s) * mix[s]
        x1_ref[0, rows(s)] = x1
        h = x1 * lax.rsqrt(jnp.mean(x1 * x1, axis=-1, keepdims=True) + RMS_EPS) * nw_ref[...]
        h = h * (1.0 + modrow(4, s)) + modrow(3, s)
        h2_ref[rows(s)] = _pack_pairs(h)
        h_hi, h_lo = _split_bf16(h)
        prod.append(jnp.dot(jnp.concatenate([h_hi, h_lo], axis=0), wr_ref[...], preferred_element_type=F32))
        if s > 0:
            route(s - 1)
    route(n_sub - 1)


def _outproj(merged, x, mod4, w_out_b, nw, w_r, b_r, tm):
    b, t, d = x.shape
    nt = t // tm
    assert nt * tm == t
    mr = mod4.shape[2]
    assert mr == 1 or (mr == t and nt == 1)
    n_sub = tm // ROW_TILE
    assert n_sub * ROW_TILE == tm
    flat = lambda i, j: (i * nt + j, 0)
    return pl.pallas_call(
        _outproj_kernel,
        out_shape=(jax.ShapeDtypeStruct((b, t, d), F32),
                   jax.ShapeDtypeStruct((b * t, HALF), I32),
                   jax.ShapeDtypeStruct((b * nt * n_sub, 2, ROW_TILE), I32),
                   jax.ShapeDtypeStruct((b * t, 8), F32)),
        grid=(b, nt),
        in_specs=[pl.BlockSpec((1, tm, d), lambda i, j: (i, j, 0)),
                  pl.BlockSpec((1, tm, d), lambda i, j: (i, j, 0)),
                  pl.BlockSpec((1, 6, mr, d), lambda i, j: (i, 0, 0, 0)),
                  pl.BlockSpec((d, d), lambda i, j: (0, 0)),
                  pl.BlockSpec((1, d), lambda i, j: (0, 0)),
                  pl.BlockSpec((d, 2 * LANES), lambda i, j: (0, 0)),
                  pl.BlockSpec((1, LANES), lambda i, j: (0, 0))],
        out_specs=(pl.BlockSpec((1, tm, d), lambda i, j: (i, j, 0)),
                   pl.BlockSpec((tm, HALF), flat),
                   pl.BlockSpec((n_sub, 2, ROW_TILE), lambda i, j: (i * nt + j, 0, 0)),
                   pl.BlockSpec((tm, 8), flat)),
        compiler_params=_cparams(("arbitrary", "arbitrary")),
        name="outproj",
    )(merged, x, mod4, w_out_b, nw, w_r, b_r)


def _col_to_row(col):
    eye = lax.broadcasted_iota(I32, (LANES, LANES), 0) == lax.broadcasted_iota(I32, (LANES, LANES), 1)
    parts = [jnp.sum(jnp.where(eye, col[r * LANES:(r + 1) * LANES], 0.0), axis=0, keepdims=True)
             for r in range(col.shape[0] // LANES)]
    return jnp.concatenate(parts, axis=1)


def _rank_kernel(e_ref, tri_ref, low_ref, d_ref, tot_ref, *, block):
    n_sub, _, t = e_ref.shape
    sub = lax.broadcasted_iota(I32, (LANES, t), 0)

    def hots(s):
        e = e_ref[s]
        return sub == e[0:1], sub == e[1:2]

    def count(s, cnt):
        h0, h1 = hots(s)
        return cnt + jnp.sum(jnp.where(h0 | h1, 1.0, 0.0), axis=1, keepdims=True)

    cnt = lax.fori_loop(0, n_sub, count, jnp.zeros((LANES, 1), F32), unroll=RANK_UNROLL)
    tot_ref[...] = _col_to_row(cnt).astype(I32)
    nblk = jnp.floor((cnt + (block - 1)) * (1.0 / block))
    hi = jnp.floor(nblk * (1.0 / 16.0))
    lo = nblk - hi * 16.0
    low = low_ref[...]
    bcast = lambda c: jnp.broadcast_to(c, (LANES, LANES)).astype(BF16)
    excl = (jnp.dot(low, bcast(hi), preferred_element_type=F32) * 16.0
            + jnp.dot(low, bcast(lo), preferred_element_type=F32))
    starts = excl[:, 0:1] * float(block)

    def place(s, running):
        h0, h1 = hots(s)
        onehot = jnp.where(h0 | h1, 1.0, 0.0)
        prefix = jnp.dot(onehot.astype(BF16), tri_ref[...], preferred_element_type=F32)
        pos = prefix + running
        d0 = jnp.sum(jnp.where(h0, pos, 0.0), axis=0, keepdims=True)
        d1 = jnp.sum(jnp.where(h1, pos, 0.0), axis=0, keepdims=True)
        d_ref[s] = jnp.concatenate([d0, d1], axis=0).astype(I32)
        return running + jnp.sum(onehot, axis=1, keepdims=True)

    lax.fori_loop(0, n_sub, place, starts, unroll=RANK_UNROLL)


def _rank(e01, block):
    n_sub, _, t = e01.shape
    r = np.arange(t)
    tri = jnp.asarray((r[:, None] < r[None, :]).astype(np.float32), BF16)
    l = np.arange(LANES)
    low = jnp.asarray((l[None, :] < l[:, None]).astype(np.float32), BF16)
    return pl.pallas_call(
        functools.partial(_rank_kernel, block=block),
        out_shape=(jax.ShapeDtypeStruct((n_sub, 2, t), I32), jax.ShapeDtypeStruct((1, LANES), I32)),
        compiler_params=pltpu.CompilerParams(vmem_limit_bytes=VMEM_LIMIT),
        name="rank",
    )(e01, tri, low)


def _expert_kernel(start_ref, nblk_ref, xs_hbm, wg_ref, wu_ref, wd_ref, ys_hbm,
                   xbuf, ybuf, wg_s, wu_s, wd_s, sem_in, sem_out, *, block):
    nbuf = EXPERT_BUFS
    e = pl.program_id(0)
    n = nblk_ref[e]
    base = start_ref[e]
    total = start_ref[N_EXPERTS - 1] + nblk_ref[N_EXPERTS - 1]

    def in_copy(g):
        rows = pl.ds(pl.multiple_of(g * block, block), block)
        return pltpu.make_async_copy(xs_hbm.at[rows], xbuf.at[g % nbuf], sem_in.at[g % nbuf])

    def out_copy(g):
        rows = pl.ds(pl.multiple_of(g * block, block), block)
        return pltpu.make_async_copy(ybuf.at[g % nbuf], ys_hbm.at[rows], sem_out.at[g % nbuf])

    @pl.when(e == 0)
    def _():
        for g0 in range(nbuf - 1):
            @pl.when(g0 < total)
            def _(g0=g0):
                in_copy(g0).start()

    @pl.when(n > 0)
    def _():
        wg_s[...] = wg_ref[0].astype(BF16)
        wu_s[...] = wu_ref[0].astype(BF16)
        wd_s[...] = wd_ref[0].astype(BF16)

        def body(i, carry):
            g = base + i
            slot = g % nbuf
            in_copy(g).wait()

            @pl.when(g + nbuf - 1 < total)
            def _():
                in_copy(g + nbuf - 1).start()

            @pl.when(g >= nbuf)
            def _():
                out_copy(g - nbuf).wait()

            a, c = _unpack_pairs(xbuf[slot])
            x = jnp.concatenate([a.astype(BF16), c.astype(BF16)], axis=1)
            gate = jnp.dot(x, wg_s[...], preferred_element_type=F32)
            up = jnp.dot(x, wu_s[...], preferred_element_type=F32)
            hmid = (gate * _sigmoid(gate) * up).astype(BF16)
            ybuf[slot] = _pack_pairs(jnp.dot(hmid, wd_s[...], preferred_element_type=F32))
            out_copy(g).start()
            return carry

        lax.fori_loop(0, n, body, 0)

    @pl.when(e == N_EXPERTS - 1)
    def _():
        for back in range(nbuf, 0, -1):
            @pl.when(total >= back)
            def _(back=back):
                out_copy(total - back).wait()


def _experts(xs, start_blk, nblk, w_gate, w_up, w_down, block):
    n_rows = xs.shape[0]
    wblk = lambda e, st, nb: (e, 0, 0)
    grid_spec = pltpu.PrefetchScalarGridSpec(
        num_scalar_prefetch=2,
        grid=(N_EXPERTS,),
        in_specs=[pl.BlockSpec(memory_space=pl.ANY),
                  pl.BlockSpec((1, D_MODEL, D_EXPERT), wblk),
                  pl.BlockSpec((1, D_MODEL, D_EXPERT), wblk),
                  pl.BlockSpec((1, D_EXPERT, D_MODEL), wblk)],
        out_specs=pl.BlockSpec(memory_space=pl.ANY),
        scratch_shapes=[pltpu.VMEM((EXPERT_BUFS, block, HALF), I32),
                        pltpu.VMEM((EXPERT_BUFS, block, HALF), I32),
                        pltpu.VMEM((D_MODEL, D_EXPERT), BF16),
                        pltpu.VMEM((D_MODEL, D_EXPERT), BF16),
                        pltpu.VMEM((D_EXPERT, D_MODEL), BF16),
                        pltpu.SemaphoreType.DMA((EXPERT_BUFS,)),
                        pltpu.SemaphoreType.DMA((EXPERT_BUFS,))])
    return pl.pallas_call(
        functools.partial(_expert_kernel, block=block),
        out_shape=jax.ShapeDtypeStruct((n_rows, HALF), I32),
        grid_spec=grid_spec,
        compiler_params=_cparams(("arbitrary",)),
        name="experts",
    )(start_blk, nblk, xs, w_gate, w_up, w_down)


def _final_kernel(x1_ref, y0_ref, y1_ref, ew_ref, mod_ref, o_ref):
    a0, b0 = _unpack_pairs(y0_ref[...])
    a1, b1 = _unpack_pairs(y1_ref[...])
    w0 = ew_ref[:, 0:1]
    w1 = ew_ref[:, 1:2]
    moe = jnp.concatenate([w0 * a0 + w1 * a1, w0 * b0 + w1 * b1], axis=1)
    o_ref[0] = x1_ref[0] + mod_ref[0][5:6] * moe


def _final(x1, y0, y1, ew, mod, tm, b0=0, nb=None, y_prev=None):
    b, t, d = x1.shape
    nb = b if nb is None else nb
    nt = t // tm
    local = lambda i, j: (i * nt + j, 0)
    glob = lambda i, j: ((i + b0) * nt + j, 0)
    rows3 = lambda i, j: (i + b0, j, 0)
    in_specs = [pl.BlockSpec((1, tm, d), rows3),
                pl.BlockSpec((tm, HALF), local),
                pl.BlockSpec((tm, HALF), local),
                pl.BlockSpec((tm, 8), glob),
                pl.BlockSpec((1, 6, d), lambda i, j: (i + b0, 0, 0))]
    args = [x1, y0, y1, ew, mod]
    aliases = {}
    kern = _final_kernel
    if y_prev is not None:
        in_specs.append(pl.BlockSpec(memory_space=pl.ANY))
        args.append(y_prev)
        aliases = {5: 0}
        kern = lambda *refs: _final_kernel(*refs[:5], refs[6])
    return pl.pallas_call(
        kern,
        out_shape=jax.ShapeDtypeStruct((b, t, d), F32),
        grid=(nb, nt),
        in_specs=in_specs,
        out_specs=pl.BlockSpec((1, tm, d), rows3),
        input_output_aliases=aliases,
        compiler_params=_cparams(("arbitrary", "arbitrary")),
        name="final",
    )(*args)


def _sc_window(rows_per_worker):
    for w in range(SC_MAX_WINDOW, 7, -8):
        if rows_per_worker % w == 0:
            return w
    raise ValueError(f"no SparseCore window divides {rows_per_worker} rows per worker")


def _sc_split(idx):
    n = idx.shape[0]
    per = n // SC_WORKERS
    assert per * SC_WORKERS == n
    win = _sc_window(per)
    return idx.reshape(SC_WORKERS, per // win, win), per // win, win


def _sc_worker_id():
    return lax.axis_index("s") * SC_CORES + lax.axis_index("c")


def _dispatch_rows(h2_groups, dest_groups, n_rows):
    splits = [(_sc_split(d0), _sc_split(d1)) for d0, d1 in dest_groups]
    ng = len(h2_groups)
    scratch = []
    for (_, _, win), _ in splits:
        scratch += [pltpu.VMEM((win,), I32), pltpu.VMEM((win,), I32), pltpu.VMEM((win, HALF), I32)]

    @functools.partial(
        pl.kernel,
        mesh=plsc.VectorSubcoreMesh(core_axis_name="c", subcore_axis_name="s"),
        out_type=jax.ShapeDtypeStruct((n_rows, HALF), I32),
        scratch_types=scratch,
        name="sc_dispatch",
    )
    def k(*refs):
        x_refs, idx_refs, o_hbm, bufs = refs[:ng], refs[ng:3 * ng], refs[3 * ng], refs[3 * ng + 1:]
        wid = _sc_worker_id()
        for g in range(ng):
            (_, nwin, win), _ = splits[g]
            x_hbm, d0_hbm, d1_hbm = x_refs[g], idx_refs[2 * g], idx_refs[2 * g + 1]
            i0_v, i1_v, rows_v = bufs[3 * g:3 * g + 3]

            @pl.loop(0, nwin)
            def _(j, nwin=nwin, win=win, x_hbm=x_hbm, d0_hbm=d0_hbm, d1_hbm=d1_hbm,
                  i0_v=i0_v, i1_v=i1_v, rows_v=rows_v):
                base = pl.multiple_of((wid * nwin + j) * win, 8)
                pltpu.sync_copy(d0_hbm.at[wid, j], i0_v)
                pltpu.sync_copy(d1_hbm.at[wid, j], i1_v)
                pltpu.sync_copy(x_hbm.at[pl.ds(base, win)], rows_v)
                pltpu.sync_copy(rows_v, o_hbm.at[i0_v])
                pltpu.sync_copy(rows_v, o_hbm.at[i1_v])

    idx_args = []
    for (s0, s1) in splits:
        idx_args += [s0[0], s1[0]]
    return k(*h2_groups, *idx_args)


def _collect_rows(ys, dest_groups):
    splits = [(_sc_split(d0), _sc_split(d1)) for d0, d1 in dest_groups]
    ng = len(dest_groups)
    outs, scratch = [], []
    for (d0, _), ((_, _, win), _) in zip(dest_groups, splits):
        o = jax.ShapeDtypeStruct((d0.shape[0], HALF), I32)
        outs += [o, o]
        scratch += [pltpu.VMEM((win,), I32), pltpu.VMEM((win, HALF), I32)]

    @functools.partial(
        pl.kernel,
        mesh=plsc.VectorSubcoreMesh(core_axis_name="c", subcore_axis_name="s"),
        out_type=tuple(outs),
        scratch_types=scratch,
        name="sc_collect",
    )
    def k(*refs):
        ys_hbm, idx_refs, out_refs, bufs = refs[0], refs[1:1 + 2 * ng], refs[1 + 2 * ng:1 + 4 * ng], refs[1 + 4 * ng:]
        wid = _sc_worker_id()
        for g in range(ng):
            (_, nwin, win), _ = splits[g]
            i_v, rows_v = bufs[2 * g:2 * g + 2]
            for kk in range(2):
                d_hbm, y_hbm = idx_refs[2 * g + kk], out_refs[2 * g + kk]

                @pl.loop(0, nwin)
                def _(j, nwin=nwin, win=win, d_hbm=d_hbm, y_hbm=y_hbm, i_v=i_v, rows_v=rows_v):
                    base = pl.multiple_of((wid * nwin + j) * win, 8)
                    pltpu.sync_copy(d_hbm.at[wid, j], i_v)
                    pltpu.sync_copy(ys_hbm.at[i_v], rows_v)
                    pltpu.sync_copy(rows_v, y_hbm.at[pl.ds(base, win)])

    idx_args = []
    for (s0, s1) in splits:
        idx_args += [s0[0], s1[0]]
    res = k(ys, *idx_args)
    return [(res[2 * g], res[2 * g + 1]) for g in range(ng)]


def _t5_bucket(rel):
    half = N_BUCKETS // 2
    max_exact = half // 2
    n = jnp.abs(rel)
    far = max_exact + (jnp.log(jnp.maximum(n, 1).astype(F32) / max_exact)
                       / math.log(MAX_DISTANCE / max_exact) * (half - max_exact)).astype(I32)
    far = jnp.minimum(far, half - 1)
    return jnp.where(rel > 0, half, 0) + jnp.where(n < max_exact, n, far)


def _bias_table(rel_bias, cq, nq, with_no_history):
    nk = WINDOW + cq
    j = jnp.arange(KEY_WIN)[:, None]
    c = jnp.arange(UNIT_Q)[None, :]
    jj = j - (c // cq) * cq
    valid = (jj >= 0) & (jj < nk) & (c < nq)
    rel = jj - WINDOW - (c % cq)
    onehot = (_t5_bucket(rel)[:, :, None] == jnp.arange(N_BUCKETS)).astype(F32)
    vals = jnp.einsum("jcb,bh->jch", onehot, rel_bias.astype(F32), precision=lax.Precision.HIGHEST) * LOG2E

    def table(mask):
        b = jnp.where(mask[:, :, None], vals, -jnp.inf)
        b = jnp.transpose(b.reshape(KEY_WIN, UNIT_Q, N_KV, GROUP), (2, 0, 3, 1))
        return b.reshape(N_KV, KEY_WIN, GROUP * UNIT_Q)

    if not with_no_history:
        return table(valid)
    return jnp.concatenate([table(valid), table(valid & (j >= WINDOW))], axis=0)


def _sink_table(sinks):
    s = sinks.astype(F32).reshape(N_KV, 1, GROUP, 1)
    return jnp.broadcast_to(s * LOG2E, (N_KV, 1, GROUP, UNIT_Q)).reshape(N_KV, 1, GROUP * UNIT_Q)


def kernel(x_prompt, x_sample, state_attn_k, state_attn_v, state_conv, c_prompt, c_sample,
           rel_bias, w_ada, b_ada, norm1_w, w_in, q_norm_w, k_norm_w, attn_sinks, conv_w,
           w_out, norm2_w, w_router_group, b_router_group, w_router_expert, b_router_expert,
           w_gate, w_up, w_down):
    depth = w_ada.shape[0]
    assert depth == 1
    bp, tp, d = x_prompt.shape
    bs, ts, _ = x_sample.shape
    n_p, n_s = bp * tp, bs * ts
    n_tok = n_p + n_s
    l = 0

    wi = w_in[l]
    qw, kw, vw, rest = wi[:, :d], wi[:, d:d + KV_COLS], wi[:, d + KV_COLS:d + 2 * KV_COLS], wi[:, d + 2 * KV_COLS:]
    w_in_b = jnp.concatenate([qw, rest, kw, vw], axis=1).astype(BF16)
    w_out_b = w_out[l].astype(BF16)
    w_r = jnp.concatenate([w_router_group[l],
                           jnp.transpose(w_router_expert[l], (1, 0, 2)).reshape(d, N_EXPERTS),
                           jnp.zeros((d, LANES - N_GROUPS - N_EXPERTS), F32)], axis=1)
    w_r_hi = lax.reduce_precision(w_r, exponent_bits=8, mantissa_bits=7)
    w_r = jnp.concatenate([w_r_hi.astype(BF16), (w_r - w_r_hi).astype(BF16)], axis=1)
    b_r = jnp.concatenate([b_router_group[l], b_router_expert[l].reshape(-1),
                           jnp.zeros((LANES - N_GROUPS - N_EXPERTS,), F32)]).reshape(1, LANES)
    knw = jnp.tile(k_norm_w[l], N_KV).reshape(1, KV_COLS)
    qsc = jnp.tile(q_norm_w[l] * (HEAD_DIM ** -0.5 * LOG2E), N_KV).reshape(1, KV_COLS)
    n1w = norm1_w[l].reshape(1, d)
    n2w = norm2_w[l].reshape(1, d)

    mod = _ada(jnp.concatenate([c_prompt, c_sample], axis=0), w_ada[l], b_ada[l]).reshape(bp + bs, 6, d)
    mod_p, mod_s = mod[:bp], mod[bp:]

    xs_rows = x_sample.reshape(1, n_s, d)
    mod4_p = mod_p[:, :, None, :]
    mod4_s = jnp.repeat(jnp.transpose(mod_s, (1, 0, 2)), ts, axis=1)[None]
    proj_s = _inproj(xs_rows, mod4_s, n1w, w_in_b, n_s).reshape(bs, ts, IN_COLS)
    sink_tab = _sink_table(attn_sinks[l])
    bias_p = _bias_table(rel_bias, CHUNK, UNIT_Q, True)
    merged_p, k_p, v_p, c_p = _front(x_prompt, mod_p, n1w, w_in_b, knw, qsc, bias_p, sink_tab, conv_w[l],
                                     MIX_TILE, UNIT_Q)
    state = (state_attn_k[l].reshape(bs, WINDOW, KV_COLS), state_attn_v[l].reshape(bs, WINDOW, KV_COLS),
             state_conv[l])
    merged_s, k_s, v_s, c_s = _mixer(proj_s, knw, qsc, _bias_table(rel_bias, ts, ts, False), sink_tab, conv_w[l],
                                     state, ts, ts)

    x1_p, h2_p, e01_p, ew_p = _outproj(merged_p, x_prompt, mod4_p, w_out_b, n2w, w_r, b_r, OUT_TILE)
    x1_s, h2_s, e01_s, ew_s = _outproj(merged_s.reshape(1, n_s, d), xs_rows, mod4_s, w_out_b, n2w, w_r, b_r, n_s)
    x1_s = x1_s.reshape(bs, ts, d)

    assert n_s == ROW_TILE
    d01, totals = _rank(jnp.concatenate([e01_p, e01_s], axis=0), EXPERT_BLOCK)
    n_sub_p = n_p // ROW_TILE
    dests = [(d01[:n_sub_p, 0].reshape(-1), d01[:n_sub_p, 1].reshape(-1)),
             (d01[n_sub_p:, 0].reshape(-1), d01[n_sub_p:, 1].reshape(-1))]
    nblk = (totals[0, :N_EXPERTS] + EXPERT_BLOCK - 1) // EXPERT_BLOCK
    start_blk = (jnp.cumsum(nblk) - nblk).astype(I32)
    nb_max = -(-2 * n_tok // EXPERT_BLOCK) + N_EXPERTS

    xs = _dispatch_rows([h2_p, h2_s], dests, nb_max * EXPERT_BLOCK)
    ys = _experts(xs, start_blk, nblk.astype(I32), w_gate[l], w_up[l], w_down[l], EXPERT_BLOCK)
    (d0_p, d1_p), dest_s = dests
    assert bp % COLLECT_PARTS == 0
    nbp = bp // COLLECT_PARTS
    rows = nbp * tp
    y_p = None
    for part in range(COLLECT_PARTS):
        sl = slice(part * rows, (part + 1) * rows)
        groups = [(d0_p[sl], d1_p[sl])] + ([dest_s] if part == 0 else [])
        got = _collect_rows(ys, groups)
        if part == 0:
            y0_s, y1_s = got[1]
        y_p = _final(x1_p, got[0][0], got[0][1], ew_p, mod_p, ROW_TILE, part * nbp, nbp, y_p)
    y_s = _final(x1_s, y0_s, y1_s, ew_s, mod_s, ts)

    kv_shape = (1, -1, WINDOW, N_KV, HEAD_DIM)
    return (y_p, y_s, k_p.reshape(kv_shape), v_p.reshape(kv_shape), c_p[None],
            k_s.reshape(kv_shape), v_s.reshape(kv_shape), c_s[None])
```

```python
import functools
import math

import numpy as np
import jax
import jax.numpy as jnp
from jax import lax
from jax.experimental import pallas as pl
from jax.experimental.pallas import tpu as pltpu
from jax.experimental.pallas import tpu_sc as plsc

F32 = jnp.float32
BF16 = jnp.bfloat16
I32 = jnp.int32

D_MODEL = 1024
HEAD_DIM = 64
N_HEADS = 16
N_KV = 4
GROUP = 4
CHUNK = 64
WINDOW = 128
N_BUCKETS = 32
MAX_DISTANCE = 128
N_GROUPS = 8
EPG = 8
N_EXPERTS = 64
D_EXPERT = 512
RMS_EPS = 1e-6
LOG2E = math.log2(math.e)
SCORE_LOOKAHEAD = 1
PROLOGUE_CHUNKS = 4
SUM_ROWS = 16
KV_COLS = N_KV * HEAD_DIM
IN_COLS = 6 * D_MODEL + 2 * KV_COLS
HALF = D_MODEL // 2
LANES = 128

VMEM_LIMIT = 56 * 1024 * 1024
ADA_TN = 2048
INPROJ_TN = 512
ROW_TILE = 512
OUT_TILE = 1024
MIX_TILE = 512
UNIT_Q = 2 * CHUNK
KEY_WIN = WINDOW + UNIT_Q
PROJ_CHUNK = 256
EXPERT_BLOCK = 512
RANK_UNROLL = 13
COLLECT_PARTS = 4
EXPERT_BUFS = 4
SC_CORES = 2
SC_SUBCORES = 16
SC_WORKERS = SC_CORES * SC_SUBCORES
SC_MAX_WINDOW = 128


def _cparams(sem):
    return pltpu.CompilerParams(dimension_semantics=sem, vmem_limit_bytes=VMEM_LIMIT)


def _split_bf16(a):
    hi = a.astype(BF16)
    lo = (a - hi.astype(F32)).astype(BF16)
    return hi, lo


def _dot3(a, b):
    ah, al = _split_bf16(a)
    bh, bl = _split_bf16(b)
    d = functools.partial(jnp.dot, preferred_element_type=F32)
    return d(ah, bh) + (d(ah, bl) + d(al, bh))


def _sigmoid(x):
    return 0.5 * jnp.tanh(0.5 * x) + 0.5


def _pack_pairs(y):
    a = lax.bitcast_convert_type(y[:, :HALF].astype(BF16).astype(F32), I32)
    b = lax.bitcast_convert_type(y[:, HALF:].astype(BF16).astype(F32), I32)
    return a | lax.shift_right_logical(b, jnp.int32(16))


def _unpack_pairs(w):
    a = lax.bitcast_convert_type(w & jnp.int32(-65536), F32)
    b = lax.bitcast_convert_type(lax.shift_left(w, jnp.int32(16)), F32)
    return a, b


def _ada_kernel(c_ref, w_ref, b_ref, o_ref):
    c = c_ref[...]
    s = c * jax.nn.sigmoid(c)
    o_ref[...] = _dot3(s, w_ref[...]) + b_ref[...]


def _ada(c_all, w_ada, b_ada):
    r, d = c_all.shape
    n = w_ada.shape[1]
    tn = ADA_TN
    return pl.pallas_call(
        _ada_kernel,
        out_shape=jax.ShapeDtypeStruct((r, n), F32),
        grid=(n // tn,),
        in_specs=[pl.BlockSpec((r, d), lambda j: (0, 0)),
                  pl.BlockSpec((d, tn), lambda j: (0, j)),
                  pl.BlockSpec((1, tn), lambda j: (0, j))],
        out_specs=pl.BlockSpec((r, tn), lambda j: (0, j)),
        compiler_params=_cparams(("arbitrary",)),
        name="ada",
    )(c_all, w_ada, b_ada.reshape(1, n))


def _inproj_kernel(x_ref, mod_ref, nw_ref, w_ref, o_ref):
    x = x_ref[0]
    mod = mod_ref[0]
    h = x * lax.rsqrt(jnp.mean(x * x, axis=-1, keepdims=True) + RMS_EPS) * nw_ref[...]
    h = h * (1.0 + mod[1]) + mod[0]
    hb = h.astype(BF16)
    for j in range(IN_COLS // INPROJ_TN):
        sl = slice(j * INPROJ_TN, (j + 1) * INPROJ_TN)
        o_ref[0, :, sl] = jnp.dot(hb, w_ref[:, sl], preferred_element_type=F32).astype(BF16)


def _inproj(x, mod4, nw, w_in_b, tm):
    b, t, d = x.shape
    mr = mod4.shape[2]
    assert mr == 1 or (mr == t and tm == t)
    return pl.pallas_call(
        _inproj_kernel,
        out_shape=jax.ShapeDtypeStruct((b, t, IN_COLS), BF16),
        grid=(b, t // tm),
        in_specs=[pl.BlockSpec((1, tm, d), lambda i, j: (i, j, 0)),
                  pl.BlockSpec((1, 6, mr, d), lambda i, j: (i, 0, 0, 0)),
                  pl.BlockSpec((1, d), lambda i, j: (0, 0)),
                  pl.BlockSpec((d, IN_COLS), lambda i, j: (0, 0), pipeline_mode=pl.Buffered(1))],
        out_specs=pl.BlockSpec((1, tm, IN_COLS), lambda i, j: (i, j, 0)),
        compiler_params=_cparams(("arbitrary", "arbitrary")),
        name="inproj",
    )(x, mod4, nw, w_in_b)


def _head_inv_rms(xf, bd, two_pass=True):
    sq = xf * xf
    if two_pass:
        hi, lo = _split_bf16(sq)
        ssq = jnp.dot(hi, bd, preferred_element_type=F32) + jnp.dot(lo, bd, preferred_element_type=F32)
    else:
        ssq = jnp.dot(sq.astype(BF16), bd, preferred_element_type=F32)
    return lax.rsqrt(ssq * (1.0 / HEAD_DIM) + RMS_EPS)


def _mixer_body(q_ref, k_ref, v_ref, bg_ref, c_ref, u_ref, ga_ref, gc_ref,
                knw_ref, qsc_ref, bd_ref, bias_ref, sink_ref, cw_ref,
                kpast_ref, vpast_ref, cpast_ref,
                merged_ref, knew_ref, vnew_ref, cnew_ref,
                kq_buf, vt_buf, attn_buf, *, tq, nq):
    hb = WINDOW
    bd = bd_ref[...]

    k = k_ref[0].astype(F32)
    kn = k * _head_inv_rms(k, bd) * knw_ref[...]
    kq = (kn * qsc_ref[...]).astype(BF16)
    vb = v_ref[0]
    vt = vb.astype(F32).T.astype(BF16)

    kp = kpast_ref[0]
    vp = vpast_ref[0]
    for kv in range(N_KV):
        kq_buf[kv, hb + tq:] = jnp.zeros((KEY_WIN - hb - tq, HEAD_DIM), BF16)
    vt_buf[:, hb + tq:] = jnp.zeros((KV_COLS, KEY_WIN - hb - tq), BF16)
    u_hist = jnp.concatenate([jnp.zeros((6, D_MODEL), F32), cpast_ref[0]], axis=0)
    knew_ref[0] = jnp.concatenate([kp[tq:], kn], axis=0)
    vnew_ref[0] = jnp.concatenate([vp[tq:], vb.astype(F32)], axis=0)
    _fill_keys(kq_buf, vt_buf, kp, vp, kq, vt, qsc_ref, tq)
    q = q_ref[0]
    _attention_units(lambda kv: q[:, kv * KV_COLS:(kv + 1) * KV_COLS].astype(F32), kq_buf, vt_buf,
                     bias_ref, sink_ref, bd, attn_buf, tq, nq, None)
    cu = c_ref[0].astype(F32) * u_ref[0].astype(F32)
    conv, u_all = _short_conv(u_hist, cu, cw_ref[...], tq)
    cnew_ref[0] = u_all[tq + 6:tq + 8]
    merged = (_sigmoid(ga_ref[0].astype(F32)) * attn_buf[...]
              + _sigmoid(gc_ref[0].astype(F32)) * (bg_ref[0].astype(F32) * conv))
    merged_ref[0] = merged.astype(BF16)


def _fill_keys(kq_buf, vt_buf, kp, vp, kq, vt, qsc_ref, tq):
    hb = WINDOW
    kqp = (kp * qsc_ref[...]).astype(BF16)
    for kv in range(N_KV):
        kq_buf[kv, 0:hb] = kqp[:, kv * HEAD_DIM:(kv + 1) * HEAD_DIM]
        kq_buf[kv, hb:hb + tq] = kq[:, kv * HEAD_DIM:(kv + 1) * HEAD_DIM]
    vt_buf[:, 0:hb] = vp.T.astype(BF16)
    vt_buf[:, hb:hb + tq] = vt


def _attention_units(q_group, kq_buf, vt_buf, bias_ref, sink_ref, bd, attn_buf, tq, nq, t_first, after_unit=None):
    pw = UNIT_Q
    n_u = tq // nq
    ones_rows = jnp.ones((SUM_ROWS, KEY_WIN), BF16)
    qn_cache = {}

    def scores(i):
        kv, u = divmod(i, n_u)
        if kv not in qn_cache:
            qf = q_group(kv)
            qn_cache[kv] = (qf * _head_inv_rms(qf, bd, two_pass=False)).astype(BF16)
        qn = qn_cache[kv]
        r0 = u * nq
        parts = [qn[r0:r0 + nq, g * HEAD_DIM:(g + 1) * HEAD_DIM] for g in range(GROUP)]
        if nq < pw:
            zpad = jnp.zeros((pw - nq, HEAD_DIM), BF16)
            parts = [x for p_ in parts for x in (p_, zpad)]
        qs = jnp.concatenate(parts, axis=0)
        kw = kq_buf[kv, r0:r0 + KEY_WIN]
        return lax.dot_general(kw, qs, (((1,), (1,)), ((), ())), preferred_element_type=F32)

    def softmax(i, st):
        kv, u = divmod(i, n_u)
        if t_first is not None and u == 0:
            bias = jnp.where(t_first == 0, bias_ref[kv + N_KV], bias_ref[kv])
        else:
            bias = bias_ref[kv]
        st = st + bias
        sink = sink_ref[kv]
        m = jnp.maximum(jnp.max(st, axis=0, keepdims=True), sink)
        return jnp.exp2((st - m).astype(BF16)), jnp.exp2(sink - m)

    def values(i, p, sink_p):
        kv, u = divmod(i, n_u)
        r0 = u * nq
        vt1 = jnp.concatenate([vt_buf[kv * HEAD_DIM:(kv + 1) * HEAD_DIM, r0:r0 + KEY_WIN], ones_rows], axis=0)
        pv = jnp.dot(vt1, p, preferred_element_type=F32)
        ot = pv[:HEAD_DIM] / (pv[HEAD_DIM:HEAD_DIM + 1] + sink_p)
        for gp in range(GROUP // 2):
            blk = jnp.concatenate([ot[:, (2 * gp) * pw:(2 * gp + 1) * pw],
                                   ot[:, (2 * gp + 1) * pw:(2 * gp + 2) * pw]], axis=0)
            c0 = (kv * GROUP + 2 * gp) * HEAD_DIM
            attn_buf[r0:r0 + nq, c0:c0 + 2 * HEAD_DIM] = blk.T[:nq]

    n_units = N_KV * n_u
    ahead = min(SCORE_LOOKAHEAD, n_units)
    queue = [scores(j) for j in range(ahead)]
    pending = None
    for i in range(n_units):
        if i + ahead < n_units:
            queue.append(scores(i + ahead))
        if after_unit is not None:
            after_unit(i)
        probs = softmax(i, queue.pop(0))
        if pending is not None:
            values(i - 1, *pending)
        pending = probs
    values(n_units - 1, *pending)


def _short_conv(u_hist, cu, cw, tq):
    u_all = jnp.concatenate([u_hist, cu], axis=0)
    conv = cw[0:1] * u_all[6:6 + tq] + cw[1:2] * u_all[7:7 + tq] + cw[2:3] * u_all[8:8 + tq]
    return conv, u_all


def _front_body(x_ref, xp_ref, mod_ref, nw_ref, w_ref, knw_ref, qsc_ref, bd_ref, bias_ref, sink_ref, cw_ref,
                merged_ref, knew_ref, vnew_ref, cnew_ref,
                kq_buf, vt_buf, u_buf, attn_buf, pbuf, *, tq, nq):
    hb = WINDOW
    d = D_MODEL
    t = pl.program_id(1)
    bd = bd_ref[...]
    mod = mod_ref[0]

    @pl.when(t == 0)
    def _():
        u_buf[0] = jnp.zeros((8, d), F32)

    def modnorm(x):
        h = x * lax.rsqrt(jnp.mean(x * x, axis=-1, keepdims=True) + RMS_EPS) * nw_ref[...]
        return (h * (1.0 + mod[1:2]) + mod[0:1]).astype(BF16)

    def proj(hrows, c0, width):
        return jnp.dot(hrows, w_ref[:, c0:c0 + width], preferred_element_type=F32)

    kv_p = proj(modnorm(xp_ref[0]), 6 * d, 2 * KV_COLS)
    hbf = modnorm(x_ref[0])
    kv_t = proj(hbf, 6 * d, 2 * KV_COLS)

    assert tq // nq >= 2
    n_units = N_KV * (tq // nq)
    rest_chunks = list(range(d, 6 * d, PROJ_CHUNK))

    def issue(c0, width):
        pbuf[:, c0:c0 + width] = proj(hbf, c0, width)

    issue(0, KV_COLS)
    for c0 in rest_chunks[:PROLOGUE_CHUNKS]:
        issue(c0, PROJ_CHUNK)
    rest_chunks = rest_chunks[PROLOGUE_CHUNKS:]
    n_rest = len(rest_chunks)

    k, v = kv_t[:, :KV_COLS], kv_t[:, KV_COLS:]
    kn = k * _head_inv_rms(k, bd) * knw_ref[...]
    kq = (kn * qsc_ref[...]).astype(BF16)
    kraw = kv_p[:, :KV_COLS]
    kp = kraw * _head_inv_rms(kraw, bd) * knw_ref[...]
    knew_ref[0] = kn[tq - hb:]
    vnew_ref[0] = v[tq - hb:]
    _fill_keys(kq_buf, vt_buf, kp, kv_p[:, KV_COLS:], kq, v.T.astype(BF16), qsc_ref, tq)

    def after_unit(i):
        kv, u = divmod(i, tq // nq)
        if u == 0 and kv + 1 < N_KV:
            issue((kv + 1) * KV_COLS, KV_COLS)
        for c0 in rest_chunks[i * n_rest // n_units:(i + 1) * n_rest // n_units]:
            issue(c0, PROJ_CHUNK)

    _attention_units(lambda kv: pbuf[:, kv * KV_COLS:(kv + 1) * KV_COLS], kq_buf, vt_buf,
                     bias_ref, sink_ref, bd, attn_buf, tq, nq, t, after_unit)

    cu = pbuf[:, 2 * d:3 * d] * pbuf[:, 3 * d:4 * d]
    conv, u_all = _short_conv(u_buf[t % 2], cu, cw_ref[...], tq)
    cnew_ref[0] = u_all[tq + 6:tq + 8]
    u_buf[(t + 1) % 2] = u_all[tq:tq + 8]
    merged = (_sigmoid(pbuf[:, 4 * d:5 * d]) * attn_buf[...]
              + _sigmoid(pbuf[:, 5 * d:6 * d]) * (pbuf[:, d:2 * d] * conv))
    merged_ref[0] = merged.astype(BF16)


def _front(x, mod, nw, w_in_b, knw, qsc, bias_tab, sink_tab, conv_w, tq, nq):
    b, t, d = x.shape
    r = np.arange(KV_COLS) // HEAD_DIM
    bd = jnp.asarray((r[:, None] == r[None, :]).astype(np.float32), BF16)
    const2 = lambda shp: pl.BlockSpec(shp, lambda i, s: (0, 0))
    const3 = lambda shp: pl.BlockSpec(shp, lambda i, s: (0, 0, 0))
    per_b = lambda shp: pl.BlockSpec(shp, lambda i, s: (i, 0, 0))
    kw_ = tq // WINDOW
    in_specs = [pl.BlockSpec((1, tq, d), lambda i, s: (i, s, 0)),
                pl.BlockSpec((1, WINDOW, d), lambda i, s: (i, jnp.maximum(s * kw_ - 1, 0), 0)),
                per_b((1, 6, d)), const2((1, d)),
                pl.BlockSpec((d, IN_COLS), lambda i, s: (0, 0), pipeline_mode=pl.Buffered(1)),
                const2((1, KV_COLS)), const2((1, KV_COLS)), const2((KV_COLS, KV_COLS)),
                const3(bias_tab.shape), const3(sink_tab.shape), const2((3, d))]
    out_shape = (jax.ShapeDtypeStruct((b, t, d), BF16),
                 jax.ShapeDtypeStruct((b, WINDOW, KV_COLS), F32),
                 jax.ShapeDtypeStruct((b, WINDOW, KV_COLS), F32),
                 jax.ShapeDtypeStruct((b, 2, d), F32))
    out_specs = (pl.BlockSpec((1, tq, d), lambda i, s: (i, s, 0)),
                 per_b((1, WINDOW, KV_COLS)), per_b((1, WINDOW, KV_COLS)), per_b((1, 2, d)))
    return pl.pallas_call(
        functools.partial(_front_body, tq=tq, nq=nq),
        out_shape=out_shape,
        grid=(b, t // tq),
        in_specs=in_specs,
        out_specs=out_specs,
        scratch_shapes=[pltpu.VMEM((N_KV, WINDOW + tq, HEAD_DIM), BF16),
                        pltpu.VMEM((KV_COLS, WINDOW + tq), BF16),
                        pltpu.VMEM((2, 8, d), F32),
                        pltpu.VMEM((tq, d), F32),
                        pltpu.VMEM((tq, 6 * d), F32)],
        compiler_params=_cparams(("arbitrary", "arbitrary")),
        name="front",
    )(x, x, mod, nw, w_in_b, knw, qsc, bd, bias_tab, sink_tab, conv_w)


def _mixer(proj, knw, qsc, bias_tab, sink_tab, conv_w, state, tq, nq):
    b, t, _ = proj.shape
    d = D_MODEL
    key_rows = max(WINDOW + tq, KEY_WIN)
    r = np.arange(KV_COLS) // HEAD_DIM
    bd = jnp.asarray((r[:, None] == r[None, :]).astype(np.float32), BF16)
    wide = lambda j: pl.BlockSpec((1, tq, d), lambda i, s, j=j: (i, s, j))
    kvspec = lambda j: pl.BlockSpec((1, tq, KV_COLS), lambda i, s, j=j: (i, s, j))
    const2 = lambda shp: pl.BlockSpec(shp, lambda i, s: (0, 0))
    const3 = lambda shp: pl.BlockSpec(shp, lambda i, s: (0, 0, 0))
    per_b = lambda shp: pl.BlockSpec(shp, lambda i, s: (i, 0, 0))
    kvblk = 6 * d // KV_COLS
    in_specs = [wide(0), kvspec(kvblk), kvspec(kvblk + 1), wide(1), wide(2), wide(3), wide(4), wide(5),
                const2((1, KV_COLS)), const2((1, KV_COLS)), const2((KV_COLS, KV_COLS)),
                const3(bias_tab.shape), const3(sink_tab.shape), const2((3, d)),
                per_b((1, WINDOW, KV_COLS)), per_b((1, WINDOW, KV_COLS)), per_b((1, 2, d))]
    out_shape = (jax.ShapeDtypeStruct((b, t, d), BF16),
                 jax.ShapeDtypeStruct((b, WINDOW, KV_COLS), F32),
                 jax.ShapeDtypeStruct((b, WINDOW, KV_COLS), F32),
                 jax.ShapeDtypeStruct((b, 2, d), F32))
    out_specs = (pl.BlockSpec((1, tq, d), lambda i, s: (i, s, 0)),
                 per_b((1, WINDOW, KV_COLS)), per_b((1, WINDOW, KV_COLS)), per_b((1, 2, d)))
    return pl.pallas_call(
        functools.partial(_mixer_body, tq=tq, nq=nq),
        out_shape=out_shape,
        grid=(b, t // tq),
        in_specs=in_specs,
        out_specs=out_specs,
        scratch_shapes=[pltpu.VMEM((N_KV, key_rows, HEAD_DIM), BF16),
                        pltpu.VMEM((KV_COLS, key_rows), BF16),
                        pltpu.VMEM((tq, d), F32)],
        compiler_params=_cparams(("arbitrary", "arbitrary")),
        name="mixer_state",
    )(proj, proj, proj, proj, proj, proj, proj, proj, knw, qsc, bd, bias_tab, sink_tab, conv_w, *state)


def _route(logits):
    lane = lax.broadcasted_iota(I32, logits.shape, 1).astype(F32)
    neg = -jnp.inf
    big = float(1 << 20)
    gl = jnp.where(lane < N_GROUPS, logits, neg)
    gmax = jnp.max(gl, axis=-1, keepdims=True)
    g_idx = jnp.min(jnp.where(gl == gmax, lane, big), axis=-1, keepdims=True)
    g_w = 1.0 / jnp.sum(jnp.exp(gl - gmax), axis=-1, keepdims=True)
    lo = N_GROUPS + g_idx * EPG
    el = jnp.where((lane >= lo) & (lane < lo + EPG), logits, neg)
    m1 = jnp.max(el, axis=-1, keepdims=True)
    i1 = jnp.min(jnp.where(el == m1, lane, big), axis=-1, keepdims=True)
    el2 = jnp.where(lane == i1, neg, el)
    m2 = jnp.max(el2, axis=-1, keepdims=True)
    i2 = jnp.min(jnp.where(el2 == m2, lane, big), axis=-1, keepdims=True)
    r = jnp.exp(m2 - m1)
    w1 = 1.0 / (1.0 + r)
    w2 = r / (1.0 + r)
    return i1 - N_GROUPS, i2 - N_GROUPS, g_w * w1, g_w * w2


def _outproj_kernel(m_ref, x_ref, mod_ref, wo_ref, nw_ref, wr_ref, br_ref,
                    x1_ref, h2_ref, e01_ref, ew_ref):
    mod = mod_ref[0]
    sub = ROW_TILE
    n_sub = x_ref.shape[1] // sub
    rows = lambda s: slice(s * sub, (s + 1) * sub)
    modrow = lambda k, s: mod[k] if mod.shape[1] == 1 else mod[k, rows(s)]

    mix = [jnp.dot(m_ref[0, rows(s)], wo_ref[...], preferred_element_type=F32) for s in range(n_sub)]
    prod = []
    lane8 = lax.broadcasted_iota(I32, (sub, 8), 1)

    def route(s):
        logits = prod[s][:sub, :LANES] + (prod[s][:sub, LANES:] + prod[s][sub:, :LANES]) + br_ref[...]
        e1, e2, w1, w2 = _route(logits)
        e01_ref[s] = jnp.concatenate([_col_to_row(e1), _col_to_row(e2)], axis=0).astype(I32)
        ew_ref[rows(s)] = jnp.where(lane8 == 0, w1, jnp.where(lane8 == 1, w2, 0.0))

    for s in range(n_sub):
        x1 = x_ref[0, rows(s)] + modrow(2, s) * mix[s]
        x1_ref[0, rows(s)] = x1
        h = x1 * lax.rsqrt(jnp.mean(x1 * x1, axis=-1, keepdims=True) + RMS_EPS) * nw_ref[...]
        h = h * (1.0 + modrow(4, s)) + modrow(3, s)
        h2_ref[rows(s)] = _pack_pairs(h)
        h_hi, h_lo = _split_bf16(h)
        prod.append(jnp.dot(jnp.concatenate([h_hi, h_lo], axis=0), wr_ref[...], preferred_element_type=F32))
        if s > 0:
            route(s - 1)
    route(n_sub - 1)


def _outproj(merged, x, mod4, w_out_b, nw, w_r, b_r, tm):
    b, t, d = x.shape
    nt = t // tm
    assert nt * tm == t
    mr = mod4.shape[2]
    assert mr == 1 or (mr == t and nt == 1)
    n_sub = tm // ROW_TILE
    assert n_sub * ROW_TILE == tm
    flat = lambda i, j: (i * nt + j, 0)
    return pl.pallas_call(
        _outproj_kernel,
        out_shape=(jax.ShapeDtypeStruct((b, t, d), F32),
                   jax.ShapeDtypeStruct((b * t, HALF), I32),
                   jax.ShapeDtypeStruct((b * nt * n_sub, 2, ROW_TILE), I32),
                   jax.ShapeDtypeStruct((b * t, 8), F32)),
        grid=(b, nt),
        in_specs=[pl.BlockSpec((1, tm, d), lambda i, j: (i, j, 0)),
                  pl.BlockSpec((1, tm, d), lambda i, j: (i, j, 0)),
                  pl.BlockSpec((1, 6, mr, d), lambda i, j: (i, 0, 0, 0)),
                  pl.BlockSpec((d, d), lambda i, j: (0, 0)),
                  pl.BlockSpec((1, d), lambda i, j: (0, 0)),
                  pl.BlockSpec((d, 2 * LANES), lambda i, j: (0, 0)),
                  pl.BlockSpec((1, LANES), lambda i, j: (0, 0))],
        out_specs=(pl.BlockSpec((1, tm, d), lambda i, j: (i, j, 0)),
                   pl.BlockSpec((tm, HALF), flat),
                   pl.BlockSpec((n_sub, 2, ROW_TILE), lambda i, j: (i * nt + j, 0, 0)),
                   pl.BlockSpec((tm, 8), flat)),
        compiler_params=_cparams(("arbitrary", "arbitrary")),
        name="outproj",
    )(merged, x, mod4, w_out_b, nw, w_r, b_r)


def _col_to_row(col):
    eye = lax.broadcasted_iota(I32, (LANES, LANES), 0) == lax.broadcasted_iota(I32, (LANES, LANES), 1)
    parts = [jnp.sum(jnp.where(eye, col[r * LANES:(r + 1) * LANES], 0.0), axis=0, keepdims=True)
             for r in range(col.shape[0] // LANES)]
    return jnp.concatenate(parts, axis=1)


def _rank_kernel(e_ref, tri_ref, low_ref, d_ref, tot_ref, *, block):
    n_sub, _, t = e_ref.shape
    sub = lax.broadcasted_iota(I32, (LANES, t), 0)

    def hots(s):
        e = e_ref[s]
        return sub == e[0:1], sub == e[1:2]

    def count(s, cnt):
        h0, h1 = hots(s)
        return cnt + jnp.sum(jnp.where(h0 | h1, 1.0, 0.0), axis=1, keepdims=True)

    cnt = lax.fori_loop(0, n_sub, count, jnp.zeros((LANES, 1), F32), unroll=RANK_UNROLL)
    tot_ref[...] = _col_to_row(cnt).astype(I32)
    nblk = jnp.floor((cnt + (block - 1)) * (1.0 / block))
    hi = jnp.floor(nblk * (1.0 / 16.0))
    lo = nblk - hi * 16.0
    low = low_ref[...]
    bcast = lambda c: jnp.broadcast_to(c, (LANES, LANES)).astype(BF16)
    excl = (jnp.dot(low, bcast(hi), preferred_element_type=F32) * 16.0
            + jnp.dot(low, bcast(lo), preferred_element_type=F32))
    starts = excl[:, 0:1] * float(block)

    def place(s, running):
        h0, h1 = hots(s)
        onehot = jnp.where(h0 | h1, 1.0, 0.0)
        prefix = jnp.dot(onehot.astype(BF16), tri_ref[...], preferred_element_type=F32)
        pos = prefix + running
        d0 = jnp.sum(jnp.where(h0, pos, 0.0), axis=0, keepdims=True)
        d1 = jnp.sum(jnp.where(h1, pos, 0.0), axis=0, keepdims=True)
        d_ref[s] = jnp.concatenate([d0, d1], axis=0).astype(I32)
        return running + jnp.sum(onehot, axis=1, keepdims=True)

    lax.fori_loop(0, n_sub, place, starts, unroll=RANK_UNROLL)


def _rank(e01, block):
    n_sub, _, t = e01.shape
    r = np.arange(t)
    tri = jnp.asarray((r[:, None] < r[None, :]).astype(np.float32), BF16)
    l = np.arange(LANES)
    low = jnp.asarray((l[None, :] < l[:, None]).astype(np.float32), BF16)
    return pl.pallas_call(
        functools.partial(_rank_kernel, block=block),
        out_shape=(jax.ShapeDtypeStruct((n_sub, 2, t), I32), jax.ShapeDtypeStruct((1, LANES), I32)),
        compiler_params=pltpu.CompilerParams(vmem_limit_bytes=VMEM_LIMIT),
        name="rank",
    )(e01, tri, low)


def _expert_kernel(start_ref, nblk_ref, xs_hbm, wg_ref, wu_ref, wd_ref, ys_hbm,
                   xbuf, ybuf, wg_s, wu_s, wd_s, sem_in, sem_out, *, block):
    nbuf = EXPERT_BUFS
    e = pl.program_id(0)
    n = nblk_ref[e]
    base = start_ref[e]
    total = start_ref[N_EXPERTS - 1] + nblk_ref[N_EXPERTS - 1]

    def in_copy(g):
        rows = pl.ds(pl.multiple_of(g * block, block), block)
        return pltpu.make_async_copy(xs_hbm.at[rows], xbuf.at[g % nbuf], sem_in.at[g % nbuf])

    def out_copy(g):
        rows = pl.ds(pl.multiple_of(g * block, block), block)
        return pltpu.make_async_copy(ybuf.at[g % nbuf], ys_hbm.at[rows], sem_out.at[g % nbuf])

    @pl.when(e == 0)
    def _():
        for g0 in range(nbuf - 1):
            @pl.when(g0 < total)
            def _(g0=g0):
                in_copy(g0).start()

    @pl.when(n > 0)
    def _():
        wg_s[...] = wg_ref[0].astype(BF16)
        wu_s[...] = wu_ref[0].astype(BF16)
        wd_s[...] = wd_ref[0].astype(BF16)

        def body(i, carry):
            g = base + i
            slot = g % nbuf
            in_copy(g).wait()

            @pl.when(g + nbuf - 1 < total)
            def _():
                in_copy(g + nbuf - 1).start()

            @pl.when(g >= nbuf)
            def _():
                out_copy(g - nbuf).wait()

            a, c = _unpack_pairs(xbuf[slot])
            x = jnp.concatenate([a.astype(BF16), c.astype(BF16)], axis=1)
            gate = jnp.dot(x, wg_s[...], preferred_element_type=F32)
            up = jnp.dot(x, wu_s[...], preferred_element_type=F32)
            hmid = (gate * _sigmoid(gate) * up).astype(BF16)
            ybuf[slot] = _pack_pairs(jnp.dot(hmid, wd_s[...], preferred_element_type=F32))
            out_copy(g).start()
            return carry

        lax.fori_loop(0, n, body, 0)

    @pl.when(e == N_EXPERTS - 1)
    def _():
        for back in range(nbuf, 0, -1):
            @pl.when(total >= back)
            def _(back=back):
                out_copy(total - back).wait()


def _experts(xs, start_blk, nblk, w_gate, w_up, w_down, block):
    n_rows = xs.shape[0]
    wblk = lambda e, st, nb: (e, 0, 0)
    grid_spec = pltpu.PrefetchScalarGridSpec(
        num_scalar_prefetch=2,
        grid=(N_EXPERTS,),
        in_specs=[pl.BlockSpec(memory_space=pl.ANY),
                  pl.BlockSpec((1, D_MODEL, D_EXPERT), wblk),
                  pl.BlockSpec((1, D_MODEL, D_EXPERT), wblk),
                  pl.BlockSpec((1, D_EXPERT, D_MODEL), wblk)],
        out_specs=pl.BlockSpec(memory_space=pl.ANY),
        scratch_shapes=[pltpu.VMEM((EXPERT_BUFS, block, HALF), I32),
                        pltpu.VMEM((EXPERT_BUFS, block, HALF), I32),
                        pltpu.VMEM((D_MODEL, D_EXPERT), BF16),
                        pltpu.VMEM((D_MODEL, D_EXPERT), BF16),
                        pltpu.VMEM((D_EXPERT, D_MODEL), BF16),
                        pltpu.SemaphoreType.DMA((EXPERT_BUFS,)),
                        pltpu.SemaphoreType.DMA((EXPERT_BUFS,))])
    return pl.pallas_call(
        functools.partial(_expert_kernel, block=block),
        out_shape=jax.ShapeDtypeStruct((n_rows, HALF), I32),
        grid_spec=grid_spec,
        compiler_params=_cparams(("arbitrary",)),
        name="experts",
    )(start_blk, nblk, xs, w_gate, w_up, w_down)


def _final_kernel(x1_ref, y0_ref, y1_ref, ew_ref, mod_ref, o_ref):
    a0, b0 = _unpack_pairs(y0_ref[...])
    a1, b1 = _unpack_pairs(y1_ref[...])
    w0 = ew_ref[:, 0:1]
    w1 = ew_ref[:, 1:2]
    moe = jnp.concatenate([w0 * a0 + w1 * a1, w0 * b0 + w1 * b1], axis=1)
    o_ref[0] = x1_ref[0] + mod_ref[0][5:6] * moe


def _final(x1, y0, y1, ew, mod, tm, b0=0, nb=None, y_prev=None):
    b, t, d = x1.shape
    nb = b if nb is None else nb
    nt = t // tm
    local = lambda i, j: (i * nt + j, 0)
    glob = lambda i, j: ((i + b0) * nt + j, 0)
    rows3 = lambda i, j: (i + b0, j, 0)
    in_specs = [pl.BlockSpec((1, tm, d), rows3),
                pl.BlockSpec((tm, HALF), local),
                pl.BlockSpec((tm, HALF), local),
                pl.BlockSpec((tm, 8), glob),
                pl.BlockSpec((1, 6, d), lambda i, j: (i + b0, 0, 0))]
    args = [x1, y0, y1, ew, mod]
    aliases = {}
    kern = _final_kernel
    if y_prev is not None:
        in_specs.append(pl.BlockSpec(memory_space=pl.ANY))
        args.append(y_prev)
        aliases = {5: 0}
        kern = lambda *refs: _final_kernel(*refs[:5], refs[6])
    return pl.pallas_call(
        kern,
        out_shape=jax.ShapeDtypeStruct((b, t, d), F32),
        grid=(nb, nt),
        in_specs=in_specs,
        out_specs=pl.BlockSpec((1, tm, d), rows3),
        input_output_aliases=aliases,
        compiler_params=_cparams(("arbitrary", "arbitrary")),
        name="final",
    )(*args)


def _sc_window(rows_per_worker):
    for w in range(SC_MAX_WINDOW, 7, -8):
        if rows_per_worker % w == 0:
            return w
    raise ValueError(f"no SparseCore window divides {rows_per_worker} rows per worker")


def _sc_split(idx):
    n = idx.shape[0]
    per = n // SC_WORKERS
    assert per * SC_WORKERS == n
    win = _sc_window(per)
    return idx.reshape(SC_WORKERS, per // win, win), per // win, win


def _sc_worker_id():
    return lax.axis_index("s") * SC_CORES + lax.axis_index("c")


def _dispatch_rows(h2_groups, dest_groups, n_rows):
    splits = [(_sc_split(d0), _sc_split(d1)) for d0, d1 in dest_groups]
    ng = len(h2_groups)
    scratch = []
    for (_, _, win), _ in splits:
        scratch += [pltpu.VMEM((win,), I32), pltpu.VMEM((win,), I32), pltpu.VMEM((win, HALF), I32)]

    @functools.partial(
        pl.kernel,
        mesh=plsc.VectorSubcoreMesh(core_axis_name="c", subcore_axis_name="s"),
        out_type=jax.ShapeDtypeStruct((n_rows, HALF), I32),
        scratch_types=scratch,
        name="sc_dispatch",
    )
    def k(*refs):
        x_refs, idx_refs, o_hbm, bufs = refs[:ng], refs[ng:3 * ng], refs[3 * ng], refs[3 * ng + 1:]
        wid = _sc_worker_id()
        for g in range(ng):
            (_, nwin, win), _ = splits[g]
            x_hbm, d0_hbm, d1_hbm = x_refs[g], idx_refs[2 * g], idx_refs[2 * g + 1]
            i0_v, i1_v, rows_v = bufs[3 * g:3 * g + 3]

            @pl.loop(0, nwin)
            def _(j, nwin=nwin, win=win, x_hbm=x_hbm, d0_hbm=d0_hbm, d1_hbm=d1_hbm,
                  i0_v=i0_v, i1_v=i1_v, rows_v=rows_v):
                base = pl.multiple_of((wid * nwin + j) * win, 8)
                pltpu.sync_copy(d0_hbm.at[wid, j], i0_v)
                pltpu.sync_copy(d1_hbm.at[wid, j], i1_v)
                pltpu.sync_copy(x_hbm.at[pl.ds(base, win)], rows_v)
                pltpu.sync_copy(rows_v, o_hbm.at[i0_v])
                pltpu.sync_copy(rows_v, o_hbm.at[i1_v])

    idx_args = []
    for (s0, s1) in splits:
        idx_args += [s0[0], s1[0]]
    return k(*h2_groups, *idx_args)


def _collect_rows(ys, dest_groups):
    splits = [(_sc_split(d0), _sc_split(d1)) for d0, d1 in dest_groups]
    ng = len(dest_groups)
    outs, scratch = [], []
    for (d0, _), ((_, _, win), _) in zip(dest_groups, splits):
        o = jax.ShapeDtypeStruct((d0.shape[0], HALF), I32)
        outs += [o, o]
        scratch += [pltpu.VMEM((win,), I32), pltpu.VMEM((win, HALF), I32)]

    @functools.partial(
        pl.kernel,
        mesh=plsc.VectorSubcoreMesh(core_axis_name="c", subcore_axis_name="s"),
        out_type=tuple(outs),
        scratch_types=scratch,
        name="sc_collect",
    )
    def k(*refs):
        ys_hbm, idx_refs, out_refs, bufs = refs[0], refs[1:1 + 2 * ng], refs[1 + 2 * ng:1 + 4 * ng], refs[1 + 4 * ng:]
        wid = _sc_worker_id()
        for g in range(ng):
            (_, nwin, win), _ = splits[g]
            i_v, rows_v = bufs[2 * g:2 * g + 2]
            for kk in range(2):
                d_hbm, y_hbm = idx_refs[2 * g + kk], out_refs[2 * g + kk]

                @pl.loop(0, nwin)
                def _(j, nwin=nwin, win=win, d_hbm=d_hbm, y_hbm=y_hbm, i_v=i_v, rows_v=rows_v):
                    base = pl.multiple_of((wid * nwin + j) * win, 8)
                    pltpu.sync_copy(d_hbm.at[wid, j], i_v)
                    pltpu.sync_copy(ys_hbm.at[i_v], rows_v)
                    pltpu.sync_copy(rows_v, y_hbm.at[pl.ds(base, win)])

    idx_args = []
    for (s0, s1) in splits:
        idx_args += [s0[0], s1[0]]
    res = k(ys, *idx_args)
    return [(res[2 * g], res[2 * g + 1]) for g in range(ng)]


def _t5_bucket(rel):
    half = N_BUCKETS // 2
    max_exact = half // 2
    n = jnp.abs(rel)
    far = max_exact + (jnp.log(jnp.maximum(n, 1).astype(F32) / max_exact)
                       / math.log(MAX_DISTANCE / max_exact) * (half - max_exact)).astype(I32)
    far = jnp.minimum(far, half - 1)
    return jnp.where(rel > 0, half, 0) + jnp.where(n < max_exact, n, far)


def _bias_table(rel_bias, cq, nq, with_no_history):
    nk = WINDOW + cq
    j = jnp.arange(KEY_WIN)[:, None]
    c = jnp.arange(UNIT_Q)[None, :]
    jj = j - (c // cq) * cq
    valid = (jj >= 0) & (jj < nk) & (c < nq)
    rel = jj - WINDOW - (c % cq)
    onehot = (_t5_bucket(rel)[:, :, None] == jnp.arange(N_BUCKETS)).astype(F32)
    vals = jnp.einsum("jcb,bh->jch", onehot, rel_bias.astype(F32), precision=lax.Precision.HIGHEST) * LOG2E

    def table(mask):
        b = jnp.where(mask[:, :, None], vals, -jnp.inf)
        b = jnp.transpose(b.reshape(KEY_WIN, UNIT_Q, N_KV, GROUP), (2, 0, 3, 1))
        return b.reshape(N_KV, KEY_WIN, GROUP * UNIT_Q)

    if not with_no_history:
        return table(valid)
    return jnp.concatenate([table(valid), table(valid & (j >= WINDOW))], axis=0)


def _sink_table(sinks):
    s = sinks.astype(F32).reshape(N_KV, 1, GROUP, 1)
    return jnp.broadcast_to(s * LOG2E, (N_KV, 1, GROUP, UNIT_Q)).reshape(N_KV, 1, GROUP * UNIT_Q)


def kernel(x_prompt, x_sample, state_attn_k, state_attn_v, state_conv, c_prompt, c_sample,
           rel_bias, w_ada, b_ada, norm1_w, w_in, q_norm_w, k_norm_w, attn_sinks, conv_w,
           w_out, norm2_w, w_router_group, b_router_group, w_router_expert, b_router_expert,
           w_gate, w_up, w_down):
    depth = w_ada.shape[0]
    assert depth == 1
    bp, tp, d = x_prompt.shape
    bs, ts, _ = x_sample.shape
    n_p, n_s = bp * tp, bs * ts
    n_tok = n_p + n_s
    l = 0

    wi = w_in[l]
    qw, kw, vw, rest = wi[:, :d], wi[:, d:d + KV_COLS], wi[:, d + KV_COLS:d + 2 * KV_COLS], wi[:, d + 2 * KV_COLS:]
    w_in_b = jnp.concatenate([qw, rest, kw, vw], axis=1).astype(BF16)
    w_out_b = w_out[l].astype(BF16)
    w_r = jnp.concatenate([w_router_group[l],
                           jnp.transpose(w_router_expert[l], (1, 0, 2)).reshape(d, N_EXPERTS),
                           jnp.zeros((d, LANES - N_GROUPS - N_EXPERTS), F32)], axis=1)
    w_r_hi = lax.reduce_precision(w_r, exponent_bits=8, mantissa_bits=7)
    w_r = jnp.concatenate([w_r_hi.astype(BF16), (w_r - w_r_hi).astype(BF16)], axis=1)
    b_r = jnp.concatenate([b_router_group[l], b_router_expert[l].reshape(-1),
                           jnp.zeros((LANES - N_GROUPS - N_EXPERTS,), F32)]).reshape(1, LANES)
    knw = jnp.tile(k_norm_w[l], N_KV).reshape(1, KV_COLS)
    qsc = jnp.tile(q_norm_w[l] * (HEAD_DIM ** -0.5 * LOG2E), N_KV).reshape(1, KV_COLS)
    n1w = norm1_w[l].reshape(1, d)
    n2w = norm2_w[l].reshape(1, d)

    mod = _ada(jnp.concatenate([c_prompt, c_sample], axis=0), w_ada[l], b_ada[l]).reshape(bp + bs, 6, d)
    mod_p, mod_s = mod[:bp], mod[bp:]

    xs_rows = x_sample.reshape(1, n_s, d)
    mod4_p = mod_p[:, :, None, :]
    mod4_s = jnp.repeat(jnp.transpose(mod_s, (1, 0, 2)), ts, axis=1)[None]
    proj_s = _inproj(xs_rows, mod4_s, n1w, w_in_b, n_s).reshape(bs, ts, IN_COLS)
    sink_tab = _sink_table(attn_sinks[l])
    bias_p = _bias_table(rel_bias, CHUNK, UNIT_Q, True)
    merged_p, k_p, v_p, c_p = _front(x_prompt, mod_p, n1w, w_in_b, knw, qsc, bias_p, sink_tab, conv_w[l],
                                     MIX_TILE, UNIT_Q)
    state = (state_attn_k[l].reshape(bs, WINDOW, KV_COLS), state_attn_v[l].reshape(bs, WINDOW, KV_COLS),
             state_conv[l])
    merged_s, k_s, v_s, c_s = _mixer(proj_s, knw, qsc, _bias_table(rel_bias, ts, ts, False), sink_tab, conv_w[l],
                                     state, ts, ts)

    x1_p, h2_p, e01_p, ew_p = _outproj(merged_p, x_prompt, mod4_p, w_out_b, n2w, w_r, b_r, OUT_TILE)
    x1_s, h2_s, e01_s, ew_s = _outproj(merged_s.reshape(1, n_s, d), xs_rows, mod4_s, w_out_b, n2w, w_r, b_r, n_s)
    x1_s = x1_s.reshape(bs, ts, d)

    assert n_s == ROW_TILE
    d01, totals = _rank(jnp.concatenate([e01_p, e01_s], axis=0), EXPERT_BLOCK)
    n_sub_p = n_p // ROW_TILE
    dests = [(d01[:n_sub_p, 0].reshape(-1), d01[:n_sub_p, 1].reshape(-1)),
             (d01[n_sub_p:, 0].reshape(-1), d01[n_sub_p:, 1].reshape(-1))]
    nblk = (totals[0, :N_EXPERTS] + EXPERT_BLOCK - 1) // EXPERT_BLOCK
    start_blk = (jnp.cumsum(nblk) - nblk).astype(I32)
    nb_max = -(-2 * n_tok // EXPERT_BLOCK) + N_EXPERTS

    xs = _dispatch_rows([h2_p, h2_s], dests, nb_max * EXPERT_BLOCK)
    ys = _experts(xs, start_blk, nblk.astype(I32), w_gate[l], w_up[l], w_down[l], EXPERT_BLOCK)
    (d0_p, d1_p), dest_s = dests
    assert bp % COLLECT_PARTS == 0
    nbp = bp // COLLECT_PARTS
    rows = nbp * tp
    y_p = None
    for part in range(COLLECT_PARTS):
        sl = slice(part * rows, (part + 1) * rows)
        groups = [(d0_p[sl], d1_p[sl])] + ([dest_s] if part == 0 else [])
        got = _collect_rows(ys, groups)
        if part == 0:
            y0_s, y1_s = got[1]
        y_p = _final(x1_p, got[0][0], got[0][1], ew_p, mod_p, ROW_TILE, part * nbp, nbp, y_p)
    y_s = _final(x1_s, y0_s, y1_s, ew_s, mod_s, ts)

    kv_shape = (1, -1, WINDOW, N_KV, HEAD_DIM)
    return (y_p, y_s, k_p.reshape(kv_shape), v_p.reshape(kv_shape), c_p[None],
            k_s.reshape(kv_shape), v_s.reshape(kv_shape), c_s[None])
```
